```python
import math
import jax, jax.numpy as jnp
from jax import lax
import numpy as np

D_MODEL = 1024
BATCH = 8
SEQ = 4096
DEPTH = 1

GMLP_WIDTH = 1024
GMLP_GROUPS = 8
GMLP_CHUNK = 128
HEAD_DIM = 64
DILATED_PATTERNS = ((128, 1), (512, 4), (2048, 16))
HEADS_PER_GROUP = 4
N_ATTN_HEADS = HEADS_PER_GROUP * len(DILATED_PATTERNS)
ATTN_WIDTH = N_ATTN_HEADS * HEAD_DIM
ATTN_OUT_WIDTH = HEADS_PER_GROUP * HEAD_DIM
IN_PROJ_WIDTH = 2 * GMLP_WIDTH + 3 * ATTN_WIDTH + 2 * D_MODEL
N_EXPERTS = 256
TOP_K = 8
N_EXPERT_GROUPS = 8
TOPK_GROUPS = 4
EXPERT_DIM = 256
ROUTED_SCALE = 2.5
MOE_BLOCK = 128
ALPHA = (2.0 * DEPTH) ** 0.25
BETA = (8.0 * DEPTH) ** -0.25
LN_EPS = 1e-5

kernel_name = 'hybrid_gmlp_dilated_alibi_moe_deepnorm'


def alibi_slopes(n):
    def pow2_slopes(m):
        start = 2.0 ** (-8.0 / m)
        return [start ** (i + 1) for i in range(m)]
    p = 2 ** int(math.floor(math.log2(n)))
    s = pow2_slopes(p)
    if p < n:
        s = s + pow2_slopes(2 * p)[0::2][: n - p]
    s = sorted(s, reverse=True)
    return jnp.asarray(np.array(s, dtype=np.float32))


def layer_norm(x, g, b):
    xf = x.astype(jnp.float32)
    mu = xf.mean(-1, keepdims=True)
    var = jnp.square(xf - mu).mean(-1, keepdims=True)
    y = (xf - mu) * lax.rsqrt(var + LN_EPS) * g.astype(jnp.float32) + b.astype(jnp.float32)
    return y.astype(x.dtype)


def dilated_window_attention(q, k, v, slopes, window, dilation):
    B, S, H, Dh = q.shape
    blk = window // dilation
    span = blk * dilation
    L = -(-S // span) * span
    nb = L // span

    def to_blocks(t):
        t = jnp.pad(t.astype(jnp.float32), ((0, 0), (0, L - S), (0, 0), (0, 0)))
        t = t.reshape(B, L // dilation, dilation, H, Dh).transpose(0, 2, 1, 3, 4)
        return t.reshape(B, dilation, nb, blk, H, Dh)

    def with_prev(t):
        prev = jnp.pad(t[:, :, :-1], ((0, 0), (0, 0), (1, 0), (0, 0), (0, 0), (0, 0)))
        return jnp.concatenate([prev, t], axis=3)

    qb = to_blocks(q)
    kc = with_prev(to_blocks(k))
    vc = with_prev(to_blocks(v))
    s = jnp.einsum('brnqhd,brnkhd->brnhqk', qb, kc) * (Dh ** -0.5)
    qi = jnp.arange(blk)[:, None]
    ki = jnp.arange(2 * blk)[None, :]
    delta = blk + qi - ki
    band = (delta >= 0) & (delta <= blk)
    not_before_start = (jnp.arange(nb)[:, None, None] > 0) | (ki[None] >= blk)
    valid = band[None] & not_before_start
    bias = -slopes[:, None, None] * (delta * dilation).astype(jnp.float32)[None]
    s = jnp.where(valid[None, None, :, None], s + bias[None, None, None], -jnp.inf)
    m = s.max(-1, keepdims=True)
    p = jnp.exp(s - m)
    den = p.sum(-1, keepdims=True)
    o = jnp.einsum('brnhqk,brnkhd->brnqhd', p, vc) / den.transpose(0, 1, 2, 4, 3, 5)
    lse = (m + jnp.log(den)).transpose(0, 1, 2, 4, 3, 5)

    def from_blocks(t):
        c = t.shape[-1]
        t = t.reshape(B, dilation, L // dilation, H, c).transpose(0, 2, 1, 3, 4)
        return t.reshape(B, L, H, c)[:, :S]

    return from_blocks(o), from_blocks(lse)[..., 0]


def hybrid_mixer(x, in_w, in_b, gmlp_norm_g, gmlp_norm_b, spatial_w, spatial_b,
                 proj_a_w, proj_b_w, out_w):
    B, S, D = x.shape
    h = x @ in_w + in_b
    cuts = [GMLP_WIDTH, 2 * GMLP_WIDTH, 2 * GMLP_WIDTH + ATTN_WIDTH,
            2 * GMLP_WIDTH + 2 * ATTN_WIDTH, 2 * GMLP_WIDTH + 3 * ATTN_WIDTH,
            2 * GMLP_WIDTH + 3 * ATTN_WIDTH + D_MODEL]
    u, v, q, k, vv, gate_a, gate_b = jnp.split(h, cuts, axis=-1)

    u = jax.nn.gelu(u, approximate=False)
    v = layer_norm(jax.nn.gelu(v, approximate=False), gmlp_norm_g, gmlp_norm_b)
    vc = v.reshape(B, S // GMLP_CHUNK, GMLP_CHUNK, GMLP_GROUPS, GMLP_WIDTH // GMLP_GROUPS)
    causal = jnp.tril(jnp.ones((GMLP_CHUNK, GMLP_CHUNK), dtype=bool))
    ws = jnp.where(causal[None], spatial_w, 0.0)
    vmix = jnp.einsum('gts,bnsgc->bntgc', ws, vc) + spatial_b.T[None, None, :, :, None]
    y_a = (u * vmix.reshape(B, S, GMLP_WIDTH)) @ proj_a_w

    q = q.reshape(B, S, N_ATTN_HEADS, HEAD_DIM)
    k = k.reshape(B, S, N_ATTN_HEADS, HEAD_DIM)
    vv = vv.reshape(B, S, N_ATTN_HEADS, HEAD_DIM)
    slopes = alibi_slopes(N_ATTN_HEADS)
    outs, lses = [], []
    for g, (window, dilation) in enumerate(DILATED_PATTERNS):
        sl = slice(g * HEADS_PER_GROUP, (g + 1) * HEADS_PER_GROUP)
        o, lse = dilated_window_attention(q[:, :, sl], k[:, :, sl], vv[:, :, sl],
                                          slopes[sl], window, dilation)
        outs.append(o)
        lses.append(lse)
    w_pat = jax.nn.softmax(jnp.stack(lses, axis=0), axis=0)
    y_b = jnp.sum(w_pat[..., None] * jnp.stack(outs, axis=0), axis=0)
    y_b = y_b.reshape(B, S, ATTN_OUT_WIDTH).astype(x.dtype) @ proj_b_w

    merged = jax.nn.sigmoid(gate_a) * y_a + jax.nn.sigmoid(gate_b) * y_b
    return merged @ out_w


def moe_ffn(x, router_w, router_bias, w_gate, w_up, w_down, sw_gate, sw_up, sw_down):
    B, S, D = x.shape
    T = B * S
    xf = x.reshape(T, D)
    scores = jax.nn.sigmoid(xf.astype(jnp.float32) @ router_w.astype(jnp.float32))
    biased = scores + router_bias.astype(jnp.float32)
    grp = biased.reshape(T, N_EXPERT_GROUPS, N_EXPERTS // N_EXPERT_GROUPS)
    grp_score = lax.top_k(grp, 2)[0].sum(-1)
    _, top_grp = lax.top_k(grp_score, TOPK_GROUPS)
    grp_mask = jax.nn.one_hot(top_grp, N_EXPERT_GROUPS, dtype=jnp.float32).sum(1) > 0
    expert_mask = jnp.repeat(grp_mask, N_EXPERTS // N_EXPERT_GROUPS, axis=1)
    _, top_e = lax.top_k(jnp.where(expert_mask, biased, -jnp.inf), TOP_K)
    gate = jnp.take_along_axis(scores, top_e, axis=1)
    gate = gate / gate.sum(-1, keepdims=True) * ROUTED_SCALE

    A = T * TOP_K
    n_blocks = (A + N_EXPERTS * (MOE_BLOCK - 1) + MOE_BLOCK - 1) // MOE_BLOCK
    rows = n_blocks * MOE_BLOCK
    e_flat = top_e.reshape(A).astype(jnp.int32)
    g_flat = gate.reshape(A)
    t_flat = jnp.arange(A, dtype=jnp.int32) // TOP_K
    order = jnp.argsort(e_flat)
    e_sorted = e_flat[order]
    counts = jnp.bincount(e_flat, length=N_EXPERTS).astype(jnp.int32)
    padded = (counts + MOE_BLOCK - 1) // MOE_BLOCK * MOE_BLOCK
    start = jnp.cumsum(counts) - counts
    pend = jnp.cumsum(padded)
    pstart = pend - padded
    dest = pstart[e_sorted] + jnp.arange(A, dtype=jnp.int32) - start[e_sorted]
    row_tok = jnp.zeros((rows,), jnp.int32).at[dest].set(t_flat[order])
    row_gate = jnp.zeros((rows,), jnp.float32).at[dest].set(g_flat[order])
    block_e = jnp.minimum(
        jnp.searchsorted(pend, jnp.arange(n_blocks, dtype=jnp.int32) * MOE_BLOCK, side='right'),
        N_EXPERTS - 1)

    def expert_block(args):
        tok, g, e = args
        xb = xf[tok]
        hb = jax.nn.silu(xb @ w_gate[e]) * (xb @ w_up[e])
        return (hb @ w_down[e]).astype(jnp.float32) * g[:, None]

    y_rows = lax.map(expert_block, (row_tok.reshape(n_blocks, MOE_BLOCK),
                                    row_gate.reshape(n_blocks, MOE_BLOCK), block_e))
    routed = jax.ops.segment_sum(y_rows.reshape(rows, D), row_tok, num_segments=T)
    shared = (jax.nn.silu(xf @ sw_gate) * (xf @ sw_up)) @ sw_down
    return (routed.astype(x.dtype) + shared).reshape(B, S, D)


def setup_inputs(seed: int = 0) -> dict:
    key = jax.random.key(seed)
    ks = jax.random.split(key, 24)
    f32 = jnp.float32

    def nrm(k, shape, scale):
        return jax.random.normal(k, shape, f32) * scale

    L = DEPTH
    return {
        'x': nrm(ks[0], (BATCH, SEQ, D_MODEL), 1.0),
        'in_w': nrm(ks[1], (L, D_MODEL, IN_PROJ_WIDTH), D_MODEL ** -0.5),
        'in_b': nrm(ks[2], (L, IN_PROJ_WIDTH), 0.01),
        'gmlp_norm_g': 1.0 + nrm(ks[3], (L, GMLP_WIDTH), 0.01),
        'gmlp_norm_b': nrm(ks[4], (L, GMLP_WIDTH), 0.01),
        'spatial_w': nrm(ks[5], (L, GMLP_GROUPS, GMLP_CHUNK, GMLP_CHUNK), GMLP_CHUNK ** -0.5),
        'spatial_b': 1.0 + nrm(ks[6], (L, GMLP_GROUPS, GMLP_CHUNK), 0.01),
        'proj_a_w': nrm(ks[7], (L, GMLP_WIDTH, D_MODEL), GMLP_WIDTH ** -0.5),
        'proj_b_w': nrm(ks[8], (L, ATTN_OUT_WIDTH, D_MODEL), ATTN_OUT_WIDTH ** -0.5),
        'out_w': nrm(ks[9], (L, D_MODEL, D_MODEL), D_MODEL ** -0.5 * BETA),
        'ln1_g': 1.0 + nrm(ks[10], (L, D_MODEL), 0.01),
        'ln1_b': nrm(ks[11], (L, D_MODEL), 0.01),
        'router_w': nrm(ks[12], (L, D_MODEL, N_EXPERTS), D_MODEL ** -0.5),
        'router_bias': nrm(ks[13], (L, N_EXPERTS), 0.01),
        'expert_w_gate': nrm(ks[14], (L, N_EXPERTS, D_MODEL, EXPERT_DIM), D_MODEL ** -0.5),
        'expert_w_up': nrm(ks[15], (L, N_EXPERTS, D_MODEL, EXPERT_DIM), D_MODEL ** -0.5),
        'expert_w_down': nrm(ks[16], (L, N_EXPERTS, EXPERT_DIM, D_MODEL), EXPERT_DIM ** -0.5 * BETA),
        'shared_w_gate': nrm(ks[17], (L, D_MODEL, EXPERT_DIM), D_MODEL ** -0.5),
        'shared_w_up': nrm(ks[18], (L, D_MODEL, EXPERT_DIM), D_MODEL ** -0.5),
        'shared_w_down': nrm(ks[19], (L, EXPERT_DIM, D_MODEL), EXPERT_DIM ** -0.5 * BETA),
        'ln2_g': 1.0 + nrm(ks[20], (L, D_MODEL), 0.01),
        'ln2_b': nrm(ks[21], (L, D_MODEL), 0.01),
    }


def reference(x, in_w, in_b, gmlp_norm_g, gmlp_norm_b, spatial_w, spatial_b, proj_a_w,
              proj_b_w, out_w, ln1_g, ln1_b, router_w, router_bias, expert_w_gate,
              expert_w_up, expert_w_down, shared_w_gate, shared_w_up, shared_w_down,
              ln2_g, ln2_b):
    for l in range(DEPTH):
        mix = hybrid_mixer(x, in_w[l], in_b[l], gmlp_norm_g[l], gmlp_norm_b[l], spatial_w[l],
                           spatial_b[l], proj_a_w[l], proj_b_w[l], out_w[l])
        x = layer_norm(ALPHA * x + mix, ln1_g[l], ln1_b[l])
        ffn = moe_ffn(x, router_w[l], router_bias[l], expert_w_gate[l], expert_w_up[l],
                      expert_w_down[l], shared_w_gate[l], shared_w_up[l], shared_w_down[l])
        x = layer_norm(ALPHA * x + ffn, ln2_g[l], ln2_b[l])
    return x
```

```python
import functools
import math

import numpy as np
import jax
import jax.numpy as jnp
from jax import lax
from jax.experimental import pallas as pl
from jax.experimental.pallas import tpu as pltpu

F32 = jnp.float32
BF16 = jnp.bfloat16
U32 = jnp.uint32
I32 = jnp.int32

D_MODEL = 1024
GMLP_WIDTH = 1024
GMLP_GROUPS = 8
GMLP_CHUNK = 128
HEAD_DIM = 64
DILATED_PATTERNS = ((128, 1), (512, 4), (2048, 16))
HEADS_PER_GROUP = 4
GROUP_WIDTH = HEADS_PER_GROUP * HEAD_DIM
ATTN_WIDTH = GROUP_WIDTH * len(DILATED_PATTERNS)
ATTN_BLOCK = 128
N_EXPERTS = 256
TOP_K = 8
TOP_K_SHIFT = 3
N_EXPERT_GROUPS = 8
TOPK_GROUPS = 4
EXPERT_DIM = 256
ROUTED_SCALE = 2.5
LN_EPS = 1e-5
LANES = 128
MASKED_SCORE = -1e30

PROJ_TILE = 512
ATTN_QBLOCKS = 2
MERGE_TILE = 512
ROUTE_TILE = 256
MOE_BLOCK = 256
DISPATCH_TILE = 256
COMBINE_TILE = 128
VMEM_LIMIT = 56 * 1024 * 1024


def _layer_norm(y, g, b):
    mu = jnp.mean(y, axis=-1, keepdims=True)
    yc = y - mu
    var = jnp.mean(yc * yc, axis=-1, keepdims=True)
    return yc * lax.rsqrt(var + LN_EPS) * g + b


def _gelu(x):
    return 0.5 * x * (1.0 + lax.erf(x * np.float32(math.sqrt(0.5))))


def _pack_bf16_pairs(x):
    w = x.shape[1] // 2
    bits = pltpu.bitcast(x.astype(BF16).astype(F32), U32)
    return (bits[:, :w] >> 16) | (bits[:, w:] & jnp.uint32(0xFFFF0000))


def _unpack_bf16_pairs(w):
    lo = pltpu.bitcast(w << 16, F32)
    hi = pltpu.bitcast(w & jnp.uint32(0xFFFF0000), F32)
    return lo, hi


def _const_spec(shape):
    nd = len(shape)
    return pl.BlockSpec(shape, lambda *_: (0,) * nd)


def _proj_kernel(x_ref, wuv_ref, buv_ref, watt_ref, batt_ref, wg_ref, bg_ref, ng_ref, nb_ref,
                 sw_ref, sb_ref, pa_ref, ya_ref, gb_ref, a1_ref, a4_ref, a16_ref, xc_ref):
    tm = x_ref.shape[1]
    gw = GMLP_WIDTH
    xb = x_ref[0].astype(BF16)

    def proj(w, b):
        return jnp.dot(xb, w, preferred_element_type=F32) + b

    u = _gelu(proj(wuv_ref[:, :gw], buv_ref[:, :gw]))
    v = _gelu(proj(wuv_ref[:, gw:], buv_ref[:, gw:]))
    v = _layer_norm(v, ng_ref[...], nb_ref[...]).astype(BF16)

    cw = gw // GMLP_GROUPS
    row = lax.broadcasted_iota(I32, (GMLP_CHUNK, GMLP_CHUNK), 0)
    col = lax.broadcasted_iota(I32, (GMLP_CHUNK, GMLP_CHUNK), 1)
    ws = [jnp.where(row >= col, sw_ref[g], 0.0).astype(BF16) for g in range(GMLP_GROUPS)]
    chunks = []
    for c in range(tm // GMLP_CHUNK):
        vc = v[c * GMLP_CHUNK:(c + 1) * GMLP_CHUNK]
        cols = [jnp.dot(ws[g], vc[:, g * cw:(g + 1) * cw], preferred_element_type=F32)
                for g in range(GMLP_GROUPS)]
        chunks.append(jnp.concatenate(cols, axis=1) + sb_ref[...])
    vmix = jnp.concatenate(chunks, axis=0)
    ya = jnp.dot((u * vmix).astype(BF16), pa_ref[...], preferred_element_type=F32)
    ga = jax.nn.sigmoid(proj(wg_ref[:, :D_MODEL], bg_ref[:, :D_MODEL]))
    ya_ref[0] = (ga * ya).astype(BF16)
    gb_ref[0] = jax.nn.sigmoid(proj(wg_ref[:, D_MODEL:], bg_ref[:, D_MODEL:])).astype(BF16)

    n_chunks = x_ref.shape[2] // LANES
    for c in range(n_chunks):
        xc_ref[c] = x_ref[0, :, c * LANES:(c + 1) * LANES]
    for p, (a_ref, (_, d)) in enumerate(zip((a1_ref, a4_ref, a16_ref), DILATED_PATTERNS)):
        n = tm // d
        if d == 1:
            xp = xb
        else:
            xp = jnp.concatenate(
                [jnp.concatenate([xc_ref[c, pl.ds(r, n, stride=d), :] for c in range(n_chunks)], axis=1)
                 for r in range(d)], axis=0).astype(BF16)
        h = (jnp.dot(xp, watt_ref[p], preferred_element_type=F32) + batt_ref[p]).astype(BF16)
        for r in range(d):
            a_ref[0, r] = h[r * n:(r + 1) * n]


def _input_projection(x, wuv, buv, watt, batt, wg, bg, ng, nb, sw, sb, pa):
    B, S, D = x.shape
    tm = PROJ_TILE
    grid = (B, S // tm)
    out_shape = [jax.ShapeDtypeStruct((B, S, D), BF16), jax.ShapeDtypeStruct((B, S, D), BF16)]
    out_specs = [pl.BlockSpec((1, tm, D), lambda b, t: (b, t, 0)),
                 pl.BlockSpec((1, tm, D), lambda b, t: (b, t, 0))]
    for _, d in DILATED_PATTERNS:
        out_shape.append(jax.ShapeDtypeStruct((B, d, S // d, ATTN_WIDTH), BF16))
        out_specs.append(pl.BlockSpec((1, d, tm // d, ATTN_WIDTH), lambda b, t: (b, 0, t, 0)))
    consts = (wuv, buv, watt, batt, wg, bg, ng, nb, sw, sb, pa)
    return pl.pallas_call(
        _proj_kernel,
        grid=grid,
        in_specs=[pl.BlockSpec((1, tm, D), lambda b, t: (b, t, 0))] + [_const_spec(c.shape) for c in consts],
        out_specs=out_specs,
        out_shape=out_shape,
        scratch_shapes=[pltpu.VMEM((D // LANES, tm, LANES), F32)],
        compiler_params=pltpu.CompilerParams(
            dimension_semantics=("arbitrary", "arbitrary"), vmem_limit_bytes=VMEM_LIMIT),
        name="input_projection",
    )(x, *consts)


def _attn_kernel(qkv_ref, bias_ref, bias0_ref, o_ref, lse_ref):
    nq = pl.program_id(2)
    gwid = GROUP_WIDTH
    blk = ATTN_BLOCK
    lane = lax.broadcasted_iota(I32, (1, gwid), 1)
    head_masks = [(lane >= h * HEAD_DIM) & (lane < (h + 1) * HEAD_DIM) for h in range(HEADS_PER_GROUP)]
    q_scales = [jnp.where(m, np.float32(HEAD_DIM ** -0.5), 0.0).astype(BF16) for m in head_masks]
    for j in range(ATTN_QBLOCKS):
        n = nq * ATTN_QBLOCKS + j
        q0 = pl.multiple_of(n * blk, blk)
        p0 = pl.multiple_of(jnp.maximum(n - 1, 0) * blk, blk)
        q = qkv_ref[0, 0, pl.ds(q0, blk), 0:gwid]
        kk = jnp.concatenate([qkv_ref[0, 0, pl.ds(p0, blk), gwid:2 * gwid],
                              qkv_ref[0, 0, pl.ds(q0, blk), gwid:2 * gwid]], axis=0)
        vv = jnp.concatenate([qkv_ref[0, 0, pl.ds(p0, blk), 2 * gwid:3 * gwid],
                              qkv_ref[0, 0, pl.ds(q0, blk), 2 * gwid:3 * gwid]], axis=0)
        qs = jnp.concatenate([q * s for s in q_scales], axis=0)
        s = lax.dot_general(qs, kk, (((1,), (1,)), ((), ())), preferred_element_type=F32)
        s = s + jnp.where(n == 0, bias0_ref[...], bias_ref[...])
        m = jnp.max(s, axis=1, keepdims=True)
        p = jnp.exp(s - m)
        den = jnp.sum(p, axis=1, keepdims=True)
        pv = jnp.dot(p.astype(BF16), vv, preferred_element_type=F32)
        on = pv / den
        lse = m + jnp.log(den)
        o = jnp.zeros((blk, gwid), F32)
        l = jnp.zeros((blk, gwid), F32)
        for h in range(HEADS_PER_GROUP):
            o = jnp.where(head_masks[h], on[h * blk:(h + 1) * blk], o)
            l = jnp.where(head_masks[h], lse[h * blk:(h + 1) * blk], l)
        o_ref[0, 0, j * blk:(j + 1) * blk, :] = o.astype(BF16)
        lse_ref[0, 0, j * blk:(j + 1) * blk, :] = l


def _alibi_slopes(n):
    def pow2_slopes(m):
        start = 2.0 ** (-8.0 / m)
        return [start ** (i + 1) for i in range(m)]
    p = 2 ** int(math.floor(math.log2(n)))
    s = pow2_slopes(p)
    if p < n:
        s = s + pow2_slopes(2 * p)[0::2][: n - p]
    return np.array(sorted(s, reverse=True), dtype=np.float32)


def _attn_bias_tables(group, dilation):
    blk = ATTN_BLOCK
    slopes = _alibi_slopes(HEADS_PER_GROUP * len(DILATED_PATTERNS))
    slopes = slopes[group * HEADS_PER_GROUP:(group + 1) * HEADS_PER_GROUP]
    qi = np.arange(blk)[:, None]
    ki = np.arange(2 * blk)[None, :]
    delta = blk + qi - ki
    band = (delta >= 0) & (delta <= blk)
    bias = -slopes[:, None, None] * (delta * dilation).astype(np.float32)[None]
    full = np.where(band[None], bias, np.float32(MASKED_SCORE)).astype(np.float32)
    first = np.where((ki >= blk)[None], full, np.float32(MASKED_SCORE)).astype(np.float32)
    return full.reshape(HEADS_PER_GROUP * blk, 2 * blk), first.reshape(HEADS_PER_GROUP * blk, 2 * blk)


def _dilated_attention(qkv, group, dilation):
    B, d, sd, _ = qkv.shape
    rows = ATTN_QBLOCKS * ATTN_BLOCK
    bias, bias0 = _attn_bias_tables(group, dilation)
    grid = (B, d, sd // rows)
    out_spec = pl.BlockSpec((1, 1, rows, GROUP_WIDTH), lambda b, r, n: (b, r, n, 0))
    return pl.pallas_call(
        _attn_kernel,
        grid=grid,
        in_specs=[pl.BlockSpec((1, 1, sd, ATTN_WIDTH), lambda b, r, n: (b, r, 0, 0)),
                  _const_spec(bias.shape), _const_spec(bias0.shape)],
        out_specs=[out_spec, out_spec],
        out_shape=[jax.ShapeDtypeStruct((B, d, sd, GROUP_WIDTH), BF16),
                   jax.ShapeDtypeStruct((B, d, sd, GROUP_WIDTH), F32)],
        compiler_params=pltpu.CompilerParams(
            dimension_semantics=("arbitrary", "arbitrary", "arbitrary"), vmem_limit_bytes=VMEM_LIMIT),
        name=f"dilated_attention_d{dilation}",
    )(qkv, jnp.asarray(bias), jnp.asarray(bias0))


def _merge_kernel(o1_ref, l1_ref, o4_ref, l4_ref, o16_ref, l16_ref, ya_ref, gb_ref, x_ref,
                  pb_ref, ow_ref, g1_ref, b1_ref, rwh_ref, rwl_ref,
                  x1_ref, x1p_ref, sc_ref, so4, sl4, so16, sl16, *, alpha):
    tm = x_ref.shape[1]
    n_chunks = GROUP_WIDTH // LANES
    for (o_ref, l_ref, so, sl, d) in ((o4_ref, l4_ref, so4, sl4, 4), (o16_ref, l16_ref, so16, sl16, 16)):
        n = tm // d
        for r in range(d):
            o_r = o_ref[0, r].astype(F32)
            l_r = l_ref[0, r]
            for c in range(n_chunks):
                so[c, pl.ds(r, n, stride=d), :] = o_r[:, c * LANES:(c + 1) * LANES]
                sl[c, pl.ds(r, n, stride=d), :] = l_r[:, c * LANES:(c + 1) * LANES]

    def natural(s):
        return jnp.concatenate([s[c] for c in range(n_chunks)], axis=1)

    l1 = l1_ref[0, 0]
    l4 = natural(sl4)
    l16 = natural(sl16)
    lmax = jnp.maximum(jnp.maximum(l1, l4), l16)
    e1 = jnp.exp(l1 - lmax)
    e4 = jnp.exp(l4 - lmax)
    e16 = jnp.exp(l16 - lmax)
    yb = (e1 * o1_ref[0, 0].astype(F32) + e4 * natural(so4) + e16 * natural(so16)) / (e1 + e4 + e16)
    ybp = jnp.dot(yb.astype(BF16), pb_ref[...], preferred_element_type=F32)
    merged = ya_ref[0].astype(F32) + gb_ref[0].astype(F32) * ybp
    mix = jnp.dot(merged.astype(BF16), ow_ref[...], preferred_element_type=F32)
    x1 = _layer_norm(alpha * x_ref[0] + mix, g1_ref[...], b1_ref[...])
    x1_ref[0] = x1
    x1p_ref[0] = _pack_bf16_pairs(x1)
    hi = x1.astype(BF16)
    lo = (x1 - hi.astype(F32)).astype(BF16)
    logits = (jnp.dot(hi, rwh_ref[...], preferred_element_type=F32)
              + jnp.dot(lo, rwh_ref[...], preferred_element_type=F32)
              + jnp.dot(hi, rwl_ref[...], preferred_element_type=F32))
    sc_ref[0] = jax.nn.sigmoid(logits)


def _merge_and_norm(attn_outs, ya, gb, x, pb, ow, g1, b1, rwh, rwl, alpha):
    B, S, D = x.shape
    tm = MERGE_TILE
    in_specs = []
    args = []
    for (o, l), (_, d) in zip(attn_outs, DILATED_PATTERNS):
        spec = pl.BlockSpec((1, d, tm // d, GROUP_WIDTH), lambda b, t: (b, 0, t, 0))
        in_specs += [spec, spec]
        args += [o, l]
    tok_spec = pl.BlockSpec((1, tm, D), lambda b, t: (b, t, 0))
    in_specs += [tok_spec, tok_spec, tok_spec]
    args += [ya, gb, x]
    consts = (pb, ow, g1, b1, rwh, rwl)
    in_specs += [_const_spec(c.shape) for c in consts]
    return pl.pallas_call(
        functools.partial(_merge_kernel, alpha=alpha),
        grid=(B, S // tm),
        in_specs=in_specs,
        out_specs=[tok_spec,
                   pl.BlockSpec((1, tm, D // 2), lambda b, t: (b, t, 0)),
                   pl.BlockSpec((1, tm, N_EXPERTS), lambda b, t: (b, t, 0))],
        out_shape=[jax.ShapeDtypeStruct((B, S, D), F32),
                   jax.ShapeDtypeStruct((B, S, D // 2), U32),
                   jax.ShapeDtypeStruct((B, S, N_EXPERTS), F32)],
        scratch_shapes=[pltpu.VMEM((GROUP_WIDTH // LANES, tm, LANES), F32) for _ in range(4)],
        compiler_params=pltpu.CompilerParams(
            dimension_semantics=("arbitrary", "arbitrary"), vmem_limit_bytes=VMEM_LIMIT),
        name="merge_norm_router",
    )(*args, *consts)


def _route_kernel(sc_ref, bias_ref, eidx_ref, rank_ref, gate_ref, cnt_ref, carry_ref):
    tm = sc_ref.shape[0]
    ne = N_EXPERTS
    gsize = ne // N_EXPERT_GROUPS
    neg_inf = np.float32(-np.inf)

    @pl.when(pl.program_id(0) == 0)
    def _():
        carry_ref[...] = jnp.zeros_like(carry_ref)

    scores = sc_ref[...]
    biased = scores + bias_ref[...]
    lane = lax.broadcasted_iota(I32, (tm, ne), 1).astype(F32)
    inv_gsize = np.float32(1.0 / gsize)
    grp = jnp.floor(lane * inv_gsize)

    def first_argmax(vals, avail):
        cur = jnp.where(avail, vals, neg_inf)
        m = jnp.max(cur, axis=1, keepdims=True)
        cand = avail & (cur == m)
        return jnp.min(jnp.where(cand, lane, np.float32(ne)), axis=1, keepdims=True)

    gscore = jnp.zeros((tm, ne), F32)
    for g in range(N_EXPERT_GROUPS):
        gm = grp == g
        v = jnp.where(gm, biased, neg_inf)
        m1 = jnp.max(v, axis=1, keepdims=True)
        n1 = jnp.sum(jnp.where(v == m1, 1.0, 0.0), axis=1, keepdims=True)
        m2 = jnp.max(jnp.where(v < m1, v, neg_inf), axis=1, keepdims=True)
        gscore = jnp.where(gm, m1 + jnp.where(n1 >= 2.0, m1, m2), gscore)

    avail = jnp.full((tm, ne), True)
    for _ in range(TOPK_GROUPS):
        idx = first_argmax(gscore, avail)
        avail = avail & (grp != jnp.floor(idx * inv_gsize))
    expert_mask = jnp.logical_not(avail)

    masked = jnp.where(expert_mask, biased, neg_inf)
    avail = jnp.full((tm, ne), True)
    picks = []
    for _ in range(TOP_K):
        idx = first_argmax(masked, avail)
        onehot = lane == idx
        avail = avail & jnp.logical_not(onehot)
        picks.append((idx, onehot))

    sel = jnp.where(avail, 0.0, 1.0)
    r_i = lax.broadcasted_iota(I32, (tm, tm), 0)
    c_i = lax.broadcasted_iota(I32, (tm, tm), 1)
    earlier = jnp.where(r_i > c_i, 1.0, 0.0).astype(BF16)
    ranks = jnp.dot(earlier, sel.astype(BF16), preferred_element_type=F32) + carry_ref[...]
    carry_ref[...] = carry_ref[...] + jnp.sum(sel, axis=0, keepdims=True)
    cnt_ref[...] = carry_ref[...]

    s_k = [jnp.sum(jnp.where(oh, scores, 0.0), axis=1, keepdims=True) for _, oh in picks]
    r_k = [jnp.sum(jnp.where(oh, ranks, 0.0), axis=1, keepdims=True) for _, oh in picks]
    total = s_k[0]
    for s in s_k[1:]:
        total = total + s
    lane_out = lax.broadcasted_iota(I32, (tm, 128), 1)
    e_out = jnp.zeros((tm, 128), I32)
    r_out = jnp.zeros((tm, 128), I32)
    g_out = jnp.zeros((tm, 128), F32)
    for k in range(TOP_K):
        here = lane_out == k
        e_out = jnp.where(here, picks[k][0].astype(I32), e_out)
        r_out = jnp.where(here, r_k[k].astype(I32), r_out)
        g_out = jnp.where(here, s_k[k] / total * np.float32(ROUTED_SCALE), g_out)
    eidx_ref[...] = e_out
    rank_ref[...] = r_out
    gate_ref[...] = g_out


def _route(scores, bias):
    T = scores.shape[0]
    tm = ROUTE_TILE
    out_spec = pl.BlockSpec((tm, 128), lambda i: (i, 0))
    return pl.pallas_call(
        _route_kernel,
        grid=(T // tm,),
        in_specs=[pl.BlockSpec((tm, N_EXPERTS), lambda i: (i, 0)), _const_spec(bias.shape)],
        out_specs=[out_spec, out_spec, out_spec, _const_spec((1, N_EXPERTS))],
        out_shape=[jax.ShapeDtypeStruct((T, 128), I32), jax.ShapeDtypeStruct((T, 128), I32),
                   jax.ShapeDtypeStruct((T, 128), F32), jax.ShapeDtypeStruct((1, N_EXPERTS), F32)],
        scratch_shapes=[pltpu.VMEM((1, N_EXPERTS), F32)],
        compiler_params=pltpu.CompilerParams(dimension_semantics=("arbitrary",)),
        name="route_topk",
    )(scores, bias)


def _dispatch_kernel(eidx_ref, rank_ref, pstart_ref, pend_ref, x_hbm, xs_hbm, dest_ref, zeros_ref, sem, zsem):
    i = pl.program_id(0)
    td = DISPATCH_TILE
    n_assign = td * TOP_K

    @pl.when(i == 0)
    def _():
        zeros_ref[...] = jnp.zeros_like(zeros_ref)

        def zcopy(e):
            start = pl.multiple_of(pend_ref[e] - MOE_BLOCK, MOE_BLOCK)
            return pltpu.make_async_copy(zeros_ref, xs_hbm.at[pl.ds(start, MOE_BLOCK)], zsem)

        def zstart(e, c):
            @pl.when(pend_ref[e] > pstart_ref[e])
            def _():
                zcopy(e).start()
            return c

        def zwait(e, c):
            @pl.when(pend_ref[e] > pstart_ref[e])
            def _():
                zcopy(e).wait()
            return c

        lax.fori_loop(0, N_EXPERTS, zstart, 0)
        lax.fori_loop(0, N_EXPERTS, zwait, 0)

    def row_copy(tok, dst):
        return pltpu.make_async_copy(x_hbm.at[pl.ds(tok, 1)], xs_hbm.at[pl.ds(dst, 1)], sem)

    def issue(j, c):
        dst = pstart_ref[eidx_ref[j]] + rank_ref[j]
        dest_ref[j] = dst
        row_copy(i * td + lax.shift_right_logical(j, TOP_K_SHIFT), dst).start()
        return c

    lax.fori_loop(0, n_assign, issue, 0, unroll=8)

    def drain(j, c):
        row_copy(0, 0).wait()
        return c

    lax.fori_loop(0, n_assign, drain, 0, unroll=8)


def _dispatch(eidx_flat, rank_flat, pstart, pend, x1p, n_rows):
    T = x1p.shape[0]
    td = DISPATCH_TILE
    blk = pl.BlockSpec((td * TOP_K,), lambda i: (i,), memory_space=pltpu.SMEM)
    whole = pl.BlockSpec(memory_space=pltpu.SMEM)
    return pl.pallas_call(
        _dispatch_kernel,
        grid=(T // td,),
        in_specs=[blk, blk, whole, whole, pl.BlockSpec(memory_space=pl.ANY)],
        out_specs=[pl.BlockSpec(memory_space=pl.ANY), blk],
        out_shape=[jax.ShapeDtypeStruct((n_rows, x1p.shape[1]), U32),
                   jax.ShapeDtypeStruct((T * TOP_K,), I32)],
        scratch_shapes=[pltpu.VMEM((MOE_BLOCK, x1p.shape[1]), U32),
                        pltpu.SemaphoreType.DMA, pltpu.SemaphoreType.DMA],
        compiler_params=pltpu.CompilerParams(dimension_semantics=("arbitrary",)),
        name="moe_dispatch",
    )(eidx_flat, rank_flat, pstart, pend, x1p)


def _expert_kernel(be_ref, nu_ref, xs_ref, wg_ref, wu_ref, wd_ref, y_ref, wg_s, wu_s, wd_s):
    i = pl.program_id(0)
    half = D_MODEL // 2

    @pl.when(i < nu_ref[0])
    def _():
        prev = be_ref[jnp.maximum(i - 1, 0)]

        @pl.when((i == 0) | (be_ref[i] != prev))
        def _():
            wg_s[...] = wg_ref[0].astype(BF16)
            wu_s[...] = wu_ref[0].astype(BF16)
            wd_s[...] = wd_ref[0].astype(BF16)

        lo, hi = _unpack_bf16_pairs(xs_ref[...])
        xlo = lo.astype(BF16)
        xhi = hi.astype(BF16)
        g = (jnp.dot(xlo, wg_s[:half], preferred_element_type=F32)
             + jnp.dot(xhi, wg_s[half:], preferred_element_type=F32))
        u = (jnp.dot(xlo, wu_s[:half], preferred_element_type=F32)
             + jnp.dot(xhi, wu_s[half:], preferred_element_type=F32))
        hb = (g * jax.nn.sigmoid(g) * u).astype(BF16)
        y = jnp.dot(hb, wd_s[...], preferred_element_type=F32)
        y_ref[...] = _pack_bf16_pairs(y)


def _experts(block_e, n_used, xs, wg, wu, wd):
    n_rows, half = xs.shape
    n_blocks = n_rows // MOE_BLOCK

    def row_map(i, be, nu):
        return (jnp.minimum(i, nu[0] - 1), 0)

    def w_map(i, be, nu):
        return (be[jnp.minimum(i, nu[0] - 1)], 0, 0)

    grid_spec = pltpu.PrefetchScalarGridSpec(
        num_scalar_prefetch=2,
        grid=(n_blocks,),
        in_specs=[pl.BlockSpec((MOE_BLOCK, half), row_map),
                  pl.BlockSpec((1, D_MODEL, EXPERT_DIM), w_map),
                  pl.BlockSpec((1, D_MODEL, EXPERT_DIM), w_map),
                  pl.BlockSpec((1, EXPERT_DIM, D_MODEL), w_map)],
        out_specs=pl.BlockSpec((MOE_BLOCK, half), row_map),
        scratch_shapes=[pltpu.VMEM((D_MODEL, EXPERT_DIM), BF16),
                        pltpu.VMEM((D_MODEL, EXPERT_DIM), BF16),
                        pltpu.VMEM((EXPERT_DIM, D_MODEL), BF16)],
    )
    return pl.pallas_call(
        _expert_kernel,
        grid_spec=grid_spec,
        out_shape=jax.ShapeDtypeStruct((n_rows, half), U32),
        compiler_params=pltpu.CompilerParams(dimension_semantics=("arbitrary",), vmem_limit_bytes=VMEM_LIMIT),
        name="moe_experts",
    )(block_e, n_used, xs, wg, wu, wd)


def _combine_kernel(dcur_ref, dnext_ref, y_hbm, gate_ref, x1_ref, swgu_ref, swd_ref, g2_ref, b2_ref,
                    out_ref, buf, sem, *, alpha):
    i = pl.program_id(0)
    n = pl.num_programs(0)
    tc = COMBINE_TILE
    slot = i % 2

    def row_copy(row, k, t, s):
        return pltpu.make_async_copy(y_hbm.at[pl.ds(row, 1)], buf.at[s, k, pl.ds(t, 1)], sem.at[s])

    def issue_from(d_ref, s):
        def issue(j, c):
            row_copy(d_ref[j], j & (TOP_K - 1), lax.shift_right_logical(j, TOP_K_SHIFT), s).start()
            return c
        lax.fori_loop(0, tc * TOP_K, issue, 0, unroll=8)

    @pl.when(i == 0)
    def _():
        issue_from(dcur_ref, 0)

    @pl.when(i + 1 < n)
    def _():
        issue_from(dnext_ref, 1 - slot)

    x1 = x1_ref[...]
    gu = jnp.dot(x1.astype(BF16), swgu_ref[...], preferred_element_type=F32)
    g = gu[:, :EXPERT_DIM]
    u = gu[:, EXPERT_DIM:]
    shared = jnp.dot((g * jax.nn.sigmoid(g) * u).astype(BF16), swd_ref[...], preferred_element_type=F32)

    def drain(j, c):
        row_copy(0, 0, 0, slot).wait()
        return c

    lax.fori_loop(0, tc * TOP_K, drain, 0, unroll=8)

    half = D_MODEL // 2
    acc_lo = jnp.zeros((tc, half), F32)
    acc_hi = jnp.zeros((tc, half), F32)
    for k in range(TOP_K):
        lo, hi = _unpack_bf16_pairs(buf[slot, k])
        gk = gate_ref[:, k:k + 1]
        acc_lo = acc_lo + gk * lo
        acc_hi = acc_hi + gk * hi
    routed = jnp.concatenate([acc_lo, acc_hi], axis=1)
    out_ref[...] = _layer_norm(alpha * x1 + (routed + shared), g2_ref[...], b2_ref[...])


def _combine(dest_flat, y_rows, gate, x1, swgu, swd, g2, b2, alpha):
    T, D = x1.shape
    tc = COMBINE_TILE
    n = T // tc
    consts = (swgu, swd, g2, b2)
    return pl.pallas_call(
        functools.partial(_combine_kernel, alpha=alpha),
        grid=(n,),
        in_specs=[pl.BlockSpec((tc * TOP_K,), lambda i: (i,), memory_space=pltpu.SMEM),
                  pl.BlockSpec((tc * TOP_K,), lambda i: (jnp.minimum(i + 1, n - 1),), memory_space=pltpu.SMEM),
                  pl.BlockSpec(memory_space=pl.ANY),
                  pl.BlockSpec((tc, 128), lambda i: (i, 0)),
                  pl.BlockSpec((tc, D), lambda i: (i, 0))] + [_const_spec(c.shape) for c in consts],
        out_specs=pl.BlockSpec((tc, D), lambda i: (i, 0)),
        out_shape=jax.ShapeDtypeStruct((T, D), F32),
        scratch_shapes=[pltpu.VMEM((2, TOP_K, tc, D // 2), U32), pltpu.SemaphoreType.DMA((2,))],
        compiler_params=pltpu.CompilerParams(dimension_semantics=("arbitrary",), vmem_limit_bytes=VMEM_LIMIT),
        name="moe_combine_norm",
    )(dest_flat, dest_flat, y_rows, gate, x1, *consts)


def _mixer(x, in_w, in_b, ng, nb, spatial_w, spatial_b, proj_a_w, proj_b_w, out_w, ln1_g, ln1_b,
           router_w, alpha):
    gw, aw, D = GMLP_WIDTH, ATTN_WIDTH, D_MODEL
    w = in_w.astype(BF16)
    q0 = 2 * gw
    wuv, buv = w[:, :q0], in_b[None, :q0]
    watt = jnp.stack([jnp.concatenate([w[:, q0 + s * aw + p * GROUP_WIDTH:q0 + s * aw + (p + 1) * GROUP_WIDTH]
                                       for s in range(3)], axis=1) for p in range(len(DILATED_PATTERNS))])
    batt = jnp.stack([jnp.concatenate([in_b[q0 + s * aw + p * GROUP_WIDTH:q0 + s * aw + (p + 1) * GROUP_WIDTH]
                                       for s in range(3)])[None] for p in range(len(DILATED_PATTERNS))])
    g0 = q0 + 3 * aw
    wg, bg = w[:, g0:], in_b[None, g0:]
    sb = jnp.repeat(spatial_b.T, gw // GMLP_GROUPS, axis=1)
    ya, gb, a1, a4, a16 = _input_projection(
        x, wuv, buv, watt, batt, wg, bg, ng[None], nb[None], spatial_w, sb, proj_a_w.astype(BF16))
    attn_outs = [_dilated_attention(a, p, d) for p, (a, (_, d)) in enumerate(zip((a1, a4, a16), DILATED_PATTERNS))]
    rwh = router_w.astype(BF16)
    rwl = (router_w - rwh.astype(F32)).astype(BF16)
    return _merge_and_norm(attn_outs, ya, gb, x, proj_b_w.astype(BF16), out_w.astype(BF16),
                           ln1_g[None], ln1_b[None], rwh, rwl, alpha)


def _moe(x1, x1p, scores, router_bias, w_gate, w_up, w_down, sw_gate, sw_up, sw_down, ln2_g, ln2_b, alpha):
    T = x1.shape[0]
    A = T * TOP_K
    n_blocks = (A + N_EXPERTS * (MOE_BLOCK - 1)) // MOE_BLOCK
    eidx, rank, gate, counts = _route(scores, router_bias[None])
    counts = counts[0].astype(I32)
    padded = (counts + MOE_BLOCK - 1) // MOE_BLOCK * MOE_BLOCK
    pend = jnp.cumsum(padded).astype(I32)
    pstart = pend - padded
    n_used = (pend[-1:] // MOE_BLOCK).astype(I32)
    block_e = jnp.minimum(
        jnp.searchsorted(pend, jnp.arange(n_blocks, dtype=I32) * MOE_BLOCK, side='right'),
        N_EXPERTS - 1).astype(I32)
    xs, dest = _dispatch(eidx[:, :TOP_K].reshape(A), rank[:, :TOP_K].reshape(A), pstart, pend, x1p,
                         n_blocks * MOE_BLOCK)
    y_rows = _experts(block_e, n_used, xs, w_gate, w_up, w_down)
    swgu = jnp.concatenate([sw_gate, sw_up], axis=1).astype(BF16)
    return _combine(dest, y_rows, gate, x1, swgu, sw_down.astype(BF16), ln2_g[None], ln2_b[None], alpha)


def kernel(x, in_w, in_b, gmlp_norm_g, gmlp_norm_b, spatial_w, spatial_b, proj_a_w, proj_b_w, out_w,
           ln1_g, ln1_b, router_w, router_bias, expert_w_gate, expert_w_up, expert_w_down,
           shared_w_gate, shared_w_up, shared_w_down, ln2_g, ln2_b):
    B, S, D = x.shape
    depth = in_w.shape[0]
    alpha = np.float32((2.0 * depth) ** 0.25)
    for l in range(depth):
        x1, x1p, scores = _mixer(x, in_w[l], in_b[l], gmlp_norm_g[l], gmlp_norm_b[l], spatial_w[l],
                                 spatial_b[l], proj_a_w[l], proj_b_w[l], out_w[l], ln1_g[l], ln1_b[l],
                                 router_w[l], alpha)
        out = _moe(x1.reshape(B * S, D), x1p.reshape(B * S, D // 2), scores.reshape(B * S, N_EXPERTS),
                   router_bias[l], expert_w_gate[l], expert_w_up[l], expert_w_down[l],
                   shared_w_gate[l], shared_w_up[l], shared_w_down[l], ln2_g[l], ln2_b[l], alpha)
        x = out.reshape(B, S, D)
    return x
```

```python
import functools
import math

import numpy as np
import jax
import jax.numpy as jnp
from jax import lax
from jax.experimental import pallas as pl
from jax.experimental.pallas import tpu as pltpu
from jax.experimental.pallas import tpu_sc as plsc

F32 = jnp.float32
BF16 = jnp.bfloat16
U32 = jnp.uint32
I32 = jnp.int32

D_MODEL = 1024
GMLP_WIDTH = 1024
GMLP_GROUPS = 8
GMLP_CHUNK = 128
HEAD_DIM = 64
DILATED_PATTERNS = ((128, 1), (512, 4), (2048, 16))
HEADS_PER_GROUP = 4
GROUP_WIDTH = HEADS_PER_GROUP * HEAD_DIM
ATTN_WIDTH = GROUP_WIDTH * len(DILATED_PATTERNS)
ATTN_BLOCK = 128
N_EXPERTS = 256
TOP_K = 8
TOP_K_SHIFT = 3
N_EXPERT_GROUPS = 8
TOPK_GROUPS = 4
EXPERT_DIM = 256
ROUTED_SCALE = 2.5
LN_EPS = 1e-5
LANES = 128
MASKED_SCORE = -1e30

PROJ_TILE = 512
ATTN_QBLOCKS = 2
MERGE_TILE = 512
ROUTE_TILE = 256
MOE_BLOCK = 256
SC_CHUNK = 64
COMBINE_TILE = 128
VMEM_LIMIT = 56 * 1024 * 1024


def _layer_norm(y, g, b):
    mu = jnp.mean(y, axis=-1, keepdims=True)
    yc = y - mu
    var = jnp.mean(yc * yc, axis=-1, keepdims=True)
    return yc * lax.rsqrt(var + LN_EPS) * g + b


def _gelu(x):
    return 0.5 * x * (1.0 + lax.erf(x * np.float32(math.sqrt(0.5))))


def _pack_bf16_pairs(x):
    w = x.shape[1] // 2
    bits = pltpu.bitcast(x.astype(BF16).astype(F32), U32)
    return (bits[:, :w] >> 16) | (bits[:, w:] & jnp.uint32(0xFFFF0000))


def _unpack_bf16_pairs(w):
    lo = pltpu.bitcast(w << 16, F32)
    hi = pltpu.bitcast(w & jnp.uint32(0xFFFF0000), F32)
    return lo, hi


def _const_spec(shape):
    nd = len(shape)
    return pl.BlockSpec(shape, lambda *_: (0,) * nd)


def _proj_kernel(x_ref, wuv_ref, buv_ref, watt_ref, batt_ref, wg_ref, bg_ref, ng_ref, nb_ref,
                 sw_ref, sb_ref, pa_ref, ya_ref, gb_ref, a1_ref, a4_ref, a16_ref, xc_ref):
    tm = x_ref.shape[1]
    gw = GMLP_WIDTH
    xb = x_ref[0].astype(BF16)

    def proj(w, b):
        return jnp.dot(xb, w, preferred_element_type=F32) + b

    u = _gelu(proj(wuv_ref[:, :gw], buv_ref[:, :gw]))
    v = _gelu(proj(wuv_ref[:, gw:], buv_ref[:, gw:]))
    v = _layer_norm(v, ng_ref[...], nb_ref[...]).astype(BF16)

    cw = gw // GMLP_GROUPS
    row = lax.broadcasted_iota(I32, (GMLP_CHUNK, GMLP_CHUNK), 0)
    col = lax.broadcasted_iota(I32, (GMLP_CHUNK, GMLP_CHUNK), 1)
    ws = [jnp.where(row >= col, sw_ref[g], 0.0).astype(BF16) for g in range(GMLP_GROUPS)]
    chunks = []
    for c in range(tm // GMLP_CHUNK):
        vc = v[c * GMLP_CHUNK:(c + 1) * GMLP_CHUNK]
        cols = [jnp.dot(ws[g], vc[:, g * cw:(g + 1) * cw], preferred_element_type=F32)
                for g in range(GMLP_GROUPS)]
        chunks.append(jnp.concatenate(cols, axis=1) + sb_ref[...])
    vmix = jnp.concatenate(chunks, axis=0)
    ya = jnp.dot((u * vmix).astype(BF16), pa_ref[...], preferred_element_type=F32)
    ga = jax.nn.sigmoid(proj(wg_ref[:, :D_MODEL], bg_ref[:, :D_MODEL]))
    ya_ref[0] = (ga * ya).astype(BF16)
    gb_ref[0] = jax.nn.sigmoid(proj(wg_ref[:, D_MODEL:], bg_ref[:, D_MODEL:])).astype(BF16)

    n_chunks = x_ref.shape[2] // LANES
    for c in range(n_chunks):
        xc_ref[c] = x_ref[0, :, c * LANES:(c + 1) * LANES]
    for p, (a_ref, (_, d)) in enumerate(zip((a1_ref, a4_ref, a16_ref), DILATED_PATTERNS)):
        n = tm // d
        if d == 1:
            xp = xb
        else:
            xp = jnp.concatenate(
                [jnp.concatenate([xc_ref[c, pl.ds(r, n, stride=d), :] for c in range(n_chunks)], axis=1)
                 for r in range(d)], axis=0).astype(BF16)
        h = (jnp.dot(xp, watt_ref[p], preferred_element_type=F32) + batt_ref[p]).astype(BF16)
        for r in range(d):
            a_ref[0, r] = h[r * n:(r + 1) * n]


def _input_projection(x, wuv, buv, watt, batt, wg, bg, ng, nb, sw, sb, pa):
    B, S, D = x.shape
    tm = PROJ_TILE
    grid = (B, S // tm)
    out_shape = [jax.ShapeDtypeStruct((B, S, D), BF16), jax.ShapeDtypeStruct((B, S, D), BF16)]
    out_specs = [pl.BlockSpec((1, tm, D), lambda b, t: (b, t, 0)),
                 pl.BlockSpec((1, tm, D), lambda b, t: (b, t, 0))]
    for _, d in DILATED_PATTERNS:
        out_shape.append(jax.ShapeDtypeStruct((B, d, S // d, ATTN_WIDTH), BF16))
        out_specs.append(pl.BlockSpec((1, d, tm // d, ATTN_WIDTH), lambda b, t: (b, 0, t, 0)))
    consts = (wuv, buv, watt, batt, wg, bg, ng, nb, sw, sb, pa)
    return pl.pallas_call(
        _proj_kernel,
        grid=grid,
        in_specs=[pl.BlockSpec((1, tm, D), lambda b, t: (b, t, 0))] + [_const_spec(c.shape) for c in consts],
        out_specs=out_specs,
        out_shape=out_shape,
        scratch_shapes=[pltpu.VMEM((D // LANES, tm, LANES), F32)],
        compiler_params=pltpu.CompilerParams(
            dimension_semantics=("arbitrary", "arbitrary"), vmem_limit_bytes=VMEM_LIMIT),
        name="input_projection",
    )(x, *consts)


def _attn_kernel(qkv_ref, bias_ref, bias0_ref, o_ref, lse_ref):
    nq = pl.program_id(2)
    gwid = GROUP_WIDTH
    blk = ATTN_BLOCK
    lane = lax.broadcasted_iota(I32, (1, gwid), 1)
    head_masks = [(lane >= h * HEAD_DIM) & (lane < (h + 1) * HEAD_DIM) for h in range(HEADS_PER_GROUP)]
    q_scales = [jnp.where(m, np.float32(HEAD_DIM ** -0.5), 0.0).astype(BF16) for m in head_masks]
    for j in range(ATTN_QBLOCKS):
        n = nq * ATTN_QBLOCKS + j
        q0 = pl.multiple_of(n * blk, blk)
        p0 = pl.multiple_of(jnp.maximum(n - 1, 0) * blk, blk)
        q = qkv_ref[0, 0, pl.ds(q0, blk), 0:gwid]
        kk = jnp.concatenate([qkv_ref[0, 0, pl.ds(p0, blk), gwid:2 * gwid],
                              qkv_ref[0, 0, pl.ds(q0, blk), gwid:2 * gwid]], axis=0)
        vv = jnp.concatenate([qkv_ref[0, 0, pl.ds(p0, blk), 2 * gwid:3 * gwid],
                              qkv_ref[0, 0, pl.ds(q0, blk), 2 * gwid:3 * gwid]], axis=0)
        qs = jnp.concatenate([q * s for s in q_scales], axis=0)
        s = lax.dot_general(qs, kk, (((1,), (1,)), ((), ())), preferred_element_type=F32)
        s = s + jnp.where(n == 0, bias0_ref[...], bias_ref[...])
        m = jnp.max(s, axis=1, keepdims=True)
        p = jnp.exp(s - m)
        den = jnp.sum(p, axis=1, keepdims=True)
        pv = jnp.dot(p.astype(BF16), vv, preferred_element_type=F32)
        on = pv / den
        lse = m + jnp.log(den)
        o = jnp.zeros((blk, gwid), F32)
        l = jnp.zeros((blk, gwid), F32)
        for h in range(HEADS_PER_GROUP):
            o = jnp.where(head_masks[h], on[h * blk:(h + 1) * blk], o)
            l = jnp.where(head_masks[h], lse[h * blk:(h + 1) * blk], l)
        o_ref[0, 0, j * blk:(j + 1) * blk, :] = o.astype(BF16)
        lse_ref[0, 0, j * blk:(j + 1) * blk, :] = l


def _alibi_slopes(n):
    def pow2_slopes(m):
        start = 2.0 ** (-8.0 / m)
        return [start ** (i + 1) for i in range(m)]
    p = 2 ** int(math.floor(math.log2(n)))
    s = pow2_slopes(p)
    if p < n:
        s = s + pow2_slopes(2 * p)[0::2][: n - p]
    return np.array(sorted(s, reverse=True), dtype=np.float32)


def _attn_bias_tables(group, dilation):
    blk = ATTN_BLOCK
    slopes = _alibi_slopes(HEADS_PER_GROUP * len(DILATED_PATTERNS))
    slopes = slopes[group * HEADS_PER_GROUP:(group + 1) * HEADS_PER_GROUP]
    qi = np.arange(blk)[:, None]
    ki = np.arange(2 * blk)[None, :]
    delta = blk + qi - ki
    band = (delta >= 0) & (delta <= blk)
    bias = -slopes[:, None, None] * (delta * dilation).astype(np.float32)[None]
    full = np.where(band[None], bias, np.float32(MASKED_SCORE)).astype(np.float32)
    first = np.where((ki >= blk)[None], full, np.float32(MASKED_SCORE)).astype(np.float32)
    return full.reshape(HEADS_PER_GROUP * blk, 2 * blk), first.reshape(HEADS_PER_GROUP * blk, 2 * blk)


def _dilated_attention(qkv, group, dilation):
    B, d, sd, _ = qkv.shape
    rows = ATTN_QBLOCKS * ATTN_BLOCK
    bias, bias0 = _attn_bias_tables(group, dilation)
    grid = (B, d, sd // rows)
    out_spec = pl.BlockSpec((1, 1, rows, GROUP_WIDTH), lambda b, r, n: (b, r, n, 0))
    return pl.pallas_call(
        _attn_kernel,
        grid=grid,
        in_specs=[pl.BlockSpec((1, 1, sd, ATTN_WIDTH), lambda b, r, n: (b, r, 0, 0)),
                  _const_spec(bias.shape), _const_spec(bias0.shape)],
        out_specs=[out_spec, out_spec],
        out_shape=[jax.ShapeDtypeStruct((B, d, sd, GROUP_WIDTH), BF16),
                   jax.ShapeDtypeStruct((B, d, sd, GROUP_WIDTH), F32)],
        compiler_params=pltpu.CompilerParams(
            dimension_semantics=("arbitrary", "arbitrary", "arbitrary"), vmem_limit_bytes=VMEM_LIMIT),
        name=f"dilated_attention_d{dilation}",
    )(qkv, jnp.asarray(bias), jnp.asarray(bias0))


def _merge_kernel(o1_ref, l1_ref, o4_ref, l4_ref, o16_ref, l16_ref, ya_ref, gb_ref, x_ref,
                  pb_ref, ow_ref, g1_ref, b1_ref, rwh_ref, rwl_ref,
                  x1_ref, x1p_ref, sc_ref, so4, sl4, so16, sl16, *, alpha):
    tm = x_ref.shape[1]
    n_chunks = GROUP_WIDTH // LANES
    for (o_ref, l_ref, so, sl, d) in ((o4_ref, l4_ref, so4, sl4, 4), (o16_ref, l16_ref, so16, sl16, 16)):
        n = tm // d
        for r in range(d):
            o_r = o_ref[0, r].astype(F32)
            l_r = l_ref[0, r]
            for c in range(n_chunks):
                so[c, pl.ds(r, n, stride=d), :] = o_r[:, c * LANES:(c + 1) * LANES]
                sl[c, pl.ds(r, n, stride=d), :] = l_r[:, c * LANES:(c + 1) * LANES]

    def natural(s):
        return jnp.concatenate([s[c] for c in range(n_chunks)], axis=1)

    l1 = l1_ref[0, 0]
    l4 = natural(sl4)
    l16 = natural(sl16)
    lmax = jnp.maximum(jnp.maximum(l1, l4), l16)
    e1 = jnp.exp(l1 - lmax)
    e4 = jnp.exp(l4 - lmax)
    e16 = jnp.exp(l16 - lmax)
    yb = (e1 * o1_ref[0, 0].astype(F32) + e4 * natural(so4) + e16 * natural(so16)) / (e1 + e4 + e16)
    ybp = jnp.dot(yb.astype(BF16), pb_ref[...], preferred_element_type=F32)
    merged = ya_ref[0].astype(F32) + gb_ref[0].astype(F32) * ybp
    mix = jnp.dot(merged.astype(BF16), ow_ref[...], preferred_element_type=F32)
    x1 = _layer_norm(alpha * x_ref[0] + mix, g1_ref[...], b1_ref[...])
    x1_ref[0] = x1
    x1p_ref[0] = _pack_bf16_pairs(x1)
    hi = x1.astype(BF16)
    lo = (x1 - hi.astype(F32)).astype(BF16)
    logits = (jnp.dot(hi, rwh_ref[...], preferred_element_type=F32)
              + jnp.dot(lo, rwh_ref[...], preferred_element_type=F32)
              + jnp.dot(hi, rwl_ref[...], preferred_element_type=F32))
    sc_ref[0] = jax.nn.sigmoid(logits)


def _merge_and_norm(attn_outs, ya, gb, x, pb, ow, g1, b1, rwh, rwl, alpha):
    B, S, D = x.shape
    tm = MERGE_TILE
    in_specs = []
    args = []
    for (o, l), (_, d) in zip(attn_outs, DILATED_PATTERNS):
        spec = pl.BlockSpec((1, d, tm // d, GROUP_WIDTH), lambda b, t: (b, 0, t, 0))
        in_specs += [spec, spec]
        args += [o, l]
    tok_spec = pl.BlockSpec((1, tm, D), lambda b, t: (b, t, 0))
    in_specs += [tok_spec, tok_spec, tok_spec]
    args += [ya, gb, x]
    consts = (pb, ow, g1, b1, rwh, rwl)
    in_specs += [_const_spec(c.shape) for c in consts]
    return pl.pallas_call(
        functools.partial(_merge_kernel, alpha=alpha),
        grid=(B, S // tm),
        in_specs=in_specs,
        out_specs=[tok_spec,
                   pl.BlockSpec((1, tm, D // 2), lambda b, t: (b, t, 0)),
                   pl.BlockSpec((1, tm, N_EXPERTS), lambda b, t: (b, t, 0))],
        out_shape=[jax.ShapeDtypeStruct((B, S, D), F32),
                   jax.ShapeDtypeStruct((B, S, D // 2), U32),
                   jax.ShapeDtypeStruct((B, S, N_EXPERTS), F32)],
        scratch_shapes=[pltpu.VMEM((GROUP_WIDTH // LANES, tm, LANES), F32) for _ in range(4)],
        compiler_params=pltpu.CompilerParams(
            dimension_semantics=("arbitrary", "arbitrary"), vmem_limit_bytes=VMEM_LIMIT),
        name="merge_norm_router",
    )(*args, *consts)


def _route_kernel(sc_ref, bias_ref, eidx_ref, rank_ref, gate_ref, cnt_ref, carry_ref):
    tm = sc_ref.shape[0]
    ne = N_EXPERTS
    gsize = ne // N_EXPERT_GROUPS
    neg_inf = np.float32(-np.inf)

    @pl.when(pl.program_id(0) == 0)
    def _():
        carry_ref[...] = jnp.zeros_like(carry_ref)

    scores = sc_ref[...]
    biased = scores + bias_ref[...]
    lane = lax.broadcasted_iota(I32, (tm, ne), 1).astype(F32)
    inv_gsize = np.float32(1.0 / gsize)
    grp = jnp.floor(lane * inv_gsize)

    def first_argmax(vals, avail):
        cur = jnp.where(avail, vals, neg_inf)
        m = jnp.max(cur, axis=1, keepdims=True)
        cand = avail & (cur == m)
        return jnp.min(jnp.where(cand, lane, np.float32(ne)), axis=1, keepdims=True)

    gscore = jnp.zeros((tm, ne), F32)
    for g in range(N_EXPERT_GROUPS):
        gm = grp == g
        v = jnp.where(gm, biased, neg_inf)
        m1 = jnp.max(v, axis=1, keepdims=True)
        n1 = jnp.sum(jnp.where(v == m1, 1.0, 0.0), axis=1, keepdims=True)
        m2 = jnp.max(jnp.where(v < m1, v, neg_inf), axis=1, keepdims=True)
        gscore = jnp.where(gm, m1 + jnp.where(n1 >= 2.0, m1, m2), gscore)

    avail = jnp.full((tm, ne), True)
    for _ in range(TOPK_GROUPS):
        idx = first_argmax(gscore, avail)
        avail = avail & (grp != jnp.floor(idx * inv_gsize))
    expert_mask = jnp.logical_not(avail)

    masked = jnp.where(expert_mask, biased, neg_inf)
    avail = jnp.full((tm, ne), True)
    picks = []
    for _ in range(TOP_K):
        idx = first_argmax(masked, avail)
        onehot = lane == idx
        avail = avail & jnp.logical_not(onehot)
        picks.append((idx, onehot))

    sel = jnp.where(avail, 0.0, 1.0)
    r_i = lax.broadcasted_iota(I32, (tm, tm), 0)
    c_i = lax.broadcasted_iota(I32, (tm, tm), 1)
    earlier = jnp.where(r_i > c_i, 1.0, 0.0).astype(BF16)
    ranks = jnp.dot(earlier, sel.astype(BF16), preferred_element_type=F32) + carry_ref[...]
    carry_ref[...] = carry_ref[...] + jnp.sum(sel, axis=0, keepdims=True)
    cnt_ref[...] = carry_ref[...]

    s_k = [jnp.sum(jnp.where(oh, scores, 0.0), axis=1, keepdims=True) for _, oh in picks]
    r_k = [jnp.sum(jnp.where(oh, ranks, 0.0), axis=1, keepdims=True) for _, oh in picks]
    total = s_k[0]
    for s in s_k[1:]:
        total = total + s
    lane_out = lax.broadcasted_iota(I32, (tm, 128), 1)
    e_out = jnp.zeros((tm, 128), I32)
    r_out = jnp.zeros((tm, 128), I32)
    g_out = jnp.zeros((tm, 128), F32)
    for k in range(TOP_K):
        here = lane_out == k
        e_out = jnp.where(here, picks[k][0].astype(I32), e_out)
        r_out = jnp.where(here, r_k[k].astype(I32), r_out)
        g_out = jnp.where(here, s_k[k] / total * np.float32(ROUTED_SCALE), g_out)
    eidx_ref[...] = e_out
    rank_ref[...] = r_out
    gate_ref[...] = g_out


def _route(scores, bias):
    T = scores.shape[0]
    tm = ROUTE_TILE
    out_spec = pl.BlockSpec((tm, 128), lambda i: (i, 0))
    return pl.pallas_call(
        _route_kernel,
        grid=(T // tm,),
        in_specs=[pl.BlockSpec((tm, N_EXPERTS), lambda i: (i, 0)), _const_spec(bias.shape)],
        out_specs=[out_spec, out_spec, out_spec, _const_spec((1, N_EXPERTS))],
        out_shape=[jax.ShapeDtypeStruct((T, 128), I32), jax.ShapeDtypeStruct((T, 128), I32),
                   jax.ShapeDtypeStruct((T, 128), F32), jax.ShapeDtypeStruct((1, N_EXPERTS), F32)],
        scratch_shapes=[pltpu.VMEM((1, N_EXPERTS), F32)],
        compiler_params=pltpu.CompilerParams(dimension_semantics=("arbitrary",)),
        name="route_topk",
    )(scores, bias)


def _sc_workers():
    info = plsc.get_sparse_core_info()
    return info.num_cores, info.num_cores * info.num_subcores


def _sc_scatter_rows(rows, dest, n_out):
    n_chunks, n_dst, ch = dest.shape
    width = rows.shape[1]
    n_cores, n_workers = _sc_workers()
    per_w = n_chunks // n_workers
    assert n_chunks % n_workers == 0 and per_w % 2 == 0 and rows.shape[0] == n_chunks * ch

    def body(rows_hbm, dest_hbm, out_hbm, idx_v, buf, lsem, ssem):
        wid = lax.axis_index("s") * n_cores + lax.axis_index("c")
        c0 = wid * per_w
        pltpu.sync_copy(dest_hbm.at[pl.ds(c0, per_w)], idx_v)

        def load(c, b):
            return pltpu.make_async_copy(rows_hbm.at[pl.ds((c0 + c) * ch, ch)], buf.at[b], lsem.at[b])

        def scatter(c, b, k):
            return pltpu.make_async_copy(buf.at[b], out_hbm.at[idx_v.at[c, k]], ssem.at[b])

        load(0, 0).start()

        @pl.loop(0, per_w, step=2)
        def _(c):
            for b in range(2):
                cc = c + b
                load(cc, b).wait()

                @pl.when(cc >= 1)
                def _():
                    for k in range(n_dst):
                        scatter(cc - 1, 1 - b, k).wait()

                @pl.when(cc + 1 < per_w)
                def _():
                    load(cc + 1, 1 - b).start()

                for k in range(n_dst):
                    scatter(cc, b, k).start()

        for k in range(n_dst):
            scatter(per_w - 1, 1, k).wait()

    mesh = plsc.VectorSubcoreMesh(core_axis_name="c", subcore_axis_name="s")
    return pl.kernel(
        body, out_type=jax.ShapeDtypeStruct((n_out, width), rows.dtype), mesh=mesh,
        scratch_types=[pltpu.VMEM((per_w, n_dst, ch), I32), pltpu.VMEM((2, ch, width), rows.dtype),
                       pltpu.SemaphoreType.DMA((2,)), pltpu.SemaphoreType.DMA((2,))],
        name="moe_dispatch_scatter",
    )(rows, dest)


def _sc_gather_rows(table, idx):
    n_chunks, ch = idx.shape
    width = table.shape[1]
    n_cores, n_workers = _sc_workers()
    per_w = n_chunks // n_workers
    assert n_chunks % n_workers == 0 and per_w % 2 == 0

    def body(table_hbm, idx_hbm, out_hbm, idx_v, buf, gsem, wsem):
        wid = lax.axis_index("s") * n_cores + lax.axis_index("c")
        c0 = wid * per_w
        pltpu.sync_copy(idx_hbm.at[pl.ds(c0, per_w)], idx_v)

        def gather(c, b):
            return pltpu.make_async_copy(table_hbm.at[idx_v.at[c]], buf.at[b], gsem.at[b])

        def write(c, b):
            return pltpu.make_async_copy(buf.at[b], out_hbm.at[pl.ds((c0 + c) * ch, ch)], wsem.at[b])

        gather(0, 0).start()

        @pl.loop(0, per_w, step=2)
        def _(c):
            for b in range(2):
                cc = c + b
                gather(cc, b).wait()

                @pl.when(cc >= 1)
                def _():
                    write(cc - 1, 1 - b).wait()

                @pl.when(cc + 1 < per_w)
                def _():
                    gather(cc + 1, 1 - b).start()

                write(cc, b).start()

        write(per_w - 1, 1).wait()

    mesh = plsc.VectorSubcoreMesh(core_axis_name="c", subcore_axis_name="s")
    return pl.kernel(
        body, out_type=jax.ShapeDtypeStruct((n_chunks * ch, width), table.dtype), mesh=mesh,
        scratch_types=[pltpu.VMEM((per_w, ch), I32), pltpu.VMEM((2, ch, width), table.dtype),
                       pltpu.SemaphoreType.DMA((2,)), pltpu.SemaphoreType.DMA((2,))],
        name="moe_combine_gather",
    )(table, idx)


def _expert_kernel(be_ref, nu_ref, xs_ref, wg_ref, wu_ref, wd_ref, y_ref, wg_s, wu_s, wd_s):
    i = pl.program_id(0)
    half = D_MODEL // 2

    @pl.when(i < nu_ref[0])
    def _():
        prev = be_ref[jnp.maximum(i - 1, 0)]

        @pl.when((i == 0) | (be_ref[i] != prev))
        def _():
            wg_s[...] = wg_ref[0].astype(BF16)
            wu_s[...] = wu_ref[0].astype(BF16)
            wd_s[...] = wd_ref[0].astype(BF16)

        lo, hi = _unpack_bf16_pairs(xs_ref[...])
        xlo = lo.astype(BF16)
        xhi = hi.astype(BF16)
        g = (jnp.dot(xlo, wg_s[:half], preferred_element_type=F32)
             + jnp.dot(xhi, wg_s[half:], preferred_element_type=F32))
        u = (jnp.dot(xlo, wu_s[:half], preferred_element_type=F32)
             + jnp.dot(xhi, wu_s[half:], preferred_element_type=F32))
        hb = (g * jax.nn.sigmoid(g) * u).astype(BF16)
        y = jnp.dot(hb, wd_s[...], preferred_element_type=F32)
        y_ref[...] = _pack_bf16_pairs(y)


def _experts(block_e, n_used, xs, wg, wu, wd):
    n_rows, half = xs.shape
    n_blocks = n_rows // MOE_BLOCK

    def row_map(i, be, nu):
        return (jnp.minimum(i, nu[0] - 1), 0)

    def w_map(i, be, nu):
        return (be[jnp.minimum(i, nu[0] - 1)], 0, 0)

    grid_spec = pltpu.PrefetchScalarGridSpec(
        num_scalar_prefetch=2,
        grid=(n_blocks,),
        in_specs=[pl.BlockSpec((MOE_BLOCK, half), row_map),
                  pl.BlockSpec((1, D_MODEL, EXPERT_DIM), w_map),
                  pl.BlockSpec((1, D_MODEL, EXPERT_DIM), w_map),
                  pl.BlockSpec((1, EXPERT_DIM, D_MODEL), w_map)],
        out_specs=pl.BlockSpec((MOE_BLOCK, half), row_map),
        scratch_shapes=[pltpu.VMEM((D_MODEL, EXPERT_DIM), BF16),
                        pltpu.VMEM((D_MODEL, EXPERT_DIM), BF16),
                        pltpu.VMEM((EXPERT_DIM, D_MODEL), BF16)],
    )
    return pl.pallas_call(
        _expert_kernel,
        grid_spec=grid_spec,
        out_shape=jax.ShapeDtypeStruct((n_rows, half), U32),
        compiler_params=pltpu.CompilerParams(dimension_semantics=("arbitrary",), vmem_limit_bytes=VMEM_LIMIT),
        name="moe_experts",
    )(block_e, n_used, xs, wg, wu, wd)


def _combine_kernel(yg_ref, gate_ref, x1_ref, swgu_ref, swd_ref, g2_ref, b2_ref, out_ref, *, alpha):
    tc = x1_ref.shape[0]
    x1 = x1_ref[...]
    gu = jnp.dot(x1.astype(BF16), swgu_ref[...], preferred_element_type=F32)
    g = gu[:, :EXPERT_DIM]
    u = gu[:, EXPERT_DIM:]
    shared = jnp.dot((g * jax.nn.sigmoid(g) * u).astype(BF16), swd_ref[...], preferred_element_type=F32)
    half = D_MODEL // 2
    acc_lo = jnp.zeros((tc, half), F32)
    acc_hi = jnp.zeros((tc, half), F32)
    for k in range(TOP_K):
        lo, hi = _unpack_bf16_pairs(yg_ref[k])
        gk = gate_ref[:, k:k + 1]
        acc_lo = acc_lo + gk * lo
        acc_hi = acc_hi + gk * hi
    routed = jnp.concatenate([acc_lo, acc_hi], axis=1)
    out_ref[...] = _layer_norm(alpha * x1 + (routed + shared), g2_ref[...], b2_ref[...])


def _combine(yg, gate, x1, swgu, swd, g2, b2, alpha):
    T, D = x1.shape
    tc = COMBINE_TILE
    consts = (swgu, swd, g2, b2)
    return pl.pallas_call(
        functools.partial(_combine_kernel, alpha=alpha),
        grid=(T // tc,),
        in_specs=[pl.BlockSpec((TOP_K, tc, D // 2), lambda i: (0, i, 0)),
                  pl.BlockSpec((tc, 128), lambda i: (i, 0)),
                  pl.BlockSpec((tc, D), lambda i: (i, 0))] + [_const_spec(c.shape) for c in consts],
        out_specs=pl.BlockSpec((tc, D), lambda i: (i, 0)),
        out_shape=jax.ShapeDtypeStruct((T, D), F32),
        compiler_params=pltpu.CompilerParams(dimension_semantics=("arbitrary",), vmem_limit_bytes=VMEM_LIMIT),
        name="moe_combine_norm",
    )(yg, gate, x1, *consts)


def _mixer(x, in_w, in_b, ng, nb, spatial_w, spatial_b, proj_a_w, proj_b_w, out_w, ln1_g, ln1_b,
           router_w, alpha):
    gw, aw, D = GMLP_WIDTH, ATTN_WIDTH, D_MODEL
    w = in_w.astype(BF16)
    q0 = 2 * gw
    wuv, buv = w[:, :q0], in_b[None, :q0]
    watt = jnp.stack([jnp.concatenate([w[:, q0 + s * aw + p * GROUP_WIDTH:q0 + s * aw + (p + 1) * GROUP_WIDTH]
                                       for s in range(3)], axis=1) for p in range(len(DILATED_PATTERNS))])
    batt = jnp.stack([jnp.concatenate([in_b[q0 + s * aw + p * GROUP_WIDTH:q0 + s * aw + (p + 1) * GROUP_WIDTH]
                                       for s in range(3)])[None] for p in range(len(DILATED_PATTERNS))])
    g0 = q0 + 3 * aw
    wg, bg = w[:, g0:], in_b[None, g0:]
    sb = jnp.repeat(spatial_b.T, gw // GMLP_GROUPS, axis=1)
    ya, gb, a1, a4, a16 = _input_projection(
        x, wuv, buv, watt, batt, wg, bg, ng[None], nb[None], spatial_w, sb, proj_a_w.astype(BF16))
    attn_outs = [_dilated_attention(a, p, d) for p, (a, (_, d)) in enumerate(zip((a1, a4, a16), DILATED_PATTERNS))]
    rwh = router_w.astype(BF16)
    rwl = (router_w - rwh.astype(F32)).astype(BF16)
    return _merge_and_norm(attn_outs, ya, gb, x, proj_b_w.astype(BF16), out_w.astype(BF16),
                           ln1_g[None], ln1_b[None], rwh, rwl, alpha)


def _moe(x1, x1p, scores, router_bias, w_gate, w_up, w_down, sw_gate, sw_up, sw_down, ln2_g, ln2_b, alpha):
    T = x1.shape[0]
    A = T * TOP_K
    n_blocks = (A + N_EXPERTS * (MOE_BLOCK - 1)) // MOE_BLOCK
    eidx, rank, gate, counts = _route(scores, router_bias[None])
    counts = counts[0].astype(I32)
    padded = (counts + MOE_BLOCK - 1) // MOE_BLOCK * MOE_BLOCK
    pend = jnp.cumsum(padded).astype(I32)
    pstart = pend - padded
    n_used = (pend[-1:] // MOE_BLOCK).astype(I32)
    block_e = jnp.minimum(
        jnp.searchsorted(pend, jnp.arange(n_blocks, dtype=I32) * MOE_BLOCK, side='right'),
        N_EXPERTS - 1).astype(I32)
    dest = pstart[eidx[:, :TOP_K]] + rank[:, :TOP_K]
    as_i32 = functools.partial(lax.bitcast_convert_type, new_dtype=I32)
    as_u32 = functools.partial(lax.bitcast_convert_type, new_dtype=U32)
    xs = _sc_scatter_rows(as_i32(x1p), dest.reshape(T // SC_CHUNK, SC_CHUNK, TOP_K).transpose(0, 2, 1),
                          n_blocks * MOE_BLOCK)
    y_rows = _experts(block_e, n_used, as_u32(xs), w_gate, w_up, w_down)
    yg = _sc_gather_rows(as_i32(y_rows), dest.T.reshape(A // SC_CHUNK, SC_CHUNK))
    swgu = jnp.concatenate([sw_gate, sw_up], axis=1).astype(BF16)
    return _combine(as_u32(yg).reshape(TOP_K, T, D_MODEL // 2), gate, x1, swgu, sw_down.astype(BF16),
                    ln2_g[None], ln2_b[None], alpha)


def kernel(x, in_w, in_b, gmlp_norm_g, gmlp_norm_b, spatial_w, spatial_b, proj_a_w, proj_b_w, out_w,
           ln1_g, ln1_b, router_w, router_bias, expert_w_gate, expert_w_up, expert_w_down,
           shared_w_gate, shared_w_up, shared_w_down, ln2_g, ln2_b):
    B, S, D = x.shape
    depth = in_w.shape[0]
    alpha = np.float32((2.0 * depth) ** 0.25)
    for l in range(depth):
        x1, x1p, scores = _mixer(x, in_w[l], in_b[l], gmlp_norm_g[l], gmlp_norm_b[l], spatial_w[l],
                                 spatial_b[l], proj_a_w[l], proj_b_w[l], out_w[l], ln1_g[l], ln1_b[l],
                                 router_w[l], alpha)
        out = _moe(x1.reshape(B * S, D), x1p.reshape(B * S, D // 2), scores.reshape(B * S, N_EXPERTS),
                   router_bias[l], expert_w_gate[l], expert_w_up[l], expert_w_down[l],
                   shared_w_gate[l], shared_w_up[l], shared_w_down[l], ln2_g[l], ln2_b[l], alpha)
        x = out.reshape(B, S, D)
    return x
```

```python
import functools
import math

import numpy as np
import jax
import jax.numpy as jnp
from jax import lax
from jax.experimental import pallas as pl
from jax.experimental.pallas import tpu as pltpu
from jax.experimental.pallas import tpu_sc as plsc

F32 = jnp.float32
BF16 = jnp.bfloat16
U32 = jnp.uint32
I32 = jnp.int32

D_MODEL = 1024
GMLP_WIDTH = 1024
GMLP_GROUPS = 8
GMLP_CHUNK = 128
HEAD_DIM = 64
DILATED_PATTERNS = ((128, 1), (512, 4), (2048, 16))
HEADS_PER_GROUP = 4
GROUP_WIDTH = HEADS_PER_GROUP * HEAD_DIM
ATTN_WIDTH = GROUP_WIDTH * len(DILATED_PATTERNS)
ATTN_BLOCK = 128
N_EXPERTS = 256
TOP_K = 8
TOP_K_SHIFT = 3
N_EXPERT_GROUPS = 8
TOPK_GROUPS = 4
EXPERT_DIM = 256
ROUTED_SCALE = 2.5
LN_EPS = 1e-5
LANES = 128
MASKED_SCORE = -1e30

PROJ_TILE = 512
ATTN_QBLOCKS = 2
MERGE_TILE = 512
ROUTE_TILE = 256
MOE_BLOCK = 256
SC_CHUNK = 64
COMBINE_TILE = 128
VMEM_LIMIT = 56 * 1024 * 1024


def _layer_norm(y, g, b):
    mu = jnp.mean(y, axis=-1, keepdims=True)
    yc = y - mu
    var = jnp.mean(yc * yc, axis=-1, keepdims=True)
    return yc * lax.rsqrt(var + LN_EPS) * g + b


def _gelu(x):
    return 0.5 * x * (1.0 + lax.erf(x * np.float32(math.sqrt(0.5))))


def _pack_bf16_pairs(x):
    w = x.shape[1] // 2
    bits = pltpu.bitcast(x.astype(BF16).astype(F32), U32)
    return pltpu.bitcast((bits[:, :w] >> 16) | (bits[:, w:] & jnp.uint32(0xFFFF0000)), I32)


def _unpack_bf16_pairs(words):
    w = pltpu.bitcast(words, U32)
    lo = pltpu.bitcast(w << 16, F32)
    hi = pltpu.bitcast(w & jnp.uint32(0xFFFF0000), F32)
    return lo, hi


def _const_spec(shape):
    nd = len(shape)
    return pl.BlockSpec(shape, lambda *_: (0,) * nd)


def _proj_kernel(x_ref, wuv_ref, buv_ref, watt_ref, batt_ref, wg_ref, bg_ref, ng_ref, nb_ref,
                 sw_ref, sb_ref, pa_ref, ya_ref, gb_ref, a1_ref, a4_ref, a16_ref, xc_ref):
    tm = x_ref.shape[1]
    gw = GMLP_WIDTH
    xb = x_ref[0].astype(BF16)

    def proj(w, b):
        return jnp.dot(xb, w, preferred_element_type=F32) + b

    u = _gelu(proj(wuv_ref[:, :gw], buv_ref[:, :gw]))
    v = _gelu(proj(wuv_ref[:, gw:], buv_ref[:, gw:]))
    v = _layer_norm(v, ng_ref[...], nb_ref[...]).astype(BF16)

    cw = gw // GMLP_GROUPS
    row = lax.broadcasted_iota(I32, (GMLP_CHUNK, GMLP_CHUNK), 0)
    col = lax.broadcasted_iota(I32, (GMLP_CHUNK, GMLP_CHUNK), 1)
    ws = [jnp.where(row >= col, sw_ref[g], 0.0).astype(BF16) for g in range(GMLP_GROUPS)]
    chunks = []
    for c in range(tm // GMLP_CHUNK):
        vc = v[c * GMLP_CHUNK:(c + 1) * GMLP_CHUNK]
        cols = [jnp.dot(ws[g], vc[:, g * cw:(g + 1) * cw], preferred_element_type=F32)
                for g in range(GMLP_GROUPS)]
        chunks.append(jnp.concatenate(cols, axis=1) + sb_ref[...])
    vmix = jnp.concatenate(chunks, axis=0)
    ya = jnp.dot((u * vmix).astype(BF16), pa_ref[...], preferred_element_type=F32)
    ga = jax.nn.sigmoid(proj(wg_ref[:, :D_MODEL], bg_ref[:, :D_MODEL]))
    ya_ref[0] = (ga * ya).astype(BF16)
    gb_ref[0] = jax.nn.sigmoid(proj(wg_ref[:, D_MODEL:], bg_ref[:, D_MODEL:])).astype(BF16)

    n_chunks = x_ref.shape[2] // LANES
    for c in range(n_chunks):
        xc_ref[c] = x_ref[0, :, c * LANES:(c + 1) * LANES]
    for p, (a_ref, (_, d)) in enumerate(zip((a1_ref, a4_ref, a16_ref), DILATED_PATTERNS)):
        n = tm // d
        if d == 1:
            xp = xb
        else:
            xp = jnp.concatenate(
                [jnp.concatenate([xc_ref[c, pl.ds(r, n, stride=d), :] for c in range(n_chunks)], axis=1)
                 for r in range(d)], axis=0).astype(BF16)
        h = (jnp.dot(xp, watt_ref[p], preferred_element_type=F32) + batt_ref[p]).astype(BF16)
        for r in range(d):
            a_ref[0, r] = h[r * n:(r + 1) * n]


def _input_projection(x, wuv, buv, watt, batt, wg, bg, ng, nb, sw, sb, pa):
    B, S, D = x.shape
    tm = PROJ_TILE
    grid = (B, S // tm)
    out_shape = [jax.ShapeDtypeStruct((B, S, D), BF16), jax.ShapeDtypeStruct((B, S, D), BF16)]
    out_specs = [pl.BlockSpec((1, tm, D), lambda b, t: (b, t, 0)),
                 pl.BlockSpec((1, tm, D), lambda b, t: (b, t, 0))]
    for _, d in DILATED_PATTERNS:
        out_shape.append(jax.ShapeDtypeStruct((B, d, S // d, ATTN_WIDTH), BF16))
        out_specs.append(pl.BlockSpec((1, d, tm // d, ATTN_WIDTH), lambda b, t: (b, 0, t, 0)))
    consts = (wuv, buv, watt, batt, wg, bg, ng, nb, sw, sb, pa)
    return pl.pallas_call(
        _proj_kernel,
        grid=grid,
        in_specs=[pl.BlockSpec((1, tm, D), lambda b, t: (b, t, 0))] + [_const_spec(c.shape) for c in consts],
        out_specs=out_specs,
        out_shape=out_shape,
        scratch_shapes=[pltpu.VMEM((D // LANES, tm, LANES), F32)],
        compiler_params=pltpu.CompilerParams(
            dimension_semantics=("arbitrary", "arbitrary"), vmem_limit_bytes=VMEM_LIMIT),
        name="input_projection",
    )(x, *consts)


def _attn_kernel(qkv_ref, bias_ref, bias0_ref, o_ref, lse_ref):
    nq = pl.program_id(2)
    gwid = GROUP_WIDTH
    blk = ATTN_BLOCK
    lane = lax.broadcasted_iota(I32, (1, gwid), 1)
    head_masks = [(lane >= h * HEAD_DIM) & (lane < (h + 1) * HEAD_DIM) for h in range(HEADS_PER_GROUP)]
    q_scales = [jnp.where(m, np.float32(HEAD_DIM ** -0.5), 0.0).astype(BF16) for m in head_masks]
    for j in range(ATTN_QBLOCKS):
        n = nq * ATTN_QBLOCKS + j
        q0 = pl.multiple_of(n * blk, blk)
        p0 = pl.multiple_of(jnp.maximum(n - 1, 0) * blk, blk)
        q = qkv_ref[0, 0, pl.ds(q0, blk), 0:gwid]
        kk = jnp.concatenate([qkv_ref[0, 0, pl.ds(p0, blk), gwid:2 * gwid],
                              qkv_ref[0, 0, pl.ds(q0, blk), gwid:2 * gwid]], axis=0)
        vv = jnp.concatenate([qkv_ref[0, 0, pl.ds(p0, blk), 2 * gwid:3 * gwid],
                              qkv_ref[0, 0, pl.ds(q0, blk), 2 * gwid:3 * gwid]], axis=0)
        qs = jnp.concatenate([q * s for s in q_scales], axis=0)
        s = lax.dot_general(qs, kk, (((1,), (1,)), ((), ())), preferred_element_type=F32)
        s = s + jnp.where(n == 0, bias0_ref[...], bias_ref[...])
        m = jnp.max(s, axis=1, keepdims=True)
        p = jnp.exp(s - m)
        den = jnp.sum(p, axis=1, keepdims=True)
        pv = jnp.dot(p.astype(BF16), vv, preferred_element_type=F32)
        on = pv / den
        lse = m + jnp.log(den)
        o = jnp.zeros((blk, gwid), F32)
        l = jnp.zeros((blk, gwid), F32)
        for h in range(HEADS_PER_GROUP):
            o = jnp.where(head_masks[h], on[h * blk:(h + 1) * blk], o)
            l = jnp.where(head_masks[h], lse[h * blk:(h + 1) * blk], l)
        o_ref[0, 0, j * blk:(j + 1) * blk, :] = o.astype(BF16)
        lse_ref[0, 0, j * blk:(j + 1) * blk, :] = l


def _alibi_slopes(n):
    def pow2_slopes(m):
        start = 2.0 ** (-8.0 / m)
        return [start ** (i + 1) for i in range(m)]
    p = 2 ** int(math.floor(math.log2(n)))
    s = pow2_slopes(p)
    if p < n:
        s = s + pow2_slopes(2 * p)[0::2][: n - p]
    return np.array(sorted(s, reverse=True), dtype=np.float32)


def _attn_bias_tables(group, dilation):
    blk = ATTN_BLOCK
    slopes = _alibi_slopes(HEADS_PER_GROUP * len(DILATED_PATTERNS))
    slopes = slopes[group * HEADS_PER_GROUP:(group + 1) * HEADS_PER_GROUP]
    qi = np.arange(blk)[:, None]
    ki = np.arange(2 * blk)[None, :]
    delta = blk + qi - ki
    band = (delta >= 0) & (delta <= blk)
    bias = -slopes[:, None, None] * (delta * dilation).astype(np.float32)[None]
    full = np.where(band[None], bias, np.float32(MASKED_SCORE)).astype(np.float32)
    first = np.where((ki >= blk)[None], full, np.float32(MASKED_SCORE)).astype(np.float32)
    return full.reshape(HEADS_PER_GROUP * blk, 2 * blk), first.reshape(HEADS_PER_GROUP * blk, 2 * blk)


def _dilated_attention(qkv, group, dilation):
    B, d, sd, _ = qkv.shape
    rows = ATTN_QBLOCKS * ATTN_BLOCK
    bias, bias0 = _attn_bias_tables(group, dilation)
    grid = (B, d, sd // rows)
    out_spec = pl.BlockSpec((1, 1, rows, GROUP_WIDTH), lambda b, r, n: (b, r, n, 0))
    return pl.pallas_call(
        _attn_kernel,
        grid=grid,
        in_specs=[pl.BlockSpec((1, 1, sd, ATTN_WIDTH), lambda b, r, n: (b, r, 0, 0)),
                  _const_spec(bias.shape), _const_spec(bias0.shape)],
        out_specs=[out_spec, out_spec],
        out_shape=[jax.ShapeDtypeStruct((B, d, sd, GROUP_WIDTH), BF16),
                   jax.ShapeDtypeStruct((B, d, sd, GROUP_WIDTH), F32)],
        compiler_params=pltpu.CompilerParams(
            dimension_semantics=("arbitrary", "arbitrary", "arbitrary"), vmem_limit_bytes=VMEM_LIMIT),
        name=f"dilated_attention_d{dilation}",
    )(qkv, jnp.asarray(bias), jnp.asarray(bias0))


def _merge_kernel(o1_ref, l1_ref, o4_ref, l4_ref, o16_ref, l16_ref, ya_ref, gb_ref, x_ref,
                  pb_ref, ow_ref, g1_ref, b1_ref, rwh_ref, rwl_ref,
                  x1_ref, x1p_ref, sc_ref, so4, sl4, so16, sl16, *, alpha):
    tm = x_ref.shape[1]
    n_chunks = GROUP_WIDTH // LANES
    for (o_ref, l_ref, so, sl, d) in ((o4_ref, l4_ref, so4, sl4, 4), (o16_ref, l16_ref, so16, sl16, 16)):
        n = tm // d
        for r in range(d):
            o_r = o_ref[0, r].astype(F32)
            l_r = l_ref[0, r]
            for c in range(n_chunks):
                so[c, pl.ds(r, n, stride=d), :] = o_r[:, c * LANES:(c + 1) * LANES]
                sl[c, pl.ds(r, n, stride=d), :] = l_r[:, c * LANES:(c + 1) * LANES]

    def natural(s):
        return jnp.concatenate([s[c] for c in range(n_chunks)], axis=1)

    l1 = l1_ref[0, 0]
    l4 = natural(sl4)
    l16 = natural(sl16)
    lmax = jnp.maximum(jnp.maximum(l1, l4), l16)
    e1 = jnp.exp(l1 - lmax)
    e4 = jnp.exp(l4 - lmax)
    e16 = jnp.exp(l16 - lmax)
    yb = (e1 * o1_ref[0, 0].astype(F32) + e4 * natural(so4) + e16 * natural(so16)) / (e1 + e4 + e16)
    ybp = jnp.dot(yb.astype(BF16), pb_ref[...], preferred_element_type=F32)
    merged = ya_ref[0].astype(F32) + gb_ref[0].astype(F32) * ybp
    mix = jnp.dot(merged.astype(BF16), ow_ref[...], preferred_element_type=F32)
    x1 = _layer_norm(alpha * x_ref[0] + mix, g1_ref[...], b1_ref[...])
    x1_ref[0] = x1
    x1p_ref[0] = _pack_bf16_pairs(x1)
    hi = x1.astype(BF16)
    lo = (x1 - hi.astype(F32)).astype(BF16)
    logits = (jnp.dot(hi, rwh_ref[...], preferred_element_type=F32)
              + jnp.dot(lo, rwh_ref[...], preferred_element_type=F32)
              + jnp.dot(hi, rwl_ref[...], preferred_element_type=F32))
    sc_ref[0] = jax.nn.sigmoid(logits)


def _merge_and_norm(attn_outs, ya, gb, x, pb, ow, g1, b1, rwh, rwl, alpha):
    B, S, D = x.shape
    tm = MERGE_TILE
    in_specs = []
    args = []
    for (o, l), (_, d) in zip(attn_outs, DILATED_PATTERNS):
        spec = pl.BlockSpec((1, d, tm // d, GROUP_WIDTH), lambda b, t: (b, 0, t, 0))
        in_specs += [spec, spec]
        args += [o, l]
    tok_spec = pl.BlockSpec((1, tm, D), lambda b, t: (b, t, 0))
    in_specs += [tok_spec, tok_spec, tok_spec]
    args += [ya, gb, x]
    consts = (pb, ow, g1, b1, rwh, rwl)
    in_specs += [_const_spec(c.shape) for c in consts]
    return pl.pallas_call(
        functools.partial(_merge_kernel, alpha=alpha),
        grid=(B, S // tm),
        in_specs=in_specs,
        out_specs=[tok_spec,
                   pl.BlockSpec((1, tm, D // 2), lambda b, t: (b, t, 0)),
                   pl.BlockSpec((1, tm, N_EXPERTS), lambda b, t: (b, t, 0))],
        out_shape=[jax.ShapeDtypeStruct((B, S, D), F32),
                   jax.ShapeDtypeStruct((B, S, D // 2), I32),
                   jax.ShapeDtypeStruct((B, S, N_EXPERTS), F32)],
        scratch_shapes=[pltpu.VMEM((GROUP_WIDTH // LANES, tm, LANES), F32) for _ in range(4)],
        compiler_params=pltpu.CompilerParams(
            dimension_semantics=("arbitrary", "arbitrary"), vmem_limit_bytes=VMEM_LIMIT),
        name="merge_norm_router",
    )(*args, *consts)


def _route_kernel(sc_ref, bias_ref, eidx_ref, rank_ref, gate_ref, cnt_ref, carry_ref):
    tm = sc_ref.shape[0]
    ne = N_EXPERTS
    gsize = ne // N_EXPERT_GROUPS
    neg_inf = np.float32(-np.inf)

    @pl.when(pl.program_id(0) == 0)
    def _():
        carry_ref[...] = jnp.zeros_like(carry_ref)

    scores = sc_ref[...]
    biased = scores + bias_ref[...]
    lane = lax.broadcasted_iota(I32, (tm, ne), 1).astype(F32)
    inv_gsize = np.float32(1.0 / gsize)
    grp = jnp.floor(lane * inv_gsize)

    def first_argmax(vals, avail):
        cur = jnp.where(avail, vals, neg_inf)
        m = jnp.max(cur, axis=1, keepdims=True)
        cand = avail & (cur == m)
        return jnp.min(jnp.where(cand, lane, np.float32(ne)), axis=1, keepdims=True)

    gscore = jnp.zeros((tm, ne), F32)
    for g in range(N_EXPERT_GROUPS):
        gm = grp == g
        v = jnp.where(gm, biased, neg_inf)
        m1 = jnp.max(v, axis=1, keepdims=True)
        n1 = jnp.sum(jnp.where(v == m1, 1.0, 0.0), axis=1, keepdims=True)
        m2 = jnp.max(jnp.where(v < m1, v, neg_inf), axis=1, keepdims=True)
        gscore = jnp.where(gm, m1 + jnp.where(n1 >= 2.0, m1, m2), gscore)

    avail = jnp.full((tm, ne), True)
    for _ in range(TOPK_GROUPS):
        idx = first_argmax(gscore, avail)
        avail = avail & (grp != jnp.floor(idx * inv_gsize))
    expert_mask = jnp.logical_not(avail)

    masked = jnp.where(expert_mask, biased, neg_inf)
    avail = jnp.full((tm, ne), True)
    picks = []
    for _ in range(TOP_K):
        idx = first_argmax(masked, avail)
        onehot = lane == idx
        avail = avail & jnp.logical_not(onehot)
        picks.append((idx, onehot))

    sel = jnp.where(avail, 0.0, 1.0)
    r_i = lax.broadcasted_iota(I32, (tm, tm), 0)
    c_i = lax.broadcasted_iota(I32, (tm, tm), 1)
    earlier = jnp.where(r_i > c_i, 1.0, 0.0).astype(BF16)
    ranks = jnp.dot(earlier, sel.astype(BF16), preferred_element_type=F32) + carry_ref[...]
    carry_ref[...] = carry_ref[...] + jnp.sum(sel, axis=0, keepdims=True)
    cnt_ref[...] = carry_ref[...]

    s_k = [jnp.sum(jnp.where(oh, scores, 0.0), axis=1, keepdims=True) for _, oh in picks]
    r_k = [jnp.sum(jnp.where(oh, ranks, 0.0), axis=1, keepdims=True) for _, oh in picks]
    total = s_k[0]
    for s in s_k[1:]:
        total = total + s
    lane_out = lax.broadcasted_iota(I32, (tm, 128), 1)
    e_out = jnp.zeros((tm, 128), I32)
    r_out = jnp.zeros((tm, 128), I32)
    g_out = jnp.zeros((tm, 128), F32)
    for k in range(TOP_K):
        here = lane_out == k
        e_out = jnp.where(here, picks[k][0].astype(I32), e_out)
        r_out = jnp.where(here, r_k[k].astype(I32), r_out)
        g_out = jnp.where(here, s_k[k] / total * np.float32(ROUTED_SCALE), g_out)
    eidx_ref[...] = e_out
    rank_ref[...] = r_out
    gate_ref[...] = g_out


def _route(scores, bias):
    T = scores.shape[0]
    tm = ROUTE_TILE
    out_spec = pl.BlockSpec((tm, 128), lambda i: (i, 0))
    return pl.pallas_call(
        _route_kernel,
        grid=(T // tm,),
        in_specs=[pl.BlockSpec((tm, N_EXPERTS), lambda i: (i, 0)), _const_spec(bias.shape)],
        out_specs=[out_spec, out_spec, out_spec, _const_spec((1, N_EXPERTS))],
        out_shape=[jax.ShapeDtypeStruct((T, 128), I32), jax.ShapeDtypeStruct((T, 128), I32),
                   jax.ShapeDtypeStruct((T, 128), F32), jax.ShapeDtypeStruct((1, N_EXPERTS), F32)],
        scratch_shapes=[pltpu.VMEM((1, N_EXPERTS), F32)],
        compiler_params=pltpu.CompilerParams(dimension_semantics=("arbitrary",)),
        name="route_topk",
    )(scores, bias)


def _dest_kernel(eidx_ref, rank_ref, pstart_ref, out_ref):
    tm = eidx_ref.shape[0]
    eidx = eidx_ref[...]
    lane_e = lax.broadcasted_iota(I32, (tm, N_EXPERTS), 1)
    lane_o = lax.broadcasted_iota(I32, (tm, LANES), 1)
    start = jnp.zeros((tm, LANES), F32)
    for k in range(TOP_K):
        hit = lane_e == eidx[:, k:k + 1]
        s_k = jnp.sum(jnp.where(hit, pstart_ref[...], 0.0), axis=1, keepdims=True)
        start = jnp.where(lane_o == k, s_k, start)
    dest = start + rank_ref[...].astype(F32)
    out_ref[...] = jnp.transpose(dest)[:TOP_K].astype(I32)


def _dest_rows(eidx, rank, pstart):
    T = eidx.shape[0]
    tm = MERGE_TILE
    tok_spec = pl.BlockSpec((tm, LANES), lambda i: (i, 0))
    return pl.pallas_call(
        _dest_kernel,
        grid=(T // tm,),
        in_specs=[tok_spec, tok_spec, _const_spec(pstart.shape)],
        out_specs=pl.BlockSpec((TOP_K, tm), lambda i: (0, i)),
        out_shape=jax.ShapeDtypeStruct((TOP_K, T), I32),
        compiler_params=pltpu.CompilerParams(dimension_semantics=("arbitrary",)),
        name="moe_dest_rows",
    )(eidx, rank, pstart)


def _sc_workers():
    info = plsc.get_sparse_core_info()
    return info.num_cores, info.num_cores * info.num_subcores


def _sc_scatter_rows(rows, dest, n_out):
    n_chunks, n_dst, ch = dest.shape
    width = rows.shape[1]
    n_cores, n_workers = _sc_workers()
    per_w = n_chunks // n_workers
    assert n_chunks % n_workers == 0 and per_w % 2 == 0 and rows.shape[0] == n_chunks * ch

    def body(rows_hbm, dest_hbm, out_hbm, idx_v, buf, lsem, ssem):
        wid = lax.axis_index("s") * n_cores + lax.axis_index("c")
        c0 = wid * per_w
        pltpu.sync_copy(dest_hbm.at[pl.ds(c0, per_w)], idx_v)

        def load(c, b):
            return pltpu.make_async_copy(rows_hbm.at[pl.ds((c0 + c) * ch, ch)], buf.at[b], lsem.at[b])

        def scatter(c, b, k):
            return pltpu.make_async_copy(buf.at[b], out_hbm.at[idx_v.at[c, k]], ssem.at[b])

        load(0, 0).start()

        @pl.loop(0, per_w, step=2)
        def _(c):
            for b in range(2):
                cc = c + b
                load(cc, b).wait()

                @pl.when(cc >= 1)
                def _():
                    for k in range(n_dst):
                        scatter(cc - 1, 1 - b, k).wait()

                @pl.when(cc + 1 < per_w)
                def _():
                    load(cc + 1, 1 - b).start()

                for k in range(n_dst):
                    scatter(cc, b, k).start()

        for k in range(n_dst):
            scatter(per_w - 1, 1, k).wait()

    mesh = plsc.VectorSubcoreMesh(core_axis_name="c", subcore_axis_name="s")
    return pl.kernel(
        body, out_type=jax.ShapeDtypeStruct((n_out, width), rows.dtype), mesh=mesh,
        scratch_types=[pltpu.VMEM((per_w, n_dst, ch), I32), pltpu.VMEM((2, ch, width), rows.dtype),
                       pltpu.SemaphoreType.DMA((2,)), pltpu.SemaphoreType.DMA((2,))],
        name="moe_dispatch_scatter",
    )(rows, dest)


def _sc_gather_rows(table, idx):
    n_chunks, ch = idx.shape
    width = table.shape[1]
    n_cores, n_workers = _sc_workers()
    per_w = n_chunks // n_workers
    assert n_chunks % n_workers == 0 and per_w % 2 == 0

    def body(table_hbm, idx_hbm, out_hbm, idx_v, buf, gsem, wsem):
        wid = lax.axis_index("s") * n_cores + lax.axis_index("c")
        c0 = wid * per_w
        pltpu.sync_copy(idx_hbm.at[pl.ds(c0, per_w)], idx_v)

        def gather(c, b):
            return pltpu.make_async_copy(table_hbm.at[idx_v.at[c]], buf.at[b], gsem.at[b])

        def write(c, b):
            return pltpu.make_async_copy(buf.at[b], out_hbm.at[pl.ds((c0 + c) * ch, ch)], wsem.at[b])

        gather(0, 0).start()

        @pl.loop(0, per_w, step=2)
        def _(c):
            for b in range(2):
                cc = c + b
                gather(cc, b).wait()

                @pl.when(cc >= 1)
                def _():
                    write(cc - 1, 1 - b).wait()

                @pl.when(cc + 1 < per_w)
                def _():
                    gather(cc + 1, 1 - b).start()

                write(cc, b).start()

        write(per_w - 1, 1).wait()

    mesh = plsc.VectorSubcoreMesh(core_axis_name="c", subcore_axis_name="s")
    return pl.kernel(
        body, out_type=jax.ShapeDtypeStruct((n_chunks * ch, width), table.dtype), mesh=mesh,
        scratch_types=[pltpu.VMEM((per_w, ch), I32), pltpu.VMEM((2, ch, width), table.dtype),
                       pltpu.SemaphoreType.DMA((2,)), pltpu.SemaphoreType.DMA((2,))],
        name="moe_combine_gather",
    )(table, idx)


def _expert_kernel(be_ref, nu_ref, xs_ref, wg_ref, wu_ref, wd_ref, y_ref, wg_s, wu_s, wd_s):
    i = pl.program_id(0)
    half = D_MODEL // 2

    @pl.when(i < nu_ref[0])
    def _():
        prev = be_ref[jnp.maximum(i - 1, 0)]

        @pl.when((i == 0) | (be_ref[i] != prev))
        def _():
            wg_s[...] = wg_ref[0].astype(BF16)
            wu_s[...] = wu_ref[0].astype(BF16)
            wd_s[...] = wd_ref[0].astype(BF16)

        lo, hi = _unpack_bf16_pairs(xs_ref[...])
        xlo = lo.astype(BF16)
        xhi = hi.astype(BF16)
        g = (jnp.dot(xlo, wg_s[:half], preferred_element_type=F32)
             + jnp.dot(xhi, wg_s[half:], preferred_element_type=F32))
        u = (jnp.dot(xlo, wu_s[:half], preferred_element_type=F32)
             + jnp.dot(xhi, wu_s[half:], preferred_element_type=F32))
        hb = (g * jax.nn.sigmoid(g) * u).astype(BF16)
        y = jnp.dot(hb, wd_s[...], preferred_element_type=F32)
        y_ref[...] = _pack_bf16_pairs(y)


def _experts(block_e, n_used, xs, wg, wu, wd):
    n_rows, half = xs.shape
    n_blocks = n_rows // MOE_BLOCK

    def row_map(i, be, nu):
        return (jnp.minimum(i, nu[0] - 1), 0)

    def w_map(i, be, nu):
        return (be[jnp.minimum(i, nu[0] - 1)], 0, 0)

    grid_spec = pltpu.PrefetchScalarGridSpec(
        num_scalar_prefetch=2,
        grid=(n_blocks,),
        in_specs=[pl.BlockSpec((MOE_BLOCK, half), row_map),
                  pl.BlockSpec((1, D_MODEL, EXPERT_DIM), w_map),
                  pl.BlockSpec((1, D_MODEL, EXPERT_DIM), w_map),
                  pl.BlockSpec((1, EXPERT_DIM, D_MODEL), w_map)],
        out_specs=pl.BlockSpec((MOE_BLOCK, half), row_map),
        scratch_shapes=[pltpu.VMEM((D_MODEL, EXPERT_DIM), BF16),
                        pltpu.VMEM((D_MODEL, EXPERT_DIM), BF16),
                        pltpu.VMEM((EXPERT_DIM, D_MODEL), BF16)],
    )
    return pl.pallas_call(
        _expert_kernel,
        grid_spec=grid_spec,
        out_shape=jax.ShapeDtypeStruct((n_rows, half), I32),
        compiler_params=pltpu.CompilerParams(dimension_semantics=("arbitrary",), vmem_limit_bytes=VMEM_LIMIT),
        name="moe_experts",
    )(block_e, n_used, xs, wg, wu, wd)


def _combine_kernel(yg_ref, gate_ref, x1_ref, swgu_ref, swd_ref, g2_ref, b2_ref, out_ref, *, alpha):
    tc = x1_ref.shape[0]
    x1 = x1_ref[...]
    gu = jnp.dot(x1.astype(BF16), swgu_ref[...], preferred_element_type=F32)
    g = gu[:, :EXPERT_DIM]
    u = gu[:, EXPERT_DIM:]
    shared = jnp.dot((g * jax.nn.sigmoid(g) * u).astype(BF16), swd_ref[...], preferred_element_type=F32)
    half = D_MODEL // 2
    acc_lo = jnp.zeros((tc, half), F32)
    acc_hi = jnp.zeros((tc, half), F32)
    for k in range(TOP_K):
        lo, hi = _unpack_bf16_pairs(yg_ref[k])
        gk = gate_ref[:, k:k + 1]
        acc_lo = acc_lo + gk * lo
        acc_hi = acc_hi + gk * hi
    routed = jnp.concatenate([acc_lo, acc_hi], axis=1)
    out_ref[...] = _layer_norm(alpha * x1 + (routed + shared), g2_ref[...], b2_ref[...])


def _combine(yg, gate, x1, swgu, swd, g2, b2, alpha):
    T, D = x1.shape
    tc = COMBINE_TILE
    consts = (swgu, swd, g2, b2)
    return pl.pallas_call(
        functools.partial(_combine_kernel, alpha=alpha),
        grid=(T // tc,),
        in_specs=[pl.BlockSpec((TOP_K, tc, D // 2), lambda i: (0, i, 0)),
                  pl.BlockSpec((tc, 128), lambda i: (i, 0)),
                  pl.BlockSpec((tc, D), lambda i: (i, 0))] + [_const_spec(c.shape) for c in consts],
        out_specs=pl.BlockSpec((tc, D), lambda i: (i, 0)),
        out_shape=jax.ShapeDtypeStruct((T, D), F32),
        compiler_params=pltpu.CompilerParams(dimension_semantics=("arbitrary",), vmem_limit_bytes=VMEM_LIMIT),
        name="moe_combine_norm",
    )(yg, gate, x1, *consts)


def _mixer(x, in_w, in_b, ng, nb, spatial_w, spatial_b, proj_a_w, proj_b_w, out_w, ln1_g, ln1_b,
           router_w, alpha):
    gw, aw, D = GMLP_WIDTH, ATTN_WIDTH, D_MODEL
    w = in_w.astype(BF16)
    q0 = 2 * gw
    wuv, buv = w[:, :q0], in_b[None, :q0]
    watt = jnp.stack([jnp.concatenate([w[:, q0 + s * aw + p * GROUP_WIDTH:q0 + s * aw + (p + 1) * GROUP_WIDTH]
                                       for s in range(3)], axis=1) for p in range(len(DILATED_PATTERNS))])
    batt = jnp.stack([jnp.concatenate([in_b[q0 + s * aw + p * GROUP_WIDTH:q0 + s * aw + (p + 1) * GROUP_WIDTH]
                                       for s in range(3)])[None] for p in range(len(DILATED_PATTERNS))])
    g0 = q0 + 3 * aw
    wg, bg = w[:, g0:], in_b[None, g0:]
    sb = jnp.repeat(spatial_b.T, gw // GMLP_GROUPS, axis=1)
    ya, gb, a1, a4, a16 = _input_projection(
        x, wuv, buv, watt, batt, wg, bg, ng[None], nb[None], spatial_w, sb, proj_a_w.astype(BF16))
    attn_outs = [_dilated_attention(a, p, d) for p, (a, (_, d)) in enumerate(zip((a1, a4, a16), DILATED_PATTERNS))]
    rwh = router_w.astype(BF16)
    rwl = (router_w - rwh.astype(F32)).astype(BF16)
    return _merge_and_norm(attn_outs, ya, gb, x, proj_b_w.astype(BF16), out_w.astype(BF16),
                           ln1_g[None], ln1_b[None], rwh, rwl, alpha)


def _moe(x1, x1p, scores, router_bias, w_gate, w_up, w_down, sw_gate, sw_up, sw_down, ln2_g, ln2_b, alpha):
    T = x1.shape[0]
    A = T * TOP_K
    n_blocks = (A + N_EXPERTS * (MOE_BLOCK - 1)) // MOE_BLOCK
    eidx, rank, gate, counts = _route(scores, router_bias[None])
    counts = counts[0].astype(I32)
    padded = (counts + MOE_BLOCK - 1) // MOE_BLOCK * MOE_BLOCK
    pend = jnp.cumsum(padded).astype(I32)
    pstart = pend - padded
    n_used = (pend[-1:] // MOE_BLOCK).astype(I32)
    block_starts = jnp.arange(n_blocks, dtype=I32) * MOE_BLOCK
    block_e = jnp.minimum(jnp.sum((pend[None, :] <= block_starts[:, None]).astype(I32), axis=1), N_EXPERTS - 1)
    dest_t = _dest_rows(eidx, rank, pstart.astype(F32)[None])
    xs = _sc_scatter_rows(x1p, dest_t.reshape(TOP_K, T // SC_CHUNK, SC_CHUNK).transpose(1, 0, 2),
                          n_blocks * MOE_BLOCK)
    y_rows = _experts(block_e, n_used, xs, w_gate, w_up, w_down)
    yg = _sc_gather_rows(y_rows, dest_t.reshape(A // SC_CHUNK, SC_CHUNK))
    swgu = jnp.concatenate([sw_gate, sw_up], axis=1).astype(BF16)
    return _combine(yg.reshape(TOP_K, T, D_MODEL // 2), gate, x1, swgu, sw_down.astype(BF16),
                    ln2_g[None], ln2_b[None], alpha)


def kernel(x, in_w, in_b, gmlp_norm_g, gmlp_norm_b, spatial_w, spatial_b, proj_a_w, proj_b_w, out_w,
           ln1_g, ln1_b, router_w, router_bias, expert_w_gate, expert_w_up, expert_w_down,
           shared_w_gate, shared_w_up, shared_w_down, ln2_g, ln2_b):
    B, S, D = x.shape
    depth = in_w.shape[0]
    alpha = np.float32((2.0 * depth) ** 0.25)
    for l in range(depth):
        x1, x1p, scores = _mixer(x, in_w[l], in_b[l], gmlp_norm_g[l], gmlp_norm_b[l], spatial_w[l],
                                 spatial_b[l], proj_a_w[l], proj_b_w[l], out_w[l], ln1_g[l], ln1_b[l],
                                 router_w[l], alpha)
        out = _moe(x1.reshape(B * S, D), x1p.reshape(B * S, D // 2), scores.reshape(B * S, N_EXPERTS),
                   router_bias[l], expert_w_gate[l], expert_w_up[l], expert_w_down[l],
                   shared_w_gate[l], shared_w_up[l], shared_w_down[l], ln2_g[l], ln2_b[l], alpha)
        x = out.reshape(B, S, D)
    return x
```

```python
import functools
import math

import numpy as np
import jax
import jax.numpy as jnp
from jax import lax
from jax.experimental import pallas as pl
from jax.experimental.pallas import tpu as pltpu
from jax.experimental.pallas import tpu_sc as plsc

F32 = jnp.float32
BF16 = jnp.bfloat16
U32 = jnp.uint32
I32 = jnp.int32

D_MODEL = 1024
GMLP_WIDTH = 1024
GMLP_GROUPS = 8
GMLP_CHUNK = 128
HEAD_DIM = 64
DILATED_PATTERNS = ((128, 1), (512, 4), (2048, 16))
HEADS_PER_GROUP = 4
GROUP_WIDTH = HEADS_PER_GROUP * HEAD_DIM
ATTN_WIDTH = GROUP_WIDTH * len(DILATED_PATTERNS)
ATTN_BLOCK = 128
N_EXPERTS = 256
TOP_K = 8
TOP_K_SHIFT = 3
N_EXPERT_GROUPS = 8
TOPK_GROUPS = 4
EXPERT_DIM = 256
ROUTED_SCALE = 2.5
LN_EPS = 1e-5
LANES = 128
MASKED_SCORE = -1e30

PROJ_TILE = 512
ATTN_QBLOCKS = 2
MERGE_TILE = 512
ROUTE_TILE = 256
MOE_BLOCK = 256
SC_CHUNK = 64
COMBINE_TILE = 128
VMEM_LIMIT = 56 * 1024 * 1024


def _layer_norm(y, g, b):
    mu = jnp.mean(y, axis=-1, keepdims=True)
    yc = y - mu
    var = jnp.mean(yc * yc, axis=-1, keepdims=True)
    return yc * lax.rsqrt(var + LN_EPS) * g + b


def _gelu(x):
    return 0.5 * x * (1.0 + lax.erf(x * np.float32(math.sqrt(0.5))))


def _pack_bf16_pairs(x):
    w = x.shape[1] // 2
    bits = pltpu.bitcast(x.astype(BF16).astype(F32), U32)
    return pltpu.bitcast((bits[:, :w] >> 16) | (bits[:, w:] & jnp.uint32(0xFFFF0000)), I32)


def _unpack_bf16_pairs(words):
    w = pltpu.bitcast(words, U32)
    lo = pltpu.bitcast(w << 16, F32)
    hi = pltpu.bitcast(w & jnp.uint32(0xFFFF0000), F32)
    return lo, hi


def _const_spec(shape):
    nd = len(shape)
    return pl.BlockSpec(shape, lambda *_: (0,) * nd)


def _proj_kernel(x_ref, wuv_ref, buv_ref, watt_ref, batt_ref, wg_ref, bg_ref, ng_ref, nb_ref,
                 sw_ref, sb_ref, pa_ref, ya_ref, gb_ref, a1_ref, a4_ref, a16_ref, xc_ref):
    tm = x_ref.shape[1]
    gw = GMLP_WIDTH
    xb = x_ref[0].astype(BF16)

    def proj(w, b):
        return jnp.dot(xb, w, preferred_element_type=F32) + b

    u = _gelu(proj(wuv_ref[:, :gw], buv_ref[:, :gw]))
    v = _gelu(proj(wuv_ref[:, gw:], buv_ref[:, gw:]))
    v = _layer_norm(v, ng_ref[...], nb_ref[...]).astype(BF16)

    cw = gw // GMLP_GROUPS
    row = lax.broadcasted_iota(I32, (GMLP_CHUNK, GMLP_CHUNK), 0)
    col = lax.broadcasted_iota(I32, (GMLP_CHUNK, GMLP_CHUNK), 1)
    ws = [jnp.where(row >= col, sw_ref[g], 0.0).astype(BF16) for g in range(GMLP_GROUPS)]
    chunks = []
    for c in range(tm // GMLP_CHUNK):
        vc = v[c * GMLP_CHUNK:(c + 1) * GMLP_CHUNK]
        cols = [jnp.dot(ws[g], vc[:, g * cw:(g + 1) * cw], preferred_element_type=F32)
                for g in range(GMLP_GROUPS)]
        chunks.append(jnp.concatenate(cols, axis=1) + sb_ref[...])
    vmix = jnp.concatenate(chunks, axis=0)
    ya = jnp.dot((u * vmix).astype(BF16), pa_ref[...], preferred_element_type=F32)
    ga = jax.nn.sigmoid(proj(wg_ref[:, :D_MODEL], bg_ref[:, :D_MODEL]))
    ya_ref[0] = (ga * ya).astype(BF16)
    gb_ref[0] = jax.nn.sigmoid(proj(wg_ref[:, D_MODEL:], bg_ref[:, D_MODEL:])).astype(BF16)

    n_chunks = x_ref.shape[2] // LANES
    for c in range(n_chunks):
        xc_ref[c] = x_ref[0, :, c * LANES:(c + 1) * LANES]
    for p, (a_ref, (_, d)) in enumerate(zip((a1_ref, a4_ref, a16_ref), DILATED_PATTERNS)):
        n = tm // d
        if d == 1:
            xp = xb
        else:
            xp = jnp.concatenate(
                [jnp.concatenate([xc_ref[c, pl.ds(r, n, stride=d), :] for c in range(n_chunks)], axis=1)
                 for r in range(d)], axis=0).astype(BF16)
        h = (jnp.dot(xp, watt_ref[p], preferred_element_type=F32) + batt_ref[p]).astype(BF16)
        for r in range(d):
            a_ref[0, r] = h[r * n:(r + 1) * n]


def _input_projection(x, wuv, buv, watt, batt, wg, bg, ng, nb, sw, sb, pa):
    B, S, D = x.shape
    tm = PROJ_TILE
    grid = (B, S // tm)
    out_shape = [jax.ShapeDtypeStruct((B, S, D), BF16), jax.ShapeDtypeStruct((B, S, D), BF16)]
    out_specs = [pl.BlockSpec((1, tm, D), lambda b, t: (b, t, 0)),
                 pl.BlockSpec((1, tm, D), lambda b, t: (b, t, 0))]
    for _, d in DILATED_PATTERNS:
        out_shape.append(jax.ShapeDtypeStruct((B, d, S // d, ATTN_WIDTH), BF16))
        out_specs.append(pl.BlockSpec((1, d, tm // d, ATTN_WIDTH), lambda b, t: (b, 0, t, 0)))
    consts = (wuv, buv, watt, batt, wg, bg, ng, nb, sw, sb, pa)
    return pl.pallas_call(
        _proj_kernel,
        grid=grid,
        in_specs=[pl.BlockSpec((1, tm, D), lambda b, t: (b, t, 0))] + [_const_spec(c.shape) for c in consts],
        out_specs=out_specs,
        out_shape=out_shape,
        scratch_shapes=[pltpu.VMEM((D // LANES, tm, LANES), F32)],
        compiler_params=pltpu.CompilerParams(
            dimension_semantics=("arbitrary", "arbitrary"), vmem_limit_bytes=VMEM_LIMIT),
        name="input_projection",
    )(x, *consts)


def _attn_kernel(qkv_ref, bias_ref, bias0_ref, o_ref, lse_ref):
    nq = pl.program_id(2)
    gwid = GROUP_WIDTH
    blk = ATTN_BLOCK
    lane = lax.broadcasted_iota(I32, (1, gwid), 1)
    head_masks = [(lane >= h * HEAD_DIM) & (lane < (h + 1) * HEAD_DIM) for h in range(HEADS_PER_GROUP)]
    q_scales = [jnp.where(m, np.float32(HEAD_DIM ** -0.5), 0.0).astype(BF16) for m in head_masks]
    for j in range(ATTN_QBLOCKS):
        n = nq * ATTN_QBLOCKS + j
        q0 = pl.multiple_of(n * blk, blk)
        p0 = pl.multiple_of(jnp.maximum(n - 1, 0) * blk, blk)
        q = qkv_ref[0, 0, pl.ds(q0, blk), 0:gwid]
        kk = jnp.concatenate([qkv_ref[0, 0, pl.ds(p0, blk), gwid:2 * gwid],
                              qkv_ref[0, 0, pl.ds(q0, blk), gwid:2 * gwid]], axis=0)
        vv = jnp.concatenate([qkv_ref[0, 0, pl.ds(p0, blk), 2 * gwid:3 * gwid],
                              qkv_ref[0, 0, pl.ds(q0, blk), 2 * gwid:3 * gwid]], axis=0)
        qs = jnp.concatenate([q * s for s in q_scales], axis=0)
        s = lax.dot_general(qs, kk, (((1,), (1,)), ((), ())), preferred_element_type=F32)
        s = s + jnp.where(n == 0, bias0_ref[...], bias_ref[...])
        m = jnp.max(s, axis=1, keepdims=True)
        p = jnp.exp(s - m)
        den = jnp.sum(p, axis=1, keepdims=True)
        pv = jnp.dot(p.astype(BF16), vv, preferred_element_type=F32)
        on = pv / den
        lse = m + jnp.log(den)
        o = jnp.zeros((blk, gwid), F32)
        l = jnp.zeros((blk, gwid), F32)
        for h in range(HEADS_PER_GROUP):
            o = jnp.where(head_masks[h], on[h * blk:(h + 1) * blk], o)
            l = jnp.where(head_masks[h], lse[h * blk:(h + 1) * blk], l)
        o_ref[0, 0, j * blk:(j + 1) * blk, :] = o.astype(BF16)
        lse_ref[0, 0, j * blk:(j + 1) * blk, :] = l


def _alibi_slopes(n):
    def pow2_slopes(m):
        start = 2.0 ** (-8.0 / m)
        return [start ** (i + 1) for i in range(m)]
    p = 2 ** int(math.floor(math.log2(n)))
    s = pow2_slopes(p)
    if p < n:
        s = s + pow2_slopes(2 * p)[0::2][: n - p]
    return np.array(sorted(s, reverse=True), dtype=np.float32)


def _attn_bias_tables(group, dilation):
    blk = ATTN_BLOCK
    slopes = _alibi_slopes(HEADS_PER_GROUP * len(DILATED_PATTERNS))
    slopes = slopes[group * HEADS_PER_GROUP:(group + 1) * HEADS_PER_GROUP]
    qi = np.arange(blk)[:, None]
    ki = np.arange(2 * blk)[None, :]
    delta = blk + qi - ki
    band = (delta >= 0) & (delta <= blk)
    bias = -slopes[:, None, None] * (delta * dilation).astype(np.float32)[None]
    full = np.where(band[None], bias, np.float32(MASKED_SCORE)).astype(np.float32)
    first = np.where((ki >= blk)[None], full, np.float32(MASKED_SCORE)).astype(np.float32)
    return full.reshape(HEADS_PER_GROUP * blk, 2 * blk), first.reshape(HEADS_PER_GROUP * blk, 2 * blk)


def _dilated_attention(qkv, group, dilation):
    B, d, sd, _ = qkv.shape
    rows = ATTN_QBLOCKS * ATTN_BLOCK
    bias, bias0 = _attn_bias_tables(group, dilation)
    grid = (B, d, sd // rows)
    out_spec = pl.BlockSpec((1, 1, rows, GROUP_WIDTH), lambda b, r, n: (b, r, n, 0))
    return pl.pallas_call(
        _attn_kernel,
        grid=grid,
        in_specs=[pl.BlockSpec((1, 1, sd, ATTN_WIDTH), lambda b, r, n: (b, r, 0, 0)),
                  _const_spec(bias.shape), _const_spec(bias0.shape)],
        out_specs=[out_spec, out_spec],
        out_shape=[jax.ShapeDtypeStruct((B, d, sd, GROUP_WIDTH), BF16),
                   jax.ShapeDtypeStruct((B, d, sd, GROUP_WIDTH), F32)],
        compiler_params=pltpu.CompilerParams(
            dimension_semantics=("arbitrary", "arbitrary", "arbitrary"), vmem_limit_bytes=VMEM_LIMIT),
        name=f"dilated_attention_d{dilation}",
    )(qkv, jnp.asarray(bias), jnp.asarray(bias0))


def _merge_kernel(o1_ref, l1_ref, o4_ref, l4_ref, o16_ref, l16_ref, ya_ref, gb_ref, x_ref,
                  pb_ref, ow_ref, g1_ref, b1_ref, rwh_ref, rwl_ref,
                  x1_ref, x1p_ref, sc_ref, so4, sl4, so16, sl16, *, alpha):
    tm = x_ref.shape[1]
    n_chunks = GROUP_WIDTH // LANES
    for (o_ref, l_ref, so, sl, d) in ((o4_ref, l4_ref, so4, sl4, 4), (o16_ref, l16_ref, so16, sl16, 16)):
        n = tm // d
        for r in range(d):
            o_r = o_ref[0, r].astype(F32)
            l_r = l_ref[0, r]
            for c in range(n_chunks):
                so[c, pl.ds(r, n, stride=d), :] = o_r[:, c * LANES:(c + 1) * LANES]
                sl[c, pl.ds(r, n, stride=d), :] = l_r[:, c * LANES:(c + 1) * LANES]

    def natural(s):
        return jnp.concatenate([s[c] for c in range(n_chunks)], axis=1)

    l1 = l1_ref[0, 0]
    l4 = natural(sl4)
    l16 = natural(sl16)
    lmax = jnp.maximum(jnp.maximum(l1, l4), l16)
    e1 = jnp.exp(l1 - lmax)
    e4 = jnp.exp(l4 - lmax)
    e16 = jnp.exp(l16 - lmax)
    yb = (e1 * o1_ref[0, 0].astype(F32) + e4 * natural(so4) + e16 * natural(so16)) / (e1 + e4 + e16)
    ybp = jnp.dot(yb.astype(BF16), pb_ref[...], preferred_element_type=F32)
    merged = ya_ref[0].astype(F32) + gb_ref[0].astype(F32) * ybp
    mix = jnp.dot(merged.astype(BF16), ow_ref[...], preferred_element_type=F32)
    x1 = _layer_norm(alpha * x_ref[0] + mix, g1_ref[...], b1_ref[...])
    x1_ref[0] = x1
    x1p_ref[0] = _pack_bf16_pairs(x1)
    hi = x1.astype(BF16)
    lo = (x1 - hi.astype(F32)).astype(BF16)
    logits = (jnp.dot(hi, rwh_ref[...], preferred_element_type=F32)
              + jnp.dot(lo, rwh_ref[...], preferred_element_type=F32)
              + jnp.dot(hi, rwl_ref[...], preferred_element_type=F32))
    sc_ref[0] = jax.nn.sigmoid(logits)


def _merge_and_norm(attn_outs, ya, gb, x, pb, ow, g1, b1, rwh, rwl, alpha):
    B, S, D = x.shape
    tm = MERGE_TILE
    in_specs = []
    args = []
    for (o, l), (_, d) in zip(attn_outs, DILATED_PATTERNS):
        spec = pl.BlockSpec((1, d, tm // d, GROUP_WIDTH), lambda b, t: (b, 0, t, 0))
        in_specs += [spec, spec]
        args += [o, l]
    tok_spec = pl.BlockSpec((1, tm, D), lambda b, t: (b, t, 0))
    in_specs += [tok_spec, tok_spec, tok_spec]
    args += [ya, gb, x]
    consts = (pb, ow, g1, b1, rwh, rwl)
    in_specs += [_const_spec(c.shape) for c in consts]
    return pl.pallas_call(
        functools.partial(_merge_kernel, alpha=alpha),
        grid=(B, S // tm),
        in_specs=in_specs,
        out_specs=[tok_spec,
                   pl.BlockSpec((1, tm, D // 2), lambda b, t: (b, t, 0)),
                   pl.BlockSpec((1, tm, N_EXPERTS), lambda b, t: (b, t, 0))],
        out_shape=[jax.ShapeDtypeStruct((B, S, D), F32),
                   jax.ShapeDtypeStruct((B, S, D // 2), I32),
                   jax.ShapeDtypeStruct((B, S, N_EXPERTS), F32)],
        scratch_shapes=[pltpu.VMEM((GROUP_WIDTH // LANES, tm, LANES), F32) for _ in range(4)],
        compiler_params=pltpu.CompilerParams(
            dimension_semantics=("arbitrary", "arbitrary"), vmem_limit_bytes=VMEM_LIMIT),
        name="merge_norm_router",
    )(*args, *consts)


def _route_kernel(sc_ref, bias_ref, eidx_ref, rank_ref, gate_ref, cnt_ref, carry_ref):
    tm = sc_ref.shape[0]
    ne = N_EXPERTS
    gsize = ne // N_EXPERT_GROUPS
    neg_inf = np.float32(-np.inf)

    @pl.when(pl.program_id(0) == 0)
    def _():
        carry_ref[...] = jnp.zeros_like(carry_ref)

    scores = sc_ref[...]
    biased = scores + bias_ref[...]
    lane = lax.broadcasted_iota(I32, (tm, ne), 1).astype(F32)
    inv_gsize = np.float32(1.0 / gsize)
    grp = jnp.floor(lane * inv_gsize)

    def first_argmax(vals, avail):
        cur = jnp.where(avail, vals, neg_inf)
        m = jnp.max(cur, axis=1, keepdims=True)
        cand = avail & (cur == m)
        return jnp.min(jnp.where(cand, lane, np.float32(ne)), axis=1, keepdims=True)

    gscore = jnp.zeros((tm, ne), F32)
    for g in range(N_EXPERT_GROUPS):
        gm = grp == g
        v = jnp.where(gm, biased, neg_inf)
        m1 = jnp.max(v, axis=1, keepdims=True)
        n1 = jnp.sum(jnp.where(v == m1, 1.0, 0.0), axis=1, keepdims=True)
        m2 = jnp.max(jnp.where(v < m1, v, neg_inf), axis=1, keepdims=True)
        gscore = jnp.where(gm, m1 + jnp.where(n1 >= 2.0, m1, m2), gscore)

    avail = jnp.full((tm, ne), True)
    for _ in range(TOPK_GROUPS):
        idx = first_argmax(gscore, avail)
        avail = avail & (grp != jnp.floor(idx * inv_gsize))
    expert_mask = jnp.logical_not(avail)

    masked = jnp.where(expert_mask, biased, neg_inf)
    avail = jnp.full((tm, ne), True)
    picks = []
    for _ in range(TOP_K):
        idx = first_argmax(masked, avail)
        onehot = lane == idx
        avail = avail & jnp.logical_not(onehot)
        picks.append((idx, onehot))

    sel = jnp.where(avail, 0.0, 1.0)
    r_i = lax.broadcasted_iota(I32, (tm, tm), 0)
    c_i = lax.broadcasted_iota(I32, (tm, tm), 1)
    earlier = jnp.where(r_i > c_i, 1.0, 0.0).astype(BF16)
    ranks = jnp.dot(earlier, sel.astype(BF16), preferred_element_type=F32) + carry_ref[...]
    carry_ref[...] = carry_ref[...] + jnp.sum(sel, axis=0, keepdims=True)
    cnt_ref[...] = carry_ref[...]

    s_k = [jnp.sum(jnp.where(oh, scores, 0.0), axis=1, keepdims=True) for _, oh in picks]
    r_k = [jnp.sum(jnp.where(oh, ranks, 0.0), axis=1, keepdims=True) for _, oh in picks]
    total = s_k[0]
    for s in s_k[1:]:
        total = total + s
    lane_out = lax.broadcasted_iota(I32, (tm, 128), 1)
    e_out = jnp.zeros((tm, 128), I32)
    r_out = jnp.zeros((tm, 128), I32)
    g_out = jnp.zeros((tm, 128), F32)
    for k in range(TOP_K):
        here = lane_out == k
        e_out = jnp.where(here, picks[k][0].astype(I32), e_out)
        r_out = jnp.where(here, r_k[k].astype(I32), r_out)
        g_out = jnp.where(here, s_k[k] / total * np.float32(ROUTED_SCALE), g_out)
    eidx_ref[...] = e_out
    rank_ref[...] = r_out
    gate_ref[...] = g_out


def _route(scores, bias):
    T = scores.shape[0]
    tm = ROUTE_TILE
    out_spec = pl.BlockSpec((tm, 128), lambda i: (i, 0))
    return pl.pallas_call(
        _route_kernel,
        grid=(T // tm,),
        in_specs=[pl.BlockSpec((tm, N_EXPERTS), lambda i: (i, 0)), _const_spec(bias.shape)],
        out_specs=[out_spec, out_spec, out_spec, _const_spec((1, N_EXPERTS))],
        out_shape=[jax.ShapeDtypeStruct((T, 128), I32), jax.ShapeDtypeStruct((T, 128), I32),
                   jax.ShapeDtypeStruct((T, 128), F32), jax.ShapeDtypeStruct((1, N_EXPERTS), F32)],
        scratch_shapes=[pltpu.VMEM((1, N_EXPERTS), F32)],
        compiler_params=pltpu.CompilerParams(dimension_semantics=("arbitrary",)),
        name="route_topk",
    )(scores, bias)


def _dest_kernel(eidx_ref, rank_ref, pstart_ref, out_ref):
    tm = eidx_ref.shape[0]
    eidx = eidx_ref[...]
    lane_e = lax.broadcasted_iota(I32, (tm, N_EXPERTS), 1)
    lane_o = lax.broadcasted_iota(I32, (tm, LANES), 1)
    start = jnp.zeros((tm, LANES), F32)
    for k in range(TOP_K):
        hit = lane_e == eidx[:, k:k + 1]
        s_k = jnp.sum(jnp.where(hit, pstart_ref[...], 0.0), axis=1, keepdims=True)
        start = jnp.where(lane_o == k, s_k, start)
    dest = start + rank_ref[...].astype(F32)
    out_ref[...] = jnp.transpose(dest)[:TOP_K].astype(I32)


def _dest_rows(eidx, rank, pstart):
    T = eidx.shape[0]
    tm = MERGE_TILE
    tok_spec = pl.BlockSpec((tm, LANES), lambda i: (i, 0))
    return pl.pallas_call(
        _dest_kernel,
        grid=(T // tm,),
        in_specs=[tok_spec, tok_spec, _const_spec(pstart.shape)],
        out_specs=pl.BlockSpec((TOP_K, tm), lambda i: (0, i)),
        out_shape=jax.ShapeDtypeStruct((TOP_K, T), I32),
        compiler_params=pltpu.CompilerParams(dimension_semantics=("arbitrary",)),
        name="moe_dest_rows",
    )(eidx, rank, pstart)


def _sc_workers():
    info = plsc.get_sparse_core_info()
    return info.num_cores, info.num_cores * info.num_subcores


def _sc_scatter_rows(rows, dest, n_out):
    n_chunks, n_dst, ch = dest.shape
    width = rows.shape[1]
    n_cores, n_workers = _sc_workers()
    per_w = n_chunks // n_workers
    assert n_chunks % n_workers == 0 and per_w % 2 == 0 and rows.shape[0] == n_chunks * ch

    def body(rows_hbm, dest_hbm, out_hbm, idx_v, buf, lsem, ssem):
        wid = lax.axis_index("s") * n_cores + lax.axis_index("c")
        c0 = wid * per_w
        pltpu.sync_copy(dest_hbm.at[pl.ds(c0, per_w)], idx_v)

        def load(c, b):
            return pltpu.make_async_copy(rows_hbm.at[pl.ds((c0 + c) * ch, ch)], buf.at[b], lsem.at[b])

        def scatter(c, b, k):
            return pltpu.make_async_copy(buf.at[b], out_hbm.at[idx_v.at[c, k]], ssem.at[b])

        load(0, 0).start()

        @pl.loop(0, per_w, step=2)
        def _(c):
            for b in range(2):
                cc = c + b
                load(cc, b).wait()

                @pl.when(cc >= 1)
                def _():
                    for k in range(n_dst):
                        scatter(cc - 1, 1 - b, k).wait()

                @pl.when(cc + 1 < per_w)
                def _():
                    load(cc + 1, 1 - b).start()

                for k in range(n_dst):
                    scatter(cc, b, k).start()

        for k in range(n_dst):
            scatter(per_w - 1, 1, k).wait()

    mesh = plsc.VectorSubcoreMesh(core_axis_name="c", subcore_axis_name="s")
    return pl.kernel(
        body, out_type=jax.ShapeDtypeStruct((n_out, width), rows.dtype), mesh=mesh,
        scratch_types=[pltpu.VMEM((per_w, n_dst, ch), I32), pltpu.VMEM((2, ch, width), rows.dtype),
                       pltpu.SemaphoreType.DMA((2,)), pltpu.SemaphoreType.DMA((2,))],
        name="moe_dispatch_scatter",
    )(rows, dest)


def _sc_gather_rows(table, idx):
    n_chunks, ch = idx.shape
    width = table.shape[1]
    n_cores, n_workers = _sc_workers()
    per_w = n_chunks // n_workers
    assert n_chunks % n_workers == 0 and per_w % 2 == 0

    def body(table_hbm, idx_hbm, out_hbm, idx_v, buf, gsem, wsem):
        wid = lax.axis_index("s") * n_cores + lax.axis_index("c")
        c0 = wid * per_w
        pltpu.sync_copy(idx_hbm.at[pl.ds(c0, per_w)], idx_v)

        def gather(c, b):
            return pltpu.make_async_copy(table_hbm.at[idx_v.at[c]], buf.at[b], gsem.at[b])

        def write(c, b):
            return pltpu.make_async_copy(buf.at[b], out_hbm.at[pl.ds((c0 + c) * ch, ch)], wsem.at[b])

        gather(0, 0).start()

        @pl.loop(0, per_w, step=2)
        def _(c):
            for b in range(2):
                cc = c + b
                gather(cc, b).wait()

                @pl.when(cc >= 1)
                def _():
                    write(cc - 1, 1 - b).wait()

                @pl.when(cc + 1 < per_w)
                def _():
                    gather(cc + 1, 1 - b).start()

                write(cc, b).start()

        write(per_w - 1, 1).wait()

    mesh = plsc.VectorSubcoreMesh(core_axis_name="c", subcore_axis_name="s")
    return pl.kernel(
        body, out_type=jax.ShapeDtypeStruct((n_chunks * ch, width), table.dtype), mesh=mesh,
        scratch_types=[pltpu.VMEM((per_w, ch), I32), pltpu.VMEM((2, ch, width), table.dtype),
                       pltpu.SemaphoreType.DMA((2,)), pltpu.SemaphoreType.DMA((2,))],
        name="moe_combine_gather",
    )(table, idx)


def _expert_kernel(be_ref, ue_ref, nu_ref, xs_hbm, wg_hbm, wu_hbm, wd_hbm, y_hbm,
                   xbuf, ybuf, rg, ru, rd, wg_s, wu_s, wd_s, xsem, ysem, wsem):
    n_used = nu_ref[0]
    n_exp = nu_ref[1]
    half = D_MODEL // 2
    blk = MOE_BLOCK

    def rows(j):
        return pl.ds(pl.multiple_of(j * blk, blk), blk)

    def x_copy(j, p):
        return pltpu.make_async_copy(xs_hbm.at[rows(j)], xbuf.at[p], xsem.at[p])

    def y_copy(j, p):
        return pltpu.make_async_copy(ybuf.at[p], y_hbm.at[rows(j)], ysem.at[p])

    def w_copies(q, s):
        e = ue_ref[q]
        return (pltpu.make_async_copy(wg_hbm.at[e], rg.at[s], wsem.at[s, 0]),
                pltpu.make_async_copy(wu_hbm.at[e], ru.at[s], wsem.at[s, 1]),
                pltpu.make_async_copy(wd_hbm.at[e], rd.at[s], wsem.at[s, 2]))

    for c in w_copies(0, 0):
        c.start()
    x_copy(0, 0).start()

    @pl.when(n_used > 1)
    def _():
        x_copy(1, 1).start()

    def block_step(j, p, q):
        is_new = (j == 0) | (be_ref[j] != be_ref[jnp.maximum(j - 1, 0)])
        q = q + is_new.astype(I32)

        @pl.when(is_new)
        def _():
            s = q % 2
            for c in w_copies(q, s):
                c.wait()
            wg_s[...] = rg[s].astype(BF16)
            wu_s[...] = ru[s].astype(BF16)
            wd_s[...] = rd[s].astype(BF16)

            @pl.when(q + 1 < n_exp)
            def _():
                for c in w_copies(q + 1, 1 - s):
                    c.start()

        x_copy(j, p).wait()

        @pl.when(j >= 2)
        def _():
            y_copy(j - 2, p).wait()

        lo, hi = _unpack_bf16_pairs(xbuf[p])
        xlo = lo.astype(BF16)
        xhi = hi.astype(BF16)
        g = (jnp.dot(xlo, wg_s[:half], preferred_element_type=F32)
             + jnp.dot(xhi, wg_s[half:], preferred_element_type=F32))
        u = (jnp.dot(xlo, wu_s[:half], preferred_element_type=F32)
             + jnp.dot(xhi, wu_s[half:], preferred_element_type=F32))
        hb = (g * jax.nn.sigmoid(g) * u).astype(BF16)
        ybuf[p] = _pack_bf16_pairs(jnp.dot(hb, wd_s[...], preferred_element_type=F32))
        y_copy(j, p).start()

        @pl.when(j + 2 < n_used)
        def _():
            x_copy(j + 2, p).start()

        return q

    def pair(m, q):
        q = block_step(2 * m, 0, q)
        return lax.cond(2 * m + 1 < n_used, lambda q: block_step(2 * m + 1, 1, q), lambda q: q, q)

    lax.fori_loop(0, (n_used + 1) // 2, pair, jnp.int32(-1))

    for back in (2, 1):
        b = n_used - back
        for p in range(2):
            @pl.when((b >= 0) & (b % 2 == p))
            def _():
                y_copy(b, p).wait()


def _experts(block_e, used_e, counts2, xs, wg, wu, wd):
    n_rows, half = xs.shape
    smem = pl.BlockSpec(memory_space=pltpu.SMEM)
    hbm = pl.BlockSpec(memory_space=pl.ANY)
    return pl.pallas_call(
        _expert_kernel,
        in_specs=[smem, smem, smem, hbm, hbm, hbm, hbm],
        out_specs=hbm,
        out_shape=jax.ShapeDtypeStruct((n_rows, half), I32),
        scratch_shapes=[pltpu.VMEM((2, MOE_BLOCK, half), I32), pltpu.VMEM((2, MOE_BLOCK, half), I32),
                        pltpu.VMEM((2, D_MODEL, EXPERT_DIM), F32),
                        pltpu.VMEM((2, D_MODEL, EXPERT_DIM), F32),
                        pltpu.VMEM((2, EXPERT_DIM, D_MODEL), F32),
                        pltpu.VMEM((D_MODEL, EXPERT_DIM), BF16),
                        pltpu.VMEM((D_MODEL, EXPERT_DIM), BF16),
                        pltpu.VMEM((EXPERT_DIM, D_MODEL), BF16),
                        pltpu.SemaphoreType.DMA((2,)), pltpu.SemaphoreType.DMA((2,)),
                        pltpu.SemaphoreType.DMA((2, 3))],
        compiler_params=pltpu.CompilerParams(vmem_limit_bytes=VMEM_LIMIT),
        name="moe_experts",
    )(block_e, used_e, counts2, xs, wg, wu, wd)


def _combine_kernel(yg_ref, gate_ref, x1_ref, swgu_ref, swd_ref, g2_ref, b2_ref, out_ref, *, alpha):
    tc = x1_ref.shape[0]
    x1 = x1_ref[...]
    gu = jnp.dot(x1.astype(BF16), swgu_ref[...], preferred_element_type=F32)
    g = gu[:, :EXPERT_DIM]
    u = gu[:, EXPERT_DIM:]
    shared = jnp.dot((g * jax.nn.sigmoid(g) * u).astype(BF16), swd_ref[...], preferred_element_type=F32)
    half = D_MODEL // 2
    acc_lo = jnp.zeros((tc, half), F32)
    acc_hi = jnp.zeros((tc, half), F32)
    for k in range(TOP_K):
        lo, hi = _unpack_bf16_pairs(yg_ref[k])
        gk = gate_ref[:, k:k + 1]
        acc_lo = acc_lo + gk * lo
        acc_hi = acc_hi + gk * hi
    routed = jnp.concatenate([acc_lo, acc_hi], axis=1)
    out_ref[...] = _layer_norm(alpha * x1 + (routed + shared), g2_ref[...], b2_ref[...])


def _combine(yg, gate, x1, swgu, swd, g2, b2, alpha):
    T, D = x1.shape
    tc = COMBINE_TILE
    consts = (swgu, swd, g2, b2)
    return pl.pallas_call(
        functools.partial(_combine_kernel, alpha=alpha),
        grid=(T // tc,),
        in_specs=[pl.BlockSpec((TOP_K, tc, D // 2), lambda i: (0, i, 0)),
                  pl.BlockSpec((tc, 128), lambda i: (i, 0)),
                  pl.BlockSpec((tc, D), lambda i: (i, 0))] + [_const_spec(c.shape) for c in consts],
        out_specs=pl.BlockSpec((tc, D), lambda i: (i, 0)),
        out_shape=jax.ShapeDtypeStruct((T, D), F32),
        compiler_params=pltpu.CompilerParams(dimension_semantics=("arbitrary",), vmem_limit_bytes=VMEM_LIMIT),
        name="moe_combine_norm",
    )(yg, gate, x1, *consts)


def _mixer(x, in_w, in_b, ng, nb, spatial_w, spatial_b, proj_a_w, proj_b_w, out_w, ln1_g, ln1_b,
           router_w, alpha):
    gw, aw, D = GMLP_WIDTH, ATTN_WIDTH, D_MODEL
    w = in_w.astype(BF16)
    q0 = 2 * gw
    wuv, buv = w[:, :q0], in_b[None, :q0]
    watt = jnp.stack([jnp.concatenate([w[:, q0 + s * aw + p * GROUP_WIDTH:q0 + s * aw + (p + 1) * GROUP_WIDTH]
                                       for s in range(3)], axis=1) for p in range(len(DILATED_PATTERNS))])
    batt = jnp.stack([jnp.concatenate([in_b[q0 + s * aw + p * GROUP_WIDTH:q0 + s * aw + (p + 1) * GROUP_WIDTH]
                                       for s in range(3)])[None] for p in range(len(DILATED_PATTERNS))])
    g0 = q0 + 3 * aw
    wg, bg = w[:, g0:], in_b[None, g0:]
    sb = jnp.repeat(spatial_b.T, gw // GMLP_GROUPS, axis=1)
    ya, gb, a1, a4, a16 = _input_projection(
        x, wuv, buv, watt, batt, wg, bg, ng[None], nb[None], spatial_w, sb, proj_a_w.astype(BF16))
    attn_outs = [_dilated_attention(a, p, d) for p, (a, (_, d)) in enumerate(zip((a1, a4, a16), DILATED_PATTERNS))]
    rwh = router_w.astype(BF16)
    rwl = (router_w - rwh.astype(F32)).astype(BF16)
    return _merge_and_norm(attn_outs, ya, gb, x, proj_b_w.astype(BF16), out_w.astype(BF16),
                           ln1_g[None], ln1_b[None], rwh, rwl, alpha)


def _moe(x1, x1p, scores, router_bias, w_gate, w_up, w_down, sw_gate, sw_up, sw_down, ln2_g, ln2_b, alpha):
    T = x1.shape[0]
    A = T * TOP_K
    n_blocks = (A + N_EXPERTS * (MOE_BLOCK - 1)) // MOE_BLOCK
    eidx, rank, gate, counts = _route(scores, router_bias[None])
    counts = counts[0].astype(I32)
    padded = (counts + MOE_BLOCK - 1) // MOE_BLOCK * MOE_BLOCK
    pend = jnp.cumsum(padded).astype(I32)
    pstart = pend - padded
    block_starts = jnp.arange(n_blocks, dtype=I32) * MOE_BLOCK
    block_e = jnp.minimum(jnp.sum((pend[None, :] <= block_starts[:, None]).astype(I32), axis=1), N_EXPERTS - 1)
    used = counts > 0
    used_e = jnp.argsort(jnp.logical_not(used), stable=True).astype(I32)
    counts2 = jnp.stack([pend[-1] // MOE_BLOCK, jnp.sum(used.astype(I32))]).astype(I32)
    dest_t = _dest_rows(eidx, rank, pstart.astype(F32)[None])
    xs = _sc_scatter_rows(x1p, dest_t.reshape(TOP_K, T // SC_CHUNK, SC_CHUNK).transpose(1, 0, 2),
                          n_blocks * MOE_BLOCK)
    y_rows = _experts(block_e, used_e, counts2, xs, w_gate, w_up, w_down)
    yg = _sc_gather_rows(y_rows, dest_t.reshape(A // SC_CHUNK, SC_CHUNK))
    swgu = jnp.concatenate([sw_gate, sw_up], axis=1).astype(BF16)
    return _combine(yg.reshape(TOP_K, T, D_MODEL // 2), gate, x1, swgu, sw_down.astype(BF16),
                    ln2_g[None], ln2_b[None], alpha)


def kernel(x, in_w, in_b, gmlp_norm_g, gmlp_norm_b, spatial_w, spatial_b, proj_a_w, proj_b_w, out_w,
           ln1_g, ln1_b, router_w, router_bias, expert_w_gate, expert_w_up, expert_w_down,
           shared_w_gate, shared_w_up, shared_w_down, ln2_g, ln2_b):
    B, S, D = x.shape
    depth = in_w.shape[0]
    alpha = np.float32((2.0 * depth) ** 0.25)
    for l in range(depth):
        x1, x1p, scores = _mixer(x, in_w[l], in_b[l], gmlp_norm_g[l], gmlp_norm_b[l], spatial_w[l],
                                 spatial_b[l], proj_a_w[l], proj_b_w[l], out_w[l], ln1_g[l], ln1_b[l],
                                 router_w[l], alpha)
        out = _moe(x1.reshape(B * S, D), x1p.reshape(B * S, D // 2), scores.reshape(B * S, N_EXPERTS),
                   router_bias[l], expert_w_gate[l], expert_w_up[l], expert_w_down[l],
                   shared_w_gate[l], shared_w_up[l], shared_w_down[l], ln2_g[l], ln2_b[l], alpha)
        x = out.reshape(B, S, D)
    return x
```

```python
import functools
import math

import numpy as np
import jax
import jax.numpy as jnp
from jax import lax
from jax.experimental import pallas as pl
from jax.experimental.pallas import tpu as pltpu
from jax.experimental.pallas import tpu_sc as plsc

F32 = jnp.float32
BF16 = jnp.bfloat16
U32 = jnp.uint32
I32 = jnp.int32

D_MODEL = 1024
GMLP_WIDTH = 1024
GMLP_GROUPS = 8
GMLP_CHUNK = 128
HEAD_DIM = 64
DILATED_PATTERNS = ((128, 1), (512, 4), (2048, 16))
HEADS_PER_GROUP = 4
GROUP_WIDTH = HEADS_PER_GROUP * HEAD_DIM
ATTN_WIDTH = GROUP_WIDTH * len(DILATED_PATTERNS)
ATTN_BLOCK = 128
N_EXPERTS = 256
TOP_K = 8
TOP_K_SHIFT = 3
N_EXPERT_GROUPS = 8
TOPK_GROUPS = 4
EXPERT_DIM = 256
ROUTED_SCALE = 2.5
LN_EPS = 1e-5
LANES = 128
MASKED_SCORE = -1e30

PROJ_TILE = 512
ATTN_QBLOCKS = 2
MERGE_TILE = 512
ROUTE_TILE = 512
DEST_TILE = 2048
MOE_BLOCK = 256
SC_CHUNK = 64
COMBINE_TILE = 128
VMEM_LIMIT = 56 * 1024 * 1024


def _layer_norm(y, g, b):
    mu = jnp.mean(y, axis=-1, keepdims=True)
    yc = y - mu
    var = jnp.mean(yc * yc, axis=-1, keepdims=True)
    return yc * lax.rsqrt(var + LN_EPS) * g + b


def _gelu(x):
    return 0.5 * x * (1.0 + lax.erf(x * np.float32(math.sqrt(0.5))))


def _pack_bf16_pairs(x):
    w = x.shape[1] // 2
    bits = pltpu.bitcast(x.astype(BF16).astype(F32), U32)
    return pltpu.bitcast((bits[:, :w] >> 16) | (bits[:, w:] & jnp.uint32(0xFFFF0000)), I32)


def _unpack_bf16_pairs(words):
    w = pltpu.bitcast(words, U32)
    lo = pltpu.bitcast(w << 16, F32)
    hi = pltpu.bitcast(w & jnp.uint32(0xFFFF0000), F32)
    return lo, hi


def _const_spec(shape):
    nd = len(shape)
    return pl.BlockSpec(shape, lambda *_: (0,) * nd)


def _proj_kernel(x_ref, wuv_ref, buv_ref, watt_ref, batt_ref, wg_ref, bg_ref, ng_ref, nb_ref,
                 sw_ref, sb_ref, pa_ref, ya_ref, gb_ref, a1_ref, a4_ref, a16_ref, xc_ref):
    tm = x_ref.shape[1]
    gw = GMLP_WIDTH
    xb = x_ref[0].astype(BF16)

    def proj(w, b):
        return jnp.dot(xb, w, preferred_element_type=F32) + b

    u = _gelu(proj(wuv_ref[:, :gw], buv_ref[:, :gw]))
    v = _gelu(proj(wuv_ref[:, gw:], buv_ref[:, gw:]))
    v = _layer_norm(v, ng_ref[...], nb_ref[...]).astype(BF16)

    cw = gw // GMLP_GROUPS
    row = lax.broadcasted_iota(I32, (GMLP_CHUNK, GMLP_CHUNK), 0)
    col = lax.broadcasted_iota(I32, (GMLP_CHUNK, GMLP_CHUNK), 1)
    ws = [jnp.where(row >= col, sw_ref[g], 0.0).astype(BF16) for g in range(GMLP_GROUPS)]
    chunks = []
    for c in range(tm // GMLP_CHUNK):
        vc = v[c * GMLP_CHUNK:(c + 1) * GMLP_CHUNK]
        cols = [jnp.dot(ws[g], vc[:, g * cw:(g + 1) * cw], preferred_element_type=F32)
                for g in range(GMLP_GROUPS)]
        chunks.append(jnp.concatenate(cols, axis=1) + sb_ref[...])
    vmix = jnp.concatenate(chunks, axis=0)
    ya = jnp.dot((u * vmix).astype(BF16), pa_ref[...], preferred_element_type=F32)
    ga = jax.nn.sigmoid(proj(wg_ref[:, :D_MODEL], bg_ref[:, :D_MODEL]))
    ya_ref[0] = (ga * ya).astype(BF16)
    gb_ref[0] = jax.nn.sigmoid(proj(wg_ref[:, D_MODEL:], bg_ref[:, D_MODEL:])).astype(BF16)

    n_chunks = x_ref.shape[2] // LANES
    for c in range(n_chunks):
        xc_ref[c] = x_ref[0, :, c * LANES:(c + 1) * LANES]
    for p, (a_ref, (_, d)) in enumerate(zip((a1_ref, a4_ref, a16_ref), DILATED_PATTERNS)):
        n = tm // d
        if d == 1:
            xp = xb
        else:
            xp = jnp.concatenate(
                [jnp.concatenate([xc_ref[c, pl.ds(r, n, stride=d), :] for c in range(n_chunks)], axis=1)
                 for r in range(d)], axis=0).astype(BF16)
        h = (jnp.dot(xp, watt_ref[p], preferred_element_type=F32) + batt_ref[p]).astype(BF16)
        for r in range(d):
            a_ref[0, r] = h[r * n:(r + 1) * n]


def _input_projection(x, wuv, buv, watt, batt, wg, bg, ng, nb, sw, sb, pa):
    B, S, D = x.shape
    tm = PROJ_TILE
    grid = (B, S // tm)
    out_shape = [jax.ShapeDtypeStruct((B, S, D), BF16), jax.ShapeDtypeStruct((B, S, D), BF16)]
    out_specs = [pl.BlockSpec((1, tm, D), lambda b, t: (b, t, 0)),
                 pl.BlockSpec((1, tm, D), lambda b, t: (b, t, 0))]
    for _, d in DILATED_PATTERNS:
        out_shape.append(jax.ShapeDtypeStruct((B, d, S // d, ATTN_WIDTH), BF16))
        out_specs.append(pl.BlockSpec((1, d, tm // d, ATTN_WIDTH), lambda b, t: (b, 0, t, 0)))
    consts = (wuv, buv, watt, batt, wg, bg, ng, nb, sw, sb, pa)
    return pl.pallas_call(
        _proj_kernel,
        grid=grid,
        in_specs=[pl.BlockSpec((1, tm, D), lambda b, t: (b, t, 0))] + [_const_spec(c.shape) for c in consts],
        out_specs=out_specs,
        out_shape=out_shape,
        scratch_shapes=[pltpu.VMEM((D // LANES, tm, LANES), F32)],
        compiler_params=pltpu.CompilerParams(
            dimension_semantics=("arbitrary", "arbitrary"), vmem_limit_bytes=VMEM_LIMIT),
        name="input_projection",
    )(x, *consts)


def _attn_kernel(qkv_ref, bias_ref, bias0_ref, o_ref, lse_ref):
    nq = pl.program_id(2)
    gwid = GROUP_WIDTH
    blk = ATTN_BLOCK
    lane = lax.broadcasted_iota(I32, (1, gwid), 1)
    head_masks = [(lane >= h * HEAD_DIM) & (lane < (h + 1) * HEAD_DIM) for h in range(HEADS_PER_GROUP)]
    q_scales = [jnp.where(m, np.float32(HEAD_DIM ** -0.5), 0.0).astype(BF16) for m in head_masks]
    for j in range(ATTN_QBLOCKS):
        n = nq * ATTN_QBLOCKS + j
        q0 = pl.multiple_of(n * blk, blk)
        p0 = pl.multiple_of(jnp.maximum(n - 1, 0) * blk, blk)
        q = qkv_ref[0, 0, pl.ds(q0, blk), 0:gwid]
        kk = jnp.concatenate([qkv_ref[0, 0, pl.ds(p0, blk), gwid:2 * gwid],
                              qkv_ref[0, 0, pl.ds(q0, blk), gwid:2 * gwid]], axis=0)
        vv = jnp.concatenate([qkv_ref[0, 0, pl.ds(p0, blk), 2 * gwid:3 * gwid],
                              qkv_ref[0, 0, pl.ds(q0, blk), 2 * gwid:3 * gwid]], axis=0)
        qs = jnp.concatenate([q * s for s in q_scales], axis=0)
        s = lax.dot_general(qs, kk, (((1,), (1,)), ((), ())), preferred_element_type=F32)
        s = s + jnp.where(n == 0, bias0_ref[...], bias_ref[...])
        m = jnp.max(s, axis=1, keepdims=True)
        p = jnp.exp(s - m)
        den = jnp.sum(p, axis=1, keepdims=True)
        pv = jnp.dot(p.astype(BF16), vv, preferred_element_type=F32)
        on = pv / den
        lse = m + jnp.log(den)
        o = jnp.zeros((blk, gwid), F32)
        l = jnp.zeros((blk, gwid), F32)
        for h in range(HEADS_PER_GROUP):
            o = jnp.where(head_masks[h], on[h * blk:(h + 1) * blk], o)
            l = jnp.where(head_masks[h], lse[h * blk:(h + 1) * blk], l)
        o_ref[0, 0, j * blk:(j + 1) * blk, :] = o.astype(BF16)
        lse_ref[0, 0, j * blk:(j + 1) * blk, :] = l


def _alibi_slopes(n):
    def pow2_slopes(m):
        start = 2.0 ** (-8.0 / m)
        return [start ** (i + 1) for i in range(m)]
    p = 2 ** int(math.floor(math.log2(n)))
    s = pow2_slopes(p)
    if p < n:
        s = s + pow2_slopes(2 * p)[0::2][: n - p]
    return np.array(sorted(s, reverse=True), dtype=np.float32)


def _attn_bias_tables(group, dilation):
    blk = ATTN_BLOCK
    slopes = _alibi_slopes(HEADS_PER_GROUP * len(DILATED_PATTERNS))
    slopes = slopes[group * HEADS_PER_GROUP:(group + 1) * HEADS_PER_GROUP]
    qi = np.arange(blk)[:, None]
    ki = np.arange(2 * blk)[None, :]
    delta = blk + qi - ki
    band = (delta >= 0) & (delta <= blk)
    bias = -slopes[:, None, None] * (delta * dilation).astype(np.float32)[None]
    full = np.where(band[None], bias, np.float32(MASKED_SCORE)).astype(np.float32)
    first = np.where((ki >= blk)[None], full, np.float32(MASKED_SCORE)).astype(np.float32)
    return full.reshape(HEADS_PER_GROUP * blk, 2 * blk), first.reshape(HEADS_PER_GROUP * blk, 2 * blk)


def _dilated_attention(qkv, group, dilation):
    B, d, sd, _ = qkv.shape
    rows = ATTN_QBLOCKS * ATTN_BLOCK
    bias, bias0 = _attn_bias_tables(group, dilation)
    grid = (B, d, sd // rows)
    out_spec = pl.BlockSpec((1, 1, rows, GROUP_WIDTH), lambda b, r, n: (b, r, n, 0))
    return pl.pallas_call(
        _attn_kernel,
        grid=grid,
        in_specs=[pl.BlockSpec((1, 1, sd, ATTN_WIDTH), lambda b, r, n: (b, r, 0, 0)),
                  _const_spec(bias.shape), _const_spec(bias0.shape)],
        out_specs=[out_spec, out_spec],
        out_shape=[jax.ShapeDtypeStruct((B, d, sd, GROUP_WIDTH), BF16),
                   jax.ShapeDtypeStruct((B, d, sd, GROUP_WIDTH), F32)],
        compiler_params=pltpu.CompilerParams(
            dimension_semantics=("arbitrary", "arbitrary", "arbitrary"), vmem_limit_bytes=VMEM_LIMIT),
        name=f"dilated_attention_d{dilation}",
    )(qkv, jnp.asarray(bias), jnp.asarray(bias0))


def _merge_kernel(o1_ref, l1_ref, o4_ref, l4_ref, o16_ref, l16_ref, ya_ref, gb_ref, x_ref,
                  pb_ref, ow_ref, g1_ref, b1_ref, rwh_ref, rwl_ref,
                  x1_ref, x1p_ref, sc_ref, so4, sl4, so16, sl16, *, alpha):
    tm = x_ref.shape[1]
    n_chunks = GROUP_WIDTH // LANES
    for (o_ref, l_ref, so, sl, d) in ((o4_ref, l4_ref, so4, sl4, 4), (o16_ref, l16_ref, so16, sl16, 16)):
        n = tm // d
        for r in range(d):
            o_r = o_ref[0, r].astype(F32)
            l_r = l_ref[0, r]
            for c in range(n_chunks):
                so[c, pl.ds(r, n, stride=d), :] = o_r[:, c * LANES:(c + 1) * LANES]
                sl[c, pl.ds(r, n, stride=d), :] = l_r[:, c * LANES:(c + 1) * LANES]

    def natural(s):
        return jnp.concatenate([s[c] for c in range(n_chunks)], axis=1)

    l1 = l1_ref[0, 0]
    l4 = natural(sl4)
    l16 = natural(sl16)
    lmax = jnp.maximum(jnp.maximum(l1, l4), l16)
    e1 = jnp.exp(l1 - lmax)
    e4 = jnp.exp(l4 - lmax)
    e16 = jnp.exp(l16 - lmax)
    yb = (e1 * o1_ref[0, 0].astype(F32) + e4 * natural(so4) + e16 * natural(so16)) / (e1 + e4 + e16)
    ybp = jnp.dot(yb.astype(BF16), pb_ref[...], preferred_element_type=F32)
    merged = ya_ref[0].astype(F32) + gb_ref[0].astype(F32) * ybp
    mix = jnp.dot(merged.astype(BF16), ow_ref[...], preferred_element_type=F32)
    x1 = _layer_norm(alpha * x_ref[0] + mix, g1_ref[...], b1_ref[...])
    x1_ref[0] = x1
    x1p_ref[0] = _pack_bf16_pairs(x1)
    hi = x1.astype(BF16)
    lo = (x1 - hi.astype(F32)).astype(BF16)
    def logits_t(w_ref, xt):
        return lax.dot_general(w_ref[...], xt, (((1,), (1,)), ((), ())), preferred_element_type=F32)
    sc_ref[...] = jax.nn.sigmoid(logits_t(rwh_ref, hi) + logits_t(rwh_ref, lo) + logits_t(rwl_ref, hi))


def _merge_and_norm(attn_outs, ya, gb, x, pb, ow, g1, b1, rwh, rwl, alpha):
    B, S, D = x.shape
    tm = MERGE_TILE
    in_specs = []
    args = []
    for (o, l), (_, d) in zip(attn_outs, DILATED_PATTERNS):
        spec = pl.BlockSpec((1, d, tm // d, GROUP_WIDTH), lambda b, t: (b, 0, t, 0))
        in_specs += [spec, spec]
        args += [o, l]
    tok_spec = pl.BlockSpec((1, tm, D), lambda b, t: (b, t, 0))
    in_specs += [tok_spec, tok_spec, tok_spec]
    args += [ya, gb, x]
    consts = (pb, ow, g1, b1, rwh, rwl)
    in_specs += [_const_spec(c.shape) for c in consts]
    return pl.pallas_call(
        functools.partial(_merge_kernel, alpha=alpha),
        grid=(B, S // tm),
        in_specs=in_specs,
        out_specs=[tok_spec,
                   pl.BlockSpec((1, tm, D // 2), lambda b, t: (b, t, 0)),
                   pl.BlockSpec((N_EXPERTS, tm), lambda b, t: (0, b * (S // tm) + t))],
        out_shape=[jax.ShapeDtypeStruct((B, S, D), F32),
                   jax.ShapeDtypeStruct((B, S, D // 2), I32),
                   jax.ShapeDtypeStruct((N_EXPERTS, B * S), F32)],
        scratch_shapes=[pltpu.VMEM((GROUP_WIDTH // LANES, tm, LANES), F32) for _ in range(4)],
        compiler_params=pltpu.CompilerParams(
            dimension_semantics=("arbitrary", "arbitrary"), vmem_limit_bytes=VMEM_LIMIT),
        name="merge_norm_router",
    )(*args, *consts)


def _sortable_key(x):
    bits = pltpu.bitcast(x, I32)
    return jnp.where(bits < 0, bits ^ jnp.int32(0x7FFFFFFF), bits)


def _route_kernel(sc_ref, bias_ref, before_ref, eidx_ref, rank_ref, gate_ref, cnt_ref, carry_ref):
    ne, tm = sc_ref.shape
    gsize = ne // N_EXPERT_GROUPS
    neg_inf = np.float32(-np.inf)
    removed = jnp.int32(-2 ** 31)

    @pl.when(pl.program_id(0) == 0)
    def _():
        carry_ref[...] = jnp.zeros_like(carry_ref)

    scores = sc_ref[...]
    biased = scores + bias_ref[...]

    gsum = []
    for g in range(N_EXPERT_GROUPS):
        v = biased[g * gsize:(g + 1) * gsize]
        m1 = jnp.max(v, axis=0, keepdims=True)
        n1 = jnp.sum(jnp.where(v == m1, 1.0, 0.0), axis=0, keepdims=True)
        m2 = jnp.max(jnp.where(v < m1, v, neg_inf), axis=0, keepdims=True)
        gsum.append(m1 + jnp.where(n1 >= 2.0, m1, m2))
    gkey = _sortable_key(jnp.concatenate(gsum, axis=0))

    def pick_first_max(keys, ids, n_ids):
        m = jnp.max(keys, axis=0, keepdims=True)
        idx = jnp.min(jnp.where(keys == m, ids, n_ids), axis=0, keepdims=True)
        hit = ids == idx
        return idx, hit, jnp.where(hit, removed, keys)

    gid = lax.broadcasted_iota(I32, (N_EXPERT_GROUPS, tm), 0)
    for _ in range(TOPK_GROUPS):
        _, _, gkey = pick_first_max(gkey, gid, N_EXPERT_GROUPS)
    group_on = gkey == removed

    masked = jnp.concatenate(
        [jnp.where(group_on[g:g + 1], biased[g * gsize:(g + 1) * gsize], neg_inf)
         for g in range(N_EXPERT_GROUPS)], axis=0)
    keys = _sortable_key(masked)
    eid = lax.broadcasted_iota(I32, (ne, tm), 0)
    picks = []
    for _ in range(TOP_K):
        idx, _, keys = pick_first_max(keys, eid, ne)
        picks.append(idx)

    sel = jnp.where(keys == removed, 1.0, 0.0)
    ranks = jnp.dot(sel.astype(BF16), before_ref[...], preferred_element_type=F32) + carry_ref[...]
    carry_ref[...] = carry_ref[...] + jnp.sum(sel, axis=1, keepdims=True)
    cnt_ref[...] = carry_ref[...]

    s_k, r_k = [], []
    for idx in picks:
        hit = eid == idx
        s_k.append(jnp.sum(jnp.where(hit, scores, 0.0), axis=0, keepdims=True))
        r_k.append(jnp.sum(jnp.where(hit, ranks, 0.0), axis=0, keepdims=True))
    total = s_k[0]
    for s in s_k[1:]:
        total = total + s
    eidx_ref[...] = jnp.concatenate(picks, axis=0)
    rank_ref[...] = jnp.concatenate(r_k, axis=0).astype(I32)
    gate_ref[...] = jnp.concatenate([s / total * np.float32(ROUTED_SCALE) for s in s_k], axis=0)


def _route(scores_t, bias):
    ne, T = scores_t.shape
    tm = ROUTE_TILE
    before = jnp.asarray(np.triu(np.ones((tm, tm), np.float32), k=1), BF16)
    out_spec = pl.BlockSpec((TOP_K, tm), lambda i: (0, i))
    return pl.pallas_call(
        _route_kernel,
        grid=(T // tm,),
        in_specs=[pl.BlockSpec((ne, tm), lambda i: (0, i)), _const_spec(bias.shape), _const_spec(before.shape)],
        out_specs=[out_spec, out_spec, out_spec, _const_spec((ne, 1))],
        out_shape=[jax.ShapeDtypeStruct((TOP_K, T), I32), jax.ShapeDtypeStruct((TOP_K, T), I32),
                   jax.ShapeDtypeStruct((TOP_K, T), F32), jax.ShapeDtypeStruct((ne, 1), F32)],
        scratch_shapes=[pltpu.VMEM((ne, 1), F32)],
        compiler_params=pltpu.CompilerParams(dimension_semantics=("arbitrary",), vmem_limit_bytes=VMEM_LIMIT),
        name="route_topk",
    )(scores_t, bias, before)


def _dest_kernel(pstart_ref, eidx_ref, rank_ref, out_ref):
    eidx = eidx_ref[...]
    start = jnp.zeros(eidx.shape, I32)
    for e in range(N_EXPERTS):
        start = jnp.where(eidx == e, pstart_ref[e], start)
    out_ref[...] = start + rank_ref[...]


def _dest_rows(eidx_t, rank_t, pstart):
    T = eidx_t.shape[1]
    tm = DEST_TILE
    tok_spec = pl.BlockSpec((TOP_K, tm), lambda i: (0, i))
    return pl.pallas_call(
        _dest_kernel,
        grid=(T // tm,),
        in_specs=[pl.BlockSpec(memory_space=pltpu.SMEM), tok_spec, tok_spec],
        out_specs=tok_spec,
        out_shape=jax.ShapeDtypeStruct((TOP_K, T), I32),
        compiler_params=pltpu.CompilerParams(dimension_semantics=("arbitrary",)),
        name="moe_dest_rows",
    )(pstart, eidx_t, rank_t)


def _sc_workers():
    info = plsc.get_sparse_core_info()
    return info.num_cores, info.num_cores * info.num_subcores


def _sc_scatter_rows(rows, dest, n_out):
    n_chunks, n_dst, ch = dest.shape
    width = rows.shape[1]
    n_cores, n_workers = _sc_workers()
    per_w = n_chunks // n_workers
    assert n_chunks % n_workers == 0 and per_w % 2 == 0 and rows.shape[0] == n_chunks * ch

    def body(rows_hbm, dest_hbm, out_hbm, idx_v, buf, lsem, ssem):
        wid = lax.axis_index("s") * n_cores + lax.axis_index("c")
        c0 = wid * per_w
        pltpu.sync_copy(dest_hbm.at[pl.ds(c0, per_w)], idx_v)

        def load(c, b):
            return pltpu.make_async_copy(rows_hbm.at[pl.ds((c0 + c) * ch, ch)], buf.at[b], lsem.at[b])

        def scatter(c, b, k):
            return pltpu.make_async_copy(buf.at[b], out_hbm.at[idx_v.at[c, k]], ssem.at[b])

        load(0, 0).start()

        @pl.loop(0, per_w, step=2)
        def _(c):
            for b in range(2):
                cc = c + b
                load(cc, b).wait()

                @pl.when(cc >= 1)
                def _():
                    for k in range(n_dst):
                        scatter(cc - 1, 1 - b, k).wait()

                @pl.when(cc + 1 < per_w)
                def _():
                    load(cc + 1, 1 - b).start()

                for k in range(n_dst):
                    scatter(cc, b, k).start()

        for k in range(n_dst):
            scatter(per_w - 1, 1, k).wait()

    mesh = plsc.VectorSubcoreMesh(core_axis_name="c", subcore_axis_name="s")
    return pl.kernel(
        body, out_type=jax.ShapeDtypeStruct((n_out, width), rows.dtype), mesh=mesh,
        scratch_types=[pltpu.VMEM((per_w, n_dst, ch), I32), pltpu.VMEM((2, ch, width), rows.dtype),
                       pltpu.SemaphoreType.DMA((2,)), pltpu.SemaphoreType.DMA((2,))],
        name="moe_dispatch_scatter",
    )(rows, dest)


def _sc_gather_rows(table, idx):
    n_chunks, ch = idx.shape
    width = table.shape[1]
    n_cores, n_workers = _sc_workers()
    per_w = n_chunks // n_workers
    assert n_chunks % n_workers == 0 and per_w % 2 == 0

    def body(table_hbm, idx_hbm, out_hbm, idx_v, buf, gsem, wsem):
        wid = lax.axis_index("s") * n_cores + lax.axis_index("c")
        c0 = wid * per_w
        pltpu.sync_copy(idx_hbm.at[pl.ds(c0, per_w)], idx_v)

        def gather(c, b):
            return pltpu.make_async_copy(table_hbm.at[idx_v.at[c]], buf.at[b], gsem.at[b])

        def write(c, b):
            return pltpu.make_async_copy(buf.at[b], out_hbm.at[pl.ds((c0 + c) * ch, ch)], wsem.at[b])

        gather(0, 0).start()

        @pl.loop(0, per_w, step=2)
        def _(c):
            for b in range(2):
                cc = c + b
                gather(cc, b).wait()

                @pl.when(cc >= 1)
                def _():
                    write(cc - 1, 1 - b).wait()

                @pl.when(cc + 1 < per_w)
                def _():
                    gather(cc + 1, 1 - b).start()

                write(cc, b).start()

        write(per_w - 1, 1).wait()

    mesh = plsc.VectorSubcoreMesh(core_axis_name="c", subcore_axis_name="s")
    return pl.kernel(
        body, out_type=jax.ShapeDtypeStruct((n_chunks * ch, width), table.dtype), mesh=mesh,
        scratch_types=[pltpu.VMEM((per_w, ch), I32), pltpu.VMEM((2, ch, width), table.dtype),
                       pltpu.SemaphoreType.DMA((2,)), pltpu.SemaphoreType.DMA((2,))],
        name="moe_combine_gather",
    )(table, idx)


def _expert_kernel(be_ref, ue_ref, nu_ref, xs_hbm, wg_hbm, wu_hbm, wd_hbm, y_hbm,
                   xbuf, ybuf, rg, ru, rd, wg_s, wu_s, wd_s, xsem, ysem, wsem):
    n_used = nu_ref[0]
    n_exp = nu_ref[1]
    half = D_MODEL // 2
    blk = MOE_BLOCK

    def rows(j):
        return pl.ds(pl.multiple_of(j * blk, blk), blk)

    def x_copy(j, p):
        return pltpu.make_async_copy(xs_hbm.at[rows(j)], xbuf.at[p], xsem.at[p])

    def y_copy(j, p):
        return pltpu.make_async_copy(ybuf.at[p], y_hbm.at[rows(j)], ysem.at[p])

    def w_copies(q, s):
        e = ue_ref[q]
        return (pltpu.make_async_copy(wg_hbm.at[e], rg.at[s], wsem.at[s, 0]),
                pltpu.make_async_copy(wu_hbm.at[e], ru.at[s], wsem.at[s, 1]),
                pltpu.make_async_copy(wd_hbm.at[e], rd.at[s], wsem.at[s, 2]))

    for c in w_copies(0, 0):
        c.start()
    x_copy(0, 0).start()

    @pl.when(n_used > 1)
    def _():
        x_copy(1, 1).start()

    def block_step(j, p, q):
        is_new = (j == 0) | (be_ref[j] != be_ref[jnp.maximum(j - 1, 0)])
        q = q + is_new.astype(I32)

        @pl.when(is_new)
        def _():
            s = q % 2
            for c in w_copies(q, s):
                c.wait()
            wg_s[...] = rg[s].astype(BF16)
            wu_s[...] = ru[s].astype(BF16)
            wd_s[...] = rd[s].astype(BF16)

            @pl.when(q + 1 < n_exp)
            def _():
                for c in w_copies(q + 1, 1 - s):
                    c.start()

        x_copy(j, p).wait()

        @pl.when(j >= 2)
        def _():
            y_copy(j - 2, p).wait()

        lo, hi = _unpack_bf16_pairs(xbuf[p])
        xlo = lo.astype(BF16)
        xhi = hi.astype(BF16)
        g = (jnp.dot(xlo, wg_s[:half], preferred_element_type=F32)
             + jnp.dot(xhi, wg_s[half:], preferred_element_type=F32))
        u = (jnp.dot(xlo, wu_s[:half], preferred_element_type=F32)
             + jnp.dot(xhi, wu_s[half:], preferred_element_type=F32))
        hb = (g * jax.nn.sigmoid(g) * u).astype(BF16)
        ybuf[p] = _pack_bf16_pairs(jnp.dot(hb, wd_s[...], preferred_element_type=F32))
        y_copy(j, p).start()

        @pl.when(j + 2 < n_used)
        def _():
            x_copy(j + 2, p).start()

        return q

    def pair(m, q):
        q = block_step(2 * m, 0, q)
        return lax.cond(2 * m + 1 < n_used, lambda q: block_step(2 * m + 1, 1, q), lambda q: q, q)

    lax.fori_loop(0, (n_used + 1) // 2, pair, jnp.int32(-1))

    for back in (2, 1):
        b = n_used - back
        for p in range(2):
            @pl.when((b >= 0) & (b % 2 == p))
            def _():
                y_copy(b, p).wait()


def _experts(block_e, used_e, counts2, xs, wg, wu, wd):
    n_rows, half = xs.shape
    smem = pl.BlockSpec(memory_space=pltpu.SMEM)
    hbm = pl.BlockSpec(memory_space=pl.ANY)
    return pl.pallas_call(
        _expert_kernel,
        in_specs=[smem, smem, smem, hbm, hbm, hbm, hbm],
        out_specs=hbm,
        out_shape=jax.ShapeDtypeStruct((n_rows, half), I32),
        scratch_shapes=[pltpu.VMEM((2, MOE_BLOCK, half), I32), pltpu.VMEM((2, MOE_BLOCK, half), I32),
                        pltpu.VMEM((2, D_MODEL, EXPERT_DIM), F32),
                        pltpu.VMEM((2, D_MODEL, EXPERT_DIM), F32),
                        pltpu.VMEM((2, EXPERT_DIM, D_MODEL), F32),
                        pltpu.VMEM((D_MODEL, EXPERT_DIM), BF16),
                        pltpu.VMEM((D_MODEL, EXPERT_DIM), BF16),
                        pltpu.VMEM((EXPERT_DIM, D_MODEL), BF16),
                        pltpu.SemaphoreType.DMA((2,)), pltpu.SemaphoreType.DMA((2,)),
                        pltpu.SemaphoreType.DMA((2, 3))],
        compiler_params=pltpu.CompilerParams(vmem_limit_bytes=VMEM_LIMIT),
        name="moe_experts",
    )(block_e, used_e, counts2, xs, wg, wu, wd)


def _combine_kernel(yg_ref, gate_ref, x1_ref, swgu_ref, swd_ref, g2_ref, b2_ref, out_ref, *, alpha):
    tc = x1_ref.shape[0]
    x1 = x1_ref[...]
    gu = jnp.dot(x1.astype(BF16), swgu_ref[...], preferred_element_type=F32)
    g = gu[:, :EXPERT_DIM]
    u = gu[:, EXPERT_DIM:]
    shared = jnp.dot((g * jax.nn.sigmoid(g) * u).astype(BF16), swd_ref[...], preferred_element_type=F32)
    half = D_MODEL // 2
    acc_lo = jnp.zeros((tc, half), F32)
    acc_hi = jnp.zeros((tc, half), F32)
    gates = jnp.transpose(jnp.concatenate([gate_ref[...], jnp.zeros((tc - TOP_K, tc), F32)], axis=0))
    for k in range(TOP_K):
        lo, hi = _unpack_bf16_pairs(yg_ref[k])
        gk = gates[:, k:k + 1]
        acc_lo = acc_lo + gk * lo
        acc_hi = acc_hi + gk * hi
    routed = jnp.concatenate([acc_lo, acc_hi], axis=1)
    out_ref[...] = _layer_norm(alpha * x1 + (routed + shared), g2_ref[...], b2_ref[...])


def _combine(yg, gate, x1, swgu, swd, g2, b2, alpha):
    T, D = x1.shape
    tc = COMBINE_TILE
    consts = (swgu, swd, g2, b2)
    return pl.pallas_call(
        functools.partial(_combine_kernel, alpha=alpha),
        grid=(T // tc,),
        in_specs=[pl.BlockSpec((TOP_K, tc, D // 2), lambda i: (0, i, 0)),
                  pl.BlockSpec((TOP_K, tc), lambda i: (0, i)),
                  pl.BlockSpec((tc, D), lambda i: (i, 0))] + [_const_spec(c.shape) for c in consts],
        out_specs=pl.BlockSpec((tc, D), lambda i: (i, 0)),
        out_shape=jax.ShapeDtypeStruct((T, D), F32),
        compiler_params=pltpu.CompilerParams(dimension_semantics=("arbitrary",), vmem_limit_bytes=VMEM_LIMIT),
        name="moe_combine_norm",
    )(yg, gate, x1, *consts)


def _mixer(x, in_w, in_b, ng, nb, spatial_w, spatial_b, proj_a_w, proj_b_w, out_w, ln1_g, ln1_b,
           router_w, alpha):
    gw, aw, D = GMLP_WIDTH, ATTN_WIDTH, D_MODEL
    w = in_w.astype(BF16)
    q0 = 2 * gw
    wuv, buv = w[:, :q0], in_b[None, :q0]
    watt = jnp.stack([jnp.concatenate([w[:, q0 + s * aw + p * GROUP_WIDTH:q0 + s * aw + (p + 1) * GROUP_WIDTH]
                                       for s in range(3)], axis=1) for p in range(len(DILATED_PATTERNS))])
    batt = jnp.stack([jnp.concatenate([in_b[q0 + s * aw + p * GROUP_WIDTH:q0 + s * aw + (p + 1) * GROUP_WIDTH]
                                       for s in range(3)])[None] for p in range(len(DILATED_PATTERNS))])
    g0 = q0 + 3 * aw
    wg, bg = w[:, g0:], in_b[None, g0:]
    sb = jnp.repeat(spatial_b.T, gw // GMLP_GROUPS, axis=1)
    ya, gb, a1, a4, a16 = _input_projection(
        x, wuv, buv, watt, batt, wg, bg, ng[None], nb[None], spatial_w, sb, proj_a_w.astype(BF16))
    attn_outs = [_dilated_attention(a, p, d) for p, (a, (_, d)) in enumerate(zip((a1, a4, a16), DILATED_PATTERNS))]
    rw_t = router_w.T
    rwh = rw_t.astype(BF16)
    rwl = (rw_t - rwh.astype(F32)).astype(BF16)
    return _merge_and_norm(attn_outs, ya, gb, x, proj_b_w.astype(BF16), out_w.astype(BF16),
                           ln1_g[None], ln1_b[None], rwh, rwl, alpha)


def _moe(x1, x1p, scores_t, router_bias, w_gate, w_up, w_down, sw_gate, sw_up, sw_down, ln2_g, ln2_b, alpha):
    T = x1.shape[0]
    A = T * TOP_K
    n_blocks = (A + N_EXPERTS * (MOE_BLOCK - 1)) // MOE_BLOCK
    eidx_t, rank_t, gate_t, counts = _route(scores_t, router_bias[:, None])
    counts = counts[:, 0].astype(I32)
    padded = (counts + MOE_BLOCK - 1) // MOE_BLOCK * MOE_BLOCK
    pend = jnp.cumsum(padded).astype(I32)
    pstart = pend - padded
    block_starts = jnp.arange(n_blocks, dtype=I32) * MOE_BLOCK
    block_e = jnp.minimum(jnp.sum((pend[None, :] <= block_starts[:, None]).astype(I32), axis=1), N_EXPERTS - 1)
    used = counts > 0
    used_e = jnp.argsort(jnp.logical_not(used), stable=True).astype(I32)
    counts2 = jnp.stack([pend[-1] // MOE_BLOCK, jnp.sum(used.astype(I32))]).astype(I32)
    dest_t = _dest_rows(eidx_t, rank_t, pstart)
    xs = _sc_scatter_rows(x1p, dest_t.reshape(TOP_K, T // SC_CHUNK, SC_CHUNK).transpose(1, 0, 2),
                          n_blocks * MOE_BLOCK)
    y_rows = _experts(block_e, used_e, counts2, xs, w_gate, w_up, w_down)
    yg = _sc_gather_rows(y_rows, dest_t.reshape(A // SC_CHUNK, SC_CHUNK))
    swgu = jnp.concatenate([sw_gate, sw_up], axis=1).astype(BF16)
    return _combine(yg.reshape(TOP_K, T, D_MODEL // 2), gate_t, x1, swgu, sw_down.astype(BF16),
                    ln2_g[None], ln2_b[None], alpha)


def kernel(x, in_w, in_b, gmlp_norm_g, gmlp_norm_b, spatial_w, spatial_b, proj_a_w, proj_b_w, out_w,
           ln1_g, ln1_b, router_w, router_bias, expert_w_gate, expert_w_up, expert_w_down,
           shared_w_gate, shared_w_up, shared_w_down, ln2_g, ln2_b):
    B, S, D = x.shape
    depth = in_w.shape[0]
    alpha = np.float32((2.0 * depth) ** 0.25)
    for l in range(depth):
        x1, x1p, scores_t = _mixer(x, in_w[l], in_b[l], gmlp_norm_g[l], gmlp_norm_b[l], spatial_w[l],
                                 spatial_b[l], proj_a_w[l], proj_b_w[l], out_w[l], ln1_g[l], ln1_b[l],
                                 router_w[l], alpha)
        out = _moe(x1.reshape(B * S, D), x1p.reshape(B * S, D // 2), scores_t,
                   router_bias[l], expert_w_gate[l], expert_w_up[l], expert_w_down[l],
                   shared_w_gate[l], shared_w_up[l], shared_w_down[l], ln2_g[l], ln2_b[l], alpha)
        x = out.reshape(B, S, D)
    return x
```

```python
import functools
import math

import numpy as np
import jax
import jax.numpy as jnp
from jax import lax
from jax.experimental import pallas as pl
from jax.experimental.pallas import tpu as pltpu
from jax.experimental.pallas import tpu_sc as plsc

F32 = jnp.float32
BF16 = jnp.bfloat16
U32 = jnp.uint32
I32 = jnp.int32

D_MODEL = 1024
GMLP_WIDTH = 1024
GMLP_GROUPS = 8
GMLP_CHUNK = 128
HEAD_DIM = 64
DILATED_PATTERNS = ((128, 1), (512, 4), (2048, 16))
HEADS_PER_GROUP = 4
GROUP_WIDTH = HEADS_PER_GROUP * HEAD_DIM
ATTN_WIDTH = GROUP_WIDTH * len(DILATED_PATTERNS)
ATTN_BLOCK = 128
N_EXPERTS = 256
TOP_K = 8
TOP_K_SHIFT = 3
N_EXPERT_GROUPS = 8
TOPK_GROUPS = 4
EXPERT_DIM = 256
ROUTED_SCALE = 2.5
LN_EPS = 1e-5
LANES = 128
MASKED_SCORE = -1e30

PROJ_TILE = 512
ATTN_QBLOCKS = 4
MERGE_TILE = 512
ROUTE_TILE = 512
DEST_TILE = 2048
MOE_BLOCK = 256
SC_CHUNK = 64
W_SLOTS = 3
COMBINE_TILE = 256
VMEM_LIMIT = 56 * 1024 * 1024


def _layer_norm(y, g, b):
    mu = jnp.mean(y, axis=-1, keepdims=True)
    yc = y - mu
    var = jnp.mean(yc * yc, axis=-1, keepdims=True)
    return yc * lax.rsqrt(var + LN_EPS) * g + b


def _gelu(x):
    return 0.5 * x * (1.0 + lax.erf(x * np.float32(math.sqrt(0.5))))


def _pack_bf16_pairs(x):
    w = x.shape[1] // 2
    bits = pltpu.bitcast(x.astype(BF16).astype(F32), U32)
    return pltpu.bitcast((bits[:, :w] >> 16) | (bits[:, w:] & jnp.uint32(0xFFFF0000)), I32)


def _unpack_bf16_pairs(words):
    w = pltpu.bitcast(words, U32)
    lo = pltpu.bitcast(w << 16, F32)
    hi = pltpu.bitcast(w & jnp.uint32(0xFFFF0000), F32)
    return lo, hi


def _const_spec(shape):
    nd = len(shape)
    return pl.BlockSpec(shape, lambda *_: (0,) * nd)


def _proj_kernel(x_ref, wuv_ref, buv_ref, watt_ref, batt_ref, wg_ref, bg_ref, ng_ref, nb_ref,
                 sw_ref, sb_ref, pa_ref, ya_ref, gb_ref, a1_ref, a4_ref, a16_ref, xc_ref):
    tm = x_ref.shape[1]
    gw = GMLP_WIDTH
    xb = x_ref[0].astype(BF16)

    def proj(w, b):
        return jnp.dot(xb, w, preferred_element_type=F32) + b

    u = _gelu(proj(wuv_ref[:, :gw], buv_ref[:, :gw]))
    v = _gelu(proj(wuv_ref[:, gw:], buv_ref[:, gw:]))
    v = _layer_norm(v, ng_ref[...], nb_ref[...]).astype(BF16)

    cw = gw // GMLP_GROUPS
    row = lax.broadcasted_iota(I32, (GMLP_CHUNK, GMLP_CHUNK), 0)
    col = lax.broadcasted_iota(I32, (GMLP_CHUNK, GMLP_CHUNK), 1)
    ws = [jnp.where(row >= col, sw_ref[g], 0.0).astype(BF16) for g in range(GMLP_GROUPS)]
    chunks = []
    for c in range(tm // GMLP_CHUNK):
        vc = v[c * GMLP_CHUNK:(c + 1) * GMLP_CHUNK]
        cols = [jnp.dot(ws[g], vc[:, g * cw:(g + 1) * cw], preferred_element_type=F32)
                for g in range(GMLP_GROUPS)]
        chunks.append(jnp.concatenate(cols, axis=1) + sb_ref[...])
    vmix = jnp.concatenate(chunks, axis=0)
    ya = jnp.dot((u * vmix).astype(BF16), pa_ref[...], preferred_element_type=F32)
    ga = jax.nn.sigmoid(proj(wg_ref[:, :D_MODEL], bg_ref[:, :D_MODEL]))
    ya_ref[0] = (ga * ya).astype(BF16)
    gb_ref[0] = jax.nn.sigmoid(proj(wg_ref[:, D_MODEL:], bg_ref[:, D_MODEL:])).astype(BF16)

    n_chunks = x_ref.shape[2] // LANES
    for c in range(n_chunks):
        xc_ref[c] = x_ref[0, :, c * LANES:(c + 1) * LANES]
    for p, (a_ref, (_, d)) in enumerate(zip((a1_ref, a4_ref, a16_ref), DILATED_PATTERNS)):
        n = tm // d
        if d == 1:
            xp = xb
        else:
            xp = jnp.concatenate(
                [jnp.concatenate([xc_ref[c, pl.ds(r, n, stride=d), :] for c in range(n_chunks)], axis=1)
                 for r in range(d)], axis=0).astype(BF16)
        h = (jnp.dot(xp, watt_ref[p], preferred_element_type=F32) + batt_ref[p]).astype(BF16)
        for r in range(d):
            a_ref[0, r] = h[r * n:(r + 1) * n]


def _input_projection(x, wuv, buv, watt, batt, wg, bg, ng, nb, sw, sb, pa):
    B, S, D = x.shape
    tm = PROJ_TILE
    grid = (B, S // tm)
    out_shape = [jax.ShapeDtypeStruct((B, S, D), BF16), jax.ShapeDtypeStruct((B, S, D), BF16)]
    out_specs = [pl.BlockSpec((1, tm, D), lambda b, t: (b, t, 0)),
                 pl.BlockSpec((1, tm, D), lambda b, t: (b, t, 0))]
    for _, d in DILATED_PATTERNS:
        out_shape.append(jax.ShapeDtypeStruct((B, d, S // d, ATTN_WIDTH), BF16))
        out_specs.append(pl.BlockSpec((1, d, tm // d, ATTN_WIDTH), lambda b, t: (b, 0, t, 0)))
    consts = (wuv, buv, watt, batt, wg, bg, ng, nb, sw, sb, pa)
    return pl.pallas_call(
        _proj_kernel,
        grid=grid,
        in_specs=[pl.BlockSpec((1, tm, D), lambda b, t: (b, t, 0))] + [_const_spec(c.shape) for c in consts],
        out_specs=out_specs,
        out_shape=out_shape,
        scratch_shapes=[pltpu.VMEM((D // LANES, tm, LANES), F32)],
        compiler_params=pltpu.CompilerParams(
            dimension_semantics=("arbitrary", "arbitrary"), vmem_limit_bytes=VMEM_LIMIT),
        name="input_projection",
    )(x, *consts)


def _attn_kernel(qkv_ref, bias_ref, o_ref, lse_ref, *, qblocks):
    nq = pl.program_id(2)
    gwid = GROUP_WIDTH
    blk = ATTN_BLOCK
    lane = lax.broadcasted_iota(I32, (1, gwid), 1)
    head_masks = [(lane >= h * HEAD_DIM) & (lane < (h + 1) * HEAD_DIM) for h in range(HEADS_PER_GROUP)]
    q_scales = [jnp.where(m, np.float32(HEAD_DIM ** -0.5), 0.0).astype(BF16) for m in head_masks]

    def rows_of(j):
        n = nq * qblocks + j
        return n, pl.multiple_of(n * blk, blk), pl.multiple_of(jnp.maximum(n - 1, 0) * blk, blk)

    def scores(j):
        n, q0, p0 = rows_of(j)
        q = qkv_ref[0, 0, pl.ds(q0, blk), 0:gwid]
        kk = jnp.concatenate([qkv_ref[0, 0, pl.ds(p0, blk), gwid:2 * gwid],
                              qkv_ref[0, 0, pl.ds(q0, blk), gwid:2 * gwid]], axis=0)
        qs = jnp.concatenate([q * s for s in q_scales], axis=0)
        s = lax.dot_general(qs, kk, (((1,), (1,)), ((), ())), preferred_element_type=F32)
        return s + bias_ref[jnp.where(n == 0, 1, 0)]

    def finish(j, s):
        _, q0, p0 = rows_of(j)
        vv = jnp.concatenate([qkv_ref[0, 0, pl.ds(p0, blk), 2 * gwid:3 * gwid],
                              qkv_ref[0, 0, pl.ds(q0, blk), 2 * gwid:3 * gwid]], axis=0)
        m = jnp.max(s, axis=1, keepdims=True)
        p = jnp.exp(s - m)
        den = jnp.sum(p, axis=1, keepdims=True)
        pv = jnp.dot(p.astype(BF16), vv, preferred_element_type=F32)
        on = pv / den
        lse = m + jnp.log(den)
        o = jnp.zeros((blk, gwid), F32)
        l = jnp.zeros((blk, gwid), F32)
        for h in range(HEADS_PER_GROUP):
            o = jnp.where(head_masks[h], on[h * blk:(h + 1) * blk], o)
            l = jnp.where(head_masks[h], lse[h * blk:(h + 1) * blk], l)
        o_ref[0, 0, j * blk:(j + 1) * blk, :] = o.astype(BF16)
        lse_ref[0, 0, j * blk:(j + 1) * blk, :] = l

    s_next = scores(0)
    for j in range(qblocks):
        s_cur = s_next
        if j + 1 < qblocks:
            s_next = scores(j + 1)
        finish(j, s_cur)


def _alibi_slopes(n):
    def pow2_slopes(m):
        start = 2.0 ** (-8.0 / m)
        return [start ** (i + 1) for i in range(m)]
    p = 2 ** int(math.floor(math.log2(n)))
    s = pow2_slopes(p)
    if p < n:
        s = s + pow2_slopes(2 * p)[0::2][: n - p]
    return np.array(sorted(s, reverse=True), dtype=np.float32)


def _attn_bias_tables(group, dilation):
    blk = ATTN_BLOCK
    slopes = _alibi_slopes(HEADS_PER_GROUP * len(DILATED_PATTERNS))
    slopes = slopes[group * HEADS_PER_GROUP:(group + 1) * HEADS_PER_GROUP]
    qi = np.arange(blk)[:, None]
    ki = np.arange(2 * blk)[None, :]
    delta = blk + qi - ki
    band = (delta >= 0) & (delta <= blk)
    bias = -slopes[:, None, None] * (delta * dilation).astype(np.float32)[None]
    full = np.where(band[None], bias, np.float32(MASKED_SCORE)).astype(np.float32)
    first = np.where((ki >= blk)[None], full, np.float32(MASKED_SCORE)).astype(np.float32)
    return full.reshape(HEADS_PER_GROUP * blk, 2 * blk), first.reshape(HEADS_PER_GROUP * blk, 2 * blk)


def _dilated_attention(qkv, group, dilation):
    B, d, sd, _ = qkv.shape
    qblocks = min(ATTN_QBLOCKS, sd // ATTN_BLOCK)
    rows = qblocks * ATTN_BLOCK
    bias = np.stack(_attn_bias_tables(group, dilation))
    grid = (B, d, sd // rows)
    out_spec = pl.BlockSpec((1, 1, rows, GROUP_WIDTH), lambda b, r, n: (b, r, n, 0))
    return pl.pallas_call(
        functools.partial(_attn_kernel, qblocks=qblocks),
        grid=grid,
        in_specs=[pl.BlockSpec((1, 1, sd, ATTN_WIDTH), lambda b, r, n: (b, r, 0, 0)),
                  _const_spec(bias.shape)],
        out_specs=[out_spec, out_spec],
        out_shape=[jax.ShapeDtypeStruct((B, d, sd, GROUP_WIDTH), BF16),
                   jax.ShapeDtypeStruct((B, d, sd, GROUP_WIDTH), F32)],
        compiler_params=pltpu.CompilerParams(
            dimension_semantics=("arbitrary", "arbitrary", "arbitrary"), vmem_limit_bytes=VMEM_LIMIT),
        name=f"dilated_attention_d{dilation}",
    )(qkv, jnp.asarray(bias))


def _merge_kernel(o1_ref, l1_ref, o4_ref, l4_ref, o16_ref, l16_ref, ya_ref, gb_ref, x_ref,
                  pb_ref, ow_ref, g1_ref, b1_ref, rwh_ref, rwl_ref,
                  x1_ref, x1p_ref, sc_ref, so4, sl4, so16, sl16, *, alpha):
    tm = x_ref.shape[1]
    n_chunks = GROUP_WIDTH // LANES
    for (o_ref, l_ref, so, sl, d) in ((o4_ref, l4_ref, so4, sl4, 4), (o16_ref, l16_ref, so16, sl16, 16)):
        n = tm // d
        for r in range(d):
            o_r = o_ref[0, r].astype(F32)
            l_r = l_ref[0, r]
            for c in range(n_chunks):
                so[c, pl.ds(r, n, stride=d), :] = o_r[:, c * LANES:(c + 1) * LANES]
                sl[c, pl.ds(r, n, stride=d), :] = l_r[:, c * LANES:(c + 1) * LANES]

    def natural(s):
        return jnp.concatenate([s[c] for c in range(n_chunks)], axis=1)

    l1 = l1_ref[0, 0]
    l4 = natural(sl4)
    l16 = natural(sl16)
    lmax = jnp.maximum(jnp.maximum(l1, l4), l16)
    e1 = jnp.exp(l1 - lmax)
    e4 = jnp.exp(l4 - lmax)
    e16 = jnp.exp(l16 - lmax)
    yb = (e1 * o1_ref[0, 0].astype(F32) + e4 * natural(so4) + e16 * natural(so16)) / (e1 + e4 + e16)
    ybp = jnp.dot(yb.astype(BF16), pb_ref[...], preferred_element_type=F32)
    merged = ya_ref[0].astype(F32) + gb_ref[0].astype(F32) * ybp
    mix = jnp.dot(merged.astype(BF16), ow_ref[...], preferred_element_type=F32)
    x1 = _layer_norm(alpha * x_ref[0] + mix, g1_ref[...], b1_ref[...])
    x1_ref[0] = x1
    x1p_ref[0] = _pack_bf16_pairs(x1)
    hi = x1.astype(BF16)
    lo = (x1 - hi.astype(F32)).astype(BF16)
    def logits_t(w_ref, xt):
        return lax.dot_general(w_ref[...], xt, (((1,), (1,)), ((), ())), preferred_element_type=F32)
    sc_ref[...] = jax.nn.sigmoid(logits_t(rwh_ref, hi) + logits_t(rwh_ref, lo) + logits_t(rwl_ref, hi))


def _merge_and_norm(attn_outs, ya, gb, x, pb, ow, g1, b1, rwh, rwl, alpha):
    B, S, D = x.shape
    tm = MERGE_TILE
    in_specs = []
    args = []
    for (o, l), (_, d) in zip(attn_outs, DILATED_PATTERNS):
        spec = pl.BlockSpec((1, d, tm // d, GROUP_WIDTH), lambda b, t: (b, 0, t, 0))
        in_specs += [spec, spec]
        args += [o, l]
    tok_spec = pl.BlockSpec((1, tm, D), lambda b, t: (b, t, 0))
    in_specs += [tok_spec, tok_spec, tok_spec]
    args += [ya, gb, x]
    consts = (pb, ow, g1, b1, rwh, rwl)
    in_specs += [_const_spec(c.shape) for c in consts]
    return pl.pallas_call(
        functools.partial(_merge_kernel, alpha=alpha),
        grid=(B, S // tm),
        in_specs=in_specs,
        out_specs=[tok_spec,
                   pl.BlockSpec((1, tm, D // 2), lambda b, t: (b, t, 0)),
                   pl.BlockSpec((N_EXPERTS, tm), lambda b, t: (0, b * (S // tm) + t))],
        out_shape=[jax.ShapeDtypeStruct((B, S, D), F32),
                   jax.ShapeDtypeStruct((B, S, D // 2), I32),
                   jax.ShapeDtypeStruct((N_EXPERTS, B * S), F32)],
        scratch_shapes=[pltpu.VMEM((GROUP_WIDTH // LANES, tm, LANES), F32) for _ in range(4)],
        compiler_params=pltpu.CompilerParams(
            dimension_semantics=("arbitrary", "arbitrary"), vmem_limit_bytes=VMEM_LIMIT),
        name="merge_norm_router",
    )(*args, *consts)


def _sortable_key(x):
    bits = pltpu.bitcast(x, I32)
    return jnp.where(bits < 0, bits ^ jnp.int32(0x7FFFFFFF), bits)


def _route_kernel(sc_ref, bias_ref, before_ref, eidx_ref, rank_ref, gate_ref, cnt_ref, carry_ref):
    ne, tm = sc_ref.shape
    gsize = ne // N_EXPERT_GROUPS
    neg_inf = np.float32(-np.inf)
    removed = jnp.int32(-2 ** 31)

    @pl.when(pl.program_id(0) == 0)
    def _():
        carry_ref[...] = jnp.zeros_like(carry_ref)

    scores = sc_ref[...]
    biased = scores + bias_ref[...]

    gsum = []
    for g in range(N_EXPERT_GROUPS):
        v = biased[g * gsize:(g + 1) * gsize]
        m1 = jnp.max(v, axis=0, keepdims=True)
        n1 = jnp.sum(jnp.where(v == m1, 1.0, 0.0), axis=0, keepdims=True)
        m2 = jnp.max(jnp.where(v < m1, v, neg_inf), axis=0, keepdims=True)
        gsum.append(m1 + jnp.where(n1 >= 2.0, m1, m2))
    gkey = _sortable_key(jnp.concatenate(gsum, axis=0))

    def pick_first_max(keys, ids, n_ids):
        m = jnp.max(keys, axis=0, keepdims=True)
        idx = jnp.min(jnp.where(keys == m, ids, n_ids), axis=0, keepdims=True)
        hit = ids == idx
        return idx, hit, jnp.where(hit, removed, keys)

    gid = lax.broadcasted_iota(I32, (N_EXPERT_GROUPS, tm), 0)
    for _ in range(TOPK_GROUPS):
        _, _, gkey = pick_first_max(gkey, gid, N_EXPERT_GROUPS)
    group_on = gkey == removed

    masked = jnp.concatenate(
        [jnp.where(group_on[g:g + 1], biased[g * gsize:(g + 1) * gsize], neg_inf)
         for g in range(N_EXPERT_GROUPS)], axis=0)
    keys = _sortable_key(masked)
    eid = lax.broadcasted_iota(I32, (ne, tm), 0)
    picks = []
    for _ in range(TOP_K):
        idx, _, keys = pick_first_max(keys, eid, ne)
        picks.append(idx)

    sel = jnp.where(keys == removed, 1.0, 0.0)
    ranks = jnp.dot(sel.astype(BF16), before_ref[...], preferred_element_type=F32) + carry_ref[...]
    carry_ref[...] = carry_ref[...] + jnp.sum(sel, axis=1, keepdims=True)
    cnt_ref[...] = carry_ref[...]

    s_k, r_k = [], []
    for idx in picks:
        hit = eid == idx
        s_k.append(jnp.sum(jnp.where(hit, scores, 0.0), axis=0, keepdims=True))
        r_k.append(jnp.sum(jnp.where(hit, ranks, 0.0), axis=0, keepdims=True))
    total = s_k[0]
    for s in s_k[1:]:
        total = total + s
    eidx_ref[...] = jnp.concatenate(picks, axis=0)
    rank_ref[...] = jnp.concatenate(r_k, axis=0).astype(I32)
    gate_ref[...] = jnp.concatenate([s / total * np.float32(ROUTED_SCALE) for s in s_k], axis=0)


def _route(scores_t, bias):
    ne, T = scores_t.shape
    tm = ROUTE_TILE
    before = jnp.asarray(np.triu(np.ones((tm, tm), np.float32), k=1), BF16)
    out_spec = pl.BlockSpec((TOP_K, tm), lambda i: (0, i))
    return pl.pallas_call(
        _route_kernel,
        grid=(T // tm,),
        in_specs=[pl.BlockSpec((ne, tm), lambda i: (0, i)), _const_spec(bias.shape), _const_spec(before.shape)],
        out_specs=[out_spec, out_spec, out_spec, _const_spec((ne, 1))],
        out_shape=[jax.ShapeDtypeStruct((TOP_K, T), I32), jax.ShapeDtypeStruct((TOP_K, T), I32),
                   jax.ShapeDtypeStruct((TOP_K, T), F32), jax.ShapeDtypeStruct((ne, 1), F32)],
        scratch_shapes=[pltpu.VMEM((ne, 1), F32)],
        compiler_params=pltpu.CompilerParams(dimension_semantics=("arbitrary",), vmem_limit_bytes=VMEM_LIMIT),
        name="route_topk",
    )(scores_t, bias, before)


def _dest_kernel(pstart_ref, eidx_ref, rank_ref, out_ref):
    eidx = eidx_ref[...]
    start = jnp.zeros(eidx.shape, I32)
    for e in range(N_EXPERTS):
        start = jnp.where(eidx == e, pstart_ref[e], start)
    out_ref[...] = start + rank_ref[...]


def _dest_rows(eidx_t, rank_t, pstart):
    T = eidx_t.shape[1]
    tm = DEST_TILE
    tok_spec = pl.BlockSpec((TOP_K, tm), lambda i: (0, i))
    return pl.pallas_call(
        _dest_kernel,
        grid=(T // tm,),
        in_specs=[pl.BlockSpec(memory_space=pltpu.SMEM), tok_spec, tok_spec],
        out_specs=tok_spec,
        out_shape=jax.ShapeDtypeStruct((TOP_K, T), I32),
        compiler_params=pltpu.CompilerParams(dimension_semantics=("arbitrary",)),
        name="moe_dest_rows",
    )(pstart, eidx_t, rank_t)


def _sc_workers():
    info = plsc.get_sparse_core_info()
    return info.num_cores, info.num_cores * info.num_subcores


def _sc_scatter_rows(rows, dest, n_out):
    n_chunks, n_dst, ch = dest.shape
    width = rows.shape[1]
    n_cores, n_workers = _sc_workers()
    per_w = n_chunks // n_workers
    assert n_chunks % n_workers == 0 and per_w % 2 == 0 and rows.shape[0] == n_chunks * ch

    def body(rows_hbm, dest_hbm, out_hbm, idx_v, buf, lsem, ssem):
        wid = lax.axis_index("s") * n_cores + lax.axis_index("c")
        c0 = wid * per_w
        pltpu.sync_copy(dest_hbm.at[pl.ds(c0, per_w)], idx_v)

        def load(c, b):
            return pltpu.make_async_copy(rows_hbm.at[pl.ds((c0 + c) * ch, ch)], buf.at[b], lsem.at[b])

        def scatter(c, b, k):
            return pltpu.make_async_copy(buf.at[b], out_hbm.at[idx_v.at[c, k]], ssem.at[b])

        load(0, 0).start()

        @pl.loop(0, per_w, step=2)
        def _(c):
            for b in range(2):
                cc = c + b
                load(cc, b).wait()

                @pl.when(cc >= 1)
                def _():
                    for k in range(n_dst):
                        scatter(cc - 1, 1 - b, k).wait()

                @pl.when(cc + 1 < per_w)
                def _():
                    load(cc + 1, 1 - b).start()

                for k in range(n_dst):
                    scatter(cc, b, k).start()

        for k in range(n_dst):
            scatter(per_w - 1, 1, k).wait()

    mesh = plsc.VectorSubcoreMesh(core_axis_name="c", subcore_axis_name="s")
    return pl.kernel(
        body, out_type=jax.ShapeDtypeStruct((n_out, width), rows.dtype), mesh=mesh,
        scratch_types=[pltpu.VMEM((per_w, n_dst, ch), I32), pltpu.VMEM((2, ch, width), rows.dtype),
                       pltpu.SemaphoreType.DMA((2,)), pltpu.SemaphoreType.DMA((2,))],
        name="moe_dispatch_scatter",
    )(rows, dest)


def _sc_gather_rows(table, idx):
    n_chunks, ch = idx.shape
    width = table.shape[1]
    n_cores, n_workers = _sc_workers()
    per_w = n_chunks // n_workers
    assert n_chunks % n_workers == 0 and per_w % 2 == 0

    def body(table_hbm, idx_hbm, out_hbm, idx_v, buf, gsem, wsem):
        wid = lax.axis_index("s") * n_cores + lax.axis_index("c")
        c0 = wid * per_w
        pltpu.sync_copy(idx_hbm.at[pl.ds(c0, per_w)], idx_v)

        def gather(c, b):
            return pltpu.make_async_copy(table_hbm.at[idx_v.at[c]], buf.at[b], gsem.at[b])

        def write(c, b):
            return pltpu.make_async_copy(buf.at[b], out_hbm.at[pl.ds((c0 + c) * ch, ch)], wsem.at[b])

        gather(0, 0).start()

        @pl.loop(0, per_w, step=2)
        def _(c):
            for b in range(2):
                cc = c + b
                gather(cc, b).wait()

                @pl.when(cc >= 1)
                def _():
                    write(cc - 1, 1 - b).wait()

                @pl.when(cc + 1 < per_w)
                def _():
                    gather(cc + 1, 1 - b).start()

                write(cc, b).start()

        write(per_w - 1, 1).wait()

    mesh = plsc.VectorSubcoreMesh(core_axis_name="c", subcore_axis_name="s")
    return pl.kernel(
        body, out_type=jax.ShapeDtypeStruct((n_chunks * ch, width), table.dtype), mesh=mesh,
        scratch_types=[pltpu.VMEM((per_w, ch), I32), pltpu.VMEM((2, ch, width), table.dtype),
                       pltpu.SemaphoreType.DMA((2,)), pltpu.SemaphoreType.DMA((2,))],
        name="moe_combine_gather",
    )(table, idx)


def _expert_kernel(be_ref, ue_ref, nu_ref, xs_hbm, wg_hbm, wu_hbm, wd_hbm, y_hbm,
                   xbuf, ybuf, rg, ru, rd, wg_s, wu_s, wd_s, xsem, ysem, wsem):
    n_used = nu_ref[0]
    n_exp = nu_ref[1]
    half = D_MODEL // 2
    blk = MOE_BLOCK

    def rows(j):
        return pl.ds(pl.multiple_of(j * blk, blk), blk)

    def x_copy(j, p):
        return pltpu.make_async_copy(xs_hbm.at[rows(j)], xbuf.at[p], xsem.at[p])

    def y_copy(j, p):
        return pltpu.make_async_copy(ybuf.at[p], y_hbm.at[rows(j)], ysem.at[p])

    def w_copies(q, s):
        e = ue_ref[q]
        return (pltpu.make_async_copy(wg_hbm.at[e], rg.at[s], wsem.at[s, 0]),
                pltpu.make_async_copy(wu_hbm.at[e], ru.at[s], wsem.at[s, 1]),
                pltpu.make_async_copy(wd_hbm.at[e], rd.at[s], wsem.at[s, 2]))

    for q0 in range(W_SLOTS - 1):
        @pl.when(q0 < n_exp)
        def _():
            for c in w_copies(q0, q0):
                c.start()

    x_copy(0, 0).start()

    @pl.when(n_used > 1)
    def _():
        x_copy(1, 1).start()

    def block_step(j, p, q):
        is_new = (j == 0) | (be_ref[j] != be_ref[jnp.maximum(j - 1, 0)])
        q = q + is_new.astype(I32)

        @pl.when(is_new)
        def _():
            s = q % W_SLOTS
            for c in w_copies(q, s):
                c.wait()
            wg_s[...] = rg[s].astype(BF16)
            wu_s[...] = ru[s].astype(BF16)
            wd_s[...] = rd[s].astype(BF16)

            @pl.when(q + W_SLOTS - 1 < n_exp)
            def _():
                for c in w_copies(q + W_SLOTS - 1, (q + W_SLOTS - 1) % W_SLOTS):
                    c.start()

        x_copy(j, p).wait()

        @pl.when(j >= 2)
        def _():
            y_copy(j - 2, p).wait()

        lo, hi = _unpack_bf16_pairs(xbuf[p])
        xlo = lo.astype(BF16)
        xhi = hi.astype(BF16)
        g = (jnp.dot(xlo, wg_s[:half], preferred_element_type=F32)
             + jnp.dot(xhi, wg_s[half:], preferred_element_type=F32))
        u = (jnp.dot(xlo, wu_s[:half], preferred_element_type=F32)
             + jnp.dot(xhi, wu_s[half:], preferred_element_type=F32))
        hb = (g * jax.nn.sigmoid(g) * u).astype(BF16)
        ybuf[p] = _pack_bf16_pairs(jnp.dot(hb, wd_s[...], preferred_element_type=F32))
        y_copy(j, p).start()

        @pl.when(j + 2 < n_used)
        def _():
            x_copy(j + 2, p).start()

        return q

    def pair(m, q):
        q = block_step(2 * m, 0, q)
        return lax.cond(2 * m + 1 < n_used, lambda q: block_step(2 * m + 1, 1, q), lambda q: q, q)

    lax.fori_loop(0, (n_used + 1) // 2, pair, jnp.int32(-1))

    for back in (2, 1):
        b = n_used - back
        for p in range(2):
            @pl.when((b >= 0) & (b % 2 == p))
            def _():
                y_copy(b, p).wait()


def _experts(block_e, used_e, counts2, xs, wg, wu, wd):
    n_rows, half = xs.shape
    smem = pl.BlockSpec(memory_space=pltpu.SMEM)
    hbm = pl.BlockSpec(memory_space=pl.ANY)
    return pl.pallas_call(
        _expert_kernel,
        in_specs=[smem, smem, smem, hbm, hbm, hbm, hbm],
        out_specs=hbm,
        out_shape=jax.ShapeDtypeStruct((n_rows, half), I32),
        scratch_shapes=[pltpu.VMEM((2, MOE_BLOCK, half), I32), pltpu.VMEM((2, MOE_BLOCK, half), I32),
                        pltpu.VMEM((W_SLOTS, D_MODEL, EXPERT_DIM), F32),
                        pltpu.VMEM((W_SLOTS, D_MODEL, EXPERT_DIM), F32),
                        pltpu.VMEM((W_SLOTS, EXPERT_DIM, D_MODEL), F32),
                        pltpu.VMEM((D_MODEL, EXPERT_DIM), BF16),
                        pltpu.VMEM((D_MODEL, EXPERT_DIM), BF16),
                        pltpu.VMEM((EXPERT_DIM, D_MODEL), BF16),
                        pltpu.SemaphoreType.DMA((2,)), pltpu.SemaphoreType.DMA((2,)),
                        pltpu.SemaphoreType.DMA((W_SLOTS, 3))],
        compiler_params=pltpu.CompilerParams(vmem_limit_bytes=VMEM_LIMIT),
        name="moe_experts",
    )(block_e, used_e, counts2, xs, wg, wu, wd)


def _combine_kernel(yg_ref, gate_ref, x1_ref, swgu_ref, swd_ref, g2_ref, b2_ref, out_ref, *, alpha):
    tc = x1_ref.shape[0]
    x1 = x1_ref[...]
    gu = jnp.dot(x1.astype(BF16), swgu_ref[...], preferred_element_type=F32)
    g = gu[:, :EXPERT_DIM]
    u = gu[:, EXPERT_DIM:]
    shared = jnp.dot((g * jax.nn.sigmoid(g) * u).astype(BF16), swd_ref[...], preferred_element_type=F32)
    half = D_MODEL // 2
    acc_lo = jnp.zeros((tc, half), F32)
    acc_hi = jnp.zeros((tc, half), F32)
    gates = jnp.transpose(jnp.concatenate([gate_ref[...], jnp.zeros((tc - TOP_K, tc), F32)], axis=0))
    for k in range(TOP_K):
        lo, hi = _unpack_bf16_pairs(yg_ref[k])
        gk = gates[:, k:k + 1]
        acc_lo = acc_lo + gk * lo
        acc_hi = acc_hi + gk * hi
    routed = jnp.concatenate([acc_lo, acc_hi], axis=1)
    out_ref[...] = _layer_norm(alpha * x1 + (routed + shared), g2_ref[...], b2_ref[...])


def _combine(yg, gate, x1, swgu, swd, g2, b2, alpha):
    T, D = x1.shape
    tc = COMBINE_TILE
    consts = (swgu, swd, g2, b2)
    return pl.pallas_call(
        functools.partial(_combine_kernel, alpha=alpha),
        grid=(T // tc,),
        in_specs=[pl.BlockSpec((TOP_K, tc, D // 2), lambda i: (0, i, 0)),
                  pl.BlockSpec((TOP_K, tc), lambda i: (0, i)),
                  pl.BlockSpec((tc, D), lambda i: (i, 0))] + [_const_spec(c.shape) for c in consts],
        out_specs=pl.BlockSpec((tc, D), lambda i: (i, 0)),
        out_shape=jax.ShapeDtypeStruct((T, D), F32),
        compiler_params=pltpu.CompilerParams(dimension_semantics=("arbitrary",), vmem_limit_bytes=VMEM_LIMIT),
        name="moe_combine_norm",
    )(yg, gate, x1, *consts)


def _mixer(x, in_w, in_b, ng, nb, spatial_w, spatial_b, proj_a_w, proj_b_w, out_w, ln1_g, ln1_b,
           router_w, alpha):
    gw, aw, D = GMLP_WIDTH, ATTN_WIDTH, D_MODEL
    w = in_w.astype(BF16)
    q0 = 2 * gw
    wuv, buv = w[:, :q0], in_b[None, :q0]
    watt = jnp.stack([jnp.concatenate([w[:, q0 + s * aw + p * GROUP_WIDTH:q0 + s * aw + (p + 1) * GROUP_WIDTH]
                                       for s in range(3)], axis=1) for p in range(len(DILATED_PATTERNS))])
    batt = jnp.stack([jnp.concatenate([in_b[q0 + s * aw + p * GROUP_WIDTH:q0 + s * aw + (p + 1) * GROUP_WIDTH]
                                       for s in range(3)])[None] for p in range(len(DILATED_PATTERNS))])
    g0 = q0 + 3 * aw
    wg, bg = w[:, g0:], in_b[None, g0:]
    sb = jnp.repeat(spatial_b.T, gw // GMLP_GROUPS, axis=1)
    ya, gb, a1, a4, a16 = _input_projection(
        x, wuv, buv, watt, batt, wg, bg, ng[None], nb[None], spatial_w, sb, proj_a_w.astype(BF16))
    attn_outs = [_dilated_attention(a, p, d) for p, (a, (_, d)) in enumerate(zip((a1, a4, a16), DILATED_PATTERNS))]
    rw_t = router_w.T
    rwh = rw_t.astype(BF16)
    rwl = (rw_t - rwh.astype(F32)).astype(BF16)
    return _merge_and_norm(attn_outs, ya, gb, x, proj_b_w.astype(BF16), out_w.astype(BF16),
                           ln1_g[None], ln1_b[None], rwh, rwl, alpha)


def _moe(x1, x1p, scores_t, router_bias, w_gate, w_up, w_down, sw_gate, sw_up, sw_down, ln2_g, ln2_b, alpha):
    T = x1.shape[0]
    A = T * TOP_K
    n_blocks = (A + N_EXPERTS * (MOE_BLOCK - 1)) // MOE_BLOCK
    eidx_t, rank_t, gate_t, counts = _route(scores_t, router_bias[:, None])
    counts = counts[:, 0].astype(I32)
    padded = (counts + MOE_BLOCK - 1) // MOE_BLOCK * MOE_BLOCK
    pend = jnp.cumsum(padded).astype(I32)
    pstart = pend - padded
    block_starts = jnp.arange(n_blocks, dtype=I32) * MOE_BLOCK
    block_e = jnp.minimum(jnp.sum((pend[None, :] <= block_starts[:, None]).astype(I32), axis=1), N_EXPERTS - 1)
    used = counts > 0
    used_e = jnp.argsort(jnp.logical_not(used), stable=True).astype(I32)
    counts2 = jnp.stack([pend[-1] // MOE_BLOCK, jnp.sum(used.astype(I32))]).astype(I32)
    dest_t = _dest_rows(eidx_t, rank_t, pstart)
    xs = _sc_scatter_rows(x1p, dest_t.reshape(TOP_K, T // SC_CHUNK, SC_CHUNK).transpose(1, 0, 2),
                          n_blocks * MOE_BLOCK)
    y_rows = _experts(block_e, used_e, counts2, xs, w_gate, w_up, w_down)
    yg = _sc_gather_rows(y_rows, dest_t.reshape(A // SC_CHUNK, SC_CHUNK))
    swgu = jnp.concatenate([sw_gate, sw_up], axis=1).astype(BF16)
    return _combine(yg.reshape(TOP_K, T, D_MODEL // 2), gate_t, x1, swgu, sw_down.astype(BF16),
                    ln2_g[None], ln2_b[None], alpha)


def kernel(x, in_w, in_b, gmlp_norm_g, gmlp_norm_b, spatial_w, spatial_b, proj_a_w, proj_b_w, out_w,
           ln1_g, ln1_b, router_w, router_bias, expert_w_gate, expert_w_up, expert_w_down,
           shared_w_gate, shared_w_up, shared_w_down, ln2_g, ln2_b):
    B, S, D = x.shape
    depth = in_w.shape[0]
    alpha = np.float32((2.0 * depth) ** 0.25)
    for l in range(depth):
        x1, x1p, scores_t = _mixer(x, in_w[l], in_b[l], gmlp_norm_g[l], gmlp_norm_b[l], spatial_w[l],
                                 spatial_b[l], proj_a_w[l], proj_b_w[l], out_w[l], ln1_g[l], ln1_b[l],
                                 router_w[l], alpha)
        out = _moe(x1.reshape(B * S, D), x1p.reshape(B * S, D // 2), scores_t,
                   router_bias[l], expert_w_gate[l], expert_w_up[l], expert_w_down[l],
                   shared_w_gate[l], shared_w_up[l], shared_w_down[l], ln2_g[l], ln2_b[l], alpha)
        x = out.reshape(B, S, D)
    return x
```

```python
import functools
import math

import numpy as np
import jax
import jax.numpy as jnp
from jax import lax
from jax.experimental import pallas as pl
from jax.experimental.pallas import tpu as pltpu
from jax.experimental.pallas import tpu_sc as plsc

F32 = jnp.float32
BF16 = jnp.bfloat16
U32 = jnp.uint32
I32 = jnp.int32

D_MODEL = 1024
GMLP_WIDTH = 1024
GMLP_GROUPS = 8
GMLP_CHUNK = 128
HEAD_DIM = 64
DILATED_PATTERNS = ((128, 1), (512, 4), (2048, 16))
HEADS_PER_GROUP = 4
GROUP_WIDTH = HEADS_PER_GROUP * HEAD_DIM
ATTN_WIDTH = GROUP_WIDTH * len(DILATED_PATTERNS)
ATTN_BLOCK = 128
N_EXPERTS = 256
TOP_K = 8
TOP_K_SHIFT = 3
N_EXPERT_GROUPS = 8
TOPK_GROUPS = 4
EXPERT_DIM = 256
ROUTED_SCALE = 2.5
LN_EPS = 1e-5
LANES = 128
MASKED_SCORE = -1e30

PROJ_TILE = 512
ATTN_QBLOCKS = 4
MERGE_TILE = 512
ROUTE_TILE = 512
DEST_TILE = 2048
MOE_BLOCK = 256
SC_CHUNK = 64
X_SLOTS = 4
W_SLOTS = 3
COMBINE_TILE = 256
VMEM_LIMIT = 56 * 1024 * 1024


def _layer_norm(y, g, b):
    mu = jnp.mean(y, axis=-1, keepdims=True)
    yc = y - mu
    var = jnp.mean(yc * yc, axis=-1, keepdims=True)
    return yc * lax.rsqrt(var + LN_EPS) * g + b


def _gelu(x):
    return 0.5 * x * (1.0 + lax.erf(x * np.float32(math.sqrt(0.5))))


def _pack_bf16_pairs(x):
    w = x.shape[1] // 2
    bits = pltpu.bitcast(x.astype(BF16).astype(F32), U32)
    return pltpu.bitcast((bits[:, :w] >> 16) | (bits[:, w:] & jnp.uint32(0xFFFF0000)), I32)


def _unpack_bf16_pairs(words):
    w = pltpu.bitcast(words, U32)
    lo = pltpu.bitcast(w << 16, F32)
    hi = pltpu.bitcast(w & jnp.uint32(0xFFFF0000), F32)
    return lo, hi


def _const_spec(shape):
    nd = len(shape)
    return pl.BlockSpec(shape, lambda *_: (0,) * nd)


def _proj_kernel(x_ref, wuv_ref, buv_ref, watt_ref, batt_ref, wg_ref, bg_ref, ng_ref, nb_ref,
                 sw_ref, sb_ref, pa_ref, ya_ref, gb_ref, a1_ref, a4_ref, a16_ref, xc_ref):
    tm = x_ref.shape[1]
    gw = GMLP_WIDTH
    xb = x_ref[0].astype(BF16)

    def proj(w, b):
        return jnp.dot(xb, w, preferred_element_type=F32) + b

    u = _gelu(proj(wuv_ref[:, :gw], buv_ref[:, :gw]))
    v = _gelu(proj(wuv_ref[:, gw:], buv_ref[:, gw:]))
    v = _layer_norm(v, ng_ref[...], nb_ref[...]).astype(BF16)

    cw = gw // GMLP_GROUPS
    row = lax.broadcasted_iota(I32, (GMLP_CHUNK, GMLP_CHUNK), 0)
    col = lax.broadcasted_iota(I32, (GMLP_CHUNK, GMLP_CHUNK), 1)
    ws = [jnp.where(row >= col, sw_ref[g], 0.0).astype(BF16) for g in range(GMLP_GROUPS)]
    chunks = []
    for c in range(tm // GMLP_CHUNK):
        vc = v[c * GMLP_CHUNK:(c + 1) * GMLP_CHUNK]
        cols = [jnp.dot(ws[g], vc[:, g * cw:(g + 1) * cw], preferred_element_type=F32)
                for g in range(GMLP_GROUPS)]
        chunks.append(jnp.concatenate(cols, axis=1) + sb_ref[...])
    vmix = jnp.concatenate(chunks, axis=0)
    ya = jnp.dot((u * vmix).astype(BF16), pa_ref[...], preferred_element_type=F32)
    ga = jax.nn.sigmoid(proj(wg_ref[:, :D_MODEL], bg_ref[:, :D_MODEL]))
    ya_ref[0] = (ga * ya).astype(BF16)
    gb_ref[0] = jax.nn.sigmoid(proj(wg_ref[:, D_MODEL:], bg_ref[:, D_MODEL:])).astype(BF16)

    n_chunks = x_ref.shape[2] // LANES
    for c in range(n_chunks):
        xc_ref[c] = x_ref[0, :, c * LANES:(c + 1) * LANES]
    for p, (a_ref, (_, d)) in enumerate(zip((a1_ref, a4_ref, a16_ref), DILATED_PATTERNS)):
        n = tm // d
        if d == 1:
            xp = xb
        else:
            xp = jnp.concatenate(
                [jnp.concatenate([xc_ref[c, pl.ds(r, n, stride=d), :] for c in range(n_chunks)], axis=1)
                 for r in range(d)], axis=0).astype(BF16)
        h = (jnp.dot(xp, watt_ref[p], preferred_element_type=F32) + batt_ref[p]).astype(BF16)
        for r in range(d):
            a_ref[0, r] = h[r * n:(r + 1) * n]


def _input_projection(x, wuv, buv, watt, batt, wg, bg, ng, nb, sw, sb, pa):
    B, S, D = x.shape
    tm = PROJ_TILE
    grid = (B, S // tm)
    out_shape = [jax.ShapeDtypeStruct((B, S, D), BF16), jax.ShapeDtypeStruct((B, S, D), BF16)]
    out_specs = [pl.BlockSpec((1, tm, D), lambda b, t: (b, t, 0)),
                 pl.BlockSpec((1, tm, D), lambda b, t: (b, t, 0))]
    for _, d in DILATED_PATTERNS:
        out_shape.append(jax.ShapeDtypeStruct((B, d, S // d, ATTN_WIDTH), BF16))
        out_specs.append(pl.BlockSpec((1, d, tm // d, ATTN_WIDTH), lambda b, t: (b, 0, t, 0)))
    consts = (wuv, buv, watt, batt, wg, bg, ng, nb, sw, sb, pa)
    return pl.pallas_call(
        _proj_kernel,
        grid=grid,
        in_specs=[pl.BlockSpec((1, tm, D), lambda b, t: (b, t, 0))] + [_const_spec(c.shape) for c in consts],
        out_specs=out_specs,
        out_shape=out_shape,
        scratch_shapes=[pltpu.VMEM((D // LANES, tm, LANES), F32)],
        compiler_params=pltpu.CompilerParams(
            dimension_semantics=("arbitrary", "arbitrary"), vmem_limit_bytes=VMEM_LIMIT),
        name="input_projection",
    )(x, *consts)


def _attn_kernel(qkv_ref, bias_ref, o_ref, lse_ref, *, qblocks):
    nq = pl.program_id(2)
    gwid = GROUP_WIDTH
    blk = ATTN_BLOCK
    lane = lax.broadcasted_iota(I32, (1, gwid), 1)
    head_masks = [(lane >= h * HEAD_DIM) & (lane < (h + 1) * HEAD_DIM) for h in range(HEADS_PER_GROUP)]
    q_scales = [jnp.where(m, np.float32(HEAD_DIM ** -0.5), 0.0).astype(BF16) for m in head_masks]

    def rows_of(j):
        n = nq * qblocks + j
        return n, pl.multiple_of(n * blk, blk), pl.multiple_of(jnp.maximum(n - 1, 0) * blk, blk)

    def scores(j):
        n, q0, p0 = rows_of(j)
        q = qkv_ref[0, 0, pl.ds(q0, blk), 0:gwid]
        kk = jnp.concatenate([qkv_ref[0, 0, pl.ds(p0, blk), gwid:2 * gwid],
                              qkv_ref[0, 0, pl.ds(q0, blk), gwid:2 * gwid]], axis=0)
        qs = jnp.concatenate([q * s for s in q_scales], axis=0)
        s = lax.dot_general(qs, kk, (((1,), (1,)), ((), ())), preferred_element_type=F32)
        return s + bias_ref[jnp.where(n == 0, 1, 0)]

    def finish(j, s):
        _, q0, p0 = rows_of(j)
        vv = jnp.concatenate([qkv_ref[0, 0, pl.ds(p0, blk), 2 * gwid:3 * gwid],
                              qkv_ref[0, 0, pl.ds(q0, blk), 2 * gwid:3 * gwid]], axis=0)
        m = jnp.max(s, axis=1, keepdims=True)
        p = jnp.exp(s - m)
        den = jnp.sum(p, axis=1, keepdims=True)
        pv = jnp.dot(p.astype(BF16), vv, preferred_element_type=F32)
        on = pv / den
        lse = m + jnp.log(den)
        o = jnp.zeros((blk, gwid), F32)
        l = jnp.zeros((blk, gwid), F32)
        for h in range(HEADS_PER_GROUP):
            o = jnp.where(head_masks[h], on[h * blk:(h + 1) * blk], o)
            l = jnp.where(head_masks[h], lse[h * blk:(h + 1) * blk], l)
        o_ref[0, 0, j * blk:(j + 1) * blk, :] = o.astype(BF16)
        lse_ref[0, 0, j * blk:(j + 1) * blk, :] = l

    s_next = scores(0)
    for j in range(qblocks):
        s_cur = s_next
        if j + 1 < qblocks:
            s_next = scores(j + 1)
        finish(j, s_cur)


def _alibi_slopes(n):
    def pow2_slopes(m):
        start = 2.0 ** (-8.0 / m)
        return [start ** (i + 1) for i in range(m)]
    p = 2 ** int(math.floor(math.log2(n)))
    s = pow2_slopes(p)
    if p < n:
        s = s + pow2_slopes(2 * p)[0::2][: n - p]
    return np.array(sorted(s, reverse=True), dtype=np.float32)


def _attn_bias_tables(group, dilation):
    blk = ATTN_BLOCK
    slopes = _alibi_slopes(HEADS_PER_GROUP * len(DILATED_PATTERNS))
    slopes = slopes[group * HEADS_PER_GROUP:(group + 1) * HEADS_PER_GROUP]
    qi = np.arange(blk)[:, None]
    ki = np.arange(2 * blk)[None, :]
    delta = blk + qi - ki
    band = (delta >= 0) & (delta <= blk)
    bias = -slopes[:, None, None] * (delta * dilation).astype(np.float32)[None]
    full = np.where(band[None], bias, np.float32(MASKED_SCORE)).astype(np.float32)
    first = np.where((ki >= blk)[None], full, np.float32(MASKED_SCORE)).astype(np.float32)
    return full.reshape(HEADS_PER_GROUP * blk, 2 * blk), first.reshape(HEADS_PER_GROUP * blk, 2 * blk)


def _dilated_attention(qkv, group, dilation):
    B, d, sd, _ = qkv.shape
    qblocks = min(ATTN_QBLOCKS, sd // ATTN_BLOCK)
    rows = qblocks * ATTN_BLOCK
    bias = np.stack(_attn_bias_tables(group, dilation))
    grid = (B, d, sd // rows)
    out_spec = pl.BlockSpec((1, 1, rows, GROUP_WIDTH), lambda b, r, n: (b, r, n, 0))
    return pl.pallas_call(
        functools.partial(_attn_kernel, qblocks=qblocks),
        grid=grid,
        in_specs=[pl.BlockSpec((1, 1, sd, ATTN_WIDTH), lambda b, r, n: (b, r, 0, 0)),
                  _const_spec(bias.shape)],
        out_specs=[out_spec, out_spec],
        out_shape=[jax.ShapeDtypeStruct((B, d, sd, GROUP_WIDTH), BF16),
                   jax.ShapeDtypeStruct((B, d, sd, GROUP_WIDTH), F32)],
        compiler_params=pltpu.CompilerParams(
            dimension_semantics=("arbitrary", "arbitrary", "arbitrary"), vmem_limit_bytes=VMEM_LIMIT),
        name=f"dilated_attention_d{dilation}",
    )(qkv, jnp.asarray(bias))


def _merge_kernel(o1_ref, l1_ref, o4_ref, l4_ref, o16_ref, l16_ref, ya_ref, gb_ref, x_ref,
                  pb_ref, ow_ref, g1_ref, b1_ref, rwh_ref, rwl_ref,
                  x1_ref, x1p_ref, sc_ref, so4, sl4, so16, sl16, *, alpha):
    tm = x_ref.shape[1]
    n_chunks = GROUP_WIDTH // LANES
    for (o_ref, l_ref, so, sl, d) in ((o4_ref, l4_ref, so4, sl4, 4), (o16_ref, l16_ref, so16, sl16, 16)):
        n = tm // d
        for r in range(d):
            o_r = o_ref[0, r].astype(F32)
            l_r = l_ref[0, r]
            for c in range(n_chunks):
                so[c, pl.ds(r, n, stride=d), :] = o_r[:, c * LANES:(c + 1) * LANES]
                sl[c, pl.ds(r, n, stride=d), :] = l_r[:, c * LANES:(c + 1) * LANES]

    def natural(s):
        return jnp.concatenate([s[c] for c in range(n_chunks)], axis=1)

    l1 = l1_ref[0, 0]
    l4 = natural(sl4)
    l16 = natural(sl16)
    lmax = jnp.maximum(jnp.maximum(l1, l4), l16)
    e1 = jnp.exp(l1 - lmax)
    e4 = jnp.exp(l4 - lmax)
    e16 = jnp.exp(l16 - lmax)
    yb = (e1 * o1_ref[0, 0].astype(F32) + e4 * natural(so4) + e16 * natural(so16)) / (e1 + e4 + e16)
    ybp = jnp.dot(yb.astype(BF16), pb_ref[...], preferred_element_type=F32)
    merged = ya_ref[0].astype(F32) + gb_ref[0].astype(F32) * ybp
    mix = jnp.dot(merged.astype(BF16), ow_ref[...], preferred_element_type=F32)
    x1 = _layer_norm(alpha * x_ref[0] + mix, g1_ref[...], b1_ref[...])
    x1_ref[0] = x1
    x1p_ref[0] = _pack_bf16_pairs(x1)
    hi = x1.astype(BF16)
    lo = (x1 - hi.astype(F32)).astype(BF16)
    def logits_t(w_ref, xt):
        return lax.dot_general(w_ref[...], xt, (((1,), (1,)), ((), ())), preferred_element_type=F32)
    sc_ref[...] = jax.nn.sigmoid(logits_t(rwh_ref, hi) + logits_t(rwh_ref, lo) + logits_t(rwl_ref, hi))


def _merge_and_norm(attn_outs, ya, gb, x, pb, ow, g1, b1, rwh, rwl, alpha):
    B, S, D = x.shape
    tm = MERGE_TILE
    in_specs = []
    args = []
    for (o, l), (_, d) in zip(attn_outs, DILATED_PATTERNS):
        spec = pl.BlockSpec((1, d, tm // d, GROUP_WIDTH), lambda b, t: (b, 0, t, 0))
        in_specs += [spec, spec]
        args += [o, l]
    tok_spec = pl.BlockSpec((1, tm, D), lambda b, t: (b, t, 0))
    in_specs += [tok_spec, tok_spec, tok_spec]
    args += [ya, gb, x]
    consts = (pb, ow, g1, b1, rwh, rwl)
    in_specs += [_const_spec(c.shape) for c in consts]
    return pl.pallas_call(
        functools.partial(_merge_kernel, alpha=alpha),
        grid=(B, S // tm),
        in_specs=in_specs,
        out_specs=[tok_spec,
                   pl.BlockSpec((1, tm, D // 2), lambda b, t: (b, t, 0)),
                   pl.BlockSpec((N_EXPERTS, tm), lambda b, t: (0, b * (S // tm) + t))],
        out_shape=[jax.ShapeDtypeStruct((B, S, D), F32),
                   jax.ShapeDtypeStruct((B, S, D // 2), I32),
                   jax.ShapeDtypeStruct((N_EXPERTS, B * S), F32)],
        scratch_shapes=[pltpu.VMEM((GROUP_WIDTH // LANES, tm, LANES), F32) for _ in range(4)],
        compiler_params=pltpu.CompilerParams(
            dimension_semantics=("arbitrary", "arbitrary"), vmem_limit_bytes=VMEM_LIMIT),
        name="merge_norm_router",
    )(*args, *consts)


def _sortable_key(x):
    bits = pltpu.bitcast(x, I32)
    return jnp.where(bits < 0, bits ^ jnp.int32(0x7FFFFFFF), bits)


def _route_kernel(sc_ref, bias_ref, before_ref, eidx_ref, rank_ref, gate_ref, cnt_ref, carry_ref):
    ne, tm = sc_ref.shape
    gsize = ne // N_EXPERT_GROUPS
    neg_inf = np.float32(-np.inf)
    removed = jnp.int32(-2 ** 31)

    @pl.when(pl.program_id(0) == 0)
    def _():
        carry_ref[...] = jnp.zeros_like(carry_ref)

    scores = sc_ref[...]
    biased = scores + bias_ref[...]

    gsum = []
    for g in range(N_EXPERT_GROUPS):
        v = biased[g * gsize:(g + 1) * gsize]
        m1 = jnp.max(v, axis=0, keepdims=True)
        n1 = jnp.sum(jnp.where(v == m1, 1.0, 0.0), axis=0, keepdims=True)
        m2 = jnp.max(jnp.where(v < m1, v, neg_inf), axis=0, keepdims=True)
        gsum.append(m1 + jnp.where(n1 >= 2.0, m1, m2))
    gkey = _sortable_key(jnp.concatenate(gsum, axis=0))

    def pick_first_max(keys, ids, n_ids):
        m = jnp.max(keys, axis=0, keepdims=True)
        idx = jnp.min(jnp.where(keys == m, ids, n_ids), axis=0, keepdims=True)
        hit = ids == idx
        return idx, hit, jnp.where(hit, removed, keys)

    gid = lax.broadcasted_iota(I32, (N_EXPERT_GROUPS, tm), 0)
    for _ in range(TOPK_GROUPS):
        _, _, gkey = pick_first_max(gkey, gid, N_EXPERT_GROUPS)
    group_on = gkey == removed

    masked = jnp.concatenate(
        [jnp.where(group_on[g:g + 1], biased[g * gsize:(g + 1) * gsize], neg_inf)
         for g in range(N_EXPERT_GROUPS)], axis=0)
    keys = _sortable_key(masked)
    eid = lax.broadcasted_iota(I32, (ne, tm), 0)
    picks = []
    for _ in range(TOP_K):
        idx, _, keys = pick_first_max(keys, eid, ne)
        picks.append(idx)

    sel = jnp.where(keys == removed, 1.0, 0.0)
    ranks = jnp.dot(sel.astype(BF16), before_ref[...], preferred_element_type=F32) + carry_ref[...]
    carry_ref[...] = carry_ref[...] + jnp.sum(sel, axis=1, keepdims=True)
    cnt_ref[...] = carry_ref[...]

    s_k, r_k = [], []
    for idx in picks:
        hit = eid == idx
        s_k.append(jnp.sum(jnp.where(hit, scores, 0.0), axis=0, keepdims=True))
        r_k.append(jnp.sum(jnp.where(hit, ranks, 0.0), axis=0, keepdims=True))
    total = s_k[0]
    for s in s_k[1:]:
        total = total + s
    eidx_ref[...] = jnp.concatenate(picks, axis=0)
    rank_ref[...] = jnp.concatenate(r_k, axis=0).astype(I32)
    gate_ref[...] = jnp.concatenate([s / total * np.float32(ROUTED_SCALE) for s in s_k], axis=0)


def _route(scores_t, bias):
    ne, T = scores_t.shape
    tm = ROUTE_TILE
    before = jnp.asarray(np.triu(np.ones((tm, tm), np.float32), k=1), BF16)
    out_spec = pl.BlockSpec((TOP_K, tm), lambda i: (0, i))
    return pl.pallas_call(
        _route_kernel,
        grid=(T // tm,),
        in_specs=[pl.BlockSpec((ne, tm), lambda i: (0, i)), _const_spec(bias.shape), _const_spec(before.shape)],
        out_specs=[out_spec, out_spec, out_spec, _const_spec((ne, 1))],
        out_shape=[jax.ShapeDtypeStruct((TOP_K, T), I32), jax.ShapeDtypeStruct((TOP_K, T), I32),
                   jax.ShapeDtypeStruct((TOP_K, T), F32), jax.ShapeDtypeStruct((ne, 1), F32)],
        scratch_shapes=[pltpu.VMEM((ne, 1), F32)],
        compiler_params=pltpu.CompilerParams(dimension_semantics=("arbitrary",), vmem_limit_bytes=VMEM_LIMIT),
        name="route_topk",
    )(scores_t, bias, before)


def _dest_kernel(pstart_ref, eidx_ref, rank_ref, out_ref):
    eidx = eidx_ref[...]
    start = jnp.zeros(eidx.shape, I32)
    for e in range(N_EXPERTS):
        start = jnp.where(eidx == e, pstart_ref[e], start)
    out_ref[...] = start + rank_ref[...]


def _dest_rows(eidx_t, rank_t, pstart):
    T = eidx_t.shape[1]
    tm = DEST_TILE
    tok_spec = pl.BlockSpec((TOP_K, tm), lambda i: (0, i))
    return pl.pallas_call(
        _dest_kernel,
        grid=(T // tm,),
        in_specs=[pl.BlockSpec(memory_space=pltpu.SMEM), tok_spec, tok_spec],
        out_specs=tok_spec,
        out_shape=jax.ShapeDtypeStruct((TOP_K, T), I32),
        compiler_params=pltpu.CompilerParams(dimension_semantics=("arbitrary",)),
        name="moe_dest_rows",
    )(pstart, eidx_t, rank_t)


def _sc_workers():
    info = plsc.get_sparse_core_info()
    return info.num_cores, info.num_cores * info.num_subcores


def _sc_scatter_rows(rows, dest, n_out):
    n_chunks, n_dst, ch = dest.shape
    width = rows.shape[1]
    n_cores, n_workers = _sc_workers()
    per_w = n_chunks // n_workers
    assert n_chunks % n_workers == 0 and per_w % 2 == 0 and rows.shape[0] == n_chunks * ch

    def body(rows_hbm, dest_hbm, out_hbm, idx_v, buf, lsem, ssem):
        wid = lax.axis_index("s") * n_cores + lax.axis_index("c")
        c0 = wid * per_w
        pltpu.sync_copy(dest_hbm.at[pl.ds(c0, per_w)], idx_v)

        def load(c, b):
            return pltpu.make_async_copy(rows_hbm.at[pl.ds((c0 + c) * ch, ch)], buf.at[b], lsem.at[b])

        def scatter(c, b, k):
            return pltpu.make_async_copy(buf.at[b], out_hbm.at[idx_v.at[c, k]], ssem.at[b])

        load(0, 0).start()

        @pl.loop(0, per_w, step=2)
        def _(c):
            for b in range(2):
                cc = c + b
                load(cc, b).wait()

                @pl.when(cc >= 1)
                def _():
                    for k in range(n_dst):
                        scatter(cc - 1, 1 - b, k).wait()

                @pl.when(cc + 1 < per_w)
                def _():
                    load(cc + 1, 1 - b).start()

                for k in range(n_dst):
                    scatter(cc, b, k).start()

        for k in range(n_dst):
            scatter(per_w - 1, 1, k).wait()

    mesh = plsc.VectorSubcoreMesh(core_axis_name="c", subcore_axis_name="s")
    return pl.kernel(
        body, out_type=jax.ShapeDtypeStruct((n_out, width), rows.dtype), mesh=mesh,
        scratch_types=[pltpu.VMEM((per_w, n_dst, ch), I32), pltpu.VMEM((2, ch, width), rows.dtype),
                       pltpu.SemaphoreType.DMA((2,)), pltpu.SemaphoreType.DMA((2,))],
        name="moe_dispatch_scatter",
    )(rows, dest)


def _sc_gather_rows(table, idx):
    n_chunks, ch = idx.shape
    width = table.shape[1]
    n_cores, n_workers = _sc_workers()
    per_w = n_chunks // n_workers
    assert n_chunks % n_workers == 0 and per_w % 2 == 0

    def body(table_hbm, idx_hbm, out_hbm, idx_v, buf, gsem, wsem):
        wid = lax.axis_index("s") * n_cores + lax.axis_index("c")
        c0 = wid * per_w
        pltpu.sync_copy(idx_hbm.at[pl.ds(c0, per_w)], idx_v)

        def gather(c, b):
            return pltpu.make_async_copy(table_hbm.at[idx_v.at[c]], buf.at[b], gsem.at[b])

        def write(c, b):
            return pltpu.make_async_copy(buf.at[b], out_hbm.at[pl.ds((c0 + c) * ch, ch)], wsem.at[b])

        gather(0, 0).start()

        @pl.loop(0, per_w, step=2)
        def _(c):
            for b in range(2):
                cc = c + b
                gather(cc, b).wait()

                @pl.when(cc >= 1)
                def _():
                    write(cc - 1, 1 - b).wait()

                @pl.when(cc + 1 < per_w)
                def _():
                    gather(cc + 1, 1 - b).start()

                write(cc, b).start()

        write(per_w - 1, 1).wait()

    mesh = plsc.VectorSubcoreMesh(core_axis_name="c", subcore_axis_name="s")
    return pl.kernel(
        body, out_type=jax.ShapeDtypeStruct((n_chunks * ch, width), table.dtype), mesh=mesh,
        scratch_types=[pltpu.VMEM((per_w, ch), I32), pltpu.VMEM((2, ch, width), table.dtype),
                       pltpu.SemaphoreType.DMA((2,)), pltpu.SemaphoreType.DMA((2,))],
        name="moe_combine_gather",
    )(table, idx)


def _expert_kernel(be_ref, ue_ref, nu_ref, xs_hbm, wg_hbm, wu_hbm, wd_hbm, y_hbm,
                   xbuf, ybuf, rg, ru, rd, wg_s, wu_s, wd_s, xsem, ysem, wsem):
    n_used = nu_ref[0]
    n_exp = nu_ref[1]
    half = D_MODEL // 2
    blk = MOE_BLOCK

    def rows(j):
        return pl.ds(pl.multiple_of(j * blk, blk), blk)

    def x_copy(j, p):
        return pltpu.make_async_copy(xs_hbm.at[rows(j)], xbuf.at[p], xsem.at[p])

    def y_copy(j, p):
        return pltpu.make_async_copy(ybuf.at[p], y_hbm.at[rows(j)], ysem.at[p])

    def w_copies(q, s):
        e = ue_ref[q]
        return (pltpu.make_async_copy(wg_hbm.at[e], rg.at[s], wsem.at[s, 0]),
                pltpu.make_async_copy(wu_hbm.at[e], ru.at[s], wsem.at[s, 1]),
                pltpu.make_async_copy(wd_hbm.at[e], rd.at[s], wsem.at[s, 2]))

    for q0 in range(W_SLOTS - 1):
        @pl.when(q0 < n_exp)
        def _():
            for c in w_copies(q0, q0):
                c.start()

    for j0 in range(X_SLOTS):
        @pl.when(j0 < n_used)
        def _():
            x_copy(j0, j0).start()

    def block_step(j, p, q):
        is_new = (j == 0) | (be_ref[j] != be_ref[jnp.maximum(j - 1, 0)])
        q = q + is_new.astype(I32)

        @pl.when(is_new)
        def _():
            s = q % W_SLOTS
            for c in w_copies(q, s):
                c.wait()
            wg_s[...] = rg[s].astype(BF16)
            wu_s[...] = ru[s].astype(BF16)
            wd_s[...] = rd[s].astype(BF16)

            @pl.when(q + W_SLOTS - 1 < n_exp)
            def _():
                for c in w_copies(q + W_SLOTS - 1, (q + W_SLOTS - 1) % W_SLOTS):
                    c.start()

        x_copy(j, p).wait()

        @pl.when(j >= X_SLOTS)
        def _():
            y_copy(j - X_SLOTS, p).wait()

        lo, hi = _unpack_bf16_pairs(xbuf[p])
        xlo = lo.astype(BF16)
        xhi = hi.astype(BF16)
        g = (jnp.dot(xlo, wg_s[:half], preferred_element_type=F32)
             + jnp.dot(xhi, wg_s[half:], preferred_element_type=F32))
        u = (jnp.dot(xlo, wu_s[:half], preferred_element_type=F32)
             + jnp.dot(xhi, wu_s[half:], preferred_element_type=F32))
        hb = (g * jax.nn.sigmoid(g) * u).astype(BF16)
        ybuf[p] = _pack_bf16_pairs(jnp.dot(hb, wd_s[...], preferred_element_type=F32))
        y_copy(j, p).start()

        @pl.when(j + X_SLOTS < n_used)
        def _():
            x_copy(j + X_SLOTS, p).start()

        return q

    def group(m, q):
        q = block_step(X_SLOTS * m, 0, q)
        for p in range(1, X_SLOTS):
            j = X_SLOTS * m + p
            q = lax.cond(j < n_used, functools.partial(block_step, j, p), lambda q: q, q)
        return q

    lax.fori_loop(0, (n_used + X_SLOTS - 1) // X_SLOTS, group, jnp.int32(-1))

    for back in range(X_SLOTS, 0, -1):
        b = n_used - back
        for p in range(X_SLOTS):
            @pl.when((b >= 0) & (b % X_SLOTS == p))
            def _():
                y_copy(b, p).wait()


def _experts(block_e, used_e, counts2, xs, wg, wu, wd):
    n_rows, half = xs.shape
    smem = pl.BlockSpec(memory_space=pltpu.SMEM)
    hbm = pl.BlockSpec(memory_space=pl.ANY)
    return pl.pallas_call(
        _expert_kernel,
        in_specs=[smem, smem, smem, hbm, hbm, hbm, hbm],
        out_specs=hbm,
        out_shape=jax.ShapeDtypeStruct((n_rows, half), I32),
        scratch_shapes=[pltpu.VMEM((X_SLOTS, MOE_BLOCK, half), I32), pltpu.VMEM((X_SLOTS, MOE_BLOCK, half), I32),
                        pltpu.VMEM((W_SLOTS, D_MODEL, EXPERT_DIM), F32),
                        pltpu.VMEM((W_SLOTS, D_MODEL, EXPERT_DIM), F32),
                        pltpu.VMEM((W_SLOTS, EXPERT_DIM, D_MODEL), F32),
                        pltpu.VMEM((D_MODEL, EXPERT_DIM), BF16),
                        pltpu.VMEM((D_MODEL, EXPERT_DIM), BF16),
                        pltpu.VMEM((EXPERT_DIM, D_MODEL), BF16),
                        pltpu.SemaphoreType.DMA((X_SLOTS,)), pltpu.SemaphoreType.DMA((X_SLOTS,)),
                        pltpu.SemaphoreType.DMA((W_SLOTS, 3))],
        compiler_params=pltpu.CompilerParams(vmem_limit_bytes=VMEM_LIMIT),
        name="moe_experts",
    )(block_e, used_e, counts2, xs, wg, wu, wd)


def _combine_kernel(yg_ref, gate_ref, x1_ref, swgu_ref, swd_ref, g2_ref, b2_ref, out_ref, *, alpha):
    tc = x1_ref.shape[0]
    x1 = x1_ref[...]
    gu = jnp.dot(x1.astype(BF16), swgu_ref[...], preferred_element_type=F32)
    g = gu[:, :EXPERT_DIM]
    u = gu[:, EXPERT_DIM:]
    shared = jnp.dot((g * jax.nn.sigmoid(g) * u).astype(BF16), swd_ref[...], preferred_element_type=F32)
    half = D_MODEL // 2
    acc_lo = jnp.zeros((tc, half), F32)
    acc_hi = jnp.zeros((tc, half), F32)
    gates = jnp.transpose(jnp.concatenate([gate_ref[...], jnp.zeros((tc - TOP_K, tc), F32)], axis=0))
    for k in range(TOP_K):
        lo, hi = _unpack_bf16_pairs(yg_ref[k])
        gk = gates[:, k:k + 1]
        acc_lo = acc_lo + gk * lo
        acc_hi = acc_hi + gk * hi
    routed = jnp.concatenate([acc_lo, acc_hi], axis=1)
    out_ref[...] = _layer_norm(alpha * x1 + (routed + shared), g2_ref[...], b2_ref[...])


def _combine(yg, gate, x1, swgu, swd, g2, b2, alpha):
    T, D = x1.shape
    tc = COMBINE_TILE
    consts = (swgu, swd, g2, b2)
    return pl.pallas_call(
        functools.partial(_combine_kernel, alpha=alpha),
        grid=(T // tc,),
        in_specs=[pl.BlockSpec((TOP_K, tc, D // 2), lambda i: (0, i, 0)),
                  pl.BlockSpec((TOP_K, tc), lambda i: (0, i)),
                  pl.BlockSpec((tc, D), lambda i: (i, 0))] + [_const_spec(c.shape) for c in consts],
        out_specs=pl.BlockSpec((tc, D), lambda i: (i, 0)),
        out_shape=jax.ShapeDtypeStruct((T, D), F32),
        compiler_params=pltpu.CompilerParams(dimension_semantics=("arbitrary",), vmem_limit_bytes=VMEM_LIMIT),
        name="moe_combine_norm",
    )(yg, gate, x1, *consts)


def _mixer(x, in_w, in_b, ng, nb, spatial_w, spatial_b, proj_a_w, proj_b_w, out_w, ln1_g, ln1_b,
           router_w, alpha):
    gw, aw, D = GMLP_WIDTH, ATTN_WIDTH, D_MODEL
    w = in_w.astype(BF16)
    q0 = 2 * gw
    wuv, buv = w[:, :q0], in_b[None, :q0]
    watt = jnp.stack([jnp.concatenate([w[:, q0 + s * aw + p * GROUP_WIDTH:q0 + s * aw + (p + 1) * GROUP_WIDTH]
                                       for s in range(3)], axis=1) for p in range(len(DILATED_PATTERNS))])
    batt = jnp.stack([jnp.concatenate([in_b[q0 + s * aw + p * GROUP_WIDTH:q0 + s * aw + (p + 1) * GROUP_WIDTH]
                                       for s in range(3)])[None] for p in range(len(DILATED_PATTERNS))])
    g0 = q0 + 3 * aw
    wg, bg = w[:, g0:], in_b[None, g0:]
    sb = jnp.repeat(spatial_b.T, gw // GMLP_GROUPS, axis=1)
    ya, gb, a1, a4, a16 = _input_projection(
        x, wuv, buv, watt, batt, wg, bg, ng[None], nb[None], spatial_w, sb, proj_a_w.astype(BF16))
    attn_outs = [_dilated_attention(a, p, d) for p, (a, (_, d)) in enumerate(zip((a1, a4, a16), DILATED_PATTERNS))]
    rw_t = router_w.T
    rwh = rw_t.astype(BF16)
    rwl = (rw_t - rwh.astype(F32)).astype(BF16)
    return _merge_and_norm(attn_outs, ya, gb, x, proj_b_w.astype(BF16), out_w.astype(BF16),
                           ln1_g[None], ln1_b[None], rwh, rwl, alpha)


def _moe(x1, x1p, scores_t, router_bias, w_gate, w_up, w_down, sw_gate, sw_up, sw_down, ln2_g, ln2_b, alpha):
    T = x1.shape[0]
    A = T * TOP_K
    n_blocks = (A + N_EXPERTS * (MOE_BLOCK - 1)) // MOE_BLOCK
    eidx_t, rank_t, gate_t, counts = _route(scores_t, router_bias[:, None])
    counts = counts[:, 0].astype(I32)
    padded = (counts + MOE_BLOCK - 1) // MOE_BLOCK * MOE_BLOCK
    pend = jnp.cumsum(padded).astype(I32)
    pstart = pend - padded
    block_starts = jnp.arange(n_blocks, dtype=I32) * MOE_BLOCK
    block_e = jnp.minimum(jnp.sum((pend[None, :] <= block_starts[:, None]).astype(I32), axis=1), N_EXPERTS - 1)
    used = counts > 0
    used_e = jnp.argsort(jnp.logical_not(used), stable=True).astype(I32)
    counts2 = jnp.stack([pend[-1] // MOE_BLOCK, jnp.sum(used.astype(I32))]).astype(I32)
    dest_t = _dest_rows(eidx_t, rank_t, pstart)
    xs = _sc_scatter_rows(x1p, dest_t.reshape(TOP_K, T // SC_CHUNK, SC_CHUNK).transpose(1, 0, 2),
                          n_blocks * MOE_BLOCK)
    y_rows = _experts(block_e, used_e, counts2, xs, w_gate, w_up, w_down)
    yg = _sc_gather_rows(y_rows, dest_t.reshape(A // SC_CHUNK, SC_CHUNK))
    swgu = jnp.concatenate([sw_gate, sw_up], axis=1).astype(BF16)
    return _combine(yg.reshape(TOP_K, T, D_MODEL // 2), gate_t, x1, swgu, sw_down.astype(BF16),
                    ln2_g[None], ln2_b[None], alpha)


def kernel(x, in_w, in_b, gmlp_norm_g, gmlp_norm_b, spatial_w, spatial_b, proj_a_w, proj_b_w, out_w,
           ln1_g, ln1_b, router_w, router_bias, expert_w_gate, expert_w_up, expert_w_down,
           shared_w_gate, shared_w_up, shared_w_down, ln2_g, ln2_b):
    B, S, D = x.shape
    depth = in_w.shape[0]
    alpha = np.float32((2.0 * depth) ** 0.25)
    for l in range(depth):
        x1, x1p, scores_t = _mixer(x, in_w[l], in_b[l], gmlp_norm_g[l], gmlp_norm_b[l], spatial_w[l],
                                 spatial_b[l], proj_a_w[l], proj_b_w[l], out_w[l], ln1_g[l], ln1_b[l],
                                 router_w[l], alpha)
        out = _moe(x1.reshape(B * S, D), x1p.reshape(B * S, D // 2), scores_t,
                   router_bias[l], expert_w_gate[l], expert_w_up[l], expert_w_down[l],
                   shared_w_gate[l], shared_w_up[l], shared_w_down[l], ln2_g[l], ln2_b[l], alpha)
        x = out.reshape(B, S, D)
    return x
```

```python
import functools
import math

import numpy as np
import jax
import jax.numpy as jnp
from jax import lax
from jax.experimental import pallas as pl
from jax.experimental.pallas import tpu as pltpu
from jax.experimental.pallas import tpu_sc as plsc

F32 = jnp.float32
BF16 = jnp.bfloat16
U32 = jnp.uint32
I32 = jnp.int32

D_MODEL = 1024
GMLP_WIDTH = 1024
GMLP_GROUPS = 8
GMLP_CHUNK = 128
HEAD_DIM = 64
DILATED_PATTERNS = ((128, 1), (512, 4), (2048, 16))
HEADS_PER_GROUP = 4
GROUP_WIDTH = HEADS_PER_GROUP * HEAD_DIM
ATTN_WIDTH = GROUP_WIDTH * len(DILATED_PATTERNS)
ATTN_BLOCK = 128
N_EXPERTS = 256
TOP_K = 8
TOP_K_SHIFT = 3
N_EXPERT_GROUPS = 8
TOPK_GROUPS = 4
EXPERT_DIM = 256
ROUTED_SCALE = 2.5
LN_EPS = 1e-5
LANES = 128
MASKED_SCORE = -1e30

PROJ_TILE = 512
ATTN_QBLOCKS = 4
MERGE_TILE = 512
ROUTE_TILE = 512
DEST_TILE = 2048
MOE_BLOCK = 256
SC_CHUNK = 64
MOE_TOKEN_SPLITS = 2
X_SLOTS = 4
W_SLOTS = 3
COMBINE_TILE = 256
VMEM_LIMIT = 56 * 1024 * 1024


def _layer_norm(y, g, b):
    mu = jnp.mean(y, axis=-1, keepdims=True)
    yc = y - mu
    var = jnp.mean(yc * yc, axis=-1, keepdims=True)
    return yc * lax.rsqrt(var + LN_EPS) * g + b


def _gelu(x):
    return 0.5 * x * (1.0 + lax.erf(x * np.float32(math.sqrt(0.5))))


def _pack_bf16_pairs(x):
    w = x.shape[1] // 2
    bits = pltpu.bitcast(x.astype(BF16).astype(F32), U32)
    return pltpu.bitcast((bits[:, :w] >> 16) | (bits[:, w:] & jnp.uint32(0xFFFF0000)), I32)


def _unpack_bf16_pairs(words):
    w = pltpu.bitcast(words, U32)
    lo = pltpu.bitcast(w << 16, F32)
    hi = pltpu.bitcast(w & jnp.uint32(0xFFFF0000), F32)
    return lo, hi


def _const_spec(shape):
    nd = len(shape)
    return pl.BlockSpec(shape, lambda *_: (0,) * nd)


def _proj_kernel(x_ref, wuv_ref, buv_ref, watt_ref, batt_ref, wg_ref, bg_ref, ng_ref, nb_ref,
                 sw_ref, sb_ref, pa_ref, ya_ref, gb_ref, a1_ref, a4_ref, a16_ref, xc_ref):
    tm = x_ref.shape[1]
    gw = GMLP_WIDTH
    xb = x_ref[0].astype(BF16)

    def proj(w, b):
        return jnp.dot(xb, w, preferred_element_type=F32) + b

    u = _gelu(proj(wuv_ref[:, :gw], buv_ref[:, :gw]))
    v = _gelu(proj(wuv_ref[:, gw:], buv_ref[:, gw:]))
    v = _layer_norm(v, ng_ref[...], nb_ref[...]).astype(BF16)

    cw = gw // GMLP_GROUPS
    row = lax.broadcasted_iota(I32, (GMLP_CHUNK, GMLP_CHUNK), 0)
    col = lax.broadcasted_iota(I32, (GMLP_CHUNK, GMLP_CHUNK), 1)
    ws = [jnp.where(row >= col, sw_ref[g], 0.0).astype(BF16) for g in range(GMLP_GROUPS)]
    chunks = []
    for c in range(tm // GMLP_CHUNK):
        vc = v[c * GMLP_CHUNK:(c + 1) * GMLP_CHUNK]
        cols = [jnp.dot(ws[g], vc[:, g * cw:(g + 1) * cw], preferred_element_type=F32)
                for g in range(GMLP_GROUPS)]
        chunks.append(jnp.concatenate(cols, axis=1) + sb_ref[...])
    vmix = jnp.concatenate(chunks, axis=0)
    ya = jnp.dot((u * vmix).astype(BF16), pa_ref[...], preferred_element_type=F32)
    ga = jax.nn.sigmoid(proj(wg_ref[:, :D_MODEL], bg_ref[:, :D_MODEL]))
    ya_ref[0] = (ga * ya).astype(BF16)
    gb_ref[0] = jax.nn.sigmoid(proj(wg_ref[:, D_MODEL:], bg_ref[:, D_MODEL:])).astype(BF16)

    n_chunks = x_ref.shape[2] // LANES
    for c in range(n_chunks):
        xc_ref[c] = x_ref[0, :, c * LANES:(c + 1) * LANES]
    for p, (a_ref, (_, d)) in enumerate(zip((a1_ref, a4_ref, a16_ref), DILATED_PATTERNS)):
        n = tm // d
        if d == 1:
            xp = xb
        else:
            xp = jnp.concatenate(
                [jnp.concatenate([xc_ref[c, pl.ds(r, n, stride=d), :] for c in range(n_chunks)], axis=1)
                 for r in range(d)], axis=0).astype(BF16)
        h = (jnp.dot(xp, watt_ref[p], preferred_element_type=F32) + batt_ref[p]).astype(BF16)
        for r in range(d):
            a_ref[0, r] = h[r * n:(r + 1) * n]


def _input_projection(x, wuv, buv, watt, batt, wg, bg, ng, nb, sw, sb, pa):
    B, S, D = x.shape
    tm = PROJ_TILE
    grid = (B, S // tm)
    out_shape = [jax.ShapeDtypeStruct((B, S, D), BF16), jax.ShapeDtypeStruct((B, S, D), BF16)]
    out_specs = [pl.BlockSpec((1, tm, D), lambda b, t: (b, t, 0)),
                 pl.BlockSpec((1, tm, D), lambda b, t: (b, t, 0))]
    for _, d in DILATED_PATTERNS:
        out_shape.append(jax.ShapeDtypeStruct((B, d, S // d, ATTN_WIDTH), BF16))
        out_specs.append(pl.BlockSpec((1, d, tm // d, ATTN_WIDTH), lambda b, t: (b, 0, t, 0)))
    consts = (wuv, buv, watt, batt, wg, bg, ng, nb, sw, sb, pa)
    return pl.pallas_call(
        _proj_kernel,
        grid=grid,
        in_specs=[pl.BlockSpec((1, tm, D), lambda b, t: (b, t, 0))] + [_const_spec(c.shape) for c in consts],
        out_specs=out_specs,
        out_shape=out_shape,
        scratch_shapes=[pltpu.VMEM((D // LANES, tm, LANES), F32)],
        compiler_params=pltpu.CompilerParams(
            dimension_semantics=("arbitrary", "arbitrary"), vmem_limit_bytes=VMEM_LIMIT),
        name="input_projection",
    )(x, *consts)


def _attn_kernel(qkv_ref, bias_ref, o_ref, lse_ref, *, qblocks):
    nq = pl.program_id(2)
    gwid = GROUP_WIDTH
    blk = ATTN_BLOCK
    lane = lax.broadcasted_iota(I32, (1, gwid), 1)
    head_masks = [(lane >= h * HEAD_DIM) & (lane < (h + 1) * HEAD_DIM) for h in range(HEADS_PER_GROUP)]
    q_scales = [jnp.where(m, np.float32(HEAD_DIM ** -0.5), 0.0).astype(BF16) for m in head_masks]

    def rows_of(j):
        n = nq * qblocks + j
        return n, pl.multiple_of(n * blk, blk), pl.multiple_of(jnp.maximum(n - 1, 0) * blk, blk)

    def scores(j):
        n, q0, p0 = rows_of(j)
        q = qkv_ref[0, 0, pl.ds(q0, blk), 0:gwid]
        kk = jnp.concatenate([qkv_ref[0, 0, pl.ds(p0, blk), gwid:2 * gwid],
                              qkv_ref[0, 0, pl.ds(q0, blk), gwid:2 * gwid]], axis=0)
        qs = jnp.concatenate([q * s for s in q_scales], axis=0)
        s = lax.dot_general(qs, kk, (((1,), (1,)), ((), ())), preferred_element_type=F32)
        return s + bias_ref[jnp.where(n == 0, 1, 0)]

    def finish(j, s):
        _, q0, p0 = rows_of(j)
        vv = jnp.concatenate([qkv_ref[0, 0, pl.ds(p0, blk), 2 * gwid:3 * gwid],
                              qkv_ref[0, 0, pl.ds(q0, blk), 2 * gwid:3 * gwid]], axis=0)
        m = jnp.max(s, axis=1, keepdims=True)
        p = jnp.exp(s - m)
        den = jnp.sum(p, axis=1, keepdims=True)
        pv = jnp.dot(p.astype(BF16), vv, preferred_element_type=F32)
        on = pv / den
        lse = m + jnp.log(den)
        o = jnp.zeros((blk, gwid), F32)
        l = jnp.zeros((blk, gwid), F32)
        for h in range(HEADS_PER_GROUP):
            o = jnp.where(head_masks[h], on[h * blk:(h + 1) * blk], o)
            l = jnp.where(head_masks[h], lse[h * blk:(h + 1) * blk], l)
        o_ref[0, 0, j * blk:(j + 1) * blk, :] = o.astype(BF16)
        lse_ref[0, 0, j * blk:(j + 1) * blk, :] = l

    s_next = scores(0)
    for j in range(qblocks):
        s_cur = s_next
        if j + 1 < qblocks:
            s_next = scores(j + 1)
        finish(j, s_cur)


def _alibi_slopes(n):
    def pow2_slopes(m):
        start = 2.0 ** (-8.0 / m)
        return [start ** (i + 1) for i in range(m)]
    p = 2 ** int(math.floor(math.log2(n)))
    s = pow2_slopes(p)
    if p < n:
        s = s + pow2_slopes(2 * p)[0::2][: n - p]
    return np.array(sorted(s, reverse=True), dtype=np.float32)


def _attn_bias_tables(group, dilation):
    blk = ATTN_BLOCK
    slopes = _alibi_slopes(HEADS_PER_GROUP * len(DILATED_PATTERNS))
    slopes = slopes[group * HEADS_PER_GROUP:(group + 1) * HEADS_PER_GROUP]
    qi = np.arange(blk)[:, None]
    ki = np.arange(2 * blk)[None, :]
    delta = blk + qi - ki
    band = (delta >= 0) & (delta <= blk)
    bias = -slopes[:, None, None] * (delta * dilation).astype(np.float32)[None]
    full = np.where(band[None], bias, np.float32(MASKED_SCORE)).astype(np.float32)
    first = np.where((ki >= blk)[None], full, np.float32(MASKED_SCORE)).astype(np.float32)
    return full.reshape(HEADS_PER_GROUP * blk, 2 * blk), first.reshape(HEADS_PER_GROUP * blk, 2 * blk)


def _dilated_attention(qkv, group, dilation):
    B, d, sd, _ = qkv.shape
    qblocks = min(ATTN_QBLOCKS, sd // ATTN_BLOCK)
    rows = qblocks * ATTN_BLOCK
    bias = np.stack(_attn_bias_tables(group, dilation))
    grid = (B, d, sd // rows)
    out_spec = pl.BlockSpec((1, 1, rows, GROUP_WIDTH), lambda b, r, n: (b, r, n, 0))
    return pl.pallas_call(
        functools.partial(_attn_kernel, qblocks=qblocks),
        grid=grid,
        in_specs=[pl.BlockSpec((1, 1, sd, ATTN_WIDTH), lambda b, r, n: (b, r, 0, 0)),
                  _const_spec(bias.shape)],
        out_specs=[out_spec, out_spec],
        out_shape=[jax.ShapeDtypeStruct((B, d, sd, GROUP_WIDTH), BF16),
                   jax.ShapeDtypeStruct((B, d, sd, GROUP_WIDTH), F32)],
        compiler_params=pltpu.CompilerParams(
            dimension_semantics=("arbitrary", "arbitrary", "arbitrary"), vmem_limit_bytes=VMEM_LIMIT),
        name=f"dilated_attention_d{dilation}",
    )(qkv, jnp.asarray(bias))


def _merge_kernel(o1_ref, l1_ref, o4_ref, l4_ref, o16_ref, l16_ref, ya_ref, gb_ref, x_ref,
                  pb_ref, ow_ref, g1_ref, b1_ref, rwh_ref, rwl_ref,
                  x1_ref, x1p_ref, sc_ref, so4, sl4, so16, sl16, *, alpha):
    tm = x_ref.shape[1]
    n_chunks = GROUP_WIDTH // LANES
    for (o_ref, l_ref, so, sl, d) in ((o4_ref, l4_ref, so4, sl4, 4), (o16_ref, l16_ref, so16, sl16, 16)):
        n = tm // d
        for r in range(d):
            o_r = o_ref[0, r].astype(F32)
            l_r = l_ref[0, r]
            for c in range(n_chunks):
                so[c, pl.ds(r, n, stride=d), :] = o_r[:, c * LANES:(c + 1) * LANES]
                sl[c, pl.ds(r, n, stride=d), :] = l_r[:, c * LANES:(c + 1) * LANES]

    def natural(s):
        return jnp.concatenate([s[c] for c in range(n_chunks)], axis=1)

    l1 = l1_ref[0, 0]
    l4 = natural(sl4)
    l16 = natural(sl16)
    lmax = jnp.maximum(jnp.maximum(l1, l4), l16)
    e1 = jnp.exp(l1 - lmax)
    e4 = jnp.exp(l4 - lmax)
    e16 = jnp.exp(l16 - lmax)
    yb = (e1 * o1_ref[0, 0].astype(F32) + e4 * natural(so4) + e16 * natural(so16)) / (e1 + e4 + e16)
    ybp = jnp.dot(yb.astype(BF16), pb_ref[...], preferred_element_type=F32)
    merged = ya_ref[0].astype(F32) + gb_ref[0].astype(F32) * ybp
    mix = jnp.dot(merged.astype(BF16), ow_ref[...], preferred_element_type=F32)
    x1 = _layer_norm(alpha * x_ref[0] + mix, g1_ref[...], b1_ref[...])
    x1_ref[0] = x1
    x1p_ref[0] = _pack_bf16_pairs(x1)
    hi = x1.astype(BF16)
    lo = (x1 - hi.astype(F32)).astype(BF16)
    def logits_t(w_ref, xt):
        return lax.dot_general(w_ref[...], xt, (((1,), (1,)), ((), ())), preferred_element_type=F32)
    sc_ref[...] = jax.nn.sigmoid(logits_t(rwh_ref, hi) + logits_t(rwh_ref, lo) + logits_t(rwl_ref, hi))


def _merge_and_norm(attn_outs, ya, gb, x, pb, ow, g1, b1, rwh, rwl, alpha):
    B, S, D = x.shape
    tm = MERGE_TILE
    in_specs = []
    args = []
    for (o, l), (_, d) in zip(attn_outs, DILATED_PATTERNS):
        spec = pl.BlockSpec((1, d, tm // d, GROUP_WIDTH), lambda b, t: (b, 0, t, 0))
        in_specs += [spec, spec]
        args += [o, l]
    tok_spec = pl.BlockSpec((1, tm, D), lambda b, t: (b, t, 0))
    in_specs += [tok_spec, tok_spec, tok_spec]
    args += [ya, gb, x]
    consts = (pb, ow, g1, b1, rwh, rwl)
    in_specs += [_const_spec(c.shape) for c in consts]
    return pl.pallas_call(
        functools.partial(_merge_kernel, alpha=alpha),
        grid=(B, S // tm),
        in_specs=in_specs,
        out_specs=[tok_spec,
                   pl.BlockSpec((1, tm, D // 2), lambda b, t: (b, t, 0)),
                   pl.BlockSpec((N_EXPERTS, tm), lambda b, t: (0, b * (S // tm) + t))],
        out_shape=[jax.ShapeDtypeStruct((B, S, D), F32),
                   jax.ShapeDtypeStruct((B, S, D // 2), I32),
                   jax.ShapeDtypeStruct((N_EXPERTS, B * S), F32)],
        scratch_shapes=[pltpu.VMEM((GROUP_WIDTH // LANES, tm, LANES), F32) for _ in range(4)],
        compiler_params=pltpu.CompilerParams(
            dimension_semantics=("arbitrary", "arbitrary"), vmem_limit_bytes=VMEM_LIMIT),
        name="merge_norm_router",
    )(*args, *consts)


def _sortable_key(x):
    bits = pltpu.bitcast(x, I32)
    return jnp.where(bits < 0, bits ^ jnp.int32(0x7FFFFFFF), bits)


def _route_kernel(sc_ref, bias_ref, before_ref, eidx_ref, rank_ref, gate_ref, cnt_ref, carry_ref):
    ne, tm = sc_ref.shape
    gsize = ne // N_EXPERT_GROUPS
    neg_inf = np.float32(-np.inf)
    removed = jnp.int32(-2 ** 31)

    @pl.when(pl.program_id(0) == 0)
    def _():
        carry_ref[...] = jnp.zeros_like(carry_ref)

    scores = sc_ref[...]
    biased = scores + bias_ref[...]

    gsum = []
    for g in range(N_EXPERT_GROUPS):
        v = biased[g * gsize:(g + 1) * gsize]
        m1 = jnp.max(v, axis=0, keepdims=True)
        n1 = jnp.sum(jnp.where(v == m1, 1.0, 0.0), axis=0, keepdims=True)
        m2 = jnp.max(jnp.where(v < m1, v, neg_inf), axis=0, keepdims=True)
        gsum.append(m1 + jnp.where(n1 >= 2.0, m1, m2))
    gkey = _sortable_key(jnp.concatenate(gsum, axis=0))

    def pick_first_max(keys, ids, n_ids):
        m = jnp.max(keys, axis=0, keepdims=True)
        idx = jnp.min(jnp.where(keys == m, ids, n_ids), axis=0, keepdims=True)
        hit = ids == idx
        return idx, hit, jnp.where(hit, removed, keys)

    gid = lax.broadcasted_iota(I32, (N_EXPERT_GROUPS, tm), 0)
    for _ in range(TOPK_GROUPS):
        _, _, gkey = pick_first_max(gkey, gid, N_EXPERT_GROUPS)
    group_on = gkey == removed

    masked = jnp.concatenate(
        [jnp.where(group_on[g:g + 1], biased[g * gsize:(g + 1) * gsize], neg_inf)
         for g in range(N_EXPERT_GROUPS)], axis=0)
    keys = _sortable_key(masked)
    eid = lax.broadcasted_iota(I32, (ne, tm), 0)
    picks = []
    for _ in range(TOP_K):
        idx, _, keys = pick_first_max(keys, eid, ne)
        picks.append(idx)

    sel = jnp.where(keys == removed, 1.0, 0.0)
    ranks = jnp.dot(sel.astype(BF16), before_ref[...], preferred_element_type=F32) + carry_ref[...]
    carry_ref[...] = carry_ref[...] + jnp.sum(sel, axis=1, keepdims=True)
    cnt_ref[...] = carry_ref[...]

    s_k, r_k = [], []
    for idx in picks:
        hit = eid == idx
        s_k.append(jnp.sum(jnp.where(hit, scores, 0.0), axis=0, keepdims=True))
        r_k.append(jnp.sum(jnp.where(hit, ranks, 0.0), axis=0, keepdims=True))
    total = s_k[0]
    for s in s_k[1:]:
        total = total + s
    eidx_ref[...] = jnp.concatenate(picks, axis=0)
    rank_ref[...] = jnp.concatenate(r_k, axis=0).astype(I32)
    gate_ref[...] = jnp.concatenate([s / total * np.float32(ROUTED_SCALE) for s in s_k], axis=0)


def _route(scores_t, bias, tok0, T):
    ne = scores_t.shape[0]
    tm = ROUTE_TILE
    first = tok0 // tm
    before = jnp.asarray(np.triu(np.ones((tm, tm), np.float32), k=1), BF16)
    out_spec = pl.BlockSpec((TOP_K, tm), lambda i: (0, i))
    return pl.pallas_call(
        _route_kernel,
        grid=(T // tm,),
        in_specs=[pl.BlockSpec((ne, tm), lambda i: (0, first + i)), _const_spec(bias.shape),
                  _const_spec(before.shape)],
        out_specs=[out_spec, out_spec, out_spec, _const_spec((ne, 1))],
        out_shape=[jax.ShapeDtypeStruct((TOP_K, T), I32), jax.ShapeDtypeStruct((TOP_K, T), I32),
                   jax.ShapeDtypeStruct((TOP_K, T), F32), jax.ShapeDtypeStruct((ne, 1), F32)],
        scratch_shapes=[pltpu.VMEM((ne, 1), F32)],
        compiler_params=pltpu.CompilerParams(dimension_semantics=("arbitrary",), vmem_limit_bytes=VMEM_LIMIT),
        name="route_topk",
    )(scores_t, bias, before)


def _dest_kernel(pstart_ref, eidx_ref, rank_ref, out_ref):
    eidx = eidx_ref[...]
    start = jnp.zeros(eidx.shape, I32)
    for e in range(N_EXPERTS):
        start = jnp.where(eidx == e, pstart_ref[e], start)
    out_ref[...] = start + rank_ref[...]


def _dest_rows(eidx_t, rank_t, pstart):
    T = eidx_t.shape[1]
    tm = DEST_TILE
    tok_spec = pl.BlockSpec((TOP_K, tm), lambda i: (0, i))
    return pl.pallas_call(
        _dest_kernel,
        grid=(T // tm,),
        in_specs=[pl.BlockSpec(memory_space=pltpu.SMEM), tok_spec, tok_spec],
        out_specs=tok_spec,
        out_shape=jax.ShapeDtypeStruct((TOP_K, T), I32),
        compiler_params=pltpu.CompilerParams(dimension_semantics=("arbitrary",)),
        name="moe_dest_rows",
    )(pstart, eidx_t, rank_t)


def _sc_workers():
    info = plsc.get_sparse_core_info()
    return info.num_cores, info.num_cores * info.num_subcores


def _sc_scatter_rows(rows, row0, dest, n_out):
    n_chunks, n_dst, ch = dest.shape
    width = rows.shape[1]
    n_cores, n_workers = _sc_workers()
    per_w = n_chunks // n_workers
    assert n_chunks % n_workers == 0 and per_w % 2 == 0 and row0 + n_chunks * ch <= rows.shape[0]

    def body(rows_hbm, dest_hbm, out_hbm, idx_v, buf, lsem, ssem):
        wid = lax.axis_index("s") * n_cores + lax.axis_index("c")
        c0 = wid * per_w
        pltpu.sync_copy(dest_hbm.at[pl.ds(c0, per_w)], idx_v)

        def load(c, b):
            return pltpu.make_async_copy(rows_hbm.at[pl.ds(row0 + (c0 + c) * ch, ch)], buf.at[b], lsem.at[b])

        def scatter(c, b, k):
            return pltpu.make_async_copy(buf.at[b], out_hbm.at[idx_v.at[c, k]], ssem.at[b])

        load(0, 0).start()

        @pl.loop(0, per_w, step=2)
        def _(c):
            for b in range(2):
                cc = c + b
                load(cc, b).wait()

                @pl.when(cc >= 1)
                def _():
                    for k in range(n_dst):
                        scatter(cc - 1, 1 - b, k).wait()

                @pl.when(cc + 1 < per_w)
                def _():
                    load(cc + 1, 1 - b).start()

                for k in range(n_dst):
                    scatter(cc, b, k).start()

        for k in range(n_dst):
            scatter(per_w - 1, 1, k).wait()

    mesh = plsc.VectorSubcoreMesh(core_axis_name="c", subcore_axis_name="s")
    return pl.kernel(
        body, out_type=jax.ShapeDtypeStruct((n_out, width), rows.dtype), mesh=mesh,
        scratch_types=[pltpu.VMEM((per_w, n_dst, ch), I32), pltpu.VMEM((2, ch, width), rows.dtype),
                       pltpu.SemaphoreType.DMA((2,)), pltpu.SemaphoreType.DMA((2,))],
        name="moe_dispatch_scatter",
    )(rows, dest)


def _sc_gather_rows(table, idx):
    n_chunks, ch = idx.shape
    width = table.shape[1]
    n_cores, n_workers = _sc_workers()
    per_w = n_chunks // n_workers
    assert n_chunks % n_workers == 0 and per_w % 2 == 0

    def body(table_hbm, idx_hbm, out_hbm, idx_v, buf, gsem, wsem):
        wid = lax.axis_index("s") * n_cores + lax.axis_index("c")
        c0 = wid * per_w
        pltpu.sync_copy(idx_hbm.at[pl.ds(c0, per_w)], idx_v)

        def gather(c, b):
            return pltpu.make_async_copy(table_hbm.at[idx_v.at[c]], buf.at[b], gsem.at[b])

        def write(c, b):
            return pltpu.make_async_copy(buf.at[b], out_hbm.at[pl.ds((c0 + c) * ch, ch)], wsem.at[b])

        gather(0, 0).start()

        @pl.loop(0, per_w, step=2)
        def _(c):
            for b in range(2):
                cc = c + b
                gather(cc, b).wait()

                @pl.when(cc >= 1)
                def _():
                    write(cc - 1, 1 - b).wait()

                @pl.when(cc + 1 < per_w)
                def _():
                    gather(cc + 1, 1 - b).start()

                write(cc, b).start()

        write(per_w - 1, 1).wait()

    mesh = plsc.VectorSubcoreMesh(core_axis_name="c", subcore_axis_name="s")
    return pl.kernel(
        body, out_type=jax.ShapeDtypeStruct((n_chunks * ch, width), table.dtype), mesh=mesh,
        scratch_types=[pltpu.VMEM((per_w, ch), I32), pltpu.VMEM((2, ch, width), table.dtype),
                       pltpu.SemaphoreType.DMA((2,)), pltpu.SemaphoreType.DMA((2,))],
        name="moe_combine_gather",
    )(table, idx)


def _expert_kernel(be_ref, ue_ref, nu_ref, xs_hbm, wg_hbm, wu_hbm, wd_hbm, y_hbm,
                   xbuf, ybuf, rg, ru, rd, wg_s, wu_s, wd_s, xsem, ysem, wsem):
    n_used = nu_ref[0]
    n_exp = nu_ref[1]
    half = D_MODEL // 2
    blk = MOE_BLOCK

    def rows(j):
        return pl.ds(pl.multiple_of(j * blk, blk), blk)

    def x_copy(j, p):
        return pltpu.make_async_copy(xs_hbm.at[rows(j)], xbuf.at[p], xsem.at[p])

    def y_copy(j, p):
        return pltpu.make_async_copy(ybuf.at[p], y_hbm.at[rows(j)], ysem.at[p])

    def w_copies(q, s):
        e = ue_ref[q]
        return (pltpu.make_async_copy(wg_hbm.at[e], rg.at[s], wsem.at[s, 0]),
                pltpu.make_async_copy(wu_hbm.at[e], ru.at[s], wsem.at[s, 1]),
                pltpu.make_async_copy(wd_hbm.at[e], rd.at[s], wsem.at[s, 2]))

    for q0 in range(W_SLOTS - 1):
        @pl.when(q0 < n_exp)
        def _():
            for c in w_copies(q0, q0):
                c.start()

    for j0 in range(X_SLOTS):
        @pl.when(j0 < n_used)
        def _():
            x_copy(j0, j0).start()

    def block_step(j, p, q):
        is_new = (j == 0) | (be_ref[j] != be_ref[jnp.maximum(j - 1, 0)])
        q = q + is_new.astype(I32)

        @pl.when(is_new)
        def _():
            s = q % W_SLOTS
            for c in w_copies(q, s):
                c.wait()
            wg_s[...] = rg[s].astype(BF16)
            wu_s[...] = ru[s].astype(BF16)
            wd_s[...] = rd[s].astype(BF16)

            @pl.when(q + W_SLOTS - 1 < n_exp)
            def _():
                for c in w_copies(q + W_SLOTS - 1, (q + W_SLOTS - 1) % W_SLOTS):
                    c.start()

        x_copy(j, p).wait()

        @pl.when(j >= X_SLOTS)
        def _():
            y_copy(j - X_SLOTS, p).wait()

        lo, hi = _unpack_bf16_pairs(xbuf[p])
        xlo = lo.astype(BF16)
        xhi = hi.astype(BF16)
        g = (jnp.dot(xlo, wg_s[:half], preferred_element_type=F32)
             + jnp.dot(xhi, wg_s[half:], preferred_element_type=F32))
        u = (jnp.dot(xlo, wu_s[:half], preferred_element_type=F32)
             + jnp.dot(xhi, wu_s[half:], preferred_element_type=F32))
        hb = (g * jax.nn.sigmoid(g) * u).astype(BF16)
        ybuf[p] = _pack_bf16_pairs(jnp.dot(hb, wd_s[...], preferred_element_type=F32))
        y_copy(j, p).start()

        @pl.when(j + X_SLOTS < n_used)
        def _():
            x_copy(j + X_SLOTS, p).start()

        return q

    def group(m, q):
        q = block_step(X_SLOTS * m, 0, q)
        for p in range(1, X_SLOTS):
            j = X_SLOTS * m + p
            q = lax.cond(j < n_used, functools.partial(block_step, j, p), lambda q: q, q)
        return q

    lax.fori_loop(0, (n_used + X_SLOTS - 1) // X_SLOTS, group, jnp.int32(-1))

    for back in range(X_SLOTS, 0, -1):
        b = n_used - back
        for p in range(X_SLOTS):
            @pl.when((b >= 0) & (b % X_SLOTS == p))
            def _():
                y_copy(b, p).wait()


def _experts(block_e, used_e, counts2, xs, wg, wu, wd):
    n_rows, half = xs.shape
    smem = pl.BlockSpec(memory_space=pltpu.SMEM)
    hbm = pl.BlockSpec(memory_space=pl.ANY)
    return pl.pallas_call(
        _expert_kernel,
        in_specs=[smem, smem, smem, hbm, hbm, hbm, hbm],
        out_specs=hbm,
        out_shape=jax.ShapeDtypeStruct((n_rows, half), I32),
        scratch_shapes=[pltpu.VMEM((X_SLOTS, MOE_BLOCK, half), I32), pltpu.VMEM((X_SLOTS, MOE_BLOCK, half), I32),
                        pltpu.VMEM((W_SLOTS, D_MODEL, EXPERT_DIM), F32),
                        pltpu.VMEM((W_SLOTS, D_MODEL, EXPERT_DIM), F32),
                        pltpu.VMEM((W_SLOTS, EXPERT_DIM, D_MODEL), F32),
                        pltpu.VMEM((D_MODEL, EXPERT_DIM), BF16),
                        pltpu.VMEM((D_MODEL, EXPERT_DIM), BF16),
                        pltpu.VMEM((EXPERT_DIM, D_MODEL), BF16),
                        pltpu.SemaphoreType.DMA((X_SLOTS,)), pltpu.SemaphoreType.DMA((X_SLOTS,)),
                        pltpu.SemaphoreType.DMA((W_SLOTS, 3))],
        compiler_params=pltpu.CompilerParams(vmem_limit_bytes=VMEM_LIMIT),
        name="moe_experts",
    )(block_e, used_e, counts2, xs, wg, wu, wd)


def _combine_kernel(yg_ref, gate_ref, x1_ref, swgu_ref, swd_ref, g2_ref, b2_ref, *rest, alpha):
    out_ref = rest[-1]
    tc = x1_ref.shape[0]
    x1 = x1_ref[...]
    gu = jnp.dot(x1.astype(BF16), swgu_ref[...], preferred_element_type=F32)
    g = gu[:, :EXPERT_DIM]
    u = gu[:, EXPERT_DIM:]
    shared = jnp.dot((g * jax.nn.sigmoid(g) * u).astype(BF16), swd_ref[...], preferred_element_type=F32)
    half = D_MODEL // 2
    acc_lo = jnp.zeros((tc, half), F32)
    acc_hi = jnp.zeros((tc, half), F32)
    gates = jnp.transpose(jnp.concatenate([gate_ref[...], jnp.zeros((tc - TOP_K, tc), F32)], axis=0))
    for k in range(TOP_K):
        lo, hi = _unpack_bf16_pairs(yg_ref[k])
        gk = gates[:, k:k + 1]
        acc_lo = acc_lo + gk * lo
        acc_hi = acc_hi + gk * hi
    routed = jnp.concatenate([acc_lo, acc_hi], axis=1)
    out_ref[...] = _layer_norm(alpha * x1 + (routed + shared), g2_ref[...], b2_ref[...])


def _combine(yg, gate, x1, tok0, swgu, swd, g2, b2, alpha, out_prev):
    T, D = x1.shape
    n_tok = gate.shape[1]
    tc = COMBINE_TILE
    first = tok0 // tc
    consts = (swgu, swd, g2, b2)
    args = [yg, gate, x1, *consts]
    in_specs = [pl.BlockSpec((TOP_K, tc, D // 2), lambda i: (0, i, 0)),
                pl.BlockSpec((TOP_K, tc), lambda i: (0, i)),
                pl.BlockSpec((tc, D), lambda i: (first + i, 0))] + [_const_spec(c.shape) for c in consts]
    aliases = {}
    if out_prev is not None:
        aliases = {len(args): 0}
        args.append(out_prev)
        in_specs.append(pl.BlockSpec(memory_space=pl.ANY))
    return pl.pallas_call(
        functools.partial(_combine_kernel, alpha=alpha),
        grid=(n_tok // tc,),
        in_specs=in_specs,
        out_specs=pl.BlockSpec((tc, D), lambda i: (first + i, 0)),
        out_shape=jax.ShapeDtypeStruct((T, D), F32),
        input_output_aliases=aliases,
        compiler_params=pltpu.CompilerParams(dimension_semantics=("arbitrary",), vmem_limit_bytes=VMEM_LIMIT),
        name="moe_combine_norm",
    )(*args)


def _mixer(x, in_w, in_b, ng, nb, spatial_w, spatial_b, proj_a_w, proj_b_w, out_w, ln1_g, ln1_b,
           router_w, alpha):
    gw, aw, D = GMLP_WIDTH, ATTN_WIDTH, D_MODEL
    w = in_w.astype(BF16)
    q0 = 2 * gw
    wuv, buv = w[:, :q0], in_b[None, :q0]
    watt = jnp.stack([jnp.concatenate([w[:, q0 + s * aw + p * GROUP_WIDTH:q0 + s * aw + (p + 1) * GROUP_WIDTH]
                                       for s in range(3)], axis=1) for p in range(len(DILATED_PATTERNS))])
    batt = jnp.stack([jnp.concatenate([in_b[q0 + s * aw + p * GROUP_WIDTH:q0 + s * aw + (p + 1) * GROUP_WIDTH]
                                       for s in range(3)])[None] for p in range(len(DILATED_PATTERNS))])
    g0 = q0 + 3 * aw
    wg, bg = w[:, g0:], in_b[None, g0:]
    sb = jnp.repeat(spatial_b.T, gw // GMLP_GROUPS, axis=1)
    ya, gb, a1, a4, a16 = _input_projection(
        x, wuv, buv, watt, batt, wg, bg, ng[None], nb[None], spatial_w, sb, proj_a_w.astype(BF16))
    attn_outs = [_dilated_attention(a, p, d) for p, (a, (_, d)) in enumerate(zip((a1, a4, a16), DILATED_PATTERNS))]
    rw_t = router_w.T
    rwh = rw_t.astype(BF16)
    rwl = (rw_t - rwh.astype(F32)).astype(BF16)
    return _merge_and_norm(attn_outs, ya, gb, x, proj_b_w.astype(BF16), out_w.astype(BF16),
                           ln1_g[None], ln1_b[None], rwh, rwl, alpha)


def _moe(x1, x1p, scores_t, router_bias, w_gate, w_up, w_down, sw_gate, sw_up, sw_down, ln2_g, ln2_b, alpha):
    swgu = jnp.concatenate([sw_gate, sw_up], axis=1).astype(BF16)
    swd = sw_down.astype(BF16)
    n_tok = x1.shape[0] // MOE_TOKEN_SPLITS
    A = n_tok * TOP_K
    n_blocks = (A + N_EXPERTS * (MOE_BLOCK - 1)) // MOE_BLOCK
    out = None
    for part in range(MOE_TOKEN_SPLITS):
        tok0 = part * n_tok
        eidx_t, rank_t, gate_t, counts = _route(scores_t, router_bias[:, None], tok0, n_tok)
        counts = counts[:, 0].astype(I32)
        padded = (counts + MOE_BLOCK - 1) // MOE_BLOCK * MOE_BLOCK
        pend = jnp.cumsum(padded).astype(I32)
        pstart = pend - padded
        block_starts = jnp.arange(n_blocks, dtype=I32) * MOE_BLOCK
        block_e = jnp.minimum(jnp.sum((pend[None, :] <= block_starts[:, None]).astype(I32), axis=1),
                              N_EXPERTS - 1)
        used = counts > 0
        used_e = jnp.argsort(jnp.logical_not(used), stable=True).astype(I32)
        counts2 = jnp.stack([pend[-1] // MOE_BLOCK, jnp.sum(used.astype(I32))]).astype(I32)
        dest_t = _dest_rows(eidx_t, rank_t, pstart)
        xs = _sc_scatter_rows(x1p, tok0, dest_t.reshape(TOP_K, n_tok // SC_CHUNK, SC_CHUNK).transpose(1, 0, 2),
                              n_blocks * MOE_BLOCK)
        y_rows = _experts(block_e, used_e, counts2, xs, w_gate, w_up, w_down)
        yg = _sc_gather_rows(y_rows, dest_t.reshape(A // SC_CHUNK, SC_CHUNK))
        out = _combine(yg.reshape(TOP_K, n_tok, D_MODEL // 2), gate_t, x1, tok0, swgu, swd,
                       ln2_g[None], ln2_b[None], alpha, out)
    return out


def kernel(x, in_w, in_b, gmlp_norm_g, gmlp_norm_b, spatial_w, spatial_b, proj_a_w, proj_b_w, out_w,
           ln1_g, ln1_b, router_w, router_bias, expert_w_gate, expert_w_up, expert_w_down,
           shared_w_gate, shared_w_up, shared_w_down, ln2_g, ln2_b):
    B, S, D = x.shape
    depth = in_w.shape[0]
    alpha = np.float32((2.0 * depth) ** 0.25)
    for l in range(depth):
        x1, x1p, scores_t = _mixer(x, in_w[l], in_b[l], gmlp_norm_g[l], gmlp_norm_b[l], spatial_w[l],
                                 spatial_b[l], proj_a_w[l], proj_b_w[l], out_w[l], ln1_g[l], ln1_b[l],
                                 router_w[l], alpha)
        out = _moe(x1.reshape(B * S, D), x1p.reshape(B * S, D // 2), scores_t,
                   router_bias[l], expert_w_gate[l], expert_w_up[l], expert_w_down[l],
                   shared_w_gate[l], shared_w_up[l], shared_w_down[l], ln2_g[l], ln2_b[l], alpha)
        x = out.reshape(B, S, D)
    return x
```

```python
import functools
import math

import numpy as np
import jax
import jax.numpy as jnp
from jax import lax
from jax.experimental import pallas as pl
from jax.experimental.pallas import tpu as pltpu
from jax.experimental.pallas import tpu_sc as plsc

F32 = jnp.float32
BF16 = jnp.bfloat16
U32 = jnp.uint32
I32 = jnp.int32

D_MODEL = 1024
GMLP_WIDTH = 1024
GMLP_GROUPS = 8
GMLP_CHUNK = 128
HEAD_DIM = 64
DILATED_PATTERNS = ((128, 1), (512, 4), (2048, 16))
HEADS_PER_GROUP = 4
GROUP_WIDTH = HEADS_PER_GROUP * HEAD_DIM
ATTN_WIDTH = GROUP_WIDTH * len(DILATED_PATTERNS)
ATTN_BLOCK = 128
N_EXPERTS = 256
TOP_K = 8
TOP_K_SHIFT = 3
N_EXPERT_GROUPS = 8
TOPK_GROUPS = 4
EXPERT_DIM = 256
ROUTED_SCALE = 2.5
LN_EPS = 1e-5
LANES = 128
MASKED_SCORE = -1e30

PROJ_TILE = 512
ATTN_QBLOCKS = 4
MERGE_TILE = 512
ROUTE_TILE = 512
DEST_TILE = 2048
MOE_BLOCK = 256
SC_PACK_CHUNK_WORDS = 16384
SC_CHUNK = 64
MOE_TOKEN_SPLITS = 2
X_SLOTS = 4
W_SLOTS = 3
COMBINE_TILE = 256
VMEM_LIMIT = 56 * 1024 * 1024


def _layer_norm(y, g, b):
    mu = jnp.mean(y, axis=-1, keepdims=True)
    yc = y - mu
    var = jnp.mean(yc * yc, axis=-1, keepdims=True)
    return yc * lax.rsqrt(var + LN_EPS) * g + b


def _gelu(x):
    return 0.5 * x * (1.0 + lax.erf(x * np.float32(math.sqrt(0.5))))


def _pack_bf16_pairs(x):
    w = x.shape[1] // 2
    bits = pltpu.bitcast(x.astype(BF16).astype(F32), U32)
    return pltpu.bitcast((bits[:, :w] >> 16) | (bits[:, w:] & jnp.uint32(0xFFFF0000)), I32)


def _unpack_bf16_pairs(words):
    w = pltpu.bitcast(words, U32)
    lo = pltpu.bitcast(w << 16, F32)
    hi = pltpu.bitcast(w & jnp.uint32(0xFFFF0000), F32)
    return lo, hi


def _const_spec(shape):
    nd = len(shape)
    return pl.BlockSpec(shape, lambda *_: (0,) * nd)


def _proj_kernel(x_ref, wuv_ref, buv_ref, watt_ref, batt_ref, wg_ref, bg_ref, ng_ref, nb_ref,
                 sw_ref, sb_ref, pa_ref, ya_ref, gb_ref, a1_ref, a4_ref, a16_ref, xc_ref):
    tm = x_ref.shape[1]
    gw = GMLP_WIDTH
    xb = x_ref[0].astype(BF16)

    def proj(w, b):
        return jnp.dot(xb, w, preferred_element_type=F32) + b

    u = _gelu(proj(wuv_ref[:, :gw], buv_ref[:, :gw]))
    v = _gelu(proj(wuv_ref[:, gw:], buv_ref[:, gw:]))
    v = _layer_norm(v, ng_ref[...], nb_ref[...]).astype(BF16)

    cw = gw // GMLP_GROUPS
    row = lax.broadcasted_iota(I32, (GMLP_CHUNK, GMLP_CHUNK), 0)
    col = lax.broadcasted_iota(I32, (GMLP_CHUNK, GMLP_CHUNK), 1)
    ws = [jnp.where(row >= col, sw_ref[g], 0.0).astype(BF16) for g in range(GMLP_GROUPS)]
    chunks = []
    for c in range(tm // GMLP_CHUNK):
        vc = v[c * GMLP_CHUNK:(c + 1) * GMLP_CHUNK]
        cols = [jnp.dot(ws[g], vc[:, g * cw:(g + 1) * cw], preferred_element_type=F32)
                for g in range(GMLP_GROUPS)]
        chunks.append(jnp.concatenate(cols, axis=1) + sb_ref[...])
    vmix = jnp.concatenate(chunks, axis=0)
    ya = jnp.dot((u * vmix).astype(BF16), pa_ref[...], preferred_element_type=F32)
    ga = jax.nn.sigmoid(proj(wg_ref[:, :D_MODEL], bg_ref[:, :D_MODEL]))
    ya_ref[0] = (ga * ya).astype(BF16)
    gb_ref[0] = jax.nn.sigmoid(proj(wg_ref[:, D_MODEL:], bg_ref[:, D_MODEL:])).astype(BF16)

    n_chunks = x_ref.shape[2] // LANES
    for c in range(n_chunks):
        xc_ref[c] = x_ref[0, :, c * LANES:(c + 1) * LANES]
    for p, (a_ref, (_, d)) in enumerate(zip((a1_ref, a4_ref, a16_ref), DILATED_PATTERNS)):
        n = tm // d
        if d == 1:
            xp = xb
        else:
            xp = jnp.concatenate(
                [jnp.concatenate([xc_ref[c, pl.ds(r, n, stride=d), :] for c in range(n_chunks)], axis=1)
                 for r in range(d)], axis=0).astype(BF16)
        h = (jnp.dot(xp, watt_ref[p], preferred_element_type=F32) + batt_ref[p]).astype(BF16)
        for r in range(d):
            a_ref[0, r] = h[r * n:(r + 1) * n]


def _input_projection(x, wuv, buv, watt, batt, wg, bg, ng, nb, sw, sb, pa):
    B, S, D = x.shape
    tm = PROJ_TILE
    grid = (B, S // tm)
    out_shape = [jax.ShapeDtypeStruct((B, S, D), BF16), jax.ShapeDtypeStruct((B, S, D), BF16)]
    out_specs = [pl.BlockSpec((1, tm, D), lambda b, t: (b, t, 0)),
                 pl.BlockSpec((1, tm, D), lambda b, t: (b, t, 0))]
    for _, d in DILATED_PATTERNS:
        out_shape.append(jax.ShapeDtypeStruct((B, d, S // d, ATTN_WIDTH), BF16))
        out_specs.append(pl.BlockSpec((1, d, tm // d, ATTN_WIDTH), lambda b, t: (b, 0, t, 0)))
    consts = (wuv, buv, watt, batt, wg, bg, ng, nb, sw, sb, pa)
    return pl.pallas_call(
        _proj_kernel,
        grid=grid,
        in_specs=[pl.BlockSpec((1, tm, D), lambda b, t: (b, t, 0))] + [_const_spec(c.shape) for c in consts],
        out_specs=out_specs,
        out_shape=out_shape,
        scratch_shapes=[pltpu.VMEM((D // LANES, tm, LANES), F32)],
        compiler_params=pltpu.CompilerParams(
            dimension_semantics=("arbitrary", "arbitrary"), vmem_limit_bytes=VMEM_LIMIT),
        name="input_projection",
    )(x, *consts)


def _attn_kernel(qkv_ref, bias_ref, o_ref, lse_ref, *, qblocks):
    nq = pl.program_id(2)
    gwid = GROUP_WIDTH
    blk = ATTN_BLOCK
    lane = lax.broadcasted_iota(I32, (1, gwid), 1)
    head_masks = [(lane >= h * HEAD_DIM) & (lane < (h + 1) * HEAD_DIM) for h in range(HEADS_PER_GROUP)]
    q_scales = [jnp.where(m, np.float32(HEAD_DIM ** -0.5), 0.0).astype(BF16) for m in head_masks]

    def rows_of(j):
        n = nq * qblocks + j
        return n, pl.multiple_of(n * blk, blk), pl.multiple_of(jnp.maximum(n - 1, 0) * blk, blk)

    def scores(j):
        n, q0, p0 = rows_of(j)
        q = qkv_ref[0, 0, pl.ds(q0, blk), 0:gwid]
        kk = jnp.concatenate([qkv_ref[0, 0, pl.ds(p0, blk), gwid:2 * gwid],
                              qkv_ref[0, 0, pl.ds(q0, blk), gwid:2 * gwid]], axis=0)
        qs = jnp.concatenate([q * s for s in q_scales], axis=0)
        s = lax.dot_general(qs, kk, (((1,), (1,)), ((), ())), preferred_element_type=F32)
        return s + bias_ref[jnp.where(n == 0, 1, 0)]

    def finish(j, s):
        _, q0, p0 = rows_of(j)
        vv = jnp.concatenate([qkv_ref[0, 0, pl.ds(p0, blk), 2 * gwid:3 * gwid],
                              qkv_ref[0, 0, pl.ds(q0, blk), 2 * gwid:3 * gwid]], axis=0)
        m = jnp.max(s, axis=1, keepdims=True)
        p = jnp.exp(s - m)
        den = jnp.sum(p, axis=1, keepdims=True)
        pv = jnp.dot(p.astype(BF16), vv, preferred_element_type=F32)
        on = pv / den
        lse = m + jnp.log(den)
        o = jnp.zeros((blk, gwid), F32)
        l = jnp.zeros((blk, gwid), F32)
        for h in range(HEADS_PER_GROUP):
            o = jnp.where(head_masks[h], on[h * blk:(h + 1) * blk], o)
            l = jnp.where(head_masks[h], lse[h * blk:(h + 1) * blk], l)
        o_ref[0, 0, j * blk:(j + 1) * blk, :] = o.astype(BF16)
        lse_ref[0, 0, j * blk:(j + 1) * blk, :] = l

    s_next = scores(0)
    for j in range(qblocks):
        s_cur = s_next
        if j + 1 < qblocks:
            s_next = scores(j + 1)
        finish(j, s_cur)


def _alibi_slopes(n):
    def pow2_slopes(m):
        start = 2.0 ** (-8.0 / m)
        return [start ** (i + 1) for i in range(m)]
    p = 2 ** int(math.floor(math.log2(n)))
    s = pow2_slopes(p)
    if p < n:
        s = s + pow2_slopes(2 * p)[0::2][: n - p]
    return np.array(sorted(s, reverse=True), dtype=np.float32)


def _attn_bias_tables(group, dilation):
    blk = ATTN_BLOCK
    slopes = _alibi_slopes(HEADS_PER_GROUP * len(DILATED_PATTERNS))
    slopes = slopes[group * HEADS_PER_GROUP:(group + 1) * HEADS_PER_GROUP]
    qi = np.arange(blk)[:, None]
    ki = np.arange(2 * blk)[None, :]
    delta = blk + qi - ki
    band = (delta >= 0) & (delta <= blk)
    bias = -slopes[:, None, None] * (delta * dilation).astype(np.float32)[None]
    full = np.where(band[None], bias, np.float32(MASKED_SCORE)).astype(np.float32)
    first = np.where((ki >= blk)[None], full, np.float32(MASKED_SCORE)).astype(np.float32)
    return full.reshape(HEADS_PER_GROUP * blk, 2 * blk), first.reshape(HEADS_PER_GROUP * blk, 2 * blk)


def _dilated_attention(qkv, group, dilation):
    B, d, sd, _ = qkv.shape
    qblocks = min(ATTN_QBLOCKS, sd // ATTN_BLOCK)
    rows = qblocks * ATTN_BLOCK
    bias = np.stack(_attn_bias_tables(group, dilation))
    grid = (B, d, sd // rows)
    out_spec = pl.BlockSpec((1, 1, rows, GROUP_WIDTH), lambda b, r, n: (b, r, n, 0))
    return pl.pallas_call(
        functools.partial(_attn_kernel, qblocks=qblocks),
        grid=grid,
        in_specs=[pl.BlockSpec((1, 1, sd, ATTN_WIDTH), lambda b, r, n: (b, r, 0, 0)),
                  _const_spec(bias.shape)],
        out_specs=[out_spec, out_spec],
        out_shape=[jax.ShapeDtypeStruct((B, d, sd, GROUP_WIDTH), BF16),
                   jax.ShapeDtypeStruct((B, d, sd, GROUP_WIDTH), F32)],
        compiler_params=pltpu.CompilerParams(
            dimension_semantics=("arbitrary", "arbitrary", "arbitrary"), vmem_limit_bytes=VMEM_LIMIT),
        name=f"dilated_attention_d{dilation}",
    )(qkv, jnp.asarray(bias))


def _merge_kernel(o1_ref, l1_ref, o4_ref, l4_ref, o16_ref, l16_ref, ya_ref, gb_ref, x_ref,
                  pb_ref, ow_ref, g1_ref, b1_ref, rwh_ref, rwl_ref,
                  x1_ref, x1p_ref, sc_ref, so4, sl4, so16, sl16, *, alpha):
    tm = x_ref.shape[1]
    n_chunks = GROUP_WIDTH // LANES
    for (o_ref, l_ref, so, sl, d) in ((o4_ref, l4_ref, so4, sl4, 4), (o16_ref, l16_ref, so16, sl16, 16)):
        n = tm // d
        for r in range(d):
            o_r = o_ref[0, r].astype(F32)
            l_r = l_ref[0, r]
            for c in range(n_chunks):
                so[c, pl.ds(r, n, stride=d), :] = o_r[:, c * LANES:(c + 1) * LANES]
                sl[c, pl.ds(r, n, stride=d), :] = l_r[:, c * LANES:(c + 1) * LANES]

    def natural(s):
        return jnp.concatenate([s[c] for c in range(n_chunks)], axis=1)

    l1 = l1_ref[0, 0]
    l4 = natural(sl4)
    l16 = natural(sl16)
    lmax = jnp.maximum(jnp.maximum(l1, l4), l16)
    e1 = jnp.exp(l1 - lmax)
    e4 = jnp.exp(l4 - lmax)
    e16 = jnp.exp(l16 - lmax)
    yb = (e1 * o1_ref[0, 0].astype(F32) + e4 * natural(so4) + e16 * natural(so16)) / (e1 + e4 + e16)
    ybp = jnp.dot(yb.astype(BF16), pb_ref[...], preferred_element_type=F32)
    merged = ya_ref[0].astype(F32) + gb_ref[0].astype(F32) * ybp
    mix = jnp.dot(merged.astype(BF16), ow_ref[...], preferred_element_type=F32)
    x1 = _layer_norm(alpha * x_ref[0] + mix, g1_ref[...], b1_ref[...])
    x1_ref[0] = x1
    x1p_ref[0] = _pack_bf16_pairs(x1)
    hi = x1.astype(BF16)
    lo = (x1 - hi.astype(F32)).astype(BF16)
    def logits_t(w_ref, xt):
        return lax.dot_general(w_ref[...], xt, (((1,), (1,)), ((), ())), preferred_element_type=F32)
    sc_ref[...] = jax.nn.sigmoid(logits_t(rwh_ref, hi) + logits_t(rwh_ref, lo) + logits_t(rwl_ref, hi))


def _merge_and_norm(attn_outs, ya, gb, x, pb, ow, g1, b1, rwh, rwl, alpha):
    B, S, D = x.shape
    tm = MERGE_TILE
    in_specs = []
    args = []
    for (o, l), (_, d) in zip(attn_outs, DILATED_PATTERNS):
        spec = pl.BlockSpec((1, d, tm // d, GROUP_WIDTH), lambda b, t: (b, 0, t, 0))
        in_specs += [spec, spec]
        args += [o, l]
    tok_spec = pl.BlockSpec((1, tm, D), lambda b, t: (b, t, 0))
    in_specs += [tok_spec, tok_spec, tok_spec]
    args += [ya, gb, x]
    consts = (pb, ow, g1, b1, rwh, rwl)
    in_specs += [_const_spec(c.shape) for c in consts]
    return pl.pallas_call(
        functools.partial(_merge_kernel, alpha=alpha),
        grid=(B, S // tm),
        in_specs=in_specs,
        out_specs=[tok_spec,
                   pl.BlockSpec((1, tm, D // 2), lambda b, t: (b, t, 0)),
                   pl.BlockSpec((N_EXPERTS, tm), lambda b, t: (0, b * (S // tm) + t))],
        out_shape=[jax.ShapeDtypeStruct((B, S, D), F32),
                   jax.ShapeDtypeStruct((B, S, D // 2), I32),
                   jax.ShapeDtypeStruct((N_EXPERTS, B * S), F32)],
        scratch_shapes=[pltpu.VMEM((GROUP_WIDTH // LANES, tm, LANES), F32) for _ in range(4)],
        compiler_params=pltpu.CompilerParams(
            dimension_semantics=("arbitrary", "arbitrary"), vmem_limit_bytes=VMEM_LIMIT),
        name="merge_norm_router",
    )(*args, *consts)


def _sortable_key(x):
    bits = pltpu.bitcast(x, I32)
    return jnp.where(bits < 0, bits ^ jnp.int32(0x7FFFFFFF), bits)


def _route_kernel(sc_ref, bias_ref, before_ref, eidx_ref, rank_ref, gate_ref, cnt_ref, carry_ref):
    ne, tm = sc_ref.shape
    gsize = ne // N_EXPERT_GROUPS
    neg_inf = np.float32(-np.inf)
    removed = jnp.int32(-2 ** 31)

    @pl.when(pl.program_id(0) == 0)
    def _():
        carry_ref[...] = jnp.zeros_like(carry_ref)

    scores = sc_ref[...]
    biased = scores + bias_ref[...]

    gsum = []
    for g in range(N_EXPERT_GROUPS):
        v = biased[g * gsize:(g + 1) * gsize]
        m1 = jnp.max(v, axis=0, keepdims=True)
        n1 = jnp.sum(jnp.where(v == m1, 1.0, 0.0), axis=0, keepdims=True)
        m2 = jnp.max(jnp.where(v < m1, v, neg_inf), axis=0, keepdims=True)
        gsum.append(m1 + jnp.where(n1 >= 2.0, m1, m2))
    gkey = _sortable_key(jnp.concatenate(gsum, axis=0))

    def pick_first_max(keys, ids, n_ids):
        m = jnp.max(keys, axis=0, keepdims=True)
        idx = jnp.min(jnp.where(keys == m, ids, n_ids), axis=0, keepdims=True)
        hit = ids == idx
        return idx, hit, jnp.where(hit, removed, keys)

    gid = lax.broadcasted_iota(I32, (N_EXPERT_GROUPS, tm), 0)
    for _ in range(TOPK_GROUPS):
        _, _, gkey = pick_first_max(gkey, gid, N_EXPERT_GROUPS)
    group_on = gkey == removed

    masked = jnp.concatenate(
        [jnp.where(group_on[g:g + 1], biased[g * gsize:(g + 1) * gsize], neg_inf)
         for g in range(N_EXPERT_GROUPS)], axis=0)
    keys = _sortable_key(masked)
    eid = lax.broadcasted_iota(I32, (ne, tm), 0)
    picks = []
    for _ in range(TOP_K):
        idx, _, keys = pick_first_max(keys, eid, ne)
        picks.append(idx)

    sel = jnp.where(keys == removed, 1.0, 0.0)
    ranks = jnp.dot(sel.astype(BF16), before_ref[...], preferred_element_type=F32) + carry_ref[...]
    carry_ref[...] = carry_ref[...] + jnp.sum(sel, axis=1, keepdims=True)
    cnt_ref[...] = carry_ref[...]

    s_k, r_k = [], []
    for idx in picks:
        hit = eid == idx
        s_k.append(jnp.sum(jnp.where(hit, scores, 0.0), axis=0, keepdims=True))
        r_k.append(jnp.sum(jnp.where(hit, ranks, 0.0), axis=0, keepdims=True))
    total = s_k[0]
    for s in s_k[1:]:
        total = total + s
    eidx_ref[...] = jnp.concatenate(picks, axis=0)
    rank_ref[...] = jnp.concatenate(r_k, axis=0).astype(I32)
    gate_ref[...] = jnp.concatenate([s / total * np.float32(ROUTED_SCALE) for s in s_k], axis=0)


def _route(scores_t, bias, tok0, T):
    ne = scores_t.shape[0]
    tm = ROUTE_TILE
    first = tok0 // tm
    before = jnp.asarray(np.triu(np.ones((tm, tm), np.float32), k=1), BF16)
    out_spec = pl.BlockSpec((TOP_K, tm), lambda i: (0, i))
    return pl.pallas_call(
        _route_kernel,
        grid=(T // tm,),
        in_specs=[pl.BlockSpec((ne, tm), lambda i: (0, first + i)), _const_spec(bias.shape),
                  _const_spec(before.shape)],
        out_specs=[out_spec, out_spec, out_spec, _const_spec((ne, 1))],
        out_shape=[jax.ShapeDtypeStruct((TOP_K, T), I32), jax.ShapeDtypeStruct((TOP_K, T), I32),
                   jax.ShapeDtypeStruct((TOP_K, T), F32), jax.ShapeDtypeStruct((ne, 1), F32)],
        scratch_shapes=[pltpu.VMEM((ne, 1), F32)],
        compiler_params=pltpu.CompilerParams(dimension_semantics=("arbitrary",), vmem_limit_bytes=VMEM_LIMIT),
        name="route_topk",
    )(scores_t, bias, before)


def _dest_kernel(pstart_ref, eidx_ref, rank_ref, out_ref):
    eidx = eidx_ref[...]
    start = jnp.zeros(eidx.shape, I32)
    for e in range(N_EXPERTS):
        start = jnp.where(eidx == e, pstart_ref[e], start)
    out_ref[...] = start + rank_ref[...]


def _dest_rows(eidx_t, rank_t, pstart):
    T = eidx_t.shape[1]
    tm = DEST_TILE
    tok_spec = pl.BlockSpec((TOP_K, tm), lambda i: (0, i))
    return pl.pallas_call(
        _dest_kernel,
        grid=(T // tm,),
        in_specs=[pl.BlockSpec(memory_space=pltpu.SMEM), tok_spec, tok_spec],
        out_specs=tok_spec,
        out_shape=jax.ShapeDtypeStruct((TOP_K, T), I32),
        compiler_params=pltpu.CompilerParams(dimension_semantics=("arbitrary",)),
        name="moe_dest_rows",
    )(pstart, eidx_t, rank_t)


def _sc_workers():
    info = plsc.get_sparse_core_info()
    return info.num_cores, info.num_cores * info.num_subcores


def _sc_scatter_rows(rows, row0, dest, n_out):
    n_chunks, n_dst, ch = dest.shape
    width = rows.shape[1]
    n_cores, n_workers = _sc_workers()
    per_w = n_chunks // n_workers
    assert n_chunks % n_workers == 0 and per_w % 2 == 0 and row0 + n_chunks * ch <= rows.shape[0]

    def body(rows_hbm, dest_hbm, out_hbm, idx_v, buf, lsem, ssem):
        wid = lax.axis_index("s") * n_cores + lax.axis_index("c")
        c0 = wid * per_w
        pltpu.sync_copy(dest_hbm.at[pl.ds(c0, per_w)], idx_v)

        def load(c, b):
            return pltpu.make_async_copy(rows_hbm.at[pl.ds(row0 + (c0 + c) * ch, ch)], buf.at[b], lsem.at[b])

        def scatter(c, b, k):
            return pltpu.make_async_copy(buf.at[b], out_hbm.at[idx_v.at[c, k]], ssem.at[b])

        load(0, 0).start()

        @pl.loop(0, per_w, step=2)
        def _(c):
            for b in range(2):
                cc = c + b
                load(cc, b).wait()

                @pl.when(cc >= 1)
                def _():
                    for k in range(n_dst):
                        scatter(cc - 1, 1 - b, k).wait()

                @pl.when(cc + 1 < per_w)
                def _():
                    load(cc + 1, 1 - b).start()

                for k in range(n_dst):
                    scatter(cc, b, k).start()

        for k in range(n_dst):
            scatter(per_w - 1, 1, k).wait()

    mesh = plsc.VectorSubcoreMesh(core_axis_name="c", subcore_axis_name="s")
    return pl.kernel(
        body, out_type=jax.ShapeDtypeStruct((n_out, width), rows.dtype), mesh=mesh,
        scratch_types=[pltpu.VMEM((per_w, n_dst, ch), I32), pltpu.VMEM((2, ch, width), rows.dtype),
                       pltpu.SemaphoreType.DMA((2,)), pltpu.SemaphoreType.DMA((2,))],
        name="moe_dispatch_scatter",
    )(rows, dest)


def _sc_gather_rows(table, idx):
    n_chunks, ch = idx.shape
    width = table.shape[1]
    n_cores, n_workers = _sc_workers()
    per_w = n_chunks // n_workers
    assert n_chunks % n_workers == 0 and per_w % 2 == 0

    def body(table_hbm, idx_hbm, out_hbm, idx_v, buf, gsem, wsem):
        wid = lax.axis_index("s") * n_cores + lax.axis_index("c")
        c0 = wid * per_w
        pltpu.sync_copy(idx_hbm.at[pl.ds(c0, per_w)], idx_v)

        def gather(c, b):
            return pltpu.make_async_copy(table_hbm.at[idx_v.at[c]], buf.at[b], gsem.at[b])

        def write(c, b):
            return pltpu.make_async_copy(buf.at[b], out_hbm.at[pl.ds((c0 + c) * ch, ch)], wsem.at[b])

        gather(0, 0).start()

        @pl.loop(0, per_w, step=2)
        def _(c):
            for b in range(2):
                cc = c + b
                gather(cc, b).wait()

                @pl.when(cc >= 1)
                def _():
                    write(cc - 1, 1 - b).wait()

                @pl.when(cc + 1 < per_w)
                def _():
                    gather(cc + 1, 1 - b).start()

                write(cc, b).start()

        write(per_w - 1, 1).wait()

    mesh = plsc.VectorSubcoreMesh(core_axis_name="c", subcore_axis_name="s")
    return pl.kernel(
        body, out_type=jax.ShapeDtypeStruct((n_chunks * ch, width), table.dtype), mesh=mesh,
        scratch_types=[pltpu.VMEM((per_w, ch), I32), pltpu.VMEM((2, ch, width), table.dtype),
                       pltpu.SemaphoreType.DMA((2,)), pltpu.SemaphoreType.DMA((2,))],
        name="moe_combine_gather",
    )(table, idx)


def _sc_pack_row_pairs(w):
    R, C = w.shape
    lanes = plsc.get_sparse_core_info().num_lanes
    cr = SC_PACK_CHUNK_WORDS // C
    n_chunks = R // cr
    n_cores, n_workers = _sc_workers()
    per_w = n_chunks // n_workers
    assert R % cr == 0 and n_chunks % n_workers == 0 and per_w % 2 == 0 and cr % 2 == 0 and C % lanes == 0

    def body(w_hbm, out_hbm, in_v, out_v, lsem, ssem):
        wid = lax.axis_index("s") * n_cores + lax.axis_index("c")
        c0 = wid * per_w

        def load(c, b):
            return pltpu.make_async_copy(w_hbm.at[pl.ds((c0 + c) * cr, cr)], in_v.at[b], lsem.at[b])

        def store(c, b):
            return pltpu.make_async_copy(out_v.at[b], out_hbm.at[pl.ds((c0 + c) * (cr // 2), cr // 2)],
                                         ssem.at[b])

        load(0, 0).start()

        @pl.loop(0, per_w, step=2)
        def _(c):
            for b in range(2):
                cc = c + b
                load(cc, b).wait()

                @pl.when(cc + 1 < per_w)
                def _():
                    load(cc + 1, 1 - b).start()

                @pl.when(cc >= 2)
                def _():
                    store(cc - 2, b).wait()

                @pl.loop(0, cr // 2)
                def _(i):
                    @plsc.parallel_loop(0, C, step=lanes, unroll=4)
                    def _(col):
                        even = in_v[b, 2 * i, pl.ds(col, lanes)]
                        odd = in_v[b, 2 * i + 1, pl.ds(col, lanes)]
                        pair = plsc.pack(even, odd, format=plsc.PackFormat.INTERLEAVED)
                        out_v[b, i, pl.ds(col, lanes)] = plsc.bitcast(pair, I32)

                store(cc, b).start()

        store(per_w - 2, 0).wait()
        store(per_w - 1, 1).wait()

    mesh = plsc.VectorSubcoreMesh(core_axis_name="c", subcore_axis_name="s")
    return pl.kernel(
        body, out_type=jax.ShapeDtypeStruct((R // 2, C), I32), mesh=mesh,
        scratch_types=[pltpu.VMEM((2, cr, C), F32), pltpu.VMEM((2, cr // 2, C), I32),
                       pltpu.SemaphoreType.DMA((2,)), pltpu.SemaphoreType.DMA((2,))],
        compiler_params=pltpu.CompilerParams(needs_layout_passes=False),
        name="expert_weights_bf16",
    )(w)


def _expert_kernel(be_ref, ue_ref, nu_ref, xs_hbm, wg_hbm, wu_hbm, wd_hbm, y_hbm,
                   xbuf, ybuf, wg_v, wu_v, wd_v, xsem, ysem, wsem):
    n_used = nu_ref[0]
    n_exp = nu_ref[1]
    half = D_MODEL // 2
    blk = MOE_BLOCK
    rows_gu = D_MODEL // 2
    rows_d = EXPERT_DIM // 2

    def rows(j):
        return pl.ds(pl.multiple_of(j * blk, blk), blk)

    def x_copy(j, p):
        return pltpu.make_async_copy(xs_hbm.at[rows(j)], xbuf.at[p], xsem.at[p])

    def y_copy(j, p):
        return pltpu.make_async_copy(ybuf.at[p], y_hbm.at[rows(j)], ysem.at[p])

    def w_copies(q, s):
        e = ue_ref[q]
        gu = pl.ds(pl.multiple_of(e * rows_gu, rows_gu), rows_gu)
        dn = pl.ds(pl.multiple_of(e * rows_d, rows_d), rows_d)
        return (pltpu.make_async_copy(wg_hbm.at[gu], wg_v.at[s], wsem.at[s, 0]),
                pltpu.make_async_copy(wu_hbm.at[gu], wu_v.at[s], wsem.at[s, 1]),
                pltpu.make_async_copy(wd_hbm.at[dn], wd_v.at[s], wsem.at[s, 2]))

    for q0 in range(W_SLOTS - 1):
        @pl.when(q0 < n_exp)
        def _():
            for c in w_copies(q0, q0):
                c.start()

    for j0 in range(X_SLOTS):
        @pl.when(j0 < n_used)
        def _():
            x_copy(j0, j0).start()

    def block_step(j, p, q):
        is_new = (j == 0) | (be_ref[j] != be_ref[jnp.maximum(j - 1, 0)])
        q = q + is_new.astype(I32)

        s = q % W_SLOTS

        @pl.when(is_new)
        def _():
            for c in w_copies(q, s):
                c.wait()

            @pl.when(q + W_SLOTS - 1 < n_exp)
            def _():
                for c in w_copies(q + W_SLOTS - 1, (q + W_SLOTS - 1) % W_SLOTS):
                    c.start()

        x_copy(j, p).wait()

        @pl.when(j >= X_SLOTS)
        def _():
            y_copy(j - X_SLOTS, p).wait()

        wg = pltpu.bitcast(wg_v[s], BF16)
        wu = pltpu.bitcast(wu_v[s], BF16)
        wd = pltpu.bitcast(wd_v[s], BF16)
        lo, hi = _unpack_bf16_pairs(xbuf[p])
        xlo = lo.astype(BF16)
        xhi = hi.astype(BF16)
        g = (jnp.dot(xlo, wg[:half], preferred_element_type=F32)
             + jnp.dot(xhi, wg[half:], preferred_element_type=F32))
        u = (jnp.dot(xlo, wu[:half], preferred_element_type=F32)
             + jnp.dot(xhi, wu[half:], preferred_element_type=F32))
        hb = (g * jax.nn.sigmoid(g) * u).astype(BF16)
        ybuf[p] = _pack_bf16_pairs(jnp.dot(hb, wd, preferred_element_type=F32))
        y_copy(j, p).start()

        @pl.when(j + X_SLOTS < n_used)
        def _():
            x_copy(j + X_SLOTS, p).start()

        return q

    def group(m, q):
        q = block_step(X_SLOTS * m, 0, q)
        for p in range(1, X_SLOTS):
            j = X_SLOTS * m + p
            q = lax.cond(j < n_used, functools.partial(block_step, j, p), lambda q: q, q)
        return q

    lax.fori_loop(0, (n_used + X_SLOTS - 1) // X_SLOTS, group, jnp.int32(-1))

    for back in range(X_SLOTS, 0, -1):
        b = n_used - back
        for p in range(X_SLOTS):
            @pl.when((b >= 0) & (b % X_SLOTS == p))
            def _():
                y_copy(b, p).wait()


def _experts(block_e, used_e, counts2, xs, wg, wu, wd):
    n_rows, half = xs.shape
    smem = pl.BlockSpec(memory_space=pltpu.SMEM)
    hbm = pl.BlockSpec(memory_space=pl.ANY)
    return pl.pallas_call(
        _expert_kernel,
        in_specs=[smem, smem, smem, hbm, hbm, hbm, hbm],
        out_specs=hbm,
        out_shape=jax.ShapeDtypeStruct((n_rows, half), I32),
        scratch_shapes=[pltpu.VMEM((X_SLOTS, MOE_BLOCK, half), I32), pltpu.VMEM((X_SLOTS, MOE_BLOCK, half), I32),
                        pltpu.VMEM((W_SLOTS, D_MODEL // 2, EXPERT_DIM), I32),
                        pltpu.VMEM((W_SLOTS, D_MODEL // 2, EXPERT_DIM), I32),
                        pltpu.VMEM((W_SLOTS, EXPERT_DIM // 2, D_MODEL), I32),
                        pltpu.SemaphoreType.DMA((X_SLOTS,)), pltpu.SemaphoreType.DMA((X_SLOTS,)),
                        pltpu.SemaphoreType.DMA((W_SLOTS, 3))],
        compiler_params=pltpu.CompilerParams(vmem_limit_bytes=VMEM_LIMIT),
        name="moe_experts",
    )(block_e, used_e, counts2, xs, wg, wu, wd)


def _combine_kernel(yg_ref, gate_ref, x1_ref, swgu_ref, swd_ref, g2_ref, b2_ref, *rest, alpha):
    out_ref = rest[-1]
    tc = x1_ref.shape[0]
    x1 = x1_ref[...]
    gu = jnp.dot(x1.astype(BF16), swgu_ref[...], preferred_element_type=F32)
    g = gu[:, :EXPERT_DIM]
    u = gu[:, EXPERT_DIM:]
    shared = jnp.dot((g * jax.nn.sigmoid(g) * u).astype(BF16), swd_ref[...], preferred_element_type=F32)
    half = D_MODEL // 2
    acc_lo = jnp.zeros((tc, half), F32)
    acc_hi = jnp.zeros((tc, half), F32)
    gates = jnp.transpose(jnp.concatenate([gate_ref[...], jnp.zeros((tc - TOP_K, tc), F32)], axis=0))
    for k in range(TOP_K):
        lo, hi = _unpack_bf16_pairs(yg_ref[k])
        gk = gates[:, k:k + 1]
        acc_lo = acc_lo + gk * lo
        acc_hi = acc_hi + gk * hi
    routed = jnp.concatenate([acc_lo, acc_hi], axis=1)
    out_ref[...] = _layer_norm(alpha * x1 + (routed + shared), g2_ref[...], b2_ref[...])


def _combine(yg, gate, x1, tok0, swgu, swd, g2, b2, alpha, out_prev):
    T, D = x1.shape
    n_tok = gate.shape[1]
    tc = COMBINE_TILE
    first = tok0 // tc
    consts = (swgu, swd, g2, b2)
    args = [yg, gate, x1, *consts]
    in_specs = [pl.BlockSpec((TOP_K, tc, D // 2), lambda i: (0, i, 0)),
                pl.BlockSpec((TOP_K, tc), lambda i: (0, i)),
                pl.BlockSpec((tc, D), lambda i: (first + i, 0))] + [_const_spec(c.shape) for c in consts]
    aliases = {}
    if out_prev is not None:
        aliases = {len(args): 0}
        args.append(out_prev)
        in_specs.append(pl.BlockSpec(memory_space=pl.ANY))
    return pl.pallas_call(
        functools.partial(_combine_kernel, alpha=alpha),
        grid=(n_tok // tc,),
        in_specs=in_specs,
        out_specs=pl.BlockSpec((tc, D), lambda i: (first + i, 0)),
        out_shape=jax.ShapeDtypeStruct((T, D), F32),
        input_output_aliases=aliases,
        compiler_params=pltpu.CompilerParams(dimension_semantics=("arbitrary",), vmem_limit_bytes=VMEM_LIMIT),
        name="moe_combine_norm",
    )(*args)


def _mixer(x, in_w, in_b, ng, nb, spatial_w, spatial_b, proj_a_w, proj_b_w, out_w, ln1_g, ln1_b,
           router_w, alpha):
    gw, aw, D = GMLP_WIDTH, ATTN_WIDTH, D_MODEL
    w = in_w.astype(BF16)
    q0 = 2 * gw
    wuv, buv = w[:, :q0], in_b[None, :q0]
    watt = jnp.stack([jnp.concatenate([w[:, q0 + s * aw + p * GROUP_WIDTH:q0 + s * aw + (p + 1) * GROUP_WIDTH]
                                       for s in range(3)], axis=1) for p in range(len(DILATED_PATTERNS))])
    batt = jnp.stack([jnp.concatenate([in_b[q0 + s * aw + p * GROUP_WIDTH:q0 + s * aw + (p + 1) * GROUP_WIDTH]
                                       for s in range(3)])[None] for p in range(len(DILATED_PATTERNS))])
    g0 = q0 + 3 * aw
    wg, bg = w[:, g0:], in_b[None, g0:]
    sb = jnp.repeat(spatial_b.T, gw // GMLP_GROUPS, axis=1)
    ya, gb, a1, a4, a16 = _input_projection(
        x, wuv, buv, watt, batt, wg, bg, ng[None], nb[None], spatial_w, sb, proj_a_w.astype(BF16))
    attn_outs = [_dilated_attention(a, p, d) for p, (a, (_, d)) in enumerate(zip((a1, a4, a16), DILATED_PATTERNS))]
    rw_t = router_w.T
    rwh = rw_t.astype(BF16)
    rwl = (rw_t - rwh.astype(F32)).astype(BF16)
    return _merge_and_norm(attn_outs, ya, gb, x, proj_b_w.astype(BF16), out_w.astype(BF16),
                           ln1_g[None], ln1_b[None], rwh, rwl, alpha)


def _moe(x1, x1p, scores_t, router_bias, w_gate, w_up, w_down, sw_gate, sw_up, sw_down, ln2_g, ln2_b, alpha):
    swgu = jnp.concatenate([sw_gate, sw_up], axis=1).astype(BF16)
    swd = sw_down.astype(BF16)
    w_gate, w_up, w_down = (_sc_pack_row_pairs(w.reshape(-1, w.shape[-1])) for w in (w_gate, w_up, w_down))
    n_tok = x1.shape[0] // MOE_TOKEN_SPLITS
    A = n_tok * TOP_K
    n_blocks = (A + N_EXPERTS * (MOE_BLOCK - 1)) // MOE_BLOCK
    out = None
    for part in range(MOE_TOKEN_SPLITS):
        tok0 = part * n_tok
        eidx_t, rank_t, gate_t, counts = _route(scores_t, router_bias[:, None], tok0, n_tok)
        counts = counts[:, 0].astype(I32)
        padded = (counts + MOE_BLOCK - 1) // MOE_BLOCK * MOE_BLOCK
        pend = jnp.cumsum(padded).astype(I32)
        pstart = pend - padded
        block_starts = jnp.arange(n_blocks, dtype=I32) * MOE_BLOCK
        block_e = jnp.minimum(jnp.sum((pend[None, :] <= block_starts[:, None]).astype(I32), axis=1),
                              N_EXPERTS - 1)
        used = counts > 0
        used_e = jnp.argsort(jnp.logical_not(used), stable=True).astype(I32)
        counts2 = jnp.stack([pend[-1] // MOE_BLOCK, jnp.sum(used.astype(I32))]).astype(I32)
        dest_t = _dest_rows(eidx_t, rank_t, pstart)
        xs = _sc_scatter_rows(x1p, tok0, dest_t.reshape(TOP_K, n_tok // SC_CHUNK, SC_CHUNK).transpose(1, 0, 2),
                              n_blocks * MOE_BLOCK)
        y_rows = _experts(block_e, used_e, counts2, xs, w_gate, w_up, w_down)
        yg = _sc_gather_rows(y_rows, dest_t.reshape(A // SC_CHUNK, SC_CHUNK))
        out = _combine(yg.reshape(TOP_K, n_tok, D_MODEL // 2), gate_t, x1, tok0, swgu, swd,
                       ln2_g[None], ln2_b[None], alpha, out)
    return out


def kernel(x, in_w, in_b, gmlp_norm_g, gmlp_norm_b, spatial_w, spatial_b, proj_a_w, proj_b_w, out_w,
           ln1_g, ln1_b, router_w, router_bias, expert_w_gate, expert_w_up, expert_w_down,
           shared_w_gate, shared_w_up, shared_w_down, ln2_g, ln2_b):
    B, S, D = x.shape
    depth = in_w.shape[0]
    alpha = np.float32((2.0 * depth) ** 0.25)
    for l in range(depth):
        x1, x1p, scores_t = _mixer(x, in_w[l], in_b[l], gmlp_norm_g[l], gmlp_norm_b[l], spatial_w[l],
                                 spatial_b[l], proj_a_w[l], proj_b_w[l], out_w[l], ln1_g[l], ln1_b[l],
                                 router_w[l], alpha)
        out = _moe(x1.reshape(B * S, D), x1p.reshape(B * S, D // 2), scores_t,
                   router_bias[l], expert_w_gate[l], expert_w_up[l], expert_w_down[l],
                   shared_w_gate[l], shared_w_up[l], shared_w_down[l], ln2_g[l], ln2_b[l], alpha)
        x = out.reshape(B, S, D)
    return x
```

```python
import functools
import math

import numpy as np
import jax
import jax.numpy as jnp
from jax import lax
from jax.experimental import pallas as pl
from jax.experimental.pallas import tpu as pltpu
from jax.experimental.pallas import tpu_sc as plsc

F32 = jnp.float32
BF16 = jnp.bfloat16
U32 = jnp.uint32
I32 = jnp.int32

D_MODEL = 1024
GMLP_WIDTH = 1024
GMLP_GROUPS = 8
GMLP_CHUNK = 128
HEAD_DIM = 64
DILATED_PATTERNS = ((128, 1), (512, 4), (2048, 16))
HEADS_PER_GROUP = 4
GROUP_WIDTH = HEADS_PER_GROUP * HEAD_DIM
ATTN_WIDTH = GROUP_WIDTH * len(DILATED_PATTERNS)
ATTN_BLOCK = 128
N_EXPERTS = 256
TOP_K = 8
TOP_K_SHIFT = 3
N_EXPERT_GROUPS = 8
TOPK_GROUPS = 4
EXPERT_DIM = 256
ROUTED_SCALE = 2.5
LN_EPS = 1e-5
LANES = 128
MASKED_SCORE = -1e30

PROJ_TILE = 512
ATTN_QBLOCKS = 4
MERGE_TILE = 512
ROUTE_TILE = 512
DEST_TILE = 2048
MOE_BLOCK = 256
SC_PACK_CHUNK_WORDS = 16384
SC_CHUNK = 64
MOE_TOKEN_SPLITS = 2
X_SLOTS = 4
W_SLOTS = 3
COMBINE_TILE = 256
VMEM_LIMIT = 56 * 1024 * 1024


def _layer_norm(y, g, b):
    mu = jnp.mean(y, axis=-1, keepdims=True)
    yc = y - mu
    var = jnp.mean(yc * yc, axis=-1, keepdims=True)
    return yc * lax.rsqrt(var + LN_EPS) * g + b


def _gelu(x):
    return 0.5 * x * (1.0 + lax.erf(x * np.float32(math.sqrt(0.5))))


def _pack_bf16_pairs(x):
    w = x.shape[1] // 2
    bits = pltpu.bitcast(x.astype(BF16).astype(F32), U32)
    return pltpu.bitcast((bits[:, :w] >> 16) | (bits[:, w:] & jnp.uint32(0xFFFF0000)), I32)


def _unpack_bf16_pairs(words):
    w = pltpu.bitcast(words, U32)
    lo = pltpu.bitcast(w << 16, F32)
    hi = pltpu.bitcast(w & jnp.uint32(0xFFFF0000), F32)
    return lo, hi


def _const_spec(shape):
    nd = len(shape)
    return pl.BlockSpec(shape, lambda *_: (0,) * nd)


def _proj_kernel(x_ref, wuv_ref, buv_ref, watt_ref, batt_ref, wg_ref, bg_ref, ng_ref, nb_ref,
                 sw_ref, sb_ref, pa_ref, ya_ref, gb_ref, a1_ref, a4_ref, a16_ref, xc_ref):
    tm = x_ref.shape[1]
    gw = GMLP_WIDTH
    xb = x_ref[0].astype(BF16)

    def proj(w, b):
        return jnp.dot(xb, w, preferred_element_type=F32) + b

    u = _gelu(proj(wuv_ref[:, :gw], buv_ref[:, :gw]))
    v = _gelu(proj(wuv_ref[:, gw:], buv_ref[:, gw:]))
    v = _layer_norm(v, ng_ref[...], nb_ref[...]).astype(BF16)

    cw = gw // GMLP_GROUPS
    row = lax.broadcasted_iota(I32, (GMLP_CHUNK, GMLP_CHUNK), 0)
    col = lax.broadcasted_iota(I32, (GMLP_CHUNK, GMLP_CHUNK), 1)
    ws = [jnp.where(row >= col, sw_ref[g], 0.0).astype(BF16) for g in range(GMLP_GROUPS)]
    chunks = []
    for c in range(tm // GMLP_CHUNK):
        vc = v[c * GMLP_CHUNK:(c + 1) * GMLP_CHUNK]
        cols = [jnp.dot(ws[g], vc[:, g * cw:(g + 1) * cw], preferred_element_type=F32)
                for g in range(GMLP_GROUPS)]
        chunks.append(jnp.concatenate(cols, axis=1) + sb_ref[...])
    vmix = jnp.concatenate(chunks, axis=0)
    ya = jnp.dot((u * vmix).astype(BF16), pa_ref[...], preferred_element_type=F32)
    ga = jax.nn.sigmoid(proj(wg_ref[:, :D_MODEL], bg_ref[:, :D_MODEL]))
    ya_ref[0] = (ga * ya).astype(BF16)
    gb_ref[0] = jax.nn.sigmoid(proj(wg_ref[:, D_MODEL:], bg_ref[:, D_MODEL:])).astype(BF16)

    n_chunks = x_ref.shape[2] // LANES
    for c in range(n_chunks):
        xc_ref[c] = x_ref[0, :, c * LANES:(c + 1) * LANES]
    for p, (a_ref, (_, d)) in enumerate(zip((a1_ref, a4_ref, a16_ref), DILATED_PATTERNS)):
        n = tm // d
        if d == 1:
            xp = xb
        else:
            xp = jnp.concatenate(
                [jnp.concatenate([xc_ref[c, pl.ds(r, n, stride=d), :] for c in range(n_chunks)], axis=1)
                 for r in range(d)], axis=0).astype(BF16)
        h = (jnp.dot(xp, watt_ref[p], preferred_element_type=F32) + batt_ref[p]).astype(BF16)
        for r in range(d):
            a_ref[0, r] = h[r * n:(r + 1) * n]


def _input_projection(x, wuv, buv, watt, batt, wg, bg, ng, nb, sw, sb, pa):
    B, S, D = x.shape
    tm = PROJ_TILE
    grid = (B, S // tm)
    out_shape = [jax.ShapeDtypeStruct((B, S, D), BF16), jax.ShapeDtypeStruct((B, S, D), BF16)]
    out_specs = [pl.BlockSpec((1, tm, D), lambda b, t: (b, t, 0)),
                 pl.BlockSpec((1, tm, D), lambda b, t: (b, t, 0))]
    for _, d in DILATED_PATTERNS:
        out_shape.append(jax.ShapeDtypeStruct((B, d, S // d, ATTN_WIDTH), BF16))
        out_specs.append(pl.BlockSpec((1, d, tm // d, ATTN_WIDTH), lambda b, t: (b, 0, t, 0)))
    consts = (wuv, buv, watt, batt, wg, bg, ng, nb, sw, sb, pa)
    return pl.pallas_call(
        _proj_kernel,
        grid=grid,
        in_specs=[pl.BlockSpec((1, tm, D), lambda b, t: (b, t, 0))] + [_const_spec(c.shape) for c in consts],
        out_specs=out_specs,
        out_shape=out_shape,
        scratch_shapes=[pltpu.VMEM((D // LANES, tm, LANES), F32)],
        compiler_params=pltpu.CompilerParams(
            dimension_semantics=("arbitrary", "arbitrary"), vmem_limit_bytes=VMEM_LIMIT),
        name="input_projection",
    )(x, *consts)


def _attn_kernel(qkv_ref, bias_ref, o_ref, lse_ref, *, qblocks):
    nq = pl.program_id(2)
    gwid = GROUP_WIDTH
    blk = ATTN_BLOCK
    lane = lax.broadcasted_iota(I32, (1, gwid), 1)
    head_masks = [(lane >= h * HEAD_DIM) & (lane < (h + 1) * HEAD_DIM) for h in range(HEADS_PER_GROUP)]
    q_scales = [jnp.where(m, np.float32(HEAD_DIM ** -0.5), 0.0).astype(BF16) for m in head_masks]

    def rows_of(j):
        n = nq * qblocks + j
        return n, pl.multiple_of(n * blk, blk), pl.multiple_of(jnp.maximum(n - 1, 0) * blk, blk)

    def scores(j):
        n, q0, p0 = rows_of(j)
        q = qkv_ref[0, 0, pl.ds(q0, blk), 0:gwid]
        kk = jnp.concatenate([qkv_ref[0, 0, pl.ds(p0, blk), gwid:2 * gwid],
                              qkv_ref[0, 0, pl.ds(q0, blk), gwid:2 * gwid]], axis=0)
        qs = jnp.concatenate([q * s for s in q_scales], axis=0)
        s = lax.dot_general(qs, kk, (((1,), (1,)), ((), ())), preferred_element_type=F32)
        return s + bias_ref[jnp.where(n == 0, 1, 0)]

    def finish(j, s):
        _, q0, p0 = rows_of(j)
        vv = jnp.concatenate([qkv_ref[0, 0, pl.ds(p0, blk), 2 * gwid:3 * gwid],
                              qkv_ref[0, 0, pl.ds(q0, blk), 2 * gwid:3 * gwid]], axis=0)
        m = jnp.max(s, axis=1, keepdims=True)
        p = jnp.exp(s - m)
        den = jnp.sum(p, axis=1, keepdims=True)
        pv = jnp.dot(p.astype(BF16), vv, preferred_element_type=F32)
        on = pv / den
        lse = m + jnp.log(den)
        o = jnp.zeros((blk, gwid), F32)
        l = jnp.zeros((blk, gwid), F32)
        for h in range(HEADS_PER_GROUP):
            o = jnp.where(head_masks[h], on[h * blk:(h + 1) * blk], o)
            l = jnp.where(head_masks[h], lse[h * blk:(h + 1) * blk], l)
        o_ref[0, 0, j * blk:(j + 1) * blk, :] = o.astype(BF16)
        lse_ref[0, 0, j * blk:(j + 1) * blk, :] = l

    s_next = scores(0)
    for j in range(qblocks):
        s_cur = s_next
        if j + 1 < qblocks:
            s_next = scores(j + 1)
        finish(j, s_cur)


def _alibi_slopes(n):
    def pow2_slopes(m):
        start = 2.0 ** (-8.0 / m)
        return [start ** (i + 1) for i in range(m)]
    p = 2 ** int(math.floor(math.log2(n)))
    s = pow2_slopes(p)
    if p < n:
        s = s + pow2_slopes(2 * p)[0::2][: n - p]
    return np.array(sorted(s, reverse=True), dtype=np.float32)


def _attn_bias_tables(group, dilation):
    blk = ATTN_BLOCK
    slopes = _alibi_slopes(HEADS_PER_GROUP * len(DILATED_PATTERNS))
    slopes = slopes[group * HEADS_PER_GROUP:(group + 1) * HEADS_PER_GROUP]
    qi = np.arange(blk)[:, None]
    ki = np.arange(2 * blk)[None, :]
    delta = blk + qi - ki
    band = (delta >= 0) & (delta <= blk)
    bias = -slopes[:, None, None] * (delta * dilation).astype(np.float32)[None]
    full = np.where(band[None], bias, np.float32(MASKED_SCORE)).astype(np.float32)
    first = np.where((ki >= blk)[None], full, np.float32(MASKED_SCORE)).astype(np.float32)
    return full.reshape(HEADS_PER_GROUP * blk, 2 * blk), first.reshape(HEADS_PER_GROUP * blk, 2 * blk)


def _dilated_attention(qkv, group, dilation):
    B, d, sd, _ = qkv.shape
    qblocks = min(ATTN_QBLOCKS, sd // ATTN_BLOCK)
    rows = qblocks * ATTN_BLOCK
    bias = np.stack(_attn_bias_tables(group, dilation))
    grid = (B, d, sd // rows)
    out_spec = pl.BlockSpec((1, 1, rows, GROUP_WIDTH), lambda b, r, n: (b, r, n, 0))
    return pl.pallas_call(
        functools.partial(_attn_kernel, qblocks=qblocks),
        grid=grid,
        in_specs=[pl.BlockSpec((1, 1, sd, ATTN_WIDTH), lambda b, r, n: (b, r, 0, 0)),
                  _const_spec(bias.shape)],
        out_specs=[out_spec, out_spec],
        out_shape=[jax.ShapeDtypeStruct((B, d, sd, GROUP_WIDTH), BF16),
                   jax.ShapeDtypeStruct((B, d, sd, GROUP_WIDTH), F32)],
        compiler_params=pltpu.CompilerParams(
            dimension_semantics=("arbitrary", "arbitrary", "arbitrary"), vmem_limit_bytes=VMEM_LIMIT),
        name=f"dilated_attention_d{dilation}",
    )(qkv, jnp.asarray(bias))


def _merge_kernel(o1_ref, l1_ref, o4_ref, l4_ref, o16_ref, l16_ref, ya_ref, gb_ref, x_ref,
                  pb_ref, ow_ref, g1_ref, b1_ref, rwh_ref, rwl_ref, *rest, alpha, n_anchors):
    x1_ref, x1p_ref, sc_ref, so4, sl4, so16, sl16 = rest[n_anchors:]
    tm = x_ref.shape[1]
    n_chunks = GROUP_WIDTH // LANES
    for (o_ref, l_ref, so, sl, d) in ((o4_ref, l4_ref, so4, sl4, 4), (o16_ref, l16_ref, so16, sl16, 16)):
        n = tm // d
        for r in range(d):
            o_r = o_ref[0, r].astype(F32)
            l_r = l_ref[0, r]
            for c in range(n_chunks):
                so[c, pl.ds(r, n, stride=d), :] = o_r[:, c * LANES:(c + 1) * LANES]
                sl[c, pl.ds(r, n, stride=d), :] = l_r[:, c * LANES:(c + 1) * LANES]

    def natural(s):
        return jnp.concatenate([s[c] for c in range(n_chunks)], axis=1)

    l1 = l1_ref[0, 0]
    l4 = natural(sl4)
    l16 = natural(sl16)
    lmax = jnp.maximum(jnp.maximum(l1, l4), l16)
    e1 = jnp.exp(l1 - lmax)
    e4 = jnp.exp(l4 - lmax)
    e16 = jnp.exp(l16 - lmax)
    yb = (e1 * o1_ref[0, 0].astype(F32) + e4 * natural(so4) + e16 * natural(so16)) / (e1 + e4 + e16)
    ybp = jnp.dot(yb.astype(BF16), pb_ref[...], preferred_element_type=F32)
    merged = ya_ref[0].astype(F32) + gb_ref[0].astype(F32) * ybp
    mix = jnp.dot(merged.astype(BF16), ow_ref[...], preferred_element_type=F32)
    x1 = _layer_norm(alpha * x_ref[0] + mix, g1_ref[...], b1_ref[...])
    x1_ref[0] = x1
    x1p_ref[0] = _pack_bf16_pairs(x1)
    hi = x1.astype(BF16)
    lo = (x1 - hi.astype(F32)).astype(BF16)
    def logits_t(w_ref, xt):
        return lax.dot_general(w_ref[...], xt, (((1,), (1,)), ((), ())), preferred_element_type=F32)
    sc_ref[...] = jax.nn.sigmoid(logits_t(rwh_ref, hi) + logits_t(rwh_ref, lo) + logits_t(rwl_ref, hi))


def _merge_and_norm(attn_outs, ya, gb, x, pb, ow, g1, b1, rwh, rwl, alpha, anchors):
    B, S, D = x.shape
    tm = MERGE_TILE
    in_specs = []
    args = []
    for (o, l), (_, d) in zip(attn_outs, DILATED_PATTERNS):
        spec = pl.BlockSpec((1, d, tm // d, GROUP_WIDTH), lambda b, t: (b, 0, t, 0))
        in_specs += [spec, spec]
        args += [o, l]
    tok_spec = pl.BlockSpec((1, tm, D), lambda b, t: (b, t, 0))
    in_specs += [tok_spec, tok_spec, tok_spec]
    args += [ya, gb, x]
    consts = (pb, ow, g1, b1, rwh, rwl)
    in_specs += [_const_spec(c.shape) for c in consts]
    in_specs += [pl.BlockSpec(memory_space=pl.ANY) for _ in anchors]
    return pl.pallas_call(
        functools.partial(_merge_kernel, alpha=alpha, n_anchors=len(anchors)),
        grid=(B, S // tm),
        in_specs=in_specs,
        out_specs=[tok_spec,
                   pl.BlockSpec((1, tm, D // 2), lambda b, t: (b, t, 0)),
                   pl.BlockSpec((N_EXPERTS, tm), lambda b, t: (0, b * (S // tm) + t))],
        out_shape=[jax.ShapeDtypeStruct((B, S, D), F32),
                   jax.ShapeDtypeStruct((B, S, D // 2), I32),
                   jax.ShapeDtypeStruct((N_EXPERTS, B * S), F32)],
        scratch_shapes=[pltpu.VMEM((GROUP_WIDTH // LANES, tm, LANES), F32) for _ in range(4)],
        compiler_params=pltpu.CompilerParams(
            dimension_semantics=("arbitrary", "arbitrary"), vmem_limit_bytes=VMEM_LIMIT),
        name="merge_norm_router",
    )(*args, *consts, *anchors)


def _sortable_key(x):
    bits = pltpu.bitcast(x, I32)
    return jnp.where(bits < 0, bits ^ jnp.int32(0x7FFFFFFF), bits)


def _route_kernel(sc_ref, bias_ref, before_ref, eidx_ref, rank_ref, gate_ref, cnt_ref, carry_ref):
    ne, tm = sc_ref.shape
    gsize = ne // N_EXPERT_GROUPS
    neg_inf = np.float32(-np.inf)
    removed = jnp.int32(-2 ** 31)

    @pl.when(pl.program_id(0) == 0)
    def _():
        carry_ref[...] = jnp.zeros_like(carry_ref)

    scores = sc_ref[...]
    biased = scores + bias_ref[...]

    gsum = []
    for g in range(N_EXPERT_GROUPS):
        v = biased[g * gsize:(g + 1) * gsize]
        m1 = jnp.max(v, axis=0, keepdims=True)
        n1 = jnp.sum(jnp.where(v == m1, 1.0, 0.0), axis=0, keepdims=True)
        m2 = jnp.max(jnp.where(v < m1, v, neg_inf), axis=0, keepdims=True)
        gsum.append(m1 + jnp.where(n1 >= 2.0, m1, m2))
    gkey = _sortable_key(jnp.concatenate(gsum, axis=0))

    def pick_first_max(keys, ids, n_ids):
        m = jnp.max(keys, axis=0, keepdims=True)
        idx = jnp.min(jnp.where(keys == m, ids, n_ids), axis=0, keepdims=True)
        hit = ids == idx
        return idx, hit, jnp.where(hit, removed, keys)

    gid = lax.broadcasted_iota(I32, (N_EXPERT_GROUPS, tm), 0)
    for _ in range(TOPK_GROUPS):
        _, _, gkey = pick_first_max(gkey, gid, N_EXPERT_GROUPS)
    group_on = gkey == removed

    masked = jnp.concatenate(
        [jnp.where(group_on[g:g + 1], biased[g * gsize:(g + 1) * gsize], neg_inf)
         for g in range(N_EXPERT_GROUPS)], axis=0)
    keys = _sortable_key(masked)
    eid = lax.broadcasted_iota(I32, (ne, tm), 0)
    picks = []
    for _ in range(TOP_K):
        idx, _, keys = pick_first_max(keys, eid, ne)
        picks.append(idx)

    sel = jnp.where(keys == removed, 1.0, 0.0)
    ranks = jnp.dot(sel.astype(BF16), before_ref[...], preferred_element_type=F32) + carry_ref[...]
    carry_ref[...] = carry_ref[...] + jnp.sum(sel, axis=1, keepdims=True)
    cnt_ref[...] = carry_ref[...]

    s_k, r_k = [], []
    for idx in picks:
        hit = eid == idx
        s_k.append(jnp.sum(jnp.where(hit, scores, 0.0), axis=0, keepdims=True))
        r_k.append(jnp.sum(jnp.where(hit, ranks, 0.0), axis=0, keepdims=True))
    total = s_k[0]
    for s in s_k[1:]:
        total = total + s
    eidx_ref[...] = jnp.concatenate(picks, axis=0)
    rank_ref[...] = jnp.concatenate(r_k, axis=0).astype(I32)
    gate_ref[...] = jnp.concatenate([s / total * np.float32(ROUTED_SCALE) for s in s_k], axis=0)


def _route(scores_t, bias, tok0, T):
    ne = scores_t.shape[0]
    tm = ROUTE_TILE
    first = tok0 // tm
    before = jnp.asarray(np.triu(np.ones((tm, tm), np.float32), k=1), BF16)
    out_spec = pl.BlockSpec((TOP_K, tm), lambda i: (0, i))
    return pl.pallas_call(
        _route_kernel,
        grid=(T // tm,),
        in_specs=[pl.BlockSpec((ne, tm), lambda i: (0, first + i)), _const_spec(bias.shape),
                  _const_spec(before.shape)],
        out_specs=[out_spec, out_spec, out_spec, _const_spec((ne, 1))],
        out_shape=[jax.ShapeDtypeStruct((TOP_K, T), I32), jax.ShapeDtypeStruct((TOP_K, T), I32),
                   jax.ShapeDtypeStruct((TOP_K, T), F32), jax.ShapeDtypeStruct((ne, 1), F32)],
        scratch_shapes=[pltpu.VMEM((ne, 1), F32)],
        compiler_params=pltpu.CompilerParams(dimension_semantics=("arbitrary",), vmem_limit_bytes=VMEM_LIMIT),
        name="route_topk",
    )(scores_t, bias, before)


def _dest_kernel(pstart_ref, eidx_ref, rank_ref, out_ref):
    eidx = eidx_ref[...]
    start = jnp.zeros(eidx.shape, I32)
    for e in range(N_EXPERTS):
        start = jnp.where(eidx == e, pstart_ref[e], start)
    out_ref[...] = start + rank_ref[...]


def _dest_rows(eidx_t, rank_t, pstart):
    T = eidx_t.shape[1]
    tm = DEST_TILE
    tok_spec = pl.BlockSpec((TOP_K, tm), lambda i: (0, i))
    return pl.pallas_call(
        _dest_kernel,
        grid=(T // tm,),
        in_specs=[pl.BlockSpec(memory_space=pltpu.SMEM), tok_spec, tok_spec],
        out_specs=tok_spec,
        out_shape=jax.ShapeDtypeStruct((TOP_K, T), I32),
        compiler_params=pltpu.CompilerParams(dimension_semantics=("arbitrary",)),
        name="moe_dest_rows",
    )(pstart, eidx_t, rank_t)


def _sc_workers():
    info = plsc.get_sparse_core_info()
    return info.num_cores, info.num_cores * info.num_subcores


def _sc_scatter_rows(rows, row0, dest, n_out):
    n_chunks, n_dst, ch = dest.shape
    width = rows.shape[1]
    n_cores, n_workers = _sc_workers()
    per_w = n_chunks // n_workers
    assert n_chunks % n_workers == 0 and per_w % 2 == 0 and row0 + n_chunks * ch <= rows.shape[0]

    def body(rows_hbm, dest_hbm, out_hbm, idx_v, buf, lsem, ssem):
        wid = lax.axis_index("s") * n_cores + lax.axis_index("c")
        c0 = wid * per_w
        pltpu.sync_copy(dest_hbm.at[pl.ds(c0, per_w)], idx_v)

        def load(c, b):
            return pltpu.make_async_copy(rows_hbm.at[pl.ds(row0 + (c0 + c) * ch, ch)], buf.at[b], lsem.at[b])

        def scatter(c, b, k):
            return pltpu.make_async_copy(buf.at[b], out_hbm.at[idx_v.at[c, k]], ssem.at[b])

        load(0, 0).start()

        @pl.loop(0, per_w, step=2)
        def _(c):
            for b in range(2):
                cc = c + b
                load(cc, b).wait()

                @pl.when(cc >= 1)
                def _():
                    for k in range(n_dst):
                        scatter(cc - 1, 1 - b, k).wait()

                @pl.when(cc + 1 < per_w)
                def _():
                    load(cc + 1, 1 - b).start()

                for k in range(n_dst):
                    scatter(cc, b, k).start()

        for k in range(n_dst):
            scatter(per_w - 1, 1, k).wait()

    mesh = plsc.VectorSubcoreMesh(core_axis_name="c", subcore_axis_name="s")
    return pl.kernel(
        body, out_type=jax.ShapeDtypeStruct((n_out, width), rows.dtype), mesh=mesh,
        scratch_types=[pltpu.VMEM((per_w, n_dst, ch), I32), pltpu.VMEM((2, ch, width), rows.dtype),
                       pltpu.SemaphoreType.DMA((2,)), pltpu.SemaphoreType.DMA((2,))],
        name="moe_dispatch_scatter",
    )(rows, dest)


def _sc_gather_rows(table, idx):
    n_chunks, ch = idx.shape
    width = table.shape[1]
    n_cores, n_workers = _sc_workers()
    per_w = n_chunks // n_workers
    assert n_chunks % n_workers == 0 and per_w % 2 == 0

    def body(table_hbm, idx_hbm, out_hbm, idx_v, buf, gsem, wsem):
        wid = lax.axis_index("s") * n_cores + lax.axis_index("c")
        c0 = wid * per_w
        pltpu.sync_copy(idx_hbm.at[pl.ds(c0, per_w)], idx_v)

        def gather(c, b):
            return pltpu.make_async_copy(table_hbm.at[idx_v.at[c]], buf.at[b], gsem.at[b])

        def write(c, b):
            return pltpu.make_async_copy(buf.at[b], out_hbm.at[pl.ds((c0 + c) * ch, ch)], wsem.at[b])

        gather(0, 0).start()

        @pl.loop(0, per_w, step=2)
        def _(c):
            for b in range(2):
                cc = c + b
                gather(cc, b).wait()

                @pl.when(cc >= 1)
                def _():
                    write(cc - 1, 1 - b).wait()

                @pl.when(cc + 1 < per_w)
                def _():
                    gather(cc + 1, 1 - b).start()

                write(cc, b).start()

        write(per_w - 1, 1).wait()

    mesh = plsc.VectorSubcoreMesh(core_axis_name="c", subcore_axis_name="s")
    return pl.kernel(
        body, out_type=jax.ShapeDtypeStruct((n_chunks * ch, width), table.dtype), mesh=mesh,
        scratch_types=[pltpu.VMEM((per_w, ch), I32), pltpu.VMEM((2, ch, width), table.dtype),
                       pltpu.SemaphoreType.DMA((2,)), pltpu.SemaphoreType.DMA((2,))],
        name="moe_combine_gather",
    )(table, idx)


def _sc_pack_row_pairs(w):
    R, C = w.shape
    lanes = plsc.get_sparse_core_info().num_lanes
    cr = SC_PACK_CHUNK_WORDS // C
    n_chunks = R // cr
    n_cores, n_workers = _sc_workers()
    per_w = n_chunks // n_workers
    assert R % cr == 0 and n_chunks % n_workers == 0 and per_w % 2 == 0 and cr % 2 == 0 and C % lanes == 0

    def body(w_hbm, out_hbm, in_v, out_v, lsem, ssem):
        wid = lax.axis_index("s") * n_cores + lax.axis_index("c")
        c0 = wid * per_w

        def load(c, b):
            return pltpu.make_async_copy(w_hbm.at[pl.ds((c0 + c) * cr, cr)], in_v.at[b], lsem.at[b])

        def store(c, b):
            return pltpu.make_async_copy(out_v.at[b], out_hbm.at[pl.ds((c0 + c) * (cr // 2), cr // 2)],
                                         ssem.at[b])

        load(0, 0).start()

        @pl.loop(0, per_w, step=2)
        def _(c):
            for b in range(2):
                cc = c + b
                load(cc, b).wait()

                @pl.when(cc + 1 < per_w)
                def _():
                    load(cc + 1, 1 - b).start()

                @pl.when(cc >= 2)
                def _():
                    store(cc - 2, b).wait()

                @pl.loop(0, cr // 2)
                def _(i):
                    @plsc.parallel_loop(0, C, step=lanes, unroll=4)
                    def _(col):
                        even = in_v[b, 2 * i, pl.ds(col, lanes)]
                        odd = in_v[b, 2 * i + 1, pl.ds(col, lanes)]
                        pair = plsc.pack(even, odd, format=plsc.PackFormat.INTERLEAVED)
                        out_v[b, i, pl.ds(col, lanes)] = plsc.bitcast(pair, I32)

                store(cc, b).start()

        store(per_w - 2, 0).wait()
        store(per_w - 1, 1).wait()

    mesh = plsc.VectorSubcoreMesh(core_axis_name="c", subcore_axis_name="s")
    return pl.kernel(
        body, out_type=jax.ShapeDtypeStruct((R // 2, C), I32), mesh=mesh,
        scratch_types=[pltpu.VMEM((2, cr, C), F32), pltpu.VMEM((2, cr // 2, C), I32),
                       pltpu.SemaphoreType.DMA((2,)), pltpu.SemaphoreType.DMA((2,))],
        compiler_params=pltpu.CompilerParams(needs_layout_passes=False),
        name="expert_weights_bf16",
    )(w)


def _expert_kernel(be_ref, ue_ref, nu_ref, xs_hbm, wg_hbm, wu_hbm, wd_hbm, y_hbm,
                   xbuf, ybuf, wg_v, wu_v, wd_v, xsem, ysem, wsem):
    n_used = nu_ref[0]
    n_exp = nu_ref[1]
    half = D_MODEL // 2
    blk = MOE_BLOCK
    rows_gu = D_MODEL // 2
    rows_d = EXPERT_DIM // 2

    def rows(j):
        return pl.ds(pl.multiple_of(j * blk, blk), blk)

    def x_copy(j, p):
        return pltpu.make_async_copy(xs_hbm.at[rows(j)], xbuf.at[p], xsem.at[p])

    def y_copy(j, p):
        return pltpu.make_async_copy(ybuf.at[p], y_hbm.at[rows(j)], ysem.at[p])

    def w_copies(q, s):
        e = ue_ref[q]
        gu = pl.ds(pl.multiple_of(e * rows_gu, rows_gu), rows_gu)
        dn = pl.ds(pl.multiple_of(e * rows_d, rows_d), rows_d)
        return (pltpu.make_async_copy(wg_hbm.at[gu], wg_v.at[s], wsem.at[s, 0]),
                pltpu.make_async_copy(wu_hbm.at[gu], wu_v.at[s], wsem.at[s, 1]),
                pltpu.make_async_copy(wd_hbm.at[dn], wd_v.at[s], wsem.at[s, 2]))

    for q0 in range(W_SLOTS - 1):
        @pl.when(q0 < n_exp)
        def _():
            for c in w_copies(q0, q0):
                c.start()

    for j0 in range(X_SLOTS):
        @pl.when(j0 < n_used)
        def _():
            x_copy(j0, j0).start()

    def block_step(j, p, q):
        is_new = (j == 0) | (be_ref[j] != be_ref[jnp.maximum(j - 1, 0)])
        q = q + is_new.astype(I32)

        s = q % W_SLOTS

        @pl.when(is_new)
        def _():
            for c in w_copies(q, s):
                c.wait()

            @pl.when(q + W_SLOTS - 1 < n_exp)
            def _():
                for c in w_copies(q + W_SLOTS - 1, (q + W_SLOTS - 1) % W_SLOTS):
                    c.start()

        x_copy(j, p).wait()

        @pl.when(j >= X_SLOTS)
        def _():
            y_copy(j - X_SLOTS, p).wait()

        wg = pltpu.bitcast(wg_v[s], BF16)
        wu = pltpu.bitcast(wu_v[s], BF16)
        wd = pltpu.bitcast(wd_v[s], BF16)
        lo, hi = _unpack_bf16_pairs(xbuf[p])
        xlo = lo.astype(BF16)
        xhi = hi.astype(BF16)
        g = (jnp.dot(xlo, wg[:half], preferred_element_type=F32)
             + jnp.dot(xhi, wg[half:], preferred_element_type=F32))
        u = (jnp.dot(xlo, wu[:half], preferred_element_type=F32)
             + jnp.dot(xhi, wu[half:], preferred_element_type=F32))
        hb = (g * jax.nn.sigmoid(g) * u).astype(BF16)
        ybuf[p] = _pack_bf16_pairs(jnp.dot(hb, wd, preferred_element_type=F32))
        y_copy(j, p).start()

        @pl.when(j + X_SLOTS < n_used)
        def _():
            x_copy(j + X_SLOTS, p).start()

        return q

    def group(m, q):
        q = block_step(X_SLOTS * m, 0, q)
        for p in range(1, X_SLOTS):
            j = X_SLOTS * m + p
            q = lax.cond(j < n_used, functools.partial(block_step, j, p), lambda q: q, q)
        return q

    lax.fori_loop(0, (n_used + X_SLOTS - 1) // X_SLOTS, group, jnp.int32(-1))

    for back in range(X_SLOTS, 0, -1):
        b = n_used - back
        for p in range(X_SLOTS):
            @pl.when((b >= 0) & (b % X_SLOTS == p))
            def _():
                y_copy(b, p).wait()


def _experts(block_e, used_e, counts2, xs, wg, wu, wd):
    n_rows, half = xs.shape
    smem = pl.BlockSpec(memory_space=pltpu.SMEM)
    hbm = pl.BlockSpec(memory_space=pl.ANY)
    return pl.pallas_call(
        _expert_kernel,
        in_specs=[smem, smem, smem, hbm, hbm, hbm, hbm],
        out_specs=hbm,
        out_shape=jax.ShapeDtypeStruct((n_rows, half), I32),
        scratch_shapes=[pltpu.VMEM((X_SLOTS, MOE_BLOCK, half), I32), pltpu.VMEM((X_SLOTS, MOE_BLOCK, half), I32),
                        pltpu.VMEM((W_SLOTS, D_MODEL // 2, EXPERT_DIM), I32),
                        pltpu.VMEM((W_SLOTS, D_MODEL // 2, EXPERT_DIM), I32),
                        pltpu.VMEM((W_SLOTS, EXPERT_DIM // 2, D_MODEL), I32),
                        pltpu.SemaphoreType.DMA((X_SLOTS,)), pltpu.SemaphoreType.DMA((X_SLOTS,)),
                        pltpu.SemaphoreType.DMA((W_SLOTS, 3))],
        compiler_params=pltpu.CompilerParams(vmem_limit_bytes=VMEM_LIMIT),
        name="moe_experts",
    )(block_e, used_e, counts2, xs, wg, wu, wd)


def _combine_kernel(yg_ref, gate_ref, x1_ref, swgu_ref, swd_ref, g2_ref, b2_ref, *rest, alpha):
    out_ref = rest[-1]
    tc = x1_ref.shape[0]
    x1 = x1_ref[...]
    gu = jnp.dot(x1.astype(BF16), swgu_ref[...], preferred_element_type=F32)
    g = gu[:, :EXPERT_DIM]
    u = gu[:, EXPERT_DIM:]
    shared = jnp.dot((g * jax.nn.sigmoid(g) * u).astype(BF16), swd_ref[...], preferred_element_type=F32)
    half = D_MODEL // 2
    acc_lo = jnp.zeros((tc, half), F32)
    acc_hi = jnp.zeros((tc, half), F32)
    gates = jnp.transpose(jnp.concatenate([gate_ref[...], jnp.zeros((tc - TOP_K, tc), F32)], axis=0))
    for k in range(TOP_K):
        lo, hi = _unpack_bf16_pairs(yg_ref[k])
        gk = gates[:, k:k + 1]
        acc_lo = acc_lo + gk * lo
        acc_hi = acc_hi + gk * hi
    routed = jnp.concatenate([acc_lo, acc_hi], axis=1)
    out_ref[...] = _layer_norm(alpha * x1 + (routed + shared), g2_ref[...], b2_ref[...])


def _combine(yg, gate, x1, tok0, swgu, swd, g2, b2, alpha, out_prev):
    T, D = x1.shape
    n_tok = gate.shape[1]
    tc = COMBINE_TILE
    first = tok0 // tc
    consts = (swgu, swd, g2, b2)
    args = [yg, gate, x1, *consts]
    in_specs = [pl.BlockSpec((TOP_K, tc, D // 2), lambda i: (0, i, 0)),
                pl.BlockSpec((TOP_K, tc), lambda i: (0, i)),
                pl.BlockSpec((tc, D), lambda i: (first + i, 0))] + [_const_spec(c.shape) for c in consts]
    aliases = {}
    if out_prev is not None:
        aliases = {len(args): 0}
        args.append(out_prev)
        in_specs.append(pl.BlockSpec(memory_space=pl.ANY))
    return pl.pallas_call(
        functools.partial(_combine_kernel, alpha=alpha),
        grid=(n_tok // tc,),
        in_specs=in_specs,
        out_specs=pl.BlockSpec((tc, D), lambda i: (first + i, 0)),
        out_shape=jax.ShapeDtypeStruct((T, D), F32),
        input_output_aliases=aliases,
        compiler_params=pltpu.CompilerParams(dimension_semantics=("arbitrary",), vmem_limit_bytes=VMEM_LIMIT),
        name="moe_combine_norm",
    )(*args)


def _mixer(x, in_w, in_b, ng, nb, spatial_w, spatial_b, proj_a_w, proj_b_w, out_w, ln1_g, ln1_b,
           router_w, alpha, anchors):
    gw, aw, D = GMLP_WIDTH, ATTN_WIDTH, D_MODEL
    w = in_w.astype(BF16)
    q0 = 2 * gw
    wuv, buv = w[:, :q0], in_b[None, :q0]
    watt = jnp.stack([jnp.concatenate([w[:, q0 + s * aw + p * GROUP_WIDTH:q0 + s * aw + (p + 1) * GROUP_WIDTH]
                                       for s in range(3)], axis=1) for p in range(len(DILATED_PATTERNS))])
    batt = jnp.stack([jnp.concatenate([in_b[q0 + s * aw + p * GROUP_WIDTH:q0 + s * aw + (p + 1) * GROUP_WIDTH]
                                       for s in range(3)])[None] for p in range(len(DILATED_PATTERNS))])
    g0 = q0 + 3 * aw
    wg, bg = w[:, g0:], in_b[None, g0:]
    sb = jnp.repeat(spatial_b.T, gw // GMLP_GROUPS, axis=1)
    ya, gb, a1, a4, a16 = _input_projection(
        x, wuv, buv, watt, batt, wg, bg, ng[None], nb[None], spatial_w, sb, proj_a_w.astype(BF16))
    attn_outs = [_dilated_attention(a, p, d) for p, (a, (_, d)) in enumerate(zip((a1, a4, a16), DILATED_PATTERNS))]
    rw_t = router_w.T
    rwh = rw_t.astype(BF16)
    rwl = (rw_t - rwh.astype(F32)).astype(BF16)
    return _merge_and_norm(attn_outs, ya, gb, x, proj_b_w.astype(BF16), out_w.astype(BF16),
                           ln1_g[None], ln1_b[None], rwh, rwl, alpha, anchors)


def _moe(x1, x1p, scores_t, router_bias, w_gate, w_up, w_down, sw_gate, sw_up, sw_down, ln2_g, ln2_b, alpha):
    swgu = jnp.concatenate([sw_gate, sw_up], axis=1).astype(BF16)
    swd = sw_down.astype(BF16)
    n_tok = x1.shape[0] // MOE_TOKEN_SPLITS
    A = n_tok * TOP_K
    n_blocks = (A + N_EXPERTS * (MOE_BLOCK - 1)) // MOE_BLOCK
    out = None
    for part in range(MOE_TOKEN_SPLITS):
        tok0 = part * n_tok
        eidx_t, rank_t, gate_t, counts = _route(scores_t, router_bias[:, None], tok0, n_tok)
        counts = counts[:, 0].astype(I32)
        padded = (counts + MOE_BLOCK - 1) // MOE_BLOCK * MOE_BLOCK
        pend = jnp.cumsum(padded).astype(I32)
        pstart = pend - padded
        block_starts = jnp.arange(n_blocks, dtype=I32) * MOE_BLOCK
        block_e = jnp.minimum(jnp.sum((pend[None, :] <= block_starts[:, None]).astype(I32), axis=1),
                              N_EXPERTS - 1)
        used = counts > 0
        used_e = jnp.argsort(jnp.logical_not(used), stable=True).astype(I32)
        counts2 = jnp.stack([pend[-1] // MOE_BLOCK, jnp.sum(used.astype(I32))]).astype(I32)
        dest_t = _dest_rows(eidx_t, rank_t, pstart)
        xs = _sc_scatter_rows(x1p, tok0, dest_t.reshape(TOP_K, n_tok // SC_CHUNK, SC_CHUNK).transpose(1, 0, 2),
                              n_blocks * MOE_BLOCK)
        y_rows = _experts(block_e, used_e, counts2, xs, w_gate, w_up, w_down)
        yg = _sc_gather_rows(y_rows, dest_t.reshape(A // SC_CHUNK, SC_CHUNK))
        out = _combine(yg.reshape(TOP_K, n_tok, D_MODEL // 2), gate_t, x1, tok0, swgu, swd,
                       ln2_g[None], ln2_b[None], alpha, out)
    return out


def kernel(x, in_w, in_b, gmlp_norm_g, gmlp_norm_b, spatial_w, spatial_b, proj_a_w, proj_b_w, out_w,
           ln1_g, ln1_b, router_w, router_bias, expert_w_gate, expert_w_up, expert_w_down,
           shared_w_gate, shared_w_up, shared_w_down, ln2_g, ln2_b):
    B, S, D = x.shape
    depth = in_w.shape[0]
    alpha = np.float32((2.0 * depth) ** 0.25)
    for l in range(depth):
        packed = [_sc_pack_row_pairs(w[l].reshape(-1, w.shape[-1]))
                  for w in (expert_w_gate, expert_w_up, expert_w_down)]
        x1, x1p, scores_t = _mixer(x, in_w[l], in_b[l], gmlp_norm_g[l], gmlp_norm_b[l], spatial_w[l],
                                 spatial_b[l], proj_a_w[l], proj_b_w[l], out_w[l], ln1_g[l], ln1_b[l],
                                 router_w[l], alpha, [p[:8] for p in packed])
        out = _moe(x1.reshape(B * S, D), x1p.reshape(B * S, D // 2), scores_t,
                   router_bias[l], *packed,
                   shared_w_gate[l], shared_w_up[l], shared_w_down[l], ln2_g[l], ln2_b[l], alpha)
        x = out.reshape(B, S, D)
    return x
```

```python
import functools
import math

import numpy as np
import jax
import jax.numpy as jnp
from jax import lax
from jax.experimental import pallas as pl
from jax.experimental.pallas import tpu as pltpu
from jax.experimental.pallas import tpu_sc as plsc

F32 = jnp.float32
BF16 = jnp.bfloat16
U32 = jnp.uint32
I32 = jnp.int32

D_MODEL = 1024
GMLP_WIDTH = 1024
GMLP_GROUPS = 8
GMLP_CHUNK = 128
HEAD_DIM = 64
DILATED_PATTERNS = ((128, 1), (512, 4), (2048, 16))
HEADS_PER_GROUP = 4
GROUP_WIDTH = HEADS_PER_GROUP * HEAD_DIM
ATTN_WIDTH = GROUP_WIDTH * len(DILATED_PATTERNS)
ATTN_BLOCK = 128
N_EXPERTS = 256
TOP_K = 8
TOP_K_SHIFT = 3
N_EXPERT_GROUPS = 8
TOPK_GROUPS = 4
EXPERT_DIM = 256
ROUTED_SCALE = 2.5
LN_EPS = 1e-5
LANES = 128
MASKED_SCORE = -1e30

PROJ_TILE = 512
ATTN_QBLOCKS = 4
MERGE_TILE = 512
ROUTE_TILE = 512
DEST_TILE = 2048
MOE_BLOCK = 256
SC_PACK_CHUNK_WORDS = 16384
SC_CHUNK = 64
SC_SUM_CHUNK = 8
MOE_TOKEN_SPLITS = 2
X_SLOTS = 4
W_SLOTS = 3
COMBINE_TILE = 256
VMEM_LIMIT = 56 * 1024 * 1024


def _layer_norm(y, g, b):
    mu = jnp.mean(y, axis=-1, keepdims=True)
    yc = y - mu
    var = jnp.mean(yc * yc, axis=-1, keepdims=True)
    return yc * lax.rsqrt(var + LN_EPS) * g + b


def _gelu(x):
    return 0.5 * x * (1.0 + lax.erf(x * np.float32(math.sqrt(0.5))))


def _pack_bf16_pairs(x):
    w = x.shape[1] // 2
    bits = pltpu.bitcast(x.astype(BF16).astype(F32), U32)
    return pltpu.bitcast((bits[:, :w] >> 16) | (bits[:, w:] & jnp.uint32(0xFFFF0000)), I32)


def _unpack_bf16_pairs(words):
    w = pltpu.bitcast(words, U32)
    lo = pltpu.bitcast(w << 16, F32)
    hi = pltpu.bitcast(w & jnp.uint32(0xFFFF0000), F32)
    return lo, hi


def _const_spec(shape):
    nd = len(shape)
    return pl.BlockSpec(shape, lambda *_: (0,) * nd)


def _proj_kernel(x_ref, wuv_ref, buv_ref, watt_ref, batt_ref, wg_ref, bg_ref, ng_ref, nb_ref,
                 sw_ref, sb_ref, pa_ref, ya_ref, gb_ref, a1_ref, a4_ref, a16_ref, xc_ref):
    tm = x_ref.shape[1]
    gw = GMLP_WIDTH
    xb = x_ref[0].astype(BF16)

    def proj(w, b):
        return jnp.dot(xb, w, preferred_element_type=F32) + b

    u = _gelu(proj(wuv_ref[:, :gw], buv_ref[:, :gw]))
    v = _gelu(proj(wuv_ref[:, gw:], buv_ref[:, gw:]))
    v = _layer_norm(v, ng_ref[...], nb_ref[...]).astype(BF16)

    cw = gw // GMLP_GROUPS
    row = lax.broadcasted_iota(I32, (GMLP_CHUNK, GMLP_CHUNK), 0)
    col = lax.broadcasted_iota(I32, (GMLP_CHUNK, GMLP_CHUNK), 1)
    ws = [jnp.where(row >= col, sw_ref[g], 0.0).astype(BF16) for g in range(GMLP_GROUPS)]
    chunks = []
    for c in range(tm // GMLP_CHUNK):
        vc = v[c * GMLP_CHUNK:(c + 1) * GMLP_CHUNK]
        cols = [jnp.dot(ws[g], vc[:, g * cw:(g + 1) * cw], preferred_element_type=F32)
                for g in range(GMLP_GROUPS)]
        chunks.append(jnp.concatenate(cols, axis=1) + sb_ref[...])
    vmix = jnp.concatenate(chunks, axis=0)
    ya = jnp.dot((u * vmix).astype(BF16), pa_ref[...], preferred_element_type=F32)
    ga = jax.nn.sigmoid(proj(wg_ref[:, :D_MODEL], bg_ref[:, :D_MODEL]))
    ya_ref[0] = (ga * ya).astype(BF16)
    gb_ref[0] = jax.nn.sigmoid(proj(wg_ref[:, D_MODEL:], bg_ref[:, D_MODEL:])).astype(BF16)

    n_chunks = x_ref.shape[2] // LANES
    for c in range(n_chunks):
        xc_ref[c] = x_ref[0, :, c * LANES:(c + 1) * LANES]
    for p, (a_ref, (_, d)) in enumerate(zip((a1_ref, a4_ref, a16_ref), DILATED_PATTERNS)):
        n = tm // d
        if d == 1:
            xp = xb
        else:
            xp = jnp.concatenate(
                [jnp.concatenate([xc_ref[c, pl.ds(r, n, stride=d), :] for c in range(n_chunks)], axis=1)
                 for r in range(d)], axis=0).astype(BF16)
        h = (jnp.dot(xp, watt_ref[p], preferred_element_type=F32) + batt_ref[p]).astype(BF16)
        for r in range(d):
            a_ref[0, r] = h[r * n:(r + 1) * n]


def _input_projection(x, wuv, buv, watt, batt, wg, bg, ng, nb, sw, sb, pa):
    B, S, D = x.shape
    tm = PROJ_TILE
    grid = (B, S // tm)
    out_shape = [jax.ShapeDtypeStruct((B, S, D), BF16), jax.ShapeDtypeStruct((B, S, D), BF16)]
    out_specs = [pl.BlockSpec((1, tm, D), lambda b, t: (b, t, 0)),
                 pl.BlockSpec((1, tm, D), lambda b, t: (b, t, 0))]
    for _, d in DILATED_PATTERNS:
        out_shape.append(jax.ShapeDtypeStruct((B, d, S // d, ATTN_WIDTH), BF16))
        out_specs.append(pl.BlockSpec((1, d, tm // d, ATTN_WIDTH), lambda b, t: (b, 0, t, 0)))
    consts = (wuv, buv, watt, batt, wg, bg, ng, nb, sw, sb, pa)
    return pl.pallas_call(
        _proj_kernel,
        grid=grid,
        in_specs=[pl.BlockSpec((1, tm, D), lambda b, t: (b, t, 0))] + [_const_spec(c.shape) for c in consts],
        out_specs=out_specs,
        out_shape=out_shape,
        scratch_shapes=[pltpu.VMEM((D // LANES, tm, LANES), F32)],
        compiler_params=pltpu.CompilerParams(
            dimension_semantics=("arbitrary", "arbitrary"), vmem_limit_bytes=VMEM_LIMIT),
        name="input_projection",
    )(x, *consts)


def _attn_kernel(qkv_ref, bias_ref, o_ref, lse_ref, *, qblocks):
    nq = pl.program_id(2)
    gwid = GROUP_WIDTH
    blk = ATTN_BLOCK
    lane = lax.broadcasted_iota(I32, (1, gwid), 1)
    head_masks = [(lane >= h * HEAD_DIM) & (lane < (h + 1) * HEAD_DIM) for h in range(HEADS_PER_GROUP)]
    q_scales = [jnp.where(m, np.float32(HEAD_DIM ** -0.5), 0.0).astype(BF16) for m in head_masks]

    def rows_of(j):
        n = nq * qblocks + j
        return n, pl.multiple_of(n * blk, blk), pl.multiple_of(jnp.maximum(n - 1, 0) * blk, blk)

    def scores(j):
        n, q0, p0 = rows_of(j)
        q = qkv_ref[0, 0, pl.ds(q0, blk), 0:gwid]
        kk = jnp.concatenate([qkv_ref[0, 0, pl.ds(p0, blk), gwid:2 * gwid],
                              qkv_ref[0, 0, pl.ds(q0, blk), gwid:2 * gwid]], axis=0)
        qs = jnp.concatenate([q * s for s in q_scales], axis=0)
        s = lax.dot_general(qs, kk, (((1,), (1,)), ((), ())), preferred_element_type=F32)
        return s + bias_ref[jnp.where(n == 0, 1, 0)]

    def finish(j, s):
        _, q0, p0 = rows_of(j)
        vv = jnp.concatenate([qkv_ref[0, 0, pl.ds(p0, blk), 2 * gwid:3 * gwid],
                              qkv_ref[0, 0, pl.ds(q0, blk), 2 * gwid:3 * gwid]], axis=0)
        m = jnp.max(s, axis=1, keepdims=True)
        p = jnp.exp(s - m)
        den = jnp.sum(p, axis=1, keepdims=True)
        pv = jnp.dot(p.astype(BF16), vv, preferred_element_type=F32)
        on = pv / den
        lse = m + jnp.log(den)
        o = jnp.zeros((blk, gwid), F32)
        l = jnp.zeros((blk, gwid), F32)
        for h in range(HEADS_PER_GROUP):
            o = jnp.where(head_masks[h], on[h * blk:(h + 1) * blk], o)
            l = jnp.where(head_masks[h], lse[h * blk:(h + 1) * blk], l)
        o_ref[0, 0, j * blk:(j + 1) * blk, :] = o.astype(BF16)
        lse_ref[0, 0, j * blk:(j + 1) * blk, :] = l

    s_next = scores(0)
    for j in range(qblocks):
        s_cur = s_next
        if j + 1 < qblocks:
            s_next = scores(j + 1)
        finish(j, s_cur)


def _alibi_slopes(n):
    def pow2_slopes(m):
        start = 2.0 ** (-8.0 / m)
        return [start ** (i + 1) for i in range(m)]
    p = 2 ** int(math.floor(math.log2(n)))
    s = pow2_slopes(p)
    if p < n:
        s = s + pow2_slopes(2 * p)[0::2][: n - p]
    return np.array(sorted(s, reverse=True), dtype=np.float32)


def _attn_bias_tables(group, dilation):
    blk = ATTN_BLOCK
    slopes = _alibi_slopes(HEADS_PER_GROUP * len(DILATED_PATTERNS))
    slopes = slopes[group * HEADS_PER_GROUP:(group + 1) * HEADS_PER_GROUP]
    qi = np.arange(blk)[:, None]
    ki = np.arange(2 * blk)[None, :]
    delta = blk + qi - ki
    band = (delta >= 0) & (delta <= blk)
    bias = -slopes[:, None, None] * (delta * dilation).astype(np.float32)[None]
    full = np.where(band[None], bias, np.float32(MASKED_SCORE)).astype(np.float32)
    first = np.where((ki >= blk)[None], full, np.float32(MASKED_SCORE)).astype(np.float32)
    return full.reshape(HEADS_PER_GROUP * blk, 2 * blk), first.reshape(HEADS_PER_GROUP * blk, 2 * blk)


def _dilated_attention(qkv, group, dilation):
    B, d, sd, _ = qkv.shape
    qblocks = min(ATTN_QBLOCKS, sd // ATTN_BLOCK)
    rows = qblocks * ATTN_BLOCK
    bias = np.stack(_attn_bias_tables(group, dilation))
    grid = (B, d, sd // rows)
    out_spec = pl.BlockSpec((1, 1, rows, GROUP_WIDTH), lambda b, r, n: (b, r, n, 0))
    return pl.pallas_call(
        functools.partial(_attn_kernel, qblocks=qblocks),
        grid=grid,
        in_specs=[pl.BlockSpec((1, 1, sd, ATTN_WIDTH), lambda b, r, n: (b, r, 0, 0)),
                  _const_spec(bias.shape)],
        out_specs=[out_spec, out_spec],
        out_shape=[jax.ShapeDtypeStruct((B, d, sd, GROUP_WIDTH), BF16),
                   jax.ShapeDtypeStruct((B, d, sd, GROUP_WIDTH), F32)],
        compiler_params=pltpu.CompilerParams(
            dimension_semantics=("arbitrary", "arbitrary", "arbitrary"), vmem_limit_bytes=VMEM_LIMIT),
        name=f"dilated_attention_d{dilation}",
    )(qkv, jnp.asarray(bias))


def _merge_kernel(o1_ref, l1_ref, o4_ref, l4_ref, o16_ref, l16_ref, ya_ref, gb_ref, x_ref,
                  pb_ref, ow_ref, g1_ref, b1_ref, rwh_ref, rwl_ref, *rest, alpha, n_anchors):
    x1_ref, x1p_ref, sc_ref, so4, sl4, so16, sl16 = rest[n_anchors:]
    tm = x_ref.shape[1]
    n_chunks = GROUP_WIDTH // LANES
    for (o_ref, l_ref, so, sl, d) in ((o4_ref, l4_ref, so4, sl4, 4), (o16_ref, l16_ref, so16, sl16, 16)):
        n = tm // d
        for r in range(d):
            o_r = o_ref[0, r].astype(F32)
            l_r = l_ref[0, r]
            for c in range(n_chunks):
                so[c, pl.ds(r, n, stride=d), :] = o_r[:, c * LANES:(c + 1) * LANES]
                sl[c, pl.ds(r, n, stride=d), :] = l_r[:, c * LANES:(c + 1) * LANES]

    def natural(s):
        return jnp.concatenate([s[c] for c in range(n_chunks)], axis=1)

    l1 = l1_ref[0, 0]
    l4 = natural(sl4)
    l16 = natural(sl16)
    lmax = jnp.maximum(jnp.maximum(l1, l4), l16)
    e1 = jnp.exp(l1 - lmax)
    e4 = jnp.exp(l4 - lmax)
    e16 = jnp.exp(l16 - lmax)
    yb = (e1 * o1_ref[0, 0].astype(F32) + e4 * natural(so4) + e16 * natural(so16)) / (e1 + e4 + e16)
    ybp = jnp.dot(yb.astype(BF16), pb_ref[...], preferred_element_type=F32)
    merged = ya_ref[0].astype(F32) + gb_ref[0].astype(F32) * ybp
    mix = jnp.dot(merged.astype(BF16), ow_ref[...], preferred_element_type=F32)
    x1 = _layer_norm(alpha * x_ref[0] + mix, g1_ref[...], b1_ref[...])
    x1_ref[0] = x1
    x1p_ref[0] = _pack_bf16_pairs(x1)
    hi = x1.astype(BF16)
    lo = (x1 - hi.astype(F32)).astype(BF16)
    def logits_t(w_ref, xt):
        return lax.dot_general(w_ref[...], xt, (((1,), (1,)), ((), ())), preferred_element_type=F32)
    sc_ref[...] = jax.nn.sigmoid(logits_t(rwh_ref, hi) + logits_t(rwh_ref, lo) + logits_t(rwl_ref, hi))


def _merge_and_norm(attn_outs, ya, gb, x, pb, ow, g1, b1, rwh, rwl, alpha, anchors):
    B, S, D = x.shape
    tm = MERGE_TILE
    in_specs = []
    args = []
    for (o, l), (_, d) in zip(attn_outs, DILATED_PATTERNS):
        spec = pl.BlockSpec((1, d, tm // d, GROUP_WIDTH), lambda b, t: (b, 0, t, 0))
        in_specs += [spec, spec]
        args += [o, l]
    tok_spec = pl.BlockSpec((1, tm, D), lambda b, t: (b, t, 0))
    in_specs += [tok_spec, tok_spec, tok_spec]
    args += [ya, gb, x]
    consts = (pb, ow, g1, b1, rwh, rwl)
    in_specs += [_const_spec(c.shape) for c in consts]
    in_specs += [pl.BlockSpec(memory_space=pl.ANY) for _ in anchors]
    return pl.pallas_call(
        functools.partial(_merge_kernel, alpha=alpha, n_anchors=len(anchors)),
        grid=(B, S // tm),
        in_specs=in_specs,
        out_specs=[tok_spec,
                   pl.BlockSpec((1, tm, D // 2), lambda b, t: (b, t, 0)),
                   pl.BlockSpec((N_EXPERTS, tm), lambda b, t: (0, b * (S // tm) + t))],
        out_shape=[jax.ShapeDtypeStruct((B, S, D), F32),
                   jax.ShapeDtypeStruct((B, S, D // 2), I32),
                   jax.ShapeDtypeStruct((N_EXPERTS, B * S), F32)],
        scratch_shapes=[pltpu.VMEM((GROUP_WIDTH // LANES, tm, LANES), F32) for _ in range(4)],
        compiler_params=pltpu.CompilerParams(
            dimension_semantics=("arbitrary", "arbitrary"), vmem_limit_bytes=VMEM_LIMIT),
        name="merge_norm_router",
    )(*args, *consts, *anchors)


def _sortable_key(x):
    bits = pltpu.bitcast(x, I32)
    return jnp.where(bits < 0, bits ^ jnp.int32(0x7FFFFFFF), bits)


def _route_kernel(sc_ref, bias_ref, before_ref, eidx_ref, rank_ref, gate_ref, cnt_ref, carry_ref):
    ne, tm = sc_ref.shape
    gsize = ne // N_EXPERT_GROUPS
    neg_inf = np.float32(-np.inf)
    removed = jnp.int32(-2 ** 31)

    @pl.when(pl.program_id(0) == 0)
    def _():
        carry_ref[...] = jnp.zeros_like(carry_ref)

    scores = sc_ref[...]
    biased = scores + bias_ref[...]

    gsum = []
    for g in range(N_EXPERT_GROUPS):
        v = biased[g * gsize:(g + 1) * gsize]
        m1 = jnp.max(v, axis=0, keepdims=True)
        n1 = jnp.sum(jnp.where(v == m1, 1.0, 0.0), axis=0, keepdims=True)
        m2 = jnp.max(jnp.where(v < m1, v, neg_inf), axis=0, keepdims=True)
        gsum.append(m1 + jnp.where(n1 >= 2.0, m1, m2))
    gkey = _sortable_key(jnp.concatenate(gsum, axis=0))

    def pick_first_max(keys, ids, n_ids):
        m = jnp.max(keys, axis=0, keepdims=True)
        idx = jnp.min(jnp.where(keys == m, ids, n_ids), axis=0, keepdims=True)
        hit = ids == idx
        return idx, hit, jnp.where(hit, removed, keys)

    gid = lax.broadcasted_iota(I32, (N_EXPERT_GROUPS, tm), 0)
    for _ in range(TOPK_GROUPS):
        _, _, gkey = pick_first_max(gkey, gid, N_EXPERT_GROUPS)
    group_on = gkey == removed

    masked = jnp.concatenate(
        [jnp.where(group_on[g:g + 1], biased[g * gsize:(g + 1) * gsize], neg_inf)
         for g in range(N_EXPERT_GROUPS)], axis=0)
    keys = _sortable_key(masked)
    eid = lax.broadcasted_iota(I32, (ne, tm), 0)
    picks = []
    for _ in range(TOP_K):
        idx, _, keys = pick_first_max(keys, eid, ne)
        picks.append(idx)

    sel = jnp.where(keys == removed, 1.0, 0.0)
    ranks = jnp.dot(sel.astype(BF16), before_ref[...], preferred_element_type=F32) + carry_ref[...]
    carry_ref[...] = carry_ref[...] + jnp.sum(sel, axis=1, keepdims=True)
    cnt_ref[...] = carry_ref[...]

    s_k, r_k = [], []
    for idx in picks:
        hit = eid == idx
        s_k.append(jnp.sum(jnp.where(hit, scores, 0.0), axis=0, keepdims=True))
        r_k.append(jnp.sum(jnp.where(hit, ranks, 0.0), axis=0, keepdims=True))
    total = s_k[0]
    for s in s_k[1:]:
        total = total + s
    eidx_ref[...] = jnp.concatenate(picks, axis=0)
    rank_ref[...] = jnp.concatenate(r_k, axis=0).astype(I32)
    gate_ref[...] = jnp.concatenate([s / total * np.float32(ROUTED_SCALE) for s in s_k], axis=0)


def _route(scores_t, bias, tok0, T):
    ne = scores_t.shape[0]
    tm = ROUTE_TILE
    first = tok0 // tm
    before = jnp.asarray(np.triu(np.ones((tm, tm), np.float32), k=1), BF16)
    out_spec = pl.BlockSpec((TOP_K, tm), lambda i: (0, i))
    return pl.pallas_call(
        _route_kernel,
        grid=(T // tm,),
        in_specs=[pl.BlockSpec((ne, tm), lambda i: (0, first + i)), _const_spec(bias.shape),
                  _const_spec(before.shape)],
        out_specs=[out_spec, out_spec, out_spec, _const_spec((ne, 1))],
        out_shape=[jax.ShapeDtypeStruct((TOP_K, T), I32), jax.ShapeDtypeStruct((TOP_K, T), I32),
                   jax.ShapeDtypeStruct((TOP_K, T), F32), jax.ShapeDtypeStruct((ne, 1), F32)],
        scratch_shapes=[pltpu.VMEM((ne, 1), F32)],
        compiler_params=pltpu.CompilerParams(dimension_semantics=("arbitrary",), vmem_limit_bytes=VMEM_LIMIT),
        name="route_topk",
    )(scores_t, bias, before)


def _dest_kernel(pstart_ref, eidx_ref, rank_ref, out_ref):
    eidx = eidx_ref[...]
    start = jnp.zeros(eidx.shape, I32)
    for e in range(N_EXPERTS):
        start = jnp.where(eidx == e, pstart_ref[e], start)
    out_ref[...] = start + rank_ref[...]


def _dest_rows(eidx_t, rank_t, pstart):
    T = eidx_t.shape[1]
    tm = DEST_TILE
    tok_spec = pl.BlockSpec((TOP_K, tm), lambda i: (0, i))
    return pl.pallas_call(
        _dest_kernel,
        grid=(T // tm,),
        in_specs=[pl.BlockSpec(memory_space=pltpu.SMEM), tok_spec, tok_spec],
        out_specs=tok_spec,
        out_shape=jax.ShapeDtypeStruct((TOP_K, T), I32),
        compiler_params=pltpu.CompilerParams(dimension_semantics=("arbitrary",)),
        name="moe_dest_rows",
    )(pstart, eidx_t, rank_t)


def _sc_workers():
    info = plsc.get_sparse_core_info()
    return info.num_cores, info.num_cores * info.num_subcores


def _sc_scatter_rows(rows, row0, dest, n_out):
    n_chunks, n_dst, ch = dest.shape
    width = rows.shape[1]
    n_cores, n_workers = _sc_workers()
    per_w = n_chunks // n_workers
    assert n_chunks % n_workers == 0 and per_w % 2 == 0 and row0 + n_chunks * ch <= rows.shape[0]

    def body(rows_hbm, dest_hbm, out_hbm, idx_v, buf, lsem, ssem):
        wid = lax.axis_index("s") * n_cores + lax.axis_index("c")
        c0 = wid * per_w
        pltpu.sync_copy(dest_hbm.at[pl.ds(c0, per_w)], idx_v)

        def load(c, b):
            return pltpu.make_async_copy(rows_hbm.at[pl.ds(row0 + (c0 + c) * ch, ch)], buf.at[b], lsem.at[b])

        def scatter(c, b, k):
            return pltpu.make_async_copy(buf.at[b], out_hbm.at[idx_v.at[c, k]], ssem.at[b])

        load(0, 0).start()

        @pl.loop(0, per_w, step=2)
        def _(c):
            for b in range(2):
                cc = c + b
                load(cc, b).wait()

                @pl.when(cc >= 1)
                def _():
                    for k in range(n_dst):
                        scatter(cc - 1, 1 - b, k).wait()

                @pl.when(cc + 1 < per_w)
                def _():
                    load(cc + 1, 1 - b).start()

                for k in range(n_dst):
                    scatter(cc, b, k).start()

        for k in range(n_dst):
            scatter(per_w - 1, 1, k).wait()

    mesh = plsc.VectorSubcoreMesh(core_axis_name="c", subcore_axis_name="s")
    return pl.kernel(
        body, out_type=jax.ShapeDtypeStruct((n_out, width), rows.dtype), mesh=mesh,
        scratch_types=[pltpu.VMEM((per_w, n_dst, ch), I32), pltpu.VMEM((2, ch, width), rows.dtype),
                       pltpu.SemaphoreType.DMA((2,)), pltpu.SemaphoreType.DMA((2,))],
        name="moe_dispatch_scatter",
    )(rows, dest)


def _sc_gather_weighted_sum(table, idx, gates):
    n_chunks = idx.shape[0]
    n_src, ct = TOP_K, SC_SUM_CHUNK
    half = table.shape[1]
    lanes = plsc.get_sparse_core_info().num_lanes
    n_cores, n_workers = _sc_workers()
    per_w = n_chunks // n_workers
    assert n_chunks % n_workers == 0 and per_w % 2 == 0 and half % lanes == 0

    def body(table_hbm, idx_hbm, gate_hbm, out_hbm, idx_v, gate_v, buf, out_v, gsem, wsem):
        wid = lax.axis_index("s") * n_cores + lax.axis_index("c")
        c0 = wid * per_w
        pltpu.sync_copy(idx_hbm.at[pl.ds(c0, per_w)], idx_v)
        pltpu.sync_copy(gate_hbm.at[pl.ds(c0 * (ct * n_src), per_w * ct * n_src)], gate_v)
        shift = jnp.full((lanes,), 16, I32)
        hi_mask = jnp.full((lanes,), -65536, I32)

        def gather(c, b):
            return pltpu.make_async_copy(table_hbm.at[idx_v.at[c]], buf.at[b], gsem.at[b])

        def write(c, b):
            return pltpu.make_async_copy(out_v.at[b], out_hbm.at[pl.ds((c0 + c) * ct, ct)], wsem.at[b])

        gather(0, 0).start()

        @pl.loop(0, per_w, step=2)
        def _(c):
            for b in range(2):
                cc = c + b
                gather(cc, b).wait()

                @pl.when(cc + 1 < per_w)
                def _():
                    gather(cc + 1, 1 - b).start()

                @pl.when(cc >= 2)
                def _():
                    write(cc - 2, b).wait()

                @pl.loop(0, ct)
                def _(t):
                    g = [plsc.load_gather(gate_v, [jnp.full((lanes,), (cc * ct + t) * n_src + k, I32)])
                         for k in range(n_src)]

                    @plsc.parallel_loop(0, half, step=lanes, unroll=2)
                    def _(col):
                        lo = jnp.zeros((lanes,), F32)
                        hi = jnp.zeros((lanes,), F32)
                        for k in range(n_src):
                            w = buf[b, k * ct + t, pl.ds(col, lanes)]
                            lo = lo + g[k] * plsc.bitcast(lax.shift_left(w, shift), F32)
                            hi = hi + g[k] * plsc.bitcast(w & hi_mask, F32)
                        out_v[b, t, pl.ds(col, lanes)] = lo
                        out_v[b, t, pl.ds(half + col, lanes)] = hi

                write(cc, b).start()

        write(per_w - 2, 0).wait()
        write(per_w - 1, 1).wait()

    mesh = plsc.VectorSubcoreMesh(core_axis_name="c", subcore_axis_name="s")
    return pl.kernel(
        body, out_type=jax.ShapeDtypeStruct((n_chunks * ct, 2 * half), F32), mesh=mesh,
        scratch_types=[pltpu.VMEM((per_w, n_src * ct), I32), pltpu.VMEM((per_w * ct * n_src,), F32),
                       pltpu.VMEM((2, n_src * ct, half), I32), pltpu.VMEM((2, ct, 2 * half), F32),
                       pltpu.SemaphoreType.DMA((2,)), pltpu.SemaphoreType.DMA((2,))],
        compiler_params=pltpu.CompilerParams(needs_layout_passes=False),
        name="moe_combine_gather_sum",
    )(table, idx, gates)


def _sc_pack_row_pairs(w):
    R, C = w.shape
    lanes = plsc.get_sparse_core_info().num_lanes
    cr = SC_PACK_CHUNK_WORDS // C
    n_chunks = R // cr
    n_cores, n_workers = _sc_workers()
    per_w = n_chunks // n_workers
    assert R % cr == 0 and n_chunks % n_workers == 0 and per_w % 2 == 0 and cr % 2 == 0 and C % lanes == 0

    def body(w_hbm, out_hbm, in_v, out_v, lsem, ssem):
        wid = lax.axis_index("s") * n_cores + lax.axis_index("c")
        c0 = wid * per_w

        def load(c, b):
            return pltpu.make_async_copy(w_hbm.at[pl.ds((c0 + c) * cr, cr)], in_v.at[b], lsem.at[b])

        def store(c, b):
            return pltpu.make_async_copy(out_v.at[b], out_hbm.at[pl.ds((c0 + c) * (cr // 2), cr // 2)],
                                         ssem.at[b])

        load(0, 0).start()

        @pl.loop(0, per_w, step=2)
        def _(c):
            for b in range(2):
                cc = c + b
                load(cc, b).wait()

                @pl.when(cc + 1 < per_w)
                def _():
                    load(cc + 1, 1 - b).start()

                @pl.when(cc >= 2)
                def _():
                    store(cc - 2, b).wait()

                @pl.loop(0, cr // 2)
                def _(i):
                    @plsc.parallel_loop(0, C, step=lanes, unroll=4)
                    def _(col):
                        even = in_v[b, 2 * i, pl.ds(col, lanes)]
                        odd = in_v[b, 2 * i + 1, pl.ds(col, lanes)]
                        pair = plsc.pack(even, odd, format=plsc.PackFormat.INTERLEAVED)
                        out_v[b, i, pl.ds(col, lanes)] = plsc.bitcast(pair, I32)

                store(cc, b).start()

        store(per_w - 2, 0).wait()
        store(per_w - 1, 1).wait()

    mesh = plsc.VectorSubcoreMesh(core_axis_name="c", subcore_axis_name="s")
    return pl.kernel(
        body, out_type=jax.ShapeDtypeStruct((R // 2, C), I32), mesh=mesh,
        scratch_types=[pltpu.VMEM((2, cr, C), F32), pltpu.VMEM((2, cr // 2, C), I32),
                       pltpu.SemaphoreType.DMA((2,)), pltpu.SemaphoreType.DMA((2,))],
        compiler_params=pltpu.CompilerParams(needs_layout_passes=False),
        name="expert_weights_bf16",
    )(w)


def _expert_kernel(be_ref, ue_ref, nu_ref, xs_hbm, wg_hbm, wu_hbm, wd_hbm, y_hbm,
                   xbuf, ybuf, wg_v, wu_v, wd_v, xsem, ysem, wsem):
    n_used = nu_ref[0]
    n_exp = nu_ref[1]
    half = D_MODEL // 2
    blk = MOE_BLOCK
    rows_gu = D_MODEL // 2
    rows_d = EXPERT_DIM // 2

    def rows(j):
        return pl.ds(pl.multiple_of(j * blk, blk), blk)

    def x_copy(j, p):
        return pltpu.make_async_copy(xs_hbm.at[rows(j)], xbuf.at[p], xsem.at[p])

    def y_copy(j, p):
        return pltpu.make_async_copy(ybuf.at[p], y_hbm.at[rows(j)], ysem.at[p])

    def w_copies(q, s):
        e = ue_ref[q]
        gu = pl.ds(pl.multiple_of(e * rows_gu, rows_gu), rows_gu)
        dn = pl.ds(pl.multiple_of(e * rows_d, rows_d), rows_d)
        return (pltpu.make_async_copy(wg_hbm.at[gu], wg_v.at[s], wsem.at[s, 0]),
                pltpu.make_async_copy(wu_hbm.at[gu], wu_v.at[s], wsem.at[s, 1]),
                pltpu.make_async_copy(wd_hbm.at[dn], wd_v.at[s], wsem.at[s, 2]))

    for q0 in range(W_SLOTS - 1):
        @pl.when(q0 < n_exp)
        def _():
            for c in w_copies(q0, q0):
                c.start()

    for j0 in range(X_SLOTS):
        @pl.when(j0 < n_used)
        def _():
            x_copy(j0, j0).start()

    def block_step(j, p, q):
        is_new = (j == 0) | (be_ref[j] != be_ref[jnp.maximum(j - 1, 0)])
        q = q + is_new.astype(I32)

        s = q % W_SLOTS

        @pl.when(is_new)
        def _():
            for c in w_copies(q, s):
                c.wait()

            @pl.when(q + W_SLOTS - 1 < n_exp)
            def _():
                for c in w_copies(q + W_SLOTS - 1, (q + W_SLOTS - 1) % W_SLOTS):
                    c.start()

        x_copy(j, p).wait()

        @pl.when(j >= X_SLOTS)
        def _():
            y_copy(j - X_SLOTS, p).wait()

        wg = pltpu.bitcast(wg_v[s], BF16)
        wu = pltpu.bitcast(wu_v[s], BF16)
        wd = pltpu.bitcast(wd_v[s], BF16)
        lo, hi = _unpack_bf16_pairs(xbuf[p])
        xlo = lo.astype(BF16)
        xhi = hi.astype(BF16)
        g = (jnp.dot(xlo, wg[:half], preferred_element_type=F32)
             + jnp.dot(xhi, wg[half:], preferred_element_type=F32))
        u = (jnp.dot(xlo, wu[:half], preferred_element_type=F32)
             + jnp.dot(xhi, wu[half:], preferred_element_type=F32))
        hb = (g * jax.nn.sigmoid(g) * u).astype(BF16)
        ybuf[p] = _pack_bf16_pairs(jnp.dot(hb, wd, preferred_element_type=F32))
        y_copy(j, p).start()

        @pl.when(j + X_SLOTS < n_used)
        def _():
            x_copy(j + X_SLOTS, p).start()

        return q

    def group(m, q):
        q = block_step(X_SLOTS * m, 0, q)
        for p in range(1, X_SLOTS):
            j = X_SLOTS * m + p
            q = lax.cond(j < n_used, functools.partial(block_step, j, p), lambda q: q, q)
        return q

    lax.fori_loop(0, (n_used + X_SLOTS - 1) // X_SLOTS, group, jnp.int32(-1))

    for back in range(X_SLOTS, 0, -1):
        b = n_used - back
        for p in range(X_SLOTS):
            @pl.when((b >= 0) & (b % X_SLOTS == p))
            def _():
                y_copy(b, p).wait()


def _experts(block_e, used_e, counts2, xs, wg, wu, wd):
    n_rows, half = xs.shape
    smem = pl.BlockSpec(memory_space=pltpu.SMEM)
    hbm = pl.BlockSpec(memory_space=pl.ANY)
    return pl.pallas_call(
        _expert_kernel,
        in_specs=[smem, smem, smem, hbm, hbm, hbm, hbm],
        out_specs=hbm,
        out_shape=jax.ShapeDtypeStruct((n_rows, half), I32),
        scratch_shapes=[pltpu.VMEM((X_SLOTS, MOE_BLOCK, half), I32), pltpu.VMEM((X_SLOTS, MOE_BLOCK, half), I32),
                        pltpu.VMEM((W_SLOTS, D_MODEL // 2, EXPERT_DIM), I32),
                        pltpu.VMEM((W_SLOTS, D_MODEL // 2, EXPERT_DIM), I32),
                        pltpu.VMEM((W_SLOTS, EXPERT_DIM // 2, D_MODEL), I32),
                        pltpu.SemaphoreType.DMA((X_SLOTS,)), pltpu.SemaphoreType.DMA((X_SLOTS,)),
                        pltpu.SemaphoreType.DMA((W_SLOTS, 3))],
        compiler_params=pltpu.CompilerParams(vmem_limit_bytes=VMEM_LIMIT),
        name="moe_experts",
    )(block_e, used_e, counts2, xs, wg, wu, wd)


def _combine_kernel(routed_ref, x1_ref, swgu_ref, swd_ref, g2_ref, b2_ref, *rest, alpha):
    out_ref = rest[-1]
    x1 = x1_ref[...]
    gu = jnp.dot(x1.astype(BF16), swgu_ref[...], preferred_element_type=F32)
    g = gu[:, :EXPERT_DIM]
    u = gu[:, EXPERT_DIM:]
    shared = jnp.dot((g * jax.nn.sigmoid(g) * u).astype(BF16), swd_ref[...], preferred_element_type=F32)
    out_ref[...] = _layer_norm(alpha * x1 + (routed_ref[...] + shared), g2_ref[...], b2_ref[...])


def _combine(routed, x1, tok0, swgu, swd, g2, b2, alpha, out_prev):
    T, D = x1.shape
    n_tok = routed.shape[0]
    tc = COMBINE_TILE
    first = tok0 // tc
    consts = (swgu, swd, g2, b2)
    args = [routed, x1, *consts]
    in_specs = [pl.BlockSpec((tc, D), lambda i: (i, 0)),
                pl.BlockSpec((tc, D), lambda i: (first + i, 0))] + [_const_spec(c.shape) for c in consts]
    aliases = {}
    if out_prev is not None:
        aliases = {len(args): 0}
        args.append(out_prev)
        in_specs.append(pl.BlockSpec(memory_space=pl.ANY))
    return pl.pallas_call(
        functools.partial(_combine_kernel, alpha=alpha),
        grid=(n_tok // tc,),
        in_specs=in_specs,
        out_specs=pl.BlockSpec((tc, D), lambda i: (first + i, 0)),
        out_shape=jax.ShapeDtypeStruct((T, D), F32),
        input_output_aliases=aliases,
        compiler_params=pltpu.CompilerParams(dimension_semantics=("arbitrary",), vmem_limit_bytes=VMEM_LIMIT),
        name="moe_combine_norm",
    )(*args)


def _mixer(x, in_w, in_b, ng, nb, spatial_w, spatial_b, proj_a_w, proj_b_w, out_w, ln1_g, ln1_b,
           router_w, alpha, anchors):
    gw, aw, D = GMLP_WIDTH, ATTN_WIDTH, D_MODEL
    w = in_w.astype(BF16)
    q0 = 2 * gw
    wuv, buv = w[:, :q0], in_b[None, :q0]
    watt = jnp.stack([jnp.concatenate([w[:, q0 + s * aw + p * GROUP_WIDTH:q0 + s * aw + (p + 1) * GROUP_WIDTH]
                                       for s in range(3)], axis=1) for p in range(len(DILATED_PATTERNS))])
    batt = jnp.stack([jnp.concatenate([in_b[q0 + s * aw + p * GROUP_WIDTH:q0 + s * aw + (p + 1) * GROUP_WIDTH]
                                       for s in range(3)])[None] for p in range(len(DILATED_PATTERNS))])
    g0 = q0 + 3 * aw
    wg, bg = w[:, g0:], in_b[None, g0:]
    sb = jnp.repeat(spatial_b.T, gw // GMLP_GROUPS, axis=1)
    ya, gb, a1, a4, a16 = _input_projection(
        x, wuv, buv, watt, batt, wg, bg, ng[None], nb[None], spatial_w, sb, proj_a_w.astype(BF16))
    attn_outs = [_dilated_attention(a, p, d) for p, (a, (_, d)) in enumerate(zip((a1, a4, a16), DILATED_PATTERNS))]
    rw_t = router_w.T
    rwh = rw_t.astype(BF16)
    rwl = (rw_t - rwh.astype(F32)).astype(BF16)
    return _merge_and_norm(attn_outs, ya, gb, x, proj_b_w.astype(BF16), out_w.astype(BF16),
                           ln1_g[None], ln1_b[None], rwh, rwl, alpha, anchors)


def _moe(x1, x1p, scores_t, router_bias, w_gate, w_up, w_down, sw_gate, sw_up, sw_down, ln2_g, ln2_b, alpha):
    swgu = jnp.concatenate([sw_gate, sw_up], axis=1).astype(BF16)
    swd = sw_down.astype(BF16)
    n_tok = x1.shape[0] // MOE_TOKEN_SPLITS
    A = n_tok * TOP_K
    n_blocks = (A + N_EXPERTS * (MOE_BLOCK - 1)) // MOE_BLOCK
    out = None
    for part in range(MOE_TOKEN_SPLITS):
        tok0 = part * n_tok
        eidx_t, rank_t, gate_t, counts = _route(scores_t, router_bias[:, None], tok0, n_tok)
        counts = counts[:, 0].astype(I32)
        padded = (counts + MOE_BLOCK - 1) // MOE_BLOCK * MOE_BLOCK
        pend = jnp.cumsum(padded).astype(I32)
        pstart = pend - padded
        block_starts = jnp.arange(n_blocks, dtype=I32) * MOE_BLOCK
        block_e = jnp.minimum(jnp.sum((pend[None, :] <= block_starts[:, None]).astype(I32), axis=1),
                              N_EXPERTS - 1)
        used = counts > 0
        used_e = jnp.argsort(jnp.logical_not(used), stable=True).astype(I32)
        counts2 = jnp.stack([pend[-1] // MOE_BLOCK, jnp.sum(used.astype(I32))]).astype(I32)
        dest_t = _dest_rows(eidx_t, rank_t, pstart)
        xs = _sc_scatter_rows(x1p, tok0, dest_t.reshape(TOP_K, n_tok // SC_CHUNK, SC_CHUNK).transpose(1, 0, 2),
                              n_blocks * MOE_BLOCK)
        y_rows = _experts(block_e, used_e, counts2, xs, w_gate, w_up, w_down)
        routed = _sc_gather_weighted_sum(
            y_rows,
            dest_t.reshape(TOP_K, n_tok // SC_SUM_CHUNK, SC_SUM_CHUNK).transpose(1, 0, 2).reshape(-1, TOP_K * SC_SUM_CHUNK),
            gate_t.T.reshape(A))
        out = _combine(routed, x1, tok0, swgu, swd, ln2_g[None], ln2_b[None], alpha, out)
    return out


def kernel(x, in_w, in_b, gmlp_norm_g, gmlp_norm_b, spatial_w, spatial_b, proj_a_w, proj_b_w, out_w,
           ln1_g, ln1_b, router_w, router_bias, expert_w_gate, expert_w_up, expert_w_down,
           shared_w_gate, shared_w_up, shared_w_down, ln2_g, ln2_b):
    B, S, D = x.shape
    depth = in_w.shape[0]
    alpha = np.float32((2.0 * depth) ** 0.25)
    for l in range(depth):
        packed = [_sc_pack_row_pairs(w[l].reshape(-1, w.shape[-1]))
                  for w in (expert_w_gate, expert_w_up, expert_w_down)]
        x1, x1p, scores_t = _mixer(x, in_w[l], in_b[l], gmlp_norm_g[l], gmlp_norm_b[l], spatial_w[l],
                                 spatial_b[l], proj_a_w[l], proj_b_w[l], out_w[l], ln1_g[l], ln1_b[l],
                                 router_w[l], alpha, [p[:8] for p in packed])
        out = _moe(x1.reshape(B * S, D), x1p.reshape(B * S, D // 2), scores_t,
                   router_bias[l], *packed,
                   shared_w_gate[l], shared_w_up[l], shared_w_down[l], ln2_g[l], ln2_b[l], alpha)
        x = out.reshape(B, S, D)
    return x
```

```python
import functools
import math

import numpy as np
import jax
import jax.numpy as jnp
from jax import lax
from jax.experimental import pallas as pl
from jax.experimental.pallas import tpu as pltpu
from jax.experimental.pallas import tpu_sc as plsc

F32 = jnp.float32
BF16 = jnp.bfloat16
U32 = jnp.uint32
I32 = jnp.int32

D_MODEL = 1024
GMLP_WIDTH = 1024
GMLP_GROUPS = 8
GMLP_CHUNK = 128
HEAD_DIM = 64
DILATED_PATTERNS = ((128, 1), (512, 4), (2048, 16))
HEADS_PER_GROUP = 4
GROUP_WIDTH = HEADS_PER_GROUP * HEAD_DIM
ATTN_WIDTH = GROUP_WIDTH * len(DILATED_PATTERNS)
ATTN_BLOCK = 128
N_EXPERTS = 256
TOP_K = 8
TOP_K_SHIFT = 3
N_EXPERT_GROUPS = 8
TOPK_GROUPS = 4
EXPERT_DIM = 256
ROUTED_SCALE = 2.5
LN_EPS = 1e-5
LANES = 128
MASKED_SCORE = -1e30

PROJ_TILE = 512
ATTN_QBLOCKS = 4
MERGE_TILE = 512
ROUTE_TILE = 512
DEST_TILE = 2048
MOE_BLOCK = 256
SC_PACK_CHUNK_WORDS = 16384
SC_CHUNK = 64
SC_SUM_CHUNK = 8
MOE_TOKEN_SPLIT = (4, 4)
X_SLOTS = 4
W_SLOTS = 3
COMBINE_TILE = 256
VMEM_LIMIT = 56 * 1024 * 1024


def _layer_norm(y, g, b):
    mu = jnp.mean(y, axis=-1, keepdims=True)
    yc = y - mu
    var = jnp.mean(yc * yc, axis=-1, keepdims=True)
    return yc * lax.rsqrt(var + LN_EPS) * g + b


def _gelu(x):
    return 0.5 * x * (1.0 + lax.erf(x * np.float32(math.sqrt(0.5))))


def _pack_bf16_pairs(x):
    w = x.shape[1] // 2
    bits = pltpu.bitcast(x.astype(BF16).astype(F32), U32)
    return pltpu.bitcast((bits[:, :w] >> 16) | (bits[:, w:] & jnp.uint32(0xFFFF0000)), I32)


def _unpack_bf16_pairs(words):
    w = pltpu.bitcast(words, U32)
    lo = pltpu.bitcast(w << 16, F32)
    hi = pltpu.bitcast(w & jnp.uint32(0xFFFF0000), F32)
    return lo, hi


def _const_spec(shape):
    nd = len(shape)
    return pl.BlockSpec(shape, lambda *_: (0,) * nd)


def _proj_kernel(x_ref, wuv_ref, buv_ref, watt_ref, batt_ref, wg_ref, bg_ref, ng_ref, nb_ref,
                 sw_ref, sb_ref, pa_ref, ya_ref, gb_ref, a1_ref, a4_ref, a16_ref, xc_ref):
    tm = x_ref.shape[1]
    gw = GMLP_WIDTH
    xb = x_ref[0].astype(BF16)

    def proj(w, b):
        return jnp.dot(xb, w, preferred_element_type=F32) + b

    u = _gelu(proj(wuv_ref[:, :gw], buv_ref[:, :gw]))
    v = _gelu(proj(wuv_ref[:, gw:], buv_ref[:, gw:]))
    v = _layer_norm(v, ng_ref[...], nb_ref[...]).astype(BF16)

    cw = gw // GMLP_GROUPS
    row = lax.broadcasted_iota(I32, (GMLP_CHUNK, GMLP_CHUNK), 0)
    col = lax.broadcasted_iota(I32, (GMLP_CHUNK, GMLP_CHUNK), 1)
    ws = [jnp.where(row >= col, sw_ref[g], 0.0).astype(BF16) for g in range(GMLP_GROUPS)]
    chunks = []
    for c in range(tm // GMLP_CHUNK):
        vc = v[c * GMLP_CHUNK:(c + 1) * GMLP_CHUNK]
        cols = [jnp.dot(ws[g], vc[:, g * cw:(g + 1) * cw], preferred_element_type=F32)
                for g in range(GMLP_GROUPS)]
        chunks.append(jnp.concatenate(cols, axis=1) + sb_ref[...])
    vmix = jnp.concatenate(chunks, axis=0)
    ya = jnp.dot((u * vmix).astype(BF16), pa_ref[...], preferred_element_type=F32)
    ga = jax.nn.sigmoid(proj(wg_ref[:, :D_MODEL], bg_ref[:, :D_MODEL]))
    ya_ref[0] = (ga * ya).astype(BF16)
    gb_ref[0] = jax.nn.sigmoid(proj(wg_ref[:, D_MODEL:], bg_ref[:, D_MODEL:])).astype(BF16)

    n_chunks = x_ref.shape[2] // LANES
    for c in range(n_chunks):
        xc_ref[c] = x_ref[0, :, c * LANES:(c + 1) * LANES]
    for p, (a_ref, (_, d)) in enumerate(zip((a1_ref, a4_ref, a16_ref), DILATED_PATTERNS)):
        n = tm // d
        if d == 1:
            xp = xb
        else:
            xp = jnp.concatenate(
                [jnp.concatenate([xc_ref[c, pl.ds(r, n, stride=d), :] for c in range(n_chunks)], axis=1)
                 for r in range(d)], axis=0).astype(BF16)
        h = (jnp.dot(xp, watt_ref[p], preferred_element_type=F32) + batt_ref[p]).astype(BF16)
        for r in range(d):
            a_ref[0, r] = h[r * n:(r + 1) * n]


def _input_projection(x, wuv, buv, watt, batt, wg, bg, ng, nb, sw, sb, pa):
    B, S, D = x.shape
    tm = PROJ_TILE
    grid = (B, S // tm)
    out_shape = [jax.ShapeDtypeStruct((B, S, D), BF16), jax.ShapeDtypeStruct((B, S, D), BF16)]
    out_specs = [pl.BlockSpec((1, tm, D), lambda b, t: (b, t, 0)),
                 pl.BlockSpec((1, tm, D), lambda b, t: (b, t, 0))]
    for _, d in DILATED_PATTERNS:
        out_shape.append(jax.ShapeDtypeStruct((B, d, S // d, ATTN_WIDTH), BF16))
        out_specs.append(pl.BlockSpec((1, d, tm // d, ATTN_WIDTH), lambda b, t: (b, 0, t, 0)))
    consts = (wuv, buv, watt, batt, wg, bg, ng, nb, sw, sb, pa)
    return pl.pallas_call(
        _proj_kernel,
        grid=grid,
        in_specs=[pl.BlockSpec((1, tm, D), lambda b, t: (b, t, 0))] + [_const_spec(c.shape) for c in consts],
        out_specs=out_specs,
        out_shape=out_shape,
        scratch_shapes=[pltpu.VMEM((D // LANES, tm, LANES), F32)],
        compiler_params=pltpu.CompilerParams(
            dimension_semantics=("arbitrary", "arbitrary"), vmem_limit_bytes=VMEM_LIMIT),
        name="input_projection",
    )(x, *consts)


def _attn_kernel(qkv_ref, bias_ref, o_ref, lse_ref, *, qblocks):
    nq = pl.program_id(2)
    gwid = GROUP_WIDTH
    blk = ATTN_BLOCK
    lane = lax.broadcasted_iota(I32, (1, gwid), 1)
    head_masks = [(lane >= h * HEAD_DIM) & (lane < (h + 1) * HEAD_DIM) for h in range(HEADS_PER_GROUP)]
    q_scales = [jnp.where(m, np.float32(HEAD_DIM ** -0.5), 0.0).astype(BF16) for m in head_masks]

    def rows_of(j):
        n = nq * qblocks + j
        return n, pl.multiple_of(n * blk, blk), pl.multiple_of(jnp.maximum(n - 1, 0) * blk, blk)

    def scores(j):
        n, q0, p0 = rows_of(j)
        q = qkv_ref[0, 0, pl.ds(q0, blk), 0:gwid]
        kk = jnp.concatenate([qkv_ref[0, 0, pl.ds(p0, blk), gwid:2 * gwid],
                              qkv_ref[0, 0, pl.ds(q0, blk), gwid:2 * gwid]], axis=0)
        qs = jnp.concatenate([q * s for s in q_scales], axis=0)
        s = lax.dot_general(qs, kk, (((1,), (1,)), ((), ())), preferred_element_type=F32)
        return s + bias_ref[jnp.where(n == 0, 1, 0)]

    def finish(j, s):
        _, q0, p0 = rows_of(j)
        vv = jnp.concatenate([qkv_ref[0, 0, pl.ds(p0, blk), 2 * gwid:3 * gwid],
                              qkv_ref[0, 0, pl.ds(q0, blk), 2 * gwid:3 * gwid]], axis=0)
        m = jnp.max(s, axis=1, keepdims=True)
        p = jnp.exp(s - m)
        den = jnp.sum(p, axis=1, keepdims=True)
        pv = jnp.dot(p.astype(BF16), vv, preferred_element_type=F32)
        on = pv / den
        lse = m + jnp.log(den)
        o = jnp.zeros((blk, gwid), F32)
        l = jnp.zeros((blk, gwid), F32)
        for h in range(HEADS_PER_GROUP):
            o = jnp.where(head_masks[h], on[h * blk:(h + 1) * blk], o)
            l = jnp.where(head_masks[h], lse[h * blk:(h + 1) * blk], l)
        o_ref[0, 0, j * blk:(j + 1) * blk, :] = o.astype(BF16)
        lse_ref[0, 0, j * blk:(j + 1) * blk, :] = l

    s_next = scores(0)
    for j in range(qblocks):
        s_cur = s_next
        if j + 1 < qblocks:
            s_next = scores(j + 1)
        finish(j, s_cur)


def _alibi_slopes(n):
    def pow2_slopes(m):
        start = 2.0 ** (-8.0 / m)
        return [start ** (i + 1) for i in range(m)]
    p = 2 ** int(math.floor(math.log2(n)))
    s = pow2_slopes(p)
    if p < n:
        s = s + pow2_slopes(2 * p)[0::2][: n - p]
    return np.array(sorted(s, reverse=True), dtype=np.float32)


def _attn_bias_tables(group, dilation):
    blk = ATTN_BLOCK
    slopes = _alibi_slopes(HEADS_PER_GROUP * len(DILATED_PATTERNS))
    slopes = slopes[group * HEADS_PER_GROUP:(group + 1) * HEADS_PER_GROUP]
    qi = np.arange(blk)[:, None]
    ki = np.arange(2 * blk)[None, :]
    delta = blk + qi - ki
    band = (delta >= 0) & (delta <= blk)
    bias = -slopes[:, None, None] * (delta * dilation).astype(np.float32)[None]
    full = np.where(band[None], bias, np.float32(MASKED_SCORE)).astype(np.float32)
    first = np.where((ki >= blk)[None], full, np.float32(MASKED_SCORE)).astype(np.float32)
    return full.reshape(HEADS_PER_GROUP * blk, 2 * blk), first.reshape(HEADS_PER_GROUP * blk, 2 * blk)


def _dilated_attention(qkv, group, dilation):
    B, d, sd, _ = qkv.shape
    qblocks = min(ATTN_QBLOCKS, sd // ATTN_BLOCK)
    rows = qblocks * ATTN_BLOCK
    bias = np.stack(_attn_bias_tables(group, dilation))
    grid = (B, d, sd // rows)
    out_spec = pl.BlockSpec((1, 1, rows, GROUP_WIDTH), lambda b, r, n: (b, r, n, 0))
    return pl.pallas_call(
        functools.partial(_attn_kernel, qblocks=qblocks),
        grid=grid,
        in_specs=[pl.BlockSpec((1, 1, sd, ATTN_WIDTH), lambda b, r, n: (b, r, 0, 0)),
                  _const_spec(bias.shape)],
        out_specs=[out_spec, out_spec],
        out_shape=[jax.ShapeDtypeStruct((B, d, sd, GROUP_WIDTH), BF16),
                   jax.ShapeDtypeStruct((B, d, sd, GROUP_WIDTH), F32)],
        compiler_params=pltpu.CompilerParams(
            dimension_semantics=("arbitrary", "arbitrary", "arbitrary"), vmem_limit_bytes=VMEM_LIMIT),
        name=f"dilated_attention_d{dilation}",
    )(qkv, jnp.asarray(bias))


def _merge_kernel(o1_ref, l1_ref, o4_ref, l4_ref, o16_ref, l16_ref, ya_ref, gb_ref, x_ref,
                  pb_ref, ow_ref, g1_ref, b1_ref, rwh_ref, rwl_ref, *rest, alpha, n_anchors):
    x1_ref, x1p_ref, sc_ref, so4, sl4, so16, sl16 = rest[n_anchors:]
    tm = x_ref.shape[1]
    n_chunks = GROUP_WIDTH // LANES
    for (o_ref, l_ref, so, sl, d) in ((o4_ref, l4_ref, so4, sl4, 4), (o16_ref, l16_ref, so16, sl16, 16)):
        n = tm // d
        for r in range(d):
            o_r = o_ref[0, r].astype(F32)
            l_r = l_ref[0, r]
            for c in range(n_chunks):
                so[c, pl.ds(r, n, stride=d), :] = o_r[:, c * LANES:(c + 1) * LANES]
                sl[c, pl.ds(r, n, stride=d), :] = l_r[:, c * LANES:(c + 1) * LANES]

    def natural(s):
        return jnp.concatenate([s[c] for c in range(n_chunks)], axis=1)

    l1 = l1_ref[0, 0]
    l4 = natural(sl4)
    l16 = natural(sl16)
    lmax = jnp.maximum(jnp.maximum(l1, l4), l16)
    e1 = jnp.exp(l1 - lmax)
    e4 = jnp.exp(l4 - lmax)
    e16 = jnp.exp(l16 - lmax)
    yb = (e1 * o1_ref[0, 0].astype(F32) + e4 * natural(so4) + e16 * natural(so16)) / (e1 + e4 + e16)
    ybp = jnp.dot(yb.astype(BF16), pb_ref[...], preferred_element_type=F32)
    merged = ya_ref[0].astype(F32) + gb_ref[0].astype(F32) * ybp
    mix = jnp.dot(merged.astype(BF16), ow_ref[...], preferred_element_type=F32)
    x1 = _layer_norm(alpha * x_ref[0] + mix, g1_ref[...], b1_ref[...])
    x1_ref[0] = x1
    x1p_ref[0] = _pack_bf16_pairs(x1)
    hi = x1.astype(BF16)
    lo = (x1 - hi.astype(F32)).astype(BF16)
    def logits_t(w_ref, xt):
        return lax.dot_general(w_ref[...], xt, (((1,), (1,)), ((), ())), preferred_element_type=F32)
    sc_ref[...] = jax.nn.sigmoid(logits_t(rwh_ref, hi) + logits_t(rwh_ref, lo) + logits_t(rwl_ref, hi))


def _merge_and_norm(attn_outs, ya, gb, x, pb, ow, g1, b1, rwh, rwl, alpha, anchors):
    B, S, D = x.shape
    tm = MERGE_TILE
    in_specs = []
    args = []
    for (o, l), (_, d) in zip(attn_outs, DILATED_PATTERNS):
        spec = pl.BlockSpec((1, d, tm // d, GROUP_WIDTH), lambda b, t: (b, 0, t, 0))
        in_specs += [spec, spec]
        args += [o, l]
    tok_spec = pl.BlockSpec((1, tm, D), lambda b, t: (b, t, 0))
    in_specs += [tok_spec, tok_spec, tok_spec]
    args += [ya, gb, x]
    consts = (pb, ow, g1, b1, rwh, rwl)
    in_specs += [_const_spec(c.shape) for c in consts]
    in_specs += [pl.BlockSpec(memory_space=pl.ANY) for _ in anchors]
    return pl.pallas_call(
        functools.partial(_merge_kernel, alpha=alpha, n_anchors=len(anchors)),
        grid=(B, S // tm),
        in_specs=in_specs,
        out_specs=[tok_spec,
                   pl.BlockSpec((1, tm, D // 2), lambda b, t: (b, t, 0)),
                   pl.BlockSpec((N_EXPERTS, tm), lambda b, t: (0, b * (S // tm) + t))],
        out_shape=[jax.ShapeDtypeStruct((B, S, D), F32),
                   jax.ShapeDtypeStruct((B, S, D // 2), I32),
                   jax.ShapeDtypeStruct((N_EXPERTS, B * S), F32)],
        scratch_shapes=[pltpu.VMEM((GROUP_WIDTH // LANES, tm, LANES), F32) for _ in range(4)],
        compiler_params=pltpu.CompilerParams(
            dimension_semantics=("arbitrary", "arbitrary"), vmem_limit_bytes=VMEM_LIMIT),
        name="merge_norm_router",
    )(*args, *consts, *anchors)


def _sortable_key(x):
    bits = pltpu.bitcast(x, I32)
    return jnp.where(bits < 0, bits ^ jnp.int32(0x7FFFFFFF), bits)


def _route_kernel(sc_ref, bias_ref, before_ref, eidx_ref, rank_ref, gate_ref, cnt_ref, carry_ref):
    ne, tm = sc_ref.shape
    gsize = ne // N_EXPERT_GROUPS
    neg_inf = np.float32(-np.inf)
    removed = jnp.int32(-2 ** 31)

    @pl.when(pl.program_id(0) == 0)
    def _():
        carry_ref[...] = jnp.zeros_like(carry_ref)

    scores = sc_ref[...]
    biased = scores + bias_ref[...]

    gsum = []
    for g in range(N_EXPERT_GROUPS):
        v = biased[g * gsize:(g + 1) * gsize]
        m1 = jnp.max(v, axis=0, keepdims=True)
        n1 = jnp.sum(jnp.where(v == m1, 1.0, 0.0), axis=0, keepdims=True)
        m2 = jnp.max(jnp.where(v < m1, v, neg_inf), axis=0, keepdims=True)
        gsum.append(m1 + jnp.where(n1 >= 2.0, m1, m2))
    gkey = _sortable_key(jnp.concatenate(gsum, axis=0))

    def pick_first_max(keys, ids, n_ids):
        m = jnp.max(keys, axis=0, keepdims=True)
        idx = jnp.min(jnp.where(keys == m, ids, n_ids), axis=0, keepdims=True)
        hit = ids == idx
        return idx, hit, jnp.where(hit, removed, keys)

    gid = lax.broadcasted_iota(I32, (N_EXPERT_GROUPS, tm), 0)
    for _ in range(TOPK_GROUPS):
        _, _, gkey = pick_first_max(gkey, gid, N_EXPERT_GROUPS)
    group_on = gkey == removed

    masked = jnp.concatenate(
        [jnp.where(group_on[g:g + 1], biased[g * gsize:(g + 1) * gsize], neg_inf)
         for g in range(N_EXPERT_GROUPS)], axis=0)
    keys = _sortable_key(masked)
    eid = lax.broadcasted_iota(I32, (ne, tm), 0)
    picks = []
    for _ in range(TOP_K):
        idx, _, keys = pick_first_max(keys, eid, ne)
        picks.append(idx)

    sel = jnp.where(keys == removed, 1.0, 0.0)
    ranks = jnp.dot(sel.astype(BF16), before_ref[...], preferred_element_type=F32) + carry_ref[...]
    carry_ref[...] = carry_ref[...] + jnp.sum(sel, axis=1, keepdims=True)
    cnt_ref[...] = carry_ref[...]

    s_k, r_k = [], []
    for idx in picks:
        hit = eid == idx
        s_k.append(jnp.sum(jnp.where(hit, scores, 0.0), axis=0, keepdims=True))
        r_k.append(jnp.sum(jnp.where(hit, ranks, 0.0), axis=0, keepdims=True))
    total = s_k[0]
    for s in s_k[1:]:
        total = total + s
    eidx_ref[...] = jnp.concatenate(picks, axis=0)
    rank_ref[...] = jnp.concatenate(r_k, axis=0).astype(I32)
    gate_ref[...] = jnp.concatenate([s / total * np.float32(ROUTED_SCALE) for s in s_k], axis=0)


def _route(scores_t, bias, tok0, T):
    ne = scores_t.shape[0]
    tm = ROUTE_TILE
    first = tok0 // tm
    before = jnp.asarray(np.triu(np.ones((tm, tm), np.float32), k=1), BF16)
    out_spec = pl.BlockSpec((TOP_K, tm), lambda i: (0, i))
    return pl.pallas_call(
        _route_kernel,
        grid=(T // tm,),
        in_specs=[pl.BlockSpec((ne, tm), lambda i: (0, first + i)), _const_spec(bias.shape),
                  _const_spec(before.shape)],
        out_specs=[out_spec, out_spec, out_spec, _const_spec((ne, 1))],
        out_shape=[jax.ShapeDtypeStruct((TOP_K, T), I32), jax.ShapeDtypeStruct((TOP_K, T), I32),
                   jax.ShapeDtypeStruct((TOP_K, T), F32), jax.ShapeDtypeStruct((ne, 1), F32)],
        scratch_shapes=[pltpu.VMEM((ne, 1), F32)],
        compiler_params=pltpu.CompilerParams(dimension_semantics=("arbitrary",), vmem_limit_bytes=VMEM_LIMIT),
        name="route_topk",
    )(scores_t, bias, before)


def _dest_kernel(pstart_ref, eidx_ref, rank_ref, out_ref):
    eidx = eidx_ref[...]
    start = jnp.zeros(eidx.shape, I32)
    for e in range(N_EXPERTS):
        start = jnp.where(eidx == e, pstart_ref[e], start)
    out_ref[...] = start + rank_ref[...]


def _dest_rows(eidx_t, rank_t, pstart):
    T = eidx_t.shape[1]
    tm = DEST_TILE
    tok_spec = pl.BlockSpec((TOP_K, tm), lambda i: (0, i))
    return pl.pallas_call(
        _dest_kernel,
        grid=(T // tm,),
        in_specs=[pl.BlockSpec(memory_space=pltpu.SMEM), tok_spec, tok_spec],
        out_specs=tok_spec,
        out_shape=jax.ShapeDtypeStruct((TOP_K, T), I32),
        compiler_params=pltpu.CompilerParams(dimension_semantics=("arbitrary",)),
        name="moe_dest_rows",
    )(pstart, eidx_t, rank_t)


def _sc_workers():
    info = plsc.get_sparse_core_info()
    return info.num_cores, info.num_cores * info.num_subcores


def _sc_scatter_rows(rows, row0, dest, n_out):
    n_dst, n_chunks, ch = dest.shape
    width = rows.shape[1]
    n_cores, n_workers = _sc_workers()
    per_w = n_chunks // n_workers
    assert n_chunks % n_workers == 0 and per_w % 8 == 0 and row0 + n_chunks * ch <= rows.shape[0]

    def body(rows_hbm, dest_hbm, out_hbm, idx_v, buf, lsem, ssem):
        wid = lax.axis_index("s") * n_cores + lax.axis_index("c")
        c0 = wid * per_w
        for k in range(n_dst):
            pltpu.sync_copy(dest_hbm.at[k, pl.ds(pl.multiple_of(c0, 8), per_w)], idx_v.at[k])

        def load(c, b):
            return pltpu.make_async_copy(rows_hbm.at[pl.ds(row0 + (c0 + c) * ch, ch)], buf.at[b], lsem.at[b])

        def scatter(c, b, k):
            return pltpu.make_async_copy(buf.at[b], out_hbm.at[idx_v.at[k, c]], ssem.at[b])

        load(0, 0).start()

        @pl.loop(0, per_w, step=2)
        def _(c):
            for b in range(2):
                cc = c + b
                load(cc, b).wait()

                @pl.when(cc >= 1)
                def _():
                    for k in range(n_dst):
                        scatter(cc - 1, 1 - b, k).wait()

                @pl.when(cc + 1 < per_w)
                def _():
                    load(cc + 1, 1 - b).start()

                for k in range(n_dst):
                    scatter(cc, b, k).start()

        for k in range(n_dst):
            scatter(per_w - 1, 1, k).wait()

    mesh = plsc.VectorSubcoreMesh(core_axis_name="c", subcore_axis_name="s")
    return pl.kernel(
        body, out_type=jax.ShapeDtypeStruct((n_out, width), rows.dtype), mesh=mesh,
        scratch_types=[pltpu.VMEM((n_dst, per_w, ch), I32), pltpu.VMEM((2, ch, width), rows.dtype),
                       pltpu.SemaphoreType.DMA((2,)), pltpu.SemaphoreType.DMA((2,))],
        name="moe_dispatch_scatter",
    )(rows, dest)


def _sc_gather_weighted_sum(table, idx, gates):
    n_src, n_tok = idx.shape
    ct = SC_SUM_CHUNK
    n_chunks = n_tok // ct
    half = table.shape[1]
    lanes = plsc.get_sparse_core_info().num_lanes
    n_cores, n_workers = _sc_workers()
    per_w = n_chunks // n_workers
    assert n_chunks % n_workers == 0 and per_w % 2 == 0 and half % lanes == 0

    def body(table_hbm, idx_hbm, gate_hbm, out_hbm, idx_v, gate_v, buf, out_v, gsem, wsem):
        wid = lax.axis_index("s") * n_cores + lax.axis_index("c")
        c0 = wid * per_w
        pltpu.sync_copy(idx_hbm.at[:, pl.ds(c0 * ct, per_w * ct)], idx_v)
        pltpu.sync_copy(gate_hbm.at[:, pl.ds(c0 * ct, per_w * ct)], gate_v)
        shift = jnp.full((lanes,), 16, I32)
        hi_mask = jnp.full((lanes,), -65536, I32)

        def gather(c, b, k):
            return pltpu.make_async_copy(table_hbm.at[idx_v.at[k, pl.ds(c * ct, ct)]], buf.at[b, k], gsem.at[b])

        def write(c, b):
            return pltpu.make_async_copy(out_v.at[b], out_hbm.at[pl.ds((c0 + c) * ct, ct)], wsem.at[b])

        for k in range(n_src):
            gather(0, 0, k).start()

        @pl.loop(0, per_w, step=2)
        def _(c):
            for b in range(2):
                cc = c + b
                for k in range(n_src):
                    gather(cc, b, k).wait()

                @pl.when(cc + 1 < per_w)
                def _():
                    for k in range(n_src):
                        gather(cc + 1, 1 - b, k).start()

                @pl.when(cc >= 2)
                def _():
                    write(cc - 2, b).wait()

                @pl.loop(0, ct)
                def _(t):
                    tok = jnp.full((lanes,), cc * ct + t, I32)
                    g = [plsc.load_gather(gate_v, [jnp.full((lanes,), k, I32), tok]) for k in range(n_src)]

                    @plsc.parallel_loop(0, half, step=lanes, unroll=2)
                    def _(col):
                        lo = jnp.zeros((lanes,), F32)
                        hi = jnp.zeros((lanes,), F32)
                        for k in range(n_src):
                            w = buf[b, k, t, pl.ds(col, lanes)]
                            lo = lo + g[k] * plsc.bitcast(lax.shift_left(w, shift), F32)
                            hi = hi + g[k] * plsc.bitcast(w & hi_mask, F32)
                        out_v[b, t, pl.ds(col, lanes)] = lo
                        out_v[b, t, pl.ds(half + col, lanes)] = hi

                write(cc, b).start()

        write(per_w - 2, 0).wait()
        write(per_w - 1, 1).wait()

    mesh = plsc.VectorSubcoreMesh(core_axis_name="c", subcore_axis_name="s")
    return pl.kernel(
        body, out_type=jax.ShapeDtypeStruct((n_chunks * ct, 2 * half), F32), mesh=mesh,
        scratch_types=[pltpu.VMEM((n_src, per_w * ct), I32), pltpu.VMEM((n_src, per_w * ct), F32),
                       pltpu.VMEM((2, n_src, ct, half), I32), pltpu.VMEM((2, ct, 2 * half), F32),
                       pltpu.SemaphoreType.DMA((2,)), pltpu.SemaphoreType.DMA((2,))],
        compiler_params=pltpu.CompilerParams(needs_layout_passes=False),
        name="moe_combine_gather_sum",
    )(table, idx, gates)


def _sc_pack_row_pairs(w):
    R, C = w.shape
    lanes = plsc.get_sparse_core_info().num_lanes
    cr = SC_PACK_CHUNK_WORDS // C
    n_chunks = R // cr
    n_cores, n_workers = _sc_workers()
    per_w = n_chunks // n_workers
    assert R % cr == 0 and n_chunks % n_workers == 0 and per_w % 2 == 0 and cr % 2 == 0 and C % lanes == 0

    def body(w_hbm, out_hbm, in_v, out_v, lsem, ssem):
        wid = lax.axis_index("s") * n_cores + lax.axis_index("c")
        c0 = wid * per_w

        def load(c, b):
            return pltpu.make_async_copy(w_hbm.at[pl.ds((c0 + c) * cr, cr)], in_v.at[b], lsem.at[b])

        def store(c, b):
            return pltpu.make_async_copy(out_v.at[b], out_hbm.at[pl.ds((c0 + c) * (cr // 2), cr // 2)],
                                         ssem.at[b])

        load(0, 0).start()

        @pl.loop(0, per_w, step=2)
        def _(c):
            for b in range(2):
                cc = c + b
                load(cc, b).wait()

                @pl.when(cc + 1 < per_w)
                def _():
                    load(cc + 1, 1 - b).start()

                @pl.when(cc >= 2)
                def _():
                    store(cc - 2, b).wait()

                @pl.loop(0, cr // 2)
                def _(i):
                    @plsc.parallel_loop(0, C, step=lanes, unroll=4)
                    def _(col):
                        even = in_v[b, 2 * i, pl.ds(col, lanes)]
                        odd = in_v[b, 2 * i + 1, pl.ds(col, lanes)]
                        pair = plsc.pack(even, odd, format=plsc.PackFormat.INTERLEAVED)
                        out_v[b, i, pl.ds(col, lanes)] = plsc.bitcast(pair, I32)

                store(cc, b).start()

        store(per_w - 2, 0).wait()
        store(per_w - 1, 1).wait()

    mesh = plsc.VectorSubcoreMesh(core_axis_name="c", subcore_axis_name="s")
    return pl.kernel(
        body, out_type=jax.ShapeDtypeStruct((R // 2, C), I32), mesh=mesh,
        scratch_types=[pltpu.VMEM((2, cr, C), F32), pltpu.VMEM((2, cr // 2, C), I32),
                       pltpu.SemaphoreType.DMA((2,)), pltpu.SemaphoreType.DMA((2,))],
        compiler_params=pltpu.CompilerParams(needs_layout_passes=False),
        name="expert_weights_bf16",
    )(w)


def _expert_kernel(be_ref, ue_ref, nu_ref, xs_hbm, wg_hbm, wu_hbm, wd_hbm, y_hbm,
                   xbuf, ybuf, wg_v, wu_v, wd_v, xsem, ysem, wsem):
    n_used = nu_ref[0]
    n_exp = nu_ref[1]
    half = D_MODEL // 2
    blk = MOE_BLOCK
    rows_gu = D_MODEL // 2
    rows_d = EXPERT_DIM // 2

    def rows(j):
        return pl.ds(pl.multiple_of(j * blk, blk), blk)

    def x_copy(j, p):
        return pltpu.make_async_copy(xs_hbm.at[rows(j)], xbuf.at[p], xsem.at[p])

    def y_copy(j, p):
        return pltpu.make_async_copy(ybuf.at[p], y_hbm.at[rows(j)], ysem.at[p])

    def w_copies(q, s):
        e = ue_ref[q]
        gu = pl.ds(pl.multiple_of(e * rows_gu, rows_gu), rows_gu)
        dn = pl.ds(pl.multiple_of(e * rows_d, rows_d), rows_d)
        return (pltpu.make_async_copy(wg_hbm.at[gu], wg_v.at[s], wsem.at[s, 0]),
                pltpu.make_async_copy(wu_hbm.at[gu], wu_v.at[s], wsem.at[s, 1]),
                pltpu.make_async_copy(wd_hbm.at[dn], wd_v.at[s], wsem.at[s, 2]))

    for q0 in range(W_SLOTS - 1):
        @pl.when(q0 < n_exp)
        def _():
            for c in w_copies(q0, q0):
                c.start()

    for j0 in range(X_SLOTS):
        @pl.when(j0 < n_used)
        def _():
            x_copy(j0, j0).start()

    def block_step(j, p, q):
        is_new = (j == 0) | (be_ref[j] != be_ref[jnp.maximum(j - 1, 0)])
        q = q + is_new.astype(I32)

        s = q % W_SLOTS

        @pl.when(is_new)
        def _():
            for c in w_copies(q, s):
                c.wait()

            @pl.when(q + W_SLOTS - 1 < n_exp)
            def _():
                for c in w_copies(q + W_SLOTS - 1, (q + W_SLOTS - 1) % W_SLOTS):
                    c.start()

        x_copy(j, p).wait()

        @pl.when(j >= X_SLOTS)
        def _():
            y_copy(j - X_SLOTS, p).wait()

        wg = pltpu.bitcast(wg_v[s], BF16)
        wu = pltpu.bitcast(wu_v[s], BF16)
        wd = pltpu.bitcast(wd_v[s], BF16)
        lo, hi = _unpack_bf16_pairs(xbuf[p])
        xlo = lo.astype(BF16)
        xhi = hi.astype(BF16)
        g = (jnp.dot(xlo, wg[:half], preferred_element_type=F32)
             + jnp.dot(xhi, wg[half:], preferred_element_type=F32))
        u = (jnp.dot(xlo, wu[:half], preferred_element_type=F32)
             + jnp.dot(xhi, wu[half:], preferred_element_type=F32))
        hb = (g * jax.nn.sigmoid(g) * u).astype(BF16)
        ybuf[p] = _pack_bf16_pairs(jnp.dot(hb, wd, preferred_element_type=F32))
        y_copy(j, p).start()

        @pl.when(j + X_SLOTS < n_used)
        def _():
            x_copy(j + X_SLOTS, p).start()

        return q

    def group(m, q):
        q = block_step(X_SLOTS * m, 0, q)
        for p in range(1, X_SLOTS):
            j = X_SLOTS * m + p
            q = lax.cond(j < n_used, functools.partial(block_step, j, p), lambda q: q, q)
        return q

    lax.fori_loop(0, (n_used + X_SLOTS - 1) // X_SLOTS, group, jnp.int32(-1))

    for back in range(X_SLOTS, 0, -1):
        b = n_used - back
        for p in range(X_SLOTS):
            @pl.when((b >= 0) & (b % X_SLOTS == p))
            def _():
                y_copy(b, p).wait()


def _experts(block_e, used_e, counts2, xs, wg, wu, wd):
    n_rows, half = xs.shape
    smem = pl.BlockSpec(memory_space=pltpu.SMEM)
    hbm = pl.BlockSpec(memory_space=pl.ANY)
    return pl.pallas_call(
        _expert_kernel,
        in_specs=[smem, smem, smem, hbm, hbm, hbm, hbm],
        out_specs=hbm,
        out_shape=jax.ShapeDtypeStruct((n_rows, half), I32),
        scratch_shapes=[pltpu.VMEM((X_SLOTS, MOE_BLOCK, half), I32), pltpu.VMEM((X_SLOTS, MOE_BLOCK, half), I32),
                        pltpu.VMEM((W_SLOTS, D_MODEL // 2, EXPERT_DIM), I32),
                        pltpu.VMEM((W_SLOTS, D_MODEL // 2, EXPERT_DIM), I32),
                        pltpu.VMEM((W_SLOTS, EXPERT_DIM // 2, D_MODEL), I32),
                        pltpu.SemaphoreType.DMA((X_SLOTS,)), pltpu.SemaphoreType.DMA((X_SLOTS,)),
                        pltpu.SemaphoreType.DMA((W_SLOTS, 3))],
        compiler_params=pltpu.CompilerParams(vmem_limit_bytes=VMEM_LIMIT),
        name="moe_experts",
    )(block_e, used_e, counts2, xs, wg, wu, wd)


def _combine_kernel(routed_ref, x1_ref, swgu_ref, swd_ref, g2_ref, b2_ref, *rest, alpha):
    out_ref = rest[-1]
    x1 = x1_ref[...]
    gu = jnp.dot(x1.astype(BF16), swgu_ref[...], preferred_element_type=F32)
    g = gu[:, :EXPERT_DIM]
    u = gu[:, EXPERT_DIM:]
    shared = jnp.dot((g * jax.nn.sigmoid(g) * u).astype(BF16), swd_ref[...], preferred_element_type=F32)
    out_ref[...] = _layer_norm(alpha * x1 + (routed_ref[...] + shared), g2_ref[...], b2_ref[...])


def _combine(routed, x1, tok0, swgu, swd, g2, b2, alpha, out_prev):
    T, D = x1.shape
    n_tok = routed.shape[0]
    tc = COMBINE_TILE
    first = tok0 // tc
    consts = (swgu, swd, g2, b2)
    args = [routed, x1, *consts]
    in_specs = [pl.BlockSpec((tc, D), lambda i: (i, 0)),
                pl.BlockSpec((tc, D), lambda i: (first + i, 0))] + [_const_spec(c.shape) for c in consts]
    aliases = {}
    if out_prev is not None:
        aliases = {len(args): 0}
        args.append(out_prev)
        in_specs.append(pl.BlockSpec(memory_space=pl.ANY))
    return pl.pallas_call(
        functools.partial(_combine_kernel, alpha=alpha),
        grid=(n_tok // tc,),
        in_specs=in_specs,
        out_specs=pl.BlockSpec((tc, D), lambda i: (first + i, 0)),
        out_shape=jax.ShapeDtypeStruct((T, D), F32),
        input_output_aliases=aliases,
        compiler_params=pltpu.CompilerParams(dimension_semantics=("arbitrary",), vmem_limit_bytes=VMEM_LIMIT),
        name="moe_combine_norm",
    )(*args)


def _mixer(x, in_w, in_b, ng, nb, spatial_w, spatial_b, proj_a_w, proj_b_w, out_w, ln1_g, ln1_b,
           router_w, alpha, anchors):
    gw, aw, D = GMLP_WIDTH, ATTN_WIDTH, D_MODEL
    w = in_w.astype(BF16)
    q0 = 2 * gw
    wuv, buv = w[:, :q0], in_b[None, :q0]
    watt = jnp.stack([jnp.concatenate([w[:, q0 + s * aw + p * GROUP_WIDTH:q0 + s * aw + (p + 1) * GROUP_WIDTH]
                                       for s in range(3)], axis=1) for p in range(len(DILATED_PATTERNS))])
    batt = jnp.stack([jnp.concatenate([in_b[q0 + s * aw + p * GROUP_WIDTH:q0 + s * aw + (p + 1) * GROUP_WIDTH]
                                       for s in range(3)])[None] for p in range(len(DILATED_PATTERNS))])
    g0 = q0 + 3 * aw
    wg, bg = w[:, g0:], in_b[None, g0:]
    sb = jnp.repeat(spatial_b.T, gw // GMLP_GROUPS, axis=1)
    ya, gb, a1, a4, a16 = _input_projection(
        x, wuv, buv, watt, batt, wg, bg, ng[None], nb[None], spatial_w, sb, proj_a_w.astype(BF16))
    attn_outs = [_dilated_attention(a, p, d) for p, (a, (_, d)) in enumerate(zip((a1, a4, a16), DILATED_PATTERNS))]
    rw_t = router_w.T
    rwh = rw_t.astype(BF16)
    rwl = (rw_t - rwh.astype(F32)).astype(BF16)
    return _merge_and_norm(attn_outs, ya, gb, x, proj_b_w.astype(BF16), out_w.astype(BF16),
                           ln1_g[None], ln1_b[None], rwh, rwl, alpha, anchors)


def _moe(x1, x1p, scores_t, router_bias, w_gate, w_up, w_down, sw_gate, sw_up, sw_down, ln2_g, ln2_b, alpha):
    swgu = jnp.concatenate([sw_gate, sw_up], axis=1).astype(BF16)
    swd = sw_down.astype(BF16)
    out = None
    tok0 = 0
    for eighths in MOE_TOKEN_SPLIT:
        n_tok = x1.shape[0] * eighths // 8
        n_blocks = (n_tok * TOP_K + N_EXPERTS * (MOE_BLOCK - 1)) // MOE_BLOCK
        eidx_t, rank_t, gate_t, counts = _route(scores_t, router_bias[:, None], tok0, n_tok)
        counts = counts[:, 0].astype(I32)
        padded = (counts + MOE_BLOCK - 1) // MOE_BLOCK * MOE_BLOCK
        pend = jnp.cumsum(padded).astype(I32)
        pstart = pend - padded
        block_starts = jnp.arange(n_blocks, dtype=I32) * MOE_BLOCK
        block_e = jnp.minimum(jnp.sum((pend[None, :] <= block_starts[:, None]).astype(I32), axis=1),
                              N_EXPERTS - 1)
        used = counts > 0
        used_e = jnp.argsort(jnp.logical_not(used), stable=True).astype(I32)
        counts2 = jnp.stack([pend[-1] // MOE_BLOCK, jnp.sum(used.astype(I32))]).astype(I32)
        dest_t = _dest_rows(eidx_t, rank_t, pstart)
        xs = _sc_scatter_rows(x1p, tok0, dest_t.reshape(TOP_K, n_tok // SC_CHUNK, SC_CHUNK), n_blocks * MOE_BLOCK)
        y_rows = _experts(block_e, used_e, counts2, xs, w_gate, w_up, w_down)
        routed = _sc_gather_weighted_sum(y_rows, dest_t, gate_t)
        out = _combine(routed, x1, tok0, swgu, swd, ln2_g[None], ln2_b[None], alpha, out)
        tok0 += n_tok
    return out


def kernel(x, in_w, in_b, gmlp_norm_g, gmlp_norm_b, spatial_w, spatial_b, proj_a_w, proj_b_w, out_w,
           ln1_g, ln1_b, router_w, router_bias, expert_w_gate, expert_w_up, expert_w_down,
           shared_w_gate, shared_w_up, shared_w_down, ln2_g, ln2_b):
    B, S, D = x.shape
    depth = in_w.shape[0]
    alpha = np.float32((2.0 * depth) ** 0.25)
    for l in range(depth):
        packed = [_sc_pack_row_pairs(w[l].reshape(-1, w.shape[-1]))
                  for w in (expert_w_gate, expert_w_up, expert_w_down)]
        x1, x1p, scores_t = _mixer(x, in_w[l], in_b[l], gmlp_norm_g[l], gmlp_norm_b[l], spatial_w[l],
                                 spatial_b[l], proj_a_w[l], proj_b_w[l], out_w[l], ln1_g[l], ln1_b[l],
                                 router_w[l], alpha, [p[:8] for p in packed])
        out = _moe(x1.reshape(B * S, D), x1p.reshape(B * S, D // 2), scores_t,
                   router_bias[l], *packed,
                   shared_w_gate[l], shared_w_up[l], shared_w_down[l], ln2_g[l], ln2_b[l], alpha)
        x = out.reshape(B, S, D)
    return x
```

```python
import functools
import math

import numpy as np
import jax
import jax.numpy as jnp
from jax import lax
from jax.experimental import pallas as pl
from jax.experimental.pallas import tpu as pltpu
from jax.experimental.pallas import tpu_sc as plsc

F32 = jnp.float32
BF16 = jnp.bfloat16
U32 = jnp.uint32
I32 = jnp.int32

D_MODEL = 1024
GMLP_WIDTH = 1024
GMLP_GROUPS = 8
GMLP_CHUNK = 128
HEAD_DIM = 64
DILATED_PATTERNS = ((128, 1), (512, 4), (2048, 16))
HEADS_PER_GROUP = 4
GROUP_WIDTH = HEADS_PER_GROUP * HEAD_DIM
ATTN_WIDTH = GROUP_WIDTH * len(DILATED_PATTERNS)
ATTN_BLOCK = 128
N_EXPERTS = 256
TOP_K = 8
TOP_K_SHIFT = 3
N_EXPERT_GROUPS = 8
TOPK_GROUPS = 4
EXPERT_DIM = 256
ROUTED_SCALE = 2.5
LN_EPS = 1e-5
LANES = 128
MASKED_SCORE = -1e30

PROJ_TILE = 512
ATTN_QBLOCKS = 4
MERGE_TILE = 512
ROUTE_TILE = 512
DEST_TILE = 2048
MOE_BLOCK = 256
SC_PACK_CHUNK_WORDS = 16384
SC_CHUNK = 64
SC_SUM_CHUNK = 8
MOE_TOKEN_SPLIT = (4, 4)
X_SLOTS = 4
W_SLOTS = 3
COMBINE_TILE = 256
VMEM_LIMIT = 56 * 1024 * 1024


def _layer_norm(y, g, b):
    mu = jnp.mean(y, axis=-1, keepdims=True)
    yc = y - mu
    var = jnp.mean(yc * yc, axis=-1, keepdims=True)
    return yc * lax.rsqrt(var + LN_EPS) * g + b


def _gelu(x):
    return 0.5 * x * (1.0 + lax.erf(x * np.float32(math.sqrt(0.5))))


def _pack_bf16_pairs(x):
    w = x.shape[1] // 2
    bits = pltpu.bitcast(x.astype(BF16).astype(F32), U32)
    return pltpu.bitcast((bits[:, :w] >> 16) | (bits[:, w:] & jnp.uint32(0xFFFF0000)), I32)


def _unpack_bf16_pairs(words):
    w = pltpu.bitcast(words, U32)
    lo = pltpu.bitcast(w << 16, F32)
    hi = pltpu.bitcast(w & jnp.uint32(0xFFFF0000), F32)
    return lo, hi


def _const_spec(shape):
    nd = len(shape)
    return pl.BlockSpec(shape, lambda *_: (0,) * nd)


def _proj_kernel(x_ref, wuv_ref, buv_ref, watt_ref, batt_ref, wg_ref, bg_ref, ng_ref, nb_ref,
                 sw_ref, sb_ref, pa_ref, ya_ref, gb_ref, a1_ref, a4_ref, a16_ref, xc_ref):
    tm = x_ref.shape[1]
    gw = GMLP_WIDTH
    xb = x_ref[0].astype(BF16)

    def proj(w, b):
        return jnp.dot(xb, w, preferred_element_type=F32) + b

    h_v = proj(wuv_ref[:, gw:], buv_ref[:, gw:])
    h_u = proj(wuv_ref[:, :gw], buv_ref[:, :gw])
    v = _layer_norm(_gelu(h_v), ng_ref[...], nb_ref[...]).astype(BF16)
    h_ga = proj(wg_ref[:, :D_MODEL], bg_ref[:, :D_MODEL])
    u = _gelu(h_u)

    cw = gw // GMLP_GROUPS
    row = lax.broadcasted_iota(I32, (GMLP_CHUNK, GMLP_CHUNK), 0)
    col = lax.broadcasted_iota(I32, (GMLP_CHUNK, GMLP_CHUNK), 1)
    ws = [jnp.where(row >= col, sw_ref[g], 0.0).astype(BF16) for g in range(GMLP_GROUPS)]
    chunks = []
    for c in range(tm // GMLP_CHUNK):
        vc = v[c * GMLP_CHUNK:(c + 1) * GMLP_CHUNK]
        cols = [jnp.dot(ws[g], vc[:, g * cw:(g + 1) * cw], preferred_element_type=F32)
                for g in range(GMLP_GROUPS)]
        chunks.append(jnp.concatenate(cols, axis=1) + sb_ref[...])
    vmix = jnp.concatenate(chunks, axis=0)
    h_gb = proj(wg_ref[:, D_MODEL:], bg_ref[:, D_MODEL:])
    ga = jax.nn.sigmoid(h_ga)
    ya = jnp.dot((u * vmix).astype(BF16), pa_ref[...], preferred_element_type=F32)
    gb_ref[0] = jax.nn.sigmoid(h_gb).astype(BF16)

    n_chunks = x_ref.shape[2] // LANES
    for c in range(n_chunks):
        xc_ref[c] = x_ref[0, :, c * LANES:(c + 1) * LANES]

    def attn_proj(p, d):
        n = tm // d
        if d == 1:
            xp = xb
        else:
            xp = jnp.concatenate(
                [jnp.concatenate([xc_ref[c, pl.ds(r, n, stride=d), :] for c in range(n_chunks)], axis=1)
                 for r in range(d)], axis=0).astype(BF16)
        return jnp.dot(xp, watt_ref[p], preferred_element_type=F32)

    def attn_store(p, d, h, a_ref):
        n = tm // d
        h = (h + batt_ref[p]).astype(BF16)
        for r in range(d):
            a_ref[0, r] = h[r * n:(r + 1) * n]

    a_refs = (a1_ref, a4_ref, a16_ref)
    dils = [d for _, d in DILATED_PATTERNS]
    h_prev = attn_proj(0, dils[0])
    ya_ref[0] = (ga * ya).astype(BF16)
    for p in range(1, len(dils)):
        h_next = attn_proj(p, dils[p])
        attn_store(p - 1, dils[p - 1], h_prev, a_refs[p - 1])
        h_prev = h_next
    attn_store(len(dils) - 1, dils[-1], h_prev, a_refs[-1])


def _input_projection(x, wuv, buv, watt, batt, wg, bg, ng, nb, sw, sb, pa):
    B, S, D = x.shape
    tm = PROJ_TILE
    grid = (B, S // tm)
    out_shape = [jax.ShapeDtypeStruct((B, S, D), BF16), jax.ShapeDtypeStruct((B, S, D), BF16)]
    out_specs = [pl.BlockSpec((1, tm, D), lambda b, t: (b, t, 0)),
                 pl.BlockSpec((1, tm, D), lambda b, t: (b, t, 0))]
    for _, d in DILATED_PATTERNS:
        out_shape.append(jax.ShapeDtypeStruct((B, d, S // d, ATTN_WIDTH), BF16))
        out_specs.append(pl.BlockSpec((1, d, tm // d, ATTN_WIDTH), lambda b, t: (b, 0, t, 0)))
    consts = (wuv, buv, watt, batt, wg, bg, ng, nb, sw, sb, pa)
    return pl.pallas_call(
        _proj_kernel,
        grid=grid,
        in_specs=[pl.BlockSpec((1, tm, D), lambda b, t: (b, t, 0))] + [_const_spec(c.shape) for c in consts],
        out_specs=out_specs,
        out_shape=out_shape,
        scratch_shapes=[pltpu.VMEM((D // LANES, tm, LANES), F32)],
        compiler_params=pltpu.CompilerParams(
            dimension_semantics=("arbitrary", "arbitrary"), vmem_limit_bytes=VMEM_LIMIT),
        name="input_projection",
    )(x, *consts)


def _attn_kernel(qkv_ref, bias_ref, o_ref, lse_ref, *, qblocks):
    nq = pl.program_id(2)
    gwid = GROUP_WIDTH
    blk = ATTN_BLOCK
    lane = lax.broadcasted_iota(I32, (1, gwid), 1)
    head_masks = [(lane >= h * HEAD_DIM) & (lane < (h + 1) * HEAD_DIM) for h in range(HEADS_PER_GROUP)]
    q_scales = [jnp.where(m, np.float32(HEAD_DIM ** -0.5), 0.0).astype(BF16) for m in head_masks]

    def rows_of(j):
        n = nq * qblocks + j
        return n, pl.multiple_of(n * blk, blk), pl.multiple_of(jnp.maximum(n - 1, 0) * blk, blk)

    def scores(j):
        n, q0, p0 = rows_of(j)
        q = qkv_ref[0, 0, pl.ds(q0, blk), 0:gwid]
        kk = jnp.concatenate([qkv_ref[0, 0, pl.ds(p0, blk), gwid:2 * gwid],
                              qkv_ref[0, 0, pl.ds(q0, blk), gwid:2 * gwid]], axis=0)
        qs = jnp.concatenate([q * s for s in q_scales], axis=0)
        s = lax.dot_general(qs, kk, (((1,), (1,)), ((), ())), preferred_element_type=F32)
        return s + bias_ref[jnp.where(n == 0, 1, 0)]

    def finish(j, s):
        _, q0, p0 = rows_of(j)
        vv = jnp.concatenate([qkv_ref[0, 0, pl.ds(p0, blk), 2 * gwid:3 * gwid],
                              qkv_ref[0, 0, pl.ds(q0, blk), 2 * gwid:3 * gwid]], axis=0)
        m = jnp.max(s, axis=1, keepdims=True)
        p = jnp.exp(s - m)
        den = jnp.sum(p, axis=1, keepdims=True)
        pv = jnp.dot(p.astype(BF16), vv, preferred_element_type=F32)
        on = pv / den
        lse = m + jnp.log(den)
        o = jnp.zeros((blk, gwid), F32)
        l = jnp.zeros((blk, gwid), F32)
        for h in range(HEADS_PER_GROUP):
            o = jnp.where(head_masks[h], on[h * blk:(h + 1) * blk], o)
            l = jnp.where(head_masks[h], lse[h * blk:(h + 1) * blk], l)
        o_ref[0, 0, j * blk:(j + 1) * blk, :] = o.astype(BF16)
        lse_ref[0, 0, j * blk:(j + 1) * blk, :] = l

    s_next = scores(0)
    for j in range(qblocks):
        s_cur = s_next
        if j + 1 < qblocks:
            s_next = scores(j + 1)
        finish(j, s_cur)


def _alibi_slopes(n):
    def pow2_slopes(m):
        start = 2.0 ** (-8.0 / m)
        return [start ** (i + 1) for i in range(m)]
    p = 2 ** int(math.floor(math.log2(n)))
    s = pow2_slopes(p)
    if p < n:
        s = s + pow2_slopes(2 * p)[0::2][: n - p]
    return np.array(sorted(s, reverse=True), dtype=np.float32)


def _attn_bias_tables(group, dilation):
    blk = ATTN_BLOCK
    slopes = _alibi_slopes(HEADS_PER_GROUP * len(DILATED_PATTERNS))
    slopes = slopes[group * HEADS_PER_GROUP:(group + 1) * HEADS_PER_GROUP]
    qi = np.arange(blk)[:, None]
    ki = np.arange(2 * blk)[None, :]
    delta = blk + qi - ki
    band = (delta >= 0) & (delta <= blk)
    bias = -slopes[:, None, None] * (delta * dilation).astype(np.float32)[None]
    full = np.where(band[None], bias, np.float32(MASKED_SCORE)).astype(np.float32)
    first = np.where((ki >= blk)[None], full, np.float32(MASKED_SCORE)).astype(np.float32)
    return full.reshape(HEADS_PER_GROUP * blk, 2 * blk), first.reshape(HEADS_PER_GROUP * blk, 2 * blk)


def _dilated_attention(qkv, group, dilation):
    B, d, sd, _ = qkv.shape
    qblocks = min(ATTN_QBLOCKS, sd // ATTN_BLOCK)
    rows = qblocks * ATTN_BLOCK
    bias = np.stack(_attn_bias_tables(group, dilation))
    grid = (B, d, sd // rows)
    out_spec = pl.BlockSpec((1, 1, rows, GROUP_WIDTH), lambda b, r, n: (b, r, n, 0))
    return pl.pallas_call(
        functools.partial(_attn_kernel, qblocks=qblocks),
        grid=grid,
        in_specs=[pl.BlockSpec((1, 1, sd, ATTN_WIDTH), lambda b, r, n: (b, r, 0, 0)),
                  _const_spec(bias.shape)],
        out_specs=[out_spec, out_spec],
        out_shape=[jax.ShapeDtypeStruct((B, d, sd, GROUP_WIDTH), BF16),
                   jax.ShapeDtypeStruct((B, d, sd, GROUP_WIDTH), F32)],
        compiler_params=pltpu.CompilerParams(
            dimension_semantics=("arbitrary", "arbitrary", "arbitrary"), vmem_limit_bytes=VMEM_LIMIT),
        name=f"dilated_attention_d{dilation}",
    )(qkv, jnp.asarray(bias))


def _merge_kernel(o1_ref, l1_ref, o4_ref, l4_ref, o16_ref, l16_ref, ya_ref, gb_ref, x_ref,
                  pb_ref, ow_ref, g1_ref, b1_ref, rwh_ref, rwl_ref, *rest, alpha, n_anchors):
    x1_ref, x1p_ref, sc_ref, so4, sl4, so16, sl16 = rest[n_anchors:]
    tm = x_ref.shape[1]
    n_chunks = GROUP_WIDTH // LANES
    for (o_ref, l_ref, so, sl, d) in ((o4_ref, l4_ref, so4, sl4, 4), (o16_ref, l16_ref, so16, sl16, 16)):
        n = tm // d
        for r in range(d):
            o_r = o_ref[0, r].astype(F32)
            l_r = l_ref[0, r]
            for c in range(n_chunks):
                so[c, pl.ds(r, n, stride=d), :] = o_r[:, c * LANES:(c + 1) * LANES]
                sl[c, pl.ds(r, n, stride=d), :] = l_r[:, c * LANES:(c + 1) * LANES]

    def natural(s):
        return jnp.concatenate([s[c] for c in range(n_chunks)], axis=1)

    l1 = l1_ref[0, 0]
    l4 = natural(sl4)
    l16 = natural(sl16)
    lmax = jnp.maximum(jnp.maximum(l1, l4), l16)
    e1 = jnp.exp(l1 - lmax)
    e4 = jnp.exp(l4 - lmax)
    e16 = jnp.exp(l16 - lmax)
    yb = (e1 * o1_ref[0, 0].astype(F32) + e4 * natural(so4) + e16 * natural(so16)) / (e1 + e4 + e16)
    ybp = jnp.dot(yb.astype(BF16), pb_ref[...], preferred_element_type=F32)
    merged = ya_ref[0].astype(F32) + gb_ref[0].astype(F32) * ybp
    mix = jnp.dot(merged.astype(BF16), ow_ref[...], preferred_element_type=F32)
    x1 = _layer_norm(alpha * x_ref[0] + mix, g1_ref[...], b1_ref[...])
    x1_ref[0] = x1
    x1p_ref[0] = _pack_bf16_pairs(x1)
    hi = x1.astype(BF16)
    lo = (x1 - hi.astype(F32)).astype(BF16)
    def logits_t(w_ref, xt):
        return lax.dot_general(w_ref[...], xt, (((1,), (1,)), ((), ())), preferred_element_type=F32)
    sc_ref[...] = jax.nn.sigmoid(logits_t(rwh_ref, hi) + logits_t(rwh_ref, lo) + logits_t(rwl_ref, hi))


def _merge_and_norm(attn_outs, ya, gb, x, pb, ow, g1, b1, rwh, rwl, alpha, anchors):
    B, S, D = x.shape
    tm = MERGE_TILE
    in_specs = []
    args = []
    for (o, l), (_, d) in zip(attn_outs, DILATED_PATTERNS):
        spec = pl.BlockSpec((1, d, tm // d, GROUP_WIDTH), lambda b, t: (b, 0, t, 0))
        in_specs += [spec, spec]
        args += [o, l]
    tok_spec = pl.BlockSpec((1, tm, D), lambda b, t: (b, t, 0))
    in_specs += [tok_spec, tok_spec, tok_spec]
    args += [ya, gb, x]
    consts = (pb, ow, g1, b1, rwh, rwl)
    in_specs += [_const_spec(c.shape) for c in consts]
    in_specs += [pl.BlockSpec(memory_space=pl.ANY) for _ in anchors]
    return pl.pallas_call(
        functools.partial(_merge_kernel, alpha=alpha, n_anchors=len(anchors)),
        grid=(B, S // tm),
        in_specs=in_specs,
        out_specs=[tok_spec,
                   pl.BlockSpec((1, tm, D // 2), lambda b, t: (b, t, 0)),
                   pl.BlockSpec((N_EXPERTS, tm), lambda b, t: (0, b * (S // tm) + t))],
        out_shape=[jax.ShapeDtypeStruct((B, S, D), F32),
                   jax.ShapeDtypeStruct((B, S, D // 2), I32),
                   jax.ShapeDtypeStruct((N_EXPERTS, B * S), F32)],
        scratch_shapes=[pltpu.VMEM((GROUP_WIDTH // LANES, tm, LANES), F32) for _ in range(4)],
        compiler_params=pltpu.CompilerParams(
            dimension_semantics=("arbitrary", "arbitrary"), vmem_limit_bytes=VMEM_LIMIT),
        name="merge_norm_router",
    )(*args, *consts, *anchors)


def _sortable_key(x):
    bits = pltpu.bitcast(x, I32)
    return jnp.where(bits < 0, bits ^ jnp.int32(0x7FFFFFFF), bits)


def _route_kernel(sc_ref, bias_ref, before_ref, eidx_ref, rank_ref, gate_ref, cnt_ref, carry_ref):
    ne, tm = sc_ref.shape
    gsize = ne // N_EXPERT_GROUPS
    neg_inf = np.float32(-np.inf)
    removed = jnp.int32(-2 ** 31)

    @pl.when(pl.program_id(0) == 0)
    def _():
        carry_ref[...] = jnp.zeros_like(carry_ref)

    scores = sc_ref[...]
    biased = scores + bias_ref[...]

    gsum = []
    for g in range(N_EXPERT_GROUPS):
        v = biased[g * gsize:(g + 1) * gsize]
        m1 = jnp.max(v, axis=0, keepdims=True)
        n1 = jnp.sum(jnp.where(v == m1, 1.0, 0.0), axis=0, keepdims=True)
        m2 = jnp.max(jnp.where(v < m1, v, neg_inf), axis=0, keepdims=True)
        gsum.append(m1 + jnp.where(n1 >= 2.0, m1, m2))
    gkey = _sortable_key(jnp.concatenate(gsum, axis=0))

    def pick_first_max(keys, ids, n_ids):
        m = jnp.max(keys, axis=0, keepdims=True)
        idx = jnp.min(jnp.where(keys == m, ids, n_ids), axis=0, keepdims=True)
        hit = ids == idx
        return idx, hit, jnp.where(hit, removed, keys)

    gid = lax.broadcasted_iota(I32, (N_EXPERT_GROUPS, tm), 0)
    for _ in range(TOPK_GROUPS):
        _, _, gkey = pick_first_max(gkey, gid, N_EXPERT_GROUPS)
    group_on = gkey == removed

    masked = jnp.concatenate(
        [jnp.where(group_on[g:g + 1], biased[g * gsize:(g + 1) * gsize], neg_inf)
         for g in range(N_EXPERT_GROUPS)], axis=0)
    keys = _sortable_key(masked)
    eid = lax.broadcasted_iota(I32, (ne, tm), 0)
    picks = []
    for _ in range(TOP_K):
        idx, _, keys = pick_first_max(keys, eid, ne)
        picks.append(idx)

    sel = jnp.where(keys == removed, 1.0, 0.0)
    ranks = jnp.dot(sel.astype(BF16), before_ref[...], preferred_element_type=F32) + carry_ref[...]
    carry_ref[...] = carry_ref[...] + jnp.sum(sel, axis=1, keepdims=True)
    cnt_ref[...] = carry_ref[...]

    s_k, r_k = [], []
    for idx in picks:
        hit = eid == idx
        s_k.append(jnp.sum(jnp.where(hit, scores, 0.0), axis=0, keepdims=True))
        r_k.append(jnp.sum(jnp.where(hit, ranks, 0.0), axis=0, keepdims=True))
    total = s_k[0]
    for s in s_k[1:]:
        total = total + s
    eidx_ref[...] = jnp.concatenate(picks, axis=0)
    rank_ref[...] = jnp.concatenate(r_k, axis=0).astype(I32)
    gate_ref[...] = jnp.concatenate([s / total * np.float32(ROUTED_SCALE) for s in s_k], axis=0)


def _route(scores_t, bias, tok0, T):
    ne = scores_t.shape[0]
    tm = ROUTE_TILE
    first = tok0 // tm
    before = jnp.asarray(np.triu(np.ones((tm, tm), np.float32), k=1), BF16)
    out_spec = pl.BlockSpec((TOP_K, tm), lambda i: (0, i))
    return pl.pallas_call(
        _route_kernel,
        grid=(T // tm,),
        in_specs=[pl.BlockSpec((ne, tm), lambda i: (0, first + i)), _const_spec(bias.shape),
                  _const_spec(before.shape)],
        out_specs=[out_spec, out_spec, out_spec, _const_spec((ne, 1))],
        out_shape=[jax.ShapeDtypeStruct((TOP_K, T), I32), jax.ShapeDtypeStruct((TOP_K, T), I32),
                   jax.ShapeDtypeStruct((TOP_K, T), F32), jax.ShapeDtypeStruct((ne, 1), F32)],
        scratch_shapes=[pltpu.VMEM((ne, 1), F32)],
        compiler_params=pltpu.CompilerParams(dimension_semantics=("arbitrary",), vmem_limit_bytes=VMEM_LIMIT),
        name="route_topk",
    )(scores_t, bias, before)


def _dest_kernel(pstart_ref, eidx_ref, rank_ref, out_ref):
    eidx = eidx_ref[...]
    start = jnp.zeros(eidx.shape, I32)
    for e in range(N_EXPERTS):
        start = jnp.where(eidx == e, pstart_ref[e], start)
    out_ref[...] = start + rank_ref[...]


def _dest_rows(eidx_t, rank_t, pstart):
    T = eidx_t.shape[1]
    tm = DEST_TILE
    tok_spec = pl.BlockSpec((TOP_K, tm), lambda i: (0, i))
    return pl.pallas_call(
        _dest_kernel,
        grid=(T // tm,),
        in_specs=[pl.BlockSpec(memory_space=pltpu.SMEM), tok_spec, tok_spec],
        out_specs=tok_spec,
        out_shape=jax.ShapeDtypeStruct((TOP_K, T), I32),
        compiler_params=pltpu.CompilerParams(dimension_semantics=("arbitrary",)),
        name="moe_dest_rows",
    )(pstart, eidx_t, rank_t)


def _sc_workers():
    info = plsc.get_sparse_core_info()
    return info.num_cores, info.num_cores * info.num_subcores


def _sc_scatter_rows(rows, row0, dest, n_out):
    n_dst, n_chunks, ch = dest.shape
    width = rows.shape[1]
    n_cores, n_workers = _sc_workers()
    per_w = n_chunks // n_workers
    assert n_chunks % n_workers == 0 and per_w % 8 == 0 and row0 + n_chunks * ch <= rows.shape[0]

    def body(rows_hbm, dest_hbm, out_hbm, idx_v, buf, lsem, ssem):
        wid = lax.axis_index("s") * n_cores + lax.axis_index("c")
        c0 = wid * per_w
        for k in range(n_dst):
            pltpu.sync_copy(dest_hbm.at[k, pl.ds(pl.multiple_of(c0, 8), per_w)], idx_v.at[k])

        def load(c, b):
            return pltpu.make_async_copy(rows_hbm.at[pl.ds(row0 + (c0 + c) * ch, ch)], buf.at[b], lsem.at[b])

        def scatter(c, b, k):
            return pltpu.make_async_copy(buf.at[b], out_hbm.at[idx_v.at[k, c]], ssem.at[b])

        load(0, 0).start()

        @pl.loop(0, per_w, step=2)
        def _(c):
            for b in range(2):
                cc = c + b
                load(cc, b).wait()

                @pl.when(cc >= 1)
                def _():
                    for k in range(n_dst):
                        scatter(cc - 1, 1 - b, k).wait()

                @pl.when(cc + 1 < per_w)
                def _():
                    load(cc + 1, 1 - b).start()

                for k in range(n_dst):
                    scatter(cc, b, k).start()

        for k in range(n_dst):
            scatter(per_w - 1, 1, k).wait()

    mesh = plsc.VectorSubcoreMesh(core_axis_name="c", subcore_axis_name="s")
    return pl.kernel(
        body, out_type=jax.ShapeDtypeStruct((n_out, width), rows.dtype), mesh=mesh,
        scratch_types=[pltpu.VMEM((n_dst, per_w, ch), I32), pltpu.VMEM((2, ch, width), rows.dtype),
                       pltpu.SemaphoreType.DMA((2,)), pltpu.SemaphoreType.DMA((2,))],
        name="moe_dispatch_scatter",
    )(rows, dest)


def _sc_gather_weighted_sum(table, idx, gates):
    n_src, n_tok = idx.shape
    ct = SC_SUM_CHUNK
    n_chunks = n_tok // ct
    half = table.shape[1]
    lanes = plsc.get_sparse_core_info().num_lanes
    n_cores, n_workers = _sc_workers()
    per_w = n_chunks // n_workers
    assert n_chunks % n_workers == 0 and per_w % 2 == 0 and half % lanes == 0

    def body(table_hbm, idx_hbm, gate_hbm, out_hbm, idx_v, gate_v, buf, out_v, gsem, wsem):
        wid = lax.axis_index("s") * n_cores + lax.axis_index("c")
        c0 = wid * per_w
        pltpu.sync_copy(idx_hbm.at[:, pl.ds(c0 * ct, per_w * ct)], idx_v)
        pltpu.sync_copy(gate_hbm.at[:, pl.ds(c0 * ct, per_w * ct)], gate_v)
        shift = jnp.full((lanes,), 16, I32)
        hi_mask = jnp.full((lanes,), -65536, I32)

        def gather(c, b, k):
            return pltpu.make_async_copy(table_hbm.at[idx_v.at[k, pl.ds(c * ct, ct)]], buf.at[b, k], gsem.at[b])

        def write(c, b):
            return pltpu.make_async_copy(out_v.at[b], out_hbm.at[pl.ds((c0 + c) * ct, ct)], wsem.at[b])

        for k in range(n_src):
            gather(0, 0, k).start()

        @pl.loop(0, per_w, step=2)
        def _(c):
            for b in range(2):
                cc = c + b
                for k in range(n_src):
                    gather(cc, b, k).wait()

                @pl.when(cc + 1 < per_w)
                def _():
                    for k in range(n_src):
                        gather(cc + 1, 1 - b, k).start()

                @pl.when(cc >= 2)
                def _():
                    write(cc - 2, b).wait()

                @pl.loop(0, ct)
                def _(t):
                    tok = jnp.full((lanes,), cc * ct + t, I32)
                    g = [plsc.load_gather(gate_v, [jnp.full((lanes,), k, I32), tok]) for k in range(n_src)]

                    @plsc.parallel_loop(0, half, step=lanes, unroll=2)
                    def _(col):
                        lo = jnp.zeros((lanes,), F32)
                        hi = jnp.zeros((lanes,), F32)
                        for k in range(n_src):
                            w = buf[b, k, t, pl.ds(col, lanes)]
                            lo = lo + g[k] * plsc.bitcast(lax.shift_left(w, shift), F32)
                            hi = hi + g[k] * plsc.bitcast(w & hi_mask, F32)
                        out_v[b, t, pl.ds(col, lanes)] = lo
                        out_v[b, t, pl.ds(half + col, lanes)] = hi

                write(cc, b).start()

        write(per_w - 2, 0).wait()
        write(per_w - 1, 1).wait()

    mesh = plsc.VectorSubcoreMesh(core_axis_name="c", subcore_axis_name="s")
    return pl.kernel(
        body, out_type=jax.ShapeDtypeStruct((n_chunks * ct, 2 * half), F32), mesh=mesh,
        scratch_types=[pltpu.VMEM((n_src, per_w * ct), I32), pltpu.VMEM((n_src, per_w * ct), F32),
                       pltpu.VMEM((2, n_src, ct, half), I32), pltpu.VMEM((2, ct, 2 * half), F32),
                       pltpu.SemaphoreType.DMA((2,)), pltpu.SemaphoreType.DMA((2,))],
        compiler_params=pltpu.CompilerParams(needs_layout_passes=False),
        name="moe_combine_gather_sum",
    )(table, idx, gates)


def _sc_pack_row_pairs(w):
    R, C = w.shape
    lanes = plsc.get_sparse_core_info().num_lanes
    cr = SC_PACK_CHUNK_WORDS // C
    n_chunks = R // cr
    n_cores, n_workers = _sc_workers()
    per_w = n_chunks // n_workers
    assert R % cr == 0 and n_chunks % n_workers == 0 and per_w % 2 == 0 and cr % 2 == 0 and C % lanes == 0

    def body(w_hbm, out_hbm, in_v, out_v, lsem, ssem):
        wid = lax.axis_index("s") * n_cores + lax.axis_index("c")
        c0 = wid * per_w

        def load(c, b):
            return pltpu.make_async_copy(w_hbm.at[pl.ds((c0 + c) * cr, cr)], in_v.at[b], lsem.at[b])

        def store(c, b):
            return pltpu.make_async_copy(out_v.at[b], out_hbm.at[pl.ds((c0 + c) * (cr // 2), cr // 2)],
                                         ssem.at[b])

        load(0, 0).start()

        @pl.loop(0, per_w, step=2)
        def _(c):
            for b in range(2):
                cc = c + b
                load(cc, b).wait()

                @pl.when(cc + 1 < per_w)
                def _():
                    load(cc + 1, 1 - b).start()

                @pl.when(cc >= 2)
                def _():
                    store(cc - 2, b).wait()

                @pl.loop(0, cr // 2)
                def _(i):
                    @plsc.parallel_loop(0, C, step=lanes, unroll=4)
                    def _(col):
                        even = in_v[b, 2 * i, pl.ds(col, lanes)]
                        odd = in_v[b, 2 * i + 1, pl.ds(col, lanes)]
                        pair = plsc.pack(even, odd, format=plsc.PackFormat.INTERLEAVED)
                        out_v[b, i, pl.ds(col, lanes)] = plsc.bitcast(pair, I32)

                store(cc, b).start()

        store(per_w - 2, 0).wait()
        store(per_w - 1, 1).wait()

    mesh = plsc.VectorSubcoreMesh(core_axis_name="c", subcore_axis_name="s")
    return pl.kernel(
        body, out_type=jax.ShapeDtypeStruct((R // 2, C), I32), mesh=mesh,
        scratch_types=[pltpu.VMEM((2, cr, C), F32), pltpu.VMEM((2, cr // 2, C), I32),
                       pltpu.SemaphoreType.DMA((2,)), pltpu.SemaphoreType.DMA((2,))],
        compiler_params=pltpu.CompilerParams(needs_layout_passes=False),
        name="expert_weights_bf16",
    )(w)


def _expert_kernel(be_ref, ue_ref, nu_ref, xs_hbm, wg_hbm, wu_hbm, wd_hbm, y_hbm,
                   xbuf, ybuf, wg_v, wu_v, wd_v, xsem, ysem, wsem):
    n_used = nu_ref[0]
    n_exp = nu_ref[1]
    half = D_MODEL // 2
    blk = MOE_BLOCK
    rows_gu = D_MODEL // 2
    rows_d = EXPERT_DIM // 2

    def rows(j):
        return pl.ds(pl.multiple_of(j * blk, blk), blk)

    def x_copy(j, p):
        return pltpu.make_async_copy(xs_hbm.at[rows(j)], xbuf.at[p], xsem.at[p])

    def y_copy(j, p):
        return pltpu.make_async_copy(ybuf.at[p], y_hbm.at[rows(j)], ysem.at[p])

    def w_copies(q, s):
        e = ue_ref[q]
        gu = pl.ds(pl.multiple_of(e * rows_gu, rows_gu), rows_gu)
        dn = pl.ds(pl.multiple_of(e * rows_d, rows_d), rows_d)
        return (pltpu.make_async_copy(wg_hbm.at[gu], wg_v.at[s], wsem.at[s, 0]),
                pltpu.make_async_copy(wu_hbm.at[gu], wu_v.at[s], wsem.at[s, 1]),
                pltpu.make_async_copy(wd_hbm.at[dn], wd_v.at[s], wsem.at[s, 2]))

    for q0 in range(W_SLOTS - 1):
        @pl.when(q0 < n_exp)
        def _():
            for c in w_copies(q0, q0):
                c.start()

    for j0 in range(X_SLOTS):
        @pl.when(j0 < n_used)
        def _():
            x_copy(j0, j0).start()

    def block_step(j, p, q):
        is_new = (j == 0) | (be_ref[j] != be_ref[jnp.maximum(j - 1, 0)])
        q = q + is_new.astype(I32)

        s = q % W_SLOTS

        @pl.when(is_new)
        def _():
            for c in w_copies(q, s):
                c.wait()

            @pl.when(q + W_SLOTS - 1 < n_exp)
            def _():
                for c in w_copies(q + W_SLOTS - 1, (q + W_SLOTS - 1) % W_SLOTS):
                    c.start()

        x_copy(j, p).wait()

        @pl.when(j >= X_SLOTS)
        def _():
            y_copy(j - X_SLOTS, p).wait()

        wg = pltpu.bitcast(wg_v[s], BF16)
        wu = pltpu.bitcast(wu_v[s], BF16)
        wd = pltpu.bitcast(wd_v[s], BF16)
        lo, hi = _unpack_bf16_pairs(xbuf[p])
        xlo = lo.astype(BF16)
        xhi = hi.astype(BF16)
        g = (jnp.dot(xlo, wg[:half], preferred_element_type=F32)
             + jnp.dot(xhi, wg[half:], preferred_element_type=F32))
        u = (jnp.dot(xlo, wu[:half], preferred_element_type=F32)
             + jnp.dot(xhi, wu[half:], preferred_element_type=F32))
        hb = (g * jax.nn.sigmoid(g) * u).astype(BF16)
        ybuf[p] = _pack_bf16_pairs(jnp.dot(hb, wd, preferred_element_type=F32))
        y_copy(j, p).start()

        @pl.when(j + X_SLOTS < n_used)
        def _():
            x_copy(j + X_SLOTS, p).start()

        return q

    def group(m, q):
        q = block_step(X_SLOTS * m, 0, q)
        for p in range(1, X_SLOTS):
            j = X_SLOTS * m + p
            q = lax.cond(j < n_used, functools.partial(block_step, j, p), lambda q: q, q)
        return q

    lax.fori_loop(0, (n_used + X_SLOTS - 1) // X_SLOTS, group, jnp.int32(-1))

    for back in range(X_SLOTS, 0, -1):
        b = n_used - back
        for p in range(X_SLOTS):
            @pl.when((b >= 0) & (b % X_SLOTS == p))
            def _():
                y_copy(b, p).wait()


def _experts(block_e, used_e, counts2, xs, wg, wu, wd):
    n_rows, half = xs.shape
    smem = pl.BlockSpec(memory_space=pltpu.SMEM)
    hbm = pl.BlockSpec(memory_space=pl.ANY)
    return pl.pallas_call(
        _expert_kernel,
        in_specs=[smem, smem, smem, hbm, hbm, hbm, hbm],
        out_specs=hbm,
        out_shape=jax.ShapeDtypeStruct((n_rows, half), I32),
        scratch_shapes=[pltpu.VMEM((X_SLOTS, MOE_BLOCK, half), I32), pltpu.VMEM((X_SLOTS, MOE_BLOCK, half), I32),
                        pltpu.VMEM((W_SLOTS, D_MODEL // 2, EXPERT_DIM), I32),
                        pltpu.VMEM((W_SLOTS, D_MODEL // 2, EXPERT_DIM), I32),
                        pltpu.VMEM((W_SLOTS, EXPERT_DIM // 2, D_MODEL), I32),
                        pltpu.SemaphoreType.DMA((X_SLOTS,)), pltpu.SemaphoreType.DMA((X_SLOTS,)),
                        pltpu.SemaphoreType.DMA((W_SLOTS, 3))],
        compiler_params=pltpu.CompilerParams(vmem_limit_bytes=VMEM_LIMIT),
        name="moe_experts",
    )(block_e, used_e, counts2, xs, wg, wu, wd)


def _combine_kernel(routed_ref, x1_ref, swgu_ref, swd_ref, g2_ref, b2_ref, *rest, alpha):
    out_ref = rest[-1]
    x1 = x1_ref[...]
    gu = jnp.dot(x1.astype(BF16), swgu_ref[...], preferred_element_type=F32)
    g = gu[:, :EXPERT_DIM]
    u = gu[:, EXPERT_DIM:]
    shared = jnp.dot((g * jax.nn.sigmoid(g) * u).astype(BF16), swd_ref[...], preferred_element_type=F32)
    out_ref[...] = _layer_norm(alpha * x1 + (routed_ref[...] + shared), g2_ref[...], b2_ref[...])


def _combine(routed, x1, tok0, swgu, swd, g2, b2, alpha, out_prev):
    T, D = x1.shape
    n_tok = routed.shape[0]
    tc = COMBINE_TILE
    first = tok0 // tc
    consts = (swgu, swd, g2, b2)
    args = [routed, x1, *consts]
    in_specs = [pl.BlockSpec((tc, D), lambda i: (i, 0)),
                pl.BlockSpec((tc, D), lambda i: (first + i, 0))] + [_const_spec(c.shape) for c in consts]
    aliases = {}
    if out_prev is not None:
        aliases = {len(args): 0}
        args.append(out_prev)
        in_specs.append(pl.BlockSpec(memory_space=pl.ANY))
    return pl.pallas_call(
        functools.partial(_combine_kernel, alpha=alpha),
        grid=(n_tok // tc,),
        in_specs=in_specs,
        out_specs=pl.BlockSpec((tc, D), lambda i: (first + i, 0)),
        out_shape=jax.ShapeDtypeStruct((T, D), F32),
        input_output_aliases=aliases,
        compiler_params=pltpu.CompilerParams(dimension_semantics=("arbitrary",), vmem_limit_bytes=VMEM_LIMIT),
        name="moe_combine_norm",
    )(*args)


def _mixer(x, in_w, in_b, ng, nb, spatial_w, spatial_b, proj_a_w, proj_b_w, out_w, ln1_g, ln1_b,
           router_w, alpha, anchors):
    gw, aw, D = GMLP_WIDTH, ATTN_WIDTH, D_MODEL
    w = in_w.astype(BF16)
    q0 = 2 * gw
    wuv, buv = w[:, :q0], in_b[None, :q0]
    watt = jnp.stack([jnp.concatenate([w[:, q0 + s * aw + p * GROUP_WIDTH:q0 + s * aw + (p + 1) * GROUP_WIDTH]
                                       for s in range(3)], axis=1) for p in range(len(DILATED_PATTERNS))])
    batt = jnp.stack([jnp.concatenate([in_b[q0 + s * aw + p * GROUP_WIDTH:q0 + s * aw + (p + 1) * GROUP_WIDTH]
                                       for s in range(3)])[None] for p in range(len(DILATED_PATTERNS))])
    g0 = q0 + 3 * aw
    wg, bg = w[:, g0:], in_b[None, g0:]
    sb = jnp.repeat(spatial_b.T, gw // GMLP_GROUPS, axis=1)
    ya, gb, a1, a4, a16 = _input_projection(
        x, wuv, buv, watt, batt, wg, bg, ng[None], nb[None], spatial_w, sb, proj_a_w.astype(BF16))
    attn_outs = [_dilated_attention(a, p, d) for p, (a, (_, d)) in enumerate(zip((a1, a4, a16), DILATED_PATTERNS))]
    rw_t = router_w.T
    rwh = rw_t.astype(BF16)
    rwl = (rw_t - rwh.astype(F32)).astype(BF16)
    return _merge_and_norm(attn_outs, ya, gb, x, proj_b_w.astype(BF16), out_w.astype(BF16),
                           ln1_g[None], ln1_b[None], rwh, rwl, alpha, anchors)


def _moe(x1, x1p, scores_t, router_bias, w_gate, w_up, w_down, sw_gate, sw_up, sw_down, ln2_g, ln2_b, alpha):
    swgu = jnp.concatenate([sw_gate, sw_up], axis=1).astype(BF16)
    swd = sw_down.astype(BF16)
    out = None
    tok0 = 0
    for eighths in MOE_TOKEN_SPLIT:
        n_tok = x1.shape[0] * eighths // 8
        n_blocks = (n_tok * TOP_K + N_EXPERTS * (MOE_BLOCK - 1)) // MOE_BLOCK
        eidx_t, rank_t, gate_t, counts = _route(scores_t, router_bias[:, None], tok0, n_tok)
        counts = counts[:, 0].astype(I32)
        padded = (counts + MOE_BLOCK - 1) // MOE_BLOCK * MOE_BLOCK
        pend = jnp.cumsum(padded).astype(I32)
        pstart = pend - padded
        block_starts = jnp.arange(n_blocks, dtype=I32) * MOE_BLOCK
        block_e = jnp.minimum(jnp.sum((pend[None, :] <= block_starts[:, None]).astype(I32), axis=1),
                              N_EXPERTS - 1)
        used = counts > 0
        used_e = jnp.argsort(jnp.logical_not(used), stable=True).astype(I32)
        counts2 = jnp.stack([pend[-1] // MOE_BLOCK, jnp.sum(used.astype(I32))]).astype(I32)
        dest_t = _dest_rows(eidx_t, rank_t, pstart)
        xs = _sc_scatter_rows(x1p, tok0, dest_t.reshape(TOP_K, n_tok // SC_CHUNK, SC_CHUNK), n_blocks * MOE_BLOCK)
        y_rows = _experts(block_e, used_e, counts2, xs, w_gate, w_up, w_down)
        routed = _sc_gather_weighted_sum(y_rows, dest_t, gate_t)
        out = _combine(routed, x1, tok0, swgu, swd, ln2_g[None], ln2_b[None], alpha, out)
        tok0 += n_tok
    return out


def kernel(x, in_w, in_b, gmlp_norm_g, gmlp_norm_b, spatial_w, spatial_b, proj_a_w, proj_b_w, out_w,
           ln1_g, ln1_b, router_w, router_bias, expert_w_gate, expert_w_up, expert_w_down,
           shared_w_gate, shared_w_up, shared_w_down, ln2_g, ln2_b):
    B, S, D = x.shape
    depth = in_w.shape[0]
    alpha = np.float32((2.0 * depth) ** 0.25)
    for l in range(depth):
        packed = [_sc_pack_row_pairs(w[l].reshape(-1, w.shape[-1]))
                  for w in (expert_w_gate, expert_w_up, expert_w_down)]
        x1, x1p, scores_t = _mixer(x, in_w[l], in_b[l], gmlp_norm_g[l], gmlp_norm_b[l], spatial_w[l],
                                 spatial_b[l], proj_a_w[l], proj_b_w[l], out_w[l], ln1_g[l], ln1_b[l],
                                 router_w[l], alpha, [p[:8] for p in packed])
        out = _moe(x1.reshape(B * S, D), x1p.reshape(B * S, D // 2), scores_t,
                   router_bias[l], *packed,
                   shared_w_gate[l], shared_w_up[l], shared_w_down[l], ln2_g[l], ln2_b[l], alpha)
        x = out.reshape(B, S, D)
    return x
```

```python
import functools
import math

import numpy as np
import jax
import jax.numpy as jnp
from jax import lax
from jax.experimental import pallas as pl
from jax.experimental.pallas import tpu as pltpu
from jax.experimental.pallas import tpu_sc as plsc

F32 = jnp.float32
BF16 = jnp.bfloat16
U32 = jnp.uint32
I32 = jnp.int32

D_MODEL = 1024
GMLP_WIDTH = 1024
GMLP_GROUPS = 8
GMLP_CHUNK = 128
HEAD_DIM = 64
DILATED_PATTERNS = ((128, 1), (512, 4), (2048, 16))
HEADS_PER_GROUP = 4
GROUP_WIDTH = HEADS_PER_GROUP * HEAD_DIM
ATTN_WIDTH = GROUP_WIDTH * len(DILATED_PATTERNS)
ATTN_BLOCK = 128
N_EXPERTS = 256
TOP_K = 8
TOP_K_SHIFT = 3
N_EXPERT_GROUPS = 8
TOPK_GROUPS = 4
EXPERT_DIM = 256
ROUTED_SCALE = 2.5
LN_EPS = 1e-5
LANES = 128
MASKED_SCORE = -1e30

PROJ_TILE = 512
ATTN_QBLOCKS = 4
MERGE_TILE = 512
ROUTE_TILE = 512
DEST_TILE = 2048
MOE_BLOCK = 256
SC_PACK_CHUNK_WORDS = 16384
SC_CHUNK = 64
SC_SUM_CHUNK = 8
MOE_TOKEN_SPLIT = (4, 4)
X_SLOTS = 4
W_SLOTS = 3
COMBINE_TILE = 512
VMEM_LIMIT = 56 * 1024 * 1024


def _layer_norm(y, g, b):
    mu = jnp.mean(y, axis=-1, keepdims=True)
    yc = y - mu
    var = jnp.mean(yc * yc, axis=-1, keepdims=True)
    return yc * lax.rsqrt(var + LN_EPS) * g + b


def _gelu(x):
    return 0.5 * x * (1.0 + lax.erf(x * np.float32(math.sqrt(0.5))))


def _pack_bf16_pairs(x):
    w = x.shape[1] // 2
    bits = pltpu.bitcast(x.astype(BF16).astype(F32), U32)
    return pltpu.bitcast((bits[:, :w] >> 16) | (bits[:, w:] & jnp.uint32(0xFFFF0000)), I32)


def _unpack_bf16_pairs(words):
    w = pltpu.bitcast(words, U32)
    lo = pltpu.bitcast(w << 16, F32)
    hi = pltpu.bitcast(w & jnp.uint32(0xFFFF0000), F32)
    return lo, hi


def _const_spec(shape):
    nd = len(shape)
    return pl.BlockSpec(shape, lambda *_: (0,) * nd)


def _proj_kernel(x_ref, wuv_ref, buv_ref, watt_ref, batt_ref, wg_ref, bg_ref, ng_ref, nb_ref,
                 sw_ref, sb_ref, pa_ref, ya_ref, gb_ref, a1_ref, a4_ref, a16_ref, xc_ref):
    tm = x_ref.shape[1]
    gw = GMLP_WIDTH
    xb = x_ref[0].astype(BF16)

    def proj(w, b):
        return jnp.dot(xb, w, preferred_element_type=F32) + b

    h_v = proj(wuv_ref[:, gw:], buv_ref[:, gw:])
    h_u = proj(wuv_ref[:, :gw], buv_ref[:, :gw])
    v = _layer_norm(_gelu(h_v), ng_ref[...], nb_ref[...]).astype(BF16)
    h_ga = proj(wg_ref[:, :D_MODEL], bg_ref[:, :D_MODEL])
    u = _gelu(h_u)

    cw = gw // GMLP_GROUPS
    row = lax.broadcasted_iota(I32, (GMLP_CHUNK, GMLP_CHUNK), 0)
    col = lax.broadcasted_iota(I32, (GMLP_CHUNK, GMLP_CHUNK), 1)
    ws = [jnp.where(row >= col, sw_ref[g], 0.0).astype(BF16) for g in range(GMLP_GROUPS)]
    chunks = []
    for c in range(tm // GMLP_CHUNK):
        vc = v[c * GMLP_CHUNK:(c + 1) * GMLP_CHUNK]
        cols = [jnp.dot(ws[g], vc[:, g * cw:(g + 1) * cw], preferred_element_type=F32)
                for g in range(GMLP_GROUPS)]
        chunks.append(jnp.concatenate(cols, axis=1) + sb_ref[...])
    vmix = jnp.concatenate(chunks, axis=0)
    h_gb = proj(wg_ref[:, D_MODEL:], bg_ref[:, D_MODEL:])
    ga = jax.nn.sigmoid(h_ga)
    ya = jnp.dot((u * vmix).astype(BF16), pa_ref[...], preferred_element_type=F32)
    gb_ref[0] = jax.nn.sigmoid(h_gb).astype(BF16)

    n_chunks = x_ref.shape[2] // LANES
    for c in range(n_chunks):
        xc_ref[c] = x_ref[0, :, c * LANES:(c + 1) * LANES]

    def attn_proj(p, d):
        n = tm // d
        if d == 1:
            xp = xb
        else:
            xp = jnp.concatenate(
                [jnp.concatenate([xc_ref[c, pl.ds(r, n, stride=d), :] for c in range(n_chunks)], axis=1)
                 for r in range(d)], axis=0).astype(BF16)
        return jnp.dot(xp, watt_ref[p], preferred_element_type=F32)

    def attn_store(p, d, h, a_ref):
        n = tm // d
        h = (h + batt_ref[p]).astype(BF16)
        for r in range(d):
            a_ref[0, r] = h[r * n:(r + 1) * n]

    a_refs = (a1_ref, a4_ref, a16_ref)
    dils = [d for _, d in DILATED_PATTERNS]
    h_prev = attn_proj(0, dils[0])
    ya_ref[0] = (ga * ya).astype(BF16)
    for p in range(1, len(dils)):
        h_next = attn_proj(p, dils[p])
        attn_store(p - 1, dils[p - 1], h_prev, a_refs[p - 1])
        h_prev = h_next
    attn_store(len(dils) - 1, dils[-1], h_prev, a_refs[-1])


def _input_projection(x, wuv, buv, watt, batt, wg, bg, ng, nb, sw, sb, pa):
    B, S, D = x.shape
    tm = PROJ_TILE
    grid = (B, S // tm)
    out_shape = [jax.ShapeDtypeStruct((B, S, D), BF16), jax.ShapeDtypeStruct((B, S, D), BF16)]
    out_specs = [pl.BlockSpec((1, tm, D), lambda b, t: (b, t, 0)),
                 pl.BlockSpec((1, tm, D), lambda b, t: (b, t, 0))]
    for _, d in DILATED_PATTERNS:
        out_shape.append(jax.ShapeDtypeStruct((B, d, S // d, ATTN_WIDTH), BF16))
        out_specs.append(pl.BlockSpec((1, d, tm // d, ATTN_WIDTH), lambda b, t: (b, 0, t, 0)))
    consts = (wuv, buv, watt, batt, wg, bg, ng, nb, sw, sb, pa)
    return pl.pallas_call(
        _proj_kernel,
        grid=grid,
        in_specs=[pl.BlockSpec((1, tm, D), lambda b, t: (b, t, 0))] + [_const_spec(c.shape) for c in consts],
        out_specs=out_specs,
        out_shape=out_shape,
        scratch_shapes=[pltpu.VMEM((D // LANES, tm, LANES), F32)],
        compiler_params=pltpu.CompilerParams(
            dimension_semantics=("arbitrary", "arbitrary"), vmem_limit_bytes=VMEM_LIMIT),
        name="input_projection",
    )(x, *consts)


def _attn_kernel(qkv_ref, bias_ref, o_ref, lse_ref, *, qblocks, rblock):
    nq = pl.program_id(2)
    gwid = GROUP_WIDTH
    blk = ATTN_BLOCK
    lane = lax.broadcasted_iota(I32, (1, gwid), 1)
    head_masks = [(lane >= h * HEAD_DIM) & (lane < (h + 1) * HEAD_DIM) for h in range(HEADS_PER_GROUP)]
    q_scales = [jnp.where(m, np.float32(HEAD_DIM ** -0.5), 0.0).astype(BF16) for m in head_masks]

    def rows_of(j):
        n = nq * qblocks + j
        return n, pl.multiple_of(n * blk, blk), pl.multiple_of(jnp.maximum(n - 1, 0) * blk, blk)

    def scores(unit):
        ri, j = unit
        n, q0, p0 = rows_of(j)
        q = qkv_ref[0, ri, pl.ds(q0, blk), 0:gwid]
        kk = jnp.concatenate([qkv_ref[0, ri, pl.ds(p0, blk), gwid:2 * gwid],
                              qkv_ref[0, ri, pl.ds(q0, blk), gwid:2 * gwid]], axis=0)
        qs = jnp.concatenate([q * s for s in q_scales], axis=0)
        s = lax.dot_general(qs, kk, (((1,), (1,)), ((), ())), preferred_element_type=F32)
        return s + bias_ref[jnp.where(n == 0, 1, 0)]

    def finish(unit, s):
        ri, j = unit
        _, q0, p0 = rows_of(j)
        vv = jnp.concatenate([qkv_ref[0, ri, pl.ds(p0, blk), 2 * gwid:3 * gwid],
                              qkv_ref[0, ri, pl.ds(q0, blk), 2 * gwid:3 * gwid]], axis=0)
        m = jnp.max(s, axis=1, keepdims=True)
        p = jnp.exp(s - m)
        den = jnp.sum(p, axis=1, keepdims=True)
        pv = jnp.dot(p.astype(BF16), vv, preferred_element_type=F32)
        on = pv / den
        lse = m + jnp.log(den)
        o = jnp.zeros((blk, gwid), F32)
        l = jnp.zeros((blk, gwid), F32)
        for h in range(HEADS_PER_GROUP):
            o = jnp.where(head_masks[h], on[h * blk:(h + 1) * blk], o)
            l = jnp.where(head_masks[h], lse[h * blk:(h + 1) * blk], l)
        o_ref[0, ri, j * blk:(j + 1) * blk, :] = o.astype(BF16)
        lse_ref[0, ri, j * blk:(j + 1) * blk, :] = l

    units = [(ri, j) for ri in range(rblock) for j in range(qblocks)]
    s_next = scores(units[0])
    for i, unit in enumerate(units):
        s_cur = s_next
        if i + 1 < len(units):
            s_next = scores(units[i + 1])
        finish(unit, s_cur)


def _alibi_slopes(n):
    def pow2_slopes(m):
        start = 2.0 ** (-8.0 / m)
        return [start ** (i + 1) for i in range(m)]
    p = 2 ** int(math.floor(math.log2(n)))
    s = pow2_slopes(p)
    if p < n:
        s = s + pow2_slopes(2 * p)[0::2][: n - p]
    return np.array(sorted(s, reverse=True), dtype=np.float32)


def _attn_bias_tables(group, dilation):
    blk = ATTN_BLOCK
    slopes = _alibi_slopes(HEADS_PER_GROUP * len(DILATED_PATTERNS))
    slopes = slopes[group * HEADS_PER_GROUP:(group + 1) * HEADS_PER_GROUP]
    qi = np.arange(blk)[:, None]
    ki = np.arange(2 * blk)[None, :]
    delta = blk + qi - ki
    band = (delta >= 0) & (delta <= blk)
    bias = -slopes[:, None, None] * (delta * dilation).astype(np.float32)[None]
    full = np.where(band[None], bias, np.float32(MASKED_SCORE)).astype(np.float32)
    first = np.where((ki >= blk)[None], full, np.float32(MASKED_SCORE)).astype(np.float32)
    return full.reshape(HEADS_PER_GROUP * blk, 2 * blk), first.reshape(HEADS_PER_GROUP * blk, 2 * blk)


def _dilated_attention(qkv, group, dilation):
    B, d, sd, _ = qkv.shape
    qblocks = min(ATTN_QBLOCKS, sd // ATTN_BLOCK)
    rblock = min(d, ATTN_QBLOCKS // qblocks)
    rows = qblocks * ATTN_BLOCK
    bias = np.stack(_attn_bias_tables(group, dilation))
    grid = (B, d // rblock, sd // rows)
    out_spec = pl.BlockSpec((1, rblock, rows, GROUP_WIDTH), lambda b, r, n: (b, r, n, 0))
    return pl.pallas_call(
        functools.partial(_attn_kernel, qblocks=qblocks, rblock=rblock),
        grid=grid,
        in_specs=[pl.BlockSpec((1, rblock, sd, ATTN_WIDTH), lambda b, r, n: (b, r, 0, 0)),
                  _const_spec(bias.shape)],
        out_specs=[out_spec, out_spec],
        out_shape=[jax.ShapeDtypeStruct((B, d, sd, GROUP_WIDTH), BF16),
                   jax.ShapeDtypeStruct((B, d, sd, GROUP_WIDTH), F32)],
        compiler_params=pltpu.CompilerParams(
            dimension_semantics=("arbitrary", "arbitrary", "arbitrary"), vmem_limit_bytes=VMEM_LIMIT),
        name=f"dilated_attention_d{dilation}",
    )(qkv, jnp.asarray(bias))


def _merge_kernel(o1_ref, l1_ref, o4_ref, l4_ref, o16_ref, l16_ref, ya_ref, gb_ref, x_ref,
                  pb_ref, ow_ref, g1_ref, b1_ref, rwh_ref, rwl_ref, *rest, alpha, n_anchors):
    x1_ref, x1p_ref, sc_ref, so4, sl4, so16, sl16 = rest[n_anchors:]
    tm = x_ref.shape[1]
    n_chunks = GROUP_WIDTH // LANES
    for (o_ref, l_ref, so, sl, d) in ((o4_ref, l4_ref, so4, sl4, 4), (o16_ref, l16_ref, so16, sl16, 16)):
        n = tm // d
        for r in range(d):
            o_r = o_ref[0, r].astype(F32)
            l_r = l_ref[0, r]
            for c in range(n_chunks):
                so[c, pl.ds(r, n, stride=d), :] = o_r[:, c * LANES:(c + 1) * LANES]
                sl[c, pl.ds(r, n, stride=d), :] = l_r[:, c * LANES:(c + 1) * LANES]

    def natural(s):
        return jnp.concatenate([s[c] for c in range(n_chunks)], axis=1)

    l1 = l1_ref[0, 0]
    l4 = natural(sl4)
    l16 = natural(sl16)
    lmax = jnp.maximum(jnp.maximum(l1, l4), l16)
    e1 = jnp.exp(l1 - lmax)
    e4 = jnp.exp(l4 - lmax)
    e16 = jnp.exp(l16 - lmax)
    yb = (e1 * o1_ref[0, 0].astype(F32) + e4 * natural(so4) + e16 * natural(so16)) / (e1 + e4 + e16)
    ybp = jnp.dot(yb.astype(BF16), pb_ref[...], preferred_element_type=F32)
    merged = ya_ref[0] + gb_ref[0] * ybp.astype(BF16)
    mix = jnp.dot(merged, ow_ref[...], preferred_element_type=F32)
    x1 = _layer_norm(alpha * x_ref[0] + mix, g1_ref[...], b1_ref[...])
    x1_ref[0] = x1
    x1p_ref[0] = _pack_bf16_pairs(x1)
    hi = x1.astype(BF16)
    lo = (x1 - hi.astype(F32)).astype(BF16)
    def logits_t(w_ref, xt):
        return lax.dot_general(w_ref[...], xt, (((1,), (1,)), ((), ())), preferred_element_type=F32)
    sc_ref[...] = jax.nn.sigmoid(logits_t(rwh_ref, hi) + logits_t(rwh_ref, lo) + logits_t(rwl_ref, hi))


def _merge_and_norm(attn_outs, ya, gb, x, pb, ow, g1, b1, rwh, rwl, alpha, anchors):
    B, S, D = x.shape
    tm = MERGE_TILE
    in_specs = []
    args = []
    for (o, l), (_, d) in zip(attn_outs, DILATED_PATTERNS):
        spec = pl.BlockSpec((1, d, tm // d, GROUP_WIDTH), lambda b, t: (b, 0, t, 0))
        in_specs += [spec, spec]
        args += [o, l]
    tok_spec = pl.BlockSpec((1, tm, D), lambda b, t: (b, t, 0))
    in_specs += [tok_spec, tok_spec, tok_spec]
    args += [ya, gb, x]
    consts = (pb, ow, g1, b1, rwh, rwl)
    in_specs += [_const_spec(c.shape) for c in consts]
    in_specs += [pl.BlockSpec(memory_space=pl.ANY) for _ in anchors]
    return pl.pallas_call(
        functools.partial(_merge_kernel, alpha=alpha, n_anchors=len(anchors)),
        grid=(B, S // tm),
        in_specs=in_specs,
        out_specs=[tok_spec,
                   pl.BlockSpec((1, tm, D // 2), lambda b, t: (b, t, 0)),
                   pl.BlockSpec((N_EXPERTS, tm), lambda b, t: (0, b * (S // tm) + t))],
        out_shape=[jax.ShapeDtypeStruct((B, S, D), F32),
                   jax.ShapeDtypeStruct((B, S, D // 2), I32),
                   jax.ShapeDtypeStruct((N_EXPERTS, B * S), F32)],
        scratch_shapes=[pltpu.VMEM((GROUP_WIDTH // LANES, tm, LANES), F32) for _ in range(4)],
        compiler_params=pltpu.CompilerParams(
            dimension_semantics=("arbitrary", "arbitrary"), vmem_limit_bytes=VMEM_LIMIT),
        name="merge_norm_router",
    )(*args, *consts, *anchors)


def _sortable_key(x):
    bits = pltpu.bitcast(x, I32)
    return jnp.where(bits < 0, bits ^ jnp.int32(0x7FFFFFFF), bits)


def _route_kernel(sc_ref, bias_ref, before_ref, eidx_ref, rank_ref, gate_ref, cnt_ref, carry_ref):
    ne, tm = sc_ref.shape
    gsize = ne // N_EXPERT_GROUPS
    neg_inf = np.float32(-np.inf)
    removed = jnp.int32(-2 ** 31)

    @pl.when(pl.program_id(0) == 0)
    def _():
        carry_ref[...] = jnp.zeros_like(carry_ref)

    scores = sc_ref[...]
    biased = scores + bias_ref[...]

    gsum = []
    for g in range(N_EXPERT_GROUPS):
        v = biased[g * gsize:(g + 1) * gsize]
        m1 = jnp.max(v, axis=0, keepdims=True)
        n1 = jnp.sum(jnp.where(v == m1, 1.0, 0.0), axis=0, keepdims=True)
        m2 = jnp.max(jnp.where(v < m1, v, neg_inf), axis=0, keepdims=True)
        gsum.append(m1 + jnp.where(n1 >= 2.0, m1, m2))
    gkey = _sortable_key(jnp.concatenate(gsum, axis=0))

    def pick_first_max(keys, ids, n_ids):
        m = jnp.max(keys, axis=0, keepdims=True)
        idx = jnp.min(jnp.where(keys == m, ids, n_ids), axis=0, keepdims=True)
        hit = ids == idx
        return idx, hit, jnp.where(hit, removed, keys)

    gid = lax.broadcasted_iota(I32, (N_EXPERT_GROUPS, tm), 0)
    for _ in range(TOPK_GROUPS):
        _, _, gkey = pick_first_max(gkey, gid, N_EXPERT_GROUPS)
    group_on = gkey == removed

    masked = jnp.concatenate(
        [jnp.where(group_on[g:g + 1], biased[g * gsize:(g + 1) * gsize], neg_inf)
         for g in range(N_EXPERT_GROUPS)], axis=0)
    keys = _sortable_key(masked)
    eid = lax.broadcasted_iota(I32, (ne, tm), 0)
    picks = []
    for _ in range(TOP_K):
        idx, _, keys = pick_first_max(keys, eid, ne)
        picks.append(idx)

    sel = jnp.where(keys == removed, 1.0, 0.0)
    ranks = jnp.dot(sel.astype(BF16), before_ref[...], preferred_element_type=F32) + carry_ref[...]
    carry_ref[...] = carry_ref[...] + jnp.sum(sel, axis=1, keepdims=True)
    cnt_ref[...] = carry_ref[...]

    s_k, r_k = [], []
    for idx in picks:
        hit = eid == idx
        s_k.append(jnp.sum(jnp.where(hit, scores, 0.0), axis=0, keepdims=True))
        r_k.append(jnp.sum(jnp.where(hit, ranks, 0.0), axis=0, keepdims=True))
    total = s_k[0]
    for s in s_k[1:]:
        total = total + s
    eidx_ref[...] = jnp.concatenate(picks, axis=0)
    rank_ref[...] = jnp.concatenate(r_k, axis=0).astype(I32)
    gate_ref[...] = jnp.concatenate([s / total * np.float32(ROUTED_SCALE) for s in s_k], axis=0)


def _route(scores_t, bias, tok0, T):
    ne = scores_t.shape[0]
    tm = ROUTE_TILE
    first = tok0 // tm
    before = jnp.asarray(np.triu(np.ones((tm, tm), np.float32), k=1), BF16)
    out_spec = pl.BlockSpec((TOP_K, tm), lambda i: (0, i))
    return pl.pallas_call(
        _route_kernel,
        grid=(T // tm,),
        in_specs=[pl.BlockSpec((ne, tm), lambda i: (0, first + i)), _const_spec(bias.shape),
                  _const_spec(before.shape)],
        out_specs=[out_spec, out_spec, out_spec, _const_spec((ne, 1))],
        out_shape=[jax.ShapeDtypeStruct((TOP_K, T), I32), jax.ShapeDtypeStruct((TOP_K, T), I32),
                   jax.ShapeDtypeStruct((TOP_K, T), F32), jax.ShapeDtypeStruct((ne, 1), F32)],
        scratch_shapes=[pltpu.VMEM((ne, 1), F32)],
        compiler_params=pltpu.CompilerParams(dimension_semantics=("arbitrary",), vmem_limit_bytes=VMEM_LIMIT),
        name="route_topk",
    )(scores_t, bias, before)


def _dest_kernel(pstart_ref, eidx_ref, rank_ref, out_ref):
    eidx = eidx_ref[...]
    start = jnp.zeros(eidx.shape, I32)
    for e in range(N_EXPERTS):
        start = jnp.where(eidx == e, pstart_ref[e], start)
    out_ref[...] = start + rank_ref[...]


def _dest_rows(eidx_t, rank_t, pstart):
    T = eidx_t.shape[1]
    tm = DEST_TILE
    tok_spec = pl.BlockSpec((TOP_K, tm), lambda i: (0, i))
    return pl.pallas_call(
        _dest_kernel,
        grid=(T // tm,),
        in_specs=[pl.BlockSpec(memory_space=pltpu.SMEM), tok_spec, tok_spec],
        out_specs=tok_spec,
        out_shape=jax.ShapeDtypeStruct((TOP_K, T), I32),
        compiler_params=pltpu.CompilerParams(dimension_semantics=("arbitrary",)),
        name="moe_dest_rows",
    )(pstart, eidx_t, rank_t)


def _sc_workers():
    info = plsc.get_sparse_core_info()
    return info.num_cores, info.num_cores * info.num_subcores


def _sc_scatter_rows(rows, row0, dest, n_out):
    n_dst, n_chunks, ch = dest.shape
    width = rows.shape[1]
    n_cores, n_workers = _sc_workers()
    per_w = n_chunks // n_workers
    assert n_chunks % n_workers == 0 and per_w % 8 == 0 and row0 + n_chunks * ch <= rows.shape[0]

    def body(rows_hbm, dest_hbm, out_hbm, idx_v, buf, lsem, ssem):
        wid = lax.axis_index("s") * n_cores + lax.axis_index("c")
        c0 = wid * per_w
        for k in range(n_dst):
            pltpu.sync_copy(dest_hbm.at[k, pl.ds(pl.multiple_of(c0, 8), per_w)], idx_v.at[k])

        def load(c, b):
            return pltpu.make_async_copy(rows_hbm.at[pl.ds(row0 + (c0 + c) * ch, ch)], buf.at[b], lsem.at[b])

        def scatter(c, b, k):
            return pltpu.make_async_copy(buf.at[b], out_hbm.at[idx_v.at[k, c]], ssem.at[b])

        load(0, 0).start()

        @pl.loop(0, per_w, step=2)
        def _(c):
            for b in range(2):
                cc = c + b
                load(cc, b).wait()

                @pl.when(cc >= 1)
                def _():
                    for k in range(n_dst):
                        scatter(cc - 1, 1 - b, k).wait()

                @pl.when(cc + 1 < per_w)
                def _():
                    load(cc + 1, 1 - b).start()

                for k in range(n_dst):
                    scatter(cc, b, k).start()

        for k in range(n_dst):
            scatter(per_w - 1, 1, k).wait()

    mesh = plsc.VectorSubcoreMesh(core_axis_name="c", subcore_axis_name="s")
    return pl.kernel(
        body, out_type=jax.ShapeDtypeStruct((n_out, width), rows.dtype), mesh=mesh,
        scratch_types=[pltpu.VMEM((n_dst, per_w, ch), I32), pltpu.VMEM((2, ch, width), rows.dtype),
                       pltpu.SemaphoreType.DMA((2,)), pltpu.SemaphoreType.DMA((2,))],
        name="moe_dispatch_scatter",
    )(rows, dest)


def _sc_gather_weighted_sum(table, idx, gates):
    n_src, n_tok = idx.shape
    ct = SC_SUM_CHUNK
    n_chunks = n_tok // ct
    half = table.shape[1]
    lanes = plsc.get_sparse_core_info().num_lanes
    n_cores, n_workers = _sc_workers()
    per_w = n_chunks // n_workers
    assert n_chunks % n_workers == 0 and per_w % 2 == 0 and half % lanes == 0

    def body(table_hbm, idx_hbm, gate_hbm, out_hbm, idx_v, gate_v, buf, out_v, gsem, wsem):
        wid = lax.axis_index("s") * n_cores + lax.axis_index("c")
        c0 = wid * per_w
        pltpu.sync_copy(idx_hbm.at[:, pl.ds(c0 * ct, per_w * ct)], idx_v)
        pltpu.sync_copy(gate_hbm.at[:, pl.ds(c0 * ct, per_w * ct)], gate_v)
        shift = jnp.full((lanes,), 16, I32)
        hi_mask = jnp.full((lanes,), -65536, I32)

        def gather(c, b, k):
            return pltpu.make_async_copy(table_hbm.at[idx_v.at[k, pl.ds(c * ct, ct)]], buf.at[b, k], gsem.at[b])

        def write(c, b):
            return pltpu.make_async_copy(out_v.at[b], out_hbm.at[pl.ds((c0 + c) * ct, ct)], wsem.at[b])

        for k in range(n_src):
            gather(0, 0, k).start()

        @pl.loop(0, per_w, step=2)
        def _(c):
            for b in range(2):
                cc = c + b
                for k in range(n_src):
                    gather(cc, b, k).wait()

                @pl.when(cc + 1 < per_w)
                def _():
                    for k in range(n_src):
                        gather(cc + 1, 1 - b, k).start()

                @pl.when(cc >= 2)
                def _():
                    write(cc - 2, b).wait()

                @pl.loop(0, ct)
                def _(t):
                    tok = jnp.full((lanes,), cc * ct + t, I32)
                    g = [plsc.load_gather(gate_v, [jnp.full((lanes,), k, I32), tok]) for k in range(n_src)]

                    @plsc.parallel_loop(0, half, step=lanes, unroll=2)
                    def _(col):
                        lo = jnp.zeros((lanes,), F32)
                        hi = jnp.zeros((lanes,), F32)
                        for k in range(n_src):
                            w = buf[b, k, t, pl.ds(col, lanes)]
                            lo = lo + g[k] * plsc.bitcast(lax.shift_left(w, shift), F32)
                            hi = hi + g[k] * plsc.bitcast(w & hi_mask, F32)
                        out_v[b, t, pl.ds(col, lanes)] = lo
                        out_v[b, t, pl.ds(half + col, lanes)] = hi

                write(cc, b).start()

        write(per_w - 2, 0).wait()
        write(per_w - 1, 1).wait()

    mesh = plsc.VectorSubcoreMesh(core_axis_name="c", subcore_axis_name="s")
    return pl.kernel(
        body, out_type=jax.ShapeDtypeStruct((n_chunks * ct, 2 * half), F32), mesh=mesh,
        scratch_types=[pltpu.VMEM((n_src, per_w * ct), I32), pltpu.VMEM((n_src, per_w * ct), F32),
                       pltpu.VMEM((2, n_src, ct, half), I32), pltpu.VMEM((2, ct, 2 * half), F32),
                       pltpu.SemaphoreType.DMA((2,)), pltpu.SemaphoreType.DMA((2,))],
        compiler_params=pltpu.CompilerParams(needs_layout_passes=False),
        name="moe_combine_gather_sum",
    )(table, idx, gates)


def _sc_pack_row_pairs(w):
    R, C = w.shape
    lanes = plsc.get_sparse_core_info().num_lanes
    cr = SC_PACK_CHUNK_WORDS // C
    n_chunks = R // cr
    n_cores, n_workers = _sc_workers()
    per_w = n_chunks // n_workers
    assert R % cr == 0 and n_chunks % n_workers == 0 and per_w % 2 == 0 and cr % 2 == 0 and C % lanes == 0

    def body(w_hbm, out_hbm, in_v, out_v, lsem, ssem):
        wid = lax.axis_index("s") * n_cores + lax.axis_index("c")
        c0 = wid * per_w

        def load(c, b):
            return pltpu.make_async_copy(w_hbm.at[pl.ds((c0 + c) * cr, cr)], in_v.at[b], lsem.at[b])

        def store(c, b):
            return pltpu.make_async_copy(out_v.at[b], out_hbm.at[pl.ds((c0 + c) * (cr // 2), cr // 2)],
                                         ssem.at[b])

        load(0, 0).start()

        @pl.loop(0, per_w, step=2)
        def _(c):
            for b in range(2):
                cc = c + b
                load(cc, b).wait()

                @pl.when(cc + 1 < per_w)
                def _():
                    load(cc + 1, 1 - b).start()

                @pl.when(cc >= 2)
                def _():
                    store(cc - 2, b).wait()

                @pl.loop(0, cr // 2)
                def _(i):
                    @plsc.parallel_loop(0, C, step=lanes, unroll=4)
                    def _(col):
                        even = in_v[b, 2 * i, pl.ds(col, lanes)]
                        odd = in_v[b, 2 * i + 1, pl.ds(col, lanes)]
                        pair = plsc.pack(even, odd, format=plsc.PackFormat.INTERLEAVED)
                        out_v[b, i, pl.ds(col, lanes)] = plsc.bitcast(pair, I32)

                store(cc, b).start()

        store(per_w - 2, 0).wait()
        store(per_w - 1, 1).wait()

    mesh = plsc.VectorSubcoreMesh(core_axis_name="c", subcore_axis_name="s")
    return pl.kernel(
        body, out_type=jax.ShapeDtypeStruct((R // 2, C), I32), mesh=mesh,
        scratch_types=[pltpu.VMEM((2, cr, C), F32), pltpu.VMEM((2, cr // 2, C), I32),
                       pltpu.SemaphoreType.DMA((2,)), pltpu.SemaphoreType.DMA((2,))],
        compiler_params=pltpu.CompilerParams(needs_layout_passes=False),
        name="expert_weights_bf16",
    )(w)


def _expert_kernel(be_ref, ue_ref, nu_ref, xs_hbm, wg_hbm, wu_hbm, wd_hbm, y_hbm,
                   xbuf, ybuf, wg_v, wu_v, wd_v, xsem, ysem, wsem):
    n_used = nu_ref[0]
    n_exp = nu_ref[1]
    half = D_MODEL // 2
    blk = MOE_BLOCK
    rows_gu = D_MODEL // 2
    rows_d = EXPERT_DIM // 2

    def rows(j):
        return pl.ds(pl.multiple_of(j * blk, blk), blk)

    def x_copy(j, p):
        return pltpu.make_async_copy(xs_hbm.at[rows(j)], xbuf.at[p], xsem.at[p])

    def y_copy(j, p):
        return pltpu.make_async_copy(ybuf.at[p], y_hbm.at[rows(j)], ysem.at[p])

    def w_copies(q, s):
        e = ue_ref[q]
        gu = pl.ds(pl.multiple_of(e * rows_gu, rows_gu), rows_gu)
        dn = pl.ds(pl.multiple_of(e * rows_d, rows_d), rows_d)
        return (pltpu.make_async_copy(wg_hbm.at[gu], wg_v.at[s], wsem.at[s, 0]),
                pltpu.make_async_copy(wu_hbm.at[gu], wu_v.at[s], wsem.at[s, 1]),
                pltpu.make_async_copy(wd_hbm.at[dn], wd_v.at[s], wsem.at[s, 2]))

    for q0 in range(W_SLOTS - 1):
        @pl.when(q0 < n_exp)
        def _():
            for c in w_copies(q0, q0):
                c.start()

    for j0 in range(X_SLOTS):
        @pl.when(j0 < n_used)
        def _():
            x_copy(j0, j0).start()

    def block_step(j, p, q):
        is_new = (j == 0) | (be_ref[j] != be_ref[jnp.maximum(j - 1, 0)])
        q = q + is_new.astype(I32)

        s = q % W_SLOTS

        @pl.when(is_new)
        def _():
            for c in w_copies(q, s):
                c.wait()

            @pl.when(q + W_SLOTS - 1 < n_exp)
            def _():
                for c in w_copies(q + W_SLOTS - 1, (q + W_SLOTS - 1) % W_SLOTS):
                    c.start()

        x_copy(j, p).wait()

        @pl.when(j >= X_SLOTS)
        def _():
            y_copy(j - X_SLOTS, p).wait()

        wg = pltpu.bitcast(wg_v[s], BF16)
        wu = pltpu.bitcast(wu_v[s], BF16)
        wd = pltpu.bitcast(wd_v[s], BF16)
        lo, hi = _unpack_bf16_pairs(xbuf[p])
        xlo = lo.astype(BF16)
        xhi = hi.astype(BF16)
        g = (jnp.dot(xlo, wg[:half], preferred_element_type=F32)
             + jnp.dot(xhi, wg[half:], preferred_element_type=F32))
        u = (jnp.dot(xlo, wu[:half], preferred_element_type=F32)
             + jnp.dot(xhi, wu[half:], preferred_element_type=F32))
        hb = (g * jax.nn.sigmoid(g) * u).astype(BF16)
        ybuf[p] = _pack_bf16_pairs(jnp.dot(hb, wd, preferred_element_type=F32))
        y_copy(j, p).start()

        @pl.when(j + X_SLOTS < n_used)
        def _():
            x_copy(j + X_SLOTS, p).start()

        return q

    def group(m, q):
        q = block_step(X_SLOTS * m, 0, q)
        for p in range(1, X_SLOTS):
            j = X_SLOTS * m + p
            q = lax.cond(j < n_used, functools.partial(block_step, j, p), lambda q: q, q)
        return q

    lax.fori_loop(0, (n_used + X_SLOTS - 1) // X_SLOTS, group, jnp.int32(-1))

    for back in range(X_SLOTS, 0, -1):
        b = n_used - back
        for p in range(X_SLOTS):
            @pl.when((b >= 0) & (b % X_SLOTS == p))
            def _():
                y_copy(b, p).wait()


def _experts(block_e, used_e, counts2, xs, wg, wu, wd):
    n_rows, half = xs.shape
    smem = pl.BlockSpec(memory_space=pltpu.SMEM)
    hbm = pl.BlockSpec(memory_space=pl.ANY)
    return pl.pallas_call(
        _expert_kernel,
        in_specs=[smem, smem, smem, hbm, hbm, hbm, hbm],
        out_specs=hbm,
        out_shape=jax.ShapeDtypeStruct((n_rows, half), I32),
        scratch_shapes=[pltpu.VMEM((X_SLOTS, MOE_BLOCK, half), I32), pltpu.VMEM((X_SLOTS, MOE_BLOCK, half), I32),
                        pltpu.VMEM((W_SLOTS, D_MODEL // 2, EXPERT_DIM), I32),
                        pltpu.VMEM((W_SLOTS, D_MODEL // 2, EXPERT_DIM), I32),
                        pltpu.VMEM((W_SLOTS, EXPERT_DIM // 2, D_MODEL), I32),
                        pltpu.SemaphoreType.DMA((X_SLOTS,)), pltpu.SemaphoreType.DMA((X_SLOTS,)),
                        pltpu.SemaphoreType.DMA((W_SLOTS, 3))],
        compiler_params=pltpu.CompilerParams(vmem_limit_bytes=VMEM_LIMIT),
        name="moe_experts",
    )(block_e, used_e, counts2, xs, wg, wu, wd)


def _combine_kernel(routed_ref, x1_ref, swgu_ref, swd_ref, g2_ref, b2_ref, *rest, alpha):
    out_ref = rest[-1]
    x1 = x1_ref[...]
    gu = jnp.dot(x1.astype(BF16), swgu_ref[...], preferred_element_type=F32)
    g = gu[:, :EXPERT_DIM]
    u = gu[:, EXPERT_DIM:]
    shared = jnp.dot((g * jax.nn.sigmoid(g) * u).astype(BF16), swd_ref[...], preferred_element_type=F32)
    out_ref[...] = _layer_norm(alpha * x1 + (routed_ref[...] + shared), g2_ref[...], b2_ref[...])


def _combine(routed, x1, tok0, swgu, swd, g2, b2, alpha, out_prev):
    T, D = x1.shape
    n_tok = routed.shape[0]
    tc = COMBINE_TILE
    first = tok0 // tc
    consts = (swgu, swd, g2, b2)
    args = [routed, x1, *consts]
    in_specs = [pl.BlockSpec((tc, D), lambda i: (i, 0)),
                pl.BlockSpec((tc, D), lambda i: (first + i, 0))] + [_const_spec(c.shape) for c in consts]
    aliases = {}
    if out_prev is not None:
        aliases = {len(args): 0}
        args.append(out_prev)
        in_specs.append(pl.BlockSpec(memory_space=pl.ANY))
    return pl.pallas_call(
        functools.partial(_combine_kernel, alpha=alpha),
        grid=(n_tok // tc,),
        in_specs=in_specs,
        out_specs=pl.BlockSpec((tc, D), lambda i: (first + i, 0)),
        out_shape=jax.ShapeDtypeStruct((T, D), F32),
        input_output_aliases=aliases,
        compiler_params=pltpu.CompilerParams(dimension_semantics=("arbitrary",), vmem_limit_bytes=VMEM_LIMIT),
        name="moe_combine_norm",
    )(*args)


def _mixer(x, in_w, in_b, ng, nb, spatial_w, spatial_b, proj_a_w, proj_b_w, out_w, ln1_g, ln1_b,
           router_w, alpha, anchors):
    gw, aw, D = GMLP_WIDTH, ATTN_WIDTH, D_MODEL
    w = in_w.astype(BF16)
    q0 = 2 * gw
    wuv, buv = w[:, :q0], in_b[None, :q0]
    watt = jnp.stack([jnp.concatenate([w[:, q0 + s * aw + p * GROUP_WIDTH:q0 + s * aw + (p + 1) * GROUP_WIDTH]
                                       for s in range(3)], axis=1) for p in range(len(DILATED_PATTERNS))])
    batt = jnp.stack([jnp.concatenate([in_b[q0 + s * aw + p * GROUP_WIDTH:q0 + s * aw + (p + 1) * GROUP_WIDTH]
                                       for s in range(3)])[None] for p in range(len(DILATED_PATTERNS))])
    g0 = q0 + 3 * aw
    wg, bg = w[:, g0:], in_b[None, g0:]
    sb = jnp.repeat(spatial_b.T, gw // GMLP_GROUPS, axis=1)
    ya, gb, a1, a4, a16 = _input_projection(
        x, wuv, buv, watt, batt, wg, bg, ng[None], nb[None], spatial_w, sb, proj_a_w.astype(BF16))
    attn_outs = [_dilated_attention(a, p, d) for p, (a, (_, d)) in enumerate(zip((a1, a4, a16), DILATED_PATTERNS))]
    rw_t = router_w.T
    rwh = rw_t.astype(BF16)
    rwl = (rw_t - rwh.astype(F32)).astype(BF16)
    return _merge_and_norm(attn_outs, ya, gb, x, proj_b_w.astype(BF16), out_w.astype(BF16),
                           ln1_g[None], ln1_b[None], rwh, rwl, alpha, anchors)


def _moe(x1, x1p, scores_t, router_bias, w_gate, w_up, w_down, sw_gate, sw_up, sw_down, ln2_g, ln2_b, alpha):
    swgu = jnp.concatenate([sw_gate, sw_up], axis=1).astype(BF16)
    swd = sw_down.astype(BF16)
    out = None
    tok0 = 0
    for eighths in MOE_TOKEN_SPLIT:
        n_tok = x1.shape[0] * eighths // 8
        n_blocks = (n_tok * TOP_K + N_EXPERTS * (MOE_BLOCK - 1)) // MOE_BLOCK
        eidx_t, rank_t, gate_t, counts = _route(scores_t, router_bias[:, None], tok0, n_tok)
        counts = counts[:, 0].astype(I32)
        padded = (counts + MOE_BLOCK - 1) // MOE_BLOCK * MOE_BLOCK
        pend = jnp.cumsum(padded).astype(I32)
        pstart = pend - padded
        block_starts = jnp.arange(n_blocks, dtype=I32) * MOE_BLOCK
        block_e = jnp.minimum(jnp.sum((pend[None, :] <= block_starts[:, None]).astype(I32), axis=1),
                              N_EXPERTS - 1)
        used = counts > 0
        used_e = jnp.argsort(jnp.logical_not(used), stable=True).astype(I32)
        counts2 = jnp.stack([pend[-1] // MOE_BLOCK, jnp.sum(used.astype(I32))]).astype(I32)
        dest_t = _dest_rows(eidx_t, rank_t, pstart)
        xs = _sc_scatter_rows(x1p, tok0, dest_t.reshape(TOP_K, n_tok // SC_CHUNK, SC_CHUNK), n_blocks * MOE_BLOCK)
        y_rows = _experts(block_e, used_e, counts2, xs, w_gate, w_up, w_down)
        routed = _sc_gather_weighted_sum(y_rows, dest_t, gate_t)
        out = _combine(routed, x1, tok0, swgu, swd, ln2_g[None], ln2_b[None], alpha, out)
        tok0 += n_tok
    return out


def kernel(x, in_w, in_b, gmlp_norm_g, gmlp_norm_b, spatial_w, spatial_b, proj_a_w, proj_b_w, out_w,
           ln1_g, ln1_b, router_w, router_bias, expert_w_gate, expert_w_up, expert_w_down,
           shared_w_gate, shared_w_up, shared_w_down, ln2_g, ln2_b):
    B, S, D = x.shape
    depth = in_w.shape[0]
    alpha = np.float32((2.0 * depth) ** 0.25)
    for l in range(depth):
        packed = [_sc_pack_row_pairs(w[l].reshape(-1, w.shape[-1]))
                  for w in (expert_w_gate, expert_w_up, expert_w_down)]
        x1, x1p, scores_t = _mixer(x, in_w[l], in_b[l], gmlp_norm_g[l], gmlp_norm_b[l], spatial_w[l],
                                 spatial_b[l], proj_a_w[l], proj_b_w[l], out_w[l], ln1_g[l], ln1_b[l],
                                 router_w[l], alpha, [p[:8] for p in packed])
        out = _moe(x1.reshape(B * S, D), x1p.reshape(B * S, D // 2), scores_t,
                   router_bias[l], *packed,
                   shared_w_gate[l], shared_w_up[l], shared_w_down[l], ln2_g[l], ln2_b[l], alpha)
        x = out.reshape(B, S, D)
    return x
```

```python
import functools
import math

import numpy as np
import jax
import jax.numpy as jnp
from jax import lax
from jax.experimental import pallas as pl
from jax.experimental.pallas import tpu as pltpu
from jax.experimental.pallas import tpu_sc as plsc

F32 = jnp.float32
BF16 = jnp.bfloat16
U32 = jnp.uint32
I32 = jnp.int32

D_MODEL = 1024
GMLP_WIDTH = 1024
GMLP_GROUPS = 8
GMLP_CHUNK = 128
HEAD_DIM = 64
DILATED_PATTERNS = ((128, 1), (512, 4), (2048, 16))
HEADS_PER_GROUP = 4
GROUP_WIDTH = HEADS_PER_GROUP * HEAD_DIM
ATTN_WIDTH = GROUP_WIDTH * len(DILATED_PATTERNS)
ATTN_BLOCK = 128
N_EXPERTS = 256
TOP_K = 8
TOP_K_SHIFT = 3
N_EXPERT_GROUPS = 8
TOPK_GROUPS = 4
EXPERT_DIM = 256
ROUTED_SCALE = 2.5
LN_EPS = 1e-5
LANES = 128
MASKED_SCORE = -1e30

PROJ_TILE = 512
ATTN_QBLOCKS = 4
MERGE_TILE = 512
ROUTE_TILE = 512
DEST_TILE = 2048
MOE_BLOCK = 256
SC_PACK_CHUNK_WORDS = 16384
SC_CHUNK = 64
SC_SUM_CHUNK = 8
MOE_TOKEN_SPLIT = (4, 4)
X_SLOTS = 6
W_SLOTS = 3
COMBINE_TILE = 512
VMEM_LIMIT = 56 * 1024 * 1024


def _layer_norm(y, g, b):
    mu = jnp.mean(y, axis=-1, keepdims=True)
    yc = y - mu
    var = jnp.mean(yc * yc, axis=-1, keepdims=True)
    return yc * lax.rsqrt(var + LN_EPS) * g + b


def _gelu(x):
    return 0.5 * x * (1.0 + lax.erf(x * np.float32(math.sqrt(0.5))))


def _pack_bf16_pairs(x):
    w = x.shape[1] // 2
    bits = pltpu.bitcast(x.astype(BF16).astype(F32), U32)
    return pltpu.bitcast((bits[:, :w] >> 16) | (bits[:, w:] & jnp.uint32(0xFFFF0000)), I32)


def _unpack_bf16_pairs(words):
    w = pltpu.bitcast(words, U32)
    lo = pltpu.bitcast(w << 16, F32)
    hi = pltpu.bitcast(w & jnp.uint32(0xFFFF0000), F32)
    return lo, hi


def _const_spec(shape):
    nd = len(shape)
    return pl.BlockSpec(shape, lambda *_: (0,) * nd)


def _proj_kernel(x_ref, wuv_ref, buv_ref, watt_ref, batt_ref, wg_ref, bg_ref, ng_ref, nb_ref,
                 sw_ref, sb_ref, pa_ref, ya_ref, gb_ref, a1_ref, a4_ref, a16_ref, xc_ref):
    tm = x_ref.shape[1]
    gw = GMLP_WIDTH
    xb = x_ref[0].astype(BF16)

    def proj(w, b):
        return jnp.dot(xb, w, preferred_element_type=F32) + b

    h_v = proj(wuv_ref[:, gw:], buv_ref[:, gw:])
    h_u = proj(wuv_ref[:, :gw], buv_ref[:, :gw])
    v = _layer_norm(_gelu(h_v), ng_ref[...], nb_ref[...]).astype(BF16)
    h_ga = proj(wg_ref[:, :D_MODEL], bg_ref[:, :D_MODEL])
    u = _gelu(h_u)

    cw = gw // GMLP_GROUPS
    row = lax.broadcasted_iota(I32, (GMLP_CHUNK, GMLP_CHUNK), 0)
    col = lax.broadcasted_iota(I32, (GMLP_CHUNK, GMLP_CHUNK), 1)
    ws = [jnp.where(row >= col, sw_ref[g], 0.0).astype(BF16) for g in range(GMLP_GROUPS)]
    chunks = []
    for c in range(tm // GMLP_CHUNK):
        vc = v[c * GMLP_CHUNK:(c + 1) * GMLP_CHUNK]
        cols = [jnp.dot(ws[g], vc[:, g * cw:(g + 1) * cw], preferred_element_type=F32)
                for g in range(GMLP_GROUPS)]
        chunks.append(jnp.concatenate(cols, axis=1) + sb_ref[...])
    vmix = jnp.concatenate(chunks, axis=0)
    h_gb = proj(wg_ref[:, D_MODEL:], bg_ref[:, D_MODEL:])
    ga = jax.nn.sigmoid(h_ga)
    ya = jnp.dot((u * vmix).astype(BF16), pa_ref[...], preferred_element_type=F32)
    gb_ref[0] = jax.nn.sigmoid(h_gb).astype(BF16)

    n_chunks = x_ref.shape[2] // LANES
    for c in range(n_chunks):
        xc_ref[c] = x_ref[0, :, c * LANES:(c + 1) * LANES]

    def attn_proj(p, d):
        n = tm // d
        if d == 1:
            xp = xb
        else:
            xp = jnp.concatenate(
                [jnp.concatenate([xc_ref[c, pl.ds(r, n, stride=d), :] for c in range(n_chunks)], axis=1)
                 for r in range(d)], axis=0).astype(BF16)
        return jnp.dot(xp, watt_ref[p], preferred_element_type=F32)

    def attn_store(p, d, h, a_ref):
        n = tm // d
        h = (h + batt_ref[p]).astype(BF16)
        for r in range(d):
            a_ref[0, r] = h[r * n:(r + 1) * n]

    a_refs = (a1_ref, a4_ref, a16_ref)
    dils = [d for _, d in DILATED_PATTERNS]
    h_prev = attn_proj(0, dils[0])
    ya_ref[0] = (ga * ya).astype(BF16)
    for p in range(1, len(dils)):
        h_next = attn_proj(p, dils[p])
        attn_store(p - 1, dils[p - 1], h_prev, a_refs[p - 1])
        h_prev = h_next
    attn_store(len(dils) - 1, dils[-1], h_prev, a_refs[-1])


def _input_projection(x, wuv, buv, watt, batt, wg, bg, ng, nb, sw, sb, pa):
    B, S, D = x.shape
    tm = PROJ_TILE
    grid = (B, S // tm)
    out_shape = [jax.ShapeDtypeStruct((B, S, D), BF16), jax.ShapeDtypeStruct((B, S, D), BF16)]
    out_specs = [pl.BlockSpec((1, tm, D), lambda b, t: (b, t, 0)),
                 pl.BlockSpec((1, tm, D), lambda b, t: (b, t, 0))]
    for _, d in DILATED_PATTERNS:
        out_shape.append(jax.ShapeDtypeStruct((B, d, S // d, ATTN_WIDTH), BF16))
        out_specs.append(pl.BlockSpec((1, d, tm // d, ATTN_WIDTH), lambda b, t: (b, 0, t, 0)))
    consts = (wuv, buv, watt, batt, wg, bg, ng, nb, sw, sb, pa)
    return pl.pallas_call(
        _proj_kernel,
        grid=grid,
        in_specs=[pl.BlockSpec((1, tm, D), lambda b, t: (b, t, 0))] + [_const_spec(c.shape) for c in consts],
        out_specs=out_specs,
        out_shape=out_shape,
        scratch_shapes=[pltpu.VMEM((D // LANES, tm, LANES), F32)],
        compiler_params=pltpu.CompilerParams(
            dimension_semantics=("arbitrary", "arbitrary"), vmem_limit_bytes=VMEM_LIMIT),
        name="input_projection",
    )(x, *consts)


def _attn_kernel(qkv_ref, bias_ref, o_ref, lse_ref, *, qblocks, rblock):
    nq = pl.program_id(2)
    gwid = GROUP_WIDTH
    blk = ATTN_BLOCK
    lane = lax.broadcasted_iota(I32, (1, gwid), 1)
    head_masks = [(lane >= h * HEAD_DIM) & (lane < (h + 1) * HEAD_DIM) for h in range(HEADS_PER_GROUP)]
    q_scales = [jnp.where(m, np.float32(HEAD_DIM ** -0.5), 0.0).astype(BF16) for m in head_masks]

    def rows_of(j):
        n = nq * qblocks + j
        return n, pl.multiple_of(n * blk, blk), pl.multiple_of(jnp.maximum(n - 1, 0) * blk, blk)

    def scores(unit):
        ri, j = unit
        n, q0, p0 = rows_of(j)
        q = qkv_ref[0, ri, pl.ds(q0, blk), 0:gwid]
        kk = jnp.concatenate([qkv_ref[0, ri, pl.ds(p0, blk), gwid:2 * gwid],
                              qkv_ref[0, ri, pl.ds(q0, blk), gwid:2 * gwid]], axis=0)
        qs = jnp.concatenate([q * s for s in q_scales], axis=0)
        s = lax.dot_general(qs, kk, (((1,), (1,)), ((), ())), preferred_element_type=F32)
        return s + bias_ref[jnp.where(n == 0, 1, 0)]

    def finish(unit, s):
        ri, j = unit
        _, q0, p0 = rows_of(j)
        vv = jnp.concatenate([qkv_ref[0, ri, pl.ds(p0, blk), 2 * gwid:3 * gwid],
                              qkv_ref[0, ri, pl.ds(q0, blk), 2 * gwid:3 * gwid]], axis=0)
        m = jnp.max(s, axis=1, keepdims=True)
        p = jnp.exp(s - m)
        den = jnp.sum(p, axis=1, keepdims=True)
        pv = jnp.dot(p.astype(BF16), vv, preferred_element_type=F32)
        on = pv / den
        lse = m + jnp.log(den)
        o = jnp.zeros((blk, gwid), F32)
        l = jnp.zeros((blk, gwid), F32)
        for h in range(HEADS_PER_GROUP):
            o = jnp.where(head_masks[h], on[h * blk:(h + 1) * blk], o)
            l = jnp.where(head_masks[h], lse[h * blk:(h + 1) * blk], l)
        o_ref[0, ri, j * blk:(j + 1) * blk, :] = o.astype(BF16)
        lse_ref[0, ri, j * blk:(j + 1) * blk, :] = l

    units = [(ri, j) for ri in range(rblock) for j in range(qblocks)]
    s_next = scores(units[0])
    for i, unit in enumerate(units):
        s_cur = s_next
        if i + 1 < len(units):
            s_next = scores(units[i + 1])
        finish(unit, s_cur)


def _alibi_slopes(n):
    def pow2_slopes(m):
        start = 2.0 ** (-8.0 / m)
        return [start ** (i + 1) for i in range(m)]
    p = 2 ** int(math.floor(math.log2(n)))
    s = pow2_slopes(p)
    if p < n:
        s = s + pow2_slopes(2 * p)[0::2][: n - p]
    return np.array(sorted(s, reverse=True), dtype=np.float32)


def _attn_bias_tables(group, dilation):
    blk = ATTN_BLOCK
    slopes = _alibi_slopes(HEADS_PER_GROUP * len(DILATED_PATTERNS))
    slopes = slopes[group * HEADS_PER_GROUP:(group + 1) * HEADS_PER_GROUP]
    qi = np.arange(blk)[:, None]
    ki = np.arange(2 * blk)[None, :]
    delta = blk + qi - ki
    band = (delta >= 0) & (delta <= blk)
    bias = -slopes[:, None, None] * (delta * dilation).astype(np.float32)[None]
    full = np.where(band[None], bias, np.float32(MASKED_SCORE)).astype(np.float32)
    first = np.where((ki >= blk)[None], full, np.float32(MASKED_SCORE)).astype(np.float32)
    return full.reshape(HEADS_PER_GROUP * blk, 2 * blk), first.reshape(HEADS_PER_GROUP * blk, 2 * blk)


def _dilated_attention(qkv, group, dilation):
    B, d, sd, _ = qkv.shape
    qblocks = min(ATTN_QBLOCKS, sd // ATTN_BLOCK)
    rblock = min(d, ATTN_QBLOCKS // qblocks)
    rows = qblocks * ATTN_BLOCK
    bias = np.stack(_attn_bias_tables(group, dilation))
    grid = (B, d // rblock, sd // rows)
    out_spec = pl.BlockSpec((1, rblock, rows, GROUP_WIDTH), lambda b, r, n: (b, r, n, 0))
    return pl.pallas_call(
        functools.partial(_attn_kernel, qblocks=qblocks, rblock=rblock),
        grid=grid,
        in_specs=[pl.BlockSpec((1, rblock, sd, ATTN_WIDTH), lambda b, r, n: (b, r, 0, 0)),
                  _const_spec(bias.shape)],
        out_specs=[out_spec, out_spec],
        out_shape=[jax.ShapeDtypeStruct((B, d, sd, GROUP_WIDTH), BF16),
                   jax.ShapeDtypeStruct((B, d, sd, GROUP_WIDTH), F32)],
        compiler_params=pltpu.CompilerParams(
            dimension_semantics=("arbitrary", "arbitrary", "arbitrary"), vmem_limit_bytes=VMEM_LIMIT),
        name=f"dilated_attention_d{dilation}",
    )(qkv, jnp.asarray(bias))


def _merge_kernel(o1_ref, l1_ref, o4_ref, l4_ref, o16_ref, l16_ref, ya_ref, gb_ref, x_ref,
                  pb_ref, ow_ref, g1_ref, b1_ref, rwh_ref, rwl_ref, *rest, alpha, n_anchors):
    x1_ref, x1p_ref, sc_ref, so4, sl4, so16, sl16 = rest[n_anchors:]
    tm = x_ref.shape[1]
    n_chunks = GROUP_WIDTH // LANES
    for (o_ref, l_ref, so, sl, d) in ((o4_ref, l4_ref, so4, sl4, 4), (o16_ref, l16_ref, so16, sl16, 16)):
        n = tm // d
        for r in range(d):
            o_r = o_ref[0, r].astype(F32)
            l_r = l_ref[0, r]
            for c in range(n_chunks):
                so[c, pl.ds(r, n, stride=d), :] = o_r[:, c * LANES:(c + 1) * LANES]
                sl[c, pl.ds(r, n, stride=d), :] = l_r[:, c * LANES:(c + 1) * LANES]

    def natural(s):
        return jnp.concatenate([s[c] for c in range(n_chunks)], axis=1)

    l1 = l1_ref[0, 0]
    l4 = natural(sl4)
    l16 = natural(sl16)
    lmax = jnp.maximum(jnp.maximum(l1, l4), l16)
    e1 = jnp.exp(l1 - lmax)
    e4 = jnp.exp(l4 - lmax)
    e16 = jnp.exp(l16 - lmax)
    yb = (e1 * o1_ref[0, 0].astype(F32) + e4 * natural(so4) + e16 * natural(so16)) / (e1 + e4 + e16)
    ybp = jnp.dot(yb.astype(BF16), pb_ref[...], preferred_element_type=F32)
    merged = ya_ref[0] + gb_ref[0] * ybp.astype(BF16)
    mix = jnp.dot(merged, ow_ref[...], preferred_element_type=F32)
    x1 = _layer_norm(alpha * x_ref[0] + mix, g1_ref[...], b1_ref[...])
    x1_ref[0] = x1
    x1p_ref[0] = _pack_bf16_pairs(x1)
    hi = x1.astype(BF16)
    lo = (x1 - hi.astype(F32)).astype(BF16)
    def logits_t(w_ref, xt):
        return lax.dot_general(w_ref[...], xt, (((1,), (1,)), ((), ())), preferred_element_type=F32)
    sc_ref[...] = jax.nn.sigmoid(logits_t(rwh_ref, hi) + logits_t(rwh_ref, lo) + logits_t(rwl_ref, hi))


def _merge_and_norm(attn_outs, ya, gb, x, pb, ow, g1, b1, rwh, rwl, alpha, anchors):
    B, S, D = x.shape
    tm = MERGE_TILE
    in_specs = []
    args = []
    for (o, l), (_, d) in zip(attn_outs, DILATED_PATTERNS):
        spec = pl.BlockSpec((1, d, tm // d, GROUP_WIDTH), lambda b, t: (b, 0, t, 0))
        in_specs += [spec, spec]
        args += [o, l]
    tok_spec = pl.BlockSpec((1, tm, D), lambda b, t: (b, t, 0))
    in_specs += [tok_spec, tok_spec, tok_spec]
    args += [ya, gb, x]
    consts = (pb, ow, g1, b1, rwh, rwl)
    in_specs += [_const_spec(c.shape) for c in consts]
    in_specs += [pl.BlockSpec(memory_space=pl.ANY) for _ in anchors]
    return pl.pallas_call(
        functools.partial(_merge_kernel, alpha=alpha, n_anchors=len(anchors)),
        grid=(B, S // tm),
        in_specs=in_specs,
        out_specs=[tok_spec,
                   pl.BlockSpec((1, tm, D // 2), lambda b, t: (b, t, 0)),
                   pl.BlockSpec((N_EXPERTS, tm), lambda b, t: (0, b * (S // tm) + t))],
        out_shape=[jax.ShapeDtypeStruct((B, S, D), F32),
                   jax.ShapeDtypeStruct((B, S, D // 2), I32),
                   jax.ShapeDtypeStruct((N_EXPERTS, B * S), F32)],
        scratch_shapes=[pltpu.VMEM((GROUP_WIDTH // LANES, tm, LANES), F32) for _ in range(4)],
        compiler_params=pltpu.CompilerParams(
            dimension_semantics=("arbitrary", "arbitrary"), vmem_limit_bytes=VMEM_LIMIT),
        name="merge_norm_router",
    )(*args, *consts, *anchors)


def _sortable_key(x):
    bits = pltpu.bitcast(x, I32)
    return jnp.where(bits < 0, bits ^ jnp.int32(0x7FFFFFFF), bits)


def _route_kernel(sc_ref, bias_ref, before_ref, eidx_ref, rank_ref, gate_ref, cnt_ref, carry_ref):
    ne, tm = sc_ref.shape
    gsize = ne // N_EXPERT_GROUPS
    neg_inf = np.float32(-np.inf)
    removed = jnp.int32(-2 ** 31)

    @pl.when(pl.program_id(0) == 0)
    def _():
        carry_ref[...] = jnp.zeros_like(carry_ref)

    scores = sc_ref[...]
    biased = scores + bias_ref[...]

    gsum = []
    for g in range(N_EXPERT_GROUPS):
        v = biased[g * gsize:(g + 1) * gsize]
        m1 = jnp.max(v, axis=0, keepdims=True)
        n1 = jnp.sum(jnp.where(v == m1, 1.0, 0.0), axis=0, keepdims=True)
        m2 = jnp.max(jnp.where(v < m1, v, neg_inf), axis=0, keepdims=True)
        gsum.append(m1 + jnp.where(n1 >= 2.0, m1, m2))
    gkey = _sortable_key(jnp.concatenate(gsum, axis=0))

    def pick_first_max(keys, ids, n_ids):
        m = jnp.max(keys, axis=0, keepdims=True)
        idx = jnp.min(jnp.where(keys == m, ids, n_ids), axis=0, keepdims=True)
        hit = ids == idx
        return idx, hit, jnp.where(hit, removed, keys)

    gid = lax.broadcasted_iota(I32, (N_EXPERT_GROUPS, tm), 0)
    for _ in range(TOPK_GROUPS):
        _, _, gkey = pick_first_max(gkey, gid, N_EXPERT_GROUPS)
    group_on = gkey == removed

    masked = jnp.concatenate(
        [jnp.where(group_on[g:g + 1], biased[g * gsize:(g + 1) * gsize], neg_inf)
         for g in range(N_EXPERT_GROUPS)], axis=0)
    keys = _sortable_key(masked)
    eid = lax.broadcasted_iota(I32, (ne, tm), 0)
    picks = []
    for _ in range(TOP_K):
        idx, _, keys = pick_first_max(keys, eid, ne)
        picks.append(idx)

    sel = jnp.where(keys == removed, 1.0, 0.0)
    ranks = jnp.dot(sel.astype(BF16), before_ref[...], preferred_element_type=F32) + carry_ref[...]
    carry_ref[...] = carry_ref[...] + jnp.sum(sel, axis=1, keepdims=True)
    cnt_ref[...] = carry_ref[...]

    s_k, r_k = [], []
    for idx in picks:
        hit = eid == idx
        s_k.append(jnp.sum(jnp.where(hit, scores, 0.0), axis=0, keepdims=True))
        r_k.append(jnp.sum(jnp.where(hit, ranks, 0.0), axis=0, keepdims=True))
    total = s_k[0]
    for s in s_k[1:]:
        total = total + s
    eidx_ref[...] = jnp.concatenate(picks, axis=0)
    rank_ref[...] = jnp.concatenate(r_k, axis=0).astype(I32)
    gate_ref[...] = jnp.concatenate([s / total * np.float32(ROUTED_SCALE) for s in s_k], axis=0)


def _route(scores_t, bias, tok0, T):
    ne = scores_t.shape[0]
    tm = ROUTE_TILE
    first = tok0 // tm
    before = jnp.asarray(np.triu(np.ones((tm, tm), np.float32), k=1), BF16)
    out_spec = pl.BlockSpec((TOP_K, tm), lambda i: (0, i))
    return pl.pallas_call(
        _route_kernel,
        grid=(T // tm,),
        in_specs=[pl.BlockSpec((ne, tm), lambda i: (0, first + i)), _const_spec(bias.shape),
                  _const_spec(before.shape)],
        out_specs=[out_spec, out_spec, out_spec, _const_spec((ne, 1))],
        out_shape=[jax.ShapeDtypeStruct((TOP_K, T), I32), jax.ShapeDtypeStruct((TOP_K, T), I32),
                   jax.ShapeDtypeStruct((TOP_K, T), F32), jax.ShapeDtypeStruct((ne, 1), F32)],
        scratch_shapes=[pltpu.VMEM((ne, 1), F32)],
        compiler_params=pltpu.CompilerParams(dimension_semantics=("arbitrary",), vmem_limit_bytes=VMEM_LIMIT),
        name="route_topk",
    )(scores_t, bias, before)


def _dest_kernel(pstart_ref, eidx_ref, rank_ref, out_ref):
    eidx = eidx_ref[...]
    start = jnp.zeros(eidx.shape, I32)
    for e in range(N_EXPERTS):
        start = jnp.where(eidx == e, pstart_ref[e], start)
    out_ref[...] = start + rank_ref[...]


def _dest_rows(eidx_t, rank_t, pstart):
    T = eidx_t.shape[1]
    tm = DEST_TILE
    tok_spec = pl.BlockSpec((TOP_K, tm), lambda i: (0, i))
    return pl.pallas_call(
        _dest_kernel,
        grid=(T // tm,),
        in_specs=[pl.BlockSpec(memory_space=pltpu.SMEM), tok_spec, tok_spec],
        out_specs=tok_spec,
        out_shape=jax.ShapeDtypeStruct((TOP_K, T), I32),
        compiler_params=pltpu.CompilerParams(dimension_semantics=("arbitrary",)),
        name="moe_dest_rows",
    )(pstart, eidx_t, rank_t)


def _sc_workers():
    info = plsc.get_sparse_core_info()
    return info.num_cores, info.num_cores * info.num_subcores


def _sc_scatter_rows(rows, row0, dest, n_out):
    n_dst, n_chunks, ch = dest.shape
    width = rows.shape[1]
    n_cores, n_workers = _sc_workers()
    per_w = n_chunks // n_workers
    assert n_chunks % n_workers == 0 and per_w % 8 == 0 and row0 + n_chunks * ch <= rows.shape[0]

    def body(rows_hbm, dest_hbm, out_hbm, idx_v, buf, lsem, ssem):
        wid = lax.axis_index("s") * n_cores + lax.axis_index("c")
        c0 = wid * per_w
        for k in range(n_dst):
            pltpu.sync_copy(dest_hbm.at[k, pl.ds(pl.multiple_of(c0, 8), per_w)], idx_v.at[k])

        def load(c, b):
            return pltpu.make_async_copy(rows_hbm.at[pl.ds(row0 + (c0 + c) * ch, ch)], buf.at[b], lsem.at[b])

        def scatter(c, b, k):
            return pltpu.make_async_copy(buf.at[b], out_hbm.at[idx_v.at[k, c]], ssem.at[b])

        load(0, 0).start()

        @pl.loop(0, per_w, step=2)
        def _(c):
            for b in range(2):
                cc = c + b
                load(cc, b).wait()

                @pl.when(cc >= 1)
                def _():
                    for k in range(n_dst):
                        scatter(cc - 1, 1 - b, k).wait()

                @pl.when(cc + 1 < per_w)
                def _():
                    load(cc + 1, 1 - b).start()

                for k in range(n_dst):
                    scatter(cc, b, k).start()

        for k in range(n_dst):
            scatter(per_w - 1, 1, k).wait()

    mesh = plsc.VectorSubcoreMesh(core_axis_name="c", subcore_axis_name="s")
    return pl.kernel(
        body, out_type=jax.ShapeDtypeStruct((n_out, width), rows.dtype), mesh=mesh,
        scratch_types=[pltpu.VMEM((n_dst, per_w, ch), I32), pltpu.VMEM((2, ch, width), rows.dtype),
                       pltpu.SemaphoreType.DMA((2,)), pltpu.SemaphoreType.DMA((2,))],
        name="moe_dispatch_scatter",
    )(rows, dest)


def _sc_gather_weighted_sum(table, idx, gates):
    n_src, n_tok = idx.shape
    ct = SC_SUM_CHUNK
    n_chunks = n_tok // ct
    half = table.shape[1]
    lanes = plsc.get_sparse_core_info().num_lanes
    n_cores, n_workers = _sc_workers()
    per_w = n_chunks // n_workers
    assert n_chunks % n_workers == 0 and per_w % 2 == 0 and half % lanes == 0

    def body(table_hbm, idx_hbm, gate_hbm, out_hbm, idx_v, gate_v, buf, out_v, gsem, wsem):
        wid = lax.axis_index("s") * n_cores + lax.axis_index("c")
        c0 = wid * per_w
        pltpu.sync_copy(idx_hbm.at[:, pl.ds(c0 * ct, per_w * ct)], idx_v)
        pltpu.sync_copy(gate_hbm.at[:, pl.ds(c0 * ct, per_w * ct)], gate_v)
        shift = jnp.full((lanes,), 16, I32)
        hi_mask = jnp.full((lanes,), -65536, I32)

        def gather(c, b, k):
            return pltpu.make_async_copy(table_hbm.at[idx_v.at[k, pl.ds(c * ct, ct)]], buf.at[b, k], gsem.at[b])

        def write(c, b):
            return pltpu.make_async_copy(out_v.at[b], out_hbm.at[pl.ds((c0 + c) * ct, ct)], wsem.at[b])

        for k in range(n_src):
            gather(0, 0, k).start()

        @pl.loop(0, per_w, step=2)
        def _(c):
            for b in range(2):
                cc = c + b
                for k in range(n_src):
                    gather(cc, b, k).wait()

                @pl.when(cc + 1 < per_w)
                def _():
                    for k in range(n_src):
                        gather(cc + 1, 1 - b, k).start()

                @pl.when(cc >= 2)
                def _():
                    write(cc - 2, b).wait()

                @pl.loop(0, ct)
                def _(t):
                    tok = jnp.full((lanes,), cc * ct + t, I32)
                    g = [plsc.load_gather(gate_v, [jnp.full((lanes,), k, I32), tok]) for k in range(n_src)]

                    @plsc.parallel_loop(0, half, step=lanes, unroll=4)
                    def _(col):
                        lo = jnp.zeros((lanes,), F32)
                        hi = jnp.zeros((lanes,), F32)
                        for k in range(n_src):
                            w = buf[b, k, t, pl.ds(col, lanes)]
                            lo = lo + g[k] * plsc.bitcast(lax.shift_left(w, shift), F32)
                            hi = hi + g[k] * plsc.bitcast(w & hi_mask, F32)
                        out_v[b, t, pl.ds(col, lanes)] = lo
                        out_v[b, t, pl.ds(half + col, lanes)] = hi

                write(cc, b).start()

        write(per_w - 2, 0).wait()
        write(per_w - 1, 1).wait()

    mesh = plsc.VectorSubcoreMesh(core_axis_name="c", subcore_axis_name="s")
    return pl.kernel(
        body, out_type=jax.ShapeDtypeStruct((n_chunks * ct, 2 * half), F32), mesh=mesh,
        scratch_types=[pltpu.VMEM((n_src, per_w * ct), I32), pltpu.VMEM((n_src, per_w * ct), F32),
                       pltpu.VMEM((2, n_src, ct, half), I32), pltpu.VMEM((2, ct, 2 * half), F32),
                       pltpu.SemaphoreType.DMA((2,)), pltpu.SemaphoreType.DMA((2,))],
        compiler_params=pltpu.CompilerParams(needs_layout_passes=False),
        name="moe_combine_gather_sum",
    )(table, idx, gates)


def _sc_pack_row_pairs(w):
    R, C = w.shape
    lanes = plsc.get_sparse_core_info().num_lanes
    cr = SC_PACK_CHUNK_WORDS // C
    n_chunks = R // cr
    n_cores, n_workers = _sc_workers()
    per_w = n_chunks // n_workers
    assert R % cr == 0 and n_chunks % n_workers == 0 and per_w % 2 == 0 and cr % 2 == 0 and C % lanes == 0

    def body(w_hbm, out_hbm, in_v, out_v, lsem, ssem):
        wid = lax.axis_index("s") * n_cores + lax.axis_index("c")
        c0 = wid * per_w

        def load(c, b):
            return pltpu.make_async_copy(w_hbm.at[pl.ds((c0 + c) * cr, cr)], in_v.at[b], lsem.at[b])

        def store(c, b):
            return pltpu.make_async_copy(out_v.at[b], out_hbm.at[pl.ds((c0 + c) * (cr // 2), cr // 2)],
                                         ssem.at[b])

        load(0, 0).start()

        @pl.loop(0, per_w, step=2)
        def _(c):
            for b in range(2):
                cc = c + b
                load(cc, b).wait()

                @pl.when(cc + 1 < per_w)
                def _():
                    load(cc + 1, 1 - b).start()

                @pl.when(cc >= 2)
                def _():
                    store(cc - 2, b).wait()

                @pl.loop(0, cr // 2)
                def _(i):
                    @plsc.parallel_loop(0, C, step=lanes, unroll=4)
                    def _(col):
                        even = in_v[b, 2 * i, pl.ds(col, lanes)]
                        odd = in_v[b, 2 * i + 1, pl.ds(col, lanes)]
                        pair = plsc.pack(even, odd, format=plsc.PackFormat.INTERLEAVED)
                        out_v[b, i, pl.ds(col, lanes)] = plsc.bitcast(pair, I32)

                store(cc, b).start()

        store(per_w - 2, 0).wait()
        store(per_w - 1, 1).wait()

    mesh = plsc.VectorSubcoreMesh(core_axis_name="c", subcore_axis_name="s")
    return pl.kernel(
        body, out_type=jax.ShapeDtypeStruct((R // 2, C), I32), mesh=mesh,
        scratch_types=[pltpu.VMEM((2, cr, C), F32), pltpu.VMEM((2, cr // 2, C), I32),
                       pltpu.SemaphoreType.DMA((2,)), pltpu.SemaphoreType.DMA((2,))],
        compiler_params=pltpu.CompilerParams(needs_layout_passes=False),
        name="expert_weights_bf16",
    )(w)


def _expert_kernel(be_ref, ue_ref, nu_ref, xs_hbm, wg_hbm, wu_hbm, wd_hbm, y_hbm,
                   xbuf, ybuf, wg_v, wu_v, wd_v, xsem, ysem, wsem):
    n_used = nu_ref[0]
    n_exp = nu_ref[1]
    half = D_MODEL // 2
    blk = MOE_BLOCK
    rows_gu = D_MODEL // 2
    rows_d = EXPERT_DIM // 2

    def rows(j):
        return pl.ds(pl.multiple_of(j * blk, blk), blk)

    def x_copy(j, p):
        return pltpu.make_async_copy(xs_hbm.at[rows(j)], xbuf.at[p], xsem.at[p])

    def y_copy(j, p):
        return pltpu.make_async_copy(ybuf.at[p], y_hbm.at[rows(j)], ysem.at[p])

    def w_copies(q, s):
        e = ue_ref[q]
        gu = pl.ds(pl.multiple_of(e * rows_gu, rows_gu), rows_gu)
        dn = pl.ds(pl.multiple_of(e * rows_d, rows_d), rows_d)
        return (pltpu.make_async_copy(wg_hbm.at[gu], wg_v.at[s], wsem.at[s, 0]),
                pltpu.make_async_copy(wu_hbm.at[gu], wu_v.at[s], wsem.at[s, 1]),
                pltpu.make_async_copy(wd_hbm.at[dn], wd_v.at[s], wsem.at[s, 2]))

    for q0 in range(W_SLOTS - 1):
        @pl.when(q0 < n_exp)
        def _():
            for c in w_copies(q0, q0):
                c.start()

    for j0 in range(X_SLOTS):
        @pl.when(j0 < n_used)
        def _():
            x_copy(j0, j0).start()

    def block_step(j, p, q):
        is_new = (j == 0) | (be_ref[j] != be_ref[jnp.maximum(j - 1, 0)])
        q = q + is_new.astype(I32)

        s = q % W_SLOTS

        @pl.when(is_new)
        def _():
            for c in w_copies(q, s):
                c.wait()

            @pl.when(q + W_SLOTS - 1 < n_exp)
            def _():
                for c in w_copies(q + W_SLOTS - 1, (q + W_SLOTS - 1) % W_SLOTS):
                    c.start()

        x_copy(j, p).wait()

        @pl.when(j >= X_SLOTS)
        def _():
            y_copy(j - X_SLOTS, p).wait()

        wg = pltpu.bitcast(wg_v[s], BF16)
        wu = pltpu.bitcast(wu_v[s], BF16)
        wd = pltpu.bitcast(wd_v[s], BF16)
        lo, hi = _unpack_bf16_pairs(xbuf[p])
        xlo = lo.astype(BF16)
        xhi = hi.astype(BF16)
        g = (jnp.dot(xlo, wg[:half], preferred_element_type=F32)
             + jnp.dot(xhi, wg[half:], preferred_element_type=F32))
        u = (jnp.dot(xlo, wu[:half], preferred_element_type=F32)
             + jnp.dot(xhi, wu[half:], preferred_element_type=F32))
        hb = (g * jax.nn.sigmoid(g) * u).astype(BF16)
        ybuf[p] = _pack_bf16_pairs(jnp.dot(hb, wd, preferred_element_type=F32))
        y_copy(j, p).start()

        @pl.when(j + X_SLOTS < n_used)
        def _():
            x_copy(j + X_SLOTS, p).start()

        return q

    def group(m, q):
        q = block_step(X_SLOTS * m, 0, q)
        for p in range(1, X_SLOTS):
            j = X_SLOTS * m + p
            q = lax.cond(j < n_used, functools.partial(block_step, j, p), lambda q: q, q)
        return q

    lax.fori_loop(0, (n_used + X_SLOTS - 1) // X_SLOTS, group, jnp.int32(-1))

    for back in range(X_SLOTS, 0, -1):
        b = n_used - back
        for p in range(X_SLOTS):
            @pl.when((b >= 0) & (b % X_SLOTS == p))
            def _():
                y_copy(b, p).wait()


def _experts(block_e, used_e, counts2, xs, wg, wu, wd):
    n_rows, half = xs.shape
    smem = pl.BlockSpec(memory_space=pltpu.SMEM)
    hbm = pl.BlockSpec(memory_space=pl.ANY)
    return pl.pallas_call(
        _expert_kernel,
        in_specs=[smem, smem, smem, hbm, hbm, hbm, hbm],
        out_specs=hbm,
        out_shape=jax.ShapeDtypeStruct((n_rows, half), I32),
        scratch_shapes=[pltpu.VMEM((X_SLOTS, MOE_BLOCK, half), I32), pltpu.VMEM((X_SLOTS, MOE_BLOCK, half), I32),
                        pltpu.VMEM((W_SLOTS, D_MODEL // 2, EXPERT_DIM), I32),
                        pltpu.VMEM((W_SLOTS, D_MODEL // 2, EXPERT_DIM), I32),
                        pltpu.VMEM((W_SLOTS, EXPERT_DIM // 2, D_MODEL), I32),
                        pltpu.SemaphoreType.DMA((X_SLOTS,)), pltpu.SemaphoreType.DMA((X_SLOTS,)),
                        pltpu.SemaphoreType.DMA((W_SLOTS, 3))],
        compiler_params=pltpu.CompilerParams(vmem_limit_bytes=VMEM_LIMIT),
        name="moe_experts",
    )(block_e, used_e, counts2, xs, wg, wu, wd)


def _combine_kernel(routed_ref, x1_ref, swgu_ref, swd_ref, g2_ref, b2_ref, *rest, alpha):
    out_ref = rest[-1]
    x1 = x1_ref[...]
    gu = jnp.dot(x1.astype(BF16), swgu_ref[...], preferred_element_type=F32)
    g = gu[:, :EXPERT_DIM]
    u = gu[:, EXPERT_DIM:]
    shared = jnp.dot((g * jax.nn.sigmoid(g) * u).astype(BF16), swd_ref[...], preferred_element_type=F32)
    out_ref[...] = _layer_norm(alpha * x1 + (routed_ref[...] + shared), g2_ref[...], b2_ref[...])


def _combine(routed, x1, tok0, swgu, swd, g2, b2, alpha, out_prev):
    T, D = x1.shape
    n_tok = routed.shape[0]
    tc = COMBINE_TILE
    first = tok0 // tc
    consts = (swgu, swd, g2, b2)
    args = [routed, x1, *consts]
    in_specs = [pl.BlockSpec((tc, D), lambda i: (i, 0)),
                pl.BlockSpec((tc, D), lambda i: (first + i, 0))] + [_const_spec(c.shape) for c in consts]
    aliases = {}
    if out_prev is not None:
        aliases = {len(args): 0}
        args.append(out_prev)
        in_specs.append(pl.BlockSpec(memory_space=pl.ANY))
    return pl.pallas_call(
        functools.partial(_combine_kernel, alpha=alpha),
        grid=(n_tok // tc,),
        in_specs=in_specs,
        out_specs=pl.BlockSpec((tc, D), lambda i: (first + i, 0)),
        out_shape=jax.ShapeDtypeStruct((T, D), F32),
        input_output_aliases=aliases,
        compiler_params=pltpu.CompilerParams(dimension_semantics=("arbitrary",), vmem_limit_bytes=VMEM_LIMIT),
        name="moe_combine_norm",
    )(*args)


def _mixer(x, in_w, in_b, ng, nb, spatial_w, spatial_b, proj_a_w, proj_b_w, out_w, ln1_g, ln1_b,
           router_w, alpha, anchors):
    gw, aw, D = GMLP_WIDTH, ATTN_WIDTH, D_MODEL
    w = in_w.astype(BF16)
    q0 = 2 * gw
    wuv, buv = w[:, :q0], in_b[None, :q0]
    watt = jnp.stack([jnp.concatenate([w[:, q0 + s * aw + p * GROUP_WIDTH:q0 + s * aw + (p + 1) * GROUP_WIDTH]
                                       for s in range(3)], axis=1) for p in range(len(DILATED_PATTERNS))])
    batt = jnp.stack([jnp.concatenate([in_b[q0 + s * aw + p * GROUP_WIDTH:q0 + s * aw + (p + 1) * GROUP_WIDTH]
                                       for s in range(3)])[None] for p in range(len(DILATED_PATTERNS))])
    g0 = q0 + 3 * aw
    wg, bg = w[:, g0:], in_b[None, g0:]
    sb = jnp.repeat(spatial_b.T, gw // GMLP_GROUPS, axis=1)
    ya, gb, a1, a4, a16 = _input_projection(
        x, wuv, buv, watt, batt, wg, bg, ng[None], nb[None], spatial_w, sb, proj_a_w.astype(BF16))
    attn_outs = [_dilated_attention(a, p, d) for p, (a, (_, d)) in enumerate(zip((a1, a4, a16), DILATED_PATTERNS))]
    rw_t = router_w.T
    rwh = rw_t.astype(BF16)
    rwl = (rw_t - rwh.astype(F32)).astype(BF16)
    return _merge_and_norm(attn_outs, ya, gb, x, proj_b_w.astype(BF16), out_w.astype(BF16),
                           ln1_g[None], ln1_b[None], rwh, rwl, alpha, anchors)


def _moe(x1, x1p, scores_t, router_bias, w_gate, w_up, w_down, sw_gate, sw_up, sw_down, ln2_g, ln2_b, alpha):
    swgu = jnp.concatenate([sw_gate, sw_up], axis=1).astype(BF16)
    swd = sw_down.astype(BF16)
    out = None
    tok0 = 0
    for eighths in MOE_TOKEN_SPLIT:
        n_tok = x1.shape[0] * eighths // 8
        n_blocks = (n_tok * TOP_K + N_EXPERTS * (MOE_BLOCK - 1)) // MOE_BLOCK
        eidx_t, rank_t, gate_t, counts = _route(scores_t, router_bias[:, None], tok0, n_tok)
        counts = counts[:, 0].astype(I32)
        padded = (counts + MOE_BLOCK - 1) // MOE_BLOCK * MOE_BLOCK
        pend = jnp.cumsum(padded).astype(I32)
        pstart = pend - padded
        block_starts = jnp.arange(n_blocks, dtype=I32) * MOE_BLOCK
        block_e = jnp.minimum(jnp.sum((pend[None, :] <= block_starts[:, None]).astype(I32), axis=1),
                              N_EXPERTS - 1)
        used = counts > 0
        used_e = jnp.argsort(jnp.logical_not(used), stable=True).astype(I32)
        counts2 = jnp.stack([pend[-1] // MOE_BLOCK, jnp.sum(used.astype(I32))]).astype(I32)
        dest_t = _dest_rows(eidx_t, rank_t, pstart)
        xs = _sc_scatter_rows(x1p, tok0, dest_t.reshape(TOP_K, n_tok // SC_CHUNK, SC_CHUNK), n_blocks * MOE_BLOCK)
        y_rows = _experts(block_e, used_e, counts2, xs, w_gate, w_up, w_down)
        routed = _sc_gather_weighted_sum(y_rows, dest_t, gate_t)
        out = _combine(routed, x1, tok0, swgu, swd, ln2_g[None], ln2_b[None], alpha, out)
        tok0 += n_tok
    return out


def kernel(x, in_w, in_b, gmlp_norm_g, gmlp_norm_b, spatial_w, spatial_b, proj_a_w, proj_b_w, out_w,
           ln1_g, ln1_b, router_w, router_bias, expert_w_gate, expert_w_up, expert_w_down,
           shared_w_gate, shared_w_up, shared_w_down, ln2_g, ln2_b):
    B, S, D = x.shape
    depth = in_w.shape[0]
    alpha = np.float32((2.0 * depth) ** 0.25)
    for l in range(depth):
        packed = [_sc_pack_row_pairs(w[l].reshape(-1, w.shape[-1]))
                  for w in (expert_w_gate, expert_w_up, expert_w_down)]
        x1, x1p, scores_t = _mixer(x, in_w[l], in_b[l], gmlp_norm_g[l], gmlp_norm_b[l], spatial_w[l],
                                 spatial_b[l], proj_a_w[l], proj_b_w[l], out_w[l], ln1_g[l], ln1_b[l],
                                 router_w[l], alpha, [p[:8] for p in packed])
        out = _moe(x1.reshape(B * S, D), x1p.reshape(B * S, D // 2), scores_t,
                   router_bias[l], *packed,
                   shared_w_gate[l], shared_w_up[l], shared_w_down[l], ln2_g[l], ln2_b[l], alpha)
        x = out.reshape(B, S, D)
    return x
```

```python
import functools
import math

import numpy as np
import jax
import jax.numpy as jnp
from jax import lax
from jax.experimental import pallas as pl
from jax.experimental.pallas import tpu as pltpu
from jax.experimental.pallas import tpu_sc as plsc

F32 = jnp.float32
BF16 = jnp.bfloat16
U32 = jnp.uint32
I32 = jnp.int32

D_MODEL = 1024
GMLP_WIDTH = 1024
GMLP_GROUPS = 8
GMLP_CHUNK = 128
HEAD_DIM = 64
DILATED_PATTERNS = ((128, 1), (512, 4), (2048, 16))
HEADS_PER_GROUP = 4
GROUP_WIDTH = HEADS_PER_GROUP * HEAD_DIM
ATTN_WIDTH = GROUP_WIDTH * len(DILATED_PATTERNS)
ATTN_BLOCK = 128
N_EXPERTS = 256
TOP_K = 8
TOP_K_SHIFT = 3
N_EXPERT_GROUPS = 8
TOPK_GROUPS = 4
EXPERT_DIM = 256
ROUTED_SCALE = 2.5
LN_EPS = 1e-5
LANES = 128
MASKED_SCORE = -1e30

PROJ_TILE = 512
ATTN_QBLOCKS = 4
MERGE_TILE = 512
ROUTE_TILE = 512
DEST_TILE = 2048
MOE_BLOCK = 256
SC_PACK_CHUNK_WORDS = 16384
SC_CHUNK = 64
SC_SUM_CHUNK = 8
MOE_TOKEN_SPLIT = (4, 4)
MOE_TAIL_PIECES = 2
X_SLOTS = 4
W_SLOTS = 3
COMBINE_TILE = 512
VMEM_LIMIT = 56 * 1024 * 1024


def _layer_norm(y, g, b):
    mu = jnp.mean(y, axis=-1, keepdims=True)
    yc = y - mu
    var = jnp.mean(yc * yc, axis=-1, keepdims=True)
    return yc * lax.rsqrt(var + LN_EPS) * g + b


def _gelu(x):
    return 0.5 * x * (1.0 + lax.erf(x * np.float32(math.sqrt(0.5))))


def _pack_bf16_pairs(x):
    w = x.shape[1] // 2
    bits = pltpu.bitcast(x.astype(BF16).astype(F32), U32)
    return pltpu.bitcast((bits[:, :w] >> 16) | (bits[:, w:] & jnp.uint32(0xFFFF0000)), I32)


def _unpack_bf16_pairs(words):
    w = pltpu.bitcast(words, U32)
    lo = pltpu.bitcast(w << 16, F32)
    hi = pltpu.bitcast(w & jnp.uint32(0xFFFF0000), F32)
    return lo, hi


def _const_spec(shape):
    nd = len(shape)
    return pl.BlockSpec(shape, lambda *_: (0,) * nd)


def _proj_kernel(x_ref, wuv_ref, buv_ref, watt_ref, batt_ref, wg_ref, bg_ref, ng_ref, nb_ref,
                 sw_ref, sb_ref, pa_ref, ya_ref, gb_ref, a1_ref, a4_ref, a16_ref, xc_ref):
    tm = x_ref.shape[1]
    gw = GMLP_WIDTH
    xb = x_ref[0].astype(BF16)

    def proj(w, b):
        return jnp.dot(xb, w, preferred_element_type=F32) + b

    h_v = proj(wuv_ref[:, gw:], buv_ref[:, gw:])
    h_u = proj(wuv_ref[:, :gw], buv_ref[:, :gw])
    v = _layer_norm(_gelu(h_v), ng_ref[...], nb_ref[...]).astype(BF16)
    h_ga = proj(wg_ref[:, :D_MODEL], bg_ref[:, :D_MODEL])
    u = _gelu(h_u)

    cw = gw // GMLP_GROUPS
    row = lax.broadcasted_iota(I32, (GMLP_CHUNK, GMLP_CHUNK), 0)
    col = lax.broadcasted_iota(I32, (GMLP_CHUNK, GMLP_CHUNK), 1)
    ws = [jnp.where(row >= col, sw_ref[g], 0.0).astype(BF16) for g in range(GMLP_GROUPS)]
    chunks = []
    for c in range(tm // GMLP_CHUNK):
        vc = v[c * GMLP_CHUNK:(c + 1) * GMLP_CHUNK]
        cols = [jnp.dot(ws[g], vc[:, g * cw:(g + 1) * cw], preferred_element_type=F32)
                for g in range(GMLP_GROUPS)]
        chunks.append(jnp.concatenate(cols, axis=1) + sb_ref[...])
    vmix = jnp.concatenate(chunks, axis=0)
    h_gb = proj(wg_ref[:, D_MODEL:], bg_ref[:, D_MODEL:])
    ga = jax.nn.sigmoid(h_ga)
    ya = jnp.dot((u * vmix).astype(BF16), pa_ref[...], preferred_element_type=F32)
    gb_ref[0] = jax.nn.sigmoid(h_gb).astype(BF16)

    n_chunks = x_ref.shape[2] // LANES
    for c in range(n_chunks):
        xc_ref[c] = x_ref[0, :, c * LANES:(c + 1) * LANES]

    def attn_proj(p, d):
        n = tm // d
        if d == 1:
            xp = xb
        else:
            xp = jnp.concatenate(
                [jnp.concatenate([xc_ref[c, pl.ds(r, n, stride=d), :] for c in range(n_chunks)], axis=1)
                 for r in range(d)], axis=0).astype(BF16)
        return jnp.dot(xp, watt_ref[p], preferred_element_type=F32)

    def attn_store(p, d, h, a_ref):
        n = tm // d
        h = (h + batt_ref[p]).astype(BF16)
        for r in range(d):
            a_ref[0, r] = h[r * n:(r + 1) * n]

    a_refs = (a1_ref, a4_ref, a16_ref)
    dils = [d for _, d in DILATED_PATTERNS]
    h_prev = attn_proj(0, dils[0])
    ya_ref[0] = (ga * ya).astype(BF16)
    for p in range(1, len(dils)):
        h_next = attn_proj(p, dils[p])
        attn_store(p - 1, dils[p - 1], h_prev, a_refs[p - 1])
        h_prev = h_next
    attn_store(len(dils) - 1, dils[-1], h_prev, a_refs[-1])


def _input_projection(x, wuv, buv, watt, batt, wg, bg, ng, nb, sw, sb, pa):
    B, S, D = x.shape
    tm = PROJ_TILE
    grid = (B, S // tm)
    out_shape = [jax.ShapeDtypeStruct((B, S, D), BF16), jax.ShapeDtypeStruct((B, S, D), BF16)]
    out_specs = [pl.BlockSpec((1, tm, D), lambda b, t: (b, t, 0)),
                 pl.BlockSpec((1, tm, D), lambda b, t: (b, t, 0))]
    for _, d in DILATED_PATTERNS:
        out_shape.append(jax.ShapeDtypeStruct((B, d, S // d, ATTN_WIDTH), BF16))
        out_specs.append(pl.BlockSpec((1, d, tm // d, ATTN_WIDTH), lambda b, t: (b, 0, t, 0)))
    consts = (wuv, buv, watt, batt, wg, bg, ng, nb, sw, sb, pa)
    return pl.pallas_call(
        _proj_kernel,
        grid=grid,
        in_specs=[pl.BlockSpec((1, tm, D), lambda b, t: (b, t, 0))] + [_const_spec(c.shape) for c in consts],
        out_specs=out_specs,
        out_shape=out_shape,
        scratch_shapes=[pltpu.VMEM((D // LANES, tm, LANES), F32)],
        compiler_params=pltpu.CompilerParams(
            dimension_semantics=("arbitrary", "arbitrary"), vmem_limit_bytes=VMEM_LIMIT),
        name="input_projection",
    )(x, *consts)


def _attn_kernel(qkv_ref, bias_ref, o_ref, lse_ref, *, qblocks, rblock):
    nq = pl.program_id(2)
    gwid = GROUP_WIDTH
    blk = ATTN_BLOCK
    lane = lax.broadcasted_iota(I32, (1, gwid), 1)
    head_masks = [(lane >= h * HEAD_DIM) & (lane < (h + 1) * HEAD_DIM) for h in range(HEADS_PER_GROUP)]
    q_scales = [jnp.where(m, np.float32(HEAD_DIM ** -0.5), 0.0).astype(BF16) for m in head_masks]

    def rows_of(j):
        n = nq * qblocks + j
        return n, pl.multiple_of(n * blk, blk), pl.multiple_of(jnp.maximum(n - 1, 0) * blk, blk)

    def scores(unit):
        ri, j = unit
        n, q0, p0 = rows_of(j)
        q = qkv_ref[0, ri, pl.ds(q0, blk), 0:gwid]
        kk = jnp.concatenate([qkv_ref[0, ri, pl.ds(p0, blk), gwid:2 * gwid],
                              qkv_ref[0, ri, pl.ds(q0, blk), gwid:2 * gwid]], axis=0)
        qs = jnp.concatenate([q * s for s in q_scales], axis=0)
        s = lax.dot_general(qs, kk, (((1,), (1,)), ((), ())), preferred_element_type=F32)
        return s + bias_ref[jnp.where(n == 0, 1, 0)]

    def finish(unit, s):
        ri, j = unit
        _, q0, p0 = rows_of(j)
        vv = jnp.concatenate([qkv_ref[0, ri, pl.ds(p0, blk), 2 * gwid:3 * gwid],
                              qkv_ref[0, ri, pl.ds(q0, blk), 2 * gwid:3 * gwid]], axis=0)
        m = jnp.max(s, axis=1, keepdims=True)
        p = jnp.exp(s - m)
        den = jnp.sum(p, axis=1, keepdims=True)
        pv = jnp.dot(p.astype(BF16), vv, preferred_element_type=F32)
        on = pv / den
        lse = m + jnp.log(den)
        o = jnp.zeros((blk, gwid), F32)
        l = jnp.zeros((blk, gwid), F32)
        for h in range(HEADS_PER_GROUP):
            o = jnp.where(head_masks[h], on[h * blk:(h + 1) * blk], o)
            l = jnp.where(head_masks[h], lse[h * blk:(h + 1) * blk], l)
        o_ref[0, ri, j * blk:(j + 1) * blk, :] = o.astype(BF16)
        lse_ref[0, ri, j * blk:(j + 1) * blk, :] = l

    units = [(ri, j) for ri in range(rblock) for j in range(qblocks)]
    s_next = scores(units[0])
    for i, unit in enumerate(units):
        s_cur = s_next
        if i + 1 < len(units):
            s_next = scores(units[i + 1])
        finish(unit, s_cur)


def _alibi_slopes(n):
    def pow2_slopes(m):
        start = 2.0 ** (-8.0 / m)
        return [start ** (i + 1) for i in range(m)]
    p = 2 ** int(math.floor(math.log2(n)))
    s = pow2_slopes(p)
    if p < n:
        s = s + pow2_slopes(2 * p)[0::2][: n - p]
    return np.array(sorted(s, reverse=True), dtype=np.float32)


def _attn_bias_tables(group, dilation):
    blk = ATTN_BLOCK
    slopes = _alibi_slopes(HEADS_PER_GROUP * len(DILATED_PATTERNS))
    slopes = slopes[group * HEADS_PER_GROUP:(group + 1) * HEADS_PER_GROUP]
    qi = np.arange(blk)[:, None]
    ki = np.arange(2 * blk)[None, :]
    delta = blk + qi - ki
    band = (delta >= 0) & (delta <= blk)
    bias = -slopes[:, None, None] * (delta * dilation).astype(np.float32)[None]
    full = np.where(band[None], bias, np.float32(MASKED_SCORE)).astype(np.float32)
    first = np.where((ki >= blk)[None], full, np.float32(MASKED_SCORE)).astype(np.float32)
    return full.reshape(HEADS_PER_GROUP * blk, 2 * blk), first.reshape(HEADS_PER_GROUP * blk, 2 * blk)


def _dilated_attention(qkv, group, dilation):
    B, d, sd, _ = qkv.shape
    qblocks = min(ATTN_QBLOCKS, sd // ATTN_BLOCK)
    rblock = min(d, ATTN_QBLOCKS // qblocks)
    rows = qblocks * ATTN_BLOCK
    bias = np.stack(_attn_bias_tables(group, dilation))
    grid = (B, d // rblock, sd // rows)
    out_spec = pl.BlockSpec((1, rblock, rows, GROUP_WIDTH), lambda b, r, n: (b, r, n, 0))
    return pl.pallas_call(
        functools.partial(_attn_kernel, qblocks=qblocks, rblock=rblock),
        grid=grid,
        in_specs=[pl.BlockSpec((1, rblock, sd, ATTN_WIDTH), lambda b, r, n: (b, r, 0, 0)),
                  _const_spec(bias.shape)],
        out_specs=[out_spec, out_spec],
        out_shape=[jax.ShapeDtypeStruct((B, d, sd, GROUP_WIDTH), BF16),
                   jax.ShapeDtypeStruct((B, d, sd, GROUP_WIDTH), F32)],
        compiler_params=pltpu.CompilerParams(
            dimension_semantics=("arbitrary", "arbitrary", "arbitrary"), vmem_limit_bytes=VMEM_LIMIT),
        name=f"dilated_attention_d{dilation}",
    )(qkv, jnp.asarray(bias))


def _merge_kernel(o1_ref, l1_ref, o4_ref, l4_ref, o16_ref, l16_ref, ya_ref, gb_ref, x_ref,
                  pb_ref, ow_ref, g1_ref, b1_ref, rwh_ref, rwl_ref, *rest, alpha, n_anchors):
    x1_ref, x1p_ref, sc_ref, so4, sl4, so16, sl16 = rest[n_anchors:]
    tm = x_ref.shape[1]
    n_chunks = GROUP_WIDTH // LANES
    for (o_ref, l_ref, so, sl, d) in ((o4_ref, l4_ref, so4, sl4, 4), (o16_ref, l16_ref, so16, sl16, 16)):
        n = tm // d
        for r in range(d):
            o_r = o_ref[0, r].astype(F32)
            l_r = l_ref[0, r]
            for c in range(n_chunks):
                so[c, pl.ds(r, n, stride=d), :] = o_r[:, c * LANES:(c + 1) * LANES]
                sl[c, pl.ds(r, n, stride=d), :] = l_r[:, c * LANES:(c + 1) * LANES]

    def natural(s):
        return jnp.concatenate([s[c] for c in range(n_chunks)], axis=1)

    l1 = l1_ref[0, 0]
    l4 = natural(sl4)
    l16 = natural(sl16)
    lmax = jnp.maximum(jnp.maximum(l1, l4), l16)
    e1 = jnp.exp(l1 - lmax)
    e4 = jnp.exp(l4 - lmax)
    e16 = jnp.exp(l16 - lmax)
    yb = (e1 * o1_ref[0, 0].astype(F32) + e4 * natural(so4) + e16 * natural(so16)) / (e1 + e4 + e16)
    ybp = jnp.dot(yb.astype(BF16), pb_ref[...], preferred_element_type=F32)
    merged = ya_ref[0] + gb_ref[0] * ybp.astype(BF16)
    mix = jnp.dot(merged, ow_ref[...], preferred_element_type=F32)
    x1 = _layer_norm(alpha * x_ref[0] + mix, g1_ref[...], b1_ref[...])
    x1_ref[0] = x1
    x1p_ref[0] = _pack_bf16_pairs(x1)
    hi = x1.astype(BF16)
    lo = (x1 - hi.astype(F32)).astype(BF16)
    def logits_t(w_ref, xt):
        return lax.dot_general(w_ref[...], xt, (((1,), (1,)), ((), ())), preferred_element_type=F32)
    sc_ref[...] = jax.nn.sigmoid(logits_t(rwh_ref, hi) + logits_t(rwh_ref, lo) + logits_t(rwl_ref, hi))


def _merge_and_norm(attn_outs, ya, gb, x, pb, ow, g1, b1, rwh, rwl, alpha, anchors):
    B, S, D = x.shape
    tm = MERGE_TILE
    in_specs = []
    args = []
    for (o, l), (_, d) in zip(attn_outs, DILATED_PATTERNS):
        spec = pl.BlockSpec((1, d, tm // d, GROUP_WIDTH), lambda b, t: (b, 0, t, 0))
        in_specs += [spec, spec]
        args += [o, l]
    tok_spec = pl.BlockSpec((1, tm, D), lambda b, t: (b, t, 0))
    in_specs += [tok_spec, tok_spec, tok_spec]
    args += [ya, gb, x]
    consts = (pb, ow, g1, b1, rwh, rwl)
    in_specs += [_const_spec(c.shape) for c in consts]
    in_specs += [pl.BlockSpec(memory_space=pl.ANY) for _ in anchors]
    return pl.pallas_call(
        functools.partial(_merge_kernel, alpha=alpha, n_anchors=len(anchors)),
        grid=(B, S // tm),
        in_specs=in_specs,
        out_specs=[tok_spec,
                   pl.BlockSpec((1, tm, D // 2), lambda b, t: (b, t, 0)),
                   pl.BlockSpec((N_EXPERTS, tm), lambda b, t: (0, b * (S // tm) + t))],
        out_shape=[jax.ShapeDtypeStruct((B, S, D), F32),
                   jax.ShapeDtypeStruct((B, S, D // 2), I32),
                   jax.ShapeDtypeStruct((N_EXPERTS, B * S), F32)],
        scratch_shapes=[pltpu.VMEM((GROUP_WIDTH // LANES, tm, LANES), F32) for _ in range(4)],
        compiler_params=pltpu.CompilerParams(
            dimension_semantics=("arbitrary", "arbitrary"), vmem_limit_bytes=VMEM_LIMIT),
        name="merge_norm_router",
    )(*args, *consts, *anchors)


def _sortable_key(x):
    bits = pltpu.bitcast(x, I32)
    return jnp.where(bits < 0, bits ^ jnp.int32(0x7FFFFFFF), bits)


def _route_kernel(sc_ref, bias_ref, before_ref, eidx_ref, rank_ref, gate_ref, cnt_ref, carry_ref):
    ne, tm = sc_ref.shape
    gsize = ne // N_EXPERT_GROUPS
    neg_inf = np.float32(-np.inf)
    removed = jnp.int32(-2 ** 31)

    @pl.when(pl.program_id(0) == 0)
    def _():
        carry_ref[...] = jnp.zeros_like(carry_ref)

    scores = sc_ref[...]
    biased = scores + bias_ref[...]

    gsum = []
    for g in range(N_EXPERT_GROUPS):
        v = biased[g * gsize:(g + 1) * gsize]
        m1 = jnp.max(v, axis=0, keepdims=True)
        n1 = jnp.sum(jnp.where(v == m1, 1.0, 0.0), axis=0, keepdims=True)
        m2 = jnp.max(jnp.where(v < m1, v, neg_inf), axis=0, keepdims=True)
        gsum.append(m1 + jnp.where(n1 >= 2.0, m1, m2))
    gkey = _sortable_key(jnp.concatenate(gsum, axis=0))

    def pick_first_max(keys, ids, n_ids):
        m = jnp.max(keys, axis=0, keepdims=True)
        idx = jnp.min(jnp.where(keys == m, ids, n_ids), axis=0, keepdims=True)
        hit = ids == idx
        return idx, hit, jnp.where(hit, removed, keys)

    gid = lax.broadcasted_iota(I32, (N_EXPERT_GROUPS, tm), 0)
    for _ in range(TOPK_GROUPS):
        _, _, gkey = pick_first_max(gkey, gid, N_EXPERT_GROUPS)
    group_on = gkey == removed

    masked = jnp.concatenate(
        [jnp.where(group_on[g:g + 1], biased[g * gsize:(g + 1) * gsize], neg_inf)
         for g in range(N_EXPERT_GROUPS)], axis=0)
    keys = _sortable_key(masked)
    eid = lax.broadcasted_iota(I32, (ne, tm), 0)
    picks = []
    for _ in range(TOP_K):
        idx, _, keys = pick_first_max(keys, eid, ne)
        picks.append(idx)

    sel = jnp.where(keys == removed, 1.0, 0.0)
    ranks = jnp.dot(sel.astype(BF16), before_ref[...], preferred_element_type=F32) + carry_ref[...]
    carry_ref[...] = carry_ref[...] + jnp.sum(sel, axis=1, keepdims=True)
    cnt_ref[...] = carry_ref[...]

    s_k, r_k = [], []
    for idx in picks:
        hit = eid == idx
        s_k.append(jnp.sum(jnp.where(hit, scores, 0.0), axis=0, keepdims=True))
        r_k.append(jnp.sum(jnp.where(hit, ranks, 0.0), axis=0, keepdims=True))
    total = s_k[0]
    for s in s_k[1:]:
        total = total + s
    eidx_ref[...] = jnp.concatenate(picks, axis=0)
    rank_ref[...] = jnp.concatenate(r_k, axis=0).astype(I32)
    gate_ref[...] = jnp.concatenate([s / total * np.float32(ROUTED_SCALE) for s in s_k], axis=0)


def _route(scores_t, bias, tok0, T):
    ne = scores_t.shape[0]
    tm = ROUTE_TILE
    first = tok0 // tm
    before = jnp.asarray(np.triu(np.ones((tm, tm), np.float32), k=1), BF16)
    out_spec = pl.BlockSpec((TOP_K, tm), lambda i: (0, i))
    return pl.pallas_call(
        _route_kernel,
        grid=(T // tm,),
        in_specs=[pl.BlockSpec((ne, tm), lambda i: (0, first + i)), _const_spec(bias.shape),
                  _const_spec(before.shape)],
        out_specs=[out_spec, out_spec, out_spec, _const_spec((ne, 1))],
        out_shape=[jax.ShapeDtypeStruct((TOP_K, T), I32), jax.ShapeDtypeStruct((TOP_K, T), I32),
                   jax.ShapeDtypeStruct((TOP_K, T), F32), jax.ShapeDtypeStruct((ne, 1), F32)],
        scratch_shapes=[pltpu.VMEM((ne, 1), F32)],
        compiler_params=pltpu.CompilerParams(dimension_semantics=("arbitrary",), vmem_limit_bytes=VMEM_LIMIT),
        name="route_topk",
    )(scores_t, bias, before)


def _dest_kernel(pstart_ref, eidx_ref, rank_ref, out_ref):
    eidx = eidx_ref[...]
    start = jnp.zeros(eidx.shape, I32)
    for e in range(N_EXPERTS):
        start = jnp.where(eidx == e, pstart_ref[e], start)
    out_ref[...] = start + rank_ref[...]


def _dest_rows(eidx_t, rank_t, pstart):
    T = eidx_t.shape[1]
    tm = DEST_TILE
    tok_spec = pl.BlockSpec((TOP_K, tm), lambda i: (0, i))
    return pl.pallas_call(
        _dest_kernel,
        grid=(T // tm,),
        in_specs=[pl.BlockSpec(memory_space=pltpu.SMEM), tok_spec, tok_spec],
        out_specs=tok_spec,
        out_shape=jax.ShapeDtypeStruct((TOP_K, T), I32),
        compiler_params=pltpu.CompilerParams(dimension_semantics=("arbitrary",)),
        name="moe_dest_rows",
    )(pstart, eidx_t, rank_t)


def _sc_workers():
    info = plsc.get_sparse_core_info()
    return info.num_cores, info.num_cores * info.num_subcores


def _sc_scatter_rows(rows, row0, dest, n_out):
    n_dst, n_chunks, ch = dest.shape
    width = rows.shape[1]
    n_cores, n_workers = _sc_workers()
    per_w = n_chunks // n_workers
    assert n_chunks % n_workers == 0 and per_w % 8 == 0 and row0 + n_chunks * ch <= rows.shape[0]

    def body(rows_hbm, dest_hbm, out_hbm, idx_v, buf, lsem, ssem):
        wid = lax.axis_index("s") * n_cores + lax.axis_index("c")
        c0 = wid * per_w
        for k in range(n_dst):
            pltpu.sync_copy(dest_hbm.at[k, pl.ds(pl.multiple_of(c0, 8), per_w)], idx_v.at[k])

        def load(c, b):
            return pltpu.make_async_copy(rows_hbm.at[pl.ds(row0 + (c0 + c) * ch, ch)], buf.at[b], lsem.at[b])

        def scatter(c, b, k):
            return pltpu.make_async_copy(buf.at[b], out_hbm.at[idx_v.at[k, c]], ssem.at[b])

        load(0, 0).start()

        @pl.loop(0, per_w, step=2)
        def _(c):
            for b in range(2):
                cc = c + b
                load(cc, b).wait()

                @pl.when(cc >= 1)
                def _():
                    for k in range(n_dst):
                        scatter(cc - 1, 1 - b, k).wait()

                @pl.when(cc + 1 < per_w)
                def _():
                    load(cc + 1, 1 - b).start()

                for k in range(n_dst):
                    scatter(cc, b, k).start()

        for k in range(n_dst):
            scatter(per_w - 1, 1, k).wait()

    mesh = plsc.VectorSubcoreMesh(core_axis_name="c", subcore_axis_name="s")
    return pl.kernel(
        body, out_type=jax.ShapeDtypeStruct((n_out, width), rows.dtype), mesh=mesh,
        scratch_types=[pltpu.VMEM((n_dst, per_w, ch), I32), pltpu.VMEM((2, ch, width), rows.dtype),
                       pltpu.SemaphoreType.DMA((2,)), pltpu.SemaphoreType.DMA((2,))],
        name="moe_dispatch_scatter",
    )(rows, dest)


def _sc_gather_weighted_sum(table, idx, gates):
    n_src, n_tok = idx.shape
    ct = SC_SUM_CHUNK
    n_chunks = n_tok // ct
    half = table.shape[1]
    lanes = plsc.get_sparse_core_info().num_lanes
    n_cores, n_workers = _sc_workers()
    per_w = n_chunks // n_workers
    assert n_chunks % n_workers == 0 and per_w % 2 == 0 and half % lanes == 0

    def body(table_hbm, idx_hbm, gate_hbm, out_hbm, idx_v, gate_v, buf, out_v, gsem, wsem):
        wid = lax.axis_index("s") * n_cores + lax.axis_index("c")
        c0 = wid * per_w
        pltpu.sync_copy(idx_hbm.at[:, pl.ds(c0 * ct, per_w * ct)], idx_v)
        pltpu.sync_copy(gate_hbm.at[:, pl.ds(c0 * ct, per_w * ct)], gate_v)
        shift = jnp.full((lanes,), 16, I32)
        hi_mask = jnp.full((lanes,), -65536, I32)

        def gather(c, b, k):
            return pltpu.make_async_copy(table_hbm.at[idx_v.at[k, pl.ds(c * ct, ct)]], buf.at[b, k], gsem.at[b])

        def write(c, b):
            return pltpu.make_async_copy(out_v.at[b], out_hbm.at[pl.ds((c0 + c) * ct, ct)], wsem.at[b])

        for k in range(n_src):
            gather(0, 0, k).start()

        @pl.loop(0, per_w, step=2)
        def _(c):
            for b in range(2):
                cc = c + b
                for k in range(n_src):
                    gather(cc, b, k).wait()

                @pl.when(cc + 1 < per_w)
                def _():
                    for k in range(n_src):
                        gather(cc + 1, 1 - b, k).start()

                @pl.when(cc >= 2)
                def _():
                    write(cc - 2, b).wait()

                @pl.loop(0, ct)
                def _(t):
                    tok = jnp.full((lanes,), cc * ct + t, I32)
                    g = [plsc.load_gather(gate_v, [jnp.full((lanes,), k, I32), tok]) for k in range(n_src)]

                    @plsc.parallel_loop(0, half, step=lanes, unroll=2)
                    def _(col):
                        lo = jnp.zeros((lanes,), F32)
                        hi = jnp.zeros((lanes,), F32)
                        for k in range(n_src):
                            w = buf[b, k, t, pl.ds(col, lanes)]
                            lo = lo + g[k] * plsc.bitcast(lax.shift_left(w, shift), F32)
                            hi = hi + g[k] * plsc.bitcast(w & hi_mask, F32)
                        out_v[b, t, pl.ds(col, lanes)] = lo
                        out_v[b, t, pl.ds(half + col, lanes)] = hi

                write(cc, b).start()

        write(per_w - 2, 0).wait()
        write(per_w - 1, 1).wait()

    mesh = plsc.VectorSubcoreMesh(core_axis_name="c", subcore_axis_name="s")
    return pl.kernel(
        body, out_type=jax.ShapeDtypeStruct((n_chunks * ct, 2 * half), F32), mesh=mesh,
        scratch_types=[pltpu.VMEM((n_src, per_w * ct), I32), pltpu.VMEM((n_src, per_w * ct), F32),
                       pltpu.VMEM((2, n_src, ct, half), I32), pltpu.VMEM((2, ct, 2 * half), F32),
                       pltpu.SemaphoreType.DMA((2,)), pltpu.SemaphoreType.DMA((2,))],
        compiler_params=pltpu.CompilerParams(needs_layout_passes=False),
        name="moe_combine_gather_sum",
    )(table, idx, gates)


def _sc_pack_row_pairs(w):
    R, C = w.shape
    lanes = plsc.get_sparse_core_info().num_lanes
    cr = SC_PACK_CHUNK_WORDS // C
    n_chunks = R // cr
    n_cores, n_workers = _sc_workers()
    per_w = n_chunks // n_workers
    assert R % cr == 0 and n_chunks % n_workers == 0 and per_w % 2 == 0 and cr % 2 == 0 and C % lanes == 0

    def body(w_hbm, out_hbm, in_v, out_v, lsem, ssem):
        wid = lax.axis_index("s") * n_cores + lax.axis_index("c")
        c0 = wid * per_w

        def load(c, b):
            return pltpu.make_async_copy(w_hbm.at[pl.ds((c0 + c) * cr, cr)], in_v.at[b], lsem.at[b])

        def store(c, b):
            return pltpu.make_async_copy(out_v.at[b], out_hbm.at[pl.ds((c0 + c) * (cr // 2), cr // 2)],
                                         ssem.at[b])

        load(0, 0).start()

        @pl.loop(0, per_w, step=2)
        def _(c):
            for b in range(2):
                cc = c + b
                load(cc, b).wait()

                @pl.when(cc + 1 < per_w)
                def _():
                    load(cc + 1, 1 - b).start()

                @pl.when(cc >= 2)
                def _():
                    store(cc - 2, b).wait()

                @pl.loop(0, cr // 2)
                def _(i):
                    @plsc.parallel_loop(0, C, step=lanes, unroll=4)
                    def _(col):
                        even = in_v[b, 2 * i, pl.ds(col, lanes)]
                        odd = in_v[b, 2 * i + 1, pl.ds(col, lanes)]
                        pair = plsc.pack(even, odd, format=plsc.PackFormat.INTERLEAVED)
                        out_v[b, i, pl.ds(col, lanes)] = plsc.bitcast(pair, I32)

                store(cc, b).start()

        store(per_w - 2, 0).wait()
        store(per_w - 1, 1).wait()

    mesh = plsc.VectorSubcoreMesh(core_axis_name="c", subcore_axis_name="s")
    return pl.kernel(
        body, out_type=jax.ShapeDtypeStruct((R // 2, C), I32), mesh=mesh,
        scratch_types=[pltpu.VMEM((2, cr, C), F32), pltpu.VMEM((2, cr // 2, C), I32),
                       pltpu.SemaphoreType.DMA((2,)), pltpu.SemaphoreType.DMA((2,))],
        compiler_params=pltpu.CompilerParams(needs_layout_passes=False),
        name="expert_weights_bf16",
    )(w)


def _expert_kernel(be_ref, ue_ref, nu_ref, xs_hbm, wg_hbm, wu_hbm, wd_hbm, y_hbm,
                   xbuf, ybuf, wg_v, wu_v, wd_v, xsem, ysem, wsem):
    n_used = nu_ref[0]
    n_exp = nu_ref[1]
    blk = MOE_BLOCK
    rows_gu = D_MODEL // 2
    rows_d = EXPERT_DIM // 2

    def rows(j):
        return pl.ds(pl.multiple_of(j * blk, blk), blk)

    def x_copy(j, p):
        return pltpu.make_async_copy(xs_hbm.at[rows(j)], xbuf.at[p], xsem.at[p])

    def y_copy(j, p):
        return pltpu.make_async_copy(ybuf.at[p], y_hbm.at[rows(j)], ysem.at[p])

    def w_copies(q, s):
        e = ue_ref[q]
        gu = pl.ds(pl.multiple_of(e * rows_gu, rows_gu), rows_gu)
        dn = pl.ds(pl.multiple_of(e * rows_d, rows_d), rows_d)
        return (pltpu.make_async_copy(wg_hbm.at[gu], wg_v.at[s], wsem.at[s, 0]),
                pltpu.make_async_copy(wu_hbm.at[gu], wu_v.at[s], wsem.at[s, 1]),
                pltpu.make_async_copy(wd_hbm.at[dn], wd_v.at[s], wsem.at[s, 2]))

    for q0 in range(W_SLOTS - 1):
        @pl.when(q0 < n_exp)
        def _():
            for c in w_copies(q0, q0):
                c.start()

    for j0 in range(X_SLOTS):
        @pl.when(j0 < n_used)
        def _():
            x_copy(j0, j0).start()

    def block_step(j, p, q):
        is_new = (j == 0) | (be_ref[j] != be_ref[jnp.maximum(j - 1, 0)])
        q = q + is_new.astype(I32)

        s = q % W_SLOTS

        @pl.when(is_new)
        def _():
            for c in w_copies(q, s):
                c.wait()

            @pl.when(q + W_SLOTS - 1 < n_exp)
            def _():
                for c in w_copies(q + W_SLOTS - 1, (q + W_SLOTS - 1) % W_SLOTS):
                    c.start()

        x_copy(j, p).wait()

        @pl.when(j >= X_SLOTS)
        def _():
            y_copy(j - X_SLOTS, p).wait()

        wg = pltpu.bitcast(wg_v[s], BF16)
        wu = pltpu.bitcast(wu_v[s], BF16)
        wd = pltpu.bitcast(wd_v[s], BF16)
        lo, hi = _unpack_bf16_pairs(xbuf[p])
        x = jnp.concatenate([lo.astype(BF16), hi.astype(BF16)], axis=1)
        g = jnp.dot(x, wg, preferred_element_type=F32)
        u = jnp.dot(x, wu, preferred_element_type=F32)
        hb = (g * jax.nn.sigmoid(g) * u).astype(BF16)
        ybuf[p] = _pack_bf16_pairs(jnp.dot(hb, wd, preferred_element_type=F32))
        y_copy(j, p).start()

        @pl.when(j + X_SLOTS < n_used)
        def _():
            x_copy(j + X_SLOTS, p).start()

        return q

    def group(m, q):
        q = block_step(X_SLOTS * m, 0, q)
        for p in range(1, X_SLOTS):
            j = X_SLOTS * m + p
            q = lax.cond(j < n_used, functools.partial(block_step, j, p), lambda q: q, q)
        return q

    lax.fori_loop(0, (n_used + X_SLOTS - 1) // X_SLOTS, group, jnp.int32(-1))

    for back in range(X_SLOTS, 0, -1):
        b = n_used - back
        for p in range(X_SLOTS):
            @pl.when((b >= 0) & (b % X_SLOTS == p))
            def _():
                y_copy(b, p).wait()


def _experts(block_e, used_e, counts2, xs, wg, wu, wd):
    n_rows, half = xs.shape
    smem = pl.BlockSpec(memory_space=pltpu.SMEM)
    hbm = pl.BlockSpec(memory_space=pl.ANY)
    return pl.pallas_call(
        _expert_kernel,
        in_specs=[smem, smem, smem, hbm, hbm, hbm, hbm],
        out_specs=hbm,
        out_shape=jax.ShapeDtypeStruct((n_rows, half), I32),
        scratch_shapes=[pltpu.VMEM((X_SLOTS, MOE_BLOCK, half), I32), pltpu.VMEM((X_SLOTS, MOE_BLOCK, half), I32),
                        pltpu.VMEM((W_SLOTS, D_MODEL // 2, EXPERT_DIM), I32),
                        pltpu.VMEM((W_SLOTS, D_MODEL // 2, EXPERT_DIM), I32),
                        pltpu.VMEM((W_SLOTS, EXPERT_DIM // 2, D_MODEL), I32),
                        pltpu.SemaphoreType.DMA((X_SLOTS,)), pltpu.SemaphoreType.DMA((X_SLOTS,)),
                        pltpu.SemaphoreType.DMA((W_SLOTS, 3))],
        compiler_params=pltpu.CompilerParams(vmem_limit_bytes=VMEM_LIMIT),
        name="moe_experts",
    )(block_e, used_e, counts2, xs, wg, wu, wd)


def _combine_kernel(routed_ref, x1_ref, swgu_ref, swd_ref, g2_ref, b2_ref, *rest, alpha):
    out_ref = rest[-1]
    x1 = x1_ref[...]
    gu = jnp.dot(x1.astype(BF16), swgu_ref[...], preferred_element_type=F32)
    g = gu[:, :EXPERT_DIM]
    u = gu[:, EXPERT_DIM:]
    shared = jnp.dot((g * jax.nn.sigmoid(g) * u).astype(BF16), swd_ref[...], preferred_element_type=F32)
    out_ref[...] = _layer_norm(alpha * x1 + (routed_ref[...] + shared), g2_ref[...], b2_ref[...])


def _combine(routed, x1, tok0, swgu, swd, g2, b2, alpha, out_prev):
    T, D = x1.shape
    n_tok = routed.shape[0]
    tc = COMBINE_TILE
    first = tok0 // tc
    consts = (swgu, swd, g2, b2)
    args = [routed, x1, *consts]
    in_specs = [pl.BlockSpec((tc, D), lambda i: (i, 0)),
                pl.BlockSpec((tc, D), lambda i: (first + i, 0))] + [_const_spec(c.shape) for c in consts]
    aliases = {}
    if out_prev is not None:
        aliases = {len(args): 0}
        args.append(out_prev)
        in_specs.append(pl.BlockSpec(memory_space=pl.ANY))
    return pl.pallas_call(
        functools.partial(_combine_kernel, alpha=alpha),
        grid=(n_tok // tc,),
        in_specs=in_specs,
        out_specs=pl.BlockSpec((tc, D), lambda i: (first + i, 0)),
        out_shape=jax.ShapeDtypeStruct((T, D), F32),
        input_output_aliases=aliases,
        compiler_params=pltpu.CompilerParams(dimension_semantics=("arbitrary",), vmem_limit_bytes=VMEM_LIMIT),
        name="moe_combine_norm",
    )(*args)


def _mixer(x, in_w, in_b, ng, nb, spatial_w, spatial_b, proj_a_w, proj_b_w, out_w, ln1_g, ln1_b,
           router_w, alpha, anchors):
    gw, aw, D = GMLP_WIDTH, ATTN_WIDTH, D_MODEL
    w = in_w.astype(BF16)
    q0 = 2 * gw
    wuv, buv = w[:, :q0], in_b[None, :q0]
    watt = jnp.stack([jnp.concatenate([w[:, q0 + s * aw + p * GROUP_WIDTH:q0 + s * aw + (p + 1) * GROUP_WIDTH]
                                       for s in range(3)], axis=1) for p in range(len(DILATED_PATTERNS))])
    batt = jnp.stack([jnp.concatenate([in_b[q0 + s * aw + p * GROUP_WIDTH:q0 + s * aw + (p + 1) * GROUP_WIDTH]
                                       for s in range(3)])[None] for p in range(len(DILATED_PATTERNS))])
    g0 = q0 + 3 * aw
    wg, bg = w[:, g0:], in_b[None, g0:]
    sb = jnp.repeat(spatial_b.T, gw // GMLP_GROUPS, axis=1)
    ya, gb, a1, a4, a16 = _input_projection(
        x, wuv, buv, watt, batt, wg, bg, ng[None], nb[None], spatial_w, sb, proj_a_w.astype(BF16))
    attn_outs = [_dilated_attention(a, p, d) for p, (a, (_, d)) in enumerate(zip((a1, a4, a16), DILATED_PATTERNS))]
    rw_t = router_w.T
    rwh = rw_t.astype(BF16)
    rwl = (rw_t - rwh.astype(F32)).astype(BF16)
    return _merge_and_norm(attn_outs, ya, gb, x, proj_b_w.astype(BF16), out_w.astype(BF16),
                           ln1_g[None], ln1_b[None], rwh, rwl, alpha, anchors)


def _moe(x1, x1p, scores_t, router_bias, w_gate, w_up, w_down, sw_gate, sw_up, sw_down, ln2_g, ln2_b, alpha):
    swgu = jnp.concatenate([sw_gate, sw_up], axis=1).astype(BF16)
    swd = sw_down.astype(BF16)
    out = None
    tok0 = 0
    for part, eighths in enumerate(MOE_TOKEN_SPLIT):
        n_tok = x1.shape[0] * eighths // 8
        n_blocks = (n_tok * TOP_K + N_EXPERTS * (MOE_BLOCK - 1)) // MOE_BLOCK
        eidx_t, rank_t, gate_t, counts = _route(scores_t, router_bias[:, None], tok0, n_tok)
        counts = counts[:, 0].astype(I32)
        padded = (counts + MOE_BLOCK - 1) // MOE_BLOCK * MOE_BLOCK
        pend = jnp.cumsum(padded).astype(I32)
        pstart = pend - padded
        block_starts = jnp.arange(n_blocks, dtype=I32) * MOE_BLOCK
        block_e = jnp.minimum(jnp.sum((pend[None, :] <= block_starts[:, None]).astype(I32), axis=1),
                              N_EXPERTS - 1)
        used = counts > 0
        used_e = jnp.argsort(jnp.logical_not(used), stable=True).astype(I32)
        counts2 = jnp.stack([pend[-1] // MOE_BLOCK, jnp.sum(used.astype(I32))]).astype(I32)
        dest_t = _dest_rows(eidx_t, rank_t, pstart)
        xs = _sc_scatter_rows(x1p, tok0, dest_t.reshape(TOP_K, n_tok // SC_CHUNK, SC_CHUNK), n_blocks * MOE_BLOCK)
        y_rows = _experts(block_e, used_e, counts2, xs, w_gate, w_up, w_down)
        pieces = MOE_TAIL_PIECES if part == len(MOE_TOKEN_SPLIT) - 1 else 1
        n_piece = n_tok // pieces
        for q in range(pieces):
            cols = slice(q * n_piece, (q + 1) * n_piece)
            routed = _sc_gather_weighted_sum(y_rows, dest_t[:, cols], gate_t[:, cols])
            out = _combine(routed, x1, tok0 + q * n_piece, swgu, swd, ln2_g[None], ln2_b[None], alpha, out)
        tok0 += n_tok
    return out


def kernel(x, in_w, in_b, gmlp_norm_g, gmlp_norm_b, spatial_w, spatial_b, proj_a_w, proj_b_w, out_w,
           ln1_g, ln1_b, router_w, router_bias, expert_w_gate, expert_w_up, expert_w_down,
           shared_w_gate, shared_w_up, shared_w_down, ln2_g, ln2_b):
    B, S, D = x.shape
    depth = in_w.shape[0]
    alpha = np.float32((2.0 * depth) ** 0.25)
    for l in range(depth):
        packed = [_sc_pack_row_pairs(w[l].reshape(-1, w.shape[-1]))
                  for w in (expert_w_gate, expert_w_up, expert_w_down)]
        x1, x1p, scores_t = _mixer(x, in_w[l], in_b[l], gmlp_norm_g[l], gmlp_norm_b[l], spatial_w[l],
                                 spatial_b[l], proj_a_w[l], proj_b_w[l], out_w[l], ln1_g[l], ln1_b[l],
                                 router_w[l], alpha, [p[:8] for p in packed])
        out = _moe(x1.reshape(B * S, D), x1p.reshape(B * S, D // 2), scores_t,
                   router_bias[l], *packed,
                   shared_w_gate[l], shared_w_up[l], shared_w_down[l], ln2_g[l], ln2_b[l], alpha)
        x = out.reshape(B, S, D)
    return x
```

```python
import functools
import math

import numpy as np
import jax
import jax.numpy as jnp
from jax import lax
from jax.experimental import pallas as pl
from jax.experimental.pallas import tpu as pltpu
from jax.experimental.pallas import tpu_sc as plsc

F32 = jnp.float32
BF16 = jnp.bfloat16
U32 = jnp.uint32
I32 = jnp.int32

D_MODEL = 1024
GMLP_WIDTH = 1024
GMLP_GROUPS = 8
GMLP_CHUNK = 128
HEAD_DIM = 64
DILATED_PATTERNS = ((128, 1), (512, 4), (2048, 16))
HEADS_PER_GROUP = 4
GROUP_WIDTH = HEADS_PER_GROUP * HEAD_DIM
ATTN_WIDTH = GROUP_WIDTH * len(DILATED_PATTERNS)
ATTN_BLOCK = 128
N_EXPERTS = 256
TOP_K = 8
TOP_K_SHIFT = 3
N_EXPERT_GROUPS = 8
TOPK_GROUPS = 4
EXPERT_DIM = 256
ROUTED_SCALE = 2.5
LN_EPS = 1e-5
LANES = 128
MASKED_SCORE = -1e30

PROJ_TILE = 512
ATTN_QBLOCKS = 8
MERGE_TILE = 512
ROUTE_TILE = 512
DEST_TILE = 2048
MOE_BLOCK = 256
SC_PACK_CHUNK_WORDS = 16384
SC_CHUNK = 64
SC_SUM_CHUNK = 8
MOE_TOKEN_SPLIT = (4, 4)
MOE_TAIL_PIECES = 2
X_SLOTS = 4
W_SLOTS = 3
COMBINE_TILE = 512
VMEM_LIMIT = 56 * 1024 * 1024


def _layer_norm(y, g, b):
    mu = jnp.mean(y, axis=-1, keepdims=True)
    yc = y - mu
    var = jnp.mean(yc * yc, axis=-1, keepdims=True)
    return yc * lax.rsqrt(var + LN_EPS) * g + b


def _gelu(x):
    return 0.5 * x * (1.0 + lax.erf(x * np.float32(math.sqrt(0.5))))


def _pack_bf16_pairs(x):
    w = x.shape[1] // 2
    bits = pltpu.bitcast(x.astype(BF16).astype(F32), U32)
    return pltpu.bitcast((bits[:, :w] >> 16) | (bits[:, w:] & jnp.uint32(0xFFFF0000)), I32)


def _unpack_bf16_pairs(words):
    w = pltpu.bitcast(words, U32)
    lo = pltpu.bitcast(w << 16, F32)
    hi = pltpu.bitcast(w & jnp.uint32(0xFFFF0000), F32)
    return lo, hi


def _const_spec(shape):
    nd = len(shape)
    return pl.BlockSpec(shape, lambda *_: (0,) * nd)


def _proj_kernel(x_ref, wuv_ref, buv_ref, watt_ref, batt_ref, wg_ref, bg_ref, ng_ref, nb_ref,
                 sw_ref, sb_ref, pa_ref, ya_ref, gb_ref, a1_ref, a4_ref, a16_ref, xc_ref):
    tm = x_ref.shape[1]
    gw = GMLP_WIDTH
    xb = x_ref[0].astype(BF16)

    def proj(w, b):
        return jnp.dot(xb, w, preferred_element_type=F32) + b

    h_v = proj(wuv_ref[:, gw:], buv_ref[:, gw:])
    h_u = proj(wuv_ref[:, :gw], buv_ref[:, :gw])
    v = _layer_norm(_gelu(h_v), ng_ref[...], nb_ref[...]).astype(BF16)
    h_ga = proj(wg_ref[:, :D_MODEL], bg_ref[:, :D_MODEL])
    u = _gelu(h_u)

    cw = gw // GMLP_GROUPS
    row = lax.broadcasted_iota(I32, (GMLP_CHUNK, GMLP_CHUNK), 0)
    col = lax.broadcasted_iota(I32, (GMLP_CHUNK, GMLP_CHUNK), 1)
    ws = [jnp.where(row >= col, sw_ref[g], 0.0).astype(BF16) for g in range(GMLP_GROUPS)]
    chunks = []
    for c in range(tm // GMLP_CHUNK):
        vc = v[c * GMLP_CHUNK:(c + 1) * GMLP_CHUNK]
        cols = [jnp.dot(ws[g], vc[:, g * cw:(g + 1) * cw], preferred_element_type=F32)
                for g in range(GMLP_GROUPS)]
        chunks.append(jnp.concatenate(cols, axis=1) + sb_ref[...])
    vmix = jnp.concatenate(chunks, axis=0)
    h_gb = proj(wg_ref[:, D_MODEL:], bg_ref[:, D_MODEL:])
    ga = jax.nn.sigmoid(h_ga)
    ya = jnp.dot((u * vmix).astype(BF16), pa_ref[...], preferred_element_type=F32)
    gb_ref[0] = jax.nn.sigmoid(h_gb).astype(BF16)

    n_chunks = x_ref.shape[2] // LANES
    for c in range(n_chunks):
        xc_ref[c] = x_ref[0, :, c * LANES:(c + 1) * LANES]

    def attn_proj(p, d):
        n = tm // d
        if d == 1:
            xp = xb
        else:
            xp = jnp.concatenate(
                [jnp.concatenate([xc_ref[c, pl.ds(r, n, stride=d), :] for c in range(n_chunks)], axis=1)
                 for r in range(d)], axis=0).astype(BF16)
        return jnp.dot(xp, watt_ref[p], preferred_element_type=F32)

    def attn_store(p, d, h, a_ref):
        n = tm // d
        h = (h + batt_ref[p]).astype(BF16)
        for r in range(d):
            a_ref[0, r] = h[r * n:(r + 1) * n]

    a_refs = (a1_ref, a4_ref, a16_ref)
    dils = [d for _, d in DILATED_PATTERNS]
    h_prev = attn_proj(0, dils[0])
    ya_ref[0] = (ga * ya).astype(BF16)
    for p in range(1, len(dils)):
        h_next = attn_proj(p, dils[p])
        attn_store(p - 1, dils[p - 1], h_prev, a_refs[p - 1])
        h_prev = h_next
    attn_store(len(dils) - 1, dils[-1], h_prev, a_refs[-1])


def _input_projection(x, wuv, buv, watt, batt, wg, bg, ng, nb, sw, sb, pa):
    B, S, D = x.shape
    tm = PROJ_TILE
    grid = (B, S // tm)
    out_shape = [jax.ShapeDtypeStruct((B, S, D), BF16), jax.ShapeDtypeStruct((B, S, D), BF16)]
    out_specs = [pl.BlockSpec((1, tm, D), lambda b, t: (b, t, 0)),
                 pl.BlockSpec((1, tm, D), lambda b, t: (b, t, 0))]
    for _, d in DILATED_PATTERNS:
        out_shape.append(jax.ShapeDtypeStruct((B, d, S // d, ATTN_WIDTH), BF16))
        out_specs.append(pl.BlockSpec((1, d, tm // d, ATTN_WIDTH), lambda b, t: (b, 0, t, 0)))
    consts = (wuv, buv, watt, batt, wg, bg, ng, nb, sw, sb, pa)
    return pl.pallas_call(
        _proj_kernel,
        grid=grid,
        in_specs=[pl.BlockSpec((1, tm, D), lambda b, t: (b, t, 0))] + [_const_spec(c.shape) for c in consts],
        out_specs=out_specs,
        out_shape=out_shape,
        scratch_shapes=[pltpu.VMEM((D // LANES, tm, LANES), F32)],
        compiler_params=pltpu.CompilerParams(
            dimension_semantics=("arbitrary", "arbitrary"), vmem_limit_bytes=VMEM_LIMIT),
        name="input_projection",
    )(x, *consts)


def _attn_kernel(qkv_ref, bias_ref, o_ref, lse_ref, *, qblocks, rblock):
    nq = pl.program_id(2)
    gwid = GROUP_WIDTH
    blk = ATTN_BLOCK
    lane = lax.broadcasted_iota(I32, (1, gwid), 1)
    head_masks = [(lane >= h * HEAD_DIM) & (lane < (h + 1) * HEAD_DIM) for h in range(HEADS_PER_GROUP)]
    q_scales = [jnp.where(m, np.float32(HEAD_DIM ** -0.5), 0.0).astype(BF16) for m in head_masks]

    def rows_of(j):
        n = nq * qblocks + j
        return n, pl.multiple_of(n * blk, blk), pl.multiple_of(jnp.maximum(n - 1, 0) * blk, blk)

    def scores(unit):
        ri, j = unit
        n, q0, p0 = rows_of(j)
        q = qkv_ref[0, ri, pl.ds(q0, blk), 0:gwid]
        kk = jnp.concatenate([qkv_ref[0, ri, pl.ds(p0, blk), gwid:2 * gwid],
                              qkv_ref[0, ri, pl.ds(q0, blk), gwid:2 * gwid]], axis=0)
        qs = jnp.concatenate([q * s for s in q_scales], axis=0)
        s = lax.dot_general(qs, kk, (((1,), (1,)), ((), ())), preferred_element_type=F32)
        return s + bias_ref[jnp.where(n == 0, 1, 0)]

    def finish(unit, s):
        ri, j = unit
        _, q0, p0 = rows_of(j)
        vv = jnp.concatenate([qkv_ref[0, ri, pl.ds(p0, blk), 2 * gwid:3 * gwid],
                              qkv_ref[0, ri, pl.ds(q0, blk), 2 * gwid:3 * gwid]], axis=0)
        m = jnp.max(s, axis=1, keepdims=True)
        p = jnp.exp(s - m)
        den = jnp.sum(p, axis=1, keepdims=True)
        pv = jnp.dot(p.astype(BF16), vv, preferred_element_type=F32)
        on = pv / den
        lse = m + jnp.log(den)
        o = jnp.zeros((blk, gwid), F32)
        l = jnp.zeros((blk, gwid), F32)
        for h in range(HEADS_PER_GROUP):
            o = jnp.where(head_masks[h], on[h * blk:(h + 1) * blk], o)
            l = jnp.where(head_masks[h], lse[h * blk:(h + 1) * blk], l)
        o_ref[0, ri, j * blk:(j + 1) * blk, :] = o.astype(BF16)
        lse_ref[0, ri, j * blk:(j + 1) * blk, :] = l

    units = [(ri, j) for ri in range(rblock) for j in range(qblocks)]
    s_next = scores(units[0])
    for i, unit in enumerate(units):
        s_cur = s_next
        if i + 1 < len(units):
            s_next = scores(units[i + 1])
        finish(unit, s_cur)


def _alibi_slopes(n):
    def pow2_slopes(m):
        start = 2.0 ** (-8.0 / m)
        return [start ** (i + 1) for i in range(m)]
    p = 2 ** int(math.floor(math.log2(n)))
    s = pow2_slopes(p)
    if p < n:
        s = s + pow2_slopes(2 * p)[0::2][: n - p]
    return np.array(sorted(s, reverse=True), dtype=np.float32)


def _attn_bias_tables(group, dilation):
    blk = ATTN_BLOCK
    slopes = _alibi_slopes(HEADS_PER_GROUP * len(DILATED_PATTERNS))
    slopes = slopes[group * HEADS_PER_GROUP:(group + 1) * HEADS_PER_GROUP]
    qi = np.arange(blk)[:, None]
    ki = np.arange(2 * blk)[None, :]
    delta = blk + qi - ki
    band = (delta >= 0) & (delta <= blk)
    bias = -slopes[:, None, None] * (delta * dilation).astype(np.float32)[None]
    full = np.where(band[None], bias, np.float32(MASKED_SCORE)).astype(np.float32)
    first = np.where((ki >= blk)[None], full, np.float32(MASKED_SCORE)).astype(np.float32)
    return full.reshape(HEADS_PER_GROUP * blk, 2 * blk), first.reshape(HEADS_PER_GROUP * blk, 2 * blk)


def _dilated_attention(qkv, group, dilation):
    B, d, sd, _ = qkv.shape
    qblocks = min(ATTN_QBLOCKS, sd // ATTN_BLOCK)
    rblock = min(d, ATTN_QBLOCKS // qblocks)
    rows = qblocks * ATTN_BLOCK
    bias = np.stack(_attn_bias_tables(group, dilation))
    grid = (B, d // rblock, sd // rows)
    out_spec = pl.BlockSpec((1, rblock, rows, GROUP_WIDTH), lambda b, r, n: (b, r, n, 0))
    return pl.pallas_call(
        functools.partial(_attn_kernel, qblocks=qblocks, rblock=rblock),
        grid=grid,
        in_specs=[pl.BlockSpec((1, rblock, sd, ATTN_WIDTH), lambda b, r, n: (b, r, 0, 0)),
                  _const_spec(bias.shape)],
        out_specs=[out_spec, out_spec],
        out_shape=[jax.ShapeDtypeStruct((B, d, sd, GROUP_WIDTH), BF16),
                   jax.ShapeDtypeStruct((B, d, sd, GROUP_WIDTH), F32)],
        compiler_params=pltpu.CompilerParams(
            dimension_semantics=("arbitrary", "arbitrary", "arbitrary"), vmem_limit_bytes=VMEM_LIMIT),
        name=f"dilated_attention_d{dilation}",
    )(qkv, jnp.asarray(bias))


def _merge_kernel(o1_ref, l1_ref, o4_ref, l4_ref, o16_ref, l16_ref, ya_ref, gb_ref, x_ref,
                  pb_ref, ow_ref, g1_ref, b1_ref, rwh_ref, rwl_ref, *rest, alpha, n_anchors):
    x1_ref, x1p_ref, sc_ref, so4, sl4, so16, sl16 = rest[n_anchors:]
    tm = x_ref.shape[1]
    n_chunks = GROUP_WIDTH // LANES
    for (o_ref, l_ref, so, sl, d) in ((o4_ref, l4_ref, so4, sl4, 4), (o16_ref, l16_ref, so16, sl16, 16)):
        n = tm // d
        for r in range(d):
            o_r = o_ref[0, r].astype(F32)
            l_r = l_ref[0, r]
            for c in range(n_chunks):
                so[c, pl.ds(r, n, stride=d), :] = o_r[:, c * LANES:(c + 1) * LANES]
                sl[c, pl.ds(r, n, stride=d), :] = l_r[:, c * LANES:(c + 1) * LANES]

    def natural(s):
        return jnp.concatenate([s[c] for c in range(n_chunks)], axis=1)

    l1 = l1_ref[0, 0]
    l4 = natural(sl4)
    l16 = natural(sl16)
    lmax = jnp.maximum(jnp.maximum(l1, l4), l16)
    e1 = jnp.exp(l1 - lmax)
    e4 = jnp.exp(l4 - lmax)
    e16 = jnp.exp(l16 - lmax)
    yb = (e1 * o1_ref[0, 0].astype(F32) + e4 * natural(so4) + e16 * natural(so16)) / (e1 + e4 + e16)
    ybp = jnp.dot(yb.astype(BF16), pb_ref[...], preferred_element_type=F32)
    merged = ya_ref[0] + gb_ref[0] * ybp.astype(BF16)
    mix = jnp.dot(merged, ow_ref[...], preferred_element_type=F32)
    x1 = _layer_norm(alpha * x_ref[0] + mix, g1_ref[...], b1_ref[...])
    x1_ref[0] = x1
    x1p_ref[0] = _pack_bf16_pairs(x1)
    hi = x1.astype(BF16)
    lo = (x1 - hi.astype(F32)).astype(BF16)
    def logits_t(w_ref, xt):
        return lax.dot_general(w_ref[...], xt, (((1,), (1,)), ((), ())), preferred_element_type=F32)
    sc_ref[...] = jax.nn.sigmoid(logits_t(rwh_ref, hi) + logits_t(rwh_ref, lo) + logits_t(rwl_ref, hi))


def _merge_and_norm(attn_outs, ya, gb, x, pb, ow, g1, b1, rwh, rwl, alpha, anchors):
    B, S, D = x.shape
    tm = MERGE_TILE
    in_specs = []
    args = []
    for (o, l), (_, d) in zip(attn_outs, DILATED_PATTERNS):
        spec = pl.BlockSpec((1, d, tm // d, GROUP_WIDTH), lambda b, t: (b, 0, t, 0))
        in_specs += [spec, spec]
        args += [o, l]
    tok_spec = pl.BlockSpec((1, tm, D), lambda b, t: (b, t, 0))
    in_specs += [tok_spec, tok_spec, tok_spec]
    args += [ya, gb, x]
    consts = (pb, ow, g1, b1, rwh, rwl)
    in_specs += [_const_spec(c.shape) for c in consts]
    in_specs += [pl.BlockSpec(memory_space=pl.ANY) for _ in anchors]
    return pl.pallas_call(
        functools.partial(_merge_kernel, alpha=alpha, n_anchors=len(anchors)),
        grid=(B, S // tm),
        in_specs=in_specs,
        out_specs=[tok_spec,
                   pl.BlockSpec((1, tm, D // 2), lambda b, t: (b, t, 0)),
                   pl.BlockSpec((N_EXPERTS, tm), lambda b, t: (0, b * (S // tm) + t))],
        out_shape=[jax.ShapeDtypeStruct((B, S, D), F32),
                   jax.ShapeDtypeStruct((B, S, D // 2), I32),
                   jax.ShapeDtypeStruct((N_EXPERTS, B * S), F32)],
        scratch_shapes=[pltpu.VMEM((GROUP_WIDTH // LANES, tm, LANES), F32) for _ in range(4)],
        compiler_params=pltpu.CompilerParams(
            dimension_semantics=("arbitrary", "arbitrary"), vmem_limit_bytes=VMEM_LIMIT),
        name="merge_norm_router",
    )(*args, *consts, *anchors)


def _sortable_key(x):
    bits = pltpu.bitcast(x, I32)
    return jnp.where(bits < 0, bits ^ jnp.int32(0x7FFFFFFF), bits)


def _route_kernel(sc_ref, bias_ref, before_ref, eidx_ref, rank_ref, gate_ref, cnt_ref, carry_ref):
    ne, tm = sc_ref.shape
    gsize = ne // N_EXPERT_GROUPS
    neg_inf = np.float32(-np.inf)
    removed = jnp.int32(-2 ** 31)

    @pl.when(pl.program_id(0) == 0)
    def _():
        carry_ref[...] = jnp.zeros_like(carry_ref)

    scores = sc_ref[...]
    biased = scores + bias_ref[...]

    gsum = []
    for g in range(N_EXPERT_GROUPS):
        v = biased[g * gsize:(g + 1) * gsize]
        m1 = jnp.max(v, axis=0, keepdims=True)
        n1 = jnp.sum(jnp.where(v == m1, 1.0, 0.0), axis=0, keepdims=True)
        m2 = jnp.max(jnp.where(v < m1, v, neg_inf), axis=0, keepdims=True)
        gsum.append(m1 + jnp.where(n1 >= 2.0, m1, m2))
    gkey = _sortable_key(jnp.concatenate(gsum, axis=0))

    def pick_first_max(keys, ids, n_ids):
        m = jnp.max(keys, axis=0, keepdims=True)
        idx = jnp.min(jnp.where(keys == m, ids, n_ids), axis=0, keepdims=True)
        hit = ids == idx
        return idx, hit, jnp.where(hit, removed, keys)

    gid = lax.broadcasted_iota(I32, (N_EXPERT_GROUPS, tm), 0)
    for _ in range(TOPK_GROUPS):
        _, _, gkey = pick_first_max(gkey, gid, N_EXPERT_GROUPS)
    group_on = gkey == removed

    masked = jnp.concatenate(
        [jnp.where(group_on[g:g + 1], biased[g * gsize:(g + 1) * gsize], neg_inf)
         for g in range(N_EXPERT_GROUPS)], axis=0)
    keys = _sortable_key(masked)
    eid = lax.broadcasted_iota(I32, (ne, tm), 0)
    picks = []
    for _ in range(TOP_K):
        idx, _, keys = pick_first_max(keys, eid, ne)
        picks.append(idx)

    sel = jnp.where(keys == removed, 1.0, 0.0)
    ranks = jnp.dot(sel.astype(BF16), before_ref[...], preferred_element_type=F32) + carry_ref[...]
    carry_ref[...] = carry_ref[...] + jnp.sum(sel, axis=1, keepdims=True)
    cnt_ref[...] = carry_ref[...]

    s_k, r_k = [], []
    for idx in picks:
        hit = eid == idx
        s_k.append(jnp.sum(jnp.where(hit, scores, 0.0), axis=0, keepdims=True))
        r_k.append(jnp.sum(jnp.where(hit, ranks, 0.0), axis=0, keepdims=True))
    total = s_k[0]
    for s in s_k[1:]:
        total = total + s
    eidx_ref[...] = jnp.concatenate(picks, axis=0)
    rank_ref[...] = jnp.concatenate(r_k, axis=0).astype(I32)
    gate_ref[...] = jnp.concatenate([s / total * np.float32(ROUTED_SCALE) for s in s_k], axis=0)


def _route(scores_t, bias, tok0, T):
    ne = scores_t.shape[0]
    tm = ROUTE_TILE
    first = tok0 // tm
    before = jnp.asarray(np.triu(np.ones((tm, tm), np.float32), k=1), BF16)
    out_spec = pl.BlockSpec((TOP_K, tm), lambda i: (0, i))
    return pl.pallas_call(
        _route_kernel,
        grid=(T // tm,),
        in_specs=[pl.BlockSpec((ne, tm), lambda i: (0, first + i)), _const_spec(bias.shape),
                  _const_spec(before.shape)],
        out_specs=[out_spec, out_spec, out_spec, _const_spec((ne, 1))],
        out_shape=[jax.ShapeDtypeStruct((TOP_K, T), I32), jax.ShapeDtypeStruct((TOP_K, T), I32),
                   jax.ShapeDtypeStruct((TOP_K, T), F32), jax.ShapeDtypeStruct((ne, 1), F32)],
        scratch_shapes=[pltpu.VMEM((ne, 1), F32)],
        compiler_params=pltpu.CompilerParams(dimension_semantics=("arbitrary",), vmem_limit_bytes=VMEM_LIMIT),
        name="route_topk",
    )(scores_t, bias, before)


def _dest_kernel(pstart_ref, eidx_ref, rank_ref, out_ref):
    eidx = eidx_ref[...]
    start = jnp.zeros(eidx.shape, I32)
    for e in range(N_EXPERTS):
        start = jnp.where(eidx == e, pstart_ref[e], start)
    out_ref[...] = start + rank_ref[...]


def _dest_rows(eidx_t, rank_t, pstart):
    T = eidx_t.shape[1]
    tm = DEST_TILE
    tok_spec = pl.BlockSpec((TOP_K, tm), lambda i: (0, i))
    return pl.pallas_call(
        _dest_kernel,
        grid=(T // tm,),
        in_specs=[pl.BlockSpec(memory_space=pltpu.SMEM), tok_spec, tok_spec],
        out_specs=tok_spec,
        out_shape=jax.ShapeDtypeStruct((TOP_K, T), I32),
        compiler_params=pltpu.CompilerParams(dimension_semantics=("arbitrary",)),
        name="moe_dest_rows",
    )(pstart, eidx_t, rank_t)


def _sc_workers():
    info = plsc.get_sparse_core_info()
    return info.num_cores, info.num_cores * info.num_subcores


def _sc_scatter_rows(rows, row0, dest, n_out):
    n_dst, n_chunks, ch = dest.shape
    width = rows.shape[1]
    n_cores, n_workers = _sc_workers()
    per_w = n_chunks // n_workers
    assert n_chunks % n_workers == 0 and per_w % 8 == 0 and row0 + n_chunks * ch <= rows.shape[0]

    def body(rows_hbm, dest_hbm, out_hbm, idx_v, buf, lsem, ssem):
        wid = lax.axis_index("s") * n_cores + lax.axis_index("c")
        c0 = wid * per_w
        for k in range(n_dst):
            pltpu.sync_copy(dest_hbm.at[k, pl.ds(pl.multiple_of(c0, 8), per_w)], idx_v.at[k])

        def load(c, b):
            return pltpu.make_async_copy(rows_hbm.at[pl.ds(row0 + (c0 + c) * ch, ch)], buf.at[b], lsem.at[b])

        def scatter(c, b, k):
            return pltpu.make_async_copy(buf.at[b], out_hbm.at[idx_v.at[k, c]], ssem.at[b])

        load(0, 0).start()

        @pl.loop(0, per_w, step=2)
        def _(c):
            for b in range(2):
                cc = c + b
                load(cc, b).wait()

                @pl.when(cc >= 1)
                def _():
                    for k in range(n_dst):
                        scatter(cc - 1, 1 - b, k).wait()

                @pl.when(cc + 1 < per_w)
                def _():
                    load(cc + 1, 1 - b).start()

                for k in range(n_dst):
                    scatter(cc, b, k).start()

        for k in range(n_dst):
            scatter(per_w - 1, 1, k).wait()

    mesh = plsc.VectorSubcoreMesh(core_axis_name="c", subcore_axis_name="s")
    return pl.kernel(
        body, out_type=jax.ShapeDtypeStruct((n_out, width), rows.dtype), mesh=mesh,
        scratch_types=[pltpu.VMEM((n_dst, per_w, ch), I32), pltpu.VMEM((2, ch, width), rows.dtype),
                       pltpu.SemaphoreType.DMA((2,)), pltpu.SemaphoreType.DMA((2,))],
        name="moe_dispatch_scatter",
    )(rows, dest)


def _sc_gather_weighted_sum(table, idx, gates):
    n_src, n_tok = idx.shape
    ct = SC_SUM_CHUNK
    n_chunks = n_tok // ct
    half = table.shape[1]
    lanes = plsc.get_sparse_core_info().num_lanes
    n_cores, n_workers = _sc_workers()
    per_w = n_chunks // n_workers
    assert n_chunks % n_workers == 0 and per_w % 2 == 0 and half % lanes == 0

    def body(table_hbm, idx_hbm, gate_hbm, out_hbm, idx_v, gate_v, buf, out_v, gsem, wsem):
        wid = lax.axis_index("s") * n_cores + lax.axis_index("c")
        c0 = wid * per_w
        pltpu.sync_copy(idx_hbm.at[:, pl.ds(c0 * ct, per_w * ct)], idx_v)
        pltpu.sync_copy(gate_hbm.at[:, pl.ds(c0 * ct, per_w * ct)], gate_v)
        shift = jnp.full((lanes,), 16, I32)
        hi_mask = jnp.full((lanes,), -65536, I32)

        def gather(c, b, k):
            return pltpu.make_async_copy(table_hbm.at[idx_v.at[k, pl.ds(c * ct, ct)]], buf.at[b, k], gsem.at[b])

        def write(c, b):
            return pltpu.make_async_copy(out_v.at[b], out_hbm.at[pl.ds((c0 + c) * ct, ct)], wsem.at[b])

        for k in range(n_src):
            gather(0, 0, k).start()

        @pl.loop(0, per_w, step=2)
        def _(c):
            for b in range(2):
                cc = c + b
                for k in range(n_src):
                    gather(cc, b, k).wait()

                @pl.when(cc + 1 < per_w)
                def _():
                    for k in range(n_src):
                        gather(cc + 1, 1 - b, k).start()

                @pl.when(cc >= 2)
                def _():
                    write(cc - 2, b).wait()

                @pl.loop(0, ct)
                def _(t):
                    tok = jnp.full((lanes,), cc * ct + t, I32)
                    g = [plsc.load_gather(gate_v, [jnp.full((lanes,), k, I32), tok]) for k in range(n_src)]

                    @plsc.parallel_loop(0, half, step=lanes, unroll=2)
                    def _(col):
                        lo = jnp.zeros((lanes,), F32)
                        hi = jnp.zeros((lanes,), F32)
                        for k in range(n_src):
                            w = buf[b, k, t, pl.ds(col, lanes)]
                            lo = lo + g[k] * plsc.bitcast(lax.shift_left(w, shift), F32)
                            hi = hi + g[k] * plsc.bitcast(w & hi_mask, F32)
                        out_v[b, t, pl.ds(col, lanes)] = lo
                        out_v[b, t, pl.ds(half + col, lanes)] = hi

                write(cc, b).start()

        write(per_w - 2, 0).wait()
        write(per_w - 1, 1).wait()

    mesh = plsc.VectorSubcoreMesh(core_axis_name="c", subcore_axis_name="s")
    return pl.kernel(
        body, out_type=jax.ShapeDtypeStruct((n_chunks * ct, 2 * half), F32), mesh=mesh,
        scratch_types=[pltpu.VMEM((n_src, per_w * ct), I32), pltpu.VMEM((n_src, per_w * ct), F32),
                       pltpu.VMEM((2, n_src, ct, half), I32), pltpu.VMEM((2, ct, 2 * half), F32),
                       pltpu.SemaphoreType.DMA((2,)), pltpu.SemaphoreType.DMA((2,))],
        compiler_params=pltpu.CompilerParams(needs_layout_passes=False),
        name="moe_combine_gather_sum",
    )(table, idx, gates)


def _sc_pack_row_pairs(w):
    R, C = w.shape
    lanes = plsc.get_sparse_core_info().num_lanes
    cr = SC_PACK_CHUNK_WORDS // C
    n_chunks = R // cr
    n_cores, n_workers = _sc_workers()
    per_w = n_chunks // n_workers
    assert R % cr == 0 and n_chunks % n_workers == 0 and per_w % 2 == 0 and cr % 2 == 0 and C % lanes == 0

    def body(w_hbm, out_hbm, in_v, out_v, lsem, ssem):
        wid = lax.axis_index("s") * n_cores + lax.axis_index("c")
        c0 = wid * per_w

        def load(c, b):
            return pltpu.make_async_copy(w_hbm.at[pl.ds((c0 + c) * cr, cr)], in_v.at[b], lsem.at[b])

        def store(c, b):
            return pltpu.make_async_copy(out_v.at[b], out_hbm.at[pl.ds((c0 + c) * (cr // 2), cr // 2)],
                                         ssem.at[b])

        load(0, 0).start()

        @pl.loop(0, per_w, step=2)
        def _(c):
            for b in range(2):
                cc = c + b
                load(cc, b).wait()

                @pl.when(cc + 1 < per_w)
                def _():
                    load(cc + 1, 1 - b).start()

                @pl.when(cc >= 2)
                def _():
                    store(cc - 2, b).wait()

                @pl.loop(0, cr // 2)
                def _(i):
                    @plsc.parallel_loop(0, C, step=lanes, unroll=4)
                    def _(col):
                        even = in_v[b, 2 * i, pl.ds(col, lanes)]
                        odd = in_v[b, 2 * i + 1, pl.ds(col, lanes)]
                        pair = plsc.pack(even, odd, format=plsc.PackFormat.INTERLEAVED)
                        out_v[b, i, pl.ds(col, lanes)] = plsc.bitcast(pair, I32)

                store(cc, b).start()

        store(per_w - 2, 0).wait()
        store(per_w - 1, 1).wait()

    mesh = plsc.VectorSubcoreMesh(core_axis_name="c", subcore_axis_name="s")
    return pl.kernel(
        body, out_type=jax.ShapeDtypeStruct((R // 2, C), I32), mesh=mesh,
        scratch_types=[pltpu.VMEM((2, cr, C), F32), pltpu.VMEM((2, cr // 2, C), I32),
                       pltpu.SemaphoreType.DMA((2,)), pltpu.SemaphoreType.DMA((2,))],
        compiler_params=pltpu.CompilerParams(needs_layout_passes=False),
        name="expert_weights_bf16",
    )(w)


def _expert_kernel(be_ref, ue_ref, nu_ref, xs_hbm, wg_hbm, wu_hbm, wd_hbm, y_hbm,
                   xbuf, ybuf, wg_v, wu_v, wd_v, xsem, ysem, wsem):
    n_used = nu_ref[0]
    n_exp = nu_ref[1]
    blk = MOE_BLOCK
    rows_gu = D_MODEL // 2
    rows_d = EXPERT_DIM // 2

    def rows(j):
        return pl.ds(pl.multiple_of(j * blk, blk), blk)

    def x_copy(j, p):
        return pltpu.make_async_copy(xs_hbm.at[rows(j)], xbuf.at[p], xsem.at[p])

    def y_copy(j, p):
        return pltpu.make_async_copy(ybuf.at[p], y_hbm.at[rows(j)], ysem.at[p])

    def w_copies(q, s):
        e = ue_ref[q]
        gu = pl.ds(pl.multiple_of(e * rows_gu, rows_gu), rows_gu)
        dn = pl.ds(pl.multiple_of(e * rows_d, rows_d), rows_d)
        return (pltpu.make_async_copy(wg_hbm.at[gu], wg_v.at[s], wsem.at[s, 0]),
                pltpu.make_async_copy(wu_hbm.at[gu], wu_v.at[s], wsem.at[s, 1]),
                pltpu.make_async_copy(wd_hbm.at[dn], wd_v.at[s], wsem.at[s, 2]))

    for q0 in range(W_SLOTS - 1):
        @pl.when(q0 < n_exp)
        def _():
            for c in w_copies(q0, q0):
                c.start()

    for j0 in range(X_SLOTS):
        @pl.when(j0 < n_used)
        def _():
            x_copy(j0, j0).start()

    def block_step(j, p, q):
        is_new = (j == 0) | (be_ref[j] != be_ref[jnp.maximum(j - 1, 0)])
        q = q + is_new.astype(I32)

        s = q % W_SLOTS

        @pl.when(is_new)
        def _():
            for c in w_copies(q, s):
                c.wait()

            @pl.when(q + W_SLOTS - 1 < n_exp)
            def _():
                for c in w_copies(q + W_SLOTS - 1, (q + W_SLOTS - 1) % W_SLOTS):
                    c.start()

        x_copy(j, p).wait()

        @pl.when(j >= X_SLOTS)
        def _():
            y_copy(j - X_SLOTS, p).wait()

        wg = pltpu.bitcast(wg_v[s], BF16)
        wu = pltpu.bitcast(wu_v[s], BF16)
        wd = pltpu.bitcast(wd_v[s], BF16)
        lo, hi = _unpack_bf16_pairs(xbuf[p])
        x = jnp.concatenate([lo.astype(BF16), hi.astype(BF16)], axis=1)
        g = jnp.dot(x, wg, preferred_element_type=F32)
        u = jnp.dot(x, wu, preferred_element_type=F32)
        hb = (g * jax.nn.sigmoid(g) * u).astype(BF16)
        ybuf[p] = _pack_bf16_pairs(jnp.dot(hb, wd, preferred_element_type=F32))
        y_copy(j, p).start()

        @pl.when(j + X_SLOTS < n_used)
        def _():
            x_copy(j + X_SLOTS, p).start()

        return q

    def group(m, q):
        q = block_step(X_SLOTS * m, 0, q)
        for p in range(1, X_SLOTS):
            j = X_SLOTS * m + p
            q = lax.cond(j < n_used, functools.partial(block_step, j, p), lambda q: q, q)
        return q

    lax.fori_loop(0, (n_used + X_SLOTS - 1) // X_SLOTS, group, jnp.int32(-1))

    for back in range(X_SLOTS, 0, -1):
        b = n_used - back
        for p in range(X_SLOTS):
            @pl.when((b >= 0) & (b % X_SLOTS == p))
            def _():
                y_copy(b, p).wait()


def _experts(block_e, used_e, counts2, xs, wg, wu, wd):
    n_rows, half = xs.shape
    smem = pl.BlockSpec(memory_space=pltpu.SMEM)
    hbm = pl.BlockSpec(memory_space=pl.ANY)
    return pl.pallas_call(
        _expert_kernel,
        in_specs=[smem, smem, smem, hbm, hbm, hbm, hbm],
        out_specs=hbm,
        out_shape=jax.ShapeDtypeStruct((n_rows, half), I32),
        scratch_shapes=[pltpu.VMEM((X_SLOTS, MOE_BLOCK, half), I32), pltpu.VMEM((X_SLOTS, MOE_BLOCK, half), I32),
                        pltpu.VMEM((W_SLOTS, D_MODEL // 2, EXPERT_DIM), I32),
                        pltpu.VMEM((W_SLOTS, D_MODEL // 2, EXPERT_DIM), I32),
                        pltpu.VMEM((W_SLOTS, EXPERT_DIM // 2, D_MODEL), I32),
                        pltpu.SemaphoreType.DMA((X_SLOTS,)), pltpu.SemaphoreType.DMA((X_SLOTS,)),
                        pltpu.SemaphoreType.DMA((W_SLOTS, 3))],
        compiler_params=pltpu.CompilerParams(vmem_limit_bytes=VMEM_LIMIT),
        name="moe_experts",
    )(block_e, used_e, counts2, xs, wg, wu, wd)


def _combine_kernel(routed_ref, x1_ref, swgu_ref, swd_ref, g2_ref, b2_ref, *rest, alpha):
    out_ref = rest[-1]
    x1 = x1_ref[...]
    gu = jnp.dot(x1.astype(BF16), swgu_ref[...], preferred_element_type=F32)
    g = gu[:, :EXPERT_DIM]
    u = gu[:, EXPERT_DIM:]
    shared = jnp.dot((g * jax.nn.sigmoid(g) * u).astype(BF16), swd_ref[...], preferred_element_type=F32)
    out_ref[...] = _layer_norm(alpha * x1 + (routed_ref[...] + shared), g2_ref[...], b2_ref[...])


def _combine(routed, x1, tok0, swgu, swd, g2, b2, alpha, out_prev):
    T, D = x1.shape
    n_tok = routed.shape[0]
    tc = COMBINE_TILE
    first = tok0 // tc
    consts = (swgu, swd, g2, b2)
    args = [routed, x1, *consts]
    in_specs = [pl.BlockSpec((tc, D), lambda i: (i, 0)),
                pl.BlockSpec((tc, D), lambda i: (first + i, 0))] + [_const_spec(c.shape) for c in consts]
    aliases = {}
    if out_prev is not None:
        aliases = {len(args): 0}
        args.append(out_prev)
        in_specs.append(pl.BlockSpec(memory_space=pl.ANY))
    return pl.pallas_call(
        functools.partial(_combine_kernel, alpha=alpha),
        grid=(n_tok // tc,),
        in_specs=in_specs,
        out_specs=pl.BlockSpec((tc, D), lambda i: (first + i, 0)),
        out_shape=jax.ShapeDtypeStruct((T, D), F32),
        input_output_aliases=aliases,
        compiler_params=pltpu.CompilerParams(dimension_semantics=("arbitrary",), vmem_limit_bytes=VMEM_LIMIT),
        name="moe_combine_norm",
    )(*args)


def _mixer(x, in_w, in_b, ng, nb, spatial_w, spatial_b, proj_a_w, proj_b_w, out_w, ln1_g, ln1_b,
           router_w, alpha, anchors):
    gw, aw, D = GMLP_WIDTH, ATTN_WIDTH, D_MODEL
    w = in_w.astype(BF16)
    q0 = 2 * gw
    wuv, buv = w[:, :q0], in_b[None, :q0]
    watt = jnp.stack([jnp.concatenate([w[:, q0 + s * aw + p * GROUP_WIDTH:q0 + s * aw + (p + 1) * GROUP_WIDTH]
                                       for s in range(3)], axis=1) for p in range(len(DILATED_PATTERNS))])
    batt = jnp.stack([jnp.concatenate([in_b[q0 + s * aw + p * GROUP_WIDTH:q0 + s * aw + (p + 1) * GROUP_WIDTH]
                                       for s in range(3)])[None] for p in range(len(DILATED_PATTERNS))])
    g0 = q0 + 3 * aw
    wg, bg = w[:, g0:], in_b[None, g0:]
    sb = jnp.repeat(spatial_b.T, gw // GMLP_GROUPS, axis=1)
    ya, gb, a1, a4, a16 = _input_projection(
        x, wuv, buv, watt, batt, wg, bg, ng[None], nb[None], spatial_w, sb, proj_a_w.astype(BF16))
    attn_outs = [_dilated_attention(a, p, d) for p, (a, (_, d)) in enumerate(zip((a1, a4, a16), DILATED_PATTERNS))]
    rw_t = router_w.T
    rwh = rw_t.astype(BF16)
    rwl = (rw_t - rwh.astype(F32)).astype(BF16)
    return _merge_and_norm(attn_outs, ya, gb, x, proj_b_w.astype(BF16), out_w.astype(BF16),
                           ln1_g[None], ln1_b[None], rwh, rwl, alpha, anchors)


def _moe(x1, x1p, scores_t, router_bias, w_gate, w_up, w_down, sw_gate, sw_up, sw_down, ln2_g, ln2_b, alpha):
    swgu = jnp.concatenate([sw_gate, sw_up], axis=1).astype(BF16)
    swd = sw_down.astype(BF16)
    out = None
    tok0 = 0
    for part, eighths in enumerate(MOE_TOKEN_SPLIT):
        n_tok = x1.shape[0] * eighths // 8
        n_blocks = (n_tok * TOP_K + N_EXPERTS * (MOE_BLOCK - 1)) // MOE_BLOCK
        eidx_t, rank_t, gate_t, counts = _route(scores_t, router_bias[:, None], tok0, n_tok)
        counts = counts[:, 0].astype(I32)
        padded = (counts + MOE_BLOCK - 1) // MOE_BLOCK * MOE_BLOCK
        pend = jnp.cumsum(padded).astype(I32)
        pstart = pend - padded
        block_starts = jnp.arange(n_blocks, dtype=I32) * MOE_BLOCK
        block_e = jnp.minimum(jnp.sum((pend[None, :] <= block_starts[:, None]).astype(I32), axis=1),
                              N_EXPERTS - 1)
        used = counts > 0
        used_e = jnp.argsort(jnp.logical_not(used), stable=True).astype(I32)
        counts2 = jnp.stack([pend[-1] // MOE_BLOCK, jnp.sum(used.astype(I32))]).astype(I32)
        dest_t = _dest_rows(eidx_t, rank_t, pstart)
        xs = _sc_scatter_rows(x1p, tok0, dest_t.reshape(TOP_K, n_tok // SC_CHUNK, SC_CHUNK), n_blocks * MOE_BLOCK)
        y_rows = _experts(block_e, used_e, counts2, xs, w_gate, w_up, w_down)
        pieces = MOE_TAIL_PIECES if part == len(MOE_TOKEN_SPLIT) - 1 else 1
        n_piece = n_tok // pieces
        for q in range(pieces):
            cols = slice(q * n_piece, (q + 1) * n_piece)
            routed = _sc_gather_weighted_sum(y_rows, dest_t[:, cols], gate_t[:, cols])
            out = _combine(routed, x1, tok0 + q * n_piece, swgu, swd, ln2_g[None], ln2_b[None], alpha, out)
        tok0 += n_tok
    return out


def kernel(x, in_w, in_b, gmlp_norm_g, gmlp_norm_b, spatial_w, spatial_b, proj_a_w, proj_b_w, out_w,
           ln1_g, ln1_b, router_w, router_bias, expert_w_gate, expert_w_up, expert_w_down,
           shared_w_gate, shared_w_up, shared_w_down, ln2_g, ln2_b):
    B, S, D = x.shape
    depth = in_w.shape[0]
    alpha = np.float32((2.0 * depth) ** 0.25)
    for l in range(depth):
        packed = [_sc_pack_row_pairs(w[l].reshape(-1, w.shape[-1]))
                  for w in (expert_w_gate, expert_w_up, expert_w_down)]
        x1, x1p, scores_t = _mixer(x, in_w[l], in_b[l], gmlp_norm_g[l], gmlp_norm_b[l], spatial_w[l],
                                 spatial_b[l], proj_a_w[l], proj_b_w[l], out_w[l], ln1_g[l], ln1_b[l],
                                 router_w[l], alpha, [p[:8] for p in packed])
        out = _moe(x1.reshape(B * S, D), x1p.reshape(B * S, D // 2), scores_t,
                   router_bias[l], *packed,
                   shared_w_gate[l], shared_w_up[l], shared_w_down[l], ln2_g[l], ln2_b[l], alpha)
        x = out.reshape(B, S, D)
    return x
```

```python
import functools
import math

import numpy as np
import jax
import jax.numpy as jnp
from jax import lax
from jax.experimental import pallas as pl
from jax.experimental.pallas import tpu as pltpu
from jax.experimental.pallas import tpu_sc as plsc

F32 = jnp.float32
BF16 = jnp.bfloat16
U32 = jnp.uint32
I32 = jnp.int32

D_MODEL = 1024
GMLP_WIDTH = 1024
GMLP_GROUPS = 8
GMLP_CHUNK = 128
HEAD_DIM = 64
DILATED_PATTERNS = ((128, 1), (512, 4), (2048, 16))
HEADS_PER_GROUP = 4
GROUP_WIDTH = HEADS_PER_GROUP * HEAD_DIM
ATTN_WIDTH = GROUP_WIDTH * len(DILATED_PATTERNS)
ATTN_BLOCK = 128
N_EXPERTS = 256
TOP_K = 8
N_EXPERT_GROUPS = 8
TOPK_GROUPS = 4
EXPERT_DIM = 256
ROUTED_SCALE = 2.5
LN_EPS = 1e-5
LANES = 128
MASKED_SCORE = -1e30

PROJ_TILE = 512
ATTN_QBLOCKS = 16
MERGE_TILE = 512
ROUTE_TILE = 512
DEST_TILE = 2048
MOE_BLOCK = 256
SC_PACK_CHUNK_WORDS = 16384
SC_CHUNK = 64
SC_SUM_CHUNK = 8
MOE_TOKEN_SPLIT = (4, 4)
MOE_TAIL_PIECES = 4
X_SLOTS = 4
W_SLOTS = 3
COMBINE_TILE = 512
VMEM_LIMIT = 56 * 1024 * 1024


def _layer_norm(y, g, b):
    mu = jnp.mean(y, axis=-1, keepdims=True)
    yc = y - mu
    var = jnp.mean(yc * yc, axis=-1, keepdims=True)
    return yc * lax.rsqrt(var + LN_EPS) * g + b


def _gelu(x):
    return 0.5 * x * (1.0 + lax.erf(x * np.float32(math.sqrt(0.5))))


def _pack_bf16_pairs(x):
    w = x.shape[1] // 2
    bits = pltpu.bitcast(x.astype(BF16).astype(F32), U32)
    return pltpu.bitcast((bits[:, :w] >> 16) | (bits[:, w:] & jnp.uint32(0xFFFF0000)), I32)


def _unpack_bf16_pairs(words):
    w = pltpu.bitcast(words, U32)
    lo = pltpu.bitcast(w << 16, F32)
    hi = pltpu.bitcast(w & jnp.uint32(0xFFFF0000), F32)
    return lo, hi


def _const_spec(shape):
    nd = len(shape)
    return pl.BlockSpec(shape, lambda *_: (0,) * nd)


def _proj_kernel(x_ref, wuv_ref, buv_ref, watt_ref, batt_ref, wg_ref, bg_ref, ng_ref, nb_ref,
                 sw_ref, sb_ref, pa_ref, ya_ref, gb_ref, a1_ref, a4_ref, a16_ref, xc_ref):
    tm = x_ref.shape[1]
    gw = GMLP_WIDTH
    xb = x_ref[0].astype(BF16)

    def proj(w, b):
        return jnp.dot(xb, w, preferred_element_type=F32) + b

    h_v = proj(wuv_ref[:, gw:], buv_ref[:, gw:])
    h_u = proj(wuv_ref[:, :gw], buv_ref[:, :gw])
    v = _layer_norm(_gelu(h_v), ng_ref[...], nb_ref[...]).astype(BF16)
    h_ga = proj(wg_ref[:, :D_MODEL], bg_ref[:, :D_MODEL])
    u = _gelu(h_u)

    cw = gw // GMLP_GROUPS
    row = lax.broadcasted_iota(I32, (GMLP_CHUNK, GMLP_CHUNK), 0)
    col = lax.broadcasted_iota(I32, (GMLP_CHUNK, GMLP_CHUNK), 1)
    ws = [jnp.where(row >= col, sw_ref[g], 0.0).astype(BF16) for g in range(GMLP_GROUPS)]
    chunks = []
    for c in range(tm // GMLP_CHUNK):
        vc = v[c * GMLP_CHUNK:(c + 1) * GMLP_CHUNK]
        cols = [jnp.dot(ws[g], vc[:, g * cw:(g + 1) * cw], preferred_element_type=F32)
                for g in range(GMLP_GROUPS)]
        chunks.append(jnp.concatenate(cols, axis=1) + sb_ref[...])
    vmix = jnp.concatenate(chunks, axis=0)
    h_gb = proj(wg_ref[:, D_MODEL:], bg_ref[:, D_MODEL:])
    ga = jax.nn.sigmoid(h_ga)
    ya = jnp.dot((u * vmix).astype(BF16), pa_ref[...], preferred_element_type=F32)
    gb_ref[0] = jax.nn.sigmoid(h_gb).astype(BF16)

    n_chunks = x_ref.shape[2] // LANES
    for c in range(n_chunks):
        xc_ref[c] = x_ref[0, :, c * LANES:(c + 1) * LANES]

    def attn_proj(p, d):
        n = tm // d
        if d == 1:
            xp = xb
        else:
            xp = jnp.concatenate(
                [jnp.concatenate([xc_ref[c, pl.ds(r, n, stride=d), :] for c in range(n_chunks)], axis=1)
                 for r in range(d)], axis=0).astype(BF16)
        return jnp.dot(xp, watt_ref[p], preferred_element_type=F32)

    def attn_store(p, d, h, a_ref):
        n = tm // d
        h = (h + batt_ref[p]).astype(BF16)
        for r in range(d):
            a_ref[0, r] = h[r * n:(r + 1) * n]

    a_refs = (a1_ref, a4_ref, a16_ref)
    dils = [d for _, d in DILATED_PATTERNS]
    h_prev = attn_proj(0, dils[0])
    ya_ref[0] = (ga * ya).astype(BF16)
    for p in range(1, len(dils)):
        h_next = attn_proj(p, dils[p])
        attn_store(p - 1, dils[p - 1], h_prev, a_refs[p - 1])
        h_prev = h_next
    attn_store(len(dils) - 1, dils[-1], h_prev, a_refs[-1])


def _input_projection(x, wuv, buv, watt, batt, wg, bg, ng, nb, sw, sb, pa):
    B, S, D = x.shape
    tm = PROJ_TILE
    grid = (B, S // tm)
    out_shape = [jax.ShapeDtypeStruct((B, S, D), BF16), jax.ShapeDtypeStruct((B, S, D), BF16)]
    out_specs = [pl.BlockSpec((1, tm, D), lambda b, t: (b, t, 0)),
                 pl.BlockSpec((1, tm, D), lambda b, t: (b, t, 0))]
    for _, d in DILATED_PATTERNS:
        out_shape.append(jax.ShapeDtypeStruct((B, d, S // d, ATTN_WIDTH), BF16))
        out_specs.append(pl.BlockSpec((1, d, tm // d, ATTN_WIDTH), lambda b, t: (b, 0, t, 0)))
    consts = (wuv, buv, watt, batt, wg, bg, ng, nb, sw, sb, pa)
    return pl.pallas_call(
        _proj_kernel,
        grid=grid,
        in_specs=[pl.BlockSpec((1, tm, D), lambda b, t: (b, t, 0))] + [_const_spec(c.shape) for c in consts],
        out_specs=out_specs,
        out_shape=out_shape,
        scratch_shapes=[pltpu.VMEM((D // LANES, tm, LANES), F32)],
        compiler_params=pltpu.CompilerParams(
            dimension_semantics=("arbitrary", "arbitrary"), vmem_limit_bytes=VMEM_LIMIT),
        name="input_projection",
    )(x, *consts)


def _attn_kernel(qkv_ref, bias_ref, o_ref, lse_ref, *, qblocks, rblock):
    nq = pl.program_id(2)
    gwid = GROUP_WIDTH
    blk = ATTN_BLOCK
    lane = lax.broadcasted_iota(I32, (1, gwid), 1)
    head_masks = [(lane >= h * HEAD_DIM) & (lane < (h + 1) * HEAD_DIM) for h in range(HEADS_PER_GROUP)]
    q_scales = [jnp.where(m, np.float32(HEAD_DIM ** -0.5), 0.0).astype(BF16) for m in head_masks]

    def rows_of(j):
        n = nq * qblocks + j
        return n, pl.multiple_of(n * blk, blk), pl.multiple_of(jnp.maximum(n - 1, 0) * blk, blk)

    def scores(unit):
        ri, j = unit
        n, q0, p0 = rows_of(j)
        q = qkv_ref[0, ri, pl.ds(q0, blk), 0:gwid]
        kk = jnp.concatenate([qkv_ref[0, ri, pl.ds(p0, blk), gwid:2 * gwid],
                              qkv_ref[0, ri, pl.ds(q0, blk), gwid:2 * gwid]], axis=0)
        qs = jnp.concatenate([q * s for s in q_scales], axis=0)
        s = lax.dot_general(qs, kk, (((1,), (1,)), ((), ())), preferred_element_type=F32)
        return s + bias_ref[jnp.where(n == 0, 1, 0)]

    def finish(unit, s):
        ri, j = unit
        _, q0, p0 = rows_of(j)
        vv = jnp.concatenate([qkv_ref[0, ri, pl.ds(p0, blk), 2 * gwid:3 * gwid],
                              qkv_ref[0, ri, pl.ds(q0, blk), 2 * gwid:3 * gwid]], axis=0)
        m = jnp.max(s, axis=1, keepdims=True)
        p = jnp.exp(s - m)
        den = jnp.sum(p, axis=1, keepdims=True)
        pv = jnp.dot(p.astype(BF16), vv, preferred_element_type=F32)
        on = pv / den
        lse = m + jnp.log(den)
        o = jnp.zeros((blk, gwid), F32)
        l = jnp.zeros((blk, gwid), F32)
        for h in range(HEADS_PER_GROUP):
            o = jnp.where(head_masks[h], on[h * blk:(h + 1) * blk], o)
            l = jnp.where(head_masks[h], lse[h * blk:(h + 1) * blk], l)
        o_ref[0, ri, j * blk:(j + 1) * blk, :] = o.astype(BF16)
        lse_ref[0, ri, j * blk:(j + 1) * blk, :] = l

    units = [(ri, j) for ri in range(rblock) for j in range(qblocks)]
    s_next = scores(units[0])
    for i, unit in enumerate(units):
        s_cur = s_next
        if i + 1 < len(units):
            s_next = scores(units[i + 1])
        finish(unit, s_cur)


def _alibi_slopes(n):
    def pow2_slopes(m):
        start = 2.0 ** (-8.0 / m)
        return [start ** (i + 1) for i in range(m)]
    p = 2 ** int(math.floor(math.log2(n)))
    s = pow2_slopes(p)
    if p < n:
        s = s + pow2_slopes(2 * p)[0::2][: n - p]
    return np.array(sorted(s, reverse=True), dtype=np.float32)


def _attn_bias_tables(group, dilation):
    blk = ATTN_BLOCK
    slopes = _alibi_slopes(HEADS_PER_GROUP * len(DILATED_PATTERNS))
    slopes = slopes[group * HEADS_PER_GROUP:(group + 1) * HEADS_PER_GROUP]
    qi = np.arange(blk)[:, None]
    ki = np.arange(2 * blk)[None, :]
    delta = blk + qi - ki
    band = (delta >= 0) & (delta <= blk)
    bias = -slopes[:, None, None] * (delta * dilation).astype(np.float32)[None]
    full = np.where(band[None], bias, np.float32(MASKED_SCORE)).astype(np.float32)
    first = np.where((ki >= blk)[None], full, np.float32(MASKED_SCORE)).astype(np.float32)
    return full.reshape(HEADS_PER_GROUP * blk, 2 * blk), first.reshape(HEADS_PER_GROUP * blk, 2 * blk)


def _dilated_attention(qkv, group, dilation):
    B, d, sd, _ = qkv.shape
    qblocks = min(ATTN_QBLOCKS, sd // ATTN_BLOCK)
    rblock = min(d, ATTN_QBLOCKS // qblocks)
    rows = qblocks * ATTN_BLOCK
    bias = np.stack(_attn_bias_tables(group, dilation))
    grid = (B, d // rblock, sd // rows)
    out_spec = pl.BlockSpec((1, rblock, rows, GROUP_WIDTH), lambda b, r, n: (b, r, n, 0))
    return pl.pallas_call(
        functools.partial(_attn_kernel, qblocks=qblocks, rblock=rblock),
        grid=grid,
        in_specs=[pl.BlockSpec((1, rblock, sd, ATTN_WIDTH), lambda b, r, n: (b, r, 0, 0)),
                  _const_spec(bias.shape)],
        out_specs=[out_spec, out_spec],
        out_shape=[jax.ShapeDtypeStruct((B, d, sd, GROUP_WIDTH), BF16),
                   jax.ShapeDtypeStruct((B, d, sd, GROUP_WIDTH), F32)],
        compiler_params=pltpu.CompilerParams(
            dimension_semantics=("arbitrary", "arbitrary", "arbitrary"), vmem_limit_bytes=VMEM_LIMIT),
        name=f"dilated_attention_d{dilation}",
    )(qkv, jnp.asarray(bias))


def _merge_kernel(o1_ref, l1_ref, o4_ref, l4_ref, o16_ref, l16_ref, ya_ref, gb_ref, x_ref,
                  pb_ref, ow_ref, g1_ref, b1_ref, rwh_ref, rwl_ref, *rest, alpha, n_anchors):
    x1_ref, x1p_ref, sc_ref, so4, sl4, so16, sl16 = rest[n_anchors:]
    tm = x_ref.shape[1]
    n_chunks = GROUP_WIDTH // LANES
    for (o_ref, l_ref, so, sl, d) in ((o4_ref, l4_ref, so4, sl4, 4), (o16_ref, l16_ref, so16, sl16, 16)):
        n = tm // d
        for r in range(d):
            o_r = o_ref[0, r].astype(F32)
            l_r = l_ref[0, r]
            for c in range(n_chunks):
                so[c, pl.ds(r, n, stride=d), :] = o_r[:, c * LANES:(c + 1) * LANES]
                sl[c, pl.ds(r, n, stride=d), :] = l_r[:, c * LANES:(c + 1) * LANES]

    def natural(s):
        return jnp.concatenate([s[c] for c in range(n_chunks)], axis=1)

    l1 = l1_ref[0, 0]
    l4 = natural(sl4)
    l16 = natural(sl16)
    lmax = jnp.maximum(jnp.maximum(l1, l4), l16)
    e1 = jnp.exp(l1 - lmax)
    e4 = jnp.exp(l4 - lmax)
    e16 = jnp.exp(l16 - lmax)
    yb = (e1 * o1_ref[0, 0].astype(F32) + e4 * natural(so4) + e16 * natural(so16)) / (e1 + e4 + e16)
    ybp = jnp.dot(yb.astype(BF16), pb_ref[...], preferred_element_type=F32)
    merged = ya_ref[0] + gb_ref[0] * ybp.astype(BF16)
    mix = jnp.dot(merged, ow_ref[...], preferred_element_type=F32)
    x1 = _layer_norm(alpha * x_ref[0] + mix, g1_ref[...], b1_ref[...])
    x1_ref[0] = x1
    x1p_ref[0] = _pack_bf16_pairs(x1)
    hi = x1.astype(BF16)
    lo = (x1 - hi.astype(F32)).astype(BF16)
    def logits_t(w_ref, xt):
        return lax.dot_general(w_ref[...], xt, (((1,), (1,)), ((), ())), preferred_element_type=F32)
    sc_ref[...] = jax.nn.sigmoid(logits_t(rwh_ref, hi) + logits_t(rwh_ref, lo) + logits_t(rwl_ref, hi))


def _merge_and_norm(attn_outs, ya, gb, x, pb, ow, g1, b1, rwh, rwl, alpha, anchors):
    B, S, D = x.shape
    tm = MERGE_TILE
    in_specs = []
    args = []
    for (o, l), (_, d) in zip(attn_outs, DILATED_PATTERNS):
        spec = pl.BlockSpec((1, d, tm // d, GROUP_WIDTH), lambda b, t: (b, 0, t, 0))
        in_specs += [spec, spec]
        args += [o, l]
    tok_spec = pl.BlockSpec((1, tm, D), lambda b, t: (b, t, 0))
    in_specs += [tok_spec, tok_spec, tok_spec]
    args += [ya, gb, x]
    consts = (pb, ow, g1, b1, rwh, rwl)
    in_specs += [_const_spec(c.shape) for c in consts]
    in_specs += [pl.BlockSpec(memory_space=pl.ANY) for _ in anchors]
    return pl.pallas_call(
        functools.partial(_merge_kernel, alpha=alpha, n_anchors=len(anchors)),
        grid=(B, S // tm),
        in_specs=in_specs,
        out_specs=[tok_spec,
                   pl.BlockSpec((1, tm, D // 2), lambda b, t: (b, t, 0)),
                   pl.BlockSpec((N_EXPERTS, tm), lambda b, t: (0, b * (S // tm) + t))],
        out_shape=[jax.ShapeDtypeStruct((B, S, D), F32),
                   jax.ShapeDtypeStruct((B, S, D // 2), I32),
                   jax.ShapeDtypeStruct((N_EXPERTS, B * S), F32)],
        scratch_shapes=[pltpu.VMEM((GROUP_WIDTH // LANES, tm, LANES), F32) for _ in range(4)],
        compiler_params=pltpu.CompilerParams(
            dimension_semantics=("arbitrary", "arbitrary"), vmem_limit_bytes=VMEM_LIMIT),
        name="merge_norm_router",
    )(*args, *consts, *anchors)


def _sortable_key(x):
    bits = pltpu.bitcast(x, I32)
    return jnp.where(bits < 0, bits ^ jnp.int32(0x7FFFFFFF), bits)


def _route_kernel(sc_ref, bias_ref, before_ref, eidx_ref, rank_ref, gate_ref, cnt_ref, carry_ref):
    ne, tm = sc_ref.shape
    gsize = ne // N_EXPERT_GROUPS
    neg_inf = np.float32(-np.inf)
    removed = jnp.int32(-2 ** 31)

    @pl.when(pl.program_id(0) == 0)
    def _():
        carry_ref[...] = jnp.zeros_like(carry_ref)

    scores = sc_ref[...]
    biased = scores + bias_ref[...]

    gsum = []
    for g in range(N_EXPERT_GROUPS):
        v = biased[g * gsize:(g + 1) * gsize]
        m1 = jnp.max(v, axis=0, keepdims=True)
        n1 = jnp.sum(jnp.where(v == m1, 1.0, 0.0), axis=0, keepdims=True)
        m2 = jnp.max(jnp.where(v < m1, v, neg_inf), axis=0, keepdims=True)
        gsum.append(m1 + jnp.where(n1 >= 2.0, m1, m2))
    gkey = _sortable_key(jnp.concatenate(gsum, axis=0))

    def pick_first_max(keys, ids, n_ids):
        m = jnp.max(keys, axis=0, keepdims=True)
        idx = jnp.min(jnp.where(keys == m, ids, n_ids), axis=0, keepdims=True)
        hit = ids == idx
        return idx, hit, jnp.where(hit, removed, keys)

    gid = lax.broadcasted_iota(I32, (N_EXPERT_GROUPS, tm), 0)
    for _ in range(TOPK_GROUPS):
        _, _, gkey = pick_first_max(gkey, gid, N_EXPERT_GROUPS)
    group_on = gkey == removed

    masked = jnp.concatenate(
        [jnp.where(group_on[g:g + 1], biased[g * gsize:(g + 1) * gsize], neg_inf)
         for g in range(N_EXPERT_GROUPS)], axis=0)
    keys = _sortable_key(masked)
    eid = lax.broadcasted_iota(I32, (ne, tm), 0)
    picks = []
    for _ in range(TOP_K):
        idx, _, keys = pick_first_max(keys, eid, ne)
        picks.append(idx)

    sel = jnp.where(keys == removed, 1.0, 0.0)
    ranks = jnp.dot(sel.astype(BF16), before_ref[...], preferred_element_type=F32) + carry_ref[...]
    carry_ref[...] = carry_ref[...] + jnp.sum(sel, axis=1, keepdims=True)
    cnt_ref[...] = carry_ref[...]

    s_k, r_k = [], []
    for idx in picks:
        hit = eid == idx
        s_k.append(jnp.sum(jnp.where(hit, scores, 0.0), axis=0, keepdims=True))
        r_k.append(jnp.sum(jnp.where(hit, ranks, 0.0), axis=0, keepdims=True))
    total = s_k[0]
    for s in s_k[1:]:
        total = total + s
    eidx_ref[...] = jnp.concatenate(picks, axis=0)
    rank_ref[...] = jnp.concatenate(r_k, axis=0).astype(I32)
    gate_ref[...] = jnp.concatenate([s / total * np.float32(ROUTED_SCALE) for s in s_k], axis=0)


def _route(scores_t, bias, tok0, T):
    ne = scores_t.shape[0]
    tm = ROUTE_TILE
    first = tok0 // tm
    before = jnp.asarray(np.triu(np.ones((tm, tm), np.float32), k=1), BF16)
    out_spec = pl.BlockSpec((TOP_K, tm), lambda i: (0, i))
    return pl.pallas_call(
        _route_kernel,
        grid=(T // tm,),
        in_specs=[pl.BlockSpec((ne, tm), lambda i: (0, first + i)), _const_spec(bias.shape),
                  _const_spec(before.shape)],
        out_specs=[out_spec, out_spec, out_spec, _const_spec((ne, 1))],
        out_shape=[jax.ShapeDtypeStruct((TOP_K, T), I32), jax.ShapeDtypeStruct((TOP_K, T), I32),
                   jax.ShapeDtypeStruct((TOP_K, T), F32), jax.ShapeDtypeStruct((ne, 1), F32)],
        scratch_shapes=[pltpu.VMEM((ne, 1), F32)],
        compiler_params=pltpu.CompilerParams(dimension_semantics=("arbitrary",), vmem_limit_bytes=VMEM_LIMIT),
        name="route_topk",
    )(scores_t, bias, before)


def _dest_kernel(pstart_ref, eidx_ref, rank_ref, out_ref):
    eidx = eidx_ref[...]
    start = jnp.zeros(eidx.shape, I32)
    for e in range(N_EXPERTS):
        start = jnp.where(eidx == e, pstart_ref[e], start)
    out_ref[...] = start + rank_ref[...]


def _dest_rows(eidx_t, rank_t, pstart):
    T = eidx_t.shape[1]
    tm = DEST_TILE
    tok_spec = pl.BlockSpec((TOP_K, tm), lambda i: (0, i))
    return pl.pallas_call(
        _dest_kernel,
        grid=(T // tm,),
        in_specs=[pl.BlockSpec(memory_space=pltpu.SMEM), tok_spec, tok_spec],
        out_specs=tok_spec,
        out_shape=jax.ShapeDtypeStruct((TOP_K, T), I32),
        compiler_params=pltpu.CompilerParams(dimension_semantics=("arbitrary",)),
        name="moe_dest_rows",
    )(pstart, eidx_t, rank_t)


def _sc_workers():
    info = plsc.get_sparse_core_info()
    return info.num_cores, info.num_cores * info.num_subcores


def _sc_scatter_rows(rows, row0, dest, n_out):
    n_dst, n_chunks, ch = dest.shape
    width = rows.shape[1]
    n_cores, n_workers = _sc_workers()
    per_w = n_chunks // n_workers
    assert n_chunks % n_workers == 0 and per_w % 8 == 0 and row0 + n_chunks * ch <= rows.shape[0]

    def body(rows_hbm, dest_hbm, out_hbm, idx_v, buf, lsem, ssem):
        wid = lax.axis_index("s") * n_cores + lax.axis_index("c")
        c0 = wid * per_w
        for k in range(n_dst):
            pltpu.sync_copy(dest_hbm.at[k, pl.ds(pl.multiple_of(c0, 8), per_w)], idx_v.at[k])

        def load(c, b):
            return pltpu.make_async_copy(rows_hbm.at[pl.ds(row0 + (c0 + c) * ch, ch)], buf.at[b], lsem.at[b])

        def scatter(c, b, k):
            return pltpu.make_async_copy(buf.at[b], out_hbm.at[idx_v.at[k, c]], ssem.at[b])

        load(0, 0).start()

        @pl.loop(0, per_w, step=2)
        def _(c):
            for b in range(2):
                cc = c + b
                load(cc, b).wait()

                @pl.when(cc >= 1)
                def _():
                    for k in range(n_dst):
                        scatter(cc - 1, 1 - b, k).wait()

                @pl.when(cc + 1 < per_w)
                def _():
                    load(cc + 1, 1 - b).start()

                for k in range(n_dst):
                    scatter(cc, b, k).start()

        for k in range(n_dst):
            scatter(per_w - 1, 1, k).wait()

    mesh = plsc.VectorSubcoreMesh(core_axis_name="c", subcore_axis_name="s")
    return pl.kernel(
        body, out_type=jax.ShapeDtypeStruct((n_out, width), rows.dtype), mesh=mesh,
        scratch_types=[pltpu.VMEM((n_dst, per_w, ch), I32), pltpu.VMEM((2, ch, width), rows.dtype),
                       pltpu.SemaphoreType.DMA((2,)), pltpu.SemaphoreType.DMA((2,))],
        name="moe_dispatch_scatter",
    )(rows, dest)


def _sc_gather_weighted_sum(table, idx, gates):
    n_src, n_tok = idx.shape
    ct = SC_SUM_CHUNK
    n_chunks = n_tok // ct
    half = table.shape[1]
    lanes = plsc.get_sparse_core_info().num_lanes
    n_cores, n_workers = _sc_workers()
    per_w = n_chunks // n_workers
    assert n_chunks % n_workers == 0 and per_w % 2 == 0 and half % lanes == 0

    def body(table_hbm, idx_hbm, gate_hbm, out_hbm, idx_v, gate_v, buf, out_v, gsem, wsem):
        wid = lax.axis_index("s") * n_cores + lax.axis_index("c")
        c0 = wid * per_w
        pltpu.sync_copy(idx_hbm.at[:, pl.ds(c0 * ct, per_w * ct)], idx_v)
        pltpu.sync_copy(gate_hbm.at[:, pl.ds(c0 * ct, per_w * ct)], gate_v)

        def gather(c, b, k):
            return pltpu.make_async_copy(table_hbm.at[idx_v.at[k, pl.ds(c * ct, ct)]], buf.at[b, k], gsem.at[b])

        def write(c, b):
            return pltpu.make_async_copy(out_v.at[b], out_hbm.at[pl.ds((c0 + c) * ct, ct)], wsem.at[b])

        for k in range(n_src):
            gather(0, 0, k).start()

        @pl.loop(0, per_w, step=2)
        def _(c):
            for b in range(2):
                cc = c + b
                for k in range(n_src):
                    gather(cc, b, k).wait()

                @pl.when(cc + 1 < per_w)
                def _():
                    for k in range(n_src):
                        gather(cc + 1, 1 - b, k).start()

                @pl.when(cc >= 2)
                def _():
                    write(cc - 2, b).wait()

                @pl.loop(0, ct)
                def _(t):
                    tok = jnp.full((lanes,), cc * ct + t, I32)
                    g = [plsc.load_gather(gate_v, [jnp.full((lanes,), k, I32), tok]) for k in range(n_src)]

                    @plsc.parallel_loop(0, half, step=lanes, unroll=2)
                    def _(col):
                        lo = jnp.zeros((lanes,), F32)
                        hi = jnp.zeros((lanes,), F32)
                        for k in range(n_src):
                            w = buf[b, k, t, pl.ds(col, lanes)]
                            w_lo, w_hi = plsc.unpack(plsc.bitcast(w, BF16), format=plsc.PackFormat.INTERLEAVED)
                            lo = lo + g[k] * w_lo
                            hi = hi + g[k] * w_hi
                        out_v[b, t, pl.ds(col, lanes)] = lo
                        out_v[b, t, pl.ds(half + col, lanes)] = hi

                write(cc, b).start()

        write(per_w - 2, 0).wait()
        write(per_w - 1, 1).wait()

    mesh = plsc.VectorSubcoreMesh(core_axis_name="c", subcore_axis_name="s")
    return pl.kernel(
        body, out_type=jax.ShapeDtypeStruct((n_chunks * ct, 2 * half), F32), mesh=mesh,
        scratch_types=[pltpu.VMEM((n_src, per_w * ct), I32), pltpu.VMEM((n_src, per_w * ct), F32),
                       pltpu.VMEM((2, n_src, ct, half), I32), pltpu.VMEM((2, ct, 2 * half), F32),
                       pltpu.SemaphoreType.DMA((2,)), pltpu.SemaphoreType.DMA((2,))],
        compiler_params=pltpu.CompilerParams(needs_layout_passes=False),
        name="moe_combine_gather_sum",
    )(table, idx, gates)


def _sc_pack_row_pairs(w):
    R, C = w.shape
    lanes = plsc.get_sparse_core_info().num_lanes
    cr = SC_PACK_CHUNK_WORDS // C
    n_chunks = R // cr
    n_cores, n_workers = _sc_workers()
    per_w = n_chunks // n_workers
    assert R % cr == 0 and n_chunks % n_workers == 0 and per_w % 2 == 0 and cr % 2 == 0 and C % lanes == 0

    def body(w_hbm, out_hbm, in_v, out_v, lsem, ssem):
        wid = lax.axis_index("s") * n_cores + lax.axis_index("c")
        c0 = wid * per_w

        def load(c, b):
            return pltpu.make_async_copy(w_hbm.at[pl.ds((c0 + c) * cr, cr)], in_v.at[b], lsem.at[b])

        def store(c, b):
            return pltpu.make_async_copy(out_v.at[b], out_hbm.at[pl.ds((c0 + c) * (cr // 2), cr // 2)],
                                         ssem.at[b])

        load(0, 0).start()

        @pl.loop(0, per_w, step=2)
        def _(c):
            for b in range(2):
                cc = c + b
                load(cc, b).wait()

                @pl.when(cc + 1 < per_w)
                def _():
                    load(cc + 1, 1 - b).start()

                @pl.when(cc >= 2)
                def _():
                    store(cc - 2, b).wait()

                @pl.loop(0, cr // 2)
                def _(i):
                    @plsc.parallel_loop(0, C, step=lanes, unroll=4)
                    def _(col):
                        even = in_v[b, 2 * i, pl.ds(col, lanes)]
                        odd = in_v[b, 2 * i + 1, pl.ds(col, lanes)]
                        pair = plsc.pack(even, odd, format=plsc.PackFormat.INTERLEAVED)
                        out_v[b, i, pl.ds(col, lanes)] = plsc.bitcast(pair, I32)

                store(cc, b).start()

        store(per_w - 2, 0).wait()
        store(per_w - 1, 1).wait()

    mesh = plsc.VectorSubcoreMesh(core_axis_name="c", subcore_axis_name="s")
    return pl.kernel(
        body, out_type=jax.ShapeDtypeStruct((R // 2, C), I32), mesh=mesh,
        scratch_types=[pltpu.VMEM((2, cr, C), F32), pltpu.VMEM((2, cr // 2, C), I32),
                       pltpu.SemaphoreType.DMA((2,)), pltpu.SemaphoreType.DMA((2,))],
        compiler_params=pltpu.CompilerParams(needs_layout_passes=False),
        name="expert_weights_bf16",
    )(w)


def _expert_kernel(be_ref, ue_ref, nu_ref, xs_hbm, wg_hbm, wu_hbm, wd_hbm, y_hbm,
                   xbuf, ybuf, wg_v, wu_v, wd_v, xsem, ysem, wsem):
    n_used = nu_ref[0]
    n_exp = nu_ref[1]
    blk = MOE_BLOCK
    rows_gu = D_MODEL // 2
    rows_d = EXPERT_DIM // 2

    def rows(j):
        return pl.ds(pl.multiple_of(j * blk, blk), blk)

    def x_copy(j, p):
        return pltpu.make_async_copy(xs_hbm.at[rows(j)], xbuf.at[p], xsem.at[p])

    def y_copy(j, p):
        return pltpu.make_async_copy(ybuf.at[p], y_hbm.at[rows(j)], ysem.at[p])

    def w_copies(q, s):
        e = ue_ref[q]
        gu = pl.ds(pl.multiple_of(e * rows_gu, rows_gu), rows_gu)
        dn = pl.ds(pl.multiple_of(e * rows_d, rows_d), rows_d)
        return (pltpu.make_async_copy(wg_hbm.at[gu], wg_v.at[s], wsem.at[s, 0]),
                pltpu.make_async_copy(wu_hbm.at[gu], wu_v.at[s], wsem.at[s, 1]),
                pltpu.make_async_copy(wd_hbm.at[dn], wd_v.at[s], wsem.at[s, 2]))

    for q0 in range(W_SLOTS - 1):
        @pl.when(q0 < n_exp)
        def _():
            for c in w_copies(q0, q0):
                c.start()

    for j0 in range(X_SLOTS):
        @pl.when(j0 < n_used)
        def _():
            x_copy(j0, j0).start()

    def block_step(j, p, q):
        is_new = (j == 0) | (be_ref[j] != be_ref[jnp.maximum(j - 1, 0)])
        q = q + is_new.astype(I32)

        s = q % W_SLOTS

        @pl.when(is_new)
        def _():
            for c in w_copies(q, s):
                c.wait()

            @pl.when(q + W_SLOTS - 1 < n_exp)
            def _():
                for c in w_copies(q + W_SLOTS - 1, (q + W_SLOTS - 1) % W_SLOTS):
                    c.start()

        x_copy(j, p).wait()

        @pl.when(j >= X_SLOTS)
        def _():
            y_copy(j - X_SLOTS, p).wait()

        wg = pltpu.bitcast(wg_v[s], BF16)
        wu = pltpu.bitcast(wu_v[s], BF16)
        wd = pltpu.bitcast(wd_v[s], BF16)
        lo, hi = _unpack_bf16_pairs(xbuf[p])
        x = jnp.concatenate([lo.astype(BF16), hi.astype(BF16)], axis=1)
        g = jnp.dot(x, wg, preferred_element_type=F32)
        u = jnp.dot(x, wu, preferred_element_type=F32)
        hb = (g * jax.nn.sigmoid(g) * u).astype(BF16)
        ybuf[p] = _pack_bf16_pairs(jnp.dot(hb, wd, preferred_element_type=F32))
        y_copy(j, p).start()

        @pl.when(j + X_SLOTS < n_used)
        def _():
            x_copy(j + X_SLOTS, p).start()

        return q

    def group(m, q):
        q = block_step(X_SLOTS * m, 0, q)
        for p in range(1, X_SLOTS):
            j = X_SLOTS * m + p
            q = lax.cond(j < n_used, functools.partial(block_step, j, p), lambda q: q, q)
        return q

    lax.fori_loop(0, (n_used + X_SLOTS - 1) // X_SLOTS, group, jnp.int32(-1))

    for back in range(X_SLOTS, 0, -1):
        b = n_used - back
        for p in range(X_SLOTS):
            @pl.when((b >= 0) & (b % X_SLOTS == p))
            def _():
                y_copy(b, p).wait()


def _experts(block_e, used_e, counts2, xs, wg, wu, wd):
    n_rows, half = xs.shape
    smem = pl.BlockSpec(memory_space=pltpu.SMEM)
    hbm = pl.BlockSpec(memory_space=pl.ANY)
    return pl.pallas_call(
        _expert_kernel,
        in_specs=[smem, smem, smem, hbm, hbm, hbm, hbm],
        out_specs=hbm,
        out_shape=jax.ShapeDtypeStruct((n_rows, half), I32),
        scratch_shapes=[pltpu.VMEM((X_SLOTS, MOE_BLOCK, half), I32), pltpu.VMEM((X_SLOTS, MOE_BLOCK, half), I32),
                        pltpu.VMEM((W_SLOTS, D_MODEL // 2, EXPERT_DIM), I32),
                        pltpu.VMEM((W_SLOTS, D_MODEL // 2, EXPERT_DIM), I32),
                        pltpu.VMEM((W_SLOTS, EXPERT_DIM // 2, D_MODEL), I32),
                        pltpu.SemaphoreType.DMA((X_SLOTS,)), pltpu.SemaphoreType.DMA((X_SLOTS,)),
                        pltpu.SemaphoreType.DMA((W_SLOTS, 3))],
        compiler_params=pltpu.CompilerParams(vmem_limit_bytes=VMEM_LIMIT),
        name="moe_experts",
    )(block_e, used_e, counts2, xs, wg, wu, wd)


def _combine_kernel(routed_ref, x1_ref, swgu_ref, swd_ref, g2_ref, b2_ref, *rest, alpha):
    out_ref = rest[-1]
    x1 = x1_ref[...]
    gu = jnp.dot(x1.astype(BF16), swgu_ref[...], preferred_element_type=F32)
    g = gu[:, :EXPERT_DIM]
    u = gu[:, EXPERT_DIM:]
    shared = jnp.dot((g * jax.nn.sigmoid(g) * u).astype(BF16), swd_ref[...], preferred_element_type=F32)
    out_ref[...] = _layer_norm(alpha * x1 + (routed_ref[...] + shared), g2_ref[...], b2_ref[...])


def _combine(routed, x1, tok0, swgu, swd, g2, b2, alpha, out_prev):
    T, D = x1.shape
    n_tok = routed.shape[0]
    tc = COMBINE_TILE
    first = tok0 // tc
    consts = (swgu, swd, g2, b2)
    args = [routed, x1, *consts]
    in_specs = [pl.BlockSpec((tc, D), lambda i: (i, 0)),
                pl.BlockSpec((tc, D), lambda i: (first + i, 0))] + [_const_spec(c.shape) for c in consts]
    aliases = {}
    if out_prev is not None:
        aliases = {len(args): 0}
        args.append(out_prev)
        in_specs.append(pl.BlockSpec(memory_space=pl.ANY))
    return pl.pallas_call(
        functools.partial(_combine_kernel, alpha=alpha),
        grid=(n_tok // tc,),
        in_specs=in_specs,
        out_specs=pl.BlockSpec((tc, D), lambda i: (first + i, 0)),
        out_shape=jax.ShapeDtypeStruct((T, D), F32),
        input_output_aliases=aliases,
        compiler_params=pltpu.CompilerParams(dimension_semantics=("arbitrary",), vmem_limit_bytes=VMEM_LIMIT),
        name="moe_combine_norm",
    )(*args)


def _mixer(x, in_w, in_b, ng, nb, spatial_w, spatial_b, proj_a_w, proj_b_w, out_w, ln1_g, ln1_b,
           router_w, alpha, anchors):
    gw, aw, D = GMLP_WIDTH, ATTN_WIDTH, D_MODEL
    w = in_w.astype(BF16)
    q0 = 2 * gw
    wuv, buv = w[:, :q0], in_b[None, :q0]
    watt = jnp.stack([jnp.concatenate([w[:, q0 + s * aw + p * GROUP_WIDTH:q0 + s * aw + (p + 1) * GROUP_WIDTH]
                                       for s in range(3)], axis=1) for p in range(len(DILATED_PATTERNS))])
    batt = jnp.stack([jnp.concatenate([in_b[q0 + s * aw + p * GROUP_WIDTH:q0 + s * aw + (p + 1) * GROUP_WIDTH]
                                       for s in range(3)])[None] for p in range(len(DILATED_PATTERNS))])
    g0 = q0 + 3 * aw
    wg, bg = w[:, g0:], in_b[None, g0:]
    sb = jnp.repeat(spatial_b.T, gw // GMLP_GROUPS, axis=1)
    ya, gb, a1, a4, a16 = _input_projection(
        x, wuv, buv, watt, batt, wg, bg, ng[None], nb[None], spatial_w, sb, proj_a_w.astype(BF16))
    attn_outs = [_dilated_attention(a, p, d) for p, (a, (_, d)) in enumerate(zip((a1, a4, a16), DILATED_PATTERNS))]
    rw_t = router_w.T
    rwh = rw_t.astype(BF16)
    rwl = (rw_t - rwh.astype(F32)).astype(BF16)
    return _merge_and_norm(attn_outs, ya, gb, x, proj_b_w.astype(BF16), out_w.astype(BF16),
                           ln1_g[None], ln1_b[None], rwh, rwl, alpha, anchors)


def _moe(x1, x1p, scores_t, router_bias, w_gate, w_up, w_down, sw_gate, sw_up, sw_down, ln2_g, ln2_b, alpha):
    swgu = jnp.concatenate([sw_gate, sw_up], axis=1).astype(BF16)
    swd = sw_down.astype(BF16)
    out = None
    tok0 = 0
    for part, eighths in enumerate(MOE_TOKEN_SPLIT):
        n_tok = x1.shape[0] * eighths // 8
        n_blocks = (n_tok * TOP_K + N_EXPERTS * (MOE_BLOCK - 1)) // MOE_BLOCK
        eidx_t, rank_t, gate_t, counts = _route(scores_t, router_bias[:, None], tok0, n_tok)
        counts = counts[:, 0].astype(I32)
        padded = (counts + MOE_BLOCK - 1) // MOE_BLOCK * MOE_BLOCK
        pend = jnp.cumsum(padded).astype(I32)
        pstart = pend - padded
        block_starts = jnp.arange(n_blocks, dtype=I32) * MOE_BLOCK
        block_e = jnp.minimum(jnp.sum((pend[None, :] <= block_starts[:, None]).astype(I32), axis=1),
                              N_EXPERTS - 1)
        used = counts > 0
        used_e = jnp.argsort(jnp.logical_not(used), stable=True).astype(I32)
        counts2 = jnp.stack([pend[-1] // MOE_BLOCK, jnp.sum(used.astype(I32))]).astype(I32)
        dest_t = _dest_rows(eidx_t, rank_t, pstart)
        xs = _sc_scatter_rows(x1p, tok0, dest_t.reshape(TOP_K, n_tok // SC_CHUNK, SC_CHUNK), n_blocks * MOE_BLOCK)
        y_rows = _experts(block_e, used_e, counts2, xs, w_gate, w_up, w_down)
        pieces = MOE_TAIL_PIECES if part == len(MOE_TOKEN_SPLIT) - 1 else 1
        n_piece = n_tok // pieces
        for q in range(pieces):
            cols = slice(q * n_piece, (q + 1) * n_piece)
            routed = _sc_gather_weighted_sum(y_rows, dest_t[:, cols], gate_t[:, cols])
            out = _combine(routed, x1, tok0 + q * n_piece, swgu, swd, ln2_g[None], ln2_b[None], alpha, out)
        tok0 += n_tok
    return out


def kernel(x, in_w, in_b, gmlp_norm_g, gmlp_norm_b, spatial_w, spatial_b, proj_a_w, proj_b_w, out_w,
           ln1_g, ln1_b, router_w, router_bias, expert_w_gate, expert_w_up, expert_w_down,
           shared_w_gate, shared_w_up, shared_w_down, ln2_g, ln2_b):
    B, S, D = x.shape
    depth = in_w.shape[0]
    alpha = np.float32((2.0 * depth) ** 0.25)
    for l in range(depth):
        packed = [_sc_pack_row_pairs(w[l].reshape(-1, w.shape[-1]))
                  for w in (expert_w_gate, expert_w_up, expert_w_down)]
        x1, x1p, scores_t = _mixer(x, in_w[l], in_b[l], gmlp_norm_g[l], gmlp_norm_b[l], spatial_w[l],
                                 spatial_b[l], proj_a_w[l], proj_b_w[l], out_w[l], ln1_g[l], ln1_b[l],
                                 router_w[l], alpha, [p[:8] for p in packed])
        out = _moe(x1.reshape(B * S, D), x1p.reshape(B * S, D // 2), scores_t,
                   router_bias[l], *packed,
                   shared_w_gate[l], shared_w_up[l], shared_w_down[l], ln2_g[l], ln2_b[l], alpha)
        x = out.reshape(B, S, D)
    return x
```

```python
import functools
import math

import numpy as np
import jax
import jax.numpy as jnp
from jax import lax
from jax.experimental import pallas as pl
from jax.experimental.pallas import tpu as pltpu
from jax.experimental.pallas import tpu_sc as plsc

F32 = jnp.float32
BF16 = jnp.bfloat16
U32 = jnp.uint32
I32 = jnp.int32

D_MODEL = 1024
GMLP_WIDTH = 1024
GMLP_GROUPS = 8
GMLP_CHUNK = 128
HEAD_DIM = 64
DILATED_PATTERNS = ((128, 1), (512, 4), (2048, 16))
HEADS_PER_GROUP = 4
GROUP_WIDTH = HEADS_PER_GROUP * HEAD_DIM
ATTN_WIDTH = GROUP_WIDTH * len(DILATED_PATTERNS)
ATTN_BLOCK = 128
N_EXPERTS = 256
TOP_K = 8
N_EXPERT_GROUPS = 8
TOPK_GROUPS = 4
EXPERT_DIM = 256
ROUTED_SCALE = 2.5
LN_EPS = 1e-5
LANES = 128
MASKED_SCORE = -1e30

PROJ_TILE = 512
ATTN_QBLOCKS = 16
MERGE_TILE = 512
ROUTE_TILE = 512
DEST_TILE = 2048
MOE_BLOCK = 256
SC_PACK_CHUNK_WORDS = 16384
SC_CHUNK = 64
SC_SUM_CHUNK = 8
MOE_TOKEN_SPLIT = (4, 4)
MOE_TAIL_PIECES = 4
X_SLOTS = 4
W_SLOTS = 6
COMBINE_TILE = 512
VMEM_LIMIT = 56 * 1024 * 1024


def _layer_norm(y, g, b):
    mu = jnp.mean(y, axis=-1, keepdims=True)
    yc = y - mu
    var = jnp.mean(yc * yc, axis=-1, keepdims=True)
    return yc * lax.rsqrt(var + LN_EPS) * g + b


def _gelu(x):
    return 0.5 * x * (1.0 + lax.erf(x * np.float32(math.sqrt(0.5))))


def _pack_bf16_pairs(x):
    w = x.shape[1] // 2
    bits = pltpu.bitcast(x.astype(BF16).astype(F32), U32)
    return pltpu.bitcast((bits[:, :w] >> 16) | (bits[:, w:] & jnp.uint32(0xFFFF0000)), I32)


def _unpack_bf16_pairs(words):
    w = pltpu.bitcast(words, U32)
    lo = pltpu.bitcast(w << 16, F32)
    hi = pltpu.bitcast(w & jnp.uint32(0xFFFF0000), F32)
    return lo, hi


def _const_spec(shape):
    nd = len(shape)
    return pl.BlockSpec(shape, lambda *_: (0,) * nd)


def _proj_kernel(x_ref, wuv_ref, buv_ref, watt_ref, batt_ref, wg_ref, bg_ref, ng_ref, nb_ref,
                 sw_ref, sb_ref, pa_ref, ya_ref, gb_ref, a1_ref, a4_ref, a16_ref, xc_ref):
    tm = x_ref.shape[1]
    gw = GMLP_WIDTH
    xb = x_ref[0].astype(BF16)

    def proj(w, b):
        return jnp.dot(xb, w, preferred_element_type=F32) + b

    h_v = proj(wuv_ref[:, gw:], buv_ref[:, gw:])
    h_u = proj(wuv_ref[:, :gw], buv_ref[:, :gw])
    v = _layer_norm(_gelu(h_v), ng_ref[...], nb_ref[...]).astype(BF16)
    h_ga = proj(wg_ref[:, :D_MODEL], bg_ref[:, :D_MODEL])
    u = _gelu(h_u)

    cw = gw // GMLP_GROUPS
    row = lax.broadcasted_iota(I32, (GMLP_CHUNK, GMLP_CHUNK), 0)
    col = lax.broadcasted_iota(I32, (GMLP_CHUNK, GMLP_CHUNK), 1)
    ws = [jnp.where(row >= col, sw_ref[g], 0.0).astype(BF16) for g in range(GMLP_GROUPS)]
    chunks = []
    for c in range(tm // GMLP_CHUNK):
        vc = v[c * GMLP_CHUNK:(c + 1) * GMLP_CHUNK]
        cols = [jnp.dot(ws[g], vc[:, g * cw:(g + 1) * cw], preferred_element_type=F32)
                for g in range(GMLP_GROUPS)]
        chunks.append(jnp.concatenate(cols, axis=1) + sb_ref[...])
    vmix = jnp.concatenate(chunks, axis=0)
    h_gb = proj(wg_ref[:, D_MODEL:], bg_ref[:, D_MODEL:])
    ga = jax.nn.sigmoid(h_ga)
    ya = jnp.dot((u * vmix).astype(BF16), pa_ref[...], preferred_element_type=F32)
    gb_ref[0] = jax.nn.sigmoid(h_gb).astype(BF16)

    n_chunks = x_ref.shape[2] // LANES
    for c in range(n_chunks):
        xc_ref[c] = x_ref[0, :, c * LANES:(c + 1) * LANES]

    def attn_proj(p, d):
        n = tm // d
        if d == 1:
            xp = xb
        else:
            xp = jnp.concatenate(
                [jnp.concatenate([xc_ref[c, pl.ds(r, n, stride=d), :] for c in range(n_chunks)], axis=1)
                 for r in range(d)], axis=0).astype(BF16)
        return jnp.dot(xp, watt_ref[p], preferred_element_type=F32)

    def attn_store(p, d, h, a_ref):
        n = tm // d
        h = (h + batt_ref[p]).astype(BF16)
        for r in range(d):
            a_ref[0, r] = h[r * n:(r + 1) * n]

    a_refs = (a1_ref, a4_ref, a16_ref)
    dils = [d for _, d in DILATED_PATTERNS]
    h_prev = attn_proj(0, dils[0])
    ya_ref[0] = (ga * ya).astype(BF16)
    for p in range(1, len(dils)):
        h_next = attn_proj(p, dils[p])
        attn_store(p - 1, dils[p - 1], h_prev, a_refs[p - 1])
        h_prev = h_next
    attn_store(len(dils) - 1, dils[-1], h_prev, a_refs[-1])


def _input_projection(x, wuv, buv, watt, batt, wg, bg, ng, nb, sw, sb, pa):
    B, S, D = x.shape
    tm = PROJ_TILE
    grid = (B, S // tm)
    out_shape = [jax.ShapeDtypeStruct((B, S, D), BF16), jax.ShapeDtypeStruct((B, S, D), BF16)]
    out_specs = [pl.BlockSpec((1, tm, D), lambda b, t: (b, t, 0)),
                 pl.BlockSpec((1, tm, D), lambda b, t: (b, t, 0))]
    for _, d in DILATED_PATTERNS:
        out_shape.append(jax.ShapeDtypeStruct((B, d, S // d, ATTN_WIDTH), BF16))
        out_specs.append(pl.BlockSpec((1, d, tm // d, ATTN_WIDTH), lambda b, t: (b, 0, t, 0)))
    consts = (wuv, buv, watt, batt, wg, bg, ng, nb, sw, sb, pa)
    return pl.pallas_call(
        _proj_kernel,
        grid=grid,
        in_specs=[pl.BlockSpec((1, tm, D), lambda b, t: (b, t, 0))] + [_const_spec(c.shape) for c in consts],
        out_specs=out_specs,
        out_shape=out_shape,
        scratch_shapes=[pltpu.VMEM((D // LANES, tm, LANES), F32)],
        compiler_params=pltpu.CompilerParams(
            dimension_semantics=("arbitrary", "arbitrary"), vmem_limit_bytes=VMEM_LIMIT),
        name="input_projection",
    )(x, *consts)


def _attn_kernel(qkv_ref, bias_ref, o_ref, lse_ref, *, qblocks, rblock):
    nq = pl.program_id(2)
    gwid = GROUP_WIDTH
    blk = ATTN_BLOCK
    lane = lax.broadcasted_iota(I32, (1, gwid), 1)
    head_masks = [(lane >= h * HEAD_DIM) & (lane < (h + 1) * HEAD_DIM) for h in range(HEADS_PER_GROUP)]
    q_scales = [jnp.where(m, np.float32(HEAD_DIM ** -0.5), 0.0).astype(BF16) for m in head_masks]

    def rows_of(j):
        n = nq * qblocks + j
        return n, pl.multiple_of(n * blk, blk), pl.multiple_of(jnp.maximum(n - 1, 0) * blk, blk)

    def scores(unit):
        ri, j = unit
        n, q0, p0 = rows_of(j)
        q = qkv_ref[0, ri, pl.ds(q0, blk), 0:gwid]
        kk = jnp.concatenate([qkv_ref[0, ri, pl.ds(p0, blk), gwid:2 * gwid],
                              qkv_ref[0, ri, pl.ds(q0, blk), gwid:2 * gwid]], axis=0)
        qs = jnp.concatenate([q * s for s in q_scales], axis=0)
        s = lax.dot_general(qs, kk, (((1,), (1,)), ((), ())), preferred_element_type=F32)
        return s + bias_ref[jnp.where(n == 0, 1, 0)]

    def finish(unit, s):
        ri, j = unit
        _, q0, p0 = rows_of(j)
        vv = jnp.concatenate([qkv_ref[0, ri, pl.ds(p0, blk), 2 * gwid:3 * gwid],
                              qkv_ref[0, ri, pl.ds(q0, blk), 2 * gwid:3 * gwid]], axis=0)
        m = jnp.max(s, axis=1, keepdims=True)
        p = jnp.exp(s - m)
        den = jnp.sum(p, axis=1, keepdims=True)
        pv = jnp.dot(p.astype(BF16), vv, preferred_element_type=F32)
        on = pv / den
        lse = m + jnp.log(den)
        o = jnp.zeros((blk, gwid), F32)
        l = jnp.zeros((blk, gwid), F32)
        for h in range(HEADS_PER_GROUP):
            o = jnp.where(head_masks[h], on[h * blk:(h + 1) * blk], o)
            l = jnp.where(head_masks[h], lse[h * blk:(h + 1) * blk], l)
        o_ref[0, ri, j * blk:(j + 1) * blk, :] = o.astype(BF16)
        lse_ref[0, ri, j * blk:(j + 1) * blk, :] = l

    units = [(ri, j) for ri in range(rblock) for j in range(qblocks)]
    s_next = scores(units[0])
    for i, unit in enumerate(units):
        s_cur = s_next
        if i + 1 < len(units):
            s_next = scores(units[i + 1])
        finish(unit, s_cur)


def _alibi_slopes(n):
    def pow2_slopes(m):
        start = 2.0 ** (-8.0 / m)
        return [start ** (i + 1) for i in range(m)]
    p = 2 ** int(math.floor(math.log2(n)))
    s = pow2_slopes(p)
    if p < n:
        s = s + pow2_slopes(2 * p)[0::2][: n - p]
    return np.array(sorted(s, reverse=True), dtype=np.float32)


def _attn_bias_tables(group, dilation):
    blk = ATTN_BLOCK
    slopes = _alibi_slopes(HEADS_PER_GROUP * len(DILATED_PATTERNS))
    slopes = slopes[group * HEADS_PER_GROUP:(group + 1) * HEADS_PER_GROUP]
    qi = np.arange(blk)[:, None]
    ki = np.arange(2 * blk)[None, :]
    delta = blk + qi - ki
    band = (delta >= 0) & (delta <= blk)
    bias = -slopes[:, None, None] * (delta * dilation).astype(np.float32)[None]
    full = np.where(band[None], bias, np.float32(MASKED_SCORE)).astype(np.float32)
    first = np.where((ki >= blk)[None], full, np.float32(MASKED_SCORE)).astype(np.float32)
    return full.reshape(HEADS_PER_GROUP * blk, 2 * blk), first.reshape(HEADS_PER_GROUP * blk, 2 * blk)


def _dilated_attention(qkv, group, dilation):
    B, d, sd, _ = qkv.shape
    qblocks = min(ATTN_QBLOCKS, sd // ATTN_BLOCK)
    rblock = min(d, ATTN_QBLOCKS // qblocks)
    rows = qblocks * ATTN_BLOCK
    bias = np.stack(_attn_bias_tables(group, dilation))
    grid = (B, d // rblock, sd // rows)
    out_spec = pl.BlockSpec((1, rblock, rows, GROUP_WIDTH), lambda b, r, n: (b, r, n, 0))
    return pl.pallas_call(
        functools.partial(_attn_kernel, qblocks=qblocks, rblock=rblock),
        grid=grid,
        in_specs=[pl.BlockSpec((1, rblock, sd, ATTN_WIDTH), lambda b, r, n: (b, r, 0, 0)),
                  _const_spec(bias.shape)],
        out_specs=[out_spec, out_spec],
        out_shape=[jax.ShapeDtypeStruct((B, d, sd, GROUP_WIDTH), BF16),
                   jax.ShapeDtypeStruct((B, d, sd, GROUP_WIDTH), F32)],
        compiler_params=pltpu.CompilerParams(
            dimension_semantics=("arbitrary", "arbitrary", "arbitrary"), vmem_limit_bytes=VMEM_LIMIT),
        name=f"dilated_attention_d{dilation}",
    )(qkv, jnp.asarray(bias))


def _merge_kernel(o1_ref, l1_ref, o4_ref, l4_ref, o16_ref, l16_ref, ya_ref, gb_ref, x_ref,
                  pb_ref, ow_ref, g1_ref, b1_ref, rwh_ref, rwl_ref, *rest, alpha, n_anchors):
    x1_ref, x1p_ref, sc_ref, so4, sl4, so16, sl16 = rest[n_anchors:]
    tm = x_ref.shape[1]
    n_chunks = GROUP_WIDTH // LANES
    for (o_ref, l_ref, so, sl, d) in ((o4_ref, l4_ref, so4, sl4, 4), (o16_ref, l16_ref, so16, sl16, 16)):
        n = tm // d
        for r in range(d):
            o_r = o_ref[0, r].astype(F32)
            l_r = l_ref[0, r]
            for c in range(n_chunks):
                so[c, pl.ds(r, n, stride=d), :] = o_r[:, c * LANES:(c + 1) * LANES]
                sl[c, pl.ds(r, n, stride=d), :] = l_r[:, c * LANES:(c + 1) * LANES]

    def natural(s):
        return jnp.concatenate([s[c] for c in range(n_chunks)], axis=1)

    l1 = l1_ref[0, 0]
    l4 = natural(sl4)
    l16 = natural(sl16)
    lmax = jnp.maximum(jnp.maximum(l1, l4), l16)
    e1 = jnp.exp(l1 - lmax)
    e4 = jnp.exp(l4 - lmax)
    e16 = jnp.exp(l16 - lmax)
    yb = (e1 * o1_ref[0, 0].astype(F32) + e4 * natural(so4) + e16 * natural(so16)) / (e1 + e4 + e16)
    ybp = jnp.dot(yb.astype(BF16), pb_ref[...], preferred_element_type=F32)
    merged = ya_ref[0] + gb_ref[0] * ybp.astype(BF16)
    mix = jnp.dot(merged, ow_ref[...], preferred_element_type=F32)
    x1 = _layer_norm(alpha * x_ref[0] + mix, g1_ref[...], b1_ref[...])
    x1_ref[0] = x1
    x1p_ref[0] = _pack_bf16_pairs(x1)
    hi = x1.astype(BF16)
    lo = (x1 - hi.astype(F32)).astype(BF16)
    def logits_t(w_ref, xt):
        return lax.dot_general(w_ref[...], xt, (((1,), (1,)), ((), ())), preferred_element_type=F32)
    sc_ref[...] = jax.nn.sigmoid(logits_t(rwh_ref, hi) + logits_t(rwh_ref, lo) + logits_t(rwl_ref, hi))


def _merge_and_norm(attn_outs, ya, gb, x, pb, ow, g1, b1, rwh, rwl, alpha, anchors):
    B, S, D = x.shape
    tm = MERGE_TILE
    in_specs = []
    args = []
    for (o, l), (_, d) in zip(attn_outs, DILATED_PATTERNS):
        spec = pl.BlockSpec((1, d, tm // d, GROUP_WIDTH), lambda b, t: (b, 0, t, 0))
        in_specs += [spec, spec]
        args += [o, l]
    tok_spec = pl.BlockSpec((1, tm, D), lambda b, t: (b, t, 0))
    in_specs += [tok_spec, tok_spec, tok_spec]
    args += [ya, gb, x]
    consts = (pb, ow, g1, b1, rwh, rwl)
    in_specs += [_const_spec(c.shape) for c in consts]
    in_specs += [pl.BlockSpec(memory_space=pl.ANY) for _ in anchors]
    return pl.pallas_call(
        functools.partial(_merge_kernel, alpha=alpha, n_anchors=len(anchors)),
        grid=(B, S // tm),
        in_specs=in_specs,
        out_specs=[tok_spec,
                   pl.BlockSpec((1, tm, D // 2), lambda b, t: (b, t, 0)),
                   pl.BlockSpec((N_EXPERTS, tm), lambda b, t: (0, b * (S // tm) + t))],
        out_shape=[jax.ShapeDtypeStruct((B, S, D), F32),
                   jax.ShapeDtypeStruct((B, S, D // 2), I32),
                   jax.ShapeDtypeStruct((N_EXPERTS, B * S), F32)],
        scratch_shapes=[pltpu.VMEM((GROUP_WIDTH // LANES, tm, LANES), F32) for _ in range(4)],
        compiler_params=pltpu.CompilerParams(
            dimension_semantics=("arbitrary", "arbitrary"), vmem_limit_bytes=VMEM_LIMIT),
        name="merge_norm_router",
    )(*args, *consts, *anchors)


def _sortable_key(x):
    bits = pltpu.bitcast(x, I32)
    return jnp.where(bits < 0, bits ^ jnp.int32(0x7FFFFFFF), bits)


def _route_kernel(sc_ref, bias_ref, before_ref, eidx_ref, rank_ref, gate_ref, cnt_ref, carry_ref):
    ne, tm = sc_ref.shape
    gsize = ne // N_EXPERT_GROUPS
    neg_inf = np.float32(-np.inf)
    removed = jnp.int32(-2 ** 31)

    @pl.when(pl.program_id(0) == 0)
    def _():
        carry_ref[...] = jnp.zeros_like(carry_ref)

    scores = sc_ref[...]
    biased = scores + bias_ref[...]

    gsum = []
    for g in range(N_EXPERT_GROUPS):
        v = biased[g * gsize:(g + 1) * gsize]
        m1 = jnp.max(v, axis=0, keepdims=True)
        n1 = jnp.sum(jnp.where(v == m1, 1.0, 0.0), axis=0, keepdims=True)
        m2 = jnp.max(jnp.where(v < m1, v, neg_inf), axis=0, keepdims=True)
        gsum.append(m1 + jnp.where(n1 >= 2.0, m1, m2))
    gkey = _sortable_key(jnp.concatenate(gsum, axis=0))

    def pick_first_max(keys, ids, n_ids):
        m = jnp.max(keys, axis=0, keepdims=True)
        idx = jnp.min(jnp.where(keys == m, ids, n_ids), axis=0, keepdims=True)
        hit = ids == idx
        return idx, hit, jnp.where(hit, removed, keys)

    gid = lax.broadcasted_iota(I32, (N_EXPERT_GROUPS, tm), 0)
    for _ in range(TOPK_GROUPS):
        _, _, gkey = pick_first_max(gkey, gid, N_EXPERT_GROUPS)
    group_on = gkey == removed

    masked = jnp.concatenate(
        [jnp.where(group_on[g:g + 1], biased[g * gsize:(g + 1) * gsize], neg_inf)
         for g in range(N_EXPERT_GROUPS)], axis=0)
    keys = _sortable_key(masked)
    eid = lax.broadcasted_iota(I32, (ne, tm), 0)
    picks = []
    for _ in range(TOP_K):
        idx, _, keys = pick_first_max(keys, eid, ne)
        picks.append(idx)

    sel = jnp.where(keys == removed, 1.0, 0.0)
    ranks = jnp.dot(sel.astype(BF16), before_ref[...], preferred_element_type=F32) + carry_ref[...]
    carry_ref[...] = carry_ref[...] + jnp.sum(sel, axis=1, keepdims=True)
    cnt_ref[...] = carry_ref[...]

    s_k, r_k = [], []
    for idx in picks:
        hit = eid == idx
        s_k.append(jnp.sum(jnp.where(hit, scores, 0.0), axis=0, keepdims=True))
        r_k.append(jnp.sum(jnp.where(hit, ranks, 0.0), axis=0, keepdims=True))
    total = s_k[0]
    for s in s_k[1:]:
        total = total + s
    eidx_ref[...] = jnp.concatenate(picks, axis=0)
    rank_ref[...] = jnp.concatenate(r_k, axis=0).astype(I32)
    gate_ref[...] = jnp.concatenate([s / total * np.float32(ROUTED_SCALE) for s in s_k], axis=0)


def _route(scores_t, bias, tok0, T):
    ne = scores_t.shape[0]
    tm = ROUTE_TILE
    first = tok0 // tm
    before = jnp.asarray(np.triu(np.ones((tm, tm), np.float32), k=1), BF16)
    out_spec = pl.BlockSpec((TOP_K, tm), lambda i: (0, i))
    return pl.pallas_call(
        _route_kernel,
        grid=(T // tm,),
        in_specs=[pl.BlockSpec((ne, tm), lambda i: (0, first + i)), _const_spec(bias.shape),
                  _const_spec(before.shape)],
        out_specs=[out_spec, out_spec, out_spec, _const_spec((ne, 1))],
        out_shape=[jax.ShapeDtypeStruct((TOP_K, T), I32), jax.ShapeDtypeStruct((TOP_K, T), I32),
                   jax.ShapeDtypeStruct((TOP_K, T), F32), jax.ShapeDtypeStruct((ne, 1), F32)],
        scratch_shapes=[pltpu.VMEM((ne, 1), F32)],
        compiler_params=pltpu.CompilerParams(dimension_semantics=("arbitrary",), vmem_limit_bytes=VMEM_LIMIT),
        name="route_topk",
    )(scores_t, bias, before)


def _dest_kernel(pstart_ref, eidx_ref, rank_ref, out_ref):
    eidx = eidx_ref[...]
    start = jnp.zeros(eidx.shape, I32)
    for e in range(N_EXPERTS):
        start = jnp.where(eidx == e, pstart_ref[e], start)
    out_ref[...] = start + rank_ref[...]


def _dest_rows(eidx_t, rank_t, pstart):
    T = eidx_t.shape[1]
    tm = DEST_TILE
    tok_spec = pl.BlockSpec((TOP_K, tm), lambda i: (0, i))
    return pl.pallas_call(
        _dest_kernel,
        grid=(T // tm,),
        in_specs=[pl.BlockSpec(memory_space=pltpu.SMEM), tok_spec, tok_spec],
        out_specs=tok_spec,
        out_shape=jax.ShapeDtypeStruct((TOP_K, T), I32),
        compiler_params=pltpu.CompilerParams(dimension_semantics=("arbitrary",)),
        name="moe_dest_rows",
    )(pstart, eidx_t, rank_t)


def _sc_workers():
    info = plsc.get_sparse_core_info()
    return info.num_cores, info.num_cores * info.num_subcores


def _sc_scatter_rows(rows, row0, dest, n_out):
    n_dst, n_chunks, ch = dest.shape
    width = rows.shape[1]
    n_cores, n_workers = _sc_workers()
    per_w = n_chunks // n_workers
    assert n_chunks % n_workers == 0 and per_w % 8 == 0 and row0 + n_chunks * ch <= rows.shape[0]

    def body(rows_hbm, dest_hbm, out_hbm, idx_v, buf, lsem, ssem):
        wid = lax.axis_index("s") * n_cores + lax.axis_index("c")
        c0 = wid * per_w
        for k in range(n_dst):
            pltpu.sync_copy(dest_hbm.at[k, pl.ds(pl.multiple_of(c0, 8), per_w)], idx_v.at[k])

        def load(c, b):
            return pltpu.make_async_copy(rows_hbm.at[pl.ds(row0 + (c0 + c) * ch, ch)], buf.at[b], lsem.at[b])

        def scatter(c, b, k):
            return pltpu.make_async_copy(buf.at[b], out_hbm.at[idx_v.at[k, c]], ssem.at[b])

        load(0, 0).start()

        @pl.loop(0, per_w, step=2)
        def _(c):
            for b in range(2):
                cc = c + b
                load(cc, b).wait()

                @pl.when(cc >= 1)
                def _():
                    for k in range(n_dst):
                        scatter(cc - 1, 1 - b, k).wait()

                @pl.when(cc + 1 < per_w)
                def _():
                    load(cc + 1, 1 - b).start()

                for k in range(n_dst):
                    scatter(cc, b, k).start()

        for k in range(n_dst):
            scatter(per_w - 1, 1, k).wait()

    mesh = plsc.VectorSubcoreMesh(core_axis_name="c", subcore_axis_name="s")
    return pl.kernel(
        body, out_type=jax.ShapeDtypeStruct((n_out, width), rows.dtype), mesh=mesh,
        scratch_types=[pltpu.VMEM((n_dst, per_w, ch), I32), pltpu.VMEM((2, ch, width), rows.dtype),
                       pltpu.SemaphoreType.DMA((2,)), pltpu.SemaphoreType.DMA((2,))],
        name="moe_dispatch_scatter",
    )(rows, dest)


def _sc_gather_weighted_sum(table, idx, gates):
    n_src, n_tok = idx.shape
    ct = SC_SUM_CHUNK
    n_chunks = n_tok // ct
    half = table.shape[1]
    lanes = plsc.get_sparse_core_info().num_lanes
    n_cores, n_workers = _sc_workers()
    per_w = n_chunks // n_workers
    assert n_chunks % n_workers == 0 and per_w % 2 == 0 and half % lanes == 0

    def body(table_hbm, idx_hbm, gate_hbm, out_hbm, idx_v, gate_v, buf, out_v, gsem, wsem):
        wid = lax.axis_index("s") * n_cores + lax.axis_index("c")
        c0 = wid * per_w
        pltpu.sync_copy(idx_hbm.at[:, pl.ds(c0 * ct, per_w * ct)], idx_v)
        pltpu.sync_copy(gate_hbm.at[:, pl.ds(c0 * ct, per_w * ct)], gate_v)

        def gather(c, b, k):
            return pltpu.make_async_copy(table_hbm.at[idx_v.at[k, pl.ds(c * ct, ct)]], buf.at[b, k], gsem.at[b])

        def write(c, b):
            return pltpu.make_async_copy(out_v.at[b], out_hbm.at[pl.ds((c0 + c) * ct, ct)], wsem.at[b])

        for k in range(n_src):
            gather(0, 0, k).start()

        @pl.loop(0, per_w, step=2)
        def _(c):
            for b in range(2):
                cc = c + b
                for k in range(n_src):
                    gather(cc, b, k).wait()

                @pl.when(cc + 1 < per_w)
                def _():
                    for k in range(n_src):
                        gather(cc + 1, 1 - b, k).start()

                @pl.when(cc >= 2)
                def _():
                    write(cc - 2, b).wait()

                @pl.loop(0, ct)
                def _(t):
                    tok = jnp.full((lanes,), cc * ct + t, I32)
                    g = [plsc.load_gather(gate_v, [jnp.full((lanes,), k, I32), tok]) for k in range(n_src)]

                    @plsc.parallel_loop(0, half, step=lanes, unroll=2)
                    def _(col):
                        lo = jnp.zeros((lanes,), F32)
                        hi = jnp.zeros((lanes,), F32)
                        for k in range(n_src):
                            w = buf[b, k, t, pl.ds(col, lanes)]
                            w_lo, w_hi = plsc.unpack(plsc.bitcast(w, BF16), format=plsc.PackFormat.INTERLEAVED)
                            lo = lo + g[k] * w_lo
                            hi = hi + g[k] * w_hi
                        out_v[b, t, pl.ds(col, lanes)] = lo
                        out_v[b, t, pl.ds(half + col, lanes)] = hi

                write(cc, b).start()

        write(per_w - 2, 0).wait()
        write(per_w - 1, 1).wait()

    mesh = plsc.VectorSubcoreMesh(core_axis_name="c", subcore_axis_name="s")
    return pl.kernel(
        body, out_type=jax.ShapeDtypeStruct((n_chunks * ct, 2 * half), F32), mesh=mesh,
        scratch_types=[pltpu.VMEM((n_src, per_w * ct), I32), pltpu.VMEM((n_src, per_w * ct), F32),
                       pltpu.VMEM((2, n_src, ct, half), I32), pltpu.VMEM((2, ct, 2 * half), F32),
                       pltpu.SemaphoreType.DMA((2,)), pltpu.SemaphoreType.DMA((2,))],
        compiler_params=pltpu.CompilerParams(needs_layout_passes=False),
        name="moe_combine_gather_sum",
    )(table, idx, gates)


def _sc_pack_row_pairs(w):
    R, C = w.shape
    lanes = plsc.get_sparse_core_info().num_lanes
    cr = SC_PACK_CHUNK_WORDS // C
    n_chunks = R // cr
    n_cores, n_workers = _sc_workers()
    per_w = n_chunks // n_workers
    assert R % cr == 0 and n_chunks % n_workers == 0 and per_w % 2 == 0 and cr % 2 == 0 and C % lanes == 0

    def body(w_hbm, out_hbm, in_v, out_v, lsem, ssem):
        wid = lax.axis_index("s") * n_cores + lax.axis_index("c")
        c0 = wid * per_w

        def load(c, b):
            return pltpu.make_async_copy(w_hbm.at[pl.ds((c0 + c) * cr, cr)], in_v.at[b], lsem.at[b])

        def store(c, b):
            return pltpu.make_async_copy(out_v.at[b], out_hbm.at[pl.ds((c0 + c) * (cr // 2), cr // 2)],
                                         ssem.at[b])

        load(0, 0).start()

        @pl.loop(0, per_w, step=2)
        def _(c):
            for b in range(2):
                cc = c + b
                load(cc, b).wait()

                @pl.when(cc + 1 < per_w)
                def _():
                    load(cc + 1, 1 - b).start()

                @pl.when(cc >= 2)
                def _():
                    store(cc - 2, b).wait()

                @pl.loop(0, cr // 2)
                def _(i):
                    @plsc.parallel_loop(0, C, step=lanes, unroll=4)
                    def _(col):
                        even = in_v[b, 2 * i, pl.ds(col, lanes)]
                        odd = in_v[b, 2 * i + 1, pl.ds(col, lanes)]
                        pair = plsc.pack(even, odd, format=plsc.PackFormat.INTERLEAVED)
                        out_v[b, i, pl.ds(col, lanes)] = plsc.bitcast(pair, I32)

                store(cc, b).start()

        store(per_w - 2, 0).wait()
        store(per_w - 1, 1).wait()

    mesh = plsc.VectorSubcoreMesh(core_axis_name="c", subcore_axis_name="s")
    return pl.kernel(
        body, out_type=jax.ShapeDtypeStruct((R // 2, C), I32), mesh=mesh,
        scratch_types=[pltpu.VMEM((2, cr, C), F32), pltpu.VMEM((2, cr // 2, C), I32),
                       pltpu.SemaphoreType.DMA((2,)), pltpu.SemaphoreType.DMA((2,))],
        compiler_params=pltpu.CompilerParams(needs_layout_passes=False),
        name="expert_weights_bf16",
    )(w)


def _expert_kernel(be_ref, ue_ref, nu_ref, xs_hbm, wg_hbm, wu_hbm, wd_hbm, y_hbm,
                   xbuf, ybuf, wg_v, wu_v, wd_v, xsem, ysem, wsem):
    n_used = nu_ref[0]
    n_exp = nu_ref[1]
    blk = MOE_BLOCK
    rows_gu = D_MODEL // 2
    rows_d = EXPERT_DIM // 2

    def rows(j):
        return pl.ds(pl.multiple_of(j * blk, blk), blk)

    def x_copy(j, p):
        return pltpu.make_async_copy(xs_hbm.at[rows(j)], xbuf.at[p], xsem.at[p])

    def y_copy(j, p):
        return pltpu.make_async_copy(ybuf.at[p], y_hbm.at[rows(j)], ysem.at[p])

    def w_copies(q, s):
        e = ue_ref[q]
        gu = pl.ds(pl.multiple_of(e * rows_gu, rows_gu), rows_gu)
        dn = pl.ds(pl.multiple_of(e * rows_d, rows_d), rows_d)
        return (pltpu.make_async_copy(wg_hbm.at[gu], wg_v.at[s], wsem.at[s, 0]),
                pltpu.make_async_copy(wu_hbm.at[gu], wu_v.at[s], wsem.at[s, 1]),
                pltpu.make_async_copy(wd_hbm.at[dn], wd_v.at[s], wsem.at[s, 2]))

    for q0 in range(W_SLOTS - 1):
        @pl.when(q0 < n_exp)
        def _():
            for c in w_copies(q0, q0):
                c.start()

    for j0 in range(X_SLOTS):
        @pl.when(j0 < n_used)
        def _():
            x_copy(j0, j0).start()

    def block_step(j, p, q):
        is_new = (j == 0) | (be_ref[j] != be_ref[jnp.maximum(j - 1, 0)])
        q = q + is_new.astype(I32)

        s = q % W_SLOTS

        @pl.when(is_new)
        def _():
            for c in w_copies(q, s):
                c.wait()

            @pl.when(q + W_SLOTS - 1 < n_exp)
            def _():
                for c in w_copies(q + W_SLOTS - 1, (q + W_SLOTS - 1) % W_SLOTS):
                    c.start()

        x_copy(j, p).wait()

        @pl.when(j >= X_SLOTS)
        def _():
            y_copy(j - X_SLOTS, p).wait()

        wg = pltpu.bitcast(wg_v[s], BF16)
        wu = pltpu.bitcast(wu_v[s], BF16)
        wd = pltpu.bitcast(wd_v[s], BF16)
        lo, hi = _unpack_bf16_pairs(xbuf[p])
        x = jnp.concatenate([lo.astype(BF16), hi.astype(BF16)], axis=1)
        g = jnp.dot(x, wg, preferred_element_type=F32)
        u = jnp.dot(x, wu, preferred_element_type=F32)
        hb = (g * jax.nn.sigmoid(g) * u).astype(BF16)
        ybuf[p] = _pack_bf16_pairs(jnp.dot(hb, wd, preferred_element_type=F32))
        y_copy(j, p).start()

        @pl.when(j + X_SLOTS < n_used)
        def _():
            x_copy(j + X_SLOTS, p).start()

        return q

    def group(m, q):
        q = block_step(X_SLOTS * m, 0, q)
        for p in range(1, X_SLOTS):
            j = X_SLOTS * m + p
            q = lax.cond(j < n_used, functools.partial(block_step, j, p), lambda q: q, q)
        return q

    lax.fori_loop(0, (n_used + X_SLOTS - 1) // X_SLOTS, group, jnp.int32(-1))

    for back in range(X_SLOTS, 0, -1):
        b = n_used - back
        for p in range(X_SLOTS):
            @pl.when((b >= 0) & (b % X_SLOTS == p))
            def _():
                y_copy(b, p).wait()


def _experts(block_e, used_e, counts2, xs, wg, wu, wd):
    n_rows, half = xs.shape
    smem = pl.BlockSpec(memory_space=pltpu.SMEM)
    hbm = pl.BlockSpec(memory_space=pl.ANY)
    return pl.pallas_call(
        _expert_kernel,
        in_specs=[smem, smem, smem, hbm, hbm, hbm, hbm],
        out_specs=hbm,
        out_shape=jax.ShapeDtypeStruct((n_rows, half), I32),
        scratch_shapes=[pltpu.VMEM((X_SLOTS, MOE_BLOCK, half), I32), pltpu.VMEM((X_SLOTS, MOE_BLOCK, half), I32),
                        pltpu.VMEM((W_SLOTS, D_MODEL // 2, EXPERT_DIM), I32),
                        pltpu.VMEM((W_SLOTS, D_MODEL // 2, EXPERT_DIM), I32),
                        pltpu.VMEM((W_SLOTS, EXPERT_DIM // 2, D_MODEL), I32),
                        pltpu.SemaphoreType.DMA((X_SLOTS,)), pltpu.SemaphoreType.DMA((X_SLOTS,)),
                        pltpu.SemaphoreType.DMA((W_SLOTS, 3))],
        compiler_params=pltpu.CompilerParams(vmem_limit_bytes=VMEM_LIMIT),
        name="moe_experts",
    )(block_e, used_e, counts2, xs, wg, wu, wd)


def _combine_kernel(routed_ref, x1_ref, swgu_ref, swd_ref, g2_ref, b2_ref, *rest, alpha):
    out_ref = rest[-1]
    x1 = x1_ref[...]
    gu = jnp.dot(x1.astype(BF16), swgu_ref[...], preferred_element_type=F32)
    g = gu[:, :EXPERT_DIM]
    u = gu[:, EXPERT_DIM:]
    shared = jnp.dot((g * jax.nn.sigmoid(g) * u).astype(BF16), swd_ref[...], preferred_element_type=F32)
    out_ref[...] = _layer_norm(alpha * x1 + (routed_ref[...] + shared), g2_ref[...], b2_ref[...])


def _combine(routed, x1, tok0, swgu, swd, g2, b2, alpha, out_prev):
    T, D = x1.shape
    n_tok = routed.shape[0]
    tc = COMBINE_TILE
    first = tok0 // tc
    consts = (swgu, swd, g2, b2)
    args = [routed, x1, *consts]
    in_specs = [pl.BlockSpec((tc, D), lambda i: (i, 0)),
                pl.BlockSpec((tc, D), lambda i: (first + i, 0))] + [_const_spec(c.shape) for c in consts]
    aliases = {}
    if out_prev is not None:
        aliases = {len(args): 0}
        args.append(out_prev)
        in_specs.append(pl.BlockSpec(memory_space=pl.ANY))
    return pl.pallas_call(
        functools.partial(_combine_kernel, alpha=alpha),
        grid=(n_tok // tc,),
        in_specs=in_specs,
        out_specs=pl.BlockSpec((tc, D), lambda i: (first + i, 0)),
        out_shape=jax.ShapeDtypeStruct((T, D), F32),
        input_output_aliases=aliases,
        compiler_params=pltpu.CompilerParams(dimension_semantics=("arbitrary",), vmem_limit_bytes=VMEM_LIMIT),
        name="moe_combine_norm",
    )(*args)


def _mixer(x, in_w, in_b, ng, nb, spatial_w, spatial_b, proj_a_w, proj_b_w, out_w, ln1_g, ln1_b,
           router_w, alpha, anchors):
    gw, aw, D = GMLP_WIDTH, ATTN_WIDTH, D_MODEL
    w = in_w.astype(BF16)
    q0 = 2 * gw
    wuv, buv = w[:, :q0], in_b[None, :q0]
    watt = jnp.stack([jnp.concatenate([w[:, q0 + s * aw + p * GROUP_WIDTH:q0 + s * aw + (p + 1) * GROUP_WIDTH]
                                       for s in range(3)], axis=1) for p in range(len(DILATED_PATTERNS))])
    batt = jnp.stack([jnp.concatenate([in_b[q0 + s * aw + p * GROUP_WIDTH:q0 + s * aw + (p + 1) * GROUP_WIDTH]
                                       for s in range(3)])[None] for p in range(len(DILATED_PATTERNS))])
    g0 = q0 + 3 * aw
    wg, bg = w[:, g0:], in_b[None, g0:]
    sb = jnp.repeat(spatial_b.T, gw // GMLP_GROUPS, axis=1)
    ya, gb, a1, a4, a16 = _input_projection(
        x, wuv, buv, watt, batt, wg, bg, ng[None], nb[None], spatial_w, sb, proj_a_w.astype(BF16))
    attn_outs = [_dilated_attention(a, p, d) for p, (a, (_, d)) in enumerate(zip((a1, a4, a16), DILATED_PATTERNS))]
    rw_t = router_w.T
    rwh = rw_t.astype(BF16)
    rwl = (rw_t - rwh.astype(F32)).astype(BF16)
    return _merge_and_norm(attn_outs, ya, gb, x, proj_b_w.astype(BF16), out_w.astype(BF16),
                           ln1_g[None], ln1_b[None], rwh, rwl, alpha, anchors)


def _moe(x1, x1p, scores_t, router_bias, w_gate, w_up, w_down, sw_gate, sw_up, sw_down, ln2_g, ln2_b, alpha):
    swgu = jnp.concatenate([sw_gate, sw_up], axis=1).astype(BF16)
    swd = sw_down.astype(BF16)
    out = None
    tok0 = 0
    for part, eighths in enumerate(MOE_TOKEN_SPLIT):
        n_tok = x1.shape[0] * eighths // 8
        n_blocks = (n_tok * TOP_K + N_EXPERTS * (MOE_BLOCK - 1)) // MOE_BLOCK
        eidx_t, rank_t, gate_t, counts = _route(scores_t, router_bias[:, None], tok0, n_tok)
        counts = counts[:, 0].astype(I32)
        padded = (counts + MOE_BLOCK - 1) // MOE_BLOCK * MOE_BLOCK
        pend = jnp.cumsum(padded).astype(I32)
        pstart = pend - padded
        block_starts = jnp.arange(n_blocks, dtype=I32) * MOE_BLOCK
        block_e = jnp.minimum(jnp.sum((pend[None, :] <= block_starts[:, None]).astype(I32), axis=1),
                              N_EXPERTS - 1)
        used = counts > 0
        used_e = jnp.argsort(jnp.logical_not(used), stable=True).astype(I32)
        counts2 = jnp.stack([pend[-1] // MOE_BLOCK, jnp.sum(used.astype(I32))]).astype(I32)
        dest_t = _dest_rows(eidx_t, rank_t, pstart)
        xs = _sc_scatter_rows(x1p, tok0, dest_t.reshape(TOP_K, n_tok // SC_CHUNK, SC_CHUNK), n_blocks * MOE_BLOCK)
        y_rows = _experts(block_e, used_e, counts2, xs, w_gate, w_up, w_down)
        pieces = MOE_TAIL_PIECES if part == len(MOE_TOKEN_SPLIT) - 1 else 1
        n_piece = n_tok // pieces
        for q in range(pieces):
            cols = slice(q * n_piece, (q + 1) * n_piece)
            routed = _sc_gather_weighted_sum(y_rows, dest_t[:, cols], gate_t[:, cols])
            out = _combine(routed, x1, tok0 + q * n_piece, swgu, swd, ln2_g[None], ln2_b[None], alpha, out)
        tok0 += n_tok
    return out


def kernel(x, in_w, in_b, gmlp_norm_g, gmlp_norm_b, spatial_w, spatial_b, proj_a_w, proj_b_w, out_w,
           ln1_g, ln1_b, router_w, router_bias, expert_w_gate, expert_w_up, expert_w_down,
           shared_w_gate, shared_w_up, shared_w_down, ln2_g, ln2_b):
    B, S, D = x.shape
    depth = in_w.shape[0]
    alpha = np.float32((2.0 * depth) ** 0.25)
    for l in range(depth):
        packed = [_sc_pack_row_pairs(w[l].reshape(-1, w.shape[-1]))
                  for w in (expert_w_gate, expert_w_up, expert_w_down)]
        x1, x1p, scores_t = _mixer(x, in_w[l], in_b[l], gmlp_norm_g[l], gmlp_norm_b[l], spatial_w[l],
                                 spatial_b[l], proj_a_w[l], proj_b_w[l], out_w[l], ln1_g[l], ln1_b[l],
                                 router_w[l], alpha, [p[:8] for p in packed])
        out = _moe(x1.reshape(B * S, D), x1p.reshape(B * S, D // 2), scores_t,
                   router_bias[l], *packed,
                   shared_w_gate[l], shared_w_up[l], shared_w_down[l], ln2_g[l], ln2_b[l], alpha)
        x = out.reshape(B, S, D)
    return x
```

```python
import functools
import math

import numpy as np
import jax
import jax.numpy as jnp
from jax import lax
from jax.experimental import pallas as pl
from jax.experimental.pallas import tpu as pltpu
from jax.experimental.pallas import tpu_sc as plsc

F32 = jnp.float32
BF16 = jnp.bfloat16
U32 = jnp.uint32
I32 = jnp.int32

D_MODEL = 1024
GMLP_WIDTH = 1024
GMLP_GROUPS = 8
GMLP_CHUNK = 128
HEAD_DIM = 64
DILATED_PATTERNS = ((128, 1), (512, 4), (2048, 16))
HEADS_PER_GROUP = 4
GROUP_WIDTH = HEADS_PER_GROUP * HEAD_DIM
ATTN_WIDTH = GROUP_WIDTH * len(DILATED_PATTERNS)
ATTN_BLOCK = 128
N_EXPERTS = 256
TOP_K = 8
N_EXPERT_GROUPS = 8
TOPK_GROUPS = 4
EXPERT_DIM = 256
ROUTED_SCALE = 2.5
LN_EPS = 1e-5
LANES = 128
MASKED_SCORE = -1e30

PROJ_TILE = 512
ATTN_QBLOCKS = 16
MERGE_TILE = 1024
ROUTE_TILE = 512
DEST_TILE = 2048
MOE_BLOCK = 256
SC_PACK_CHUNK_WORDS = 16384
SC_CHUNK = 64
SC_SUM_CHUNK = 8
MOE_TOKEN_SPLIT = (4, 4)
MOE_TAIL_PIECES = 4
X_SLOTS = 4
W_SLOTS = 3
COMBINE_TILE = 512
VMEM_LIMIT = 56 * 1024 * 1024


def _layer_norm(y, g, b):
    mu = jnp.mean(y, axis=-1, keepdims=True)
    yc = y - mu
    var = jnp.mean(yc * yc, axis=-1, keepdims=True)
    return yc * lax.rsqrt(var + LN_EPS) * g + b


def _gelu(x):
    return 0.5 * x * (1.0 + lax.erf(x * np.float32(math.sqrt(0.5))))


def _pack_bf16_pairs(x):
    w = x.shape[1] // 2
    bits = pltpu.bitcast(x.astype(BF16).astype(F32), U32)
    return pltpu.bitcast((bits[:, :w] >> 16) | (bits[:, w:] & jnp.uint32(0xFFFF0000)), I32)


def _unpack_bf16_pairs(words):
    w = pltpu.bitcast(words, U32)
    lo = pltpu.bitcast(w << 16, F32)
    hi = pltpu.bitcast(w & jnp.uint32(0xFFFF0000), F32)
    return lo, hi


def _const_spec(shape):
    nd = len(shape)
    return pl.BlockSpec(shape, lambda *_: (0,) * nd)


def _proj_kernel(x_ref, wuv_ref, buv_ref, watt_ref, batt_ref, wg_ref, bg_ref, ng_ref, nb_ref,
                 sw_ref, sb_ref, pa_ref, ya_ref, gb_ref, a1_ref, a4_ref, a16_ref, xc_ref):
    tm = x_ref.shape[1]
    gw = GMLP_WIDTH
    xb = x_ref[0].astype(BF16)

    def proj(w, b):
        return jnp.dot(xb, w, preferred_element_type=F32) + b

    h_v = proj(wuv_ref[:, gw:], buv_ref[:, gw:])
    h_u = proj(wuv_ref[:, :gw], buv_ref[:, :gw])
    v = _layer_norm(_gelu(h_v), ng_ref[...], nb_ref[...]).astype(BF16)
    h_ga = proj(wg_ref[:, :D_MODEL], bg_ref[:, :D_MODEL])
    u = _gelu(h_u)

    cw = gw // GMLP_GROUPS
    row = lax.broadcasted_iota(I32, (GMLP_CHUNK, GMLP_CHUNK), 0)
    col = lax.broadcasted_iota(I32, (GMLP_CHUNK, GMLP_CHUNK), 1)
    ws = [jnp.where(row >= col, sw_ref[g], 0.0).astype(BF16) for g in range(GMLP_GROUPS)]
    chunks = []
    for c in range(tm // GMLP_CHUNK):
        vc = v[c * GMLP_CHUNK:(c + 1) * GMLP_CHUNK]
        cols = [jnp.dot(ws[g], vc[:, g * cw:(g + 1) * cw], preferred_element_type=F32)
                for g in range(GMLP_GROUPS)]
        chunks.append(jnp.concatenate(cols, axis=1) + sb_ref[...])
    vmix = jnp.concatenate(chunks, axis=0)
    h_gb = proj(wg_ref[:, D_MODEL:], bg_ref[:, D_MODEL:])
    ga = jax.nn.sigmoid(h_ga)
    ya = jnp.dot((u * vmix).astype(BF16), pa_ref[...], preferred_element_type=F32)
    gb_ref[0] = jax.nn.sigmoid(h_gb).astype(BF16)

    n_chunks = x_ref.shape[2] // LANES
    for c in range(n_chunks):
        xc_ref[c] = x_ref[0, :, c * LANES:(c + 1) * LANES]

    def attn_proj(p, d):
        n = tm // d
        if d == 1:
            xp = xb
        else:
            xp = jnp.concatenate(
                [jnp.concatenate([xc_ref[c, pl.ds(r, n, stride=d), :] for c in range(n_chunks)], axis=1)
                 for r in range(d)], axis=0).astype(BF16)
        return jnp.dot(xp, watt_ref[p], preferred_element_type=F32)

    def attn_store(p, d, h, a_ref):
        n = tm // d
        h = (h + batt_ref[p]).astype(BF16)
        for r in range(d):
            a_ref[0, r] = h[r * n:(r + 1) * n]

    a_refs = (a1_ref, a4_ref, a16_ref)
    dils = [d for _, d in DILATED_PATTERNS]
    h_prev = attn_proj(0, dils[0])
    ya_ref[0] = (ga * ya).astype(BF16)
    for p in range(1, len(dils)):
        h_next = attn_proj(p, dils[p])
        attn_store(p - 1, dils[p - 1], h_prev, a_refs[p - 1])
        h_prev = h_next
    attn_store(len(dils) - 1, dils[-1], h_prev, a_refs[-1])


def _input_projection(x, wuv, buv, watt, batt, wg, bg, ng, nb, sw, sb, pa):
    B, S, D = x.shape
    tm = PROJ_TILE
    grid = (B, S // tm)
    out_shape = [jax.ShapeDtypeStruct((B, S, D), BF16), jax.ShapeDtypeStruct((B, S, D), BF16)]
    out_specs = [pl.BlockSpec((1, tm, D), lambda b, t: (b, t, 0)),
                 pl.BlockSpec((1, tm, D), lambda b, t: (b, t, 0))]
    for _, d in DILATED_PATTERNS:
        out_shape.append(jax.ShapeDtypeStruct((B, d, S // d, ATTN_WIDTH), BF16))
        out_specs.append(pl.BlockSpec((1, d, tm // d, ATTN_WIDTH), lambda b, t: (b, 0, t, 0)))
    consts = (wuv, buv, watt, batt, wg, bg, ng, nb, sw, sb, pa)
    return pl.pallas_call(
        _proj_kernel,
        grid=grid,
        in_specs=[pl.BlockSpec((1, tm, D), lambda b, t: (b, t, 0))] + [_const_spec(c.shape) for c in consts],
        out_specs=out_specs,
        out_shape=out_shape,
        scratch_shapes=[pltpu.VMEM((D // LANES, tm, LANES), F32)],
        compiler_params=pltpu.CompilerParams(
            dimension_semantics=("arbitrary", "arbitrary"), vmem_limit_bytes=VMEM_LIMIT),
        name="input_projection",
    )(x, *consts)


def _attn_kernel(qkv_ref, bias_ref, o_ref, lse_ref, *, qblocks, rblock):
    nq = pl.program_id(2)
    gwid = GROUP_WIDTH
    blk = ATTN_BLOCK
    lane = lax.broadcasted_iota(I32, (1, gwid), 1)
    head_masks = [(lane >= h * HEAD_DIM) & (lane < (h + 1) * HEAD_DIM) for h in range(HEADS_PER_GROUP)]
    q_scales = [jnp.where(m, np.float32(HEAD_DIM ** -0.5), 0.0).astype(BF16) for m in head_masks]

    def rows_of(j):
        n = nq * qblocks + j
        return n, pl.multiple_of(n * blk, blk), pl.multiple_of(jnp.maximum(n - 1, 0) * blk, blk)

    def scores(unit):
        ri, j = unit
        n, q0, p0 = rows_of(j)
        q = qkv_ref[0, ri, pl.ds(q0, blk), 0:gwid]
        kk = jnp.concatenate([qkv_ref[0, ri, pl.ds(p0, blk), gwid:2 * gwid],
                              qkv_ref[0, ri, pl.ds(q0, blk), gwid:2 * gwid]], axis=0)
        qs = jnp.concatenate([q * s for s in q_scales], axis=0)
        s = lax.dot_general(qs, kk, (((1,), (1,)), ((), ())), preferred_element_type=F32)
        return s + bias_ref[jnp.where(n == 0, 1, 0)]

    def finish(unit, s):
        ri, j = unit
        _, q0, p0 = rows_of(j)
        vv = jnp.concatenate([qkv_ref[0, ri, pl.ds(p0, blk), 2 * gwid:3 * gwid],
                              qkv_ref[0, ri, pl.ds(q0, blk), 2 * gwid:3 * gwid]], axis=0)
        m = jnp.max(s, axis=1, keepdims=True)
        p = jnp.exp(s - m)
        den = jnp.sum(p, axis=1, keepdims=True)
        pv = jnp.dot(p.astype(BF16), vv, preferred_element_type=F32)
        on = pv / den
        lse = m + jnp.log(den)
        o = jnp.zeros((blk, gwid), F32)
        l = jnp.zeros((blk, gwid), F32)
        for h in range(HEADS_PER_GROUP):
            o = jnp.where(head_masks[h], on[h * blk:(h + 1) * blk], o)
            l = jnp.where(head_masks[h], lse[h * blk:(h + 1) * blk], l)
        o_ref[0, ri, j * blk:(j + 1) * blk, :] = o.astype(BF16)
        lse_ref[0, ri, j * blk:(j + 1) * blk, :] = l

    units = [(ri, j) for ri in range(rblock) for j in range(qblocks)]
    s_next = scores(units[0])
    for i, unit in enumerate(units):
        s_cur = s_next
        if i + 1 < len(units):
            s_next = scores(units[i + 1])
        finish(unit, s_cur)


def _alibi_slopes(n):
    def pow2_slopes(m):
        start = 2.0 ** (-8.0 / m)
        return [start ** (i + 1) for i in range(m)]
    p = 2 ** int(math.floor(math.log2(n)))
    s = pow2_slopes(p)
    if p < n:
        s = s + pow2_slopes(2 * p)[0::2][: n - p]
    return np.array(sorted(s, reverse=True), dtype=np.float32)


def _attn_bias_tables(group, dilation):
    blk = ATTN_BLOCK
    slopes = _alibi_slopes(HEADS_PER_GROUP * len(DILATED_PATTERNS))
    slopes = slopes[group * HEADS_PER_GROUP:(group + 1) * HEADS_PER_GROUP]
    qi = np.arange(blk)[:, None]
    ki = np.arange(2 * blk)[None, :]
    delta = blk + qi - ki
    band = (delta >= 0) & (delta <= blk)
    bias = -slopes[:, None, None] * (delta * dilation).astype(np.float32)[None]
    full = np.where(band[None], bias, np.float32(MASKED_SCORE)).astype(np.float32)
    first = np.where((ki >= blk)[None], full, np.float32(MASKED_SCORE)).astype(np.float32)
    return full.reshape(HEADS_PER_GROUP * blk, 2 * blk), first.reshape(HEADS_PER_GROUP * blk, 2 * blk)


def _dilated_attention(qkv, group, dilation):
    B, d, sd, _ = qkv.shape
    qblocks = min(ATTN_QBLOCKS, sd // ATTN_BLOCK)
    rblock = min(d, ATTN_QBLOCKS // qblocks)
    rows = qblocks * ATTN_BLOCK
    bias = np.stack(_attn_bias_tables(group, dilation))
    grid = (B, d // rblock, sd // rows)
    out_spec = pl.BlockSpec((1, rblock, rows, GROUP_WIDTH), lambda b, r, n: (b, r, n, 0))
    return pl.pallas_call(
        functools.partial(_attn_kernel, qblocks=qblocks, rblock=rblock),
        grid=grid,
        in_specs=[pl.BlockSpec((1, rblock, sd, ATTN_WIDTH), lambda b, r, n: (b, r, 0, 0)),
                  _const_spec(bias.shape)],
        out_specs=[out_spec, out_spec],
        out_shape=[jax.ShapeDtypeStruct((B, d, sd, GROUP_WIDTH), BF16),
                   jax.ShapeDtypeStruct((B, d, sd, GROUP_WIDTH), F32)],
        compiler_params=pltpu.CompilerParams(
            dimension_semantics=("arbitrary", "arbitrary", "arbitrary"), vmem_limit_bytes=VMEM_LIMIT),
        name=f"dilated_attention_d{dilation}",
    )(qkv, jnp.asarray(bias))


def _merge_kernel(o1_ref, l1_ref, o4_ref, l4_ref, o16_ref, l16_ref, ya_ref, gb_ref, x_ref,
                  pb_ref, ow_ref, g1_ref, b1_ref, rwh_ref, rwl_ref, *rest, alpha, n_anchors):
    x1_ref, x1p_ref, sc_ref, so4, sl4, so16, sl16 = rest[n_anchors:]
    tm = x_ref.shape[1]
    n_chunks = GROUP_WIDTH // LANES
    for (o_ref, l_ref, so, sl, d) in ((o4_ref, l4_ref, so4, sl4, 4), (o16_ref, l16_ref, so16, sl16, 16)):
        n = tm // d
        for r in range(d):
            o_r = o_ref[0, r].astype(F32)
            l_r = l_ref[0, r]
            for c in range(n_chunks):
                so[c, pl.ds(r, n, stride=d), :] = o_r[:, c * LANES:(c + 1) * LANES]
                sl[c, pl.ds(r, n, stride=d), :] = l_r[:, c * LANES:(c + 1) * LANES]

    def natural(s):
        return jnp.concatenate([s[c] for c in range(n_chunks)], axis=1)

    l1 = l1_ref[0, 0]
    l4 = natural(sl4)
    l16 = natural(sl16)
    lmax = jnp.maximum(jnp.maximum(l1, l4), l16)
    e1 = jnp.exp(l1 - lmax)
    e4 = jnp.exp(l4 - lmax)
    e16 = jnp.exp(l16 - lmax)
    yb = (e1 * o1_ref[0, 0].astype(F32) + e4 * natural(so4) + e16 * natural(so16)) / (e1 + e4 + e16)
    ybp = jnp.dot(yb.astype(BF16), pb_ref[...], preferred_element_type=F32)
    merged = ya_ref[0] + gb_ref[0] * ybp.astype(BF16)
    mix = jnp.dot(merged, ow_ref[...], preferred_element_type=F32)
    x1 = _layer_norm(alpha * x_ref[0] + mix, g1_ref[...], b1_ref[...])
    x1_ref[0] = x1
    x1p_ref[0] = _pack_bf16_pairs(x1)
    hi = x1.astype(BF16)
    lo = (x1 - hi.astype(F32)).astype(BF16)
    def logits_t(w_ref, xt):
        return lax.dot_general(w_ref[...], xt, (((1,), (1,)), ((), ())), preferred_element_type=F32)
    sc_ref[...] = jax.nn.sigmoid(logits_t(rwh_ref, hi) + logits_t(rwh_ref, lo) + logits_t(rwl_ref, hi))


def _merge_and_norm(attn_outs, ya, gb, x, pb, ow, g1, b1, rwh, rwl, alpha, anchors):
    B, S, D = x.shape
    tm = MERGE_TILE
    in_specs = []
    args = []
    for (o, l), (_, d) in zip(attn_outs, DILATED_PATTERNS):
        spec = pl.BlockSpec((1, d, tm // d, GROUP_WIDTH), lambda b, t: (b, 0, t, 0))
        in_specs += [spec, spec]
        args += [o, l]
    tok_spec = pl.BlockSpec((1, tm, D), lambda b, t: (b, t, 0))
    in_specs += [tok_spec, tok_spec, tok_spec]
    args += [ya, gb, x]
    consts = (pb, ow, g1, b1, rwh, rwl)
    in_specs += [_const_spec(c.shape) for c in consts]
    in_specs += [pl.BlockSpec(memory_space=pl.ANY) for _ in anchors]
    return pl.pallas_call(
        functools.partial(_merge_kernel, alpha=alpha, n_anchors=len(anchors)),
        grid=(B, S // tm),
        in_specs=in_specs,
        out_specs=[tok_spec,
                   pl.BlockSpec((1, tm, D // 2), lambda b, t: (b, t, 0)),
                   pl.BlockSpec((N_EXPERTS, tm), lambda b, t: (0, b * (S // tm) + t))],
        out_shape=[jax.ShapeDtypeStruct((B, S, D), F32),
                   jax.ShapeDtypeStruct((B, S, D // 2), I32),
                   jax.ShapeDtypeStruct((N_EXPERTS, B * S), F32)],
        scratch_shapes=[pltpu.VMEM((GROUP_WIDTH // LANES, tm, LANES), F32) for _ in range(4)],
        compiler_params=pltpu.CompilerParams(
            dimension_semantics=("arbitrary", "arbitrary"), vmem_limit_bytes=VMEM_LIMIT),
        name="merge_norm_router",
    )(*args, *consts, *anchors)


def _sortable_key(x):
    bits = pltpu.bitcast(x, I32)
    return jnp.where(bits < 0, bits ^ jnp.int32(0x7FFFFFFF), bits)


def _route_kernel(sc_ref, bias_ref, before_ref, eidx_ref, rank_ref, gate_ref, cnt_ref, carry_ref):
    ne, tm = sc_ref.shape
    gsize = ne // N_EXPERT_GROUPS
    neg_inf = np.float32(-np.inf)
    removed = jnp.int32(-2 ** 31)

    @pl.when(pl.program_id(0) == 0)
    def _():
        carry_ref[...] = jnp.zeros_like(carry_ref)

    scores = sc_ref[...]
    biased = scores + bias_ref[...]

    gsum = []
    for g in range(N_EXPERT_GROUPS):
        v = biased[g * gsize:(g + 1) * gsize]
        m1 = jnp.max(v, axis=0, keepdims=True)
        n1 = jnp.sum(jnp.where(v == m1, 1.0, 0.0), axis=0, keepdims=True)
        m2 = jnp.max(jnp.where(v < m1, v, neg_inf), axis=0, keepdims=True)
        gsum.append(m1 + jnp.where(n1 >= 2.0, m1, m2))
    gkey = _sortable_key(jnp.concatenate(gsum, axis=0))

    def pick_first_max(keys, ids, n_ids):
        m = jnp.max(keys, axis=0, keepdims=True)
        idx = jnp.min(jnp.where(keys == m, ids, n_ids), axis=0, keepdims=True)
        hit = ids == idx
        return idx, hit, jnp.where(hit, removed, keys)

    gid = lax.broadcasted_iota(I32, (N_EXPERT_GROUPS, tm), 0)
    for _ in range(TOPK_GROUPS):
        _, _, gkey = pick_first_max(gkey, gid, N_EXPERT_GROUPS)
    group_on = gkey == removed

    masked = jnp.concatenate(
        [jnp.where(group_on[g:g + 1], biased[g * gsize:(g + 1) * gsize], neg_inf)
         for g in range(N_EXPERT_GROUPS)], axis=0)
    keys = _sortable_key(masked)
    eid = lax.broadcasted_iota(I32, (ne, tm), 0)
    picks = []
    for _ in range(TOP_K):
        idx, _, keys = pick_first_max(keys, eid, ne)
        picks.append(idx)

    sel = jnp.where(keys == removed, 1.0, 0.0)
    ranks = jnp.dot(sel.astype(BF16), before_ref[...], preferred_element_type=F32) + carry_ref[...]
    carry_ref[...] = carry_ref[...] + jnp.sum(sel, axis=1, keepdims=True)
    cnt_ref[...] = carry_ref[...]

    s_k, r_k = [], []
    for idx in picks:
        hit = eid == idx
        s_k.append(jnp.sum(jnp.where(hit, scores, 0.0), axis=0, keepdims=True))
        r_k.append(jnp.sum(jnp.where(hit, ranks, 0.0), axis=0, keepdims=True))
    total = s_k[0]
    for s in s_k[1:]:
        total = total + s
    eidx_ref[...] = jnp.concatenate(picks, axis=0)
    rank_ref[...] = jnp.concatenate(r_k, axis=0).astype(I32)
    gate_ref[...] = jnp.concatenate([s / total * np.float32(ROUTED_SCALE) for s in s_k], axis=0)


def _route(scores_t, bias, tok0, T):
    ne = scores_t.shape[0]
    tm = ROUTE_TILE
    first = tok0 // tm
    before = jnp.asarray(np.triu(np.ones((tm, tm), np.float32), k=1), BF16)
    out_spec = pl.BlockSpec((TOP_K, tm), lambda i: (0, i))
    return pl.pallas_call(
        _route_kernel,
        grid=(T // tm,),
        in_specs=[pl.BlockSpec((ne, tm), lambda i: (0, first + i)), _const_spec(bias.shape),
                  _const_spec(before.shape)],
        out_specs=[out_spec, out_spec, out_spec, _const_spec((ne, 1))],
        out_shape=[jax.ShapeDtypeStruct((TOP_K, T), I32), jax.ShapeDtypeStruct((TOP_K, T), I32),
                   jax.ShapeDtypeStruct((TOP_K, T), F32), jax.ShapeDtypeStruct((ne, 1), F32)],
        scratch_shapes=[pltpu.VMEM((ne, 1), F32)],
        compiler_params=pltpu.CompilerParams(dimension_semantics=("arbitrary",), vmem_limit_bytes=VMEM_LIMIT),
        name="route_topk",
    )(scores_t, bias, before)


def _dest_kernel(pstart_ref, eidx_ref, rank_ref, out_ref):
    eidx = eidx_ref[...]
    start = jnp.zeros(eidx.shape, I32)
    for e in range(N_EXPERTS):
        start = jnp.where(eidx == e, pstart_ref[e], start)
    out_ref[...] = start + rank_ref[...]


def _dest_rows(eidx_t, rank_t, pstart):
    T = eidx_t.shape[1]
    tm = DEST_TILE
    tok_spec = pl.BlockSpec((TOP_K, tm), lambda i: (0, i))
    return pl.pallas_call(
        _dest_kernel,
        grid=(T // tm,),
        in_specs=[pl.BlockSpec(memory_space=pltpu.SMEM), tok_spec, tok_spec],
        out_specs=tok_spec,
        out_shape=jax.ShapeDtypeStruct((TOP_K, T), I32),
        compiler_params=pltpu.CompilerParams(dimension_semantics=("arbitrary",)),
        name="moe_dest_rows",
    )(pstart, eidx_t, rank_t)


def _sc_workers():
    info = plsc.get_sparse_core_info()
    return info.num_cores, info.num_cores * info.num_subcores


def _sc_scatter_rows(rows, row0, dest, n_out):
    n_dst, n_chunks, ch = dest.shape
    width = rows.shape[1]
    n_cores, n_workers = _sc_workers()
    per_w = n_chunks // n_workers
    assert n_chunks % n_workers == 0 and per_w % 8 == 0 and row0 + n_chunks * ch <= rows.shape[0]

    def body(rows_hbm, dest_hbm, out_hbm, idx_v, buf, lsem, ssem):
        wid = lax.axis_index("s") * n_cores + lax.axis_index("c")
        c0 = wid * per_w
        for k in range(n_dst):
            pltpu.sync_copy(dest_hbm.at[k, pl.ds(pl.multiple_of(c0, 8), per_w)], idx_v.at[k])

        def load(c, b):
            return pltpu.make_async_copy(rows_hbm.at[pl.ds(row0 + (c0 + c) * ch, ch)], buf.at[b], lsem.at[b])

        def scatter(c, b, k):
            return pltpu.make_async_copy(buf.at[b], out_hbm.at[idx_v.at[k, c]], ssem.at[b])

        load(0, 0).start()

        @pl.loop(0, per_w, step=2)
        def _(c):
            for b in range(2):
                cc = c + b
                load(cc, b).wait()

                @pl.when(cc >= 1)
                def _():
                    for k in range(n_dst):
                        scatter(cc - 1, 1 - b, k).wait()

                @pl.when(cc + 1 < per_w)
                def _():
                    load(cc + 1, 1 - b).start()

                for k in range(n_dst):
                    scatter(cc, b, k).start()

        for k in range(n_dst):
            scatter(per_w - 1, 1, k).wait()

    mesh = plsc.VectorSubcoreMesh(core_axis_name="c", subcore_axis_name="s")
    return pl.kernel(
        body, out_type=jax.ShapeDtypeStruct((n_out, width), rows.dtype), mesh=mesh,
        scratch_types=[pltpu.VMEM((n_dst, per_w, ch), I32), pltpu.VMEM((2, ch, width), rows.dtype),
                       pltpu.SemaphoreType.DMA((2,)), pltpu.SemaphoreType.DMA((2,))],
        name="moe_dispatch_scatter",
    )(rows, dest)


def _sc_gather_weighted_sum(table, idx, gates):
    n_src, n_tok = idx.shape
    ct = SC_SUM_CHUNK
    n_chunks = n_tok // ct
    half = table.shape[1]
    lanes = plsc.get_sparse_core_info().num_lanes
    n_cores, n_workers = _sc_workers()
    per_w = n_chunks // n_workers
    assert n_chunks % n_workers == 0 and per_w % 2 == 0 and half % lanes == 0

    def body(table_hbm, idx_hbm, gate_hbm, out_hbm, idx_v, gate_v, buf, out_v, gsem, wsem):
        wid = lax.axis_index("s") * n_cores + lax.axis_index("c")
        c0 = wid * per_w
        pltpu.sync_copy(idx_hbm.at[:, pl.ds(c0 * ct, per_w * ct)], idx_v)
        pltpu.sync_copy(gate_hbm.at[:, pl.ds(c0 * ct, per_w * ct)], gate_v)

        def gather(c, b, k):
            return pltpu.make_async_copy(table_hbm.at[idx_v.at[k, pl.ds(c * ct, ct)]], buf.at[b, k], gsem.at[b])

        def write(c, b):
            return pltpu.make_async_copy(out_v.at[b], out_hbm.at[pl.ds((c0 + c) * ct, ct)], wsem.at[b])

        for k in range(n_src):
            gather(0, 0, k).start()

        @pl.loop(0, per_w, step=2)
        def _(c):
            for b in range(2):
                cc = c + b
                for k in range(n_src):
                    gather(cc, b, k).wait()

                @pl.when(cc + 1 < per_w)
                def _():
                    for k in range(n_src):
                        gather(cc + 1, 1 - b, k).start()

                @pl.when(cc >= 2)
                def _():
                    write(cc - 2, b).wait()

                @pl.loop(0, ct)
                def _(t):
                    tok = jnp.full((lanes,), cc * ct + t, I32)
                    g = [plsc.load_gather(gate_v, [jnp.full((lanes,), k, I32), tok]) for k in range(n_src)]

                    @plsc.parallel_loop(0, half, step=lanes, unroll=2)
                    def _(col):
                        lo = jnp.zeros((lanes,), F32)
                        hi = jnp.zeros((lanes,), F32)
                        for k in range(n_src):
                            w = buf[b, k, t, pl.ds(col, lanes)]
                            w_lo, w_hi = plsc.unpack(plsc.bitcast(w, BF16), format=plsc.PackFormat.INTERLEAVED)
                            lo = lo + g[k] * w_lo
                            hi = hi + g[k] * w_hi
                        out_v[b, t, pl.ds(col, lanes)] = lo
                        out_v[b, t, pl.ds(half + col, lanes)] = hi

                write(cc, b).start()

        write(per_w - 2, 0).wait()
        write(per_w - 1, 1).wait()

    mesh = plsc.VectorSubcoreMesh(core_axis_name="c", subcore_axis_name="s")
    return pl.kernel(
        body, out_type=jax.ShapeDtypeStruct((n_chunks * ct, 2 * half), F32), mesh=mesh,
        scratch_types=[pltpu.VMEM((n_src, per_w * ct), I32), pltpu.VMEM((n_src, per_w * ct), F32),
                       pltpu.VMEM((2, n_src, ct, half), I32), pltpu.VMEM((2, ct, 2 * half), F32),
                       pltpu.SemaphoreType.DMA((2,)), pltpu.SemaphoreType.DMA((2,))],
        compiler_params=pltpu.CompilerParams(needs_layout_passes=False),
        name="moe_combine_gather_sum",
    )(table, idx, gates)


def _sc_pack_row_pairs(w):
    R, C = w.shape
    lanes = plsc.get_sparse_core_info().num_lanes
    cr = SC_PACK_CHUNK_WORDS // C
    n_chunks = R // cr
    n_cores, n_workers = _sc_workers()
    per_w = n_chunks // n_workers
    assert R % cr == 0 and n_chunks % n_workers == 0 and per_w % 2 == 0 and cr % 2 == 0 and C % lanes == 0

    def body(w_hbm, out_hbm, in_v, out_v, lsem, ssem):
        wid = lax.axis_index("s") * n_cores + lax.axis_index("c")
        c0 = wid * per_w

        def load(c, b):
            return pltpu.make_async_copy(w_hbm.at[pl.ds((c0 + c) * cr, cr)], in_v.at[b], lsem.at[b])

        def store(c, b):
            return pltpu.make_async_copy(out_v.at[b], out_hbm.at[pl.ds((c0 + c) * (cr // 2), cr // 2)],
                                         ssem.at[b])

        load(0, 0).start()

        @pl.loop(0, per_w, step=2)
        def _(c):
            for b in range(2):
                cc = c + b
                load(cc, b).wait()

                @pl.when(cc + 1 < per_w)
                def _():
                    load(cc + 1, 1 - b).start()

                @pl.when(cc >= 2)
                def _():
                    store(cc - 2, b).wait()

                @pl.loop(0, cr // 2)
                def _(i):
                    @plsc.parallel_loop(0, C, step=lanes, unroll=4)
                    def _(col):
                        even = in_v[b, 2 * i, pl.ds(col, lanes)]
                        odd = in_v[b, 2 * i + 1, pl.ds(col, lanes)]
                        pair = plsc.pack(even, odd, format=plsc.PackFormat.INTERLEAVED)
                        out_v[b, i, pl.ds(col, lanes)] = plsc.bitcast(pair, I32)

                store(cc, b).start()

        store(per_w - 2, 0).wait()
        store(per_w - 1, 1).wait()

    mesh = plsc.VectorSubcoreMesh(core_axis_name="c", subcore_axis_name="s")
    return pl.kernel(
        body, out_type=jax.ShapeDtypeStruct((R // 2, C), I32), mesh=mesh,
        scratch_types=[pltpu.VMEM((2, cr, C), F32), pltpu.VMEM((2, cr // 2, C), I32),
                       pltpu.SemaphoreType.DMA((2,)), pltpu.SemaphoreType.DMA((2,))],
        compiler_params=pltpu.CompilerParams(needs_layout_passes=False),
        name="expert_weights_bf16",
    )(w)


def _expert_kernel(be_ref, ue_ref, nu_ref, xs_hbm, wg_hbm, wu_hbm, wd_hbm, y_hbm,
                   xbuf, ybuf, wg_v, wu_v, wd_v, xsem, ysem, wsem):
    n_used = nu_ref[0]
    n_exp = nu_ref[1]
    blk = MOE_BLOCK
    rows_gu = D_MODEL // 2
    rows_d = EXPERT_DIM // 2

    def rows(j):
        return pl.ds(pl.multiple_of(j * blk, blk), blk)

    def x_copy(j, p):
        return pltpu.make_async_copy(xs_hbm.at[rows(j)], xbuf.at[p], xsem.at[p])

    def y_copy(j, p):
        return pltpu.make_async_copy(ybuf.at[p], y_hbm.at[rows(j)], ysem.at[p])

    def w_copies(q, s):
        e = ue_ref[q]
        gu = pl.ds(pl.multiple_of(e * rows_gu, rows_gu), rows_gu)
        dn = pl.ds(pl.multiple_of(e * rows_d, rows_d), rows_d)
        return (pltpu.make_async_copy(wg_hbm.at[gu], wg_v.at[s], wsem.at[s, 0]),
                pltpu.make_async_copy(wu_hbm.at[gu], wu_v.at[s], wsem.at[s, 1]),
                pltpu.make_async_copy(wd_hbm.at[dn], wd_v.at[s], wsem.at[s, 2]))

    for q0 in range(W_SLOTS - 1):
        @pl.when(q0 < n_exp)
        def _():
            for c in w_copies(q0, q0):
                c.start()

    for j0 in range(X_SLOTS):
        @pl.when(j0 < n_used)
        def _():
            x_copy(j0, j0).start()

    def block_step(j, p, q):
        is_new = (j == 0) | (be_ref[j] != be_ref[jnp.maximum(j - 1, 0)])
        q = q + is_new.astype(I32)

        s = q % W_SLOTS

        @pl.when(is_new)
        def _():
            for c in w_copies(q, s):
                c.wait()

            @pl.when(q + W_SLOTS - 1 < n_exp)
            def _():
                for c in w_copies(q + W_SLOTS - 1, (q + W_SLOTS - 1) % W_SLOTS):
                    c.start()

        x_copy(j, p).wait()

        @pl.when(j >= X_SLOTS)
        def _():
            y_copy(j - X_SLOTS, p).wait()

        wg = pltpu.bitcast(wg_v[s], BF16)
        wu = pltpu.bitcast(wu_v[s], BF16)
        wd = pltpu.bitcast(wd_v[s], BF16)
        lo, hi = _unpack_bf16_pairs(xbuf[p])
        x = jnp.concatenate([lo.astype(BF16), hi.astype(BF16)], axis=1)
        g = jnp.dot(x, wg, preferred_element_type=F32)
        u = jnp.dot(x, wu, preferred_element_type=F32)
        hb = (g * jax.nn.sigmoid(g) * u).astype(BF16)
        ybuf[p] = _pack_bf16_pairs(jnp.dot(hb, wd, preferred_element_type=F32))
        y_copy(j, p).start()

        @pl.when(j + X_SLOTS < n_used)
        def _():
            x_copy(j + X_SLOTS, p).start()

        return q

    def group(m, q):
        q = block_step(X_SLOTS * m, 0, q)
        for p in range(1, X_SLOTS):
            j = X_SLOTS * m + p
            q = lax.cond(j < n_used, functools.partial(block_step, j, p), lambda q: q, q)
        return q

    lax.fori_loop(0, (n_used + X_SLOTS - 1) // X_SLOTS, group, jnp.int32(-1))

    for back in range(X_SLOTS, 0, -1):
        b = n_used - back
        for p in range(X_SLOTS):
            @pl.when((b >= 0) & (b % X_SLOTS == p))
            def _():
                y_copy(b, p).wait()


def _experts(block_e, used_e, counts2, xs, wg, wu, wd):
    n_rows, half = xs.shape
    smem = pl.BlockSpec(memory_space=pltpu.SMEM)
    hbm = pl.BlockSpec(memory_space=pl.ANY)
    return pl.pallas_call(
        _expert_kernel,
        in_specs=[smem, smem, smem, hbm, hbm, hbm, hbm],
        out_specs=hbm,
        out_shape=jax.ShapeDtypeStruct((n_rows, half), I32),
        scratch_shapes=[pltpu.VMEM((X_SLOTS, MOE_BLOCK, half), I32), pltpu.VMEM((X_SLOTS, MOE_BLOCK, half), I32),
                        pltpu.VMEM((W_SLOTS, D_MODEL // 2, EXPERT_DIM), I32),
                        pltpu.VMEM((W_SLOTS, D_MODEL // 2, EXPERT_DIM), I32),
                        pltpu.VMEM((W_SLOTS, EXPERT_DIM // 2, D_MODEL), I32),
                        pltpu.SemaphoreType.DMA((X_SLOTS,)), pltpu.SemaphoreType.DMA((X_SLOTS,)),
                        pltpu.SemaphoreType.DMA((W_SLOTS, 3))],
        compiler_params=pltpu.CompilerParams(vmem_limit_bytes=VMEM_LIMIT),
        name="moe_experts",
    )(block_e, used_e, counts2, xs, wg, wu, wd)


def _combine_kernel(routed_ref, x1_ref, swgu_ref, swd_ref, g2_ref, b2_ref, *rest, alpha):
    out_ref = rest[-1]
    x1 = x1_ref[...]
    gu = jnp.dot(x1.astype(BF16), swgu_ref[...], preferred_element_type=F32)
    g = gu[:, :EXPERT_DIM]
    u = gu[:, EXPERT_DIM:]
    shared = jnp.dot((g * jax.nn.sigmoid(g) * u).astype(BF16), swd_ref[...], preferred_element_type=F32)
    out_ref[...] = _layer_norm(alpha * x1 + (routed_ref[...] + shared), g2_ref[...], b2_ref[...])


def _combine(routed, x1, tok0, swgu, swd, g2, b2, alpha, out_prev):
    T, D = x1.shape
    n_tok = routed.shape[0]
    tc = COMBINE_TILE
    first = tok0 // tc
    consts = (swgu, swd, g2, b2)
    args = [routed, x1, *consts]
    in_specs = [pl.BlockSpec((tc, D), lambda i: (i, 0)),
                pl.BlockSpec((tc, D), lambda i: (first + i, 0))] + [_const_spec(c.shape) for c in consts]
    aliases = {}
    if out_prev is not None:
        aliases = {len(args): 0}
        args.append(out_prev)
        in_specs.append(pl.BlockSpec(memory_space=pl.ANY))
    return pl.pallas_call(
        functools.partial(_combine_kernel, alpha=alpha),
        grid=(n_tok // tc,),
        in_specs=in_specs,
        out_specs=pl.BlockSpec((tc, D), lambda i: (first + i, 0)),
        out_shape=jax.ShapeDtypeStruct((T, D), F32),
        input_output_aliases=aliases,
        compiler_params=pltpu.CompilerParams(dimension_semantics=("arbitrary",), vmem_limit_bytes=VMEM_LIMIT),
        name="moe_combine_norm",
    )(*args)


def _mixer(x, in_w, in_b, ng, nb, spatial_w, spatial_b, proj_a_w, proj_b_w, out_w, ln1_g, ln1_b,
           router_w, alpha, anchors):
    gw, aw, D = GMLP_WIDTH, ATTN_WIDTH, D_MODEL
    w = in_w.astype(BF16)
    q0 = 2 * gw
    wuv, buv = w[:, :q0], in_b[None, :q0]
    watt = jnp.stack([jnp.concatenate([w[:, q0 + s * aw + p * GROUP_WIDTH:q0 + s * aw + (p + 1) * GROUP_WIDTH]
                                       for s in range(3)], axis=1) for p in range(len(DILATED_PATTERNS))])
    batt = jnp.stack([jnp.concatenate([in_b[q0 + s * aw + p * GROUP_WIDTH:q0 + s * aw + (p + 1) * GROUP_WIDTH]
                                       for s in range(3)])[None] for p in range(len(DILATED_PATTERNS))])
    g0 = q0 + 3 * aw
    wg, bg = w[:, g0:], in_b[None, g0:]
    sb = jnp.repeat(spatial_b.T, gw // GMLP_GROUPS, axis=1)
    ya, gb, a1, a4, a16 = _input_projection(
        x, wuv, buv, watt, batt, wg, bg, ng[None], nb[None], spatial_w, sb, proj_a_w.astype(BF16))
    attn_outs = [_dilated_attention(a, p, d) for p, (a, (_, d)) in enumerate(zip((a1, a4, a16), DILATED_PATTERNS))]
    rw_t = router_w.T
    rwh = rw_t.astype(BF16)
    rwl = (rw_t - rwh.astype(F32)).astype(BF16)
    return _merge_and_norm(attn_outs, ya, gb, x, proj_b_w.astype(BF16), out_w.astype(BF16),
                           ln1_g[None], ln1_b[None], rwh, rwl, alpha, anchors)


def _moe(x1, x1p, scores_t, router_bias, w_gate, w_up, w_down, sw_gate, sw_up, sw_down, ln2_g, ln2_b, alpha):
    swgu = jnp.concatenate([sw_gate, sw_up], axis=1).astype(BF16)
    swd = sw_down.astype(BF16)
    out = None
    tok0 = 0
    for part, eighths in enumerate(MOE_TOKEN_SPLIT):
        n_tok = x1.shape[0] * eighths // 8
        n_blocks = (n_tok * TOP_K + N_EXPERTS * (MOE_BLOCK - 1)) // MOE_BLOCK
        eidx_t, rank_t, gate_t, counts = _route(scores_t, router_bias[:, None], tok0, n_tok)
        counts = counts[:, 0].astype(I32)
        padded = (counts + MOE_BLOCK - 1) // MOE_BLOCK * MOE_BLOCK
        pend = jnp.cumsum(padded).astype(I32)
        pstart = pend - padded
        block_starts = jnp.arange(n_blocks, dtype=I32) * MOE_BLOCK
        block_e = jnp.minimum(jnp.sum((pend[None, :] <= block_starts[:, None]).astype(I32), axis=1),
                              N_EXPERTS - 1)
        used = counts > 0
        used_e = jnp.argsort(jnp.logical_not(used), stable=True).astype(I32)
        counts2 = jnp.stack([pend[-1] // MOE_BLOCK, jnp.sum(used.astype(I32))]).astype(I32)
        dest_t = _dest_rows(eidx_t, rank_t, pstart)
        xs = _sc_scatter_rows(x1p, tok0, dest_t.reshape(TOP_K, n_tok // SC_CHUNK, SC_CHUNK), n_blocks * MOE_BLOCK)
        y_rows = _experts(block_e, used_e, counts2, xs, w_gate, w_up, w_down)
        pieces = MOE_TAIL_PIECES if part == len(MOE_TOKEN_SPLIT) - 1 else 1
        n_piece = n_tok // pieces
        for q in range(pieces):
            cols = slice(q * n_piece, (q + 1) * n_piece)
            routed = _sc_gather_weighted_sum(y_rows, dest_t[:, cols], gate_t[:, cols])
            out = _combine(routed, x1, tok0 + q * n_piece, swgu, swd, ln2_g[None], ln2_b[None], alpha, out)
        tok0 += n_tok
    return out


def kernel(x, in_w, in_b, gmlp_norm_g, gmlp_norm_b, spatial_w, spatial_b, proj_a_w, proj_b_w, out_w,
           ln1_g, ln1_b, router_w, router_bias, expert_w_gate, expert_w_up, expert_w_down,
           shared_w_gate, shared_w_up, shared_w_down, ln2_g, ln2_b):
    B, S, D = x.shape
    depth = in_w.shape[0]
    alpha = np.float32((2.0 * depth) ** 0.25)
    for l in range(depth):
        packed = [_sc_pack_row_pairs(w[l].reshape(-1, w.shape[-1]))
                  for w in (expert_w_gate, expert_w_up, expert_w_down)]
        x1, x1p, scores_t = _mixer(x, in_w[l], in_b[l], gmlp_norm_g[l], gmlp_norm_b[l], spatial_w[l],
                                 spatial_b[l], proj_a_w[l], proj_b_w[l], out_w[l], ln1_g[l], ln1_b[l],
                                 router_w[l], alpha, [p[:8] for p in packed])
        out = _moe(x1.reshape(B * S, D), x1p.reshape(B * S, D // 2), scores_t,
                   router_bias[l], *packed,
                   shared_w_gate[l], shared_w_up[l], shared_w_down[l], ln2_g[l], ln2_b[l], alpha)
        x = out.reshape(B, S, D)
    return x
```

```python
import functools
import math

import numpy as np
import jax
import jax.numpy as jnp
from jax import lax
from jax.experimental import pallas as pl
from jax.experimental.pallas import tpu as pltpu
from jax.experimental.pallas import tpu_sc as plsc

F32 = jnp.float32
BF16 = jnp.bfloat16
U32 = jnp.uint32
I32 = jnp.int32

D_MODEL = 1024
GMLP_WIDTH = 1024
GMLP_GROUPS = 8
GMLP_CHUNK = 128
HEAD_DIM = 64
DILATED_PATTERNS = ((128, 1), (512, 4), (2048, 16))
HEADS_PER_GROUP = 4
GROUP_WIDTH = HEADS_PER_GROUP * HEAD_DIM
ATTN_WIDTH = GROUP_WIDTH * len(DILATED_PATTERNS)
ATTN_BLOCK = 128
N_EXPERTS = 256
TOP_K = 8
N_EXPERT_GROUPS = 8
TOPK_GROUPS = 4
EXPERT_DIM = 256
ROUTED_SCALE = 2.5
LN_EPS = 1e-5
LANES = 128
MASKED_SCORE = -1e30

PROJ_TILE = 512
ATTN_QBLOCKS = 16
MERGE_TILE = 1024
ROUTE_TILE = 512
DEST_TILE = 2048
MOE_BLOCK = 256
SC_PACK_CHUNK_WORDS = 16384
SC_CHUNK = 64
SC_SUM_CHUNK = 8
MOE_TOKEN_SPLIT = (4, 4)
MOE_TAIL_PIECES = 4
X_SLOTS = 4
W_SLOTS = 3
COMBINE_TILE = 1024
VMEM_LIMIT = 56 * 1024 * 1024


def _layer_norm(y, g, b):
    mu = jnp.mean(y, axis=-1, keepdims=True)
    yc = y - mu
    var = jnp.mean(yc * yc, axis=-1, keepdims=True)
    return yc * lax.rsqrt(var + LN_EPS) * g + b


def _gelu(x):
    return 0.5 * x * (1.0 + lax.erf(x * np.float32(math.sqrt(0.5))))


def _pack_bf16_pairs(x):
    w = x.shape[1] // 2
    bits = pltpu.bitcast(x.astype(BF16).astype(F32), U32)
    return pltpu.bitcast((bits[:, :w] >> 16) | (bits[:, w:] & jnp.uint32(0xFFFF0000)), I32)


def _unpack_bf16_pairs(words):
    w = pltpu.bitcast(words, U32)
    lo = pltpu.bitcast(w << 16, F32)
    hi = pltpu.bitcast(w & jnp.uint32(0xFFFF0000), F32)
    return lo, hi


def _const_spec(shape):
    nd = len(shape)
    return pl.BlockSpec(shape, lambda *_: (0,) * nd)


def _proj_kernel(x_ref, wuv_ref, buv_ref, watt_ref, batt_ref, wg_ref, bg_ref, ng_ref, nb_ref,
                 sw_ref, sb_ref, pa_ref, ya_ref, gb_ref, a1_ref, a4_ref, a16_ref, xc_ref):
    tm = x_ref.shape[1]
    gw = GMLP_WIDTH
    xb = x_ref[0].astype(BF16)

    def proj(w, b):
        return jnp.dot(xb, w, preferred_element_type=F32) + b

    h_v = proj(wuv_ref[:, gw:], buv_ref[:, gw:])
    h_u = proj(wuv_ref[:, :gw], buv_ref[:, :gw])
    v = _layer_norm(_gelu(h_v), ng_ref[...], nb_ref[...]).astype(BF16)
    h_ga = proj(wg_ref[:, :D_MODEL], bg_ref[:, :D_MODEL])
    u = _gelu(h_u)

    cw = gw // GMLP_GROUPS
    row = lax.broadcasted_iota(I32, (GMLP_CHUNK, GMLP_CHUNK), 0)
    col = lax.broadcasted_iota(I32, (GMLP_CHUNK, GMLP_CHUNK), 1)
    ws = [jnp.where(row >= col, sw_ref[g], 0.0).astype(BF16) for g in range(GMLP_GROUPS)]
    chunks = []
    for c in range(tm // GMLP_CHUNK):
        vc = v[c * GMLP_CHUNK:(c + 1) * GMLP_CHUNK]
        cols = [jnp.dot(ws[g], vc[:, g * cw:(g + 1) * cw], preferred_element_type=F32)
                for g in range(GMLP_GROUPS)]
        chunks.append(jnp.concatenate(cols, axis=1) + sb_ref[...])
    vmix = jnp.concatenate(chunks, axis=0)
    h_gb = proj(wg_ref[:, D_MODEL:], bg_ref[:, D_MODEL:])
    ga = jax.nn.sigmoid(h_ga)
    ya = jnp.dot((u * vmix).astype(BF16), pa_ref[...], preferred_element_type=F32)
    gb_ref[0] = jax.nn.sigmoid(h_gb).astype(BF16)

    n_chunks = x_ref.shape[2] // LANES
    for c in range(n_chunks):
        xc_ref[c] = x_ref[0, :, c * LANES:(c + 1) * LANES]

    def attn_proj(p, d):
        n = tm // d
        if d == 1:
            xp = xb
        else:
            xp = jnp.concatenate(
                [jnp.concatenate([xc_ref[c, pl.ds(r, n, stride=d), :] for c in range(n_chunks)], axis=1)
                 for r in range(d)], axis=0).astype(BF16)
        return jnp.dot(xp, watt_ref[p], preferred_element_type=F32)

    def attn_store(p, d, h, a_ref):
        n = tm // d
        h = (h + batt_ref[p]).astype(BF16)
        for r in range(d):
            a_ref[0, r] = h[r * n:(r + 1) * n]

    a_refs = (a1_ref, a4_ref, a16_ref)
    dils = [d for _, d in DILATED_PATTERNS]
    h_prev = attn_proj(0, dils[0])
    ya_ref[0] = (ga * ya).astype(BF16)
    for p in range(1, len(dils)):
        h_next = attn_proj(p, dils[p])
        attn_store(p - 1, dils[p - 1], h_prev, a_refs[p - 1])
        h_prev = h_next
    attn_store(len(dils) - 1, dils[-1], h_prev, a_refs[-1])


def _input_projection(x, wuv, buv, watt, batt, wg, bg, ng, nb, sw, sb, pa):
    B, S, D = x.shape
    tm = PROJ_TILE
    grid = (B, S // tm)
    out_shape = [jax.ShapeDtypeStruct((B, S, D), BF16), jax.ShapeDtypeStruct((B, S, D), BF16)]
    out_specs = [pl.BlockSpec((1, tm, D), lambda b, t: (b, t, 0)),
                 pl.BlockSpec((1, tm, D), lambda b, t: (b, t, 0))]
    for _, d in DILATED_PATTERNS:
        out_shape.append(jax.ShapeDtypeStruct((B, d, S // d, ATTN_WIDTH), BF16))
        out_specs.append(pl.BlockSpec((1, d, tm // d, ATTN_WIDTH), lambda b, t: (b, 0, t, 0)))
    consts = (wuv, buv, watt, batt, wg, bg, ng, nb, sw, sb, pa)
    return pl.pallas_call(
        _proj_kernel,
        grid=grid,
        in_specs=[pl.BlockSpec((1, tm, D), lambda b, t: (b, t, 0))] + [_const_spec(c.shape) for c in consts],
        out_specs=out_specs,
        out_shape=out_shape,
        scratch_shapes=[pltpu.VMEM((D // LANES, tm, LANES), F32)],
        compiler_params=pltpu.CompilerParams(
            dimension_semantics=("arbitrary", "arbitrary"), vmem_limit_bytes=VMEM_LIMIT),
        name="input_projection",
    )(x, *consts)


def _attn_kernel(qkv_ref, bias_ref, o_ref, lse_ref, *, qblocks, rblock):
    nq = pl.program_id(2)
    gwid = GROUP_WIDTH
    blk = ATTN_BLOCK
    lane = lax.broadcasted_iota(I32, (1, gwid), 1)
    head_masks = [(lane >= h * HEAD_DIM) & (lane < (h + 1) * HEAD_DIM) for h in range(HEADS_PER_GROUP)]
    q_scales = [jnp.where(m, np.float32(HEAD_DIM ** -0.5), 0.0).astype(BF16) for m in head_masks]

    def rows_of(j):
        n = nq * qblocks + j
        return n, pl.multiple_of(n * blk, blk), pl.multiple_of(jnp.maximum(n - 1, 0) * blk, blk)

    def scores(unit):
        ri, j = unit
        n, q0, p0 = rows_of(j)
        q = qkv_ref[0, ri, pl.ds(q0, blk), 0:gwid]
        kk = jnp.concatenate([qkv_ref[0, ri, pl.ds(p0, blk), gwid:2 * gwid],
                              qkv_ref[0, ri, pl.ds(q0, blk), gwid:2 * gwid]], axis=0)
        qs = jnp.concatenate([q * s for s in q_scales], axis=0)
        s = lax.dot_general(qs, kk, (((1,), (1,)), ((), ())), preferred_element_type=F32)
        return s + bias_ref[jnp.where(n == 0, 1, 0)]

    def finish(unit, s):
        ri, j = unit
        _, q0, p0 = rows_of(j)
        vv = jnp.concatenate([qkv_ref[0, ri, pl.ds(p0, blk), 2 * gwid:3 * gwid],
                              qkv_ref[0, ri, pl.ds(q0, blk), 2 * gwid:3 * gwid]], axis=0)
        m = jnp.max(s, axis=1, keepdims=True)
        p = jnp.exp(s - m)
        den = jnp.sum(p, axis=1, keepdims=True)
        pv = jnp.dot(p.astype(BF16), vv, preferred_element_type=F32)
        on = pv / den
        lse = m + jnp.log(den)
        o = jnp.zeros((blk, gwid), F32)
        l = jnp.zeros((blk, gwid), F32)
        for h in range(HEADS_PER_GROUP):
            o = jnp.where(head_masks[h], on[h * blk:(h + 1) * blk], o)
            l = jnp.where(head_masks[h], lse[h * blk:(h + 1) * blk], l)
        o_ref[0, ri, j * blk:(j + 1) * blk, :] = o.astype(BF16)
        lse_ref[0, ri, j * blk:(j + 1) * blk, :] = l

    units = [(ri, j) for ri in range(rblock) for j in range(qblocks)]
    s_next = scores(units[0])
    for i, unit in enumerate(units):
        s_cur = s_next
        if i + 1 < len(units):
            s_next = scores(units[i + 1])
        finish(unit, s_cur)


def _alibi_slopes(n):
    def pow2_slopes(m):
        start = 2.0 ** (-8.0 / m)
        return [start ** (i + 1) for i in range(m)]
    p = 2 ** int(math.floor(math.log2(n)))
    s = pow2_slopes(p)
    if p < n:
        s = s + pow2_slopes(2 * p)[0::2][: n - p]
    return np.array(sorted(s, reverse=True), dtype=np.float32)


def _attn_bias_tables(group, dilation):
    blk = ATTN_BLOCK
    slopes = _alibi_slopes(HEADS_PER_GROUP * len(DILATED_PATTERNS))
    slopes = slopes[group * HEADS_PER_GROUP:(group + 1) * HEADS_PER_GROUP]
    qi = np.arange(blk)[:, None]
    ki = np.arange(2 * blk)[None, :]
    delta = blk + qi - ki
    band = (delta >= 0) & (delta <= blk)
    bias = -slopes[:, None, None] * (delta * dilation).astype(np.float32)[None]
    full = np.where(band[None], bias, np.float32(MASKED_SCORE)).astype(np.float32)
    first = np.where((ki >= blk)[None], full, np.float32(MASKED_SCORE)).astype(np.float32)
    return full.reshape(HEADS_PER_GROUP * blk, 2 * blk), first.reshape(HEADS_PER_GROUP * blk, 2 * blk)


def _dilated_attention(qkv, group, dilation):
    B, d, sd, _ = qkv.shape
    qblocks = min(ATTN_QBLOCKS, sd // ATTN_BLOCK)
    rblock = min(d, ATTN_QBLOCKS // qblocks)
    rows = qblocks * ATTN_BLOCK
    bias = np.stack(_attn_bias_tables(group, dilation))
    grid = (B, d // rblock, sd // rows)
    out_spec = pl.BlockSpec((1, rblock, rows, GROUP_WIDTH), lambda b, r, n: (b, r, n, 0))
    return pl.pallas_call(
        functools.partial(_attn_kernel, qblocks=qblocks, rblock=rblock),
        grid=grid,
        in_specs=[pl.BlockSpec((1, rblock, sd, ATTN_WIDTH), lambda b, r, n: (b, r, 0, 0)),
                  _const_spec(bias.shape)],
        out_specs=[out_spec, out_spec],
        out_shape=[jax.ShapeDtypeStruct((B, d, sd, GROUP_WIDTH), BF16),
                   jax.ShapeDtypeStruct((B, d, sd, GROUP_WIDTH), F32)],
        compiler_params=pltpu.CompilerParams(
            dimension_semantics=("arbitrary", "arbitrary", "arbitrary"), vmem_limit_bytes=VMEM_LIMIT),
        name=f"dilated_attention_d{dilation}",
    )(qkv, jnp.asarray(bias))


def _merge_kernel(o1_ref, l1_ref, o4_ref, l4_ref, o16_ref, l16_ref, ya_ref, gb_ref, x_ref,
                  pb_ref, ow_ref, g1_ref, b1_ref, rwh_ref, rwl_ref, *rest, alpha, n_anchors):
    x1_ref, x1p_ref, sc_ref, so4, sl4, so16, sl16 = rest[n_anchors:]
    tm = x_ref.shape[1]
    n_chunks = GROUP_WIDTH // LANES
    for (o_ref, l_ref, so, sl, d) in ((o4_ref, l4_ref, so4, sl4, 4), (o16_ref, l16_ref, so16, sl16, 16)):
        n = tm // d
        for r in range(d):
            o_r = o_ref[0, r].astype(F32)
            l_r = l_ref[0, r]
            for c in range(n_chunks):
                so[c, pl.ds(r, n, stride=d), :] = o_r[:, c * LANES:(c + 1) * LANES]
                sl[c, pl.ds(r, n, stride=d), :] = l_r[:, c * LANES:(c + 1) * LANES]

    def natural(s):
        return jnp.concatenate([s[c] for c in range(n_chunks)], axis=1)

    l1 = l1_ref[0, 0]
    l4 = natural(sl4)
    l16 = natural(sl16)
    lmax = jnp.maximum(jnp.maximum(l1, l4), l16)
    e1 = jnp.exp(l1 - lmax)
    e4 = jnp.exp(l4 - lmax)
    e16 = jnp.exp(l16 - lmax)
    yb = (e1 * o1_ref[0, 0].astype(F32) + e4 * natural(so4) + e16 * natural(so16)) / (e1 + e4 + e16)
    ybp = jnp.dot(yb.astype(BF16), pb_ref[...], preferred_element_type=F32)
    merged = ya_ref[0] + gb_ref[0] * ybp.astype(BF16)
    mix = jnp.dot(merged, ow_ref[...], preferred_element_type=F32)
    x1 = _layer_norm(alpha * x_ref[0] + mix, g1_ref[...], b1_ref[...])
    x1_ref[0] = x1
    x1p_ref[0] = _pack_bf16_pairs(x1)
    hi = x1.astype(BF16)
    lo = (x1 - hi.astype(F32)).astype(BF16)
    def logits_t(w_ref, xt):
        return lax.dot_general(w_ref[...], xt, (((1,), (1,)), ((), ())), preferred_element_type=F32)
    sc_ref[...] = jax.nn.sigmoid(logits_t(rwh_ref, hi) + logits_t(rwh_ref, lo) + logits_t(rwl_ref, hi))


def _merge_and_norm(attn_outs, ya, gb, x, pb, ow, g1, b1, rwh, rwl, alpha, anchors):
    B, S, D = x.shape
    tm = MERGE_TILE
    in_specs = []
    args = []
    for (o, l), (_, d) in zip(attn_outs, DILATED_PATTERNS):
        spec = pl.BlockSpec((1, d, tm // d, GROUP_WIDTH), lambda b, t: (b, 0, t, 0))
        in_specs += [spec, spec]
        args += [o, l]
    tok_spec = pl.BlockSpec((1, tm, D), lambda b, t: (b, t, 0))
    in_specs += [tok_spec, tok_spec, tok_spec]
    args += [ya, gb, x]
    consts = (pb, ow, g1, b1, rwh, rwl)
    in_specs += [_const_spec(c.shape) for c in consts]
    in_specs += [pl.BlockSpec(memory_space=pl.ANY) for _ in anchors]
    return pl.pallas_call(
        functools.partial(_merge_kernel, alpha=alpha, n_anchors=len(anchors)),
        grid=(B, S // tm),
        in_specs=in_specs,
        out_specs=[tok_spec,
                   pl.BlockSpec((1, tm, D // 2), lambda b, t: (b, t, 0)),
                   pl.BlockSpec((N_EXPERTS, tm), lambda b, t: (0, b * (S // tm) + t))],
        out_shape=[jax.ShapeDtypeStruct((B, S, D), F32),
                   jax.ShapeDtypeStruct((B, S, D // 2), I32),
                   jax.ShapeDtypeStruct((N_EXPERTS, B * S), F32)],
        scratch_shapes=[pltpu.VMEM((GROUP_WIDTH // LANES, tm, LANES), F32) for _ in range(4)],
        compiler_params=pltpu.CompilerParams(
            dimension_semantics=("arbitrary", "arbitrary"), vmem_limit_bytes=VMEM_LIMIT),
        name="merge_norm_router",
    )(*args, *consts, *anchors)


def _sortable_key(x):
    bits = pltpu.bitcast(x, I32)
    return jnp.where(bits < 0, bits ^ jnp.int32(0x7FFFFFFF), bits)


def _route_kernel(sc_ref, bias_ref, before_ref, eidx_ref, rank_ref, gate_ref, cnt_ref, carry_ref):
    ne, tm = sc_ref.shape
    gsize = ne // N_EXPERT_GROUPS
    neg_inf = np.float32(-np.inf)
    removed = jnp.int32(-2 ** 31)

    @pl.when(pl.program_id(0) == 0)
    def _():
        carry_ref[...] = jnp.zeros_like(carry_ref)

    scores = sc_ref[...]
    biased = scores + bias_ref[...]

    gsum = []
    for g in range(N_EXPERT_GROUPS):
        v = biased[g * gsize:(g + 1) * gsize]
        m1 = jnp.max(v, axis=0, keepdims=True)
        n1 = jnp.sum(jnp.where(v == m1, 1.0, 0.0), axis=0, keepdims=True)
        m2 = jnp.max(jnp.where(v < m1, v, neg_inf), axis=0, keepdims=True)
        gsum.append(m1 + jnp.where(n1 >= 2.0, m1, m2))
    gkey = _sortable_key(jnp.concatenate(gsum, axis=0))

    def pick_first_max(keys, ids, n_ids):
        m = jnp.max(keys, axis=0, keepdims=True)
        idx = jnp.min(jnp.where(keys == m, ids, n_ids), axis=0, keepdims=True)
        hit = ids == idx
        return idx, hit, jnp.where(hit, removed, keys)

    gid = lax.broadcasted_iota(I32, (N_EXPERT_GROUPS, tm), 0)
    for _ in range(TOPK_GROUPS):
        _, _, gkey = pick_first_max(gkey, gid, N_EXPERT_GROUPS)
    group_on = gkey == removed

    masked = jnp.concatenate(
        [jnp.where(group_on[g:g + 1], biased[g * gsize:(g + 1) * gsize], neg_inf)
         for g in range(N_EXPERT_GROUPS)], axis=0)
    keys = _sortable_key(masked)
    eid = lax.broadcasted_iota(I32, (ne, tm), 0)
    picks = []
    for _ in range(TOP_K):
        idx, _, keys = pick_first_max(keys, eid, ne)
        picks.append(idx)

    sel = jnp.where(keys == removed, 1.0, 0.0)
    ranks = jnp.dot(sel.astype(BF16), before_ref[...], preferred_element_type=F32) + carry_ref[...]
    carry_ref[...] = carry_ref[...] + jnp.sum(sel, axis=1, keepdims=True)
    cnt_ref[...] = carry_ref[...]

    s_k, r_k = [], []
    for idx in picks:
        hit = eid == idx
        s_k.append(jnp.sum(jnp.where(hit, scores, 0.0), axis=0, keepdims=True))
        r_k.append(jnp.sum(jnp.where(hit, ranks, 0.0), axis=0, keepdims=True))
    total = s_k[0]
    for s in s_k[1:]:
        total = total + s
    eidx_ref[...] = jnp.concatenate(picks, axis=0)
    rank_ref[...] = jnp.concatenate(r_k, axis=0).astype(I32)
    gate_ref[...] = jnp.concatenate([s / total * np.float32(ROUTED_SCALE) for s in s_k], axis=0)


def _route(scores_t, bias, tok0, T):
    ne = scores_t.shape[0]
    tm = ROUTE_TILE
    first = tok0 // tm
    before = jnp.asarray(np.triu(np.ones((tm, tm), np.float32), k=1), BF16)
    out_spec = pl.BlockSpec((TOP_K, tm), lambda i: (0, i))
    return pl.pallas_call(
        _route_kernel,
        grid=(T // tm,),
        in_specs=[pl.BlockSpec((ne, tm), lambda i: (0, first + i)), _const_spec(bias.shape),
                  _const_spec(before.shape)],
        out_specs=[out_spec, out_spec, out_spec, _const_spec((ne, 1))],
        out_shape=[jax.ShapeDtypeStruct((TOP_K, T), I32), jax.ShapeDtypeStruct((TOP_K, T), I32),
                   jax.ShapeDtypeStruct((TOP_K, T), F32), jax.ShapeDtypeStruct((ne, 1), F32)],
        scratch_shapes=[pltpu.VMEM((ne, 1), F32)],
        compiler_params=pltpu.CompilerParams(dimension_semantics=("arbitrary",), vmem_limit_bytes=VMEM_LIMIT),
        name="route_topk",
    )(scores_t, bias, before)


def _dest_kernel(pstart_ref, eidx_ref, rank_ref, out_ref):
    eidx = eidx_ref[...]
    start = jnp.zeros(eidx.shape, I32)
    for e in range(N_EXPERTS):
        start = jnp.where(eidx == e, pstart_ref[e], start)
    out_ref[...] = start + rank_ref[...]


def _dest_rows(eidx_t, rank_t, pstart):
    T = eidx_t.shape[1]
    tm = DEST_TILE
    tok_spec = pl.BlockSpec((TOP_K, tm), lambda i: (0, i))
    return pl.pallas_call(
        _dest_kernel,
        grid=(T // tm,),
        in_specs=[pl.BlockSpec(memory_space=pltpu.SMEM), tok_spec, tok_spec],
        out_specs=tok_spec,
        out_shape=jax.ShapeDtypeStruct((TOP_K, T), I32),
        compiler_params=pltpu.CompilerParams(dimension_semantics=("arbitrary",)),
        name="moe_dest_rows",
    )(pstart, eidx_t, rank_t)


def _sc_workers():
    info = plsc.get_sparse_core_info()
    return info.num_cores, info.num_cores * info.num_subcores


def _sc_scatter_rows(rows, row0, dest, n_out):
    n_dst, n_chunks, ch = dest.shape
    width = rows.shape[1]
    n_cores, n_workers = _sc_workers()
    per_w = n_chunks // n_workers
    assert n_chunks % n_workers == 0 and per_w % 8 == 0 and row0 + n_chunks * ch <= rows.shape[0]

    def body(rows_hbm, dest_hbm, out_hbm, idx_v, buf, lsem, ssem):
        wid = lax.axis_index("s") * n_cores + lax.axis_index("c")
        c0 = wid * per_w
        for k in range(n_dst):
            pltpu.sync_copy(dest_hbm.at[k, pl.ds(pl.multiple_of(c0, 8), per_w)], idx_v.at[k])

        def load(c, b):
            return pltpu.make_async_copy(rows_hbm.at[pl.ds(row0 + (c0 + c) * ch, ch)], buf.at[b], lsem.at[b])

        def scatter(c, b, k):
            return pltpu.make_async_copy(buf.at[b], out_hbm.at[idx_v.at[k, c]], ssem.at[b])

        load(0, 0).start()

        @pl.loop(0, per_w, step=2)
        def _(c):
            for b in range(2):
                cc = c + b
                load(cc, b).wait()

                @pl.when(cc >= 1)
                def _():
                    for k in range(n_dst):
                        scatter(cc - 1, 1 - b, k).wait()

                @pl.when(cc + 1 < per_w)
                def _():
                    load(cc + 1, 1 - b).start()

                for k in range(n_dst):
                    scatter(cc, b, k).start()

        for k in range(n_dst):
            scatter(per_w - 1, 1, k).wait()

    mesh = plsc.VectorSubcoreMesh(core_axis_name="c", subcore_axis_name="s")
    return pl.kernel(
        body, out_type=jax.ShapeDtypeStruct((n_out, width), rows.dtype), mesh=mesh,
        scratch_types=[pltpu.VMEM((n_dst, per_w, ch), I32), pltpu.VMEM((2, ch, width), rows.dtype),
                       pltpu.SemaphoreType.DMA((2,)), pltpu.SemaphoreType.DMA((2,))],
        name="moe_dispatch_scatter",
    )(rows, dest)


def _sc_gather_weighted_sum(table, idx, gates):
    n_src, n_tok = idx.shape
    ct = SC_SUM_CHUNK
    n_chunks = n_tok // ct
    half = table.shape[1]
    lanes = plsc.get_sparse_core_info().num_lanes
    n_cores, n_workers = _sc_workers()
    per_w = n_chunks // n_workers
    assert n_chunks % n_workers == 0 and per_w % 2 == 0 and half % lanes == 0

    def body(table_hbm, idx_hbm, gate_hbm, out_hbm, idx_v, gate_v, buf, out_v, gsem, wsem):
        wid = lax.axis_index("s") * n_cores + lax.axis_index("c")
        c0 = wid * per_w
        pltpu.sync_copy(idx_hbm.at[:, pl.ds(c0 * ct, per_w * ct)], idx_v)
        pltpu.sync_copy(gate_hbm.at[:, pl.ds(c0 * ct, per_w * ct)], gate_v)

        def gather(c, b, k):
            return pltpu.make_async_copy(table_hbm.at[idx_v.at[k, pl.ds(c * ct, ct)]], buf.at[b, k], gsem.at[b])

        def write(c, b):
            return pltpu.make_async_copy(out_v.at[b], out_hbm.at[pl.ds((c0 + c) * ct, ct)], wsem.at[b])

        for k in range(n_src):
            gather(0, 0, k).start()

        @pl.loop(0, per_w, step=2)
        def _(c):
            for b in range(2):
                cc = c + b
                for k in range(n_src):
                    gather(cc, b, k).wait()

                @pl.when(cc + 1 < per_w)
                def _():
                    for k in range(n_src):
                        gather(cc + 1, 1 - b, k).start()

                @pl.when(cc >= 2)
                def _():
                    write(cc - 2, b).wait()

                @pl.loop(0, ct)
                def _(t):
                    tok = jnp.full((lanes,), cc * ct + t, I32)
                    g = [plsc.load_gather(gate_v, [jnp.full((lanes,), k, I32), tok]) for k in range(n_src)]

                    @plsc.parallel_loop(0, half, step=lanes, unroll=2)
                    def _(col):
                        lo = jnp.zeros((lanes,), F32)
                        hi = jnp.zeros((lanes,), F32)
                        for k in range(n_src):
                            w = buf[b, k, t, pl.ds(col, lanes)]
                            w_lo, w_hi = plsc.unpack(plsc.bitcast(w, BF16), format=plsc.PackFormat.INTERLEAVED)
                            lo = lo + g[k] * w_lo
                            hi = hi + g[k] * w_hi
                        out_v[b, t, pl.ds(col, lanes)] = lo
                        out_v[b, t, pl.ds(half + col, lanes)] = hi

                write(cc, b).start()

        write(per_w - 2, 0).wait()
        write(per_w - 1, 1).wait()

    mesh = plsc.VectorSubcoreMesh(core_axis_name="c", subcore_axis_name="s")
    return pl.kernel(
        body, out_type=jax.ShapeDtypeStruct((n_chunks * ct, 2 * half), F32), mesh=mesh,
        scratch_types=[pltpu.VMEM((n_src, per_w * ct), I32), pltpu.VMEM((n_src, per_w * ct), F32),
                       pltpu.VMEM((2, n_src, ct, half), I32), pltpu.VMEM((2, ct, 2 * half), F32),
                       pltpu.SemaphoreType.DMA((2,)), pltpu.SemaphoreType.DMA((2,))],
        compiler_params=pltpu.CompilerParams(needs_layout_passes=False),
        name="moe_combine_gather_sum",
    )(table, idx, gates)


def _sc_pack_row_pairs(w):
    R, C = w.shape
    lanes = plsc.get_sparse_core_info().num_lanes
    cr = SC_PACK_CHUNK_WORDS // C
    n_chunks = R // cr
    n_cores, n_workers = _sc_workers()
    per_w = n_chunks // n_workers
    assert R % cr == 0 and n_chunks % n_workers == 0 and per_w % 2 == 0 and cr % 2 == 0 and C % lanes == 0

    def body(w_hbm, out_hbm, in_v, out_v, lsem, ssem):
        wid = lax.axis_index("s") * n_cores + lax.axis_index("c")
        c0 = wid * per_w

        def load(c, b):
            return pltpu.make_async_copy(w_hbm.at[pl.ds((c0 + c) * cr, cr)], in_v.at[b], lsem.at[b])

        def store(c, b):
            return pltpu.make_async_copy(out_v.at[b], out_hbm.at[pl.ds((c0 + c) * (cr // 2), cr // 2)],
                                         ssem.at[b])

        load(0, 0).start()

        @pl.loop(0, per_w, step=2)
        def _(c):
            for b in range(2):
                cc = c + b
                load(cc, b).wait()

                @pl.when(cc + 1 < per_w)
                def _():
                    load(cc + 1, 1 - b).start()

                @pl.when(cc >= 2)
                def _():
                    store(cc - 2, b).wait()

                @pl.loop(0, cr // 2)
                def _(i):
                    @plsc.parallel_loop(0, C, step=lanes, unroll=4)
                    def _(col):
                        even = in_v[b, 2 * i, pl.ds(col, lanes)]
                        odd = in_v[b, 2 * i + 1, pl.ds(col, lanes)]
                        pair = plsc.pack(even, odd, format=plsc.PackFormat.INTERLEAVED)
                        out_v[b, i, pl.ds(col, lanes)] = plsc.bitcast(pair, I32)

                store(cc, b).start()

        store(per_w - 2, 0).wait()
        store(per_w - 1, 1).wait()

    mesh = plsc.VectorSubcoreMesh(core_axis_name="c", subcore_axis_name="s")
    return pl.kernel(
        body, out_type=jax.ShapeDtypeStruct((R // 2, C), I32), mesh=mesh,
        scratch_types=[pltpu.VMEM((2, cr, C), F32), pltpu.VMEM((2, cr // 2, C), I32),
                       pltpu.SemaphoreType.DMA((2,)), pltpu.SemaphoreType.DMA((2,))],
        compiler_params=pltpu.CompilerParams(needs_layout_passes=False),
        name="expert_weights_bf16",
    )(w)


def _expert_kernel(be_ref, ue_ref, nu_ref, xs_hbm, wg_hbm, wu_hbm, wd_hbm, y_hbm,
                   xbuf, ybuf, wg_v, wu_v, wd_v, xsem, ysem, wsem):
    n_used = nu_ref[0]
    n_exp = nu_ref[1]
    blk = MOE_BLOCK
    rows_gu = D_MODEL // 2
    rows_d = EXPERT_DIM // 2

    def rows(j):
        return pl.ds(pl.multiple_of(j * blk, blk), blk)

    def x_copy(j, p):
        return pltpu.make_async_copy(xs_hbm.at[rows(j)], xbuf.at[p], xsem.at[p])

    def y_copy(j, p):
        return pltpu.make_async_copy(ybuf.at[p], y_hbm.at[rows(j)], ysem.at[p])

    def w_copies(q, s):
        e = ue_ref[q]
        gu = pl.ds(pl.multiple_of(e * rows_gu, rows_gu), rows_gu)
        dn = pl.ds(pl.multiple_of(e * rows_d, rows_d), rows_d)
        return (pltpu.make_async_copy(wg_hbm.at[gu], wg_v.at[s], wsem.at[s, 0]),
                pltpu.make_async_copy(wu_hbm.at[gu], wu_v.at[s], wsem.at[s, 1]),
                pltpu.make_async_copy(wd_hbm.at[dn], wd_v.at[s], wsem.at[s, 2]))

    for q0 in range(W_SLOTS - 1):
        @pl.when(q0 < n_exp)
        def _():
            for c in w_copies(q0, q0):
                c.start()

    for j0 in range(X_SLOTS):
        @pl.when(j0 < n_used)
        def _():
            x_copy(j0, j0).start()

    def block_step(j, p, q):
        is_new = (j == 0) | (be_ref[j] != be_ref[jnp.maximum(j - 1, 0)])
        q = q + is_new.astype(I32)

        s = q % W_SLOTS

        @pl.when(is_new)
        def _():
            for c in w_copies(q, s):
                c.wait()

            @pl.when(q + W_SLOTS - 1 < n_exp)
            def _():
                for c in w_copies(q + W_SLOTS - 1, (q + W_SLOTS - 1) % W_SLOTS):
                    c.start()

        x_copy(j, p).wait()

        @pl.when(j >= X_SLOTS)
        def _():
            y_copy(j - X_SLOTS, p).wait()

        wg = pltpu.bitcast(wg_v[s], BF16)
        wu = pltpu.bitcast(wu_v[s], BF16)
        wd = pltpu.bitcast(wd_v[s], BF16)
        lo, hi = _unpack_bf16_pairs(xbuf[p])
        x = jnp.concatenate([lo.astype(BF16), hi.astype(BF16)], axis=1)
        g = jnp.dot(x, wg, preferred_element_type=F32)
        u = jnp.dot(x, wu, preferred_element_type=F32)
        hb = (g * jax.nn.sigmoid(g) * u).astype(BF16)
        ybuf[p] = _pack_bf16_pairs(jnp.dot(hb, wd, preferred_element_type=F32))
        y_copy(j, p).start()

        @pl.when(j + X_SLOTS < n_used)
        def _():
            x_copy(j + X_SLOTS, p).start()

        return q

    def group(m, q):
        q = block_step(X_SLOTS * m, 0, q)
        for p in range(1, X_SLOTS):
            j = X_SLOTS * m + p
            q = lax.cond(j < n_used, functools.partial(block_step, j, p), lambda q: q, q)
        return q

    lax.fori_loop(0, (n_used + X_SLOTS - 1) // X_SLOTS, group, jnp.int32(-1))

    for back in range(X_SLOTS, 0, -1):
        b = n_used - back
        for p in range(X_SLOTS):
            @pl.when((b >= 0) & (b % X_SLOTS == p))
            def _():
                y_copy(b, p).wait()


def _experts(block_e, used_e, counts2, xs, wg, wu, wd):
    n_rows, half = xs.shape
    smem = pl.BlockSpec(memory_space=pltpu.SMEM)
    hbm = pl.BlockSpec(memory_space=pl.ANY)
    return pl.pallas_call(
        _expert_kernel,
        in_specs=[smem, smem, smem, hbm, hbm, hbm, hbm],
        out_specs=hbm,
        out_shape=jax.ShapeDtypeStruct((n_rows, half), I32),
        scratch_shapes=[pltpu.VMEM((X_SLOTS, MOE_BLOCK, half), I32), pltpu.VMEM((X_SLOTS, MOE_BLOCK, half), I32),
                        pltpu.VMEM((W_SLOTS, D_MODEL // 2, EXPERT_DIM), I32),
                        pltpu.VMEM((W_SLOTS, D_MODEL // 2, EXPERT_DIM), I32),
                        pltpu.VMEM((W_SLOTS, EXPERT_DIM // 2, D_MODEL), I32),
                        pltpu.SemaphoreType.DMA((X_SLOTS,)), pltpu.SemaphoreType.DMA((X_SLOTS,)),
                        pltpu.SemaphoreType.DMA((W_SLOTS, 3))],
        compiler_params=pltpu.CompilerParams(vmem_limit_bytes=VMEM_LIMIT),
        name="moe_experts",
    )(block_e, used_e, counts2, xs, wg, wu, wd)


def _combine_kernel(routed_ref, x1_ref, swgu_ref, swd_ref, g2_ref, b2_ref, *rest, alpha):
    out_ref = rest[-1]
    x1 = x1_ref[...]
    gu = jnp.dot(x1.astype(BF16), swgu_ref[...], preferred_element_type=F32)
    g = gu[:, :EXPERT_DIM]
    u = gu[:, EXPERT_DIM:]
    shared = jnp.dot((g * jax.nn.sigmoid(g) * u).astype(BF16), swd_ref[...], preferred_element_type=F32)
    out_ref[...] = _layer_norm(alpha * x1 + (routed_ref[...] + shared), g2_ref[...], b2_ref[...])


def _combine(routed, x1, tok0, swgu, swd, g2, b2, alpha, out_prev):
    T, D = x1.shape
    n_tok = routed.shape[0]
    tc = COMBINE_TILE
    first = tok0 // tc
    consts = (swgu, swd, g2, b2)
    args = [routed, x1, *consts]
    in_specs = [pl.BlockSpec((tc, D), lambda i: (i, 0)),
                pl.BlockSpec((tc, D), lambda i: (first + i, 0))] + [_const_spec(c.shape) for c in consts]
    aliases = {}
    if out_prev is not None:
        aliases = {len(args): 0}
        args.append(out_prev)
        in_specs.append(pl.BlockSpec(memory_space=pl.ANY))
    return pl.pallas_call(
        functools.partial(_combine_kernel, alpha=alpha),
        grid=(n_tok // tc,),
        in_specs=in_specs,
        out_specs=pl.BlockSpec((tc, D), lambda i: (first + i, 0)),
        out_shape=jax.ShapeDtypeStruct((T, D), F32),
        input_output_aliases=aliases,
        compiler_params=pltpu.CompilerParams(dimension_semantics=("arbitrary",), vmem_limit_bytes=VMEM_LIMIT),
        name="moe_combine_norm",
    )(*args)


def _mixer(x, in_w, in_b, ng, nb, spatial_w, spatial_b, proj_a_w, proj_b_w, out_w, ln1_g, ln1_b,
           router_w, alpha, anchors):
    gw, aw, D = GMLP_WIDTH, ATTN_WIDTH, D_MODEL
    w = in_w.astype(BF16)
    q0 = 2 * gw
    wuv, buv = w[:, :q0], in_b[None, :q0]
    watt = jnp.stack([jnp.concatenate([w[:, q0 + s * aw + p * GROUP_WIDTH:q0 + s * aw + (p + 1) * GROUP_WIDTH]
                                       for s in range(3)], axis=1) for p in range(len(DILATED_PATTERNS))])
    batt = jnp.stack([jnp.concatenate([in_b[q0 + s * aw + p * GROUP_WIDTH:q0 + s * aw + (p + 1) * GROUP_WIDTH]
                                       for s in range(3)])[None] for p in range(len(DILATED_PATTERNS))])
    g0 = q0 + 3 * aw
    wg, bg = w[:, g0:], in_b[None, g0:]
    sb = jnp.repeat(spatial_b.T, gw // GMLP_GROUPS, axis=1)
    ya, gb, a1, a4, a16 = _input_projection(
        x, wuv, buv, watt, batt, wg, bg, ng[None], nb[None], spatial_w, sb, proj_a_w.astype(BF16))
    attn_outs = [_dilated_attention(a, p, d) for p, (a, (_, d)) in enumerate(zip((a1, a4, a16), DILATED_PATTERNS))]
    rw_t = router_w.T
    rwh = rw_t.astype(BF16)
    rwl = (rw_t - rwh.astype(F32)).astype(BF16)
    return _merge_and_norm(attn_outs, ya, gb, x, proj_b_w.astype(BF16), out_w.astype(BF16),
                           ln1_g[None], ln1_b[None], rwh, rwl, alpha, anchors)


def _moe(x1, x1p, scores_t, router_bias, w_gate, w_up, w_down, sw_gate, sw_up, sw_down, ln2_g, ln2_b, alpha):
    swgu = jnp.concatenate([sw_gate, sw_up], axis=1).astype(BF16)
    swd = sw_down.astype(BF16)
    out = None
    tok0 = 0
    for part, eighths in enumerate(MOE_TOKEN_SPLIT):
        n_tok = x1.shape[0] * eighths // 8
        n_blocks = (n_tok * TOP_K + N_EXPERTS * (MOE_BLOCK - 1)) // MOE_BLOCK
        eidx_t, rank_t, gate_t, counts = _route(scores_t, router_bias[:, None], tok0, n_tok)
        counts = counts[:, 0].astype(I32)
        padded = (counts + MOE_BLOCK - 1) // MOE_BLOCK * MOE_BLOCK
        pend = jnp.cumsum(padded).astype(I32)
        pstart = pend - padded
        block_starts = jnp.arange(n_blocks, dtype=I32) * MOE_BLOCK
        block_e = jnp.minimum(jnp.sum((pend[None, :] <= block_starts[:, None]).astype(I32), axis=1),
                              N_EXPERTS - 1)
        used = counts > 0
        used_e = jnp.argsort(jnp.logical_not(used), stable=True).astype(I32)
        counts2 = jnp.stack([pend[-1] // MOE_BLOCK, jnp.sum(used.astype(I32))]).astype(I32)
        dest_t = _dest_rows(eidx_t, rank_t, pstart)
        xs = _sc_scatter_rows(x1p, tok0, dest_t.reshape(TOP_K, n_tok // SC_CHUNK, SC_CHUNK), n_blocks * MOE_BLOCK)
        y_rows = _experts(block_e, used_e, counts2, xs, w_gate, w_up, w_down)
        pieces = MOE_TAIL_PIECES if part == len(MOE_TOKEN_SPLIT) - 1 else 1
        n_piece = n_tok // pieces
        for q in range(pieces):
            cols = slice(q * n_piece, (q + 1) * n_piece)
            routed = _sc_gather_weighted_sum(y_rows, dest_t[:, cols], gate_t[:, cols])
            out = _combine(routed, x1, tok0 + q * n_piece, swgu, swd, ln2_g[None], ln2_b[None], alpha, out)
        tok0 += n_tok
    return out


def kernel(x, in_w, in_b, gmlp_norm_g, gmlp_norm_b, spatial_w, spatial_b, proj_a_w, proj_b_w, out_w,
           ln1_g, ln1_b, router_w, router_bias, expert_w_gate, expert_w_up, expert_w_down,
           shared_w_gate, shared_w_up, shared_w_down, ln2_g, ln2_b):
    B, S, D = x.shape
    depth = in_w.shape[0]
    alpha = np.float32((2.0 * depth) ** 0.25)
    for l in range(depth):
        packed = [_sc_pack_row_pairs(w[l].reshape(-1, w.shape[-1]))
                  for w in (expert_w_gate, expert_w_up, expert_w_down)]
        x1, x1p, scores_t = _mixer(x, in_w[l], in_b[l], gmlp_norm_g[l], gmlp_norm_b[l], spatial_w[l],
                                 spatial_b[l], proj_a_w[l], proj_b_w[l], out_w[l], ln1_g[l], ln1_b[l],
                                 router_w[l], alpha, [p[:8] for p in packed])
        out = _moe(x1.reshape(B * S, D), x1p.reshape(B * S, D // 2), scores_t,
                   router_bias[l], *packed,
                   shared_w_gate[l], shared_w_up[l], shared_w_down[l], ln2_g[l], ln2_b[l], alpha)
        x = out.reshape(B, S, D)
    return x
```

```python
import functools
import math

import numpy as np
import jax
import jax.numpy as jnp
from jax import lax
from jax.experimental import pallas as pl
from jax.experimental.pallas import tpu as pltpu
from jax.experimental.pallas import tpu_sc as plsc

F32 = jnp.float32
BF16 = jnp.bfloat16
U32 = jnp.uint32
I32 = jnp.int32

D_MODEL = 1024
GMLP_WIDTH = 1024
GMLP_GROUPS = 8
GMLP_CHUNK = 128
HEAD_DIM = 64
DILATED_PATTERNS = ((128, 1), (512, 4), (2048, 16))
HEADS_PER_GROUP = 4
GROUP_WIDTH = HEADS_PER_GROUP * HEAD_DIM
ATTN_WIDTH = GROUP_WIDTH * len(DILATED_PATTERNS)
ATTN_BLOCK = 128
N_EXPERTS = 256
TOP_K = 8
N_EXPERT_GROUPS = 8
TOPK_GROUPS = 4
EXPERT_DIM = 256
ROUTED_SCALE = 2.5
LN_EPS = 1e-5
LANES = 128
MASKED_SCORE = -1e30

PROJ_TILE = 512
ATTN_QBLOCKS = 16
MERGE_TILE = 1024
ROUTE_TILE = 512
DEST_TILE = 2048
MOE_BLOCK = 256
SC_PACK_CHUNK_WORDS = 16384
SC_CHUNK = 64
SC_SUM_CHUNK = 8
MOE_TOKEN_SPLIT = (4, 4)
MOE_TAIL_PIECES = 4
X_SLOTS = 4
W_SLOTS = 3
COMBINE_TILE = 512
VMEM_LIMIT = 56 * 1024 * 1024


def _layer_norm(y, g, b):
    mu = jnp.mean(y, axis=-1, keepdims=True)
    yc = y - mu
    var = jnp.mean(yc * yc, axis=-1, keepdims=True)
    return yc * lax.rsqrt(var + LN_EPS) * g + b


def _gelu(x):
    return 0.5 * x * (1.0 + lax.erf(x * np.float32(math.sqrt(0.5))))


def _pack_bf16_pairs(x):
    w = x.shape[1] // 2
    bits = pltpu.bitcast(x.astype(BF16).astype(F32), U32)
    return pltpu.bitcast((bits[:, :w] >> 16) | (bits[:, w:] & jnp.uint32(0xFFFF0000)), I32)


def _unpack_bf16_pairs(words):
    w = pltpu.bitcast(words, U32)
    lo = pltpu.bitcast(w << 16, F32)
    hi = pltpu.bitcast(w & jnp.uint32(0xFFFF0000), F32)
    return lo, hi


def _const_spec(shape):
    nd = len(shape)
    return pl.BlockSpec(shape, lambda *_: (0,) * nd)


def _proj_kernel(x_ref, wuv_ref, buv_ref, watt_ref, batt_ref, wg_ref, bg_ref, ng_ref, nb_ref,
                 sw_ref, sb_ref, pa_ref, ya_ref, gb_ref, a1_ref, a4_ref, a16_ref, xc_ref):
    tm = x_ref.shape[1]
    gw = GMLP_WIDTH
    xb = x_ref[0].astype(BF16)

    def proj(w, b):
        return jnp.dot(xb, w, preferred_element_type=F32) + b

    h_v = proj(wuv_ref[:, gw:], buv_ref[:, gw:])
    h_u = proj(wuv_ref[:, :gw], buv_ref[:, :gw])
    v = _layer_norm(_gelu(h_v), ng_ref[...], nb_ref[...]).astype(BF16)
    h_ga = proj(wg_ref[:, :D_MODEL], bg_ref[:, :D_MODEL])
    u = _gelu(h_u)

    cw = gw // GMLP_GROUPS
    row = lax.broadcasted_iota(I32, (GMLP_CHUNK, GMLP_CHUNK), 0)
    col = lax.broadcasted_iota(I32, (GMLP_CHUNK, GMLP_CHUNK), 1)
    ws = [jnp.where(row >= col, sw_ref[g], 0.0).astype(BF16) for g in range(GMLP_GROUPS)]
    chunks = []
    for c in range(tm // GMLP_CHUNK):
        vc = v[c * GMLP_CHUNK:(c + 1) * GMLP_CHUNK]
        cols = [jnp.dot(ws[g], vc[:, g * cw:(g + 1) * cw], preferred_element_type=F32)
                for g in range(GMLP_GROUPS)]
        chunks.append(jnp.concatenate(cols, axis=1) + sb_ref[...])
    vmix = jnp.concatenate(chunks, axis=0)
    h_gb = proj(wg_ref[:, D_MODEL:], bg_ref[:, D_MODEL:])
    ga = jax.nn.sigmoid(h_ga)
    ya = jnp.dot((u * vmix).astype(BF16), pa_ref[...], preferred_element_type=F32)
    gb_ref[0] = jax.nn.sigmoid(h_gb).astype(BF16)

    n_chunks = x_ref.shape[2] // LANES
    for c in range(n_chunks):
        xc_ref[c] = x_ref[0, :, c * LANES:(c + 1) * LANES]

    def attn_proj(p, d):
        n = tm // d
        if d == 1:
            xp = xb
        else:
            xp = jnp.concatenate(
                [jnp.concatenate([xc_ref[c, pl.ds(r, n, stride=d), :] for c in range(n_chunks)], axis=1)
                 for r in range(d)], axis=0).astype(BF16)
        return jnp.dot(xp, watt_ref[p], preferred_element_type=F32)

    def attn_store(p, d, h, a_ref):
        n = tm // d
        h = (h + batt_ref[p]).astype(BF16)
        for r in range(d):
            a_ref[0, r] = h[r * n:(r + 1) * n]

    a_refs = (a1_ref, a4_ref, a16_ref)
    dils = [d for _, d in DILATED_PATTERNS]
    h_prev = attn_proj(0, dils[0])
    ya_ref[0] = (ga * ya).astype(BF16)
    for p in range(1, len(dils)):
        h_next = attn_proj(p, dils[p])
        attn_store(p - 1, dils[p - 1], h_prev, a_refs[p - 1])
        h_prev = h_next
    attn_store(len(dils) - 1, dils[-1], h_prev, a_refs[-1])


def _input_projection(x, wuv, buv, watt, batt, wg, bg, ng, nb, sw, sb, pa):
    B, S, D = x.shape
    tm = PROJ_TILE
    grid = (B, S // tm)
    out_shape = [jax.ShapeDtypeStruct((B, S, D), BF16), jax.ShapeDtypeStruct((B, S, D), BF16)]
    out_specs = [pl.BlockSpec((1, tm, D), lambda b, t: (b, t, 0)),
                 pl.BlockSpec((1, tm, D), lambda b, t: (b, t, 0))]
    for _, d in DILATED_PATTERNS:
        out_shape.append(jax.ShapeDtypeStruct((B, d, S // d, ATTN_WIDTH), BF16))
        out_specs.append(pl.BlockSpec((1, d, tm // d, ATTN_WIDTH), lambda b, t: (b, 0, t, 0)))
    consts = (wuv, buv, watt, batt, wg, bg, ng, nb, sw, sb, pa)
    return pl.pallas_call(
        _proj_kernel,
        grid=grid,
        in_specs=[pl.BlockSpec((1, tm, D), lambda b, t: (b, t, 0))] + [_const_spec(c.shape) for c in consts],
        out_specs=out_specs,
        out_shape=out_shape,
        scratch_shapes=[pltpu.VMEM((D // LANES, tm, LANES), F32)],
        compiler_params=pltpu.CompilerParams(
            dimension_semantics=("arbitrary", "arbitrary"), vmem_limit_bytes=VMEM_LIMIT),
        name="input_projection",
    )(x, *consts)


def _attn_kernel(*refs, plans):
    n_pat = len(plans)
    qkv_refs, bias_refs = refs[0:2 * n_pat:2], refs[1:2 * n_pat:2]
    o_refs, lse_refs = refs[2 * n_pat::2], refs[2 * n_pat + 1::2]
    step = pl.program_id(1)
    gwid = GROUP_WIDTH
    blk = ATTN_BLOCK
    lane = lax.broadcasted_iota(I32, (1, gwid), 1)
    head_masks = [(lane >= h * HEAD_DIM) & (lane < (h + 1) * HEAD_DIM) for h in range(HEADS_PER_GROUP)]
    q_scales = [jnp.where(m, np.float32(HEAD_DIM ** -0.5), 0.0).astype(BF16) for m in head_masks]

    def rows_of(unit):
        pi, _, j = unit
        qblocks, _, n_steps = plans[pi]
        n = (step % n_steps) * qblocks + j if n_steps > 1 else jnp.int32(j)
        return n, pl.multiple_of(n * blk, blk), pl.multiple_of(jnp.maximum(n - 1, 0) * blk, blk)

    def scores(unit):
        pi, ri, _ = unit
        qkv_ref = qkv_refs[pi]
        n, q0, p0 = rows_of(unit)
        q = qkv_ref[0, ri, pl.ds(q0, blk), 0:gwid]
        kk = jnp.concatenate([qkv_ref[0, ri, pl.ds(p0, blk), gwid:2 * gwid],
                              qkv_ref[0, ri, pl.ds(q0, blk), gwid:2 * gwid]], axis=0)
        qs = jnp.concatenate([q * s for s in q_scales], axis=0)
        s = lax.dot_general(qs, kk, (((1,), (1,)), ((), ())), preferred_element_type=F32)
        return s + bias_refs[pi][jnp.where(n == 0, 1, 0)]

    def finish(unit, s):
        pi, ri, j = unit
        qkv_ref = qkv_refs[pi]
        _, q0, p0 = rows_of(unit)
        vv = jnp.concatenate([qkv_ref[0, ri, pl.ds(p0, blk), 2 * gwid:3 * gwid],
                              qkv_ref[0, ri, pl.ds(q0, blk), 2 * gwid:3 * gwid]], axis=0)
        m = jnp.max(s, axis=1, keepdims=True)
        p = jnp.exp(s - m)
        den = jnp.sum(p, axis=1, keepdims=True)
        pv = jnp.dot(p.astype(BF16), vv, preferred_element_type=F32)
        on = pv / den
        lse = m + jnp.log(den)
        o = jnp.zeros((blk, gwid), F32)
        l = jnp.zeros((blk, gwid), F32)
        for h in range(HEADS_PER_GROUP):
            o = jnp.where(head_masks[h], on[h * blk:(h + 1) * blk], o)
            l = jnp.where(head_masks[h], lse[h * blk:(h + 1) * blk], l)
        o_refs[pi][0, ri, j * blk:(j + 1) * blk, :] = o.astype(BF16)
        lse_refs[pi][0, ri, j * blk:(j + 1) * blk, :] = l

    units = [(pi, ri, j) for pi, (qblocks, rblock, _) in enumerate(plans)
             for ri in range(rblock) for j in range(qblocks)]
    s_next = scores(units[0])
    for i, unit in enumerate(units):
        s_cur = s_next
        if i + 1 < len(units):
            s_next = scores(units[i + 1])
        finish(unit, s_cur)


def _alibi_slopes(n):
    def pow2_slopes(m):
        start = 2.0 ** (-8.0 / m)
        return [start ** (i + 1) for i in range(m)]
    p = 2 ** int(math.floor(math.log2(n)))
    s = pow2_slopes(p)
    if p < n:
        s = s + pow2_slopes(2 * p)[0::2][: n - p]
    return np.array(sorted(s, reverse=True), dtype=np.float32)


def _attn_bias_tables(group, dilation):
    blk = ATTN_BLOCK
    slopes = _alibi_slopes(HEADS_PER_GROUP * len(DILATED_PATTERNS))
    slopes = slopes[group * HEADS_PER_GROUP:(group + 1) * HEADS_PER_GROUP]
    qi = np.arange(blk)[:, None]
    ki = np.arange(2 * blk)[None, :]
    delta = blk + qi - ki
    band = (delta >= 0) & (delta <= blk)
    bias = -slopes[:, None, None] * (delta * dilation).astype(np.float32)[None]
    full = np.where(band[None], bias, np.float32(MASKED_SCORE)).astype(np.float32)
    first = np.where((ki >= blk)[None], full, np.float32(MASKED_SCORE)).astype(np.float32)
    return full.reshape(HEADS_PER_GROUP * blk, 2 * blk), first.reshape(HEADS_PER_GROUP * blk, 2 * blk)


def _dilated_attention(qkvs):
    B = qkvs[0].shape[0]
    plans, args, in_specs, out_specs, out_shape = [], [], [], [], []
    n_grid = None
    for group, (qkv, (_, dilation)) in enumerate(zip(qkvs, DILATED_PATTERNS)):
        _, d, sd, _ = qkv.shape
        qblocks = min(ATTN_QBLOCKS, sd // ATTN_BLOCK)
        rblock = min(d, ATTN_QBLOCKS // qblocks)
        rows = qblocks * ATTN_BLOCK
        n_steps = sd // rows
        assert n_grid in (None, (d // rblock) * n_steps)
        n_grid = (d // rblock) * n_steps
        plans.append((qblocks, rblock, n_steps))
        bias = np.stack(_attn_bias_tables(group, dilation))
        args += [qkv, jnp.asarray(bias)]
        in_specs += [pl.BlockSpec((1, rblock, sd, ATTN_WIDTH), lambda b, s, n_steps=n_steps: (b, s // n_steps, 0, 0)),
                     _const_spec(bias.shape)]
        out_spec = pl.BlockSpec((1, rblock, rows, GROUP_WIDTH),
                                lambda b, s, n_steps=n_steps: (b, s // n_steps, s % n_steps, 0))
        out_specs += [out_spec, out_spec]
        out_shape += [jax.ShapeDtypeStruct((B, d, sd, GROUP_WIDTH), BF16),
                      jax.ShapeDtypeStruct((B, d, sd, GROUP_WIDTH), F32)]
    outs = pl.pallas_call(
        functools.partial(_attn_kernel, plans=tuple(plans)),
        grid=(B, n_grid),
        in_specs=in_specs,
        out_specs=out_specs,
        out_shape=out_shape,
        compiler_params=pltpu.CompilerParams(
            dimension_semantics=("arbitrary", "arbitrary"), vmem_limit_bytes=VMEM_LIMIT),
        name="dilated_attention",
    )(*args)
    return [(outs[2 * i], outs[2 * i + 1]) for i in range(len(qkvs))]


def _merge_kernel(o1_ref, l1_ref, o4_ref, l4_ref, o16_ref, l16_ref, ya_ref, gb_ref, x_ref,
                  pb_ref, ow_ref, g1_ref, b1_ref, rwh_ref, rwl_ref, *rest, alpha, n_anchors):
    x1_ref, x1p_ref, sc_ref, so4, sl4, so16, sl16 = rest[n_anchors:]
    tm = x_ref.shape[1]
    n_chunks = GROUP_WIDTH // LANES
    for (o_ref, l_ref, so, sl, d) in ((o4_ref, l4_ref, so4, sl4, 4), (o16_ref, l16_ref, so16, sl16, 16)):
        n = tm // d
        for r in range(d):
            o_r = o_ref[0, r].astype(F32)
            l_r = l_ref[0, r]
            for c in range(n_chunks):
                so[c, pl.ds(r, n, stride=d), :] = o_r[:, c * LANES:(c + 1) * LANES]
                sl[c, pl.ds(r, n, stride=d), :] = l_r[:, c * LANES:(c + 1) * LANES]

    def natural(s):
        return jnp.concatenate([s[c] for c in range(n_chunks)], axis=1)

    l1 = l1_ref[0, 0]
    l4 = natural(sl4)
    l16 = natural(sl16)
    lmax = jnp.maximum(jnp.maximum(l1, l4), l16)
    e1 = jnp.exp(l1 - lmax)
    e4 = jnp.exp(l4 - lmax)
    e16 = jnp.exp(l16 - lmax)
    yb = (e1 * o1_ref[0, 0].astype(F32) + e4 * natural(so4) + e16 * natural(so16)) / (e1 + e4 + e16)
    ybp = jnp.dot(yb.astype(BF16), pb_ref[...], preferred_element_type=F32)
    merged = ya_ref[0] + gb_ref[0] * ybp.astype(BF16)
    mix = jnp.dot(merged, ow_ref[...], preferred_element_type=F32)
    x1 = _layer_norm(alpha * x_ref[0] + mix, g1_ref[...], b1_ref[...])
    x1_ref[0] = x1
    x1p_ref[0] = _pack_bf16_pairs(x1)
    hi = x1.astype(BF16)
    lo = (x1 - hi.astype(F32)).astype(BF16)
    def logits_t(w_ref, xt):
        return lax.dot_general(w_ref[...], xt, (((1,), (1,)), ((), ())), preferred_element_type=F32)
    sc_ref[...] = jax.nn.sigmoid(logits_t(rwh_ref, hi) + logits_t(rwh_ref, lo) + logits_t(rwl_ref, hi))


def _merge_and_norm(attn_outs, ya, gb, x, pb, ow, g1, b1, rwh, rwl, alpha, anchors):
    B, S, D = x.shape
    tm = MERGE_TILE
    in_specs = []
    args = []
    for (o, l), (_, d) in zip(attn_outs, DILATED_PATTERNS):
        spec = pl.BlockSpec((1, d, tm // d, GROUP_WIDTH), lambda b, t: (b, 0, t, 0))
        in_specs += [spec, spec]
        args += [o, l]
    tok_spec = pl.BlockSpec((1, tm, D), lambda b, t: (b, t, 0))
    in_specs += [tok_spec, tok_spec, tok_spec]
    args += [ya, gb, x]
    consts = (pb, ow, g1, b1, rwh, rwl)
    in_specs += [_const_spec(c.shape) for c in consts]
    in_specs += [pl.BlockSpec(memory_space=pl.ANY) for _ in anchors]
    return pl.pallas_call(
        functools.partial(_merge_kernel, alpha=alpha, n_anchors=len(anchors)),
        grid=(B, S // tm),
        in_specs=in_specs,
        out_specs=[tok_spec,
                   pl.BlockSpec((1, tm, D // 2), lambda b, t: (b, t, 0)),
                   pl.BlockSpec((N_EXPERTS, tm), lambda b, t: (0, b * (S // tm) + t))],
        out_shape=[jax.ShapeDtypeStruct((B, S, D), F32),
                   jax.ShapeDtypeStruct((B, S, D // 2), I32),
                   jax.ShapeDtypeStruct((N_EXPERTS, B * S), F32)],
        scratch_shapes=[pltpu.VMEM((GROUP_WIDTH // LANES, tm, LANES), F32) for _ in range(4)],
        compiler_params=pltpu.CompilerParams(
            dimension_semantics=("arbitrary", "arbitrary"), vmem_limit_bytes=VMEM_LIMIT),
        name="merge_norm_router",
    )(*args, *consts, *anchors)


def _sortable_key(x):
    bits = pltpu.bitcast(x, I32)
    return jnp.where(bits < 0, bits ^ jnp.int32(0x7FFFFFFF), bits)


def _route_kernel(sc_ref, bias_ref, before_ref, eidx_ref, rank_ref, gate_ref, cnt_ref, carry_ref):
    ne, tm = sc_ref.shape
    gsize = ne // N_EXPERT_GROUPS
    neg_inf = np.float32(-np.inf)
    removed = jnp.int32(-2 ** 31)

    @pl.when(pl.program_id(0) == 0)
    def _():
        carry_ref[...] = jnp.zeros_like(carry_ref)

    scores = sc_ref[...]
    biased = scores + bias_ref[...]

    gsum = []
    for g in range(N_EXPERT_GROUPS):
        v = biased[g * gsize:(g + 1) * gsize]
        m1 = jnp.max(v, axis=0, keepdims=True)
        n1 = jnp.sum(jnp.where(v == m1, 1.0, 0.0), axis=0, keepdims=True)
        m2 = jnp.max(jnp.where(v < m1, v, neg_inf), axis=0, keepdims=True)
        gsum.append(m1 + jnp.where(n1 >= 2.0, m1, m2))
    gkey = _sortable_key(jnp.concatenate(gsum, axis=0))

    def pick_first_max(keys, ids, n_ids):
        m = jnp.max(keys, axis=0, keepdims=True)
        idx = jnp.min(jnp.where(keys == m, ids, n_ids), axis=0, keepdims=True)
        hit = ids == idx
        return idx, hit, jnp.where(hit, removed, keys)

    gid = lax.broadcasted_iota(I32, (N_EXPERT_GROUPS, tm), 0)
    for _ in range(TOPK_GROUPS):
        _, _, gkey = pick_first_max(gkey, gid, N_EXPERT_GROUPS)
    group_on = gkey == removed

    masked = jnp.concatenate(
        [jnp.where(group_on[g:g + 1], biased[g * gsize:(g + 1) * gsize], neg_inf)
         for g in range(N_EXPERT_GROUPS)], axis=0)
    keys = _sortable_key(masked)
    eid = lax.broadcasted_iota(I32, (ne, tm), 0)
    picks = []
    for _ in range(TOP_K):
        idx, _, keys = pick_first_max(keys, eid, ne)
        picks.append(idx)

    sel = jnp.where(keys == removed, 1.0, 0.0)
    ranks = jnp.dot(sel.astype(BF16), before_ref[...], preferred_element_type=F32) + carry_ref[...]
    carry_ref[...] = carry_ref[...] + jnp.sum(sel, axis=1, keepdims=True)
    cnt_ref[...] = carry_ref[...]

    s_k, r_k = [], []
    for idx in picks:
        hit = eid == idx
        s_k.append(jnp.sum(jnp.where(hit, scores, 0.0), axis=0, keepdims=True))
        r_k.append(jnp.sum(jnp.where(hit, ranks, 0.0), axis=0, keepdims=True))
    total = s_k[0]
    for s in s_k[1:]:
        total = total + s
    eidx_ref[...] = jnp.concatenate(picks, axis=0)
    rank_ref[...] = jnp.concatenate(r_k, axis=0).astype(I32)
    gate_ref[...] = jnp.concatenate([s / total * np.float32(ROUTED_SCALE) for s in s_k], axis=0)


def _route(scores_t, bias, tok0, T):
    ne = scores_t.shape[0]
    tm = ROUTE_TILE
    first = tok0 // tm
    before = jnp.asarray(np.triu(np.ones((tm, tm), np.float32), k=1), BF16)
    out_spec = pl.BlockSpec((TOP_K, tm), lambda i: (0, i))
    return pl.pallas_call(
        _route_kernel,
        grid=(T // tm,),
        in_specs=[pl.BlockSpec((ne, tm), lambda i: (0, first + i)), _const_spec(bias.shape),
                  _const_spec(before.shape)],
        out_specs=[out_spec, out_spec, out_spec, _const_spec((ne, 1))],
        out_shape=[jax.ShapeDtypeStruct((TOP_K, T), I32), jax.ShapeDtypeStruct((TOP_K, T), I32),
                   jax.ShapeDtypeStruct((TOP_K, T), F32), jax.ShapeDtypeStruct((ne, 1), F32)],
        scratch_shapes=[pltpu.VMEM((ne, 1), F32)],
        compiler_params=pltpu.CompilerParams(dimension_semantics=("arbitrary",), vmem_limit_bytes=VMEM_LIMIT),
        name="route_topk",
    )(scores_t, bias, before)


def _dest_kernel(pstart_ref, eidx_ref, rank_ref, out_ref):
    eidx = eidx_ref[...]
    start = jnp.zeros(eidx.shape, I32)
    for e in range(N_EXPERTS):
        start = jnp.where(eidx == e, pstart_ref[e], start)
    out_ref[...] = start + rank_ref[...]


def _dest_rows(eidx_t, rank_t, pstart):
    T = eidx_t.shape[1]
    tm = DEST_TILE
    tok_spec = pl.BlockSpec((TOP_K, tm), lambda i: (0, i))
    return pl.pallas_call(
        _dest_kernel,
        grid=(T // tm,),
        in_specs=[pl.BlockSpec(memory_space=pltpu.SMEM), tok_spec, tok_spec],
        out_specs=tok_spec,
        out_shape=jax.ShapeDtypeStruct((TOP_K, T), I32),
        compiler_params=pltpu.CompilerParams(dimension_semantics=("arbitrary",)),
        name="moe_dest_rows",
    )(pstart, eidx_t, rank_t)


def _sc_workers():
    info = plsc.get_sparse_core_info()
    return info.num_cores, info.num_cores * info.num_subcores


def _sc_scatter_rows(rows, row0, dest, n_out):
    n_dst, n_chunks, ch = dest.shape
    width = rows.shape[1]
    n_cores, n_workers = _sc_workers()
    per_w = n_chunks // n_workers
    assert n_chunks % n_workers == 0 and per_w % 8 == 0 and row0 + n_chunks * ch <= rows.shape[0]

    def body(rows_hbm, dest_hbm, out_hbm, idx_v, buf, lsem, ssem):
        wid = lax.axis_index("s") * n_cores + lax.axis_index("c")
        c0 = wid * per_w
        for k in range(n_dst):
            pltpu.sync_copy(dest_hbm.at[k, pl.ds(pl.multiple_of(c0, 8), per_w)], idx_v.at[k])

        def load(c, b):
            return pltpu.make_async_copy(rows_hbm.at[pl.ds(row0 + (c0 + c) * ch, ch)], buf.at[b], lsem.at[b])

        def scatter(c, b, k):
            return pltpu.make_async_copy(buf.at[b], out_hbm.at[idx_v.at[k, c]], ssem.at[b])

        load(0, 0).start()

        @pl.loop(0, per_w, step=2)
        def _(c):
            for b in range(2):
                cc = c + b
                load(cc, b).wait()

                @pl.when(cc >= 1)
                def _():
                    for k in range(n_dst):
                        scatter(cc - 1, 1 - b, k).wait()

                @pl.when(cc + 1 < per_w)
                def _():
                    load(cc + 1, 1 - b).start()

                for k in range(n_dst):
                    scatter(cc, b, k).start()

        for k in range(n_dst):
            scatter(per_w - 1, 1, k).wait()

    mesh = plsc.VectorSubcoreMesh(core_axis_name="c", subcore_axis_name="s")
    return pl.kernel(
        body, out_type=jax.ShapeDtypeStruct((n_out, width), rows.dtype), mesh=mesh,
        scratch_types=[pltpu.VMEM((n_dst, per_w, ch), I32), pltpu.VMEM((2, ch, width), rows.dtype),
                       pltpu.SemaphoreType.DMA((2,)), pltpu.SemaphoreType.DMA((2,))],
        name="moe_dispatch_scatter",
    )(rows, dest)


def _sc_gather_weighted_sum(table, idx, gates):
    n_src, n_tok = idx.shape
    ct = SC_SUM_CHUNK
    n_chunks = n_tok // ct
    half = table.shape[1]
    lanes = plsc.get_sparse_core_info().num_lanes
    n_cores, n_workers = _sc_workers()
    per_w = n_chunks // n_workers
    assert n_chunks % n_workers == 0 and per_w % 2 == 0 and half % lanes == 0

    def body(table_hbm, idx_hbm, gate_hbm, out_hbm, idx_v, gate_v, buf, out_v, gsem, wsem):
        wid = lax.axis_index("s") * n_cores + lax.axis_index("c")
        c0 = wid * per_w
        pltpu.sync_copy(idx_hbm.at[:, pl.ds(c0 * ct, per_w * ct)], idx_v)
        pltpu.sync_copy(gate_hbm.at[:, pl.ds(c0 * ct, per_w * ct)], gate_v)

        def gather(c, b, k):
            return pltpu.make_async_copy(table_hbm.at[idx_v.at[k, pl.ds(c * ct, ct)]], buf.at[b, k], gsem.at[b])

        def write(c, b):
            return pltpu.make_async_copy(out_v.at[b], out_hbm.at[pl.ds((c0 + c) * ct, ct)], wsem.at[b])

        for k in range(n_src):
            gather(0, 0, k).start()

        @pl.loop(0, per_w, step=2)
        def _(c):
            for b in range(2):
                cc = c + b
                for k in range(n_src):
                    gather(cc, b, k).wait()

                @pl.when(cc + 1 < per_w)
                def _():
                    for k in range(n_src):
                        gather(cc + 1, 1 - b, k).start()

                @pl.when(cc >= 2)
                def _():
                    write(cc - 2, b).wait()

                @pl.loop(0, ct)
                def _(t):
                    tok = jnp.full((lanes,), cc * ct + t, I32)
                    g = [plsc.load_gather(gate_v, [jnp.full((lanes,), k, I32), tok]) for k in range(n_src)]

                    @plsc.parallel_loop(0, half, step=lanes, unroll=2)
                    def _(col):
                        lo = jnp.zeros((lanes,), F32)
                        hi = jnp.zeros((lanes,), F32)
                        for k in range(n_src):
                            w = buf[b, k, t, pl.ds(col, lanes)]
                            w_lo, w_hi = plsc.unpack(plsc.bitcast(w, BF16), format=plsc.PackFormat.INTERLEAVED)
                            lo = lo + g[k] * w_lo
                            hi = hi + g[k] * w_hi
                        out_v[b, t, pl.ds(col, lanes)] = lo
                        out_v[b, t, pl.ds(half + col, lanes)] = hi

                write(cc, b).start()

        write(per_w - 2, 0).wait()
        write(per_w - 1, 1).wait()

    mesh = plsc.VectorSubcoreMesh(core_axis_name="c", subcore_axis_name="s")
    return pl.kernel(
        body, out_type=jax.ShapeDtypeStruct((n_chunks * ct, 2 * half), F32), mesh=mesh,
        scratch_types=[pltpu.VMEM((n_src, per_w * ct), I32), pltpu.VMEM((n_src, per_w * ct), F32),
                       pltpu.VMEM((2, n_src, ct, half), I32), pltpu.VMEM((2, ct, 2 * half), F32),
                       pltpu.SemaphoreType.DMA((2,)), pltpu.SemaphoreType.DMA((2,))],
        compiler_params=pltpu.CompilerParams(needs_layout_passes=False),
        name="moe_combine_gather_sum",
    )(table, idx, gates)


def _sc_pack_row_pairs(w):
    R, C = w.shape
    lanes = plsc.get_sparse_core_info().num_lanes
    cr = SC_PACK_CHUNK_WORDS // C
    n_chunks = R // cr
    n_cores, n_workers = _sc_workers()
    per_w = n_chunks // n_workers
    assert R % cr == 0 and n_chunks % n_workers == 0 and per_w % 2 == 0 and cr % 2 == 0 and C % lanes == 0

    def body(w_hbm, out_hbm, in_v, out_v, lsem, ssem):
        wid = lax.axis_index("s") * n_cores + lax.axis_index("c")
        c0 = wid * per_w

        def load(c, b):
            return pltpu.make_async_copy(w_hbm.at[pl.ds((c0 + c) * cr, cr)], in_v.at[b], lsem.at[b])

        def store(c, b):
            return pltpu.make_async_copy(out_v.at[b], out_hbm.at[pl.ds((c0 + c) * (cr // 2), cr // 2)],
                                         ssem.at[b])

        load(0, 0).start()

        @pl.loop(0, per_w, step=2)
        def _(c):
            for b in range(2):
                cc = c + b
                load(cc, b).wait()

                @pl.when(cc + 1 < per_w)
                def _():
                    load(cc + 1, 1 - b).start()

                @pl.when(cc >= 2)
                def _():
                    store(cc - 2, b).wait()

                @pl.loop(0, cr // 2)
                def _(i):
                    @plsc.parallel_loop(0, C, step=lanes, unroll=4)
                    def _(col):
                        even = in_v[b, 2 * i, pl.ds(col, lanes)]
                        odd = in_v[b, 2 * i + 1, pl.ds(col, lanes)]
                        pair = plsc.pack(even, odd, format=plsc.PackFormat.INTERLEAVED)
                        out_v[b, i, pl.ds(col, lanes)] = plsc.bitcast(pair, I32)

                store(cc, b).start()

        store(per_w - 2, 0).wait()
        store(per_w - 1, 1).wait()

    mesh = plsc.VectorSubcoreMesh(core_axis_name="c", subcore_axis_name="s")
    return pl.kernel(
        body, out_type=jax.ShapeDtypeStruct((R // 2, C), I32), mesh=mesh,
        scratch_types=[pltpu.VMEM((2, cr, C), F32), pltpu.VMEM((2, cr // 2, C), I32),
                       pltpu.SemaphoreType.DMA((2,)), pltpu.SemaphoreType.DMA((2,))],
        compiler_params=pltpu.CompilerParams(needs_layout_passes=False),
        name="expert_weights_bf16",
    )(w)


def _expert_kernel(be_ref, ue_ref, nu_ref, xs_hbm, wg_hbm, wu_hbm, wd_hbm, y_hbm,
                   xbuf, ybuf, wg_v, wu_v, wd_v, xsem, ysem, wsem):
    n_used = nu_ref[0]
    n_exp = nu_ref[1]
    blk = MOE_BLOCK
    rows_gu = D_MODEL // 2
    rows_d = EXPERT_DIM // 2

    def rows(j):
        return pl.ds(pl.multiple_of(j * blk, blk), blk)

    def x_copy(j, p):
        return pltpu.make_async_copy(xs_hbm.at[rows(j)], xbuf.at[p], xsem.at[p])

    def y_copy(j, p):
        return pltpu.make_async_copy(ybuf.at[p], y_hbm.at[rows(j)], ysem.at[p])

    def w_copies(q, s):
        e = ue_ref[q]
        gu = pl.ds(pl.multiple_of(e * rows_gu, rows_gu), rows_gu)
        dn = pl.ds(pl.multiple_of(e * rows_d, rows_d), rows_d)
        return (pltpu.make_async_copy(wg_hbm.at[gu], wg_v.at[s], wsem.at[s, 0]),
                pltpu.make_async_copy(wu_hbm.at[gu], wu_v.at[s], wsem.at[s, 1]),
                pltpu.make_async_copy(wd_hbm.at[dn], wd_v.at[s], wsem.at[s, 2]))

    for q0 in range(W_SLOTS - 1):
        @pl.when(q0 < n_exp)
        def _():
            for c in w_copies(q0, q0):
                c.start()

    for j0 in range(X_SLOTS):
        @pl.when(j0 < n_used)
        def _():
            x_copy(j0, j0).start()

    def block_step(j, p, q):
        is_new = (j == 0) | (be_ref[j] != be_ref[jnp.maximum(j - 1, 0)])
        q = q + is_new.astype(I32)

        s = q % W_SLOTS

        @pl.when(is_new)
        def _():
            for c in w_copies(q, s):
                c.wait()

            @pl.when(q + W_SLOTS - 1 < n_exp)
            def _():
                for c in w_copies(q + W_SLOTS - 1, (q + W_SLOTS - 1) % W_SLOTS):
                    c.start()

        x_copy(j, p).wait()

        @pl.when(j >= X_SLOTS)
        def _():
            y_copy(j - X_SLOTS, p).wait()

        wg = pltpu.bitcast(wg_v[s], BF16)
        wu = pltpu.bitcast(wu_v[s], BF16)
        wd = pltpu.bitcast(wd_v[s], BF16)
        lo, hi = _unpack_bf16_pairs(xbuf[p])
        x = jnp.concatenate([lo.astype(BF16), hi.astype(BF16)], axis=1)
        g = jnp.dot(x, wg, preferred_element_type=F32)
        u = jnp.dot(x, wu, preferred_element_type=F32)
        hb = (g * jax.nn.sigmoid(g) * u).astype(BF16)
        ybuf[p] = _pack_bf16_pairs(jnp.dot(hb, wd, preferred_element_type=F32))
        y_copy(j, p).start()

        @pl.when(j + X_SLOTS < n_used)
        def _():
            x_copy(j + X_SLOTS, p).start()

        return q

    def group(m, q):
        q = block_step(X_SLOTS * m, 0, q)
        for p in range(1, X_SLOTS):
            j = X_SLOTS * m + p
            q = lax.cond(j < n_used, functools.partial(block_step, j, p), lambda q: q, q)
        return q

    lax.fori_loop(0, (n_used + X_SLOTS - 1) // X_SLOTS, group, jnp.int32(-1))

    for back in range(X_SLOTS, 0, -1):
        b = n_used - back
        for p in range(X_SLOTS):
            @pl.when((b >= 0) & (b % X_SLOTS == p))
            def _():
                y_copy(b, p).wait()


def _experts(block_e, used_e, counts2, xs, wg, wu, wd):
    n_rows, half = xs.shape
    smem = pl.BlockSpec(memory_space=pltpu.SMEM)
    hbm = pl.BlockSpec(memory_space=pl.ANY)
    return pl.pallas_call(
        _expert_kernel,
        in_specs=[smem, smem, smem, hbm, hbm, hbm, hbm],
        out_specs=hbm,
        out_shape=jax.ShapeDtypeStruct((n_rows, half), I32),
        scratch_shapes=[pltpu.VMEM((X_SLOTS, MOE_BLOCK, half), I32), pltpu.VMEM((X_SLOTS, MOE_BLOCK, half), I32),
                        pltpu.VMEM((W_SLOTS, D_MODEL // 2, EXPERT_DIM), I32),
                        pltpu.VMEM((W_SLOTS, D_MODEL // 2, EXPERT_DIM), I32),
                        pltpu.VMEM((W_SLOTS, EXPERT_DIM // 2, D_MODEL), I32),
                        pltpu.SemaphoreType.DMA((X_SLOTS,)), pltpu.SemaphoreType.DMA((X_SLOTS,)),
                        pltpu.SemaphoreType.DMA((W_SLOTS, 3))],
        compiler_params=pltpu.CompilerParams(vmem_limit_bytes=VMEM_LIMIT),
        name="moe_experts",
    )(block_e, used_e, counts2, xs, wg, wu, wd)


def _combine_kernel(routed_ref, x1_ref, swgu_ref, swd_ref, g2_ref, b2_ref, *rest, alpha):
    out_ref = rest[-1]
    x1 = x1_ref[...]
    gu = jnp.dot(x1.astype(BF16), swgu_ref[...], preferred_element_type=F32)
    g = gu[:, :EXPERT_DIM]
    u = gu[:, EXPERT_DIM:]
    shared = jnp.dot((g * jax.nn.sigmoid(g) * u).astype(BF16), swd_ref[...], preferred_element_type=F32)
    out_ref[...] = _layer_norm(alpha * x1 + (routed_ref[...] + shared), g2_ref[...], b2_ref[...])


def _combine(routed, x1, tok0, swgu, swd, g2, b2, alpha, out_prev):
    T, D = x1.shape
    n_tok = routed.shape[0]
    tc = COMBINE_TILE
    first = tok0 // tc
    consts = (swgu, swd, g2, b2)
    args = [routed, x1, *consts]
    in_specs = [pl.BlockSpec((tc, D), lambda i: (i, 0)),
                pl.BlockSpec((tc, D), lambda i: (first + i, 0))] + [_const_spec(c.shape) for c in consts]
    aliases = {}
    if out_prev is not None:
        aliases = {len(args): 0}
        args.append(out_prev)
        in_specs.append(pl.BlockSpec(memory_space=pl.ANY))
    return pl.pallas_call(
        functools.partial(_combine_kernel, alpha=alpha),
        grid=(n_tok // tc,),
        in_specs=in_specs,
        out_specs=pl.BlockSpec((tc, D), lambda i: (first + i, 0)),
        out_shape=jax.ShapeDtypeStruct((T, D), F32),
        input_output_aliases=aliases,
        compiler_params=pltpu.CompilerParams(dimension_semantics=("arbitrary",), vmem_limit_bytes=VMEM_LIMIT),
        name="moe_combine_norm",
    )(*args)


def _mixer(x, in_w, in_b, ng, nb, spatial_w, spatial_b, proj_a_w, proj_b_w, out_w, ln1_g, ln1_b,
           router_w, alpha, anchors):
    gw, aw, D = GMLP_WIDTH, ATTN_WIDTH, D_MODEL
    w = in_w.astype(BF16)
    q0 = 2 * gw
    wuv, buv = w[:, :q0], in_b[None, :q0]
    watt = jnp.stack([jnp.concatenate([w[:, q0 + s * aw + p * GROUP_WIDTH:q0 + s * aw + (p + 1) * GROUP_WIDTH]
                                       for s in range(3)], axis=1) for p in range(len(DILATED_PATTERNS))])
    batt = jnp.stack([jnp.concatenate([in_b[q0 + s * aw + p * GROUP_WIDTH:q0 + s * aw + (p + 1) * GROUP_WIDTH]
                                       for s in range(3)])[None] for p in range(len(DILATED_PATTERNS))])
    g0 = q0 + 3 * aw
    wg, bg = w[:, g0:], in_b[None, g0:]
    sb = jnp.repeat(spatial_b.T, gw // GMLP_GROUPS, axis=1)
    ya, gb, a1, a4, a16 = _input_projection(
        x, wuv, buv, watt, batt, wg, bg, ng[None], nb[None], spatial_w, sb, proj_a_w.astype(BF16))
    attn_outs = _dilated_attention((a1, a4, a16))
    rw_t = router_w.T
    rwh = rw_t.astype(BF16)
    rwl = (rw_t - rwh.astype(F32)).astype(BF16)
    return _merge_and_norm(attn_outs, ya, gb, x, proj_b_w.astype(BF16), out_w.astype(BF16),
                           ln1_g[None], ln1_b[None], rwh, rwl, alpha, anchors)


def _moe(x1, x1p, scores_t, router_bias, w_gate, w_up, w_down, sw_gate, sw_up, sw_down, ln2_g, ln2_b, alpha):
    swgu = jnp.concatenate([sw_gate, sw_up], axis=1).astype(BF16)
    swd = sw_down.astype(BF16)
    out = None
    tok0 = 0
    for part, eighths in enumerate(MOE_TOKEN_SPLIT):
        n_tok = x1.shape[0] * eighths // 8
        n_blocks = (n_tok * TOP_K + N_EXPERTS * (MOE_BLOCK - 1)) // MOE_BLOCK
        eidx_t, rank_t, gate_t, counts = _route(scores_t, router_bias[:, None], tok0, n_tok)
        counts = counts[:, 0].astype(I32)
        padded = (counts + MOE_BLOCK - 1) // MOE_BLOCK * MOE_BLOCK
        pend = jnp.cumsum(padded).astype(I32)
        pstart = pend - padded
        block_starts = jnp.arange(n_blocks, dtype=I32) * MOE_BLOCK
        block_e = jnp.minimum(jnp.sum((pend[None, :] <= block_starts[:, None]).astype(I32), axis=1),
                              N_EXPERTS - 1)
        used = counts > 0
        used_e = jnp.argsort(jnp.logical_not(used), stable=True).astype(I32)
        counts2 = jnp.stack([pend[-1] // MOE_BLOCK, jnp.sum(used.astype(I32))]).astype(I32)
        dest_t = _dest_rows(eidx_t, rank_t, pstart)
        xs = _sc_scatter_rows(x1p, tok0, dest_t.reshape(TOP_K, n_tok // SC_CHUNK, SC_CHUNK), n_blocks * MOE_BLOCK)
        y_rows = _experts(block_e, used_e, counts2, xs, w_gate, w_up, w_down)
        pieces = MOE_TAIL_PIECES if part == len(MOE_TOKEN_SPLIT) - 1 else 1
        n_piece = n_tok // pieces
        for q in range(pieces):
            cols = slice(q * n_piece, (q + 1) * n_piece)
            routed = _sc_gather_weighted_sum(y_rows, dest_t[:, cols], gate_t[:, cols])
            out = _combine(routed, x1, tok0 + q * n_piece, swgu, swd, ln2_g[None], ln2_b[None], alpha, out)
        tok0 += n_tok
    return out


def kernel(x, in_w, in_b, gmlp_norm_g, gmlp_norm_b, spatial_w, spatial_b, proj_a_w, proj_b_w, out_w,
           ln1_g, ln1_b, router_w, router_bias, expert_w_gate, expert_w_up, expert_w_down,
           shared_w_gate, shared_w_up, shared_w_down, ln2_g, ln2_b):
    B, S, D = x.shape
    depth = in_w.shape[0]
    alpha = np.float32((2.0 * depth) ** 0.25)
    for l in range(depth):
        packed = [_sc_pack_row_pairs(w[l].reshape(-1, w.shape[-1]))
                  for w in (expert_w_gate, expert_w_up, expert_w_down)]
        x1, x1p, scores_t = _mixer(x, in_w[l], in_b[l], gmlp_norm_g[l], gmlp_norm_b[l], spatial_w[l],
                                 spatial_b[l], proj_a_w[l], proj_b_w[l], out_w[l], ln1_g[l], ln1_b[l],
                                 router_w[l], alpha, [p[:8] for p in packed])
        out = _moe(x1.reshape(B * S, D), x1p.reshape(B * S, D // 2), scores_t,
                   router_bias[l], *packed,
                   shared_w_gate[l], shared_w_up[l], shared_w_down[l], ln2_g[l], ln2_b[l], alpha)
        x = out.reshape(B, S, D)
    return x
```
